```python
import math
import jax, jax.numpy as jnp
from jax import lax
import numpy as np

D_MODEL = 1024
BATCH = 2
SEQ = 16384
DEPTH = 1
DEC_BATCH = 1
DEC_SEQ = 16384
PAST_LEN = 128

N_META = 16
CHUNK = 128
PAD_FRONT = CHUNK - N_META
N_DIR = 2
SSD_WIDTH = D_MODEL
SSD_HEAD_DIM = 64
SSD_HEADS = SSD_WIDTH // SSD_HEAD_DIM
SSD_GROUPS = 4
SSD_STATE = 128
SSD_CONV_W = 5
SSD_CONV_CH = SSD_WIDTH + 2 * SSD_GROUPS * SSD_STATE
MLSTM_WIDTH = D_MODEL
MLSTM_HEADS = 8
MLSTM_DV = MLSTM_WIDTH // MLSTM_HEADS
MLSTM_DK = MLSTM_DV // 2
MIX_WIDTH = SSD_WIDTH + MLSTM_WIDTH
IN_SIZES = (SSD_WIDTH,
            SSD_CONV_CH,
            N_DIR * SSD_HEADS,
            MLSTM_HEADS * MLSTM_DK,
            MLSTM_HEADS * MLSTM_DK,
            MLSTM_WIDTH,
            MLSTM_WIDTH,
            N_DIR * MLSTM_HEADS,
            N_DIR * MLSTM_HEADS)
IN_COLS = (SSD_WIDTH + SSD_CONV_CH + N_DIR * SSD_HEADS + 2 * MLSTM_HEADS * MLSTM_DK
           + 2 * MLSTM_WIDTH + 2 * N_DIR * MLSTM_HEADS)
N_EXPERTS = 32
TOP_K = 4
D_FF = D_MODEL
SWIGLU_LIMIT = 7.0
SWIGLU_ALPHA = 1.702
MOE_BLOCK = 128
DEEPNORM_ALPHA = (2 * DEPTH) ** 0.25
DEEPNORM_BETA = (8 * DEPTH) ** -0.25
LN_EPS = 1e-5
RMS_EPS = 1e-5
NEG_GATE = -1e30

kernel_name = "hymba_ssd_mlstm_moe_encoder"


def _split_points(sizes):
    pts, acc = [], 0
    for s in sizes[:-1]:
        acc += s
        pts.append(acc)
    return pts


def _layer_norm(x, g, b):
    xf = x.astype(jnp.float32)
    mu = jnp.mean(xf, axis=-1, keepdims=True)
    xc = xf - mu
    var = jnp.mean(xc * xc, axis=-1, keepdims=True)
    y = xc * lax.rsqrt(var + LN_EPS) * g.astype(jnp.float32) + b.astype(jnp.float32)
    return y.astype(x.dtype)


def _rms_norm(x, g):
    xf = x.astype(jnp.float32)
    return xf * lax.rsqrt(jnp.mean(xf * xf, axis=-1, keepdims=True) + RMS_EPS) * g.astype(jnp.float32)


def _centred_dwconv(x, w, b):
    half = SSD_CONV_W // 2
    L = x.shape[1]
    xp = jnp.pad(x, ((0, 0), (half, half), (0, 0)))
    y = xp[:, 0:L] * w[0]
    for t in range(1, SSD_CONV_W):
        y = y + xp[:, t:t + L] * w[t]
    return y + b


def _front_pad(t, value=0.0):
    return jnp.pad(t, [(0, 0), (PAD_FRONT, 0)] + [(0, 0)] * (t.ndim - 2), constant_values=value)


def _flip(t):
    return jnp.flip(t, axis=1)


def _ssd_chunked(xs, a, Bm, Cm):
    b, Lp = xs.shape[:2]
    c = Lp // CHUNK
    R = SSD_HEADS // SSD_GROUPS
    xs = xs.reshape(b, c, CHUNK, SSD_GROUPS, R, SSD_HEAD_DIM)
    Bm = Bm.reshape(b, c, CHUNK, SSD_GROUPS, SSD_STATE)
    Cm = Cm.reshape(b, c, CHUNK, SSD_GROUPS, SSD_STATE)
    a = a.reshape(b, c, CHUNK, SSD_GROUPS, R).transpose(0, 1, 3, 4, 2)
    acs = jnp.cumsum(a, axis=-1)
    causal = jnp.tril(jnp.ones((CHUNK, CHUNK), dtype=bool))
    seg = acs[..., :, None] - acs[..., None, :]
    decay_in = jnp.exp(jnp.where(causal, seg, -jnp.inf))
    cb = jnp.einsum('bclgn,bcsgn->bcgls', Cm, Bm)
    y_diag = jnp.einsum('bcgrls,bcsgrp->bclgrp', cb[:, :, :, None] * decay_in, xs)
    decay_to_end = jnp.exp(acs[..., -1:] - acs)
    states = jnp.einsum('bcsgn,bcgrs,bcsgrp->bcgrpn', Bm, decay_to_end, xs)
    chunk_decay = jnp.exp(acs[..., -1])

    def step(carry, inp):
        dec, st = inp
        return dec[..., None, None] * carry + st, carry

    _, states_in = lax.scan(step, jnp.zeros_like(states[:, 0]),
                            (jnp.moveaxis(chunk_decay, 1, 0), jnp.moveaxis(states, 1, 0)))
    states_in = jnp.moveaxis(states_in, 0, 1)
    y_off = jnp.einsum('bclgn,bcgrpn->bclgrp', Cm, states_in) * \
        jnp.exp(acs).transpose(0, 1, 4, 2, 3)[..., None]
    return (y_diag + y_off).reshape(b, Lp, SSD_HEADS, SSD_HEAD_DIM)


def _mlstm_chunked(q, k, v, i_pre, log_f):
    b, Lp = q.shape[:2]
    c = Lp // CHUNK

    def to_chunks(t):
        return t.reshape(b, c, CHUNK, MLSTM_HEADS, t.shape[-1]).transpose(0, 1, 3, 2, 4)

    q, k, v = to_chunks(q), to_chunks(k), to_chunks(v)
    i_pre = i_pre.reshape(b, c, CHUNK, MLSTM_HEADS).transpose(0, 1, 3, 2)
    log_f = log_f.reshape(b, c, CHUNK, MLSTM_HEADS).transpose(0, 1, 3, 2)
    bcum = jnp.cumsum(log_f, axis=-1)
    g = bcum[..., -1]
    a = g[..., None] - bcum + i_pre
    a_max = jnp.max(a, axis=-1)
    w = jnp.exp(a - a_max[..., None])
    kv = jnp.einsum('bchsk,bchsv->bchkv', k * w[..., None], v)
    ksum = jnp.einsum('bchs,bchsk->bchk', w, k)

    def step(carry, inp):
        C, n, m = carry
        g_c, am_c, kv_c, ks_c = inp
        m_new = jnp.maximum(g_c + m, am_c)
        s_prev = jnp.exp(g_c + m - m_new)
        s_new = jnp.exp(am_c - m_new)
        C_new = s_prev[..., None, None] * C + s_new[..., None, None] * kv_c
        n_new = s_prev[..., None] * n + s_new[..., None] * ks_c
        return (C_new, n_new, m_new), (C, n, m)

    init = (jnp.zeros((b, MLSTM_HEADS, MLSTM_DK, MLSTM_DV), jnp.float32),
            jnp.zeros((b, MLSTM_HEADS, MLSTM_DK), jnp.float32),
            jnp.zeros((b, MLSTM_HEADS), jnp.float32))
    _, (C_prev, n_prev, m_prev) = lax.scan(
        step, init, (jnp.moveaxis(g, 1, 0), jnp.moveaxis(a_max, 1, 0),
                     jnp.moveaxis(kv, 1, 0), jnp.moveaxis(ksum, 1, 0)))
    C_prev = jnp.moveaxis(C_prev, 0, 1)
    n_prev = jnp.moveaxis(n_prev, 0, 1)
    m_prev = jnp.moveaxis(m_prev, 0, 1)

    causal = jnp.tril(jnp.ones((CHUNK, CHUNK), dtype=bool))
    dlog = jnp.where(causal, bcum[..., :, None] - bcum[..., None, :] + i_pre[..., None, :], -jnp.inf)
    m_inter = bcum + m_prev[..., None]
    m_t = jnp.maximum(m_inter, jnp.max(dlog, axis=-1))
    s = jnp.einsum('bchtk,bchsk->bchts', q, k) * jnp.exp(dlog - m_t[..., None])
    s_inter = jnp.exp(m_inter - m_t)
    num = s_inter[..., None] * jnp.einsum('bchtk,bchkv->bchtv', q, C_prev) + \
        jnp.einsum('bchts,bchsv->bchtv', s, v)
    den = s_inter * jnp.einsum('bchtk,bchk->bcht', q, n_prev) + jnp.sum(s, axis=-1)
    h = num / jnp.maximum(jnp.abs(den), jnp.exp(-m_t))[..., None]
    return h.transpose(0, 1, 3, 2, 4).reshape(b, Lp, MLSTM_HEADS, MLSTM_DV)


def _mixer(h, w_in, conv_w, conv_b, dt_bias, a_log, d_skip, ssd_norm_g,
           i_bias, f_bias, mlstm_norm_g, w_out):
    f32 = jnp.float32
    b, L, _ = h.shape
    proj = jnp.einsum('bld,de->ble', h, w_in)
    z, xbc, dt_raw, q, k, v, o_raw, i_raw, f_raw = jnp.split(proj, _split_points(IN_SIZES), axis=-1)

    xbc = jax.nn.silu(_centred_dwconv(xbc, conv_w, conv_b)).astype(f32)
    xs, Bm, Cm = jnp.split(xbc, [SSD_WIDTH, SSD_WIDTH + SSD_GROUPS * SSD_STATE], axis=-1)
    xs = xs.reshape(b, L, SSD_HEADS, SSD_HEAD_DIM)
    Bm = Bm.reshape(b, L, SSD_GROUPS, SSD_STATE)
    Cm = Cm.reshape(b, L, SSD_GROUPS, SSD_STATE)
    dt = jax.nn.softplus(dt_raw.astype(f32).reshape(b, L, N_DIR, SSD_HEADS) + dt_bias.astype(f32))
    A = -jnp.exp(a_log.astype(f32))
    xs_p, B_p, C_p, dt_p = _front_pad(xs), _front_pad(Bm), _front_pad(Cm), _front_pad(dt)
    y_f = _ssd_chunked(xs_p * dt_p[:, :, 0, :, None], dt_p[:, :, 0] * A[0], B_p, C_p)
    y_b = _flip(_ssd_chunked(_flip(xs_p * dt_p[:, :, 1, :, None]), _flip(dt_p[:, :, 1] * A[1]),
                             _flip(B_p), _flip(C_p)))
    y_ssd = (y_f + y_b)[:, PAD_FRONT:] + d_skip.astype(f32)[:, None] * xs
    y_ssd = _rms_norm(y_ssd.reshape(b, L, SSD_WIDTH) * jax.nn.silu(z.astype(f32)), ssd_norm_g)

    q = q.astype(f32).reshape(b, L, MLSTM_HEADS, MLSTM_DK)
    k = k.astype(f32).reshape(b, L, MLSTM_HEADS, MLSTM_DK) * (MLSTM_DK ** -0.5)
    v = v.astype(f32).reshape(b, L, MLSTM_HEADS, MLSTM_DV)
    i_pre = i_raw.astype(f32).reshape(b, L, N_DIR, MLSTM_HEADS) + i_bias.astype(f32)
    log_f = jax.nn.log_sigmoid(f_raw.astype(f32).reshape(b, L, N_DIR, MLSTM_HEADS) + f_bias.astype(f32))
    q_p, k_p, v_p = _front_pad(q), _front_pad(k), _front_pad(v)
    i_p, f_p = _front_pad(i_pre, NEG_GATE), _front_pad(log_f)
    h_f = _mlstm_chunked(q_p, k_p, v_p, i_p[:, :, 0], f_p[:, :, 0])
    h_b = _flip(_mlstm_chunked(_flip(q_p), _flip(k_p), _flip(v_p), _flip(i_p[:, :, 1]), _flip(f_p[:, :, 1])))
    h_m = _rms_norm((h_f + h_b)[:, PAD_FRONT:], mlstm_norm_g.reshape(MLSTM_HEADS, MLSTM_DV))
    y_mlstm = jax.nn.sigmoid(o_raw.astype(f32)) * h_m.reshape(b, L, MLSTM_WIDTH)

    y = jnp.concatenate([y_ssd, y_mlstm], axis=-1).astype(h.dtype)
    return jnp.einsum('ble,ed->bld', y, w_out)


def _moe(x2d, w_router, b_router, w1, b1, w2, b2):
    T = x2d.shape[0]
    logits = jnp.einsum('td,de->te', x2d.astype(jnp.float32), w_router.astype(jnp.float32)) + \
        b_router.astype(jnp.float32)
    top_vals, top_idx = lax.top_k(logits, TOP_K)
    gates = jax.nn.softmax(top_vals, axis=-1)
    flat_e = top_idx.reshape(-1).astype(jnp.int32)
    flat_tok = jnp.broadcast_to(jnp.arange(T, dtype=jnp.int32)[:, None], (T, TOP_K)).reshape(-1)
    flat_g = gates.reshape(-1)
    order = jnp.argsort(flat_e)
    e_s, tok_s, g_s = flat_e[order], flat_tok[order], flat_g[order]
    counts = jnp.bincount(flat_e, length=N_EXPERTS)
    starts = jnp.cumsum(counts) - counts
    padded = ((counts + MOE_BLOCK - 1) // MOE_BLOCK) * MOE_BLOCK
    pends = jnp.cumsum(padded)
    pstarts = pends - padded
    dest = pstarts[e_s] + (jnp.arange(T * TOP_K, dtype=jnp.int32) - starts[e_s])
    n_blocks = -(-(T * TOP_K) // MOE_BLOCK) + N_EXPERTS
    P = n_blocks * MOE_BLOCK
    buf_tok = jnp.full((P,), T, dtype=jnp.int32).at[dest].set(tok_s)
    buf_g = jnp.zeros((P,), jnp.float32).at[dest].set(g_s)
    block_start = jnp.arange(n_blocks, dtype=jnp.int32) * MOE_BLOCK
    block_e = jnp.minimum(jnp.sum(pends[None, :] <= block_start[:, None], axis=1), N_EXPERTS - 1)
    x_pad = jnp.concatenate([x2d, jnp.zeros((1, x2d.shape[1]), x2d.dtype)], axis=0)
    xb = x_pad[buf_tok].reshape(n_blocks, MOE_BLOCK, x2d.shape[1])

    def expert_block(args):
        xe, e = args
        hcat = xe @ w1[e] + b1[e]
        gate, up = hcat[:, :D_FF], hcat[:, D_FF:]
        gate = jnp.minimum(gate, SWIGLU_LIMIT)
        up = jnp.clip(up, -SWIGLU_LIMIT, SWIGLU_LIMIT)
        act = (up + 1.0) * gate * jax.nn.sigmoid(SWIGLU_ALPHA * gate)
        return act @ w2[e] + b2[e]

    yb = lax.map(expert_block, (xb, block_e)).reshape(P, x2d.shape[1])
    yb = yb * buf_g[:, None].astype(yb.dtype)
    y = jnp.zeros((T + 1, x2d.shape[1]), yb.dtype).at[buf_tok].add(yb)
    return y[:T].astype(x2d.dtype)


def _encode(x, meta_tokens, ln_emb_g, ln_emb_b, w_in, conv_w, conv_b, dt_bias, a_log, d_skip,
            ssd_norm_g, i_bias, f_bias, mlstm_norm_g, w_out, ln1_g, ln1_b, w_router, b_router,
            w1, b1, w2, b2, ln2_g, ln2_b):
    b, L, D = x.shape
    meta = jnp.broadcast_to(meta_tokens[None].astype(x.dtype), (b, N_META, D))
    h = _layer_norm(jnp.concatenate([meta, x], axis=1), ln_emb_g, ln_emb_b)
    for l in range(DEPTH):
        mix = _mixer(h, w_in[l], conv_w[l], conv_b[l], dt_bias[l], a_log[l], d_skip[l],
                     ssd_norm_g[l], i_bias[l], f_bias[l], mlstm_norm_g[l], w_out[l])
        h = _layer_norm(DEEPNORM_ALPHA * h + mix, ln1_g[l], ln1_b[l])
        ffn = _moe(h.reshape(-1, D), w_router[l], b_router[l], w1[l], b1[l], w2[l], b2[l]).reshape(h.shape)
        h = _layer_norm(DEEPNORM_ALPHA * h + ffn, ln2_g[l], ln2_b[l])
    return h[:, N_META:]


def setup_inputs(seed: int = 0) -> dict:
    key = jax.random.key(seed)
    ks = jax.random.split(key, 26)
    f32 = jnp.float32

    def nrm(k, shape, scale):
        return jax.random.normal(k, shape, f32) * scale

    dt0 = jnp.exp(jax.random.uniform(ks[8], (DEPTH, N_DIR, SSD_HEADS), f32, math.log(1e-3), math.log(1e-1)))
    return {
        "x_prompt": nrm(ks[0], (BATCH, SEQ, D_MODEL), 1.0),
        "x_sample": nrm(ks[1], (DEC_BATCH, DEC_SEQ, D_MODEL), 1.0),
        "meta_tokens": nrm(ks[2], (N_META, D_MODEL), 1.0),
        "ln_emb_g": 1.0 + nrm(ks[3], (D_MODEL,), 0.02),
        "ln_emb_b": nrm(ks[4], (D_MODEL,), 0.01),
        "w_in": nrm(ks[5], (DEPTH, D_MODEL, IN_COLS), D_MODEL ** -0.5),
        "conv_w": nrm(ks[6], (DEPTH, SSD_CONV_W, SSD_CONV_CH), SSD_CONV_W ** -0.5),
        "conv_b": nrm(ks[7], (DEPTH, SSD_CONV_CH), 0.01),
        "dt_bias": dt0 + jnp.log(-jnp.expm1(-dt0)),
        "a_log": jnp.log(jax.random.uniform(ks[9], (DEPTH, N_DIR, SSD_HEADS), f32, 1.0, 16.0)),
        "d_skip": 1.0 + nrm(ks[10], (DEPTH, SSD_HEADS), 0.1),
        "ssd_norm_g": 1.0 + nrm(ks[11], (DEPTH, SSD_WIDTH), 0.02),
        "i_bias": nrm(ks[12], (DEPTH, N_DIR, MLSTM_HEADS), 0.1),
        "f_bias": jax.random.uniform(ks[13], (DEPTH, N_DIR, MLSTM_HEADS), f32, 3.0, 6.0),
        "mlstm_norm_g": 1.0 + nrm(ks[14], (DEPTH, MLSTM_WIDTH), 0.02),
        "w_out": nrm(ks[15], (DEPTH, MIX_WIDTH, D_MODEL), MIX_WIDTH ** -0.5 * DEEPNORM_BETA),
        "ln1_g": 1.0 + nrm(ks[16], (DEPTH, D_MODEL), 0.02),
        "ln1_b": nrm(ks[17], (DEPTH, D_MODEL), 0.01),
        "w_router": nrm(ks[18], (DEPTH, D_MODEL, N_EXPERTS), D_MODEL ** -0.5),
        "b_router": nrm(ks[19], (DEPTH, N_EXPERTS), 0.01),
        "w1": nrm(ks[20], (DEPTH, N_EXPERTS, D_MODEL, 2 * D_FF), D_MODEL ** -0.5),
        "b1": nrm(ks[21], (DEPTH, N_EXPERTS, 2 * D_FF), 0.01),
        "w2": nrm(ks[22], (DEPTH, N_EXPERTS, D_FF, D_MODEL), D_FF ** -0.5 * DEEPNORM_BETA),
        "b2": nrm(ks[23], (DEPTH, N_EXPERTS, D_MODEL), 0.01),
        "ln2_g": 1.0 + nrm(ks[24], (DEPTH, D_MODEL), 0.02),
        "ln2_b": nrm(ks[25], (DEPTH, D_MODEL), 0.01),
    }


def reference(x_prompt, x_sample, meta_tokens, ln_emb_g, ln_emb_b, w_in, conv_w, conv_b, dt_bias,
              a_log, d_skip, ssd_norm_g, i_bias, f_bias, mlstm_norm_g, w_out, ln1_g, ln1_b,
              w_router, b_router, w1, b1, w2, b2, ln2_g, ln2_b):
    y_prompt = _encode(x_prompt, meta_tokens, ln_emb_g, ln_emb_b, w_in, conv_w, conv_b, dt_bias,
                       a_log, d_skip, ssd_norm_g, i_bias, f_bias, mlstm_norm_g, w_out, ln1_g, ln1_b,
                       w_router, b_router, w1, b1, w2, b2, ln2_g, ln2_b)
    y_sample = _encode(x_sample, meta_tokens, ln_emb_g, ln_emb_b, w_in, conv_w, conv_b, dt_bias,
                       a_log, d_skip, ssd_norm_g, i_bias, f_bias, mlstm_norm_g, w_out, ln1_g, ln1_b,
                       w_router, b_router, w1, b1, w2, b2, ln2_g, ln2_b)
    return (y_prompt, y_sample)
```

```python
import functools
import math

import jax
import jax.numpy as jnp
from jax import lax
from jax.experimental import pallas as pl
from jax.experimental.pallas import tpu as pltpu

F32 = jnp.float32
BF16 = jnp.bfloat16

D_MODEL = 1024
N_META = 16
CHUNK = 128
PAD_FRONT = CHUNK - N_META
SSD_HEADS = 16
SSD_HEAD_DIM = 64
SSD_GROUPS = 4
SSD_STATE = 128
HEADS_PER_GROUP = SSD_HEADS // SSD_GROUPS
GROUP_WIDTH = HEADS_PER_GROUP * SSD_HEAD_DIM
CONV_W = 5
CONV_HALF = CONV_W // 2
CONV_CH = D_MODEL + 2 * SSD_GROUPS * SSD_STATE
MLSTM_HEADS = 8
MLSTM_DK = 64
MLSTM_DV = 128
N_EXPERTS = 32
TOP_K = 4
D_FF = D_MODEL
SWIGLU_LIMIT = 7.0
SWIGLU_ALPHA = 1.702
DEEPNORM_ALPHA = 2.0 ** 0.25
LN_EPS = 1e-5
RMS_EPS = 1e-5
NEG_GATE = -1e30

LANES = 128
BF16_SUBLANES = 16
VMEM_LIMIT_BYTES = 56 * 1024 * 1024

GATE_DT0, GATE_I0, GATE_F0, GATE_END = 0, SSD_HEADS, SSD_HEADS + MLSTM_HEADS, SSD_HEADS + 2 * MLSTM_HEADS

INPROJ_ROWS = 384
EPILOGUE_ROWS = 512
RANK_ROWS = 512
DISPATCH_ROWS = 512
COMBINE_ROWS = 256
MOE_BLOCK = 256


def _dot(a, b):
    return jnp.dot(a, b, preferred_element_type=F32)


def _dot_nt(a, b):
    return lax.dot_general(a, b, (((1,), (1,)), ((), ())), preferred_element_type=F32)


def _split3(x):
    hi = x.astype(BF16)
    r1 = x - hi.astype(F32)
    mid = r1.astype(BF16)
    lo = (r1 - mid.astype(F32)).astype(BF16)
    return hi, mid, lo


def _dot_exact_lhs(a_bf16, x):
    hi, mid, lo = _split3(x)
    return _dot(a_bf16, hi) + _dot(a_bf16, mid) + _dot(a_bf16, lo)


def _dot_exact_rhs(x, b_bf16):
    hi, mid, lo = _split3(x)
    return _dot(hi, b_bf16) + _dot(mid, b_bf16) + _dot(lo, b_bf16)


def _layer_norm(x, g, b):
    mu = jnp.mean(x, axis=-1, keepdims=True)
    xc = x - mu
    var = jnp.mean(xc * xc, axis=-1, keepdims=True)
    return xc * lax.rsqrt(var + LN_EPS) * g + b


def _sigmoid(x):
    return 1.0 / (1.0 + jnp.exp(-x))


def _log1p_exp_neg_abs(x):
    return jnp.log(1.0 + jnp.exp(-jnp.abs(x)))


def _inproj_kernel(x_ref, g_ref, b_ref, wbig_ref, wg_ref,
                   z_ref, xbc_ref, q_ref, k_ref, v_ref, o_ref, gf_ref, gb_ref):
    h = _layer_norm(x_ref[...], g_ref[...], b_ref[...]).astype(BF16)

    def mm(c0, c1):
        return _dot(h, wbig_ref[:, c0:c1]).astype(BF16)

    z_ref[...] = mm(0, 1024)
    xbc_ref[:, 0:1024] = mm(1024, 2048)
    xbc_ref[:, 1024:2048] = mm(2048, 3072)
    q_ref[...] = mm(3072, 3584)
    k_ref[...] = mm(3584, 4096)
    v_ref[...] = mm(4096, 5120)
    o_ref[...] = mm(5120, 6144)
    gates = _dot(h, wg_ref[...])
    gf_ref[...] = gates[:, :LANES]
    gb_ref[...] = gates[:, LANES:]


def _inproj(hp, ln_g, ln_b, w_big, w_gates):
    rows = hp.shape[0]
    tm = INPROJ_ROWS
    assert rows % tm == 0
    row_spec = lambda n: pl.BlockSpec((tm, n), lambda i: (i, 0))
    const = lambda a: pl.BlockSpec(a.shape, lambda i: (0,) * a.ndim)
    out_shapes = [
        jax.ShapeDtypeStruct((rows, 1024), BF16),
        jax.ShapeDtypeStruct((rows, CONV_CH), BF16),
        jax.ShapeDtypeStruct((rows, 512), BF16),
        jax.ShapeDtypeStruct((rows, 512), BF16),
        jax.ShapeDtypeStruct((rows, 1024), BF16),
        jax.ShapeDtypeStruct((rows, 1024), BF16),
        jax.ShapeDtypeStruct((rows, LANES), F32),
        jax.ShapeDtypeStruct((rows, LANES), F32),
    ]
    return pl.pallas_call(
        _inproj_kernel,
        grid=(rows // tm,),
        in_specs=[row_spec(D_MODEL), const(ln_g), const(ln_b), const(w_big), const(w_gates)],
        out_specs=[row_spec(s.shape[1]) for s in out_shapes],
        out_shape=out_shapes,
        compiler_params=pltpu.CompilerParams(
            dimension_semantics=("arbitrary",), vmem_limit_bytes=VMEM_LIMIT_BYTES),
        name="inproj",
    )(hp, ln_g, ln_b, w_big, w_gates)


def _conv_kernel(prev_ref, main_ref, next_ref, w_ref, b_ref, xs_ref, bm_ref, cm_ref, *, n_chunks):
    c = pl.program_id(1)
    row = lax.broadcasted_iota(jnp.int32, (CHUNK, 1), 0)
    pad_rows = jnp.logical_and(c == 0, row < PAD_FRONT)
    main = jnp.where(pad_rows, 0.0, main_ref[...].astype(F32))
    prev = jnp.where(c == 0, 0.0, prev_ref[...].astype(F32))[BF16_SUBLANES - 8:, :]
    nxt = jnp.where(c == n_chunks - 1, 0.0, next_ref[...].astype(F32))[:8, :]
    xp = jnp.concatenate([prev, main, nxt], axis=0)
    w = w_ref[...]
    acc = b_ref[...] + xp[8 - CONV_HALF:8 - CONV_HALF + CHUNK, :] * w[0:1, :]
    for t in range(1, CONV_W):
        off = 8 - CONV_HALF + t
        acc = acc + xp[off:off + CHUNK, :] * w[t:t + 1, :]
    y = acc * _sigmoid(acc)
    y = jnp.where(pad_rows, 0.0, y)
    xs_ref[...] = y[:, :D_MODEL].astype(BF16)
    bm_ref[...] = y[:, D_MODEL:D_MODEL + 512].astype(BF16)
    cm_ref[...] = y[:, D_MODEL + 512:].astype(BF16)


def _conv(xbc, conv_w8, conv_b, n_seq, n_chunks):
    rows = xbc.shape[0]
    halo_blocks_per_chunk = CHUNK // BF16_SUBLANES
    last_halo = rows // BF16_SUBLANES - 1

    def main_map(b, c):
        return (b * n_chunks + c, 0)

    def prev_map(b, c):
        return (jnp.maximum((b * n_chunks + c) * halo_blocks_per_chunk - 1, 0), 0)

    def next_map(b, c):
        return (jnp.minimum((b * n_chunks + c + 1) * halo_blocks_per_chunk, last_halo), 0)

    const = lambda a: pl.BlockSpec(a.shape, lambda b, c: (0,) * a.ndim)
    out_shapes = [jax.ShapeDtypeStruct((rows, D_MODEL), BF16),
                  jax.ShapeDtypeStruct((rows, 512), BF16),
                  jax.ShapeDtypeStruct((rows, 512), BF16)]
    return pl.pallas_call(
        functools.partial(_conv_kernel, n_chunks=n_chunks),
        grid=(n_seq, n_chunks),
        in_specs=[pl.BlockSpec((BF16_SUBLANES, CONV_CH), prev_map),
                  pl.BlockSpec((CHUNK, CONV_CH), main_map),
                  pl.BlockSpec((BF16_SUBLANES, CONV_CH), next_map),
                  const(conv_w8), const(conv_b)],
        out_specs=[pl.BlockSpec((CHUNK, s.shape[1]), main_map) for s in out_shapes],
        out_shape=out_shapes,
        compiler_params=pltpu.CompilerParams(
            dimension_semantics=("arbitrary", "arbitrary"), vmem_limit_bytes=VMEM_LIMIT_BYTES),
        name="conv",
    )(xbc, xbc, xbc, conv_w8, conv_b)


def _mixer_kernel(*refs, reverse, final, n_chunks):
    if final:
        (xs_ref, bm_ref, cm_ref, g_ref, q_ref, k_ref, v_ref, z_ref, o_ref, yb_ref, hb_ref,
         gbias_ref, alog_ref, expand_ref, dskip_ref, ngs_ref, ngm_ref,
         ycat_ref, s_ref, cst_ref, m_ref) = refs
    else:
        (xs_ref, bm_ref, cm_ref, g_ref, q_ref, k_ref, v_ref,
         gbias_ref, alog_ref, expand_ref,
         yout_ref, hout_ref, s_ref, cst_ref, m_ref) = refs

    t = pl.program_id(1)
    c = (n_chunks - 1 - t) if reverse else t
    end = 0 if reverse else CHUNK - 1

    @pl.when(t == 0)
    def _():
        s_ref[...] = jnp.zeros_like(s_ref)
        cst_ref[...] = jnp.zeros_like(cst_ref)
        m_ref[...] = jnp.zeros_like(m_ref)

    row = lax.broadcasted_iota(jnp.int32, (CHUNK, 1), 0)
    col = lax.broadcasted_iota(jnp.int32, (1, CHUNK), 1)
    lane = col
    allowed = (col >= row) if reverse else (col <= row)
    tri = allowed.astype(BF16)
    tri_t = ((row >= col) if reverse else (row <= col)).astype(BF16)

    gr = g_ref[...] + gbias_ref[...]
    lse = _log1p_exp_neg_abs(gr)
    softplus = jnp.maximum(gr, 0.0) + lse
    logsig = jnp.minimum(gr, 0.0) - lse
    is_dt = lane < GATE_I0
    is_i = jnp.logical_and(lane >= GATE_I0, lane < GATE_F0)
    is_f = jnp.logical_and(lane >= GATE_F0, lane < GATE_END)
    val = jnp.where(is_dt, softplus, jnp.where(is_i, gr, jnp.where(is_f, logsig, 0.0)))
    pad_rows = jnp.logical_and(c == 0, row < PAD_FRONT)
    val = jnp.where(pad_rows, jnp.where(is_i, NEG_GATE, 0.0), val)
    a_coef = -jnp.exp(alog_ref[...])
    u = jnp.where(is_dt, val * a_coef, jnp.where(is_f, val, 0.0))
    val_t = val.T
    u_t = u.T
    cums = _dot_exact_lhs(tri, u)
    cums_t = _dot_exact_rhs(u_t, tri_t)

    cums_end = cums[end:end + 1, :]
    p1 = jnp.exp(cums)
    p2 = jnp.exp(cums_end - cums) * val
    expand = expand_ref[...]
    ex1 = _dot(p1.astype(BF16), expand)
    ex2 = _dot(p2.astype(BF16), expand)
    chunk_decay = _dot_exact_rhs(jnp.broadcast_to(jnp.exp(cums_end), (8, LANES)), expand)[0:1, :]
    xs = xs_ref[...]
    xsf = xs.astype(F32)
    xs_w = (xsf * ex2).astype(BF16)
    left_half = lane < SSD_HEAD_DIM
    top_half = row < MLSTM_DK
    y_groups = []
    for g in range(SSD_GROUPS):
        cg = cm_ref[:, g * SSD_STATE:(g + 1) * SSD_STATE]
        bg = bm_ref[:, g * SSD_STATE:(g + 1) * SSD_STATE]
        cb = _dot_nt(cg, bg)
        ys = []
        for pp in range(HEADS_PER_GROUP // 2):
            pair = g * (HEADS_PER_GROUP // 2) + pp
            xs_pair = xs[:, pair * LANES:(pair + 1) * LANES]
            y_pair = None
            for hh in range(2):
                h = 2 * pair + hh
                seg = cums[:, h:h + 1] - cums_t[h:h + 1, :]
                dec = jnp.exp(jnp.where(allowed, seg, -jnp.inf))
                m_mat = (cb * dec * val_t[h:h + 1, :]).astype(BF16)
                keep = left_half if hh == 0 else jnp.logical_not(left_half)
                part = _dot(m_mat, jnp.where(keep, xs_pair, jnp.zeros_like(xs_pair)))
                y_pair = part if y_pair is None else y_pair + part
            ys.append(y_pair)
        y_diag = jnp.concatenate(ys, axis=1)
        gs = slice(g * GROUP_WIDTH, (g + 1) * GROUP_WIDTH)
        s_g = s_ref[g]
        y_off = _dot(cg, s_g.astype(BF16)) * ex1[:, gs]
        y_groups.append(y_diag + y_off)
        bg_t = bg.astype(F32).T.astype(BF16)
        s_ref[g] = chunk_decay[:, gs] * s_g + _dot(bg_t, xs_w[:, gs])
    y_ssd = jnp.concatenate(y_groups, axis=1)

    bcum_t = cums_t[GATE_F0:GATE_END, :]
    ip_t = val_t[GATE_I0:GATE_F0, :]
    g_t = bcum_t[:, end:end + 1]
    a_t = g_t - bcum_t + ip_t
    a_max = jnp.max(a_t, axis=1, keepdims=True)
    w_t = jnp.exp(a_t - a_max)
    m_prev = m_ref[...][:, 0:1]
    m_new = jnp.maximum(g_t + m_prev, a_max)
    s_prev = jnp.exp(g_t + m_prev - m_new)
    s_new = jnp.exp(a_max - m_new)
    r_t = ip_t - bcum_t
    ones_blk = jnp.ones((CHUNK, MLSTM_DV), BF16)
    h_heads = []
    for pair in range(MLSTM_HEADS // 2):
        h0, h1 = 2 * pair, 2 * pair + 1
        q_pair = q_ref[:, pair * LANES:(pair + 1) * LANES]
        k_pair = k_ref[:, pair * LANES:(pair + 1) * LANES]
        cst = cst_ref[pair]
        cst_b = cst.astype(BF16)
        v_pair = []
        for hh, h in enumerate((h0, h1)):
            keep = left_half if hh == 0 else jnp.logical_not(left_half)
            qh = jnp.where(keep, q_pair, jnp.zeros_like(q_pair))
            vh = v_ref[:, h * MLSTM_DV:(h + 1) * MLSTM_DV]
            v_pair.append(vh)
            qk = _dot_nt(qh, k_pair)
            bc = cums[:, GATE_F0 + h:GATE_F0 + h + 1]
            dlog = jnp.where(allowed, bc + r_t[h:h + 1, :], -jnp.inf)
            m_intra = jnp.max(dlog, axis=1, keepdims=True)
            m_inter = bc + m_prev[h:h + 1, :]
            m_t = jnp.maximum(m_inter, m_intra)
            s_mat = (qk * jnp.exp(dlog - m_t)).astype(BF16)
            intra = _dot(s_mat, jnp.concatenate([vh, ones_blk], axis=1))
            inter = _dot(qh, cst_b)
            tot = jnp.exp(m_inter - m_t) * inter + intra
            num = tot[:, :MLSTM_DV]
            den = tot[:, MLSTM_DV:]
            h_heads.append(num / jnp.maximum(jnp.abs(den), jnp.exp(-m_t)))
        w_rows = jnp.where(top_half, w_t[h0:h0 + 1, :], w_t[h1:h1 + 1, :])
        kw = (k_pair.astype(F32).T * w_rows).astype(BF16)
        full = _dot(kw, jnp.concatenate([v_pair[0], v_pair[1], ones_blk], axis=1))
        kvn = jnp.concatenate(
            [jnp.where(top_half, full[:, :MLSTM_DV], full[:, MLSTM_DV:2 * MLSTM_DV]),
             full[:, 2 * MLSTM_DV:]], axis=1)
        sp_rows = jnp.where(top_half, s_prev[h0:h0 + 1, :], s_prev[h1:h1 + 1, :])
        sn_rows = jnp.where(top_half, s_new[h0:h0 + 1, :], s_new[h1:h1 + 1, :])
        cst_ref[pair] = sp_rows * cst + sn_rows * kvn
    m_ref[...] = jnp.broadcast_to(m_new, m_ref.shape)
    h_ml = jnp.concatenate(h_heads, axis=1)

    if not final:
        @pl.when(c > 0)
        def _():
            yout_ref[...] = y_ssd.astype(BF16)
            hout_ref[...] = h_ml.astype(BF16)
    else:
        @pl.when(c > 0)
        def _():
            y_tot = y_ssd + yb_ref[...].astype(F32) + dskip_ref[...] * xsf
            zz = z_ref[...].astype(F32)
            y2 = y_tot * (zz * _sigmoid(zz))
            y_n = y2 * lax.rsqrt(jnp.mean(y2 * y2, axis=-1, keepdims=True) + RMS_EPS) * ngs_ref[...]
            h_tot = h_ml + hb_ref[...].astype(F32)
            segs = []
            for h in range(MLSTM_HEADS):
                seg = h_tot[:, h * MLSTM_DV:(h + 1) * MLSTM_DV]
                segs.append(seg * lax.rsqrt(jnp.mean(seg * seg, axis=-1, keepdims=True) + RMS_EPS))
            h_n = jnp.concatenate(segs, axis=1) * ngm_ref[...]
            y_ml = _sigmoid(o_ref[...].astype(F32)) * h_n
            ycat_ref[:, :D_MODEL] = y_n.astype(BF16)
            ycat_ref[:, D_MODEL:] = y_ml.astype(BF16)


def _mixer_pass(xs, bm, cm, gates, q, k, v, gbias, alog, expand, n_seq, n_chunks, *,
                reverse, final_inputs=None):
    final = final_inputs is not None
    n_out_rows = n_seq * (n_chunks - 1) * CHUNK

    def chunk_of(t):
        return (n_chunks - 1 - t) if reverse else t

    def pad_map(b, t):
        return (b * n_chunks + chunk_of(t), 0)

    def out_map(b, t):
        return (b * (n_chunks - 1) + jnp.maximum(chunk_of(t) - 1, 0), 0)

    const = lambda a: pl.BlockSpec(a.shape, lambda b, t: (0,) * a.ndim)
    pad_spec = lambda n: pl.BlockSpec((CHUNK, n), pad_map)
    out_spec = lambda n: pl.BlockSpec((CHUNK, n), out_map)
    in_arrays = [xs, bm, cm, gates, q, k, v]
    in_specs = [pad_spec(a.shape[1]) for a in in_arrays]
    if final:
        z, o, yb, hb, dskip, ngs, ngm = final_inputs
        in_arrays += [z, o, yb, hb]
        in_specs += [pad_spec(1024), pad_spec(1024), out_spec(1024), out_spec(1024)]
        in_arrays += [gbias, alog, expand, dskip, ngs, ngm]
        in_specs += [const(a) for a in (gbias, alog, expand, dskip, ngs, ngm)]
        out_shape = [jax.ShapeDtypeStruct((n_out_rows, 2 * D_MODEL), BF16)]
        out_specs = [out_spec(2 * D_MODEL)]
    else:
        in_arrays += [gbias, alog, expand]
        in_specs += [const(a) for a in (gbias, alog, expand)]
        out_shape = [jax.ShapeDtypeStruct((n_out_rows, D_MODEL), BF16),
                     jax.ShapeDtypeStruct((n_out_rows, D_MODEL), BF16)]
        out_specs = [out_spec(D_MODEL), out_spec(D_MODEL)]
    return pl.pallas_call(
        functools.partial(_mixer_kernel, reverse=reverse, final=final, n_chunks=n_chunks),
        grid=(n_seq, n_chunks),
        in_specs=in_specs,
        out_specs=out_specs,
        out_shape=out_shape,
        scratch_shapes=[pltpu.VMEM((SSD_GROUPS, SSD_STATE, GROUP_WIDTH), F32),
                        pltpu.VMEM((MLSTM_HEADS // 2, 2 * MLSTM_DK, 2 * MLSTM_DV), F32),
                        pltpu.VMEM((MLSTM_HEADS, LANES), F32)],
        compiler_params=pltpu.CompilerParams(
            dimension_semantics=("arbitrary", "arbitrary"), vmem_limit_bytes=VMEM_LIMIT_BYTES),
        name="mixer_fwd" if final else "mixer_bwd",
    )(*in_arrays)


def _epilogue_kernel(x_ref, ycat_ref, wout_ref, lng0_ref, lnb0_ref, lng1_ref, lnb1_ref,
                     wrh_ref, wrl_ref, br_ref, h1_ref, sel_ref, gate_ref):
    h0 = _layer_norm(x_ref[...], lng0_ref[...], lnb0_ref[...])
    mix = _dot(ycat_ref[...], wout_ref[...])
    h1 = _layer_norm(DEEPNORM_ALPHA * h0 + mix, lng1_ref[...], lnb1_ref[...])
    h1_ref[...] = h1
    hh = h1.astype(BF16)
    hl = (h1 - hh.astype(F32)).astype(BF16)
    wrh = wrh_ref[...]
    logits = _dot(hh, wrh) + _dot(hl, wrh) + _dot(hh, wrl_ref[...]) + br_ref[...]
    lane = lax.broadcasted_iota(jnp.int32, (1, LANES), 1)
    lane_f = lane.astype(F32)
    logits = jnp.where(lane < N_EXPERTS, logits, -jnp.inf)
    work = logits
    sel = jnp.zeros(logits.shape, jnp.bool_)
    top = None
    for _ in range(TOP_K):
        m = jnp.max(work, axis=-1, keepdims=True)
        if top is None:
            top = m
        first = jnp.min(jnp.where(work == m, lane_f, float(LANES)), axis=-1, keepdims=True)
        pick = lane_f == first
        sel = jnp.logical_or(sel, pick)
        work = jnp.where(pick, -jnp.inf, work)
    e = jnp.where(sel, jnp.exp(logits - top), 0.0)
    gate_ref[...] = e / jnp.sum(e, axis=-1, keepdims=True)
    sel_ref[...] = sel.astype(F32)


def _epilogue(x_all, ycat, w_out, lng0, lnb0, lng1, lnb1, wrh, wrl, br):
    rows = x_all.shape[0]
    tm = EPILOGUE_ROWS
    assert rows % tm == 0
    row_spec = lambda n: pl.BlockSpec((tm, n), lambda i: (i, 0))
    const = lambda a: pl.BlockSpec(a.shape, lambda i: (0,) * a.ndim)
    out_shape = [jax.ShapeDtypeStruct((rows, D_MODEL), F32),
                 jax.ShapeDtypeStruct((rows, LANES), F32),
                 jax.ShapeDtypeStruct((rows, LANES), F32)]
    consts = (w_out, lng0, lnb0, lng1, lnb1, wrh, wrl, br)
    return pl.pallas_call(
        _epilogue_kernel,
        grid=(rows // tm,),
        in_specs=[row_spec(D_MODEL), row_spec(2 * D_MODEL)] + [const(a) for a in consts],
        out_specs=[row_spec(D_MODEL), row_spec(LANES), row_spec(LANES)],
        out_shape=out_shape,
        compiler_params=pltpu.CompilerParams(
            dimension_semantics=("arbitrary",), vmem_limit_bytes=VMEM_LIMIT_BYTES),
        name="epilogue",
    )(x_all, ycat, *consts)


def _rank_kernel(sel_ref, gate_ref, lstrict_ref, ucum_ref, dest_ref, gk_ref, stats_ref,
                 base_ref, *, trash_row):
    phase = pl.program_id(0)
    i = pl.program_id(1)
    sel = sel_ref[...]
    colsum = jnp.sum(sel, axis=0, keepdims=True)

    @pl.when(jnp.logical_and(phase == 0, i == 0))
    def _():
        base_ref[...] = jnp.zeros_like(base_ref)

    @pl.when(phase == 0)
    def _():
        base_ref[0:1, :] = base_ref[0:1, :] + colsum

    @pl.when(jnp.logical_and(phase == 1, i == 0))
    def _():
        counts = base_ref[0:1, :]
        padded = jnp.ceil(counts / MOE_BLOCK) * MOE_BLOCK
        pend = _dot_exact_rhs(jnp.broadcast_to(padded, (8, LANES)), ucum_ref[...])[0:1, :]
        stats_ref[0:1, :] = counts
        stats_ref[1:2, :] = pend - padded
        stats_ref[2:3, :] = pend
        stats_ref[3:8, :] = jnp.zeros((5, LANES), F32)
        base_ref[1:2, :] = pend - padded

    @pl.when(phase == 1)
    def _():
        before = _dot(lstrict_ref[...], sel.astype(BF16))
        pos = base_ref[1:2, :] + before
        base_ref[1:2, :] = base_ref[1:2, :] + colsum
        work = jnp.where(sel > 0.0, pos + 1.0, 0.0)
        gates = gate_ref[...]
        for kk in range(TOP_K):
            m = jnp.max(work, axis=-1, keepdims=True)
            pick = jnp.logical_and(work == m, m > 0.0)
            gk_ref[:, kk:kk + 1] = jnp.sum(jnp.where(pick, gates, 0.0), axis=-1, keepdims=True)
            dest_ref[:, kk:kk + 1] = jnp.where(m > 0.0, m - 1.0, float(trash_row)).astype(jnp.int32)
            work = jnp.where(pick, 0.0, work)


def _rank(sel, gates, lstrict, ucum, trash_row):
    rows = sel.shape[0]
    tm = RANK_ROWS
    assert rows % tm == 0
    row_spec = lambda n: pl.BlockSpec((tm, n), lambda p, i: (i, 0))
    out_row_spec = lambda n: pl.BlockSpec((tm, n), lambda p, i: (i * p, 0))
    const = lambda a: pl.BlockSpec(a.shape, lambda p, i: (0,) * a.ndim)
    return pl.pallas_call(
        functools.partial(_rank_kernel, trash_row=trash_row),
        grid=(2, rows // tm),
        in_specs=[row_spec(LANES), row_spec(LANES), const(lstrict), const(ucum)],
        out_specs=[out_row_spec(TOP_K), out_row_spec(TOP_K),
                   pl.BlockSpec((8, LANES), lambda p, i: (0, 0))],
        out_shape=[jax.ShapeDtypeStruct((rows, TOP_K), jnp.int32),
                   jax.ShapeDtypeStruct((rows, TOP_K), F32),
                   jax.ShapeDtypeStruct((8, LANES), F32)],
        scratch_shapes=[pltpu.VMEM((8, LANES), F32)],
        compiler_params=pltpu.CompilerParams(
            dimension_semantics=("arbitrary", "arbitrary"), vmem_limit_bytes=VMEM_LIMIT_BYTES),
        name="rank",
    )(sel, gates, lstrict, ucum)


def _dispatch_kernel(cnt_ref, start_ref, pend_ref, dest_ref, h_ref, xb_ref, zrow_ref, sems,
                     *, rows_per_step):
    i = pl.program_id(0)

    def row_copy(t, d):
        return pltpu.make_async_copy(h_ref.at[pl.ds(t, 1), :], xb_ref.at[pl.ds(d, 1), :], sems.at[0])

    def pad_copy(d):
        return pltpu.make_async_copy(zrow_ref.at[pl.ds(0, 1), :], xb_ref.at[pl.ds(d, 1), :], sems.at[1])

    @pl.when(i == 0)
    def _():
        zrow_ref[...] = jnp.zeros_like(zrow_ref)

        def per_expert(e, carry):
            first = start_ref[e] + cnt_ref[e]
            n_pad = pend_ref[e] - first

            def start(r, cc):
                pad_copy(first + r).start()
                return cc

            def wait(r, cc):
                pad_copy(first + r).wait()
                return cc

            lax.fori_loop(0, n_pad, start, 0)
            lax.fori_loop(0, n_pad, wait, 0)
            return carry

        lax.fori_loop(0, N_EXPERTS, per_expert, 0)

    def start_rows(t, carry):
        for kk in range(TOP_K):
            row_copy(t, dest_ref[0, 0, t * TOP_K + kk]).start()
        return carry

    def wait_rows(t, carry):
        for kk in range(TOP_K):
            row_copy(t, dest_ref[0, 0, t * TOP_K + kk]).wait()
        return carry

    lax.fori_loop(0, rows_per_step, start_rows, 0)
    lax.fori_loop(0, rows_per_step, wait_rows, 0)


def _dispatch(counts, starts, pends, dest, h1, n_buf_rows):
    rows = h1.shape[0]
    tm = DISPATCH_ROWS
    assert rows % tm == 0
    dest3 = dest.reshape(rows // tm, 1, tm * TOP_K)
    grid_spec = pltpu.PrefetchScalarGridSpec(
        num_scalar_prefetch=3,
        grid=(rows // tm,),
        in_specs=[pl.BlockSpec((1, 1, tm * TOP_K), lambda i, *_: (i, 0, 0), memory_space=pltpu.SMEM),
                  pl.BlockSpec((tm, D_MODEL), lambda i, *_: (i, 0))],
        out_specs=pl.BlockSpec(memory_space=pl.ANY),
        scratch_shapes=[pltpu.VMEM((8, D_MODEL), F32), pltpu.SemaphoreType.DMA((2,))],
    )
    return pl.pallas_call(
        functools.partial(_dispatch_kernel, rows_per_step=tm),
        grid_spec=grid_spec,
        out_shape=jax.ShapeDtypeStruct((n_buf_rows, D_MODEL), F32),
        compiler_params=pltpu.CompilerParams(
            dimension_semantics=("arbitrary",), vmem_limit_bytes=VMEM_LIMIT_BYTES,
            has_side_effects=True),
        name="dispatch",
    )(counts, starts, pends, dest3, h1)


def _ffn_kernel(blk_ref, exp_ref, nused_ref, xb_ref, w1_ref, b1_ref, w2_ref, b2_ref, yb_ref):
    j = pl.program_id(0)

    @pl.when(j < nused_ref[0])
    def _():
        x = xb_ref[...].astype(BF16)
        hc = _dot(x, w1_ref[0]) + b1_ref[0]
        gate = jnp.minimum(hc[:, :D_FF], SWIGLU_LIMIT)
        up = jnp.clip(hc[:, D_FF:], -SWIGLU_LIMIT, SWIGLU_LIMIT)
        act = (up + 1.0) * gate * _sigmoid(SWIGLU_ALPHA * gate)
        yb_ref[...] = _dot(act.astype(BF16), w2_ref[0]) + b2_ref[0]


def _ffn(blk_idx, blk_exp, n_used, xb, w1, b1, w2, b2, n_blocks):
    bm = MOE_BLOCK
    grid_spec = pltpu.PrefetchScalarGridSpec(
        num_scalar_prefetch=3,
        grid=(n_blocks,),
        in_specs=[pl.BlockSpec((bm, D_MODEL), lambda j, bi, be, nu: (bi[j], 0)),
                  pl.BlockSpec((1, D_MODEL, 2 * D_FF), lambda j, bi, be, nu: (be[j], 0, 0)),
                  pl.BlockSpec((1, 1, 2 * D_FF), lambda j, bi, be, nu: (be[j], 0, 0)),
                  pl.BlockSpec((1, D_FF, D_MODEL), lambda j, bi, be, nu: (be[j], 0, 0)),
                  pl.BlockSpec((1, 1, D_MODEL), lambda j, bi, be, nu: (be[j], 0, 0))],
        out_specs=pl.BlockSpec((bm, D_MODEL), lambda j, bi, be, nu: (bi[j], 0)),
    )
    return pl.pallas_call(
        _ffn_kernel,
        grid_spec=grid_spec,
        out_shape=jax.ShapeDtypeStruct(xb.shape, F32),
        compiler_params=pltpu.CompilerParams(
            dimension_semantics=("arbitrary",), vmem_limit_bytes=VMEM_LIMIT_BYTES),
        name="expert_ffn",
    )(blk_idx, blk_exp, n_used, xb, w1, b1, w2, b2)


def _combine_kernel(dest_ref, gk_ref, h1_ref, yb_ref, lng_ref, lnb_ref, out_ref, buf_ref, sem,
                    *, rows_per_step):
    def row_copy(t, kk):
        d = dest_ref[0, 0, t * TOP_K + kk]
        return pltpu.make_async_copy(yb_ref.at[pl.ds(d, 1), :], buf_ref.at[kk, pl.ds(t, 1), :], sem.at[0])

    def start_rows(t, carry):
        for kk in range(TOP_K):
            row_copy(t, kk).start()
        return carry

    def wait_rows(t, carry):
        for kk in range(TOP_K):
            row_copy(t, kk).wait()
        return carry

    lax.fori_loop(0, rows_per_step, start_rows, 0)
    lax.fori_loop(0, rows_per_step, wait_rows, 0)
    gk = gk_ref[...]
    ffn = gk[:, 0:1] * buf_ref[0]
    for kk in range(1, TOP_K):
        ffn = ffn + gk[:, kk:kk + 1] * buf_ref[kk]
    out_ref[...] = _layer_norm(DEEPNORM_ALPHA * h1_ref[...] + ffn, lng_ref[...], lnb_ref[...])


def _combine(dest, gk, h1, yb, lng, lnb):
    rows = h1.shape[0]
    tm = COMBINE_ROWS
    assert rows % tm == 0
    dest3 = dest.reshape(rows // tm, 1, tm * TOP_K)
    const = lambda a: pl.BlockSpec(a.shape, lambda i: (0,) * a.ndim)
    return pl.pallas_call(
        functools.partial(_combine_kernel, rows_per_step=tm),
        grid=(rows // tm,),
        in_specs=[pl.BlockSpec((1, 1, tm * TOP_K), lambda i: (i, 0, 0), memory_space=pltpu.SMEM),
                  pl.BlockSpec((tm, TOP_K), lambda i: (i, 0)),
                  pl.BlockSpec((tm, D_MODEL), lambda i: (i, 0)),
                  pl.BlockSpec(memory_space=pl.ANY),
                  const(lng), const(lnb)],
        out_specs=pl.BlockSpec((tm, D_MODEL), lambda i: (i, 0)),
        out_shape=jax.ShapeDtypeStruct((rows, D_MODEL), F32),
        scratch_shapes=[pltpu.VMEM((TOP_K, tm, D_MODEL), F32), pltpu.SemaphoreType.DMA((1,))],
        compiler_params=pltpu.CompilerParams(
            dimension_semantics=("arbitrary",), vmem_limit_bytes=VMEM_LIMIT_BYTES),
        name="combine",
    )(dest3, gk, h1, yb, lng, lnb)


def _row(v, width=None):
    v = v.reshape(1, -1).astype(F32)
    if width is not None and v.shape[1] < width:
        v = jnp.pad(v, ((0, 0), (0, width - v.shape[1])))
    return v


def _encode_all(x_all, meta_tokens, ln_emb_g, ln_emb_b, w_in, conv_w, conv_b, dt_bias, a_log, d_skip,
                ssd_norm_g, i_bias, f_bias, mlstm_norm_g, w_out, ln1_g, ln1_b, w_router, b_router,
                w1, b1, w2, b2, ln2_g, ln2_b):
    n_seq, seq_len, _ = x_all.shape
    assert seq_len % CHUNK == 0
    n_chunks = seq_len // CHUNK + 1
    n_tok = n_seq * seq_len

    sizes = (1024, CONV_CH, 2 * SSD_HEADS, 512, 512, 1024, 1024, 2 * MLSTM_HEADS, 2 * MLSTM_HEADS)
    offs = [0]
    for s in sizes:
        offs.append(offs[-1] + s)
    w_z, w_xbc, w_dt, w_q, w_k, w_v, w_o, w_i, w_f = [w_in[:, offs[j]:offs[j + 1]] for j in range(9)]
    w_big = jnp.concatenate([w_z, w_xbc, w_q, w_k * (MLSTM_DK ** -0.5), w_v, w_o], axis=1).astype(BF16)
    zpad = jnp.zeros((D_MODEL, LANES - GATE_END), F32)
    gate_cols = []
    for d in range(2):
        gate_cols += [w_dt[:, d * SSD_HEADS:(d + 1) * SSD_HEADS],
                      w_i[:, d * MLSTM_HEADS:(d + 1) * MLSTM_HEADS],
                      w_f[:, d * MLSTM_HEADS:(d + 1) * MLSTM_HEADS], zpad]
    w_gates = jnp.concatenate(gate_cols, axis=1).astype(BF16)
    gbias = [_row(jnp.concatenate([dt_bias[d], i_bias[d], f_bias[d]]), LANES) for d in range(2)]
    alog = [_row(a_log[d], LANES) for d in range(2)]
    head_of_col = jnp.arange(D_MODEL, dtype=jnp.int32) // SSD_HEAD_DIM
    expand = (jnp.arange(LANES, dtype=jnp.int32)[:, None] == head_of_col[None, :]).astype(BF16)
    dskip = _row(jnp.repeat(d_skip, SSD_HEAD_DIM))
    conv_w8 = jnp.pad(conv_w.astype(F32), ((0, 8 - CONV_W), (0, 0)))

    meta = jnp.broadcast_to(meta_tokens[None].astype(F32), (n_seq, N_META, D_MODEL))
    hp = jnp.concatenate([jnp.zeros((n_seq, PAD_FRONT, D_MODEL), F32), meta, x_all], axis=1)
    hp = hp.reshape(n_seq * n_chunks * CHUNK, D_MODEL)

    z, xbc, q, k, v, o, gates_f, gates_b = _inproj(hp, _row(ln_emb_g), _row(ln_emb_b), w_big, w_gates)
    xs, bm, cm = _conv(xbc, conv_w8, _row(conv_b), n_seq, n_chunks)
    yb, hb = _mixer_pass(xs, bm, cm, gates_b, q, k, v, gbias[1], alog[1], expand, n_seq, n_chunks,
                         reverse=True)
    (ycat,) = _mixer_pass(xs, bm, cm, gates_f, q, k, v, gbias[0], alog[0], expand, n_seq, n_chunks,
                          reverse=False,
                          final_inputs=(z, o, yb, hb, dskip, _row(ssd_norm_g), _row(mlstm_norm_g)))

    wr = jnp.pad(w_router.astype(F32), ((0, 0), (0, LANES - N_EXPERTS)))
    wrh = wr.astype(BF16)
    wrl = (wr - wrh.astype(F32)).astype(BF16)
    h1, sel, gates = _epilogue(x_all.reshape(n_tok, D_MODEL), ycat, w_out.astype(BF16),
                               _row(ln_emb_g), _row(ln_emb_b), _row(ln1_g), _row(ln1_b),
                               wrh, wrl, _row(b_router, LANES))

    n_blocks = n_tok * TOP_K // MOE_BLOCK + N_EXPERTS
    trash_row = n_blocks * MOE_BLOCK
    n_buf_rows = trash_row + MOE_BLOCK
    r_i = jnp.arange(RANK_ROWS, dtype=jnp.int32)
    lstrict = (r_i[None, :] < r_i[:, None]).astype(BF16)
    l_i = jnp.arange(LANES, dtype=jnp.int32)
    ucum = (l_i[:, None] <= l_i[None, :]).astype(BF16)
    dest, gk, stats = _rank(sel, gates, lstrict, ucum, trash_row)
    counts = stats[0, :N_EXPERTS].astype(jnp.int32)
    starts = stats[1, :N_EXPERTS].astype(jnp.int32)
    pends = stats[2, :N_EXPERTS].astype(jnp.int32)
    n_used = pends[N_EXPERTS - 1] // MOE_BLOCK
    blk = jnp.minimum(jnp.arange(n_blocks, dtype=jnp.int32), jnp.maximum(n_used - 1, 0))
    blk_exp = jnp.minimum(jnp.sum((pends[None, :] <= (blk * MOE_BLOCK)[:, None]).astype(jnp.int32), axis=1),
                          N_EXPERTS - 1).astype(jnp.int32)

    xb = _dispatch(counts, starts, pends, dest, h1, n_buf_rows)
    yexp = _ffn(blk, blk_exp, n_used.reshape(1), xb, w1.astype(BF16), b1.reshape(N_EXPERTS, 1, -1),
                w2.astype(BF16), b2.reshape(N_EXPERTS, 1, -1), n_blocks)
    return _combine(dest, gk, h1, yexp, _row(ln2_g), _row(ln2_b))


def kernel(x_prompt, x_sample, meta_tokens, ln_emb_g, ln_emb_b, w_in, conv_w, conv_b, dt_bias, a_log,
           d_skip, ssd_norm_g, i_bias, f_bias, mlstm_norm_g, w_out, ln1_g, ln1_b, w_router, b_router,
           w1, b1, w2, b2, ln2_g, ln2_b):
    assert x_prompt.shape[1:] == x_sample.shape[1:]
    n_p, seq_len, d = x_prompt.shape
    n_s = x_sample.shape[0]
    x_all = jnp.concatenate([x_prompt, x_sample], axis=0).astype(F32)
    y = _encode_all(x_all, meta_tokens, ln_emb_g, ln_emb_b, w_in[0], conv_w[0], conv_b[0], dt_bias[0],
                    a_log[0], d_skip[0], ssd_norm_g[0], i_bias[0], f_bias[0], mlstm_norm_g[0], w_out[0],
                    ln1_g[0], ln1_b[0], w_router[0], b_router[0], w1[0], b1[0], w2[0], b2[0],
                    ln2_g[0], ln2_b[0])
    y = y.reshape(n_p + n_s, seq_len, d)
    return (y[:n_p], y[n_p:])
```

```python
import functools

import jax
import jax.numpy as jnp
from jax import lax
from jax.experimental import pallas as pl
from jax.experimental.pallas import tpu as pltpu

F32 = jnp.float32
BF16 = jnp.bfloat16

D_MODEL = 1024
N_META = 16
CHUNK = 128
PAD_FRONT = CHUNK - N_META
SSD_HEADS = 16
SSD_HEAD_DIM = 64
SSD_GROUPS = 4
SSD_STATE = 128
HEADS_PER_GROUP = SSD_HEADS // SSD_GROUPS
GROUP_WIDTH = HEADS_PER_GROUP * SSD_HEAD_DIM
CONV_W = 5
CONV_HALF = CONV_W // 2
CONV_CH = D_MODEL + 2 * SSD_GROUPS * SSD_STATE
MLSTM_HEADS = 8
MLSTM_DK = 64
MLSTM_DV = 128
N_EXPERTS = 32
TOP_K = 4
D_FF = D_MODEL
SWIGLU_LIMIT = 7.0
SWIGLU_ALPHA = 1.702
DEEPNORM_ALPHA = 2.0 ** 0.25
LN_EPS = 1e-5
RMS_EPS = 1e-5
NEG_GATE = -1e30

LANES = 128
BF16_SUBLANES = 16
VMEM_LIMIT_BYTES = 56 * 1024 * 1024

GATE_DT0, GATE_I0, GATE_F0, GATE_END = 0, SSD_HEADS, SSD_HEADS + MLSTM_HEADS, SSD_HEADS + 2 * MLSTM_HEADS

INPROJ_ROWS = 512
EPILOGUE_ROWS = 512
RANK_ROWS = 512
DISPATCH_ROWS = 512
COMBINE_ROWS = 256
MOE_BLOCK = 256


def _dot(a, b):
    return jnp.dot(a, b, preferred_element_type=F32)


def _dot_nt(a, b):
    return lax.dot_general(a, b, (((1,), (1,)), ((), ())), preferred_element_type=F32)


def _split3(x):
    hi = x.astype(BF16)
    r1 = x - hi.astype(F32)
    mid = r1.astype(BF16)
    lo = (r1 - mid.astype(F32)).astype(BF16)
    return hi, mid, lo


def _dot_exact_lhs(a_bf16, x):
    hi, mid, lo = _split3(x)
    return _dot(a_bf16, hi) + _dot(a_bf16, mid) + _dot(a_bf16, lo)


def _dot_exact_rhs(x, b_bf16):
    hi, mid, lo = _split3(x)
    return _dot(hi, b_bf16) + _dot(mid, b_bf16) + _dot(lo, b_bf16)


def _layer_norm(x, g, b):
    mu = jnp.mean(x, axis=-1, keepdims=True)
    xc = x - mu
    var = jnp.mean(xc * xc, axis=-1, keepdims=True)
    return xc * lax.rsqrt(var + LN_EPS) * g + b


def _sigmoid(x):
    return 1.0 / (1.0 + jnp.exp(-x))


def _log1p_exp_neg_abs(x):
    return jnp.log(1.0 + jnp.exp(-jnp.abs(x)))


def _storage_chunk(c, n_chunks):
    return jnp.where(c == 0, n_chunks - 1, c - 1)


def _inproj_kernel(x_ref, meta_ref, g_ref, b_ref, wbig_ref, wg_ref,
                   z_ref, xbc_ref, q_ref, k_ref, v_ref, o_ref, gf_ref, gb_ref, *, n_x_tiles):
    x = jnp.where(pl.program_id(1) == n_x_tiles, meta_ref[...], x_ref[...])
    h = _layer_norm(x, g_ref[...], b_ref[...]).astype(BF16)

    def mm(c0, c1):
        return _dot(h, wbig_ref[:, c0:c1]).astype(BF16)

    z_ref[...] = mm(0, 1024)
    xbc_ref[:, 0:1024] = mm(1024, 2048)
    xbc_ref[:, 1024:2048] = mm(2048, 3072)
    q_ref[...] = mm(3072, 3584)
    k_ref[...] = mm(3584, 4096)
    v_ref[...] = mm(4096, 5120)
    o_ref[...] = mm(5120, 6144)
    gates = _dot(h, wg_ref[...])
    gf_ref[...] = gates[:, :LANES]
    gb_ref[...] = gates[:, LANES:]


def _inproj(x_all, meta_tile, ln_g, ln_b, w_big, w_gates):
    n_seq, seq_len, _ = x_all.shape
    tm = INPROJ_ROWS
    assert seq_len % tm == 0 and tm >= CHUNK
    n_x_tiles = seq_len // tm
    rows = seq_len + CHUNK
    row_spec = lambda n: pl.BlockSpec((None, tm, n), lambda b, i: (b, i, 0))
    const = lambda a: pl.BlockSpec(a.shape, lambda b, i: (0,) * a.ndim)
    resident = lambda a: pl.BlockSpec(a.shape, lambda b, i: (0,) * a.ndim, pipeline_mode=pl.Buffered(1))
    widths = (1024, CONV_CH, 512, 512, 1024, 1024)
    out_shapes = [jax.ShapeDtypeStruct((n_seq, rows, w), BF16) for w in widths]
    out_shapes += [jax.ShapeDtypeStruct((n_seq, rows, LANES), F32)] * 2
    return pl.pallas_call(
        functools.partial(_inproj_kernel, n_x_tiles=n_x_tiles),
        grid=(n_seq, n_x_tiles + 1),
        in_specs=[pl.BlockSpec((None, tm, D_MODEL), lambda b, i: (b, jnp.minimum(i, n_x_tiles - 1), 0)),
                  resident(meta_tile), const(ln_g), const(ln_b), resident(w_big), resident(w_gates)],
        out_specs=[row_spec(s.shape[2]) for s in out_shapes],
        out_shape=out_shapes,
        compiler_params=pltpu.CompilerParams(
            dimension_semantics=("arbitrary", "arbitrary"), vmem_limit_bytes=VMEM_LIMIT_BYTES),
        name="inproj",
    )(x_all, meta_tile, ln_g, ln_b, w_big, w_gates)


def _conv_kernel(prev_ref, main_ref, next_ref, w_ref, b_ref, xs_ref, bm_ref, cm_ref, *, n_chunks):
    c = pl.program_id(1)
    row = lax.broadcasted_iota(jnp.int32, (CHUNK, 1), 0)
    pad_rows = jnp.logical_and(c == 0, row < PAD_FRONT)
    main = jnp.where(pad_rows, 0.0, main_ref[...].astype(F32))
    prev = jnp.where(c == 0, 0.0, prev_ref[...].astype(F32))[BF16_SUBLANES - 8:, :]
    nxt = jnp.where(c == n_chunks - 1, 0.0, next_ref[...].astype(F32))[:8, :]
    xp = jnp.concatenate([prev, main, nxt], axis=0)
    w = w_ref[...]
    acc = b_ref[...] + xp[8 - CONV_HALF:8 - CONV_HALF + CHUNK, :] * w[0:1, :]
    for t in range(1, CONV_W):
        off = 8 - CONV_HALF + t
        acc = acc + xp[off:off + CHUNK, :] * w[t:t + 1, :]
    y = acc * _sigmoid(acc)
    y = jnp.where(pad_rows, 0.0, y)
    xs_ref[...] = y[:, :D_MODEL].astype(BF16)
    bm_ref[...] = y[:, D_MODEL:D_MODEL + 512].astype(BF16)
    cm_ref[...] = y[:, D_MODEL + 512:].astype(BF16)


def _conv(xbc, conv_w8, conv_b, n_chunks):
    n_seq, rows, _ = xbc.shape
    halo_blocks_per_chunk = CHUNK // BF16_SUBLANES

    def main_map(b, c):
        return (b, _storage_chunk(c, n_chunks), 0)

    def prev_map(b, c):
        sc = _storage_chunk(jnp.maximum(c - 1, 0), n_chunks)
        return (b, sc * halo_blocks_per_chunk + halo_blocks_per_chunk - 1, 0)

    def next_map(b, c):
        sc = _storage_chunk(jnp.minimum(c + 1, n_chunks - 1), n_chunks)
        return (b, sc * halo_blocks_per_chunk, 0)

    const = lambda a: pl.BlockSpec(a.shape, lambda b, c: (0,) * a.ndim)
    out_shapes = [jax.ShapeDtypeStruct((n_seq, rows, D_MODEL), BF16),
                  jax.ShapeDtypeStruct((n_seq, rows, 512), BF16),
                  jax.ShapeDtypeStruct((n_seq, rows, 512), BF16)]
    return pl.pallas_call(
        functools.partial(_conv_kernel, n_chunks=n_chunks),
        grid=(n_seq, n_chunks),
        in_specs=[pl.BlockSpec((None, BF16_SUBLANES, CONV_CH), prev_map),
                  pl.BlockSpec((None, CHUNK, CONV_CH), main_map),
                  pl.BlockSpec((None, BF16_SUBLANES, CONV_CH), next_map),
                  const(conv_w8), const(conv_b)],
        out_specs=[pl.BlockSpec((None, CHUNK, s.shape[2]), main_map) for s in out_shapes],
        out_shape=out_shapes,
        compiler_params=pltpu.CompilerParams(
            dimension_semantics=("arbitrary", "arbitrary"), vmem_limit_bytes=VMEM_LIMIT_BYTES),
        name="conv",
    )(xbc, xbc, xbc, conv_w8, conv_b)


def _mixer_kernel(*refs, reverse, final, n_chunks, n_seq):
    if final:
        (xs_ref, bm_ref, cm_ref, g_ref, q_ref, k_ref, v_ref, z_ref, o_ref, yb_ref, hb_ref,
         gbias_ref, alog_ref, expand_ref, dskip_ref, ngs_ref, ngm_ref,
         ycat_ref, s_ref, cst_ref, m_ref) = refs
    else:
        (xs_ref, bm_ref, cm_ref, g_ref, q_ref, k_ref, v_ref,
         gbias_ref, alog_ref, expand_ref,
         yout_ref, hout_ref, s_ref, cst_ref, m_ref) = refs

    t = pl.program_id(0)
    c = (n_chunks - 1 - t) if reverse else t
    end = 0 if reverse else CHUNK - 1

    @pl.when(t == 0)
    def _():
        s_ref[...] = jnp.zeros_like(s_ref)
        cst_ref[...] = jnp.zeros_like(cst_ref)
        m_ref[...] = jnp.zeros_like(m_ref)

    row = lax.broadcasted_iota(jnp.int32, (CHUNK, 1), 0)
    col = lax.broadcasted_iota(jnp.int32, (1, CHUNK), 1)
    lane = col
    allowed = (col >= row) if reverse else (col <= row)
    tri = allowed.astype(BF16)
    tri_t = ((row >= col) if reverse else (row <= col)).astype(BF16)
    is_dt = lane < GATE_I0
    is_i = jnp.logical_and(lane >= GATE_I0, lane < GATE_F0)
    is_f = jnp.logical_and(lane >= GATE_F0, lane < GATE_END)
    pad_rows = jnp.logical_and(c == 0, row < PAD_FRONT)
    a_coef = -jnp.exp(alog_ref[...])
    expand = expand_ref[...]
    left_half = lane < SSD_HEAD_DIM
    top_half = row < MLSTM_DK
    ones_blk = jnp.ones((CHUNK, MLSTM_DV), BF16)

    def one_sequence(b):
        gr = g_ref[b] + gbias_ref[...]
        lse = _log1p_exp_neg_abs(gr)
        softplus = jnp.maximum(gr, 0.0) + lse
        logsig = jnp.minimum(gr, 0.0) - lse
        val = jnp.where(is_dt, softplus, jnp.where(is_i, gr, jnp.where(is_f, logsig, 0.0)))
        val = jnp.where(pad_rows, jnp.where(is_i, NEG_GATE, 0.0), val)
        u = jnp.where(is_dt, val * a_coef, jnp.where(is_f, val, 0.0))
        val_t = val.T
        u_t = u.T
        cums = _dot_exact_lhs(tri, u)
        cums_t = _dot_exact_rhs(u_t, tri_t)

        cums_end = cums[end:end + 1, :]
        p1 = jnp.exp(cums)
        p2 = jnp.exp(cums_end - cums) * val
        ex1 = _dot(p1.astype(BF16), expand)
        ex2 = _dot(p2.astype(BF16), expand)
        chunk_decay = _dot_exact_rhs(jnp.broadcast_to(jnp.exp(cums_end), (8, LANES)), expand)[0:1, :]
        xs = xs_ref[b]
        xsf = xs.astype(F32)
        xs_w = (xsf * ex2).astype(BF16)
        y_groups = []
        for g in range(SSD_GROUPS):
            cg = cm_ref[b, :, g * SSD_STATE:(g + 1) * SSD_STATE]
            bg = bm_ref[b, :, g * SSD_STATE:(g + 1) * SSD_STATE]
            cb = _dot_nt(cg, bg)
            ys = []
            for pp in range(HEADS_PER_GROUP // 2):
                pair = g * (HEADS_PER_GROUP // 2) + pp
                xs_pair = xs[:, pair * LANES:(pair + 1) * LANES]
                y_pair = None
                for hh in range(2):
                    h = 2 * pair + hh
                    seg = cums[:, h:h + 1] - cums_t[h:h + 1, :]
                    dec = jnp.exp(jnp.where(allowed, seg, -jnp.inf))
                    m_mat = (cb * dec * val_t[h:h + 1, :]).astype(BF16)
                    keep = left_half if hh == 0 else jnp.logical_not(left_half)
                    part = _dot(m_mat, jnp.where(keep, xs_pair, jnp.zeros_like(xs_pair)))
                    y_pair = part if y_pair is None else y_pair + part
                ys.append(y_pair)
            y_diag = jnp.concatenate(ys, axis=1)
            gs = slice(g * GROUP_WIDTH, (g + 1) * GROUP_WIDTH)
            s_g = s_old[b][g]
            y_off = _dot(cg, s_g.astype(BF16)) * ex1[:, gs]
            y_groups.append(y_diag + y_off)
            bg_t = bg.astype(F32).T.astype(BF16)
            s_new_all[b].append(chunk_decay[:, gs] * s_g + _dot(bg_t, xs_w[:, gs]))
        y_ssd = jnp.concatenate(y_groups, axis=1)

        bcum_t = cums_t[GATE_F0:GATE_END, :]
        ip_t = val_t[GATE_I0:GATE_F0, :]
        g_t = bcum_t[:, end:end + 1]
        a_t = g_t - bcum_t + ip_t
        a_max = jnp.max(a_t, axis=1, keepdims=True)
        w_t = jnp.exp(a_t - a_max)
        m_prev = m_old[b][:, 0:1]
        m_new = jnp.maximum(g_t + m_prev, a_max)
        s_prev = jnp.exp(g_t + m_prev - m_new)
        s_new = jnp.exp(a_max - m_new)
        r_t = ip_t - bcum_t
        h_heads = []
        for pair in range(MLSTM_HEADS // 2):
            h0, h1 = 2 * pair, 2 * pair + 1
            q_pair = q_ref[b, :, pair * LANES:(pair + 1) * LANES]
            k_pair = k_ref[b, :, pair * LANES:(pair + 1) * LANES]
            cst = cst_old[b][pair]
            cst_b = cst.astype(BF16)
            v_pair = []
            for hh, h in enumerate((h0, h1)):
                keep = left_half if hh == 0 else jnp.logical_not(left_half)
                qh = jnp.where(keep, q_pair, jnp.zeros_like(q_pair))
                vh = v_ref[b, :, h * MLSTM_DV:(h + 1) * MLSTM_DV]
                v_pair.append(vh)
                qk = _dot_nt(qh, k_pair)
                bc = cums[:, GATE_F0 + h:GATE_F0 + h + 1]
                dlog = jnp.where(allowed, bc + r_t[h:h + 1, :], -jnp.inf)
                m_intra = jnp.max(dlog, axis=1, keepdims=True)
                m_inter = bc + m_prev[h:h + 1, :]
                m_t = jnp.maximum(m_inter, m_intra)
                s_mat = (qk * jnp.exp(dlog - m_t)).astype(BF16)
                intra = _dot(s_mat, jnp.concatenate([vh, ones_blk], axis=1))
                inter = _dot(qh, cst_b)
                tot = jnp.exp(m_inter - m_t) * inter + intra
                num = tot[:, :MLSTM_DV]
                den = tot[:, MLSTM_DV:]
                h_heads.append(num / jnp.maximum(jnp.abs(den), jnp.exp(-m_t)))
            w_rows = jnp.where(top_half, w_t[h0:h0 + 1, :], w_t[h1:h1 + 1, :])
            kw = (k_pair.astype(F32).T * w_rows).astype(BF16)
            full = _dot(kw, jnp.concatenate([v_pair[0], v_pair[1], ones_blk], axis=1))
            kvn = jnp.concatenate(
                [jnp.where(top_half, full[:, :MLSTM_DV], full[:, MLSTM_DV:2 * MLSTM_DV]),
                 full[:, 2 * MLSTM_DV:]], axis=1)
            sp_rows = jnp.where(top_half, s_prev[h0:h0 + 1, :], s_prev[h1:h1 + 1, :])
            sn_rows = jnp.where(top_half, s_new[h0:h0 + 1, :], s_new[h1:h1 + 1, :])
            cst_new_all[b].append(sp_rows * cst + sn_rows * kvn)
        m_new_all[b] = jnp.broadcast_to(m_new, (MLSTM_HEADS, LANES))
        h_ml = jnp.concatenate(h_heads, axis=1)
        if not final:
            return y_ssd.astype(BF16), h_ml.astype(BF16)

        y_tot = y_ssd + yb_ref[b].astype(F32) + dskip_ref[...] * xsf
        zz = z_ref[b].astype(F32)
        y2 = y_tot * (zz * _sigmoid(zz))
        y_n = y2 * lax.rsqrt(jnp.mean(y2 * y2, axis=-1, keepdims=True) + RMS_EPS) * ngs_ref[...]
        h_tot = h_ml + hb_ref[b].astype(F32)
        segs = []
        for h in range(MLSTM_HEADS):
            seg = h_tot[:, h * MLSTM_DV:(h + 1) * MLSTM_DV]
            segs.append(seg * lax.rsqrt(jnp.mean(seg * seg, axis=-1, keepdims=True) + RMS_EPS))
        h_n = jnp.concatenate(segs, axis=1) * ngm_ref[...]
        y_ml = _sigmoid(o_ref[b].astype(F32)) * h_n
        return y_n.astype(BF16), y_ml.astype(BF16)

    s_old = [[s_ref[b, g] for g in range(SSD_GROUPS)] for b in range(n_seq)]
    cst_old = [[cst_ref[b, p] for p in range(MLSTM_HEADS // 2)] for b in range(n_seq)]
    m_old = [m_ref[b] for b in range(n_seq)]
    s_new_all = [[] for _ in range(n_seq)]
    cst_new_all = [[] for _ in range(n_seq)]
    m_new_all = [None] * n_seq
    results = [one_sequence(b) for b in range(n_seq)]
    for b in range(n_seq):
        for g in range(SSD_GROUPS):
            s_ref[b, g] = s_new_all[b][g]
        for p in range(MLSTM_HEADS // 2):
            cst_ref[b, p] = cst_new_all[b][p]
        m_ref[b] = m_new_all[b]

    @pl.when(c > 0)
    def _():
        for b, (first, second) in enumerate(results):
            if final:
                ycat_ref[b, :, :D_MODEL] = first
                ycat_ref[b, :, D_MODEL:] = second
            else:
                yout_ref[b] = first
                hout_ref[b] = second


def _mixer_pass(xs, bm, cm, gates, q, k, v, gbias, alog, expand, n_chunks, *, reverse, final_inputs=None):
    final = final_inputs is not None
    n_seq = xs.shape[0]
    seq_len = (n_chunks - 1) * CHUNK

    def chunk_of(t):
        return (n_chunks - 1 - t) if reverse else t

    def pad_map(t):
        return (0, _storage_chunk(chunk_of(t), n_chunks), 0)

    def out_map(t):
        return (0, jnp.maximum(chunk_of(t) - 1, 0), 0)

    const = lambda a: pl.BlockSpec(a.shape, lambda t: (0,) * a.ndim)
    pad_spec = lambda n: pl.BlockSpec((n_seq, CHUNK, n), pad_map)
    out_spec = lambda n: pl.BlockSpec((n_seq, CHUNK, n), out_map)
    in_arrays = [xs, bm, cm, gates, q, k, v]
    in_specs = [pad_spec(a.shape[2]) for a in in_arrays]
    if final:
        z, o, yb, hb, dskip, ngs, ngm = final_inputs
        in_arrays += [z, o, yb, hb]
        in_specs += [pad_spec(1024), pad_spec(1024), out_spec(1024), out_spec(1024)]
        in_arrays += [gbias, alog, expand, dskip, ngs, ngm]
        in_specs += [const(a) for a in (gbias, alog, expand, dskip, ngs, ngm)]
        out_shape = [jax.ShapeDtypeStruct((n_seq, seq_len, 2 * D_MODEL), BF16)]
        out_specs = [out_spec(2 * D_MODEL)]
    else:
        in_arrays += [gbias, alog, expand]
        in_specs += [const(a) for a in (gbias, alog, expand)]
        out_shape = [jax.ShapeDtypeStruct((n_seq, seq_len, D_MODEL), BF16),
                     jax.ShapeDtypeStruct((n_seq, seq_len, D_MODEL), BF16)]
        out_specs = [out_spec(D_MODEL), out_spec(D_MODEL)]
    return pl.pallas_call(
        functools.partial(_mixer_kernel, reverse=reverse, final=final, n_chunks=n_chunks, n_seq=n_seq),
        grid=(n_chunks,),
        in_specs=in_specs,
        out_specs=out_specs,
        out_shape=out_shape,
        scratch_shapes=[pltpu.VMEM((n_seq, SSD_GROUPS, SSD_STATE, GROUP_WIDTH), F32),
                        pltpu.VMEM((n_seq, MLSTM_HEADS // 2, 2 * MLSTM_DK, 2 * MLSTM_DV), F32),
                        pltpu.VMEM((n_seq, MLSTM_HEADS, LANES), F32)],
        compiler_params=pltpu.CompilerParams(
            dimension_semantics=("arbitrary",), vmem_limit_bytes=VMEM_LIMIT_BYTES),
        name="mixer_fwd" if final else "mixer_bwd",
    )(*in_arrays)


def _epilogue_kernel(x_ref, ycat_ref, wout_ref, lng0_ref, lnb0_ref, lng1_ref, lnb1_ref,
                     wrh_ref, wrl_ref, br_ref, h1_ref, sel_ref, gate_ref):
    h0 = _layer_norm(x_ref[...], lng0_ref[...], lnb0_ref[...])
    mix = _dot(ycat_ref[...], wout_ref[...])
    h1 = _layer_norm(DEEPNORM_ALPHA * h0 + mix, lng1_ref[...], lnb1_ref[...])
    h1_ref[...] = h1
    hh = h1.astype(BF16)
    hl = (h1 - hh.astype(F32)).astype(BF16)
    wrh = wrh_ref[...]
    logits = _dot(hh, wrh) + _dot(hl, wrh) + _dot(hh, wrl_ref[...]) + br_ref[...]
    lane = lax.broadcasted_iota(jnp.int32, (1, LANES), 1)
    lane_f = lane.astype(F32)
    logits = jnp.where(lane < N_EXPERTS, logits, -jnp.inf)
    work = logits
    sel = jnp.zeros(logits.shape, jnp.bool_)
    top = None
    for _ in range(TOP_K):
        m = jnp.max(work, axis=-1, keepdims=True)
        if top is None:
            top = m
        first = jnp.min(jnp.where(work == m, lane_f, float(LANES)), axis=-1, keepdims=True)
        pick = lane_f == first
        sel = jnp.logical_or(sel, pick)
        work = jnp.where(pick, -jnp.inf, work)
    e = jnp.where(sel, jnp.exp(logits - top), 0.0)
    gate_ref[...] = e / jnp.sum(e, axis=-1, keepdims=True)
    sel_ref[...] = sel.astype(F32)


def _epilogue(x_all, ycat, w_out, lng0, lnb0, lng1, lnb1, wrh, wrl, br):
    rows = x_all.shape[0]
    tm = EPILOGUE_ROWS
    assert rows % tm == 0
    row_spec = lambda n: pl.BlockSpec((tm, n), lambda i: (i, 0))
    const = lambda a: pl.BlockSpec(a.shape, lambda i: (0,) * a.ndim)
    out_shape = [jax.ShapeDtypeStruct((rows, D_MODEL), F32),
                 jax.ShapeDtypeStruct((rows, LANES), F32),
                 jax.ShapeDtypeStruct((rows, LANES), F32)]
    consts = (w_out, lng0, lnb0, lng1, lnb1, wrh, wrl, br)
    return pl.pallas_call(
        _epilogue_kernel,
        grid=(rows // tm,),
        in_specs=[row_spec(D_MODEL), row_spec(2 * D_MODEL)] + [const(a) for a in consts],
        out_specs=[row_spec(D_MODEL), row_spec(LANES), row_spec(LANES)],
        out_shape=out_shape,
        compiler_params=pltpu.CompilerParams(
            dimension_semantics=("arbitrary",), vmem_limit_bytes=VMEM_LIMIT_BYTES),
        name="epilogue",
    )(x_all, ycat, *consts)


def _rank_kernel(sel_ref, gate_ref, lstrict_ref, ucum_ref, dest_ref, gk_ref, stats_ref,
                 base_ref, *, trash_row):
    phase = pl.program_id(0)
    i = pl.program_id(1)
    sel = sel_ref[...]
    colsum = jnp.sum(sel, axis=0, keepdims=True)

    @pl.when(jnp.logical_and(phase == 0, i == 0))
    def _():
        base_ref[...] = jnp.zeros_like(base_ref)

    @pl.when(phase == 0)
    def _():
        base_ref[0:1, :] = base_ref[0:1, :] + colsum

    @pl.when(jnp.logical_and(phase == 1, i == 0))
    def _():
        counts = base_ref[0:1, :]
        padded = jnp.ceil(counts / MOE_BLOCK) * MOE_BLOCK
        pend = _dot_exact_rhs(jnp.broadcast_to(padded, (8, LANES)), ucum_ref[...])[0:1, :]
        stats_ref[0:1, :] = counts
        stats_ref[1:2, :] = pend - padded
        stats_ref[2:3, :] = pend
        stats_ref[3:8, :] = jnp.zeros((5, LANES), F32)
        base_ref[1:2, :] = pend - padded

    @pl.when(phase == 1)
    def _():
        before = _dot(lstrict_ref[...], sel.astype(BF16))
        pos = base_ref[1:2, :] + before
        base_ref[1:2, :] = base_ref[1:2, :] + colsum
        work = jnp.where(sel > 0.0, pos + 1.0, 0.0)
        gates = gate_ref[...]
        for kk in range(TOP_K):
            m = jnp.max(work, axis=-1, keepdims=True)
            pick = jnp.logical_and(work == m, m > 0.0)
            gk_ref[:, kk:kk + 1] = jnp.sum(jnp.where(pick, gates, 0.0), axis=-1, keepdims=True)
            dest_ref[:, kk:kk + 1] = jnp.where(m > 0.0, m - 1.0, float(trash_row)).astype(jnp.int32)
            work = jnp.where(pick, 0.0, work)


def _rank(sel, gates, lstrict, ucum, trash_row):
    rows = sel.shape[0]
    tm = RANK_ROWS
    assert rows % tm == 0
    row_spec = lambda n: pl.BlockSpec((tm, n), lambda p, i: (i, 0))
    out_row_spec = lambda n: pl.BlockSpec((tm, n), lambda p, i: (i * p, 0))
    const = lambda a: pl.BlockSpec(a.shape, lambda p, i: (0,) * a.ndim)
    return pl.pallas_call(
        functools.partial(_rank_kernel, trash_row=trash_row),
        grid=(2, rows // tm),
        in_specs=[row_spec(LANES), row_spec(LANES), const(lstrict), const(ucum)],
        out_specs=[out_row_spec(TOP_K), out_row_spec(TOP_K),
                   pl.BlockSpec((8, LANES), lambda p, i: (0, 0))],
        out_shape=[jax.ShapeDtypeStruct((rows, TOP_K), jnp.int32),
                   jax.ShapeDtypeStruct((rows, TOP_K), F32),
                   jax.ShapeDtypeStruct((8, LANES), F32)],
        scratch_shapes=[pltpu.VMEM((8, LANES), F32)],
        compiler_params=pltpu.CompilerParams(
            dimension_semantics=("arbitrary", "arbitrary"), vmem_limit_bytes=VMEM_LIMIT_BYTES),
        name="rank",
    )(sel, gates, lstrict, ucum)


def _dispatch_kernel(cnt_ref, start_ref, pend_ref, dest_ref, h_ref, xb_ref, zrow_ref, sems,
                     *, rows_per_step):
    i = pl.program_id(0)

    def row_copy(t, d):
        return pltpu.make_async_copy(h_ref.at[pl.ds(t, 1), :], xb_ref.at[pl.ds(d, 1), :], sems.at[0])

    def pad_copy(d):
        return pltpu.make_async_copy(zrow_ref.at[pl.ds(0, 1), :], xb_ref.at[pl.ds(d, 1), :], sems.at[1])

    @pl.when(i == 0)
    def _():
        zrow_ref[...] = jnp.zeros_like(zrow_ref)

        def per_expert(e, carry):
            first = start_ref[e] + cnt_ref[e]
            n_pad = pend_ref[e] - first

            def start(r, cc):
                pad_copy(first + r).start()
                return cc

            def wait(r, cc):
                pad_copy(first + r).wait()
                return cc

            lax.fori_loop(0, n_pad, start, 0)
            lax.fori_loop(0, n_pad, wait, 0)
            return carry

        lax.fori_loop(0, N_EXPERTS, per_expert, 0)

    def start_rows(t, carry):
        for kk in range(TOP_K):
            row_copy(t, dest_ref[0, 0, t * TOP_K + kk]).start()
        return carry

    def wait_rows(t, carry):
        for kk in range(TOP_K):
            row_copy(t, dest_ref[0, 0, t * TOP_K + kk]).wait()
        return carry

    lax.fori_loop(0, rows_per_step, start_rows, 0)
    lax.fori_loop(0, rows_per_step, wait_rows, 0)


def _dispatch(counts, starts, pends, dest, h1, n_buf_rows):
    rows = h1.shape[0]
    tm = DISPATCH_ROWS
    assert rows % tm == 0
    dest3 = dest.reshape(rows // tm, 1, tm * TOP_K)
    grid_spec = pltpu.PrefetchScalarGridSpec(
        num_scalar_prefetch=3,
        grid=(rows // tm,),
        in_specs=[pl.BlockSpec((1, 1, tm * TOP_K), lambda i, *_: (i, 0, 0), memory_space=pltpu.SMEM),
                  pl.BlockSpec((tm, D_MODEL), lambda i, *_: (i, 0))],
        out_specs=pl.BlockSpec(memory_space=pl.ANY),
        scratch_shapes=[pltpu.VMEM((8, D_MODEL), F32), pltpu.SemaphoreType.DMA((2,))],
    )
    return pl.pallas_call(
        functools.partial(_dispatch_kernel, rows_per_step=tm),
        grid_spec=grid_spec,
        out_shape=jax.ShapeDtypeStruct((n_buf_rows, D_MODEL), F32),
        compiler_params=pltpu.CompilerParams(
            dimension_semantics=("arbitrary",), vmem_limit_bytes=VMEM_LIMIT_BYTES,
            has_side_effects=True),
        name="dispatch",
    )(counts, starts, pends, dest3, h1)


def _ffn_kernel(blk_ref, exp_ref, nused_ref, xb_ref, w1_ref, b1_ref, w2_ref, b2_ref, yb_ref):
    j = pl.program_id(0)

    @pl.when(j < nused_ref[0])
    def _():
        x = xb_ref[...].astype(BF16)
        hc = _dot(x, w1_ref[0]) + b1_ref[0]
        gate = jnp.minimum(hc[:, :D_FF], SWIGLU_LIMIT)
        up = jnp.clip(hc[:, D_FF:], -SWIGLU_LIMIT, SWIGLU_LIMIT)
        act = (up + 1.0) * gate * _sigmoid(SWIGLU_ALPHA * gate)
        yb_ref[...] = _dot(act.astype(BF16), w2_ref[0]) + b2_ref[0]


def _ffn(blk_idx, blk_exp, n_used, xb, w1, b1, w2, b2, n_blocks):
    bm = MOE_BLOCK
    grid_spec = pltpu.PrefetchScalarGridSpec(
        num_scalar_prefetch=3,
        grid=(n_blocks,),
        in_specs=[pl.BlockSpec((bm, D_MODEL), lambda j, bi, be, nu: (bi[j], 0)),
                  pl.BlockSpec((1, D_MODEL, 2 * D_FF), lambda j, bi, be, nu: (be[j], 0, 0)),
                  pl.BlockSpec((1, 1, 2 * D_FF), lambda j, bi, be, nu: (be[j], 0, 0)),
                  pl.BlockSpec((1, D_FF, D_MODEL), lambda j, bi, be, nu: (be[j], 0, 0)),
                  pl.BlockSpec((1, 1, D_MODEL), lambda j, bi, be, nu: (be[j], 0, 0))],
        out_specs=pl.BlockSpec((bm, D_MODEL), lambda j, bi, be, nu: (bi[j], 0)),
    )
    return pl.pallas_call(
        _ffn_kernel,
        grid_spec=grid_spec,
        out_shape=jax.ShapeDtypeStruct(xb.shape, F32),
        compiler_params=pltpu.CompilerParams(
            dimension_semantics=("arbitrary",), vmem_limit_bytes=VMEM_LIMIT_BYTES),
        name="expert_ffn",
    )(blk_idx, blk_exp, n_used, xb, w1, b1, w2, b2)


def _combine_kernel(dest_ref, gk_ref, h1_ref, yb_ref, lng_ref, lnb_ref, outa_ref, outb_ref, buf_ref, sem,
                    *, rows_per_step, n_tiles_a):
    def row_copy(t, kk):
        d = dest_ref[0, 0, t * TOP_K + kk]
        return pltpu.make_async_copy(yb_ref.at[pl.ds(d, 1), :], buf_ref.at[kk, pl.ds(t, 1), :], sem.at[0])

    def start_rows(t, carry):
        for kk in range(TOP_K):
            row_copy(t, kk).start()
        return carry

    def wait_rows(t, carry):
        for kk in range(TOP_K):
            row_copy(t, kk).wait()
        return carry

    lax.fori_loop(0, rows_per_step, start_rows, 0)
    lax.fori_loop(0, rows_per_step, wait_rows, 0)
    gk = gk_ref[...]
    ffn = gk[:, 0:1] * buf_ref[0]
    for kk in range(1, TOP_K):
        ffn = ffn + gk[:, kk:kk + 1] * buf_ref[kk]
    out = _layer_norm(DEEPNORM_ALPHA * h1_ref[...] + ffn, lng_ref[...], lnb_ref[...])
    i = pl.program_id(0)

    @pl.when(i < n_tiles_a)
    def _():
        outa_ref[...] = out

    @pl.when(i >= n_tiles_a)
    def _():
        outb_ref[...] = out


def _combine(dest, gk, h1, yb, lng, lnb, rows_a):
    rows = h1.shape[0]
    tm = COMBINE_ROWS
    assert rows % tm == 0 and rows_a % tm == 0 and 0 < rows_a < rows
    n_tiles_a = rows_a // tm
    dest3 = dest.reshape(rows // tm, 1, tm * TOP_K)
    const = lambda a: pl.BlockSpec(a.shape, lambda i: (0,) * a.ndim)
    return pl.pallas_call(
        functools.partial(_combine_kernel, rows_per_step=tm, n_tiles_a=n_tiles_a),
        grid=(rows // tm,),
        in_specs=[pl.BlockSpec((1, 1, tm * TOP_K), lambda i: (i, 0, 0), memory_space=pltpu.SMEM),
                  pl.BlockSpec((tm, TOP_K), lambda i: (i, 0)),
                  pl.BlockSpec((tm, D_MODEL), lambda i: (i, 0)),
                  pl.BlockSpec(memory_space=pl.ANY),
                  const(lng), const(lnb)],
        out_specs=[pl.BlockSpec((tm, D_MODEL), lambda i: (jnp.minimum(i, n_tiles_a - 1), 0)),
                   pl.BlockSpec((tm, D_MODEL), lambda i: (jnp.maximum(i - n_tiles_a, 0), 0))],
        out_shape=[jax.ShapeDtypeStruct((rows_a, D_MODEL), F32),
                   jax.ShapeDtypeStruct((rows - rows_a, D_MODEL), F32)],
        scratch_shapes=[pltpu.VMEM((TOP_K, tm, D_MODEL), F32), pltpu.SemaphoreType.DMA((1,))],
        compiler_params=pltpu.CompilerParams(
            dimension_semantics=("arbitrary",), vmem_limit_bytes=VMEM_LIMIT_BYTES),
        name="combine",
    )(dest3, gk, h1, yb, lng, lnb)


def _row(v, width=None):
    v = v.reshape(1, -1).astype(F32)
    if width is not None and v.shape[1] < width:
        v = jnp.pad(v, ((0, 0), (0, width - v.shape[1])))
    return v


def _encode_all(x_all, rows_a, meta_tokens, ln_emb_g, ln_emb_b, w_in, conv_w, conv_b, dt_bias, a_log,
                d_skip, ssd_norm_g, i_bias, f_bias, mlstm_norm_g, w_out, ln1_g, ln1_b, w_router, b_router,
                w1, b1, w2, b2, ln2_g, ln2_b):
    n_seq, seq_len, _ = x_all.shape
    assert seq_len % CHUNK == 0
    n_chunks = seq_len // CHUNK + 1
    n_tok = n_seq * seq_len

    sizes = (1024, CONV_CH, 2 * SSD_HEADS, 512, 512, 1024, 1024, 2 * MLSTM_HEADS, 2 * MLSTM_HEADS)
    offs = [0]
    for s in sizes:
        offs.append(offs[-1] + s)
    w_z, w_xbc, w_dt, w_q, w_k, w_v, w_o, w_i, w_f = [w_in[:, offs[j]:offs[j + 1]] for j in range(9)]
    w_big = jnp.concatenate([w_z, w_xbc, w_q, w_k * (MLSTM_DK ** -0.5), w_v, w_o], axis=1).astype(BF16)
    zpad = jnp.zeros((D_MODEL, LANES - GATE_END), F32)
    gate_cols = []
    for d in range(2):
        gate_cols += [w_dt[:, d * SSD_HEADS:(d + 1) * SSD_HEADS],
                      w_i[:, d * MLSTM_HEADS:(d + 1) * MLSTM_HEADS],
                      w_f[:, d * MLSTM_HEADS:(d + 1) * MLSTM_HEADS], zpad]
    w_gates = jnp.concatenate(gate_cols, axis=1).astype(BF16)
    gbias = [_row(jnp.concatenate([dt_bias[d], i_bias[d], f_bias[d]]), LANES) for d in range(2)]
    alog = [_row(a_log[d], LANES) for d in range(2)]
    head_of_col = jnp.arange(D_MODEL, dtype=jnp.int32) // SSD_HEAD_DIM
    expand = (jnp.arange(LANES, dtype=jnp.int32)[:, None] == head_of_col[None, :]).astype(BF16)
    dskip = _row(jnp.repeat(d_skip, SSD_HEAD_DIM))
    conv_w8 = jnp.pad(conv_w.astype(F32), ((0, 8 - CONV_W), (0, 0)))
    meta_tile = jnp.pad(meta_tokens.astype(F32), ((PAD_FRONT, INPROJ_ROWS - CHUNK), (0, 0)))

    z, xbc, q, k, v, o, gates_f, gates_b = _inproj(x_all, meta_tile, _row(ln_emb_g), _row(ln_emb_b),
                                                   w_big, w_gates)
    xs, bm, cm = _conv(xbc, conv_w8, _row(conv_b), n_chunks)
    yb, hb = _mixer_pass(xs, bm, cm, gates_b, q, k, v, gbias[1], alog[1], expand, n_chunks, reverse=True)
    (ycat,) = _mixer_pass(xs, bm, cm, gates_f, q, k, v, gbias[0], alog[0], expand, n_chunks,
                          reverse=False,
                          final_inputs=(z, o, yb, hb, dskip, _row(ssd_norm_g), _row(mlstm_norm_g)))

    wr = jnp.pad(w_router.astype(F32), ((0, 0), (0, LANES - N_EXPERTS)))
    wrh = wr.astype(BF16)
    wrl = (wr - wrh.astype(F32)).astype(BF16)
    h1, sel, gates = _epilogue(x_all.reshape(n_tok, D_MODEL), ycat.reshape(n_tok, 2 * D_MODEL),
                               w_out.astype(BF16), _row(ln_emb_g), _row(ln_emb_b), _row(ln1_g),
                               _row(ln1_b), wrh, wrl, _row(b_router, LANES))

    n_blocks = n_tok * TOP_K // MOE_BLOCK + N_EXPERTS
    trash_row = n_blocks * MOE_BLOCK
    n_buf_rows = trash_row + MOE_BLOCK
    r_i = jnp.arange(RANK_ROWS, dtype=jnp.int32)
    lstrict = (r_i[None, :] < r_i[:, None]).astype(BF16)
    l_i = jnp.arange(LANES, dtype=jnp.int32)
    ucum = (l_i[:, None] <= l_i[None, :]).astype(BF16)
    dest, gk, stats = _rank(sel, gates, lstrict, ucum, trash_row)
    counts = stats[0, :N_EXPERTS].astype(jnp.int32)
    starts = stats[1, :N_EXPERTS].astype(jnp.int32)
    pends = stats[2, :N_EXPERTS].astype(jnp.int32)
    n_used = pends[N_EXPERTS - 1] // MOE_BLOCK
    blk = jnp.minimum(jnp.arange(n_blocks, dtype=jnp.int32), jnp.maximum(n_used - 1, 0))
    blk_exp = jnp.minimum(jnp.sum((pends[None, :] <= (blk * MOE_BLOCK)[:, None]).astype(jnp.int32), axis=1),
                          N_EXPERTS - 1).astype(jnp.int32)

    xb = _dispatch(counts, starts, pends, dest, h1, n_buf_rows)
    yexp = _ffn(blk, blk_exp, n_used.reshape(1), xb, w1.astype(BF16), b1.reshape(N_EXPERTS, 1, -1),
                w2.astype(BF16), b2.reshape(N_EXPERTS, 1, -1), n_blocks)
    return _combine(dest, gk, h1, yexp, _row(ln2_g), _row(ln2_b), rows_a)


def kernel(x_prompt, x_sample, meta_tokens, ln_emb_g, ln_emb_b, w_in, conv_w, conv_b, dt_bias, a_log,
           d_skip, ssd_norm_g, i_bias, f_bias, mlstm_norm_g, w_out, ln1_g, ln1_b, w_router, b_router,
           w1, b1, w2, b2, ln2_g, ln2_b):
    assert x_prompt.shape[1:] == x_sample.shape[1:]
    n_p, seq_len, d = x_prompt.shape
    n_s = x_sample.shape[0]
    x_all = jnp.concatenate([x_prompt, x_sample], axis=0).astype(F32)
    y_p, y_s = _encode_all(x_all, n_p * seq_len, meta_tokens, ln_emb_g, ln_emb_b, w_in[0], conv_w[0],
                           conv_b[0], dt_bias[0], a_log[0], d_skip[0], ssd_norm_g[0], i_bias[0], f_bias[0],
                           mlstm_norm_g[0], w_out[0], ln1_g[0], ln1_b[0], w_router[0], b_router[0],
                           w1[0], b1[0], w2[0], b2[0], ln2_g[0], ln2_b[0])
    return (y_p.reshape(n_p, seq_len, d), y_s.reshape(n_s, seq_len, d))
```

```python
import functools

import jax
import jax.numpy as jnp
from jax import lax
from jax.experimental import pallas as pl
from jax.experimental.pallas import tpu as pltpu
from jax.experimental.pallas import tpu_sc as plsc

F32 = jnp.float32
BF16 = jnp.bfloat16

D_MODEL = 1024
N_META = 16
CHUNK = 128
PAD_FRONT = CHUNK - N_META
SSD_HEADS = 16
SSD_HEAD_DIM = 64
SSD_GROUPS = 4
SSD_STATE = 128
HEADS_PER_GROUP = SSD_HEADS // SSD_GROUPS
GROUP_WIDTH = HEADS_PER_GROUP * SSD_HEAD_DIM
CONV_W = 5
CONV_HALF = CONV_W // 2
CONV_CH = D_MODEL + 2 * SSD_GROUPS * SSD_STATE
MLSTM_HEADS = 8
MLSTM_DK = 64
MLSTM_DV = 128
N_EXPERTS = 32
TOP_K = 4
D_FF = D_MODEL
SWIGLU_LIMIT = 7.0
SWIGLU_ALPHA = 1.702
DEEPNORM_ALPHA = 2.0 ** 0.25
LN_EPS = 1e-5
RMS_EPS = 1e-5
NEG_GATE = -1e30

LANES = 128
BF16_SUBLANES = 16
VMEM_LIMIT_BYTES = 56 * 1024 * 1024

GATE_DT0, GATE_I0, GATE_F0, GATE_END = 0, SSD_HEADS, SSD_HEADS + MLSTM_HEADS, SSD_HEADS + 2 * MLSTM_HEADS

INPROJ_ROWS = 512
EPILOGUE_ROWS = 512
RANK_ROWS = 512
COMBINE_ROWS = 512
MOE_BLOCK = 256

SC_CORES = 2
SC_SUBCORES = 16
SC_WORKERS = SC_CORES * SC_SUBCORES
SC_WINDOW = 32


def _dot(a, b):
    return jnp.dot(a, b, preferred_element_type=F32)


def _dot_nt(a, b):
    return lax.dot_general(a, b, (((1,), (1,)), ((), ())), preferred_element_type=F32)


def _split3(x):
    hi = x.astype(BF16)
    r1 = x - hi.astype(F32)
    mid = r1.astype(BF16)
    lo = (r1 - mid.astype(F32)).astype(BF16)
    return hi, mid, lo


def _dot_exact_lhs(a_bf16, x):
    hi, mid, lo = _split3(x)
    return _dot(a_bf16, hi) + _dot(a_bf16, mid) + _dot(a_bf16, lo)


def _dot_exact_rhs(x, b_bf16):
    hi, mid, lo = _split3(x)
    return _dot(hi, b_bf16) + _dot(mid, b_bf16) + _dot(lo, b_bf16)


def _layer_norm(x, g, b):
    mu = jnp.mean(x, axis=-1, keepdims=True)
    xc = x - mu
    var = jnp.mean(xc * xc, axis=-1, keepdims=True)
    return xc * lax.rsqrt(var + LN_EPS) * g + b


def _sigmoid(x):
    return 1.0 / (1.0 + jnp.exp(-x))


def _log1p_exp_neg_abs(x):
    return jnp.log(1.0 + jnp.exp(-jnp.abs(x)))


def _storage_chunk(c, n_chunks):
    return jnp.where(c == 0, n_chunks - 1, c - 1)


def _inproj_kernel(x_ref, meta_ref, g_ref, b_ref, wbig_ref, wg_ref,
                   z_ref, xbc_ref, q_ref, k_ref, v_ref, o_ref, gf_ref, gb_ref, *, n_x_tiles):
    x = jnp.where(pl.program_id(1) == n_x_tiles, meta_ref[...], x_ref[...])
    h = _layer_norm(x, g_ref[...], b_ref[...]).astype(BF16)

    def mm(c0, c1):
        return _dot(h, wbig_ref[:, c0:c1]).astype(BF16)

    z_ref[...] = mm(0, 1024)
    xbc_ref[:, 0:1024] = mm(1024, 2048)
    xbc_ref[:, 1024:2048] = mm(2048, 3072)
    q_ref[...] = mm(3072, 3584)
    k_ref[...] = mm(3584, 4096)
    v_ref[...] = mm(4096, 5120)
    o_ref[...] = mm(5120, 6144)
    gates = _dot(h, wg_ref[...])
    gf_ref[...] = gates[:, :LANES]
    gb_ref[...] = gates[:, LANES:]


def _inproj(x_all, meta_tile, ln_g, ln_b, w_big, w_gates):
    n_seq, seq_len, _ = x_all.shape
    tm = INPROJ_ROWS
    assert seq_len % tm == 0 and tm >= CHUNK
    n_x_tiles = seq_len // tm
    rows = seq_len + CHUNK
    row_spec = lambda n: pl.BlockSpec((None, tm, n), lambda b, i: (b, i, 0))
    const = lambda a: pl.BlockSpec(a.shape, lambda b, i: (0,) * a.ndim)
    resident = lambda a: pl.BlockSpec(a.shape, lambda b, i: (0,) * a.ndim, pipeline_mode=pl.Buffered(1))
    widths = (1024, CONV_CH, 512, 512, 1024, 1024)
    out_shapes = [jax.ShapeDtypeStruct((n_seq, rows, w), BF16) for w in widths]
    out_shapes += [jax.ShapeDtypeStruct((n_seq, rows, LANES), F32)] * 2
    return pl.pallas_call(
        functools.partial(_inproj_kernel, n_x_tiles=n_x_tiles),
        grid=(n_seq, n_x_tiles + 1),
        in_specs=[pl.BlockSpec((None, tm, D_MODEL), lambda b, i: (b, jnp.minimum(i, n_x_tiles - 1), 0)),
                  resident(meta_tile), const(ln_g), const(ln_b), resident(w_big), resident(w_gates)],
        out_specs=[row_spec(s.shape[2]) for s in out_shapes],
        out_shape=out_shapes,
        compiler_params=pltpu.CompilerParams(
            dimension_semantics=("arbitrary", "arbitrary"), vmem_limit_bytes=VMEM_LIMIT_BYTES),
        name="inproj",
    )(x_all, meta_tile, ln_g, ln_b, w_big, w_gates)


def _conv_kernel(prev_ref, main_ref, next_ref, w_ref, b_ref, xs_ref, bm_ref, cm_ref, *, n_chunks):
    c = pl.program_id(1)
    row = lax.broadcasted_iota(jnp.int32, (CHUNK, 1), 0)
    pad_rows = jnp.logical_and(c == 0, row < PAD_FRONT)
    main = jnp.where(pad_rows, 0.0, main_ref[...].astype(F32))
    prev = jnp.where(c == 0, 0.0, prev_ref[...].astype(F32))[BF16_SUBLANES - 8:, :]
    nxt = jnp.where(c == n_chunks - 1, 0.0, next_ref[...].astype(F32))[:8, :]
    xp = jnp.concatenate([prev, main, nxt], axis=0)
    w = w_ref[...]
    acc = b_ref[...] + xp[8 - CONV_HALF:8 - CONV_HALF + CHUNK, :] * w[0:1, :]
    for t in range(1, CONV_W):
        off = 8 - CONV_HALF + t
        acc = acc + xp[off:off + CHUNK, :] * w[t:t + 1, :]
    y = acc * _sigmoid(acc)
    y = jnp.where(pad_rows, 0.0, y)
    xs_ref[...] = y[:, :D_MODEL].astype(BF16)
    bm_ref[...] = y[:, D_MODEL:D_MODEL + 512].astype(BF16)
    cm_ref[...] = y[:, D_MODEL + 512:].astype(BF16)


def _conv(xbc, conv_w8, conv_b, n_chunks):
    n_seq, rows, _ = xbc.shape
    halo_blocks_per_chunk = CHUNK // BF16_SUBLANES

    def main_map(b, c):
        return (b, _storage_chunk(c, n_chunks), 0)

    def prev_map(b, c):
        sc = _storage_chunk(jnp.maximum(c - 1, 0), n_chunks)
        return (b, sc * halo_blocks_per_chunk + halo_blocks_per_chunk - 1, 0)

    def next_map(b, c):
        sc = _storage_chunk(jnp.minimum(c + 1, n_chunks - 1), n_chunks)
        return (b, sc * halo_blocks_per_chunk, 0)

    const = lambda a: pl.BlockSpec(a.shape, lambda b, c: (0,) * a.ndim)
    out_shapes = [jax.ShapeDtypeStruct((n_seq, rows, D_MODEL), BF16),
                  jax.ShapeDtypeStruct((n_seq, rows, 512), BF16),
                  jax.ShapeDtypeStruct((n_seq, rows, 512), BF16)]
    return pl.pallas_call(
        functools.partial(_conv_kernel, n_chunks=n_chunks),
        grid=(n_seq, n_chunks),
        in_specs=[pl.BlockSpec((None, BF16_SUBLANES, CONV_CH), prev_map),
                  pl.BlockSpec((None, CHUNK, CONV_CH), main_map),
                  pl.BlockSpec((None, BF16_SUBLANES, CONV_CH), next_map),
                  const(conv_w8), const(conv_b)],
        out_specs=[pl.BlockSpec((None, CHUNK, s.shape[2]), main_map) for s in out_shapes],
        out_shape=out_shapes,
        compiler_params=pltpu.CompilerParams(
            dimension_semantics=("arbitrary", "arbitrary"), vmem_limit_bytes=VMEM_LIMIT_BYTES),
        name="conv",
    )(xbc, xbc, xbc, conv_w8, conv_b)


def _mixer_kernel(*refs, reverse, final, n_chunks, n_seq):
    if final:
        (xs_ref, bm_ref, cm_ref, g_ref, q_ref, k_ref, v_ref, z_ref, o_ref, yb_ref, hb_ref,
         gbias_ref, alog_ref, expand_ref, dskip_ref, ngs_ref, ngm_ref,
         ycat_ref, s_ref, cst_ref, m_ref) = refs
    else:
        (xs_ref, bm_ref, cm_ref, g_ref, q_ref, k_ref, v_ref,
         gbias_ref, alog_ref, expand_ref,
         yout_ref, hout_ref, s_ref, cst_ref, m_ref) = refs

    t = pl.program_id(0)
    c = (n_chunks - 1 - t) if reverse else t
    end = 0 if reverse else CHUNK - 1

    @pl.when(t == 0)
    def _():
        s_ref[...] = jnp.zeros_like(s_ref)
        cst_ref[...] = jnp.zeros_like(cst_ref)
        m_ref[...] = jnp.zeros_like(m_ref)

    row = lax.broadcasted_iota(jnp.int32, (CHUNK, 1), 0)
    col = lax.broadcasted_iota(jnp.int32, (1, CHUNK), 1)
    lane = col
    allowed = (col >= row) if reverse else (col <= row)
    tri = allowed.astype(BF16)
    tri_t = ((row >= col) if reverse else (row <= col)).astype(BF16)
    is_dt = lane < GATE_I0
    is_i = jnp.logical_and(lane >= GATE_I0, lane < GATE_F0)
    is_f = jnp.logical_and(lane >= GATE_F0, lane < GATE_END)
    pad_rows = jnp.logical_and(c == 0, row < PAD_FRONT)
    a_coef = -jnp.exp(alog_ref[...])
    expand = expand_ref[...]
    left_half = lane < SSD_HEAD_DIM
    top_half = row < MLSTM_DK
    ones_blk = jnp.ones((CHUNK, MLSTM_DV), BF16)

    def one_sequence(b):
        gr = g_ref[b] + gbias_ref[...]
        lse = _log1p_exp_neg_abs(gr)
        softplus = jnp.maximum(gr, 0.0) + lse
        logsig = jnp.minimum(gr, 0.0) - lse
        val = jnp.where(is_dt, softplus, jnp.where(is_i, gr, jnp.where(is_f, logsig, 0.0)))
        val = jnp.where(pad_rows, jnp.where(is_i, NEG_GATE, 0.0), val)
        u = jnp.where(is_dt, val * a_coef, jnp.where(is_f, val, 0.0))
        val_t = val.T
        u_t = u.T
        cums = _dot_exact_lhs(tri, u)
        cums_t = _dot_exact_rhs(u_t, tri_t)

        cums_end = cums[end:end + 1, :]
        p1 = jnp.exp(cums)
        p2 = jnp.exp(cums_end - cums) * val
        ex1 = _dot(p1.astype(BF16), expand)
        ex2 = _dot(p2.astype(BF16), expand)
        chunk_decay = _dot_exact_rhs(jnp.broadcast_to(jnp.exp(cums_end), (8, LANES)), expand)[0:1, :]
        xs = xs_ref[b]
        xsf = xs.astype(F32)
        xs_w = (xsf * ex2).astype(BF16)
        y_groups = []
        for g in range(SSD_GROUPS):
            cg = cm_ref[b, :, g * SSD_STATE:(g + 1) * SSD_STATE]
            bg = bm_ref[b, :, g * SSD_STATE:(g + 1) * SSD_STATE]
            cb = _dot_nt(cg, bg)
            ys = []
            for pp in range(HEADS_PER_GROUP // 2):
                pair = g * (HEADS_PER_GROUP // 2) + pp
                xs_pair = xs[:, pair * LANES:(pair + 1) * LANES]
                y_pair = None
                for hh in range(2):
                    h = 2 * pair + hh
                    seg = cums[:, h:h + 1] - cums_t[h:h + 1, :]
                    dec = jnp.exp(jnp.where(allowed, seg, -jnp.inf))
                    m_mat = (cb * dec * val_t[h:h + 1, :]).astype(BF16)
                    keep = left_half if hh == 0 else jnp.logical_not(left_half)
                    part = _dot(m_mat, jnp.where(keep, xs_pair, jnp.zeros_like(xs_pair)))
                    y_pair = part if y_pair is None else y_pair + part
                ys.append(y_pair)
            y_diag = jnp.concatenate(ys, axis=1)
            gs = slice(g * GROUP_WIDTH, (g + 1) * GROUP_WIDTH)
            s_g = s_old[b][g]
            y_off = _dot(cg, s_g.astype(BF16)) * ex1[:, gs]
            y_groups.append(y_diag + y_off)
            bg_t = bg.astype(F32).T.astype(BF16)
            s_new_all[b].append(chunk_decay[:, gs] * s_g + _dot(bg_t, xs_w[:, gs]))
        y_ssd = jnp.concatenate(y_groups, axis=1)

        bcum_t = cums_t[GATE_F0:GATE_END, :]
        ip_t = val_t[GATE_I0:GATE_F0, :]
        g_t = bcum_t[:, end:end + 1]
        a_t = g_t - bcum_t + ip_t
        a_max = jnp.max(a_t, axis=1, keepdims=True)
        w_t = jnp.exp(a_t - a_max)
        m_prev = m_old[b][:, 0:1]
        m_new = jnp.maximum(g_t + m_prev, a_max)
        s_prev = jnp.exp(g_t + m_prev - m_new)
        s_new = jnp.exp(a_max - m_new)
        r_t = ip_t - bcum_t
        h_heads = []
        for pair in range(MLSTM_HEADS // 2):
            h0, h1 = 2 * pair, 2 * pair + 1
            q_pair = q_ref[b, :, pair * LANES:(pair + 1) * LANES]
            k_pair = k_ref[b, :, pair * LANES:(pair + 1) * LANES]
            cst = cst_old[b][pair]
            cst_b = cst.astype(BF16)
            v_pair = []
            for hh, h in enumerate((h0, h1)):
                keep = left_half if hh == 0 else jnp.logical_not(left_half)
                qh = jnp.where(keep, q_pair, jnp.zeros_like(q_pair))
                vh = v_ref[b, :, h * MLSTM_DV:(h + 1) * MLSTM_DV]
                v_pair.append(vh)
                qk = _dot_nt(qh, k_pair)
                bc = cums[:, GATE_F0 + h:GATE_F0 + h + 1]
                dlog = jnp.where(allowed, bc + r_t[h:h + 1, :], -jnp.inf)
                m_intra = jnp.max(dlog, axis=1, keepdims=True)
                m_inter = bc + m_prev[h:h + 1, :]
                m_t = jnp.maximum(m_inter, m_intra)
                s_mat = (qk * jnp.exp(dlog - m_t)).astype(BF16)
                intra = _dot(s_mat, jnp.concatenate([vh, ones_blk], axis=1))
                inter = _dot(qh, cst_b)
                tot = jnp.exp(m_inter - m_t) * inter + intra
                num = tot[:, :MLSTM_DV]
                den = tot[:, MLSTM_DV:]
                h_heads.append(num / jnp.maximum(jnp.abs(den), jnp.exp(-m_t)))
            w_rows = jnp.where(top_half, w_t[h0:h0 + 1, :], w_t[h1:h1 + 1, :])
            kw = (k_pair.astype(F32).T * w_rows).astype(BF16)
            full = _dot(kw, jnp.concatenate([v_pair[0], v_pair[1], ones_blk], axis=1))
            kvn = jnp.concatenate(
                [jnp.where(top_half, full[:, :MLSTM_DV], full[:, MLSTM_DV:2 * MLSTM_DV]),
                 full[:, 2 * MLSTM_DV:]], axis=1)
            sp_rows = jnp.where(top_half, s_prev[h0:h0 + 1, :], s_prev[h1:h1 + 1, :])
            sn_rows = jnp.where(top_half, s_new[h0:h0 + 1, :], s_new[h1:h1 + 1, :])
            cst_new_all[b].append(sp_rows * cst + sn_rows * kvn)
        m_new_all[b] = jnp.broadcast_to(m_new, (MLSTM_HEADS, LANES))
        h_ml = jnp.concatenate(h_heads, axis=1)
        if not final:
            return y_ssd.astype(BF16), h_ml.astype(BF16)

        y_tot = y_ssd + yb_ref[b].astype(F32) + dskip_ref[...] * xsf
        zz = z_ref[b].astype(F32)
        y2 = y_tot * (zz * _sigmoid(zz))
        y_n = y2 * lax.rsqrt(jnp.mean(y2 * y2, axis=-1, keepdims=True) + RMS_EPS) * ngs_ref[...]
        h_tot = h_ml + hb_ref[b].astype(F32)
        segs = []
        for h in range(MLSTM_HEADS):
            seg = h_tot[:, h * MLSTM_DV:(h + 1) * MLSTM_DV]
            segs.append(seg * lax.rsqrt(jnp.mean(seg * seg, axis=-1, keepdims=True) + RMS_EPS))
        h_n = jnp.concatenate(segs, axis=1) * ngm_ref[...]
        y_ml = _sigmoid(o_ref[b].astype(F32)) * h_n
        return y_n.astype(BF16), y_ml.astype(BF16)

    s_old = [[s_ref[b, g] for g in range(SSD_GROUPS)] for b in range(n_seq)]
    cst_old = [[cst_ref[b, p] for p in range(MLSTM_HEADS // 2)] for b in range(n_seq)]
    m_old = [m_ref[b] for b in range(n_seq)]
    s_new_all = [[] for _ in range(n_seq)]
    cst_new_all = [[] for _ in range(n_seq)]
    m_new_all = [None] * n_seq
    results = [one_sequence(b) for b in range(n_seq)]
    for b in range(n_seq):
        for g in range(SSD_GROUPS):
            s_ref[b, g] = s_new_all[b][g]
        for p in range(MLSTM_HEADS // 2):
            cst_ref[b, p] = cst_new_all[b][p]
        m_ref[b] = m_new_all[b]

    @pl.when(c > 0)
    def _():
        for b, (first, second) in enumerate(results):
            if final:
                ycat_ref[b, :, :D_MODEL] = first
                ycat_ref[b, :, D_MODEL:] = second
            else:
                yout_ref[b] = first
                hout_ref[b] = second


def _mixer_pass(xs, bm, cm, gates, q, k, v, gbias, alog, expand, n_chunks, *, reverse, final_inputs=None):
    final = final_inputs is not None
    n_seq = xs.shape[0]
    seq_len = (n_chunks - 1) * CHUNK

    def chunk_of(t):
        return (n_chunks - 1 - t) if reverse else t

    def pad_map(t):
        return (0, _storage_chunk(chunk_of(t), n_chunks), 0)

    def out_map(t):
        return (0, jnp.maximum(chunk_of(t) - 1, 0), 0)

    const = lambda a: pl.BlockSpec(a.shape, lambda t: (0,) * a.ndim)
    pad_spec = lambda n: pl.BlockSpec((n_seq, CHUNK, n), pad_map)
    out_spec = lambda n: pl.BlockSpec((n_seq, CHUNK, n), out_map)
    in_arrays = [xs, bm, cm, gates, q, k, v]
    in_specs = [pad_spec(a.shape[2]) for a in in_arrays]
    if final:
        z, o, yb, hb, dskip, ngs, ngm = final_inputs
        in_arrays += [z, o, yb, hb]
        in_specs += [pad_spec(1024), pad_spec(1024), out_spec(1024), out_spec(1024)]
        in_arrays += [gbias, alog, expand, dskip, ngs, ngm]
        in_specs += [const(a) for a in (gbias, alog, expand, dskip, ngs, ngm)]
        out_shape = [jax.ShapeDtypeStruct((n_seq, seq_len, 2 * D_MODEL), BF16)]
        out_specs = [out_spec(2 * D_MODEL)]
    else:
        in_arrays += [gbias, alog, expand]
        in_specs += [const(a) for a in (gbias, alog, expand)]
        out_shape = [jax.ShapeDtypeStruct((n_seq, seq_len, D_MODEL), BF16),
                     jax.ShapeDtypeStruct((n_seq, seq_len, D_MODEL), BF16)]
        out_specs = [out_spec(D_MODEL), out_spec(D_MODEL)]
    return pl.pallas_call(
        functools.partial(_mixer_kernel, reverse=reverse, final=final, n_chunks=n_chunks, n_seq=n_seq),
        grid=(n_chunks,),
        in_specs=in_specs,
        out_specs=out_specs,
        out_shape=out_shape,
        scratch_shapes=[pltpu.VMEM((n_seq, SSD_GROUPS, SSD_STATE, GROUP_WIDTH), F32),
                        pltpu.VMEM((n_seq, MLSTM_HEADS // 2, 2 * MLSTM_DK, 2 * MLSTM_DV), F32),
                        pltpu.VMEM((n_seq, MLSTM_HEADS, LANES), F32)],
        compiler_params=pltpu.CompilerParams(
            dimension_semantics=("arbitrary",), vmem_limit_bytes=VMEM_LIMIT_BYTES),
        name="mixer_fwd" if final else "mixer_bwd",
    )(*in_arrays)


def _epilogue_kernel(x_ref, ycat_ref, wout_ref, lng0_ref, lnb0_ref, lng1_ref, lnb1_ref,
                     wrh_ref, wrl_ref, br_ref, h1_ref, sel_ref, gate_ref):
    h0 = _layer_norm(x_ref[...], lng0_ref[...], lnb0_ref[...])
    mix = _dot(ycat_ref[...], wout_ref[...])
    h1 = _layer_norm(DEEPNORM_ALPHA * h0 + mix, lng1_ref[...], lnb1_ref[...])
    h1_ref[...] = h1
    hh = h1.astype(BF16)
    hl = (h1 - hh.astype(F32)).astype(BF16)
    wrh = wrh_ref[...]
    logits = _dot(hh, wrh) + _dot(hl, wrh) + _dot(hh, wrl_ref[...]) + br_ref[...]
    lane = lax.broadcasted_iota(jnp.int32, (1, LANES), 1)
    lane_f = lane.astype(F32)
    logits = jnp.where(lane < N_EXPERTS, logits, -jnp.inf)
    work = logits
    sel = jnp.zeros(logits.shape, jnp.bool_)
    top = None
    for _ in range(TOP_K):
        m = jnp.max(work, axis=-1, keepdims=True)
        if top is None:
            top = m
        first = jnp.min(jnp.where(work == m, lane_f, float(LANES)), axis=-1, keepdims=True)
        pick = lane_f == first
        sel = jnp.logical_or(sel, pick)
        work = jnp.where(pick, -jnp.inf, work)
    e = jnp.where(sel, jnp.exp(logits - top), 0.0)
    gate_ref[...] = e / jnp.sum(e, axis=-1, keepdims=True)
    sel_ref[...] = sel.astype(F32)


def _epilogue(x_all, ycat, w_out, lng0, lnb0, lng1, lnb1, wrh, wrl, br):
    rows = x_all.shape[0]
    tm = EPILOGUE_ROWS
    assert rows % tm == 0
    row_spec = lambda n: pl.BlockSpec((tm, n), lambda i: (i, 0))
    const = lambda a: pl.BlockSpec(a.shape, lambda i: (0,) * a.ndim)
    out_shape = [jax.ShapeDtypeStruct((rows, D_MODEL), F32),
                 jax.ShapeDtypeStruct((rows, LANES), F32),
                 jax.ShapeDtypeStruct((rows, LANES), F32)]
    consts = (w_out, lng0, lnb0, lng1, lnb1, wrh, wrl, br)
    return pl.pallas_call(
        _epilogue_kernel,
        grid=(rows // tm,),
        in_specs=[row_spec(D_MODEL), row_spec(2 * D_MODEL)] + [const(a) for a in consts],
        out_specs=[row_spec(D_MODEL), row_spec(LANES), row_spec(LANES)],
        out_shape=out_shape,
        compiler_params=pltpu.CompilerParams(
            dimension_semantics=("arbitrary",), vmem_limit_bytes=VMEM_LIMIT_BYTES),
        name="epilogue",
    )(x_all, ycat, *consts)


def _rank_kernel(sel_ref, gate_ref, lstrict_ref, ucum_ref, dest_ref, gk_ref, stats_ref,
                 base_ref, *, trash_row):
    phase = pl.program_id(0)
    i = pl.program_id(1)
    sel = sel_ref[...]
    colsum = jnp.sum(sel, axis=0, keepdims=True)

    @pl.when(jnp.logical_and(phase == 0, i == 0))
    def _():
        base_ref[...] = jnp.zeros_like(base_ref)

    @pl.when(phase == 0)
    def _():
        base_ref[0:1, :] = base_ref[0:1, :] + colsum

    @pl.when(jnp.logical_and(phase == 1, i == 0))
    def _():
        counts = base_ref[0:1, :]
        padded = jnp.ceil(counts / MOE_BLOCK) * MOE_BLOCK
        pend = _dot_exact_rhs(jnp.broadcast_to(padded, (8, LANES)), ucum_ref[...])[0:1, :]
        stats_ref[0:1, :] = counts
        stats_ref[1:2, :] = pend - padded
        stats_ref[2:3, :] = pend
        stats_ref[3:8, :] = jnp.zeros((5, LANES), F32)
        base_ref[1:2, :] = pend - padded

    @pl.when(phase == 1)
    def _():
        before = _dot(lstrict_ref[...], sel.astype(BF16))
        pos = base_ref[1:2, :] + before
        base_ref[1:2, :] = base_ref[1:2, :] + colsum
        work = jnp.where(sel > 0.0, pos + 1.0, 0.0)
        gates = gate_ref[...]
        for kk in range(TOP_K):
            m = jnp.max(work, axis=-1, keepdims=True)
            pick = jnp.logical_and(work == m, m > 0.0)
            gk_ref[:, kk:kk + 1] = jnp.sum(jnp.where(pick, gates, 0.0), axis=-1, keepdims=True)
            dest_ref[:, kk:kk + 1] = jnp.where(m > 0.0, m - 1.0, float(trash_row)).astype(jnp.int32)
            work = jnp.where(pick, 0.0, work)


def _rank(sel, gates, lstrict, ucum, trash_row):
    rows = sel.shape[0]
    tm = RANK_ROWS
    assert rows % tm == 0
    row_spec = lambda n: pl.BlockSpec((tm, n), lambda p, i: (i, 0))
    out_row_spec = lambda n: pl.BlockSpec((tm, n), lambda p, i: (i * p, 0))
    const = lambda a: pl.BlockSpec(a.shape, lambda p, i: (0,) * a.ndim)
    return pl.pallas_call(
        functools.partial(_rank_kernel, trash_row=trash_row),
        grid=(2, rows // tm),
        in_specs=[row_spec(LANES), row_spec(LANES), const(lstrict), const(ucum)],
        out_specs=[out_row_spec(TOP_K), out_row_spec(TOP_K),
                   pl.BlockSpec((8, LANES), lambda p, i: (0, 0))],
        out_shape=[jax.ShapeDtypeStruct((rows, TOP_K), jnp.int32),
                   jax.ShapeDtypeStruct((rows, TOP_K), F32),
                   jax.ShapeDtypeStruct((8, LANES), F32)],
        scratch_shapes=[pltpu.VMEM((8, LANES), F32)],
        compiler_params=pltpu.CompilerParams(
            dimension_semantics=("arbitrary", "arbitrary"), vmem_limit_bytes=VMEM_LIMIT_BYTES),
        name="rank",
    )(sel, gates, lstrict, ucum)


def _sc_mesh():
    return plsc.VectorSubcoreMesh(core_axis_name="c", subcore_axis_name="s",
                                  num_cores=SC_CORES, num_subcores=SC_SUBCORES)


def _sc_scatter_rows(src, idx, n_out_rows):
    n_src, d = src.shape
    w = SC_WINDOW
    n_win = idx.shape[0] // w
    assert idx.shape[0] % (w * SC_WORKERS) == 0 and n_src % w == 0
    per_worker = n_win // SC_WORKERS
    wins_per_copy = n_src // w

    @functools.partial(
        pl.kernel, mesh=_sc_mesh(),
        out_type=jax.ShapeDtypeStruct((n_out_rows, d), src.dtype),
        scratch_types=[pltpu.VMEM((w,), jnp.int32), pltpu.VMEM((w, d), src.dtype), pltpu.SemaphoreType.DMA],
        name="sc_scatter_rows")
    def body(src_hbm, idx_hbm, out_hbm, idx_v, rows_v, sem):
        wid = lax.axis_index("s") * SC_CORES + lax.axis_index("c")

        @pl.loop(0, per_worker)
        def _(i):
            win = wid * per_worker + i
            src_win = lax.rem(win, wins_per_copy)
            pltpu.sync_copy(idx_hbm.at[pl.ds(pl.multiple_of(win * w, w), w)], idx_v)
            pltpu.sync_copy(src_hbm.at[pl.ds(pl.multiple_of(src_win * w, w), w)], rows_v)
            pltpu.async_copy(rows_v, out_hbm.at[idx_v], sem).wait()

    return body(src, idx)


def _sc_gather_rows(table, idx):
    d = table.shape[1]
    w = SC_WINDOW
    n_win = idx.shape[0] // w
    assert idx.shape[0] % (w * SC_WORKERS) == 0
    per_worker = n_win // SC_WORKERS

    @functools.partial(
        pl.kernel, mesh=_sc_mesh(),
        out_type=jax.ShapeDtypeStruct((idx.shape[0], d), table.dtype),
        scratch_types=[pltpu.VMEM((w,), jnp.int32), pltpu.VMEM((w, d), table.dtype), pltpu.SemaphoreType.DMA],
        name="sc_gather_rows")
    def body(table_hbm, idx_hbm, out_hbm, idx_v, rows_v, sem):
        wid = lax.axis_index("s") * SC_CORES + lax.axis_index("c")

        @pl.loop(0, per_worker)
        def _(i):
            win = wid * per_worker + i
            off = pl.multiple_of(win * w, w)
            pltpu.sync_copy(idx_hbm.at[pl.ds(off, w)], idx_v)
            pltpu.async_copy(table_hbm.at[idx_v], rows_v, sem).wait()
            pltpu.sync_copy(rows_v, out_hbm.at[pl.ds(off, w)])

    return body(table, idx)


def _ffn_kernel(blk_ref, exp_ref, valid_ref, nused_ref, xb_ref, w1_ref, b1_ref, w2_ref, b2_ref, yb_ref):
    j = pl.program_id(0)

    @pl.when(j < nused_ref[0])
    def _():
        row = lax.broadcasted_iota(jnp.int32, (MOE_BLOCK, 1), 0)
        x = jnp.where(row < valid_ref[j], xb_ref[...], 0.0).astype(BF16)
        hc = _dot(x, w1_ref[0]) + b1_ref[0]
        gate = jnp.minimum(hc[:, :D_FF], SWIGLU_LIMIT)
        up = jnp.clip(hc[:, D_FF:], -SWIGLU_LIMIT, SWIGLU_LIMIT)
        act = (up + 1.0) * gate * _sigmoid(SWIGLU_ALPHA * gate)
        yb_ref[...] = _dot(act.astype(BF16), w2_ref[0]) + b2_ref[0]


def _ffn(blk_idx, blk_exp, blk_valid, n_used, xb, w1, b1, w2, b2, n_blocks):
    bm = MOE_BLOCK
    grid_spec = pltpu.PrefetchScalarGridSpec(
        num_scalar_prefetch=4,
        grid=(n_blocks,),
        in_specs=[pl.BlockSpec((bm, D_MODEL), lambda j, bi, be, bv, nu: (bi[j], 0)),
                  pl.BlockSpec((1, D_MODEL, 2 * D_FF), lambda j, bi, be, bv, nu: (be[j], 0, 0)),
                  pl.BlockSpec((1, 1, 2 * D_FF), lambda j, bi, be, bv, nu: (be[j], 0, 0)),
                  pl.BlockSpec((1, D_FF, D_MODEL), lambda j, bi, be, bv, nu: (be[j], 0, 0)),
                  pl.BlockSpec((1, 1, D_MODEL), lambda j, bi, be, bv, nu: (be[j], 0, 0))],
        out_specs=pl.BlockSpec((bm, D_MODEL), lambda j, bi, be, bv, nu: (bi[j], 0)),
    )
    return pl.pallas_call(
        _ffn_kernel,
        grid_spec=grid_spec,
        out_shape=jax.ShapeDtypeStruct(xb.shape, F32),
        compiler_params=pltpu.CompilerParams(
            dimension_semantics=("arbitrary",), vmem_limit_bytes=VMEM_LIMIT_BYTES),
        name="expert_ffn",
    )(blk_idx, blk_exp, blk_valid, n_used, xb, w1, b1, w2, b2)


def _combine_kernel(gk_ref, h1_ref, y0_ref, y1_ref, y2_ref, y3_ref, lng_ref, lnb_ref, outa_ref, outb_ref,
                    *, n_tiles_a):
    gk = gk_ref[...]
    ffn = gk[:, 0:1] * y0_ref[...]
    for kk, y_ref in enumerate((y1_ref, y2_ref, y3_ref), start=1):
        ffn = ffn + gk[:, kk:kk + 1] * y_ref[...]
    out = _layer_norm(DEEPNORM_ALPHA * h1_ref[...] + ffn, lng_ref[...], lnb_ref[...])
    i = pl.program_id(0)

    @pl.when(i < n_tiles_a)
    def _():
        outa_ref[...] = out

    @pl.when(i >= n_tiles_a)
    def _():
        outb_ref[...] = out


def _combine(gk, h1, ysel, lng, lnb, rows_a):
    rows = h1.shape[0]
    tm = COMBINE_ROWS
    assert rows % tm == 0 and rows_a % tm == 0 and 0 < rows_a < rows
    n_tiles_a = rows_a // tm
    n_tiles = rows // tm
    const = lambda a: pl.BlockSpec(a.shape, lambda i: (0,) * a.ndim)
    ksel = lambda kk: pl.BlockSpec((tm, D_MODEL), lambda i: (kk * n_tiles + i, 0))
    return pl.pallas_call(
        functools.partial(_combine_kernel, n_tiles_a=n_tiles_a),
        grid=(n_tiles,),
        in_specs=[pl.BlockSpec((tm, TOP_K), lambda i: (i, 0)),
                  pl.BlockSpec((tm, D_MODEL), lambda i: (i, 0)),
                  ksel(0), ksel(1), ksel(2), ksel(3),
                  const(lng), const(lnb)],
        out_specs=[pl.BlockSpec((tm, D_MODEL), lambda i: (jnp.minimum(i, n_tiles_a - 1), 0)),
                   pl.BlockSpec((tm, D_MODEL), lambda i: (jnp.maximum(i - n_tiles_a, 0), 0))],
        out_shape=[jax.ShapeDtypeStruct((rows_a, D_MODEL), F32),
                   jax.ShapeDtypeStruct((rows - rows_a, D_MODEL), F32)],
        compiler_params=pltpu.CompilerParams(
            dimension_semantics=("arbitrary",), vmem_limit_bytes=VMEM_LIMIT_BYTES),
        name="combine",
    )(gk, h1, ysel, ysel, ysel, ysel, lng, lnb)


def _row(v, width=None):
    v = v.reshape(1, -1).astype(F32)
    if width is not None and v.shape[1] < width:
        v = jnp.pad(v, ((0, 0), (0, width - v.shape[1])))
    return v


def _encode_all(x_all, rows_a, meta_tokens, ln_emb_g, ln_emb_b, w_in, conv_w, conv_b, dt_bias, a_log,
                d_skip, ssd_norm_g, i_bias, f_bias, mlstm_norm_g, w_out, ln1_g, ln1_b, w_router, b_router,
                w1, b1, w2, b2, ln2_g, ln2_b):
    n_seq, seq_len, _ = x_all.shape
    assert seq_len % CHUNK == 0
    n_chunks = seq_len // CHUNK + 1
    n_tok = n_seq * seq_len

    sizes = (1024, CONV_CH, 2 * SSD_HEADS, 512, 512, 1024, 1024, 2 * MLSTM_HEADS, 2 * MLSTM_HEADS)
    offs = [0]
    for s in sizes:
        offs.append(offs[-1] + s)
    w_z, w_xbc, w_dt, w_q, w_k, w_v, w_o, w_i, w_f = [w_in[:, offs[j]:offs[j + 1]] for j in range(9)]
    w_big = jnp.concatenate([w_z, w_xbc, w_q, w_k * (MLSTM_DK ** -0.5), w_v, w_o], axis=1).astype(BF16)
    zpad = jnp.zeros((D_MODEL, LANES - GATE_END), F32)
    gate_cols = []
    for d in range(2):
        gate_cols += [w_dt[:, d * SSD_HEADS:(d + 1) * SSD_HEADS],
                      w_i[:, d * MLSTM_HEADS:(d + 1) * MLSTM_HEADS],
                      w_f[:, d * MLSTM_HEADS:(d + 1) * MLSTM_HEADS], zpad]
    w_gates = jnp.concatenate(gate_cols, axis=1).astype(BF16)
    gbias = [_row(jnp.concatenate([dt_bias[d], i_bias[d], f_bias[d]]), LANES) for d in range(2)]
    alog = [_row(a_log[d], LANES) for d in range(2)]
    head_of_col = jnp.arange(D_MODEL, dtype=jnp.int32) // SSD_HEAD_DIM
    expand = (jnp.arange(LANES, dtype=jnp.int32)[:, None] == head_of_col[None, :]).astype(BF16)
    dskip = _row(jnp.repeat(d_skip, SSD_HEAD_DIM))
    conv_w8 = jnp.pad(conv_w.astype(F32), ((0, 8 - CONV_W), (0, 0)))
    meta_tile = jnp.pad(meta_tokens.astype(F32), ((PAD_FRONT, INPROJ_ROWS - CHUNK), (0, 0)))

    z, xbc, q, k, v, o, gates_f, gates_b = _inproj(x_all, meta_tile, _row(ln_emb_g), _row(ln_emb_b),
                                                   w_big, w_gates)
    xs, bm, cm = _conv(xbc, conv_w8, _row(conv_b), n_chunks)
    yb, hb = _mixer_pass(xs, bm, cm, gates_b, q, k, v, gbias[1], alog[1], expand, n_chunks, reverse=True)
    (ycat,) = _mixer_pass(xs, bm, cm, gates_f, q, k, v, gbias[0], alog[0], expand, n_chunks,
                          reverse=False,
                          final_inputs=(z, o, yb, hb, dskip, _row(ssd_norm_g), _row(mlstm_norm_g)))

    wr = jnp.pad(w_router.astype(F32), ((0, 0), (0, LANES - N_EXPERTS)))
    wrh = wr.astype(BF16)
    wrl = (wr - wrh.astype(F32)).astype(BF16)
    h1, sel, gates = _epilogue(x_all.reshape(n_tok, D_MODEL), ycat.reshape(n_tok, 2 * D_MODEL),
                               w_out.astype(BF16), _row(ln_emb_g), _row(ln_emb_b), _row(ln1_g),
                               _row(ln1_b), wrh, wrl, _row(b_router, LANES))

    n_blocks = n_tok * TOP_K // MOE_BLOCK + N_EXPERTS
    trash_row = n_blocks * MOE_BLOCK
    n_buf_rows = trash_row + MOE_BLOCK
    r_i = jnp.arange(RANK_ROWS, dtype=jnp.int32)
    lstrict = (r_i[None, :] < r_i[:, None]).astype(BF16)
    l_i = jnp.arange(LANES, dtype=jnp.int32)
    ucum = (l_i[:, None] <= l_i[None, :]).astype(BF16)
    dest, gk, stats = _rank(sel, gates, lstrict, ucum, trash_row)
    counts = stats[0, :N_EXPERTS].astype(jnp.int32)
    starts = stats[1, :N_EXPERTS].astype(jnp.int32)
    pends = stats[2, :N_EXPERTS].astype(jnp.int32)
    n_used = pends[N_EXPERTS - 1] // MOE_BLOCK
    blk = jnp.minimum(jnp.arange(n_blocks, dtype=jnp.int32), jnp.maximum(n_used - 1, 0))
    blk_exp = jnp.minimum(jnp.sum((pends[None, :] <= (blk * MOE_BLOCK)[:, None]).astype(jnp.int32), axis=1),
                          N_EXPERTS - 1).astype(jnp.int32)

    blk_last = jnp.take(starts + counts, blk_exp)
    blk_valid = jnp.clip(blk_last - blk * MOE_BLOCK, 0, MOE_BLOCK).astype(jnp.int32)
    dest_km = dest.T.reshape(-1)

    xb = _sc_scatter_rows(h1, dest_km, n_buf_rows)
    yexp = _ffn(blk, blk_exp, blk_valid, n_used.reshape(1), xb, w1.astype(BF16),
                b1.reshape(N_EXPERTS, 1, -1), w2.astype(BF16), b2.reshape(N_EXPERTS, 1, -1), n_blocks)
    ysel = _sc_gather_rows(yexp, dest_km)
    return _combine(gk, h1, ysel, _row(ln2_g), _row(ln2_b), rows_a)


def kernel(x_prompt, x_sample, meta_tokens, ln_emb_g, ln_emb_b, w_in, conv_w, conv_b, dt_bias, a_log,
           d_skip, ssd_norm_g, i_bias, f_bias, mlstm_norm_g, w_out, ln1_g, ln1_b, w_router, b_router,
           w1, b1, w2, b2, ln2_g, ln2_b):
    assert x_prompt.shape[1:] == x_sample.shape[1:]
    n_p, seq_len, d = x_prompt.shape
    n_s = x_sample.shape[0]
    x_all = jnp.concatenate([x_prompt, x_sample], axis=0).astype(F32)
    y_p, y_s = _encode_all(x_all, n_p * seq_len, meta_tokens, ln_emb_g, ln_emb_b, w_in[0], conv_w[0],
                           conv_b[0], dt_bias[0], a_log[0], d_skip[0], ssd_norm_g[0], i_bias[0], f_bias[0],
                           mlstm_norm_g[0], w_out[0], ln1_g[0], ln1_b[0], w_router[0], b_router[0],
                           w1[0], b1[0], w2[0], b2[0], ln2_g[0], ln2_b[0])
    return (y_p.reshape(n_p, seq_len, d), y_s.reshape(n_s, seq_len, d))
```

```python
import functools

import jax
import jax.numpy as jnp
from jax import lax
from jax.experimental import pallas as pl
from jax.experimental.pallas import tpu as pltpu
from jax.experimental.pallas import tpu_sc as plsc

F32 = jnp.float32
BF16 = jnp.bfloat16

D_MODEL = 1024
N_META = 16
CHUNK = 128
PAD_FRONT = CHUNK - N_META
SSD_HEADS = 16
SSD_HEAD_DIM = 64
SSD_GROUPS = 4
SSD_STATE = 128
HEADS_PER_GROUP = SSD_HEADS // SSD_GROUPS
GROUP_WIDTH = HEADS_PER_GROUP * SSD_HEAD_DIM
CONV_W = 5
CONV_HALF = CONV_W // 2
CONV_CH = D_MODEL + 2 * SSD_GROUPS * SSD_STATE
MLSTM_HEADS = 8
MLSTM_DK = 64
MLSTM_DV = 128
N_EXPERTS = 32
TOP_K = 4
D_FF = D_MODEL
SWIGLU_LIMIT = 7.0
SWIGLU_ALPHA = 1.702
DEEPNORM_ALPHA = 2.0 ** 0.25
LN_EPS = 1e-5
RMS_EPS = 1e-5
NEG_GATE = -1e30

LANES = 128
BF16_SUBLANES = 16
VMEM_LIMIT_BYTES = 56 * 1024 * 1024

GATE_DT0, GATE_I0, GATE_F0, GATE_END = 0, SSD_HEADS, SSD_HEADS + MLSTM_HEADS, SSD_HEADS + 2 * MLSTM_HEADS

INPROJ_ROWS = 512
EPILOGUE_ROWS = 512
RANK_ROWS = 512
COMBINE_ROWS = 512
MOE_BLOCK = 512

SC_CORES = 2
SC_SUBCORES = 16
SC_WORKERS = SC_CORES * SC_SUBCORES
SC_WINDOW = 32


def _dot(a, b):
    return jnp.dot(a, b, preferred_element_type=F32)


def _dot_nt(a, b):
    return lax.dot_general(a, b, (((1,), (1,)), ((), ())), preferred_element_type=F32)


def _split3(x):
    hi = x.astype(BF16)
    r1 = x - hi.astype(F32)
    mid = r1.astype(BF16)
    lo = (r1 - mid.astype(F32)).astype(BF16)
    return hi, mid, lo


def _dot_exact_lhs(a_bf16, x):
    hi, mid, lo = _split3(x)
    return _dot(a_bf16, hi) + _dot(a_bf16, mid) + _dot(a_bf16, lo)


def _dot_exact_rhs(x, b_bf16):
    hi, mid, lo = _split3(x)
    return _dot(hi, b_bf16) + _dot(mid, b_bf16) + _dot(lo, b_bf16)


def _layer_norm(x, g, b):
    mu = jnp.mean(x, axis=-1, keepdims=True)
    xc = x - mu
    var = jnp.mean(xc * xc, axis=-1, keepdims=True)
    return xc * lax.rsqrt(var + LN_EPS) * g + b


def _sigmoid(x):
    return 1.0 / (1.0 + jnp.exp(-x))


def _log1p_exp_neg_abs(x):
    return jnp.log(1.0 + jnp.exp(-jnp.abs(x)))


def _storage_chunk(c, n_chunks):
    return jnp.where(c == 0, n_chunks - 1, c - 1)


def _inproj_kernel(x_ref, meta_ref, g_ref, b_ref, wbig_ref, wg_ref,
                   z_ref, xbc_ref, q_ref, k_ref, v_ref, o_ref, gf_ref, gb_ref, *, n_x_tiles):
    x = jnp.where(pl.program_id(1) == n_x_tiles, meta_ref[...], x_ref[...])
    h = _layer_norm(x, g_ref[...], b_ref[...]).astype(BF16)

    def mm(c0, c1):
        return _dot(h, wbig_ref[:, c0:c1]).astype(BF16)

    z_ref[...] = mm(0, 1024)
    xbc_ref[:, 0:1024] = mm(1024, 2048)
    xbc_ref[:, 1024:2048] = mm(2048, 3072)
    q_ref[...] = mm(3072, 3584)
    k_ref[...] = mm(3584, 4096)
    v_ref[...] = mm(4096, 5120)
    o_ref[...] = mm(5120, 6144)
    gates = _dot(h, wg_ref[...])
    gf_ref[...] = gates[:, :LANES]
    gb_ref[...] = gates[:, LANES:]


def _inproj(x_all, meta_tile, ln_g, ln_b, w_big, w_gates):
    n_seq, seq_len, _ = x_all.shape
    tm = INPROJ_ROWS
    assert seq_len % tm == 0 and tm >= CHUNK
    n_x_tiles = seq_len // tm
    rows = seq_len + CHUNK
    row_spec = lambda n: pl.BlockSpec((None, tm, n), lambda b, i: (b, i, 0))
    const = lambda a: pl.BlockSpec(a.shape, lambda b, i: (0,) * a.ndim)
    resident = lambda a: pl.BlockSpec(a.shape, lambda b, i: (0,) * a.ndim, pipeline_mode=pl.Buffered(1))
    widths = (1024, CONV_CH, 512, 512, 1024, 1024)
    out_shapes = [jax.ShapeDtypeStruct((n_seq, rows, w), BF16) for w in widths]
    out_shapes += [jax.ShapeDtypeStruct((n_seq, rows, LANES), F32)] * 2
    return pl.pallas_call(
        functools.partial(_inproj_kernel, n_x_tiles=n_x_tiles),
        grid=(n_seq, n_x_tiles + 1),
        in_specs=[pl.BlockSpec((None, tm, D_MODEL), lambda b, i: (b, jnp.minimum(i, n_x_tiles - 1), 0)),
                  resident(meta_tile), const(ln_g), const(ln_b), resident(w_big), resident(w_gates)],
        out_specs=[row_spec(s.shape[2]) for s in out_shapes],
        out_shape=out_shapes,
        compiler_params=pltpu.CompilerParams(
            dimension_semantics=("arbitrary", "arbitrary"), vmem_limit_bytes=VMEM_LIMIT_BYTES),
        name="inproj",
    )(x_all, meta_tile, ln_g, ln_b, w_big, w_gates)


def _conv_kernel(prev_ref, main_ref, next_ref, w_ref, b_ref, xs_ref, bm_ref, cm_ref, *, n_chunks):
    c = pl.program_id(1)
    row = lax.broadcasted_iota(jnp.int32, (CHUNK, 1), 0)
    pad_rows = jnp.logical_and(c == 0, row < PAD_FRONT)
    main = jnp.where(pad_rows, 0.0, main_ref[...].astype(F32))
    prev = jnp.where(c == 0, 0.0, prev_ref[...].astype(F32))[BF16_SUBLANES - 8:, :]
    nxt = jnp.where(c == n_chunks - 1, 0.0, next_ref[...].astype(F32))[:8, :]
    xp = jnp.concatenate([prev, main, nxt], axis=0)
    w = w_ref[...]
    acc = b_ref[...] + xp[8 - CONV_HALF:8 - CONV_HALF + CHUNK, :] * w[0:1, :]
    for t in range(1, CONV_W):
        off = 8 - CONV_HALF + t
        acc = acc + xp[off:off + CHUNK, :] * w[t:t + 1, :]
    y = acc * _sigmoid(acc)
    y = jnp.where(pad_rows, 0.0, y)
    xs_ref[...] = y[:, :D_MODEL].astype(BF16)
    bm_ref[...] = y[:, D_MODEL:D_MODEL + 512].astype(BF16)
    cm_ref[...] = y[:, D_MODEL + 512:].astype(BF16)


def _conv(xbc, conv_w8, conv_b, n_chunks):
    n_seq, rows, _ = xbc.shape
    halo_blocks_per_chunk = CHUNK // BF16_SUBLANES

    def main_map(b, c):
        return (b, _storage_chunk(c, n_chunks), 0)

    def prev_map(b, c):
        sc = _storage_chunk(jnp.maximum(c - 1, 0), n_chunks)
        return (b, sc * halo_blocks_per_chunk + halo_blocks_per_chunk - 1, 0)

    def next_map(b, c):
        sc = _storage_chunk(jnp.minimum(c + 1, n_chunks - 1), n_chunks)
        return (b, sc * halo_blocks_per_chunk, 0)

    const = lambda a: pl.BlockSpec(a.shape, lambda b, c: (0,) * a.ndim)
    out_shapes = [jax.ShapeDtypeStruct((n_seq, rows, D_MODEL), BF16),
                  jax.ShapeDtypeStruct((n_seq, rows, 512), BF16),
                  jax.ShapeDtypeStruct((n_seq, rows, 512), BF16)]
    return pl.pallas_call(
        functools.partial(_conv_kernel, n_chunks=n_chunks),
        grid=(n_seq, n_chunks),
        in_specs=[pl.BlockSpec((None, BF16_SUBLANES, CONV_CH), prev_map),
                  pl.BlockSpec((None, CHUNK, CONV_CH), main_map),
                  pl.BlockSpec((None, BF16_SUBLANES, CONV_CH), next_map),
                  const(conv_w8), const(conv_b)],
        out_specs=[pl.BlockSpec((None, CHUNK, s.shape[2]), main_map) for s in out_shapes],
        out_shape=out_shapes,
        compiler_params=pltpu.CompilerParams(
            dimension_semantics=("arbitrary", "arbitrary"), vmem_limit_bytes=VMEM_LIMIT_BYTES),
        name="conv",
    )(xbc, xbc, xbc, conv_w8, conv_b)


def _mixer_kernel(*refs, reverse, final, n_chunks, n_seq):
    if final:
        (xs_ref, bm_ref, cm_ref, g_ref, q_ref, k_ref, v_ref, z_ref, o_ref, yb_ref, hb_ref,
         gbias_ref, alog_ref, expand_ref, dskip_ref, ngs_ref, ngm_ref,
         ycat_ref, s_ref, cst_ref, m_ref) = refs
    else:
        (xs_ref, bm_ref, cm_ref, g_ref, q_ref, k_ref, v_ref,
         gbias_ref, alog_ref, expand_ref,
         yout_ref, hout_ref, s_ref, cst_ref, m_ref) = refs

    t = pl.program_id(0)
    c = (n_chunks - 1 - t) if reverse else t
    end = 0 if reverse else CHUNK - 1

    @pl.when(t == 0)
    def _():
        s_ref[...] = jnp.zeros_like(s_ref)
        cst_ref[...] = jnp.zeros_like(cst_ref)
        m_ref[...] = jnp.zeros_like(m_ref)

    row = lax.broadcasted_iota(jnp.int32, (CHUNK, 1), 0)
    col = lax.broadcasted_iota(jnp.int32, (1, CHUNK), 1)
    lane = col
    allowed = (col >= row) if reverse else (col <= row)
    tri = allowed.astype(BF16)
    tri_t = ((row >= col) if reverse else (row <= col)).astype(BF16)
    is_dt = lane < GATE_I0
    is_i = jnp.logical_and(lane >= GATE_I0, lane < GATE_F0)
    is_f = jnp.logical_and(lane >= GATE_F0, lane < GATE_END)
    pad_rows = jnp.logical_and(c == 0, row < PAD_FRONT)
    a_coef = -jnp.exp(alog_ref[...])
    expand = expand_ref[...]
    left_half = lane < SSD_HEAD_DIM
    top_half = row < MLSTM_DK
    ones_blk = jnp.ones((CHUNK, MLSTM_DV), BF16)

    def one_sequence(b):
        gr = g_ref[b] + gbias_ref[...]
        lse = _log1p_exp_neg_abs(gr)
        softplus = jnp.maximum(gr, 0.0) + lse
        logsig = jnp.minimum(gr, 0.0) - lse
        val = jnp.where(is_dt, softplus, jnp.where(is_i, gr, jnp.where(is_f, logsig, 0.0)))
        val = jnp.where(pad_rows, jnp.where(is_i, NEG_GATE, 0.0), val)
        u = jnp.where(is_dt, val * a_coef, jnp.where(is_f, val, 0.0))
        val_t = val.T
        u_t = u.T
        cums = _dot_exact_lhs(tri, u)
        cums_t = _dot_exact_rhs(u_t, tri_t)

        cums_end = cums[end:end + 1, :]
        p1 = jnp.exp(cums)
        p2 = jnp.exp(cums_end - cums) * val
        ex1 = _dot(p1.astype(BF16), expand)
        ex2 = _dot(p2.astype(BF16), expand)
        chunk_decay = _dot_exact_rhs(jnp.broadcast_to(jnp.exp(cums_end), (8, LANES)), expand)[0:1, :]
        xs = xs_ref[b]
        xsf = xs.astype(F32)
        xs_w = (xsf * ex2).astype(BF16)
        y_groups = []
        for g in range(SSD_GROUPS):
            cg = cm_ref[b, :, g * SSD_STATE:(g + 1) * SSD_STATE]
            bg = bm_ref[b, :, g * SSD_STATE:(g + 1) * SSD_STATE]
            cb = _dot_nt(cg, bg)
            ys = []
            for pp in range(HEADS_PER_GROUP // 2):
                pair = g * (HEADS_PER_GROUP // 2) + pp
                xs_pair = xs[:, pair * LANES:(pair + 1) * LANES]
                y_pair = None
                for hh in range(2):
                    h = 2 * pair + hh
                    seg = cums[:, h:h + 1] - cums_t[h:h + 1, :]
                    dec = jnp.exp(jnp.where(allowed, seg, -jnp.inf))
                    m_mat = (cb * dec * val_t[h:h + 1, :]).astype(BF16)
                    keep = left_half if hh == 0 else jnp.logical_not(left_half)
                    part = _dot(m_mat, jnp.where(keep, xs_pair, jnp.zeros_like(xs_pair)))
                    y_pair = part if y_pair is None else y_pair + part
                ys.append(y_pair)
            y_diag = jnp.concatenate(ys, axis=1)
            gs = slice(g * GROUP_WIDTH, (g + 1) * GROUP_WIDTH)
            s_g = s_old[b][g]
            y_off = _dot(cg, s_g.astype(BF16)) * ex1[:, gs]
            y_groups.append(y_diag + y_off)
            bg_t = bg.astype(F32).T.astype(BF16)
            s_new_all[b].append(chunk_decay[:, gs] * s_g + _dot(bg_t, xs_w[:, gs]))
        y_ssd = jnp.concatenate(y_groups, axis=1)

        bcum_t = cums_t[GATE_F0:GATE_END, :]
        ip_t = val_t[GATE_I0:GATE_F0, :]
        g_t = bcum_t[:, end:end + 1]
        a_t = g_t - bcum_t + ip_t
        a_max = jnp.max(a_t, axis=1, keepdims=True)
        w_t = jnp.exp(a_t - a_max)
        m_prev = m_old[b][:, 0:1]
        m_new = jnp.maximum(g_t + m_prev, a_max)
        s_prev = jnp.exp(g_t + m_prev - m_new)
        s_new = jnp.exp(a_max - m_new)
        r_t = ip_t - bcum_t
        h_heads = []
        for pair in range(MLSTM_HEADS // 2):
            h0, h1 = 2 * pair, 2 * pair + 1
            q_pair = q_ref[b, :, pair * LANES:(pair + 1) * LANES]
            k_pair = k_ref[b, :, pair * LANES:(pair + 1) * LANES]
            cst = cst_old[b][pair]
            cst_b = cst.astype(BF16)
            v_pair = []
            for hh, h in enumerate((h0, h1)):
                keep = left_half if hh == 0 else jnp.logical_not(left_half)
                qh = jnp.where(keep, q_pair, jnp.zeros_like(q_pair))
                vh = v_ref[b, :, h * MLSTM_DV:(h + 1) * MLSTM_DV]
                v_pair.append(vh)
                qk = _dot_nt(qh, k_pair)
                bc = cums[:, GATE_F0 + h:GATE_F0 + h + 1]
                dlog = jnp.where(allowed, bc + r_t[h:h + 1, :], -jnp.inf)
                m_intra = jnp.max(dlog, axis=1, keepdims=True)
                m_inter = bc + m_prev[h:h + 1, :]
                m_t = jnp.maximum(m_inter, m_intra)
                s_mat = (qk * jnp.exp(dlog - m_t)).astype(BF16)
                intra = _dot(s_mat, jnp.concatenate([vh, ones_blk], axis=1))
                inter = _dot(qh, cst_b)
                tot = jnp.exp(m_inter - m_t) * inter + intra
                num = tot[:, :MLSTM_DV]
                den = tot[:, MLSTM_DV:]
                h_heads.append(num / jnp.maximum(jnp.abs(den), jnp.exp(-m_t)))
            w_rows = jnp.where(top_half, w_t[h0:h0 + 1, :], w_t[h1:h1 + 1, :])
            kw = (k_pair.astype(F32).T * w_rows).astype(BF16)
            full = _dot(kw, jnp.concatenate([v_pair[0], v_pair[1], ones_blk], axis=1))
            kvn = jnp.concatenate(
                [jnp.where(top_half, full[:, :MLSTM_DV], full[:, MLSTM_DV:2 * MLSTM_DV]),
                 full[:, 2 * MLSTM_DV:]], axis=1)
            sp_rows = jnp.where(top_half, s_prev[h0:h0 + 1, :], s_prev[h1:h1 + 1, :])
            sn_rows = jnp.where(top_half, s_new[h0:h0 + 1, :], s_new[h1:h1 + 1, :])
            cst_new_all[b].append(sp_rows * cst + sn_rows * kvn)
        m_new_all[b] = jnp.broadcast_to(m_new, (MLSTM_HEADS, LANES))
        h_ml = jnp.concatenate(h_heads, axis=1)
        if not final:
            return y_ssd.astype(BF16), h_ml.astype(BF16)

        y_tot = y_ssd + yb_ref[b].astype(F32) + dskip_ref[...] * xsf
        zz = z_ref[b].astype(F32)
        y2 = y_tot * (zz * _sigmoid(zz))
        y_n = y2 * lax.rsqrt(jnp.mean(y2 * y2, axis=-1, keepdims=True) + RMS_EPS) * ngs_ref[...]
        h_tot = h_ml + hb_ref[b].astype(F32)
        segs = []
        for h in range(MLSTM_HEADS):
            seg = h_tot[:, h * MLSTM_DV:(h + 1) * MLSTM_DV]
            segs.append(seg * lax.rsqrt(jnp.mean(seg * seg, axis=-1, keepdims=True) + RMS_EPS))
        h_n = jnp.concatenate(segs, axis=1) * ngm_ref[...]
        y_ml = _sigmoid(o_ref[b].astype(F32)) * h_n
        return y_n.astype(BF16), y_ml.astype(BF16)

    s_old = [[s_ref[b, g] for g in range(SSD_GROUPS)] for b in range(n_seq)]
    cst_old = [[cst_ref[b, p] for p in range(MLSTM_HEADS // 2)] for b in range(n_seq)]
    m_old = [m_ref[b] for b in range(n_seq)]
    s_new_all = [[] for _ in range(n_seq)]
    cst_new_all = [[] for _ in range(n_seq)]
    m_new_all = [None] * n_seq
    results = [one_sequence(b) for b in range(n_seq)]
    for b in range(n_seq):
        for g in range(SSD_GROUPS):
            s_ref[b, g] = s_new_all[b][g]
        for p in range(MLSTM_HEADS // 2):
            cst_ref[b, p] = cst_new_all[b][p]
        m_ref[b] = m_new_all[b]

    @pl.when(c > 0)
    def _():
        for b, (first, second) in enumerate(results):
            if final:
                ycat_ref[b, :, :D_MODEL] = first
                ycat_ref[b, :, D_MODEL:] = second
            else:
                yout_ref[b] = first
                hout_ref[b] = second


def _mixer_pass(xs, bm, cm, gates, q, k, v, gbias, alog, expand, n_chunks, *, reverse, final_inputs=None):
    final = final_inputs is not None
    n_seq = xs.shape[0]
    seq_len = (n_chunks - 1) * CHUNK

    def chunk_of(t):
        return (n_chunks - 1 - t) if reverse else t

    def pad_map(t):
        return (0, _storage_chunk(chunk_of(t), n_chunks), 0)

    def out_map(t):
        return (0, jnp.maximum(chunk_of(t) - 1, 0), 0)

    const = lambda a: pl.BlockSpec(a.shape, lambda t: (0,) * a.ndim)
    pad_spec = lambda n: pl.BlockSpec((n_seq, CHUNK, n), pad_map)
    out_spec = lambda n: pl.BlockSpec((n_seq, CHUNK, n), out_map)
    in_arrays = [xs, bm, cm, gates, q, k, v]
    in_specs = [pad_spec(a.shape[2]) for a in in_arrays]
    if final:
        z, o, yb, hb, dskip, ngs, ngm = final_inputs
        in_arrays += [z, o, yb, hb]
        in_specs += [pad_spec(1024), pad_spec(1024), out_spec(1024), out_spec(1024)]
        in_arrays += [gbias, alog, expand, dskip, ngs, ngm]
        in_specs += [const(a) for a in (gbias, alog, expand, dskip, ngs, ngm)]
        out_shape = [jax.ShapeDtypeStruct((n_seq, seq_len, 2 * D_MODEL), BF16)]
        out_specs = [out_spec(2 * D_MODEL)]
    else:
        in_arrays += [gbias, alog, expand]
        in_specs += [const(a) for a in (gbias, alog, expand)]
        out_shape = [jax.ShapeDtypeStruct((n_seq, seq_len, D_MODEL), BF16),
                     jax.ShapeDtypeStruct((n_seq, seq_len, D_MODEL), BF16)]
        out_specs = [out_spec(D_MODEL), out_spec(D_MODEL)]
    return pl.pallas_call(
        functools.partial(_mixer_kernel, reverse=reverse, final=final, n_chunks=n_chunks, n_seq=n_seq),
        grid=(n_chunks,),
        in_specs=in_specs,
        out_specs=out_specs,
        out_shape=out_shape,
        scratch_shapes=[pltpu.VMEM((n_seq, SSD_GROUPS, SSD_STATE, GROUP_WIDTH), F32),
                        pltpu.VMEM((n_seq, MLSTM_HEADS // 2, 2 * MLSTM_DK, 2 * MLSTM_DV), F32),
                        pltpu.VMEM((n_seq, MLSTM_HEADS, LANES), F32)],
        compiler_params=pltpu.CompilerParams(
            dimension_semantics=("arbitrary",), vmem_limit_bytes=VMEM_LIMIT_BYTES),
        name="mixer_fwd" if final else "mixer_bwd",
    )(*in_arrays)


def _epilogue_kernel(x_ref, ycat_ref, wout_ref, lng0_ref, lnb0_ref, lng1_ref, lnb1_ref,
                     wrh_ref, wrl_ref, br_ref, h1_ref, sel_ref, gate_ref):
    h0 = _layer_norm(x_ref[...], lng0_ref[...], lnb0_ref[...])
    mix = _dot(ycat_ref[...], wout_ref[...])
    h1 = _layer_norm(DEEPNORM_ALPHA * h0 + mix, lng1_ref[...], lnb1_ref[...])
    h1_ref[...] = h1
    hh = h1.astype(BF16)
    hl = (h1 - hh.astype(F32)).astype(BF16)
    wrh = wrh_ref[...]
    logits = _dot(hh, wrh) + _dot(hl, wrh) + _dot(hh, wrl_ref[...]) + br_ref[...]
    lane = lax.broadcasted_iota(jnp.int32, (1, LANES), 1)
    lane_f = lane.astype(F32)
    logits = jnp.where(lane < N_EXPERTS, logits, -jnp.inf)
    work = logits
    sel = jnp.zeros(logits.shape, jnp.bool_)
    top = None
    for _ in range(TOP_K):
        m = jnp.max(work, axis=-1, keepdims=True)
        if top is None:
            top = m
        first = jnp.min(jnp.where(work == m, lane_f, float(LANES)), axis=-1, keepdims=True)
        pick = lane_f == first
        sel = jnp.logical_or(sel, pick)
        work = jnp.where(pick, -jnp.inf, work)
    e = jnp.where(sel, jnp.exp(logits - top), 0.0)
    gate_ref[...] = e / jnp.sum(e, axis=-1, keepdims=True)
    sel_ref[...] = sel.astype(F32)


def _epilogue(x_all, ycat, w_out, lng0, lnb0, lng1, lnb1, wrh, wrl, br):
    rows = x_all.shape[0]
    tm = EPILOGUE_ROWS
    assert rows % tm == 0
    row_spec = lambda n: pl.BlockSpec((tm, n), lambda i: (i, 0))
    const = lambda a: pl.BlockSpec(a.shape, lambda i: (0,) * a.ndim)
    out_shape = [jax.ShapeDtypeStruct((rows, D_MODEL), F32),
                 jax.ShapeDtypeStruct((rows, LANES), F32),
                 jax.ShapeDtypeStruct((rows, LANES), F32)]
    consts = (w_out, lng0, lnb0, lng1, lnb1, wrh, wrl, br)
    return pl.pallas_call(
        _epilogue_kernel,
        grid=(rows // tm,),
        in_specs=[row_spec(D_MODEL), row_spec(2 * D_MODEL)] + [const(a) for a in consts],
        out_specs=[row_spec(D_MODEL), row_spec(LANES), row_spec(LANES)],
        out_shape=out_shape,
        compiler_params=pltpu.CompilerParams(
            dimension_semantics=("arbitrary",), vmem_limit_bytes=VMEM_LIMIT_BYTES),
        name="epilogue",
    )(x_all, ycat, *consts)


def _rank_kernel(sel_ref, gate_ref, lstrict_ref, ucum_ref, dest_ref, gk_ref, stats_ref,
                 base_ref, *, trash_row):
    phase = pl.program_id(0)
    i = pl.program_id(1)
    sel = sel_ref[...]
    colsum = jnp.sum(sel, axis=0, keepdims=True)

    @pl.when(jnp.logical_and(phase == 0, i == 0))
    def _():
        base_ref[...] = jnp.zeros_like(base_ref)

    @pl.when(phase == 0)
    def _():
        base_ref[0:1, :] = base_ref[0:1, :] + colsum

    @pl.when(jnp.logical_and(phase == 1, i == 0))
    def _():
        counts = base_ref[0:1, :]
        padded = jnp.ceil(counts / MOE_BLOCK) * MOE_BLOCK
        pend = _dot_exact_rhs(jnp.broadcast_to(padded, (8, LANES)), ucum_ref[...])[0:1, :]
        stats_ref[0:1, :] = counts
        stats_ref[1:2, :] = pend - padded
        stats_ref[2:3, :] = pend
        stats_ref[3:8, :] = jnp.zeros((5, LANES), F32)
        base_ref[1:2, :] = pend - padded

    @pl.when(phase == 1)
    def _():
        before = _dot(lstrict_ref[...], sel.astype(BF16))
        pos = base_ref[1:2, :] + before
        base_ref[1:2, :] = base_ref[1:2, :] + colsum
        work = jnp.where(sel > 0.0, pos + 1.0, 0.0)
        gates = gate_ref[...]
        for kk in range(TOP_K):
            m = jnp.max(work, axis=-1, keepdims=True)
            pick = jnp.logical_and(work == m, m > 0.0)
            gk_ref[:, kk:kk + 1] = jnp.sum(jnp.where(pick, gates, 0.0), axis=-1, keepdims=True)
            dest_ref[:, kk:kk + 1] = jnp.where(m > 0.0, m - 1.0, float(trash_row)).astype(jnp.int32)
            work = jnp.where(pick, 0.0, work)


def _rank(sel, gates, lstrict, ucum, trash_row):
    rows = sel.shape[0]
    tm = RANK_ROWS
    assert rows % tm == 0
    row_spec = lambda n: pl.BlockSpec((tm, n), lambda p, i: (i, 0))
    out_row_spec = lambda n: pl.BlockSpec((tm, n), lambda p, i: (i * p, 0))
    const = lambda a: pl.BlockSpec(a.shape, lambda p, i: (0,) * a.ndim)
    return pl.pallas_call(
        functools.partial(_rank_kernel, trash_row=trash_row),
        grid=(2, rows // tm),
        in_specs=[row_spec(LANES), row_spec(LANES), const(lstrict), const(ucum)],
        out_specs=[out_row_spec(TOP_K), out_row_spec(TOP_K),
                   pl.BlockSpec((8, LANES), lambda p, i: (0, 0))],
        out_shape=[jax.ShapeDtypeStruct((rows, TOP_K), jnp.int32),
                   jax.ShapeDtypeStruct((rows, TOP_K), F32),
                   jax.ShapeDtypeStruct((8, LANES), F32)],
        scratch_shapes=[pltpu.VMEM((8, LANES), F32)],
        compiler_params=pltpu.CompilerParams(
            dimension_semantics=("arbitrary", "arbitrary"), vmem_limit_bytes=VMEM_LIMIT_BYTES),
        name="rank",
    )(sel, gates, lstrict, ucum)


def _sc_mesh():
    return plsc.VectorSubcoreMesh(core_axis_name="c", subcore_axis_name="s",
                                  num_cores=SC_CORES, num_subcores=SC_SUBCORES)


def _sc_scatter_rows(src, idx, n_out_rows):
    n_src, d = src.shape
    w = SC_WINDOW
    n_win = idx.shape[0] // w
    assert idx.shape[0] % (w * SC_WORKERS) == 0 and n_src % w == 0
    per_worker = n_win // SC_WORKERS
    wins_per_copy = n_src // w

    @functools.partial(
        pl.kernel, mesh=_sc_mesh(),
        out_type=jax.ShapeDtypeStruct((n_out_rows, d), src.dtype),
        scratch_types=[pltpu.VMEM((w,), jnp.int32), pltpu.VMEM((w, d), src.dtype), pltpu.SemaphoreType.DMA],
        name="sc_scatter_rows")
    def body(src_hbm, idx_hbm, out_hbm, idx_v, rows_v, sem):
        wid = lax.axis_index("s") * SC_CORES + lax.axis_index("c")

        @pl.loop(0, per_worker)
        def _(i):
            win = wid * per_worker + i
            src_win = lax.rem(win, wins_per_copy)
            pltpu.sync_copy(idx_hbm.at[pl.ds(pl.multiple_of(win * w, w), w)], idx_v)
            pltpu.sync_copy(src_hbm.at[pl.ds(pl.multiple_of(src_win * w, w), w)], rows_v)
            pltpu.async_copy(rows_v, out_hbm.at[idx_v], sem).wait()

    return body(src, idx)


def _sc_gather_rows(table, idx):
    d = table.shape[1]
    w = SC_WINDOW
    n_win = idx.shape[0] // w
    assert idx.shape[0] % (w * SC_WORKERS) == 0
    per_worker = n_win // SC_WORKERS

    @functools.partial(
        pl.kernel, mesh=_sc_mesh(),
        out_type=jax.ShapeDtypeStruct((idx.shape[0], d), table.dtype),
        scratch_types=[pltpu.VMEM((w,), jnp.int32), pltpu.VMEM((w, d), table.dtype), pltpu.SemaphoreType.DMA],
        name="sc_gather_rows")
    def body(table_hbm, idx_hbm, out_hbm, idx_v, rows_v, sem):
        wid = lax.axis_index("s") * SC_CORES + lax.axis_index("c")

        @pl.loop(0, per_worker)
        def _(i):
            win = wid * per_worker + i
            off = pl.multiple_of(win * w, w)
            pltpu.sync_copy(idx_hbm.at[pl.ds(off, w)], idx_v)
            pltpu.async_copy(table_hbm.at[idx_v], rows_v, sem).wait()
            pltpu.sync_copy(rows_v, out_hbm.at[pl.ds(off, w)])

    return body(table, idx)


def _ffn_kernel(blk_ref, exp_ref, valid_ref, nused_ref, xb_ref, w1_ref, b1_ref, w2_ref, b2_ref, yb_ref,
                w1b_ref, w2b_ref):
    j = pl.program_id(0)
    active = j < nused_ref[0]

    @pl.when(jnp.logical_and(active, jnp.logical_or(j == 0, exp_ref[j] != exp_ref[jnp.maximum(j - 1, 0)])))
    def _():
        w1b_ref[...] = w1_ref[0].astype(BF16)
        w2b_ref[...] = w2_ref[0].astype(BF16)

    @pl.when(active)
    def _():
        row = lax.broadcasted_iota(jnp.int32, (MOE_BLOCK, 1), 0)
        x = jnp.where(row < valid_ref[j], xb_ref[...], 0.0).astype(BF16)
        hc = _dot(x, w1b_ref[...]) + b1_ref[0]
        gate = jnp.minimum(hc[:, :D_FF], SWIGLU_LIMIT)
        up = jnp.clip(hc[:, D_FF:], -SWIGLU_LIMIT, SWIGLU_LIMIT)
        act = (up + 1.0) * gate * _sigmoid(SWIGLU_ALPHA * gate)
        yb_ref[...] = _dot(act.astype(BF16), w2b_ref[...]) + b2_ref[0]


def _ffn(blk_idx, blk_exp, blk_valid, n_used, xb, w1, b1, w2, b2, n_blocks):
    bm = MOE_BLOCK
    grid_spec = pltpu.PrefetchScalarGridSpec(
        num_scalar_prefetch=4,
        grid=(n_blocks,),
        in_specs=[pl.BlockSpec((bm, D_MODEL), lambda j, bi, be, bv, nu: (bi[j], 0)),
                  pl.BlockSpec((1, D_MODEL, 2 * D_FF), lambda j, bi, be, bv, nu: (be[j], 0, 0)),
                  pl.BlockSpec((1, 1, 2 * D_FF), lambda j, bi, be, bv, nu: (be[j], 0, 0)),
                  pl.BlockSpec((1, D_FF, D_MODEL), lambda j, bi, be, bv, nu: (be[j], 0, 0)),
                  pl.BlockSpec((1, 1, D_MODEL), lambda j, bi, be, bv, nu: (be[j], 0, 0))],
        out_specs=pl.BlockSpec((bm, D_MODEL), lambda j, bi, be, bv, nu: (bi[j], 0)),
        scratch_shapes=[pltpu.VMEM((D_MODEL, 2 * D_FF), BF16), pltpu.VMEM((D_FF, D_MODEL), BF16)],
    )
    return pl.pallas_call(
        _ffn_kernel,
        grid_spec=grid_spec,
        out_shape=jax.ShapeDtypeStruct(xb.shape, F32),
        compiler_params=pltpu.CompilerParams(
            dimension_semantics=("arbitrary",), vmem_limit_bytes=VMEM_LIMIT_BYTES),
        name="expert_ffn",
    )(blk_idx, blk_exp, blk_valid, n_used, xb, w1, b1, w2, b2)


def _combine_kernel(gk_ref, h1_ref, y0_ref, y1_ref, y2_ref, y3_ref, lng_ref, lnb_ref, outa_ref, outb_ref,
                    *, n_tiles_a):
    gk = gk_ref[...]
    ffn = gk[:, 0:1] * y0_ref[...]
    for kk, y_ref in enumerate((y1_ref, y2_ref, y3_ref), start=1):
        ffn = ffn + gk[:, kk:kk + 1] * y_ref[...]
    out = _layer_norm(DEEPNORM_ALPHA * h1_ref[...] + ffn, lng_ref[...], lnb_ref[...])
    i = pl.program_id(0)

    @pl.when(i < n_tiles_a)
    def _():
        outa_ref[...] = out

    @pl.when(i >= n_tiles_a)
    def _():
        outb_ref[...] = out


def _combine(gk, h1, ysel, lng, lnb, rows_a):
    rows = h1.shape[0]
    tm = COMBINE_ROWS
    assert rows % tm == 0 and rows_a % tm == 0 and 0 < rows_a < rows
    n_tiles_a = rows_a // tm
    n_tiles = rows // tm
    const = lambda a: pl.BlockSpec(a.shape, lambda i: (0,) * a.ndim)
    ksel = lambda kk: pl.BlockSpec((tm, D_MODEL), lambda i: (kk * n_tiles + i, 0))
    return pl.pallas_call(
        functools.partial(_combine_kernel, n_tiles_a=n_tiles_a),
        grid=(n_tiles,),
        in_specs=[pl.BlockSpec((tm, TOP_K), lambda i: (i, 0)),
                  pl.BlockSpec((tm, D_MODEL), lambda i: (i, 0)),
                  ksel(0), ksel(1), ksel(2), ksel(3),
                  const(lng), const(lnb)],
        out_specs=[pl.BlockSpec((tm, D_MODEL), lambda i: (jnp.minimum(i, n_tiles_a - 1), 0)),
                   pl.BlockSpec((tm, D_MODEL), lambda i: (jnp.maximum(i - n_tiles_a, 0), 0))],
        out_shape=[jax.ShapeDtypeStruct((rows_a, D_MODEL), F32),
                   jax.ShapeDtypeStruct((rows - rows_a, D_MODEL), F32)],
        compiler_params=pltpu.CompilerParams(
            dimension_semantics=("arbitrary",), vmem_limit_bytes=VMEM_LIMIT_BYTES),
        name="combine",
    )(gk, h1, ysel, ysel, ysel, ysel, lng, lnb)


def _row(v, width=None):
    v = v.reshape(1, -1).astype(F32)
    if width is not None and v.shape[1] < width:
        v = jnp.pad(v, ((0, 0), (0, width - v.shape[1])))
    return v


def _encode_all(x_all, rows_a, meta_tokens, ln_emb_g, ln_emb_b, w_in, conv_w, conv_b, dt_bias, a_log,
                d_skip, ssd_norm_g, i_bias, f_bias, mlstm_norm_g, w_out, ln1_g, ln1_b, w_router, b_router,
                w1, b1, w2, b2, ln2_g, ln2_b):
    n_seq, seq_len, _ = x_all.shape
    assert seq_len % CHUNK == 0
    n_chunks = seq_len // CHUNK + 1
    n_tok = n_seq * seq_len

    sizes = (1024, CONV_CH, 2 * SSD_HEADS, 512, 512, 1024, 1024, 2 * MLSTM_HEADS, 2 * MLSTM_HEADS)
    offs = [0]
    for s in sizes:
        offs.append(offs[-1] + s)
    w_z, w_xbc, w_dt, w_q, w_k, w_v, w_o, w_i, w_f = [w_in[:, offs[j]:offs[j + 1]] for j in range(9)]
    w_big = jnp.concatenate([w_z, w_xbc, w_q, w_k * (MLSTM_DK ** -0.5), w_v, w_o], axis=1).astype(BF16)
    zpad = jnp.zeros((D_MODEL, LANES - GATE_END), F32)
    gate_cols = []
    for d in range(2):
        gate_cols += [w_dt[:, d * SSD_HEADS:(d + 1) * SSD_HEADS],
                      w_i[:, d * MLSTM_HEADS:(d + 1) * MLSTM_HEADS],
                      w_f[:, d * MLSTM_HEADS:(d + 1) * MLSTM_HEADS], zpad]
    w_gates = jnp.concatenate(gate_cols, axis=1).astype(BF16)
    gbias = [_row(jnp.concatenate([dt_bias[d], i_bias[d], f_bias[d]]), LANES) for d in range(2)]
    alog = [_row(a_log[d], LANES) for d in range(2)]
    head_of_col = jnp.arange(D_MODEL, dtype=jnp.int32) // SSD_HEAD_DIM
    expand = (jnp.arange(LANES, dtype=jnp.int32)[:, None] == head_of_col[None, :]).astype(BF16)
    dskip = _row(jnp.repeat(d_skip, SSD_HEAD_DIM))
    conv_w8 = jnp.pad(conv_w.astype(F32), ((0, 8 - CONV_W), (0, 0)))
    meta_tile = jnp.pad(meta_tokens.astype(F32), ((PAD_FRONT, INPROJ_ROWS - CHUNK), (0, 0)))

    z, xbc, q, k, v, o, gates_f, gates_b = _inproj(x_all, meta_tile, _row(ln_emb_g), _row(ln_emb_b),
                                                   w_big, w_gates)
    xs, bm, cm = _conv(xbc, conv_w8, _row(conv_b), n_chunks)
    yb, hb = _mixer_pass(xs, bm, cm, gates_b, q, k, v, gbias[1], alog[1], expand, n_chunks, reverse=True)
    (ycat,) = _mixer_pass(xs, bm, cm, gates_f, q, k, v, gbias[0], alog[0], expand, n_chunks,
                          reverse=False,
                          final_inputs=(z, o, yb, hb, dskip, _row(ssd_norm_g), _row(mlstm_norm_g)))

    wr = jnp.pad(w_router.astype(F32), ((0, 0), (0, LANES - N_EXPERTS)))
    wrh = wr.astype(BF16)
    wrl = (wr - wrh.astype(F32)).astype(BF16)
    h1, sel, gates = _epilogue(x_all.reshape(n_tok, D_MODEL), ycat.reshape(n_tok, 2 * D_MODEL),
                               w_out.astype(BF16), _row(ln_emb_g), _row(ln_emb_b), _row(ln1_g),
                               _row(ln1_b), wrh, wrl, _row(b_router, LANES))

    n_blocks = n_tok * TOP_K // MOE_BLOCK + N_EXPERTS
    trash_row = n_blocks * MOE_BLOCK
    n_buf_rows = trash_row + MOE_BLOCK
    r_i = jnp.arange(RANK_ROWS, dtype=jnp.int32)
    lstrict = (r_i[None, :] < r_i[:, None]).astype(BF16)
    l_i = jnp.arange(LANES, dtype=jnp.int32)
    ucum = (l_i[:, None] <= l_i[None, :]).astype(BF16)
    dest, gk, stats = _rank(sel, gates, lstrict, ucum, trash_row)
    counts = stats[0, :N_EXPERTS].astype(jnp.int32)
    starts = stats[1, :N_EXPERTS].astype(jnp.int32)
    pends = stats[2, :N_EXPERTS].astype(jnp.int32)
    n_used = pends[N_EXPERTS - 1] // MOE_BLOCK
    blk = jnp.minimum(jnp.arange(n_blocks, dtype=jnp.int32), jnp.maximum(n_used - 1, 0))
    blk_exp = jnp.minimum(jnp.sum((pends[None, :] <= (blk * MOE_BLOCK)[:, None]).astype(jnp.int32), axis=1),
                          N_EXPERTS - 1).astype(jnp.int32)

    blk_last = jnp.take(starts + counts, blk_exp)
    blk_valid = jnp.clip(blk_last - blk * MOE_BLOCK, 0, MOE_BLOCK).astype(jnp.int32)
    dest_km = dest.T.reshape(-1)

    xb = _sc_scatter_rows(h1, dest_km, n_buf_rows)
    yexp = _ffn(blk, blk_exp, blk_valid, n_used.reshape(1), xb, w1,
                b1.reshape(N_EXPERTS, 1, -1), w2, b2.reshape(N_EXPERTS, 1, -1), n_blocks)
    ysel = _sc_gather_rows(yexp, dest_km)
    return _combine(gk, h1, ysel, _row(ln2_g), _row(ln2_b), rows_a)


def kernel(x_prompt, x_sample, meta_tokens, ln_emb_g, ln_emb_b, w_in, conv_w, conv_b, dt_bias, a_log,
           d_skip, ssd_norm_g, i_bias, f_bias, mlstm_norm_g, w_out, ln1_g, ln1_b, w_router, b_router,
           w1, b1, w2, b2, ln2_g, ln2_b):
    assert x_prompt.shape[1:] == x_sample.shape[1:]
    n_p, seq_len, d = x_prompt.shape
    n_s = x_sample.shape[0]
    x_all = jnp.concatenate([x_prompt, x_sample], axis=0).astype(F32)
    y_p, y_s = _encode_all(x_all, n_p * seq_len, meta_tokens, ln_emb_g, ln_emb_b, w_in[0], conv_w[0],
                           conv_b[0], dt_bias[0], a_log[0], d_skip[0], ssd_norm_g[0], i_bias[0], f_bias[0],
                           mlstm_norm_g[0], w_out[0], ln1_g[0], ln1_b[0], w_router[0], b_router[0],
                           w1[0], b1[0], w2[0], b2[0], ln2_g[0], ln2_b[0])
    return (y_p.reshape(n_p, seq_len, d), y_s.reshape(n_s, seq_len, d))
```

```python
import functools

import jax
import jax.numpy as jnp
from jax import lax
from jax.experimental import pallas as pl
from jax.experimental.pallas import tpu as pltpu
from jax.experimental.pallas import tpu_sc as plsc

F32 = jnp.float32
BF16 = jnp.bfloat16

D_MODEL = 1024
N_META = 16
CHUNK = 128
PAD_FRONT = CHUNK - N_META
SSD_HEADS = 16
SSD_HEAD_DIM = 64
SSD_GROUPS = 4
SSD_STATE = 128
HEADS_PER_GROUP = SSD_HEADS // SSD_GROUPS
GROUP_WIDTH = HEADS_PER_GROUP * SSD_HEAD_DIM
CONV_W = 5
CONV_HALF = CONV_W // 2
CONV_CH = D_MODEL + 2 * SSD_GROUPS * SSD_STATE
MLSTM_HEADS = 8
MLSTM_DK = 64
MLSTM_DV = 128
N_EXPERTS = 32
TOP_K = 4
D_FF = D_MODEL
SWIGLU_LIMIT = 7.0
SWIGLU_ALPHA = 1.702
DEEPNORM_ALPHA = 2.0 ** 0.25
LN_EPS = 1e-5
RMS_EPS = 1e-5
NEG_GATE = -1e30

LANES = 128
BF16_SUBLANES = 16
VMEM_LIMIT_BYTES = 56 * 1024 * 1024

GATE_DT0, GATE_I0, GATE_F0, GATE_END = 0, SSD_HEADS, SSD_HEADS + MLSTM_HEADS, SSD_HEADS + 2 * MLSTM_HEADS

INPROJ_ROWS = 512
EPILOGUE_ROWS = 512
RANK_ROWS = 512
COMBINE_ROWS = 512
MOE_BLOCK = 512

SC_CORES = 2
SC_SUBCORES = 16
SC_WORKERS = SC_CORES * SC_SUBCORES
SC_WINDOW = 32


def _dot(a, b):
    return jnp.dot(a, b, preferred_element_type=F32)


def _dot_nt(a, b):
    return lax.dot_general(a, b, (((1,), (1,)), ((), ())), preferred_element_type=F32)


def _split3(x):
    hi = x.astype(BF16)
    r1 = x - hi.astype(F32)
    mid = r1.astype(BF16)
    lo = (r1 - mid.astype(F32)).astype(BF16)
    return hi, mid, lo


def _dot_exact_lhs(a_bf16, x):
    hi, mid, lo = _split3(x)
    return _dot(a_bf16, hi) + _dot(a_bf16, mid) + _dot(a_bf16, lo)


def _dot_exact_rhs(x, b_bf16):
    hi, mid, lo = _split3(x)
    return _dot(hi, b_bf16) + _dot(mid, b_bf16) + _dot(lo, b_bf16)


def _layer_norm(x, g, b):
    mu = jnp.mean(x, axis=-1, keepdims=True)
    xc = x - mu
    var = jnp.mean(xc * xc, axis=-1, keepdims=True)
    return xc * lax.rsqrt(var + LN_EPS) * g + b


def _sigmoid(x):
    return 1.0 / (1.0 + jnp.exp(-x))


def _log1p_exp_neg_abs(x):
    return jnp.log(1.0 + jnp.exp(-jnp.abs(x)))


def _pack_bf16_pairs(x):
    n = x.shape[1] // 2
    bits = lax.bitcast_convert_type(x.astype(BF16).astype(F32), jnp.uint32)
    return (bits[:, :n] >> 16) | bits[:, n:]


def _unpack_bf16_pairs(words):
    lo = lax.bitcast_convert_type(words << 16, F32)
    hi = lax.bitcast_convert_type(words & jnp.uint32(0xFFFF0000), F32)
    return jnp.concatenate([lo, hi], axis=1)


def _storage_chunk(c, n_chunks):
    return jnp.where(c == 0, n_chunks - 1, c - 1)


def _inproj_kernel(x_ref, meta_ref, g_ref, b_ref, wbig_ref, wg_ref,
                   z_ref, xbc_ref, q_ref, k_ref, v_ref, o_ref, gf_ref, gb_ref, *, n_x_tiles):
    x = jnp.where(pl.program_id(1) == n_x_tiles, meta_ref[...], x_ref[...])
    h = _layer_norm(x, g_ref[...], b_ref[...]).astype(BF16)

    def mm(c0, c1):
        return _dot(h, wbig_ref[:, c0:c1]).astype(BF16)

    z_ref[...] = mm(0, 1024)
    xbc_ref[:, 0:1024] = mm(1024, 2048)
    xbc_ref[:, 1024:2048] = mm(2048, 3072)
    q_ref[...] = mm(3072, 3584)
    k_ref[...] = mm(3584, 4096)
    v_ref[...] = mm(4096, 5120)
    o_ref[...] = mm(5120, 6144)
    gates = _dot(h, wg_ref[...])
    gf_ref[...] = gates[:, :LANES]
    gb_ref[...] = gates[:, LANES:]


def _inproj(x_all, meta_tile, ln_g, ln_b, w_big, w_gates):
    n_seq, seq_len, _ = x_all.shape
    tm = INPROJ_ROWS
    assert seq_len % tm == 0 and tm >= CHUNK
    n_x_tiles = seq_len // tm
    rows = seq_len + CHUNK
    row_spec = lambda n: pl.BlockSpec((None, tm, n), lambda b, i: (b, i, 0))
    const = lambda a: pl.BlockSpec(a.shape, lambda b, i: (0,) * a.ndim)
    resident = lambda a: pl.BlockSpec(a.shape, lambda b, i: (0,) * a.ndim, pipeline_mode=pl.Buffered(1))
    widths = (1024, CONV_CH, 512, 512, 1024, 1024)
    out_shapes = [jax.ShapeDtypeStruct((n_seq, rows, w), BF16) for w in widths]
    out_shapes += [jax.ShapeDtypeStruct((n_seq, rows, LANES), F32)] * 2
    return pl.pallas_call(
        functools.partial(_inproj_kernel, n_x_tiles=n_x_tiles),
        grid=(n_seq, n_x_tiles + 1),
        in_specs=[pl.BlockSpec((None, tm, D_MODEL), lambda b, i: (b, jnp.minimum(i, n_x_tiles - 1), 0)),
                  resident(meta_tile), const(ln_g), const(ln_b), resident(w_big), resident(w_gates)],
        out_specs=[row_spec(s.shape[2]) for s in out_shapes],
        out_shape=out_shapes,
        compiler_params=pltpu.CompilerParams(
            dimension_semantics=("arbitrary", "arbitrary"), vmem_limit_bytes=VMEM_LIMIT_BYTES),
        name="inproj",
    )(x_all, meta_tile, ln_g, ln_b, w_big, w_gates)


def _conv_kernel(prev_ref, main_ref, next_ref, w_ref, b_ref, xs_ref, bm_ref, cm_ref, *, n_chunks):
    c = pl.program_id(1)
    row = lax.broadcasted_iota(jnp.int32, (CHUNK, 1), 0)
    pad_rows = jnp.logical_and(c == 0, row < PAD_FRONT)
    main = jnp.where(pad_rows, 0.0, main_ref[...].astype(F32))
    prev = jnp.where(c == 0, 0.0, prev_ref[...].astype(F32))[BF16_SUBLANES - 8:, :]
    nxt = jnp.where(c == n_chunks - 1, 0.0, next_ref[...].astype(F32))[:8, :]
    xp = jnp.concatenate([prev, main, nxt], axis=0)
    w = w_ref[...]
    acc = b_ref[...] + xp[8 - CONV_HALF:8 - CONV_HALF + CHUNK, :] * w[0:1, :]
    for t in range(1, CONV_W):
        off = 8 - CONV_HALF + t
        acc = acc + xp[off:off + CHUNK, :] * w[t:t + 1, :]
    y = acc * _sigmoid(acc)
    y = jnp.where(pad_rows, 0.0, y)
    xs_ref[...] = y[:, :D_MODEL].astype(BF16)
    bm_ref[...] = y[:, D_MODEL:D_MODEL + 512].astype(BF16)
    cm_ref[...] = y[:, D_MODEL + 512:].astype(BF16)


def _conv(xbc, conv_w8, conv_b, n_chunks):
    n_seq, rows, _ = xbc.shape
    halo_blocks_per_chunk = CHUNK // BF16_SUBLANES

    def main_map(b, c):
        return (b, _storage_chunk(c, n_chunks), 0)

    def prev_map(b, c):
        sc = _storage_chunk(jnp.maximum(c - 1, 0), n_chunks)
        return (b, sc * halo_blocks_per_chunk + halo_blocks_per_chunk - 1, 0)

    def next_map(b, c):
        sc = _storage_chunk(jnp.minimum(c + 1, n_chunks - 1), n_chunks)
        return (b, sc * halo_blocks_per_chunk, 0)

    const = lambda a: pl.BlockSpec(a.shape, lambda b, c: (0,) * a.ndim)
    out_shapes = [jax.ShapeDtypeStruct((n_seq, rows, D_MODEL), BF16),
                  jax.ShapeDtypeStruct((n_seq, rows, 512), BF16),
                  jax.ShapeDtypeStruct((n_seq, rows, 512), BF16)]
    return pl.pallas_call(
        functools.partial(_conv_kernel, n_chunks=n_chunks),
        grid=(n_seq, n_chunks),
        in_specs=[pl.BlockSpec((None, BF16_SUBLANES, CONV_CH), prev_map),
                  pl.BlockSpec((None, CHUNK, CONV_CH), main_map),
                  pl.BlockSpec((None, BF16_SUBLANES, CONV_CH), next_map),
                  const(conv_w8), const(conv_b)],
        out_specs=[pl.BlockSpec((None, CHUNK, s.shape[2]), main_map) for s in out_shapes],
        out_shape=out_shapes,
        compiler_params=pltpu.CompilerParams(
            dimension_semantics=("arbitrary", "arbitrary"), vmem_limit_bytes=VMEM_LIMIT_BYTES),
        name="conv",
    )(xbc, xbc, xbc, conv_w8, conv_b)


def _mixer_kernel(*refs, reverse, final, n_chunks, n_seq):
    if final:
        (xs_ref, bm_ref, cm_ref, g_ref, q_ref, k_ref, v_ref, z_ref, o_ref, yb_ref, hb_ref,
         gbias_ref, alog_ref, expand_ref, dskip_ref, ngs_ref, ngm_ref,
         ycat_ref, s_ref, cst_ref, m_ref) = refs
    else:
        (xs_ref, bm_ref, cm_ref, g_ref, q_ref, k_ref, v_ref,
         gbias_ref, alog_ref, expand_ref,
         yout_ref, hout_ref, s_ref, cst_ref, m_ref) = refs

    t = pl.program_id(0)
    c = (n_chunks - 1 - t) if reverse else t
    end = 0 if reverse else CHUNK - 1

    @pl.when(t == 0)
    def _():
        s_ref[...] = jnp.zeros_like(s_ref)
        cst_ref[...] = jnp.zeros_like(cst_ref)
        m_ref[...] = jnp.zeros_like(m_ref)

    row = lax.broadcasted_iota(jnp.int32, (CHUNK, 1), 0)
    col = lax.broadcasted_iota(jnp.int32, (1, CHUNK), 1)
    lane = col
    allowed = (col >= row) if reverse else (col <= row)
    tri = allowed.astype(BF16)
    tri_t = ((row >= col) if reverse else (row <= col)).astype(BF16)
    is_dt = lane < GATE_I0
    is_i = jnp.logical_and(lane >= GATE_I0, lane < GATE_F0)
    is_f = jnp.logical_and(lane >= GATE_F0, lane < GATE_END)
    pad_rows = jnp.logical_and(c == 0, row < PAD_FRONT)
    a_coef = -jnp.exp(alog_ref[...])
    expand = expand_ref[...]
    left_half = lane < SSD_HEAD_DIM
    top_half = row < MLSTM_DK
    ones_blk = jnp.ones((CHUNK, MLSTM_DV), BF16)

    def one_sequence(b):
        gr = g_ref[b] + gbias_ref[...]
        lse = _log1p_exp_neg_abs(gr)
        softplus = jnp.maximum(gr, 0.0) + lse
        logsig = jnp.minimum(gr, 0.0) - lse
        val = jnp.where(is_dt, softplus, jnp.where(is_i, gr, jnp.where(is_f, logsig, 0.0)))
        val = jnp.where(pad_rows, jnp.where(is_i, NEG_GATE, 0.0), val)
        u = jnp.where(is_dt, val * a_coef, jnp.where(is_f, val, 0.0))
        val_t = val.T
        u_t = u.T
        cums = _dot_exact_lhs(tri, u)
        cums_t = _dot_exact_rhs(u_t, tri_t)

        cums_end = cums[end:end + 1, :]
        p1 = jnp.exp(cums)
        p2 = jnp.exp(cums_end - cums) * val
        ex1 = _dot(p1.astype(BF16), expand)
        ex2 = _dot(p2.astype(BF16), expand)
        chunk_decay = _dot_exact_rhs(jnp.broadcast_to(jnp.exp(cums_end), (8, LANES)), expand)[0:1, :]
        xs = xs_ref[b]
        xsf = xs.astype(F32)
        xs_w = (xsf * ex2).astype(BF16)
        y_groups = []
        for g in range(SSD_GROUPS):
            cg = cm_ref[b, :, g * SSD_STATE:(g + 1) * SSD_STATE]
            bg = bm_ref[b, :, g * SSD_STATE:(g + 1) * SSD_STATE]
            cb = _dot_nt(cg, bg)
            ys = []
            for pp in range(HEADS_PER_GROUP // 2):
                pair = g * (HEADS_PER_GROUP // 2) + pp
                xs_pair = xs[:, pair * LANES:(pair + 1) * LANES]
                y_pair = None
                for hh in range(2):
                    h = 2 * pair + hh
                    seg = cums[:, h:h + 1] - cums_t[h:h + 1, :]
                    dec = jnp.exp(jnp.where(allowed, seg, -jnp.inf))
                    m_mat = (cb * dec * val_t[h:h + 1, :]).astype(BF16)
                    keep = left_half if hh == 0 else jnp.logical_not(left_half)
                    part = _dot(m_mat, jnp.where(keep, xs_pair, jnp.zeros_like(xs_pair)))
                    y_pair = part if y_pair is None else y_pair + part
                ys.append(y_pair)
            y_diag = jnp.concatenate(ys, axis=1)
            gs = slice(g * GROUP_WIDTH, (g + 1) * GROUP_WIDTH)
            s_g = s_old[b][g]
            y_off = _dot(cg, s_g.astype(BF16)) * ex1[:, gs]
            y_groups.append(y_diag + y_off)
            bg_t = bg.astype(F32).T.astype(BF16)
            s_new_all[b].append(chunk_decay[:, gs] * s_g + _dot(bg_t, xs_w[:, gs]))
        y_ssd = jnp.concatenate(y_groups, axis=1)

        bcum_t = cums_t[GATE_F0:GATE_END, :]
        ip_t = val_t[GATE_I0:GATE_F0, :]
        g_t = bcum_t[:, end:end + 1]
        a_t = g_t - bcum_t + ip_t
        a_max = jnp.max(a_t, axis=1, keepdims=True)
        w_t = jnp.exp(a_t - a_max)
        m_prev = m_old[b][:, 0:1]
        m_new = jnp.maximum(g_t + m_prev, a_max)
        s_prev = jnp.exp(g_t + m_prev - m_new)
        s_new = jnp.exp(a_max - m_new)
        r_t = ip_t - bcum_t
        h_heads = []
        for pair in range(MLSTM_HEADS // 2):
            h0, h1 = 2 * pair, 2 * pair + 1
            q_pair = q_ref[b, :, pair * LANES:(pair + 1) * LANES]
            k_pair = k_ref[b, :, pair * LANES:(pair + 1) * LANES]
            cst = cst_old[b][pair]
            cst_b = cst.astype(BF16)
            v_pair = []
            for hh, h in enumerate((h0, h1)):
                keep = left_half if hh == 0 else jnp.logical_not(left_half)
                qh = jnp.where(keep, q_pair, jnp.zeros_like(q_pair))
                vh = v_ref[b, :, h * MLSTM_DV:(h + 1) * MLSTM_DV]
                v_pair.append(vh)
                qk = _dot_nt(qh, k_pair)
                bc = cums[:, GATE_F0 + h:GATE_F0 + h + 1]
                dlog = jnp.where(allowed, bc + r_t[h:h + 1, :], -jnp.inf)
                m_intra = jnp.max(dlog, axis=1, keepdims=True)
                m_inter = bc + m_prev[h:h + 1, :]
                m_t = jnp.maximum(m_inter, m_intra)
                s_mat = (qk * jnp.exp(dlog - m_t)).astype(BF16)
                intra = _dot(s_mat, jnp.concatenate([vh, ones_blk], axis=1))
                inter = _dot(qh, cst_b)
                tot = jnp.exp(m_inter - m_t) * inter + intra
                num = tot[:, :MLSTM_DV]
                den = tot[:, MLSTM_DV:]
                h_heads.append(num / jnp.maximum(jnp.abs(den), jnp.exp(-m_t)))
            w_rows = jnp.where(top_half, w_t[h0:h0 + 1, :], w_t[h1:h1 + 1, :])
            kw = (k_pair.astype(F32).T * w_rows).astype(BF16)
            full = _dot(kw, jnp.concatenate([v_pair[0], v_pair[1], ones_blk], axis=1))
            kvn = jnp.concatenate(
                [jnp.where(top_half, full[:, :MLSTM_DV], full[:, MLSTM_DV:2 * MLSTM_DV]),
                 full[:, 2 * MLSTM_DV:]], axis=1)
            sp_rows = jnp.where(top_half, s_prev[h0:h0 + 1, :], s_prev[h1:h1 + 1, :])
            sn_rows = jnp.where(top_half, s_new[h0:h0 + 1, :], s_new[h1:h1 + 1, :])
            cst_new_all[b].append(sp_rows * cst + sn_rows * kvn)
        m_new_all[b] = jnp.broadcast_to(m_new, (MLSTM_HEADS, LANES))
        h_ml = jnp.concatenate(h_heads, axis=1)
        if not final:
            return y_ssd.astype(BF16), h_ml.astype(BF16)

        y_tot = y_ssd + yb_ref[b].astype(F32) + dskip_ref[...] * xsf
        zz = z_ref[b].astype(F32)
        y2 = y_tot * (zz * _sigmoid(zz))
        y_n = y2 * lax.rsqrt(jnp.mean(y2 * y2, axis=-1, keepdims=True) + RMS_EPS) * ngs_ref[...]
        h_tot = h_ml + hb_ref[b].astype(F32)
        segs = []
        for h in range(MLSTM_HEADS):
            seg = h_tot[:, h * MLSTM_DV:(h + 1) * MLSTM_DV]
            segs.append(seg * lax.rsqrt(jnp.mean(seg * seg, axis=-1, keepdims=True) + RMS_EPS))
        h_n = jnp.concatenate(segs, axis=1) * ngm_ref[...]
        y_ml = _sigmoid(o_ref[b].astype(F32)) * h_n
        return y_n.astype(BF16), y_ml.astype(BF16)

    s_old = [[s_ref[b, g] for g in range(SSD_GROUPS)] for b in range(n_seq)]
    cst_old = [[cst_ref[b, p] for p in range(MLSTM_HEADS // 2)] for b in range(n_seq)]
    m_old = [m_ref[b] for b in range(n_seq)]
    s_new_all = [[] for _ in range(n_seq)]
    cst_new_all = [[] for _ in range(n_seq)]
    m_new_all = [None] * n_seq
    results = [one_sequence(b) for b in range(n_seq)]
    for b in range(n_seq):
        for g in range(SSD_GROUPS):
            s_ref[b, g] = s_new_all[b][g]
        for p in range(MLSTM_HEADS // 2):
            cst_ref[b, p] = cst_new_all[b][p]
        m_ref[b] = m_new_all[b]

    @pl.when(c > 0)
    def _():
        for b, (first, second) in enumerate(results):
            if final:
                ycat_ref[b, :, :D_MODEL] = first
                ycat_ref[b, :, D_MODEL:] = second
            else:
                yout_ref[b] = first
                hout_ref[b] = second


def _mixer_pass(xs, bm, cm, gates, q, k, v, gbias, alog, expand, n_chunks, *, reverse, final_inputs=None):
    final = final_inputs is not None
    n_seq = xs.shape[0]
    seq_len = (n_chunks - 1) * CHUNK

    def chunk_of(t):
        return (n_chunks - 1 - t) if reverse else t

    def pad_map(t):
        return (0, _storage_chunk(chunk_of(t), n_chunks), 0)

    def out_map(t):
        return (0, jnp.maximum(chunk_of(t) - 1, 0), 0)

    const = lambda a: pl.BlockSpec(a.shape, lambda t: (0,) * a.ndim)
    pad_spec = lambda n: pl.BlockSpec((n_seq, CHUNK, n), pad_map)
    out_spec = lambda n: pl.BlockSpec((n_seq, CHUNK, n), out_map)
    in_arrays = [xs, bm, cm, gates, q, k, v]
    in_specs = [pad_spec(a.shape[2]) for a in in_arrays]
    if final:
        z, o, yb, hb, dskip, ngs, ngm = final_inputs
        in_arrays += [z, o, yb, hb]
        in_specs += [pad_spec(1024), pad_spec(1024), out_spec(1024), out_spec(1024)]
        in_arrays += [gbias, alog, expand, dskip, ngs, ngm]
        in_specs += [const(a) for a in (gbias, alog, expand, dskip, ngs, ngm)]
        out_shape = [jax.ShapeDtypeStruct((n_seq, seq_len, 2 * D_MODEL), BF16)]
        out_specs = [out_spec(2 * D_MODEL)]
    else:
        in_arrays += [gbias, alog, expand]
        in_specs += [const(a) for a in (gbias, alog, expand)]
        out_shape = [jax.ShapeDtypeStruct((n_seq, seq_len, D_MODEL), BF16),
                     jax.ShapeDtypeStruct((n_seq, seq_len, D_MODEL), BF16)]
        out_specs = [out_spec(D_MODEL), out_spec(D_MODEL)]
    return pl.pallas_call(
        functools.partial(_mixer_kernel, reverse=reverse, final=final, n_chunks=n_chunks, n_seq=n_seq),
        grid=(n_chunks,),
        in_specs=in_specs,
        out_specs=out_specs,
        out_shape=out_shape,
        scratch_shapes=[pltpu.VMEM((n_seq, SSD_GROUPS, SSD_STATE, GROUP_WIDTH), F32),
                        pltpu.VMEM((n_seq, MLSTM_HEADS // 2, 2 * MLSTM_DK, 2 * MLSTM_DV), F32),
                        pltpu.VMEM((n_seq, MLSTM_HEADS, LANES), F32)],
        compiler_params=pltpu.CompilerParams(
            dimension_semantics=("arbitrary",), vmem_limit_bytes=VMEM_LIMIT_BYTES),
        name="mixer_fwd" if final else "mixer_bwd",
    )(*in_arrays)


def _epilogue_kernel(x_ref, ycat_ref, wout_ref, lng0_ref, lnb0_ref, lng1_ref, lnb1_ref,
                     wrh_ref, wrl_ref, br_ref, h1_ref, h1p_ref, sel_ref, gate_ref):
    h0 = _layer_norm(x_ref[...], lng0_ref[...], lnb0_ref[...])
    mix = _dot(ycat_ref[...], wout_ref[...])
    h1 = _layer_norm(DEEPNORM_ALPHA * h0 + mix, lng1_ref[...], lnb1_ref[...])
    h1_ref[...] = h1
    h1p_ref[...] = _pack_bf16_pairs(h1)
    hh = h1.astype(BF16)
    hl = (h1 - hh.astype(F32)).astype(BF16)
    wrh = wrh_ref[...]
    logits = _dot(hh, wrh) + _dot(hl, wrh) + _dot(hh, wrl_ref[...]) + br_ref[...]
    lane = lax.broadcasted_iota(jnp.int32, (1, LANES), 1)
    lane_f = lane.astype(F32)
    logits = jnp.where(lane < N_EXPERTS, logits, -jnp.inf)
    work = logits
    sel = jnp.zeros(logits.shape, jnp.bool_)
    top = None
    for _ in range(TOP_K):
        m = jnp.max(work, axis=-1, keepdims=True)
        if top is None:
            top = m
        first = jnp.min(jnp.where(work == m, lane_f, float(LANES)), axis=-1, keepdims=True)
        pick = lane_f == first
        sel = jnp.logical_or(sel, pick)
        work = jnp.where(pick, -jnp.inf, work)
    e = jnp.where(sel, jnp.exp(logits - top), 0.0)
    gate_ref[...] = e / jnp.sum(e, axis=-1, keepdims=True)
    sel_ref[...] = sel.astype(F32)


def _epilogue(x_all, ycat, w_out, lng0, lnb0, lng1, lnb1, wrh, wrl, br):
    rows = x_all.shape[0]
    tm = EPILOGUE_ROWS
    assert rows % tm == 0
    row_spec = lambda n: pl.BlockSpec((tm, n), lambda i: (i, 0))
    const = lambda a: pl.BlockSpec(a.shape, lambda i: (0,) * a.ndim)
    out_shape = [jax.ShapeDtypeStruct((rows, D_MODEL), F32),
                 jax.ShapeDtypeStruct((rows, D_MODEL // 2), jnp.uint32),
                 jax.ShapeDtypeStruct((rows, LANES), F32),
                 jax.ShapeDtypeStruct((rows, LANES), F32)]
    consts = (w_out, lng0, lnb0, lng1, lnb1, wrh, wrl, br)
    return pl.pallas_call(
        _epilogue_kernel,
        grid=(rows // tm,),
        in_specs=[row_spec(D_MODEL), row_spec(2 * D_MODEL)] + [const(a) for a in consts],
        out_specs=[row_spec(D_MODEL), row_spec(D_MODEL // 2), row_spec(LANES), row_spec(LANES)],
        out_shape=out_shape,
        compiler_params=pltpu.CompilerParams(
            dimension_semantics=("arbitrary",), vmem_limit_bytes=VMEM_LIMIT_BYTES),
        name="epilogue",
    )(x_all, ycat, *consts)


def _rank_kernel(sel_ref, gate_ref, lstrict_ref, ucum_ref, dest_ref, gk_ref, stats_ref,
                 base_ref, *, trash_row):
    phase = pl.program_id(0)
    i = pl.program_id(1)
    sel = sel_ref[...]
    colsum = jnp.sum(sel, axis=0, keepdims=True)

    @pl.when(jnp.logical_and(phase == 0, i == 0))
    def _():
        base_ref[...] = jnp.zeros_like(base_ref)

    @pl.when(phase == 0)
    def _():
        base_ref[0:1, :] = base_ref[0:1, :] + colsum

    @pl.when(jnp.logical_and(phase == 1, i == 0))
    def _():
        counts = base_ref[0:1, :]
        padded = jnp.ceil(counts / MOE_BLOCK) * MOE_BLOCK
        pend = _dot_exact_rhs(jnp.broadcast_to(padded, (8, LANES)), ucum_ref[...])[0:1, :]
        stats_ref[0:1, :] = counts
        stats_ref[1:2, :] = pend - padded
        stats_ref[2:3, :] = pend
        stats_ref[3:8, :] = jnp.zeros((5, LANES), F32)
        base_ref[1:2, :] = pend - padded

    @pl.when(phase == 1)
    def _():
        before = _dot(lstrict_ref[...], sel.astype(BF16))
        pos = base_ref[1:2, :] + before
        base_ref[1:2, :] = base_ref[1:2, :] + colsum
        work = jnp.where(sel > 0.0, pos + 1.0, 0.0)
        gates = gate_ref[...]
        for kk in range(TOP_K):
            m = jnp.max(work, axis=-1, keepdims=True)
            pick = jnp.logical_and(work == m, m > 0.0)
            gk_ref[:, kk:kk + 1] = jnp.sum(jnp.where(pick, gates, 0.0), axis=-1, keepdims=True)
            dest_ref[:, kk:kk + 1] = jnp.where(m > 0.0, m - 1.0, float(trash_row)).astype(jnp.int32)
            work = jnp.where(pick, 0.0, work)


def _rank(sel, gates, lstrict, ucum, trash_row):
    rows = sel.shape[0]
    tm = RANK_ROWS
    assert rows % tm == 0
    row_spec = lambda n: pl.BlockSpec((tm, n), lambda p, i: (i, 0))
    out_row_spec = lambda n: pl.BlockSpec((tm, n), lambda p, i: (i * p, 0))
    const = lambda a: pl.BlockSpec(a.shape, lambda p, i: (0,) * a.ndim)
    return pl.pallas_call(
        functools.partial(_rank_kernel, trash_row=trash_row),
        grid=(2, rows // tm),
        in_specs=[row_spec(LANES), row_spec(LANES), const(lstrict), const(ucum)],
        out_specs=[out_row_spec(TOP_K), out_row_spec(TOP_K),
                   pl.BlockSpec((8, LANES), lambda p, i: (0, 0))],
        out_shape=[jax.ShapeDtypeStruct((rows, TOP_K), jnp.int32),
                   jax.ShapeDtypeStruct((rows, TOP_K), F32),
                   jax.ShapeDtypeStruct((8, LANES), F32)],
        scratch_shapes=[pltpu.VMEM((8, LANES), F32)],
        compiler_params=pltpu.CompilerParams(
            dimension_semantics=("arbitrary", "arbitrary"), vmem_limit_bytes=VMEM_LIMIT_BYTES),
        name="rank",
    )(sel, gates, lstrict, ucum)


def _sc_mesh():
    return plsc.VectorSubcoreMesh(core_axis_name="c", subcore_axis_name="s",
                                  num_cores=SC_CORES, num_subcores=SC_SUBCORES)


def _sc_scatter_rows(src, idx, n_out_rows):
    n_src, d = src.shape
    w = SC_WINDOW
    assert idx.shape[0] == TOP_K * n_src and n_src % (w * SC_WORKERS) == 0
    per_worker = n_src // (w * SC_WORKERS)

    @functools.partial(
        pl.kernel, mesh=_sc_mesh(),
        out_type=jax.ShapeDtypeStruct((n_out_rows, d), src.dtype),
        scratch_types=[pltpu.VMEM((TOP_K, w), jnp.int32), pltpu.VMEM((w, d), src.dtype),
                       pltpu.SemaphoreType.DMA],
        name="sc_scatter_rows")
    def body(src_hbm, idx_hbm, out_hbm, idx_v, rows_v, sem):
        wid = lax.axis_index("s") * SC_CORES + lax.axis_index("c")

        @pl.loop(0, per_worker)
        def _(i):
            off = pl.multiple_of((wid * per_worker + i) * w, w)
            pltpu.sync_copy(src_hbm.at[pl.ds(off, w)], rows_v)
            for kk in range(TOP_K):
                pltpu.sync_copy(idx_hbm.at[pl.ds(pl.multiple_of(kk * n_src + off, w), w)], idx_v.at[kk])
            copies = [pltpu.async_copy(rows_v, out_hbm.at[idx_v.at[kk]], sem) for kk in range(TOP_K)]
            for copy in copies:
                copy.wait()

    return body(src, idx)


def _sc_gather_rows(table, idx):
    d = table.shape[1]
    w = SC_WINDOW
    n_win = idx.shape[0] // w
    assert idx.shape[0] % (w * SC_WORKERS) == 0
    per_worker = n_win // SC_WORKERS

    @functools.partial(
        pl.kernel, mesh=_sc_mesh(),
        out_type=jax.ShapeDtypeStruct((idx.shape[0], d), table.dtype),
        scratch_types=[pltpu.VMEM((w,), jnp.int32), pltpu.VMEM((w, d), table.dtype), pltpu.SemaphoreType.DMA],
        name="sc_gather_rows")
    def body(table_hbm, idx_hbm, out_hbm, idx_v, rows_v, sem):
        wid = lax.axis_index("s") * SC_CORES + lax.axis_index("c")

        @pl.loop(0, per_worker)
        def _(i):
            win = wid * per_worker + i
            off = pl.multiple_of(win * w, w)
            pltpu.sync_copy(idx_hbm.at[pl.ds(off, w)], idx_v)
            pltpu.async_copy(table_hbm.at[idx_v], rows_v, sem).wait()
            pltpu.sync_copy(rows_v, out_hbm.at[pl.ds(off, w)])

    return body(table, idx)


def _ffn_kernel(blk_ref, exp_ref, valid_ref, nused_ref, xb_ref, w1_ref, b1_ref, w2_ref, b2_ref, yb_ref,
                w1b_ref, w2b_ref):
    j = pl.program_id(0)
    active = j < nused_ref[0]

    @pl.when(jnp.logical_and(active, jnp.logical_or(j == 0, exp_ref[j] != exp_ref[jnp.maximum(j - 1, 0)])))
    def _():
        w1b_ref[...] = w1_ref[0].astype(BF16)
        w2b_ref[...] = w2_ref[0].astype(BF16)

    @pl.when(active)
    def _():
        row = lax.broadcasted_iota(jnp.int32, (MOE_BLOCK, 1), 0)
        x = jnp.where(row < valid_ref[j], _unpack_bf16_pairs(xb_ref[...]), 0.0).astype(BF16)
        hc = _dot(x, w1b_ref[...]) + b1_ref[0]
        gate = jnp.minimum(hc[:, :D_FF], SWIGLU_LIMIT)
        up = jnp.clip(hc[:, D_FF:], -SWIGLU_LIMIT, SWIGLU_LIMIT)
        act = (up + 1.0) * gate * _sigmoid(SWIGLU_ALPHA * gate)
        yb_ref[...] = _pack_bf16_pairs(_dot(act.astype(BF16), w2b_ref[...]) + b2_ref[0])


def _ffn(blk_idx, blk_exp, blk_valid, n_used, xb, w1, b1, w2, b2, n_blocks):
    bm = MOE_BLOCK
    grid_spec = pltpu.PrefetchScalarGridSpec(
        num_scalar_prefetch=4,
        grid=(n_blocks,),
        in_specs=[pl.BlockSpec((bm, D_MODEL // 2), lambda j, bi, be, bv, nu: (bi[j], 0)),
                  pl.BlockSpec((1, D_MODEL, 2 * D_FF), lambda j, bi, be, bv, nu: (be[j], 0, 0)),
                  pl.BlockSpec((1, 1, 2 * D_FF), lambda j, bi, be, bv, nu: (be[j], 0, 0)),
                  pl.BlockSpec((1, D_FF, D_MODEL), lambda j, bi, be, bv, nu: (be[j], 0, 0)),
                  pl.BlockSpec((1, 1, D_MODEL), lambda j, bi, be, bv, nu: (be[j], 0, 0))],
        out_specs=pl.BlockSpec((bm, D_MODEL // 2), lambda j, bi, be, bv, nu: (bi[j], 0)),
        scratch_shapes=[pltpu.VMEM((D_MODEL, 2 * D_FF), BF16), pltpu.VMEM((D_FF, D_MODEL), BF16)],
    )
    return pl.pallas_call(
        _ffn_kernel,
        grid_spec=grid_spec,
        out_shape=jax.ShapeDtypeStruct(xb.shape, jnp.uint32),
        compiler_params=pltpu.CompilerParams(
            dimension_semantics=("arbitrary",), vmem_limit_bytes=VMEM_LIMIT_BYTES),
        name="expert_ffn",
    )(blk_idx, blk_exp, blk_valid, n_used, xb, w1, b1, w2, b2)


def _combine_kernel(gk_ref, h1_ref, y0_ref, y1_ref, y2_ref, y3_ref, lng_ref, lnb_ref, outa_ref, outb_ref,
                    *, n_tiles_a):
    gk = gk_ref[...]
    ffn = gk[:, 0:1] * _unpack_bf16_pairs(y0_ref[...])
    for kk, y_ref in enumerate((y1_ref, y2_ref, y3_ref), start=1):
        ffn = ffn + gk[:, kk:kk + 1] * _unpack_bf16_pairs(y_ref[...])
    out = _layer_norm(DEEPNORM_ALPHA * h1_ref[...] + ffn, lng_ref[...], lnb_ref[...])
    i = pl.program_id(0)

    @pl.when(i < n_tiles_a)
    def _():
        outa_ref[...] = out

    @pl.when(i >= n_tiles_a)
    def _():
        outb_ref[...] = out


def _combine(gk, h1, ysel, lng, lnb, rows_a):
    rows = h1.shape[0]
    tm = COMBINE_ROWS
    assert rows % tm == 0 and rows_a % tm == 0 and 0 < rows_a < rows
    n_tiles_a = rows_a // tm
    n_tiles = rows // tm
    const = lambda a: pl.BlockSpec(a.shape, lambda i: (0,) * a.ndim)
    ksel = lambda kk: pl.BlockSpec((tm, D_MODEL // 2), lambda i: (kk * n_tiles + i, 0))
    return pl.pallas_call(
        functools.partial(_combine_kernel, n_tiles_a=n_tiles_a),
        grid=(n_tiles,),
        in_specs=[pl.BlockSpec((tm, TOP_K), lambda i: (i, 0)),
                  pl.BlockSpec((tm, D_MODEL), lambda i: (i, 0)),
                  ksel(0), ksel(1), ksel(2), ksel(3),
                  const(lng), const(lnb)],
        out_specs=[pl.BlockSpec((tm, D_MODEL), lambda i: (jnp.minimum(i, n_tiles_a - 1), 0)),
                   pl.BlockSpec((tm, D_MODEL), lambda i: (jnp.maximum(i - n_tiles_a, 0), 0))],
        out_shape=[jax.ShapeDtypeStruct((rows_a, D_MODEL), F32),
                   jax.ShapeDtypeStruct((rows - rows_a, D_MODEL), F32)],
        compiler_params=pltpu.CompilerParams(
            dimension_semantics=("arbitrary",), vmem_limit_bytes=VMEM_LIMIT_BYTES),
        name="combine",
    )(gk, h1, ysel, ysel, ysel, ysel, lng, lnb)


def _row(v, width=None):
    v = v.reshape(1, -1).astype(F32)
    if width is not None and v.shape[1] < width:
        v = jnp.pad(v, ((0, 0), (0, width - v.shape[1])))
    return v


def _encode_all(x_all, rows_a, meta_tokens, ln_emb_g, ln_emb_b, w_in, conv_w, conv_b, dt_bias, a_log,
                d_skip, ssd_norm_g, i_bias, f_bias, mlstm_norm_g, w_out, ln1_g, ln1_b, w_router, b_router,
                w1, b1, w2, b2, ln2_g, ln2_b):
    n_seq, seq_len, _ = x_all.shape
    assert seq_len % CHUNK == 0
    n_chunks = seq_len // CHUNK + 1
    n_tok = n_seq * seq_len

    sizes = (1024, CONV_CH, 2 * SSD_HEADS, 512, 512, 1024, 1024, 2 * MLSTM_HEADS, 2 * MLSTM_HEADS)
    offs = [0]
    for s in sizes:
        offs.append(offs[-1] + s)
    w_z, w_xbc, w_dt, w_q, w_k, w_v, w_o, w_i, w_f = [w_in[:, offs[j]:offs[j + 1]] for j in range(9)]
    w_big = jnp.concatenate([w_z, w_xbc, w_q, w_k * (MLSTM_DK ** -0.5), w_v, w_o], axis=1).astype(BF16)
    zpad = jnp.zeros((D_MODEL, LANES - GATE_END), F32)
    gate_cols = []
    for d in range(2):
        gate_cols += [w_dt[:, d * SSD_HEADS:(d + 1) * SSD_HEADS],
                      w_i[:, d * MLSTM_HEADS:(d + 1) * MLSTM_HEADS],
                      w_f[:, d * MLSTM_HEADS:(d + 1) * MLSTM_HEADS], zpad]
    w_gates = jnp.concatenate(gate_cols, axis=1).astype(BF16)
    gbias = [_row(jnp.concatenate([dt_bias[d], i_bias[d], f_bias[d]]), LANES) for d in range(2)]
    alog = [_row(a_log[d], LANES) for d in range(2)]
    head_of_col = jnp.arange(D_MODEL, dtype=jnp.int32) // SSD_HEAD_DIM
    expand = (jnp.arange(LANES, dtype=jnp.int32)[:, None] == head_of_col[None, :]).astype(BF16)
    dskip = _row(jnp.repeat(d_skip, SSD_HEAD_DIM))
    conv_w8 = jnp.pad(conv_w.astype(F32), ((0, 8 - CONV_W), (0, 0)))
    meta_tile = jnp.pad(meta_tokens.astype(F32), ((PAD_FRONT, INPROJ_ROWS - CHUNK), (0, 0)))

    z, xbc, q, k, v, o, gates_f, gates_b = _inproj(x_all, meta_tile, _row(ln_emb_g), _row(ln_emb_b),
                                                   w_big, w_gates)
    xs, bm, cm = _conv(xbc, conv_w8, _row(conv_b), n_chunks)
    yb, hb = _mixer_pass(xs, bm, cm, gates_b, q, k, v, gbias[1], alog[1], expand, n_chunks, reverse=True)
    (ycat,) = _mixer_pass(xs, bm, cm, gates_f, q, k, v, gbias[0], alog[0], expand, n_chunks,
                          reverse=False,
                          final_inputs=(z, o, yb, hb, dskip, _row(ssd_norm_g), _row(mlstm_norm_g)))

    wr = jnp.pad(w_router.astype(F32), ((0, 0), (0, LANES - N_EXPERTS)))
    wrh = wr.astype(BF16)
    wrl = (wr - wrh.astype(F32)).astype(BF16)
    h1, h1p, sel, gates = _epilogue(x_all.reshape(n_tok, D_MODEL), ycat.reshape(n_tok, 2 * D_MODEL),
                               w_out.astype(BF16), _row(ln_emb_g), _row(ln_emb_b), _row(ln1_g),
                               _row(ln1_b), wrh, wrl, _row(b_router, LANES))

    n_blocks = n_tok * TOP_K // MOE_BLOCK + N_EXPERTS
    trash_row = n_blocks * MOE_BLOCK
    n_buf_rows = trash_row + MOE_BLOCK
    r_i = jnp.arange(RANK_ROWS, dtype=jnp.int32)
    lstrict = (r_i[None, :] < r_i[:, None]).astype(BF16)
    l_i = jnp.arange(LANES, dtype=jnp.int32)
    ucum = (l_i[:, None] <= l_i[None, :]).astype(BF16)
    dest, gk, stats = _rank(sel, gates, lstrict, ucum, trash_row)
    counts = stats[0, :N_EXPERTS].astype(jnp.int32)
    starts = stats[1, :N_EXPERTS].astype(jnp.int32)
    pends = stats[2, :N_EXPERTS].astype(jnp.int32)
    n_used = pends[N_EXPERTS - 1] // MOE_BLOCK
    blk = jnp.minimum(jnp.arange(n_blocks, dtype=jnp.int32), jnp.maximum(n_used - 1, 0))
    blk_exp = jnp.minimum(jnp.sum((pends[None, :] <= (blk * MOE_BLOCK)[:, None]).astype(jnp.int32), axis=1),
                          N_EXPERTS - 1).astype(jnp.int32)

    blk_last = jnp.take(starts + counts, blk_exp)
    blk_valid = jnp.clip(blk_last - blk * MOE_BLOCK, 0, MOE_BLOCK).astype(jnp.int32)
    dest_km = dest.T.reshape(-1)

    xb = _sc_scatter_rows(h1p, dest_km, n_buf_rows)
    yexp = _ffn(blk, blk_exp, blk_valid, n_used.reshape(1), xb, w1,
                b1.reshape(N_EXPERTS, 1, -1), w2, b2.reshape(N_EXPERTS, 1, -1), n_blocks)
    ysel = _sc_gather_rows(yexp, dest_km)
    return _combine(gk, h1, ysel, _row(ln2_g), _row(ln2_b), rows_a)


def kernel(x_prompt, x_sample, meta_tokens, ln_emb_g, ln_emb_b, w_in, conv_w, conv_b, dt_bias, a_log,
           d_skip, ssd_norm_g, i_bias, f_bias, mlstm_norm_g, w_out, ln1_g, ln1_b, w_router, b_router,
           w1, b1, w2, b2, ln2_g, ln2_b):
    assert x_prompt.shape[1:] == x_sample.shape[1:]
    n_p, seq_len, d = x_prompt.shape
    n_s = x_sample.shape[0]
    x_all = jnp.concatenate([x_prompt, x_sample], axis=0).astype(F32)
    y_p, y_s = _encode_all(x_all, n_p * seq_len, meta_tokens, ln_emb_g, ln_emb_b, w_in[0], conv_w[0],
                           conv_b[0], dt_bias[0], a_log[0], d_skip[0], ssd_norm_g[0], i_bias[0], f_bias[0],
                           mlstm_norm_g[0], w_out[0], ln1_g[0], ln1_b[0], w_router[0], b_router[0],
                           w1[0], b1[0], w2[0], b2[0], ln2_g[0], ln2_b[0])
    return (y_p.reshape(n_p, seq_len, d), y_s.reshape(n_s, seq_len, d))
```

```python
import functools

import jax
import jax.numpy as jnp
from jax import lax
from jax.experimental import pallas as pl
from jax.experimental.pallas import tpu as pltpu
from jax.experimental.pallas import tpu_sc as plsc

F32 = jnp.float32
BF16 = jnp.bfloat16

D_MODEL = 1024
N_META = 16
CHUNK = 128
PAD_FRONT = CHUNK - N_META
SSD_HEADS = 16
SSD_HEAD_DIM = 64
SSD_GROUPS = 4
SSD_STATE = 128
HEADS_PER_GROUP = SSD_HEADS // SSD_GROUPS
GROUP_WIDTH = HEADS_PER_GROUP * SSD_HEAD_DIM
CONV_W = 5
CONV_HALF = CONV_W // 2
CONV_CH = D_MODEL + 2 * SSD_GROUPS * SSD_STATE
MLSTM_HEADS = 8
MLSTM_DK = 64
MLSTM_DV = 128
N_EXPERTS = 32
TOP_K = 4
D_FF = D_MODEL
SWIGLU_LIMIT = 7.0
SWIGLU_ALPHA = 1.702
DEEPNORM_ALPHA = 2.0 ** 0.25
LN_EPS = 1e-5
RMS_EPS = 1e-5
NEG_GATE = -1e30

LANES = 128
BF16_SUBLANES = 16
VMEM_LIMIT_BYTES = 56 * 1024 * 1024

GATE_DT0, GATE_I0, GATE_F0, GATE_END = 0, SSD_HEADS, SSD_HEADS + MLSTM_HEADS, SSD_HEADS + 2 * MLSTM_HEADS

INPROJ_ROWS = 512
EPILOGUE_ROWS = 512
RANK_ROWS = 512
COMBINE_ROWS = 512
MOE_BLOCK = 512

SC_CORES = 2
SC_SUBCORES = 16
SC_WORKERS = SC_CORES * SC_SUBCORES
SC_WINDOW = 32


def _dot(a, b):
    return jnp.dot(a, b, preferred_element_type=F32)


def _dot_nt(a, b):
    return lax.dot_general(a, b, (((1,), (1,)), ((), ())), preferred_element_type=F32)


def _split3(x):
    hi = x.astype(BF16)
    r1 = x - hi.astype(F32)
    mid = r1.astype(BF16)
    lo = (r1 - mid.astype(F32)).astype(BF16)
    return hi, mid, lo


def _dot_exact_lhs(a_bf16, x):
    hi, mid, lo = _split3(x)
    return _dot(a_bf16, hi) + _dot(a_bf16, mid) + _dot(a_bf16, lo)


def _dot_exact_rhs(x, b_bf16):
    hi, mid, lo = _split3(x)
    return _dot(hi, b_bf16) + _dot(mid, b_bf16) + _dot(lo, b_bf16)


def _layer_norm(x, g, b):
    mu = jnp.mean(x, axis=-1, keepdims=True)
    xc = x - mu
    var = jnp.mean(xc * xc, axis=-1, keepdims=True)
    return xc * lax.rsqrt(var + LN_EPS) * g + b


def _sigmoid(x):
    return 1.0 / (1.0 + jnp.exp(-x))


def _log1p_exp_neg_abs(x):
    return jnp.log(1.0 + jnp.exp(-jnp.abs(x)))


def _pack_bf16_pairs(x):
    n = x.shape[1] // 2
    bits = lax.bitcast_convert_type(x.astype(BF16).astype(F32), jnp.uint32)
    return (bits[:, :n] >> 16) | bits[:, n:]


def _unpack_bf16_pairs(words):
    lo = lax.bitcast_convert_type(words << 16, F32)
    hi = lax.bitcast_convert_type(words & jnp.uint32(0xFFFF0000), F32)
    return jnp.concatenate([lo, hi], axis=1)


def _storage_chunk(c, n_chunks):
    return jnp.where(c == 0, n_chunks - 1, c - 1)


def _inproj_kernel(xa_ref, xb_ref, meta_ref, g_ref, b_ref, wbig_ref, wg_ref,
                   z_ref, xbc_ref, q_ref, k_ref, v_ref, o_ref, gf_ref, gb_ref, *, n_x_tiles, n_a):
    x = jnp.where(pl.program_id(0) < n_a, xa_ref[...], xb_ref[...])
    x = jnp.where(pl.program_id(1) == n_x_tiles, meta_ref[...], x)
    h = _layer_norm(x, g_ref[...], b_ref[...]).astype(BF16)

    def mm(c0, c1):
        return _dot(h, wbig_ref[:, c0:c1]).astype(BF16)

    z_ref[...] = mm(0, 1024)
    xbc_ref[:, 0:1024] = mm(1024, 2048)
    xbc_ref[:, 1024:2048] = mm(2048, 3072)
    q_ref[...] = mm(3072, 3584)
    k_ref[...] = mm(3584, 4096)
    v_ref[...] = mm(4096, 5120)
    o_ref[...] = mm(5120, 6144)
    gates = _dot(h, wg_ref[...])
    gf_ref[...] = gates[:, :LANES]
    gb_ref[...] = gates[:, LANES:]


def _inproj(x_a, x_b, meta_tile, ln_g, ln_b, w_big, w_gates):
    n_a, seq_len, _ = x_a.shape
    n_seq = n_a + x_b.shape[0]
    tm = INPROJ_ROWS
    assert seq_len % tm == 0 and tm >= CHUNK
    n_x_tiles = seq_len // tm
    rows = seq_len + CHUNK
    row_spec = lambda n: pl.BlockSpec((None, tm, n), lambda b, i: (b, i, 0))
    const = lambda a: pl.BlockSpec(a.shape, lambda b, i: (0,) * a.ndim)
    resident = lambda a: pl.BlockSpec(a.shape, lambda b, i: (0,) * a.ndim, pipeline_mode=pl.Buffered(1))
    widths = (1024, CONV_CH, 512, 512, 1024, 1024)
    out_shapes = [jax.ShapeDtypeStruct((n_seq, rows, w), BF16) for w in widths]
    out_shapes += [jax.ShapeDtypeStruct((n_seq, rows, LANES), F32)] * 2
    return pl.pallas_call(
        functools.partial(_inproj_kernel, n_x_tiles=n_x_tiles, n_a=n_a),
        grid=(n_seq, n_x_tiles + 1),
        in_specs=[pl.BlockSpec((None, tm, D_MODEL), lambda b, i: (
                      jnp.minimum(b, n_a - 1),
                      jnp.where(b < n_a, jnp.minimum(i, n_x_tiles - 1), n_x_tiles - 1), 0)),
                  pl.BlockSpec((None, tm, D_MODEL), lambda b, i: (
                      jnp.maximum(b - n_a, 0),
                      jnp.where(b < n_a, 0, jnp.minimum(i, n_x_tiles - 1)), 0)),
                  resident(meta_tile), const(ln_g), const(ln_b), resident(w_big), resident(w_gates)],
        out_specs=[row_spec(s.shape[2]) for s in out_shapes],
        out_shape=out_shapes,
        compiler_params=pltpu.CompilerParams(
            dimension_semantics=("arbitrary", "arbitrary"), vmem_limit_bytes=VMEM_LIMIT_BYTES),
        name="inproj",
    )(x_a, x_b, meta_tile, ln_g, ln_b, w_big, w_gates)


def _conv_kernel(prev_ref, main_ref, next_ref, shift_ref, w_ref, b_ref, xs_ref, bm_ref, cm_ref, *, n_chunks):
    c = pl.program_id(1)
    row = lax.broadcasted_iota(jnp.int32, (CHUNK, 1), 0)
    pad_rows = jnp.logical_and(c == 0, row < PAD_FRONT)
    main = jnp.where(pad_rows, 0.0, main_ref[...].astype(F32))
    prev = jnp.where(c == 0, 0.0, prev_ref[...].astype(F32))
    nxt = jnp.where(c == n_chunks - 1, 0.0, next_ref[...].astype(F32))
    xp = jnp.concatenate([prev, main, nxt], axis=0).astype(BF16)
    w = w_ref[...]
    acc = b_ref[...] + main * w[CONV_HALF:CONV_HALF + 1, :]
    shifted = _dot(shift_ref[...], xp)
    for j, t in enumerate(t for t in range(CONV_W) if t != CONV_HALF):
        acc = acc + shifted[j * CHUNK:(j + 1) * CHUNK, :] * w[t:t + 1, :]
    y = acc * _sigmoid(acc)
    y = jnp.where(pad_rows, 0.0, y)
    xs_ref[...] = y[:, :D_MODEL].astype(BF16)
    bm_ref[...] = y[:, D_MODEL:D_MODEL + 512].astype(BF16)
    cm_ref[...] = y[:, D_MODEL + 512:].astype(BF16)


def _conv(xbc, conv_w8, conv_b, n_chunks):
    n_seq, rows, _ = xbc.shape
    halo_blocks_per_chunk = CHUNK // BF16_SUBLANES

    def main_map(b, c):
        return (b, _storage_chunk(c, n_chunks), 0)

    def prev_map(b, c):
        sc = _storage_chunk(jnp.maximum(c - 1, 0), n_chunks)
        return (b, sc * halo_blocks_per_chunk + halo_blocks_per_chunk - 1, 0)

    def next_map(b, c):
        sc = _storage_chunk(jnp.minimum(c + 1, n_chunks - 1), n_chunks)
        return (b, sc * halo_blocks_per_chunk, 0)

    const = lambda a: pl.BlockSpec(a.shape, lambda b, c: (0,) * a.ndim)
    l_i = jnp.arange(CHUNK, dtype=jnp.int32)[:, None]
    j_i = jnp.arange(CHUNK + 2 * BF16_SUBLANES, dtype=jnp.int32)[None, :]
    shifts = jnp.concatenate([(j_i == BF16_SUBLANES + l_i + t - CONV_HALF)
                              for t in range(CONV_W) if t != CONV_HALF], axis=0).astype(BF16)
    out_shapes = [jax.ShapeDtypeStruct((n_seq, rows, D_MODEL), BF16),
                  jax.ShapeDtypeStruct((n_seq, rows, 512), BF16),
                  jax.ShapeDtypeStruct((n_seq, rows, 512), BF16)]
    return pl.pallas_call(
        functools.partial(_conv_kernel, n_chunks=n_chunks),
        grid=(n_seq, n_chunks),
        in_specs=[pl.BlockSpec((None, BF16_SUBLANES, CONV_CH), prev_map),
                  pl.BlockSpec((None, CHUNK, CONV_CH), main_map),
                  pl.BlockSpec((None, BF16_SUBLANES, CONV_CH), next_map),
                  const(shifts), const(conv_w8), const(conv_b)],
        out_specs=[pl.BlockSpec((None, CHUNK, s.shape[2]), main_map) for s in out_shapes],
        out_shape=out_shapes,
        compiler_params=pltpu.CompilerParams(
            dimension_semantics=("arbitrary", "arbitrary"), vmem_limit_bytes=VMEM_LIMIT_BYTES),
        name="conv",
    )(xbc, xbc, xbc, shifts, conv_w8, conv_b)


def _mixer_kernel(*refs, reverse, final, n_chunks, n_seq):
    if final:
        (xs_ref, bm_ref, cm_ref, g_ref, q_ref, k_ref, v_ref, z_ref, o_ref, yb_ref, hb_ref,
         gbias_ref, alog_ref, expand_ref, dskip_ref, ngs_ref, ngm_ref,
         ycat_ref, s_ref, cst_ref, m_ref) = refs
    else:
        (xs_ref, bm_ref, cm_ref, g_ref, q_ref, k_ref, v_ref,
         gbias_ref, alog_ref, expand_ref,
         yout_ref, hout_ref, s_ref, cst_ref, m_ref) = refs

    t = pl.program_id(0)
    c = (n_chunks - 1 - t) if reverse else t
    end = 0 if reverse else CHUNK - 1

    @pl.when(t == 0)
    def _():
        s_ref[...] = jnp.zeros_like(s_ref)
        cst_ref[...] = jnp.zeros_like(cst_ref)
        m_ref[...] = jnp.zeros_like(m_ref)

    row = lax.broadcasted_iota(jnp.int32, (CHUNK, 1), 0)
    col = lax.broadcasted_iota(jnp.int32, (1, CHUNK), 1)
    lane = col
    allowed = (col >= row) if reverse else (col <= row)
    tri = allowed.astype(BF16)
    tri_t = ((row >= col) if reverse else (row <= col)).astype(BF16)
    is_dt = lane < GATE_I0
    is_i = jnp.logical_and(lane >= GATE_I0, lane < GATE_F0)
    is_f = jnp.logical_and(lane >= GATE_F0, lane < GATE_END)
    pad_rows = jnp.logical_and(c == 0, row < PAD_FRONT)
    a_coef = -jnp.exp(alog_ref[...])
    expand = expand_ref[...]
    left_half = lane < SSD_HEAD_DIM
    top_half = row < MLSTM_DK
    ones_blk = jnp.ones((CHUNK, MLSTM_DV), BF16)

    def one_sequence(b):
        gr = g_ref[b] + gbias_ref[...]
        lse = _log1p_exp_neg_abs(gr)
        softplus = jnp.maximum(gr, 0.0) + lse
        logsig = jnp.minimum(gr, 0.0) - lse
        val = jnp.where(is_dt, softplus, jnp.where(is_i, gr, jnp.where(is_f, logsig, 0.0)))
        val = jnp.where(pad_rows, jnp.where(is_i, NEG_GATE, 0.0), val)
        u = jnp.where(is_dt, val * a_coef, jnp.where(is_f, val, 0.0))
        val_t = val.T
        u_t = u.T
        cums = _dot_exact_lhs(tri, u)
        cums_t = _dot_exact_rhs(u_t, tri_t)

        cums_end = cums[end:end + 1, :]
        p1 = jnp.exp(cums)
        p2 = jnp.exp(cums_end - cums) * val
        ex1 = _dot(p1.astype(BF16), expand)
        ex2 = _dot(p2.astype(BF16), expand)
        chunk_decay = _dot_exact_rhs(jnp.broadcast_to(jnp.exp(cums_end), (8, LANES)), expand)[0:1, :]
        xs = xs_ref[b]
        xsf = xs.astype(F32)
        xs_w = (xsf * ex2).astype(BF16)
        y_groups = []
        for g in range(SSD_GROUPS):
            cg = cm_ref[b, :, g * SSD_STATE:(g + 1) * SSD_STATE]
            bg = bm_ref[b, :, g * SSD_STATE:(g + 1) * SSD_STATE]
            cb = _dot_nt(cg, bg)
            ys = []
            for pp in range(HEADS_PER_GROUP // 2):
                pair = g * (HEADS_PER_GROUP // 2) + pp
                xs_pair = xs[:, pair * LANES:(pair + 1) * LANES]
                y_pair = None
                for hh in range(2):
                    h = 2 * pair + hh
                    seg = cums[:, h:h + 1] - cums_t[h:h + 1, :]
                    dec = jnp.exp(jnp.where(allowed, seg, -jnp.inf))
                    m_mat = (cb * dec * val_t[h:h + 1, :]).astype(BF16)
                    keep = left_half if hh == 0 else jnp.logical_not(left_half)
                    part = _dot(m_mat, jnp.where(keep, xs_pair, jnp.zeros_like(xs_pair)))
                    y_pair = part if y_pair is None else y_pair + part
                ys.append(y_pair)
            y_diag = jnp.concatenate(ys, axis=1)
            gs = slice(g * GROUP_WIDTH, (g + 1) * GROUP_WIDTH)
            s_g = s_old[b][g]
            y_off = _dot(cg, s_g.astype(BF16)) * ex1[:, gs]
            y_groups.append(y_diag + y_off)
            bg_t = bg.astype(F32).T.astype(BF16)
            s_new_all[b].append(chunk_decay[:, gs] * s_g + _dot(bg_t, xs_w[:, gs]))
        y_ssd = jnp.concatenate(y_groups, axis=1)

        bcum_t = cums_t[GATE_F0:GATE_END, :]
        ip_t = val_t[GATE_I0:GATE_F0, :]
        g_t = bcum_t[:, end:end + 1]
        a_t = g_t - bcum_t + ip_t
        a_max = jnp.max(a_t, axis=1, keepdims=True)
        w_t = jnp.exp(a_t - a_max)
        m_prev = m_old[b][:, 0:1]
        m_new = jnp.maximum(g_t + m_prev, a_max)
        s_prev = jnp.exp(g_t + m_prev - m_new)
        s_new = jnp.exp(a_max - m_new)
        r_t = ip_t - bcum_t
        h_heads = []
        for pair in range(MLSTM_HEADS // 2):
            h0, h1 = 2 * pair, 2 * pair + 1
            q_pair = q_ref[b, :, pair * LANES:(pair + 1) * LANES]
            k_pair = k_ref[b, :, pair * LANES:(pair + 1) * LANES]
            cst = cst_old[b][pair]
            cst_b = cst.astype(BF16)
            v_pair = []
            for hh, h in enumerate((h0, h1)):
                keep = left_half if hh == 0 else jnp.logical_not(left_half)
                qh = jnp.where(keep, q_pair, jnp.zeros_like(q_pair))
                vh = v_ref[b, :, h * MLSTM_DV:(h + 1) * MLSTM_DV]
                v_pair.append(vh)
                qk = _dot_nt(qh, k_pair)
                bc = cums[:, GATE_F0 + h:GATE_F0 + h + 1]
                dlog = jnp.where(allowed, bc + r_t[h:h + 1, :], -jnp.inf)
                m_intra = jnp.max(dlog, axis=1, keepdims=True)
                m_inter = bc + m_prev[h:h + 1, :]
                m_t = jnp.maximum(m_inter, m_intra)
                s_mat = (qk * jnp.exp(dlog - m_t)).astype(BF16)
                intra = _dot(s_mat, jnp.concatenate([vh, ones_blk], axis=1))
                inter = _dot(qh, cst_b)
                tot = jnp.exp(m_inter - m_t) * inter + intra
                num = tot[:, :MLSTM_DV]
                den = tot[:, MLSTM_DV:]
                h_heads.append(num / jnp.maximum(jnp.abs(den), jnp.exp(-m_t)))
            w_rows = jnp.where(top_half, w_t[h0:h0 + 1, :], w_t[h1:h1 + 1, :])
            kw = (k_pair.astype(F32).T * w_rows).astype(BF16)
            full = _dot(kw, jnp.concatenate([v_pair[0], v_pair[1], ones_blk], axis=1))
            kvn = jnp.concatenate(
                [jnp.where(top_half, full[:, :MLSTM_DV], full[:, MLSTM_DV:2 * MLSTM_DV]),
                 full[:, 2 * MLSTM_DV:]], axis=1)
            sp_rows = jnp.where(top_half, s_prev[h0:h0 + 1, :], s_prev[h1:h1 + 1, :])
            sn_rows = jnp.where(top_half, s_new[h0:h0 + 1, :], s_new[h1:h1 + 1, :])
            cst_new_all[b].append(sp_rows * cst + sn_rows * kvn)
        m_new_all[b] = jnp.broadcast_to(m_new, (MLSTM_HEADS, LANES))
        h_ml = jnp.concatenate(h_heads, axis=1)
        if not final:
            return y_ssd.astype(BF16), h_ml.astype(BF16)

        y_tot = y_ssd + yb_ref[b].astype(F32) + dskip_ref[...] * xsf
        zz = z_ref[b].astype(F32)
        y2 = y_tot * (zz * _sigmoid(zz))
        y_n = y2 * lax.rsqrt(jnp.mean(y2 * y2, axis=-1, keepdims=True) + RMS_EPS) * ngs_ref[...]
        h_tot = h_ml + hb_ref[b].astype(F32)
        segs = []
        for h in range(MLSTM_HEADS):
            seg = h_tot[:, h * MLSTM_DV:(h + 1) * MLSTM_DV]
            segs.append(seg * lax.rsqrt(jnp.mean(seg * seg, axis=-1, keepdims=True) + RMS_EPS))
        h_n = jnp.concatenate(segs, axis=1) * ngm_ref[...]
        y_ml = _sigmoid(o_ref[b].astype(F32)) * h_n
        return y_n.astype(BF16), y_ml.astype(BF16)

    s_old = [[s_ref[b, g] for g in range(SSD_GROUPS)] for b in range(n_seq)]
    cst_old = [[cst_ref[b, p] for p in range(MLSTM_HEADS // 2)] for b in range(n_seq)]
    m_old = [m_ref[b] for b in range(n_seq)]
    s_new_all = [[] for _ in range(n_seq)]
    cst_new_all = [[] for _ in range(n_seq)]
    m_new_all = [None] * n_seq
    results = [one_sequence(b) for b in range(n_seq)]
    for b in range(n_seq):
        for g in range(SSD_GROUPS):
            s_ref[b, g] = s_new_all[b][g]
        for p in range(MLSTM_HEADS // 2):
            cst_ref[b, p] = cst_new_all[b][p]
        m_ref[b] = m_new_all[b]

    @pl.when(c > 0)
    def _():
        for b, (first, second) in enumerate(results):
            if final:
                ycat_ref[b, :, :D_MODEL] = first
                ycat_ref[b, :, D_MODEL:] = second
            else:
                yout_ref[b] = first
                hout_ref[b] = second


def _mixer_pass(xs, bm, cm, gates, q, k, v, gbias, alog, expand, n_chunks, *, reverse, final_inputs=None):
    final = final_inputs is not None
    n_seq = xs.shape[0]
    seq_len = (n_chunks - 1) * CHUNK

    def chunk_of(t):
        return (n_chunks - 1 - t) if reverse else t

    def pad_map(t):
        return (0, _storage_chunk(chunk_of(t), n_chunks), 0)

    def out_map(t):
        return (0, jnp.maximum(chunk_of(t) - 1, 0), 0)

    const = lambda a: pl.BlockSpec(a.shape, lambda t: (0,) * a.ndim)
    pad_spec = lambda n: pl.BlockSpec((n_seq, CHUNK, n), pad_map)
    out_spec = lambda n: pl.BlockSpec((n_seq, CHUNK, n), out_map)
    in_arrays = [xs, bm, cm, gates, q, k, v]
    in_specs = [pad_spec(a.shape[2]) for a in in_arrays]
    if final:
        z, o, yb, hb, dskip, ngs, ngm = final_inputs
        in_arrays += [z, o, yb, hb]
        in_specs += [pad_spec(1024), pad_spec(1024), out_spec(1024), out_spec(1024)]
        in_arrays += [gbias, alog, expand, dskip, ngs, ngm]
        in_specs += [const(a) for a in (gbias, alog, expand, dskip, ngs, ngm)]
        out_shape = [jax.ShapeDtypeStruct((n_seq, seq_len, 2 * D_MODEL), BF16)]
        out_specs = [out_spec(2 * D_MODEL)]
    else:
        in_arrays += [gbias, alog, expand]
        in_specs += [const(a) for a in (gbias, alog, expand)]
        out_shape = [jax.ShapeDtypeStruct((n_seq, seq_len, D_MODEL), BF16),
                     jax.ShapeDtypeStruct((n_seq, seq_len, D_MODEL), BF16)]
        out_specs = [out_spec(D_MODEL), out_spec(D_MODEL)]
    return pl.pallas_call(
        functools.partial(_mixer_kernel, reverse=reverse, final=final, n_chunks=n_chunks, n_seq=n_seq),
        grid=(n_chunks,),
        in_specs=in_specs,
        out_specs=out_specs,
        out_shape=out_shape,
        scratch_shapes=[pltpu.VMEM((n_seq, SSD_GROUPS, SSD_STATE, GROUP_WIDTH), F32),
                        pltpu.VMEM((n_seq, MLSTM_HEADS // 2, 2 * MLSTM_DK, 2 * MLSTM_DV), F32),
                        pltpu.VMEM((n_seq, MLSTM_HEADS, LANES), F32)],
        compiler_params=pltpu.CompilerParams(
            dimension_semantics=("arbitrary",), vmem_limit_bytes=VMEM_LIMIT_BYTES),
        name="mixer_fwd" if final else "mixer_bwd",
    )(*in_arrays)


def _epilogue_kernel(xa_ref, xb_ref, ycat_ref, wout_ref, lng0_ref, lnb0_ref, lng1_ref, lnb1_ref,
                     wrh_ref, wrl_ref, br_ref, h1_ref, h1p_ref, sel_ref, gate_ref, *, n_tiles_a):
    x = jnp.where(pl.program_id(0) < n_tiles_a, xa_ref[...], xb_ref[...])
    h0 = _layer_norm(x, lng0_ref[...], lnb0_ref[...])
    mix = _dot(ycat_ref[...], wout_ref[...])
    h1 = _layer_norm(DEEPNORM_ALPHA * h0 + mix, lng1_ref[...], lnb1_ref[...])
    h1_ref[...] = h1
    h1p_ref[...] = _pack_bf16_pairs(h1)
    hh = h1.astype(BF16)
    hl = (h1 - hh.astype(F32)).astype(BF16)
    wrh = wrh_ref[...]
    logits = _dot(hh, wrh) + _dot(hl, wrh) + _dot(hh, wrl_ref[...]) + br_ref[...]
    lane = lax.broadcasted_iota(jnp.int32, (1, LANES), 1)
    lane_f = lane.astype(F32)
    logits = jnp.where(lane < N_EXPERTS, logits, -jnp.inf)
    work = logits
    sel = jnp.zeros(logits.shape, jnp.bool_)
    top = None
    for _ in range(TOP_K):
        m = jnp.max(work, axis=-1, keepdims=True)
        if top is None:
            top = m
        first = jnp.min(jnp.where(work == m, lane_f, float(LANES)), axis=-1, keepdims=True)
        pick = lane_f == first
        sel = jnp.logical_or(sel, pick)
        work = jnp.where(pick, -jnp.inf, work)
    e = jnp.where(sel, jnp.exp(logits - top), 0.0)
    gate_ref[...] = e / jnp.sum(e, axis=-1, keepdims=True)
    sel_ref[...] = sel.astype(F32)


def _epilogue(x_a, x_b, ycat, w_out, lng0, lnb0, lng1, lnb1, wrh, wrl, br):
    rows = x_a.shape[0] + x_b.shape[0]
    tm = EPILOGUE_ROWS
    assert x_a.shape[0] % tm == 0 and x_b.shape[0] % tm == 0
    n_tiles_a = x_a.shape[0] // tm
    row_spec = lambda n: pl.BlockSpec((tm, n), lambda i: (i, 0))
    const = lambda a: pl.BlockSpec(a.shape, lambda i: (0,) * a.ndim)
    out_shape = [jax.ShapeDtypeStruct((rows, D_MODEL), F32),
                 jax.ShapeDtypeStruct((rows, D_MODEL // 2), jnp.uint32),
                 jax.ShapeDtypeStruct((rows, LANES), F32),
                 jax.ShapeDtypeStruct((rows, LANES), F32)]
    consts = (w_out, lng0, lnb0, lng1, lnb1, wrh, wrl, br)
    return pl.pallas_call(
        functools.partial(_epilogue_kernel, n_tiles_a=n_tiles_a),
        grid=(rows // tm,),
        in_specs=[pl.BlockSpec((tm, D_MODEL), lambda i: (jnp.minimum(i, n_tiles_a - 1), 0)),
                  pl.BlockSpec((tm, D_MODEL), lambda i: (jnp.maximum(i - n_tiles_a, 0), 0)),
                  row_spec(2 * D_MODEL)] + [const(a) for a in consts],
        out_specs=[row_spec(D_MODEL), row_spec(D_MODEL // 2), row_spec(LANES), row_spec(LANES)],
        out_shape=out_shape,
        compiler_params=pltpu.CompilerParams(
            dimension_semantics=("arbitrary",), vmem_limit_bytes=VMEM_LIMIT_BYTES),
        name="epilogue",
    )(x_a, x_b, ycat, *consts)


def _rank_kernel(sel_ref, gate_ref, lstrict_ref, ucum_ref, dest_ref, gk_ref, stats_ref,
                 base_ref, *, trash_row):
    phase = pl.program_id(0)
    i = pl.program_id(1)
    sel = sel_ref[...]
    colsum = jnp.sum(sel, axis=0, keepdims=True)

    @pl.when(jnp.logical_and(phase == 0, i == 0))
    def _():
        base_ref[...] = jnp.zeros_like(base_ref)

    @pl.when(phase == 0)
    def _():
        base_ref[0:1, :] = base_ref[0:1, :] + colsum

    @pl.when(jnp.logical_and(phase == 1, i == 0))
    def _():
        counts = base_ref[0:1, :]
        padded = jnp.ceil(counts / MOE_BLOCK) * MOE_BLOCK
        pend = _dot_exact_rhs(jnp.broadcast_to(padded, (8, LANES)), ucum_ref[...])[0:1, :]
        stats_ref[0:1, :] = counts
        stats_ref[1:2, :] = pend - padded
        stats_ref[2:3, :] = pend
        stats_ref[3:8, :] = jnp.zeros((5, LANES), F32)
        base_ref[1:2, :] = pend - padded

    @pl.when(phase == 1)
    def _():
        before = _dot(lstrict_ref[...], sel.astype(BF16))
        pos = base_ref[1:2, :] + before
        base_ref[1:2, :] = base_ref[1:2, :] + colsum
        work = jnp.where(sel > 0.0, pos + 1.0, 0.0)
        gates = gate_ref[...]
        for kk in range(TOP_K):
            m = jnp.max(work, axis=-1, keepdims=True)
            pick = jnp.logical_and(work == m, m > 0.0)
            gk_ref[:, kk:kk + 1] = jnp.sum(jnp.where(pick, gates, 0.0), axis=-1, keepdims=True)
            dest_ref[:, kk:kk + 1] = jnp.where(m > 0.0, m - 1.0, float(trash_row)).astype(jnp.int32)
            work = jnp.where(pick, 0.0, work)


def _rank(sel, gates, lstrict, ucum, trash_row):
    rows = sel.shape[0]
    tm = RANK_ROWS
    assert rows % tm == 0
    row_spec = lambda n: pl.BlockSpec((tm, n), lambda p, i: (i, 0))
    out_row_spec = lambda n: pl.BlockSpec((tm, n), lambda p, i: (i * p, 0))
    const = lambda a: pl.BlockSpec(a.shape, lambda p, i: (0,) * a.ndim)
    return pl.pallas_call(
        functools.partial(_rank_kernel, trash_row=trash_row),
        grid=(2, rows // tm),
        in_specs=[row_spec(LANES), row_spec(LANES), const(lstrict), const(ucum)],
        out_specs=[out_row_spec(TOP_K), out_row_spec(TOP_K),
                   pl.BlockSpec((8, LANES), lambda p, i: (0, 0))],
        out_shape=[jax.ShapeDtypeStruct((rows, TOP_K), jnp.int32),
                   jax.ShapeDtypeStruct((rows, TOP_K), F32),
                   jax.ShapeDtypeStruct((8, LANES), F32)],
        scratch_shapes=[pltpu.VMEM((8, LANES), F32)],
        compiler_params=pltpu.CompilerParams(
            dimension_semantics=("arbitrary", "arbitrary"), vmem_limit_bytes=VMEM_LIMIT_BYTES),
        name="rank",
    )(sel, gates, lstrict, ucum)


def _sc_mesh():
    return plsc.VectorSubcoreMesh(core_axis_name="c", subcore_axis_name="s",
                                  num_cores=SC_CORES, num_subcores=SC_SUBCORES)


def _sc_scatter_rows(src, idx, n_out_rows):
    n_src, d = src.shape
    w = SC_WINDOW
    assert idx.shape[0] == TOP_K * n_src and n_src % (w * SC_WORKERS) == 0
    per_worker = n_src // (w * SC_WORKERS)

    @functools.partial(
        pl.kernel, mesh=_sc_mesh(),
        out_type=jax.ShapeDtypeStruct((n_out_rows, d), src.dtype),
        scratch_types=[pltpu.VMEM((TOP_K, w), jnp.int32), pltpu.VMEM((w, d), src.dtype),
                       pltpu.SemaphoreType.DMA],
        name="sc_scatter_rows")
    def body(src_hbm, idx_hbm, out_hbm, idx_v, rows_v, sem):
        wid = lax.axis_index("s") * SC_CORES + lax.axis_index("c")

        @pl.loop(0, per_worker)
        def _(i):
            off = pl.multiple_of((wid * per_worker + i) * w, w)
            pltpu.sync_copy(src_hbm.at[pl.ds(off, w)], rows_v)
            for kk in range(TOP_K):
                pltpu.sync_copy(idx_hbm.at[pl.ds(pl.multiple_of(kk * n_src + off, w), w)], idx_v.at[kk])
            copies = [pltpu.async_copy(rows_v, out_hbm.at[idx_v.at[kk]], sem) for kk in range(TOP_K)]
            for copy in copies:
                copy.wait()

    return body(src, idx)


def _sc_gather_rows(table, idx):
    d = table.shape[1]
    w = SC_WINDOW
    n_win = idx.shape[0] // w
    assert idx.shape[0] % (w * SC_WORKERS) == 0
    per_worker = n_win // SC_WORKERS

    @functools.partial(
        pl.kernel, mesh=_sc_mesh(),
        out_type=jax.ShapeDtypeStruct((idx.shape[0], d), table.dtype),
        scratch_types=[pltpu.VMEM((w,), jnp.int32), pltpu.VMEM((w, d), table.dtype), pltpu.SemaphoreType.DMA],
        name="sc_gather_rows")
    def body(table_hbm, idx_hbm, out_hbm, idx_v, rows_v, sem):
        wid = lax.axis_index("s") * SC_CORES + lax.axis_index("c")

        @pl.loop(0, per_worker)
        def _(i):
            win = wid * per_worker + i
            off = pl.multiple_of(win * w, w)
            pltpu.sync_copy(idx_hbm.at[pl.ds(off, w)], idx_v)
            pltpu.async_copy(table_hbm.at[idx_v], rows_v, sem).wait()
            pltpu.sync_copy(rows_v, out_hbm.at[pl.ds(off, w)])

    return body(table, idx)


def _ffn_kernel(blk_ref, exp_ref, valid_ref, nused_ref, xb_ref, w1_ref, b1_ref, w2_ref, b2_ref, yb_ref,
                w1b_ref, w2b_ref):
    j = pl.program_id(0)
    active = j < nused_ref[0]

    @pl.when(jnp.logical_and(active, jnp.logical_or(j == 0, exp_ref[j] != exp_ref[jnp.maximum(j - 1, 0)])))
    def _():
        w1b_ref[...] = w1_ref[0].astype(BF16)
        w2b_ref[...] = w2_ref[0].astype(BF16)

    @pl.when(active)
    def _():
        row = lax.broadcasted_iota(jnp.int32, (MOE_BLOCK, 1), 0)
        x = jnp.where(row < valid_ref[j], _unpack_bf16_pairs(xb_ref[...]), 0.0).astype(BF16)
        hc = _dot(x, w1b_ref[...]) + b1_ref[0]
        gate = jnp.minimum(hc[:, :D_FF], SWIGLU_LIMIT)
        up = jnp.clip(hc[:, D_FF:], -SWIGLU_LIMIT, SWIGLU_LIMIT)
        act = (up + 1.0) * gate * _sigmoid(SWIGLU_ALPHA * gate)
        yb_ref[...] = _pack_bf16_pairs(_dot(act.astype(BF16), w2b_ref[...]) + b2_ref[0])


def _ffn(blk_idx, blk_exp, blk_valid, n_used, xb, w1, b1, w2, b2, n_blocks):
    bm = MOE_BLOCK
    grid_spec = pltpu.PrefetchScalarGridSpec(
        num_scalar_prefetch=4,
        grid=(n_blocks,),
        in_specs=[pl.BlockSpec((bm, D_MODEL // 2), lambda j, bi, be, bv, nu: (bi[j], 0)),
                  pl.BlockSpec((1, D_MODEL, 2 * D_FF), lambda j, bi, be, bv, nu: (be[j], 0, 0)),
                  pl.BlockSpec((1, 1, 2 * D_FF), lambda j, bi, be, bv, nu: (be[j], 0, 0)),
                  pl.BlockSpec((1, D_FF, D_MODEL), lambda j, bi, be, bv, nu: (be[j], 0, 0)),
                  pl.BlockSpec((1, 1, D_MODEL), lambda j, bi, be, bv, nu: (be[j], 0, 0))],
        out_specs=pl.BlockSpec((bm, D_MODEL // 2), lambda j, bi, be, bv, nu: (bi[j], 0)),
        scratch_shapes=[pltpu.VMEM((D_MODEL, 2 * D_FF), BF16), pltpu.VMEM((D_FF, D_MODEL), BF16)],
    )
    return pl.pallas_call(
        _ffn_kernel,
        grid_spec=grid_spec,
        out_shape=jax.ShapeDtypeStruct(xb.shape, jnp.uint32),
        compiler_params=pltpu.CompilerParams(
            dimension_semantics=("arbitrary",), vmem_limit_bytes=VMEM_LIMIT_BYTES),
        name="expert_ffn",
    )(blk_idx, blk_exp, blk_valid, n_used, xb, w1, b1, w2, b2)


def _combine_kernel(gk_ref, h1_ref, y0_ref, y1_ref, y2_ref, y3_ref, lng_ref, lnb_ref, outa_ref, outb_ref,
                    *, n_tiles_a):
    gk = gk_ref[...]
    ffn = gk[:, 0:1] * _unpack_bf16_pairs(y0_ref[...])
    for kk, y_ref in enumerate((y1_ref, y2_ref, y3_ref), start=1):
        ffn = ffn + gk[:, kk:kk + 1] * _unpack_bf16_pairs(y_ref[...])
    out = _layer_norm(DEEPNORM_ALPHA * h1_ref[...] + ffn, lng_ref[...], lnb_ref[...])
    i = pl.program_id(0)

    @pl.when(i < n_tiles_a)
    def _():
        outa_ref[...] = out

    @pl.when(i >= n_tiles_a)
    def _():
        outb_ref[...] = out


def _combine(gk, h1, ysel, lng, lnb, rows_a):
    rows = h1.shape[0]
    tm = COMBINE_ROWS
    assert rows % tm == 0 and rows_a % tm == 0 and 0 < rows_a < rows
    n_tiles_a = rows_a // tm
    n_tiles = rows // tm
    const = lambda a: pl.BlockSpec(a.shape, lambda i: (0,) * a.ndim)
    ksel = lambda kk: pl.BlockSpec((tm, D_MODEL // 2), lambda i: (kk * n_tiles + i, 0))
    return pl.pallas_call(
        functools.partial(_combine_kernel, n_tiles_a=n_tiles_a),
        grid=(n_tiles,),
        in_specs=[pl.BlockSpec((tm, TOP_K), lambda i: (i, 0)),
                  pl.BlockSpec((tm, D_MODEL), lambda i: (i, 0)),
                  ksel(0), ksel(1), ksel(2), ksel(3),
                  const(lng), const(lnb)],
        out_specs=[pl.BlockSpec((tm, D_MODEL), lambda i: (jnp.minimum(i, n_tiles_a - 1), 0)),
                   pl.BlockSpec((tm, D_MODEL), lambda i: (jnp.maximum(i - n_tiles_a, 0), 0))],
        out_shape=[jax.ShapeDtypeStruct((rows_a, D_MODEL), F32),
                   jax.ShapeDtypeStruct((rows - rows_a, D_MODEL), F32)],
        compiler_params=pltpu.CompilerParams(
            dimension_semantics=("arbitrary",), vmem_limit_bytes=VMEM_LIMIT_BYTES),
        name="combine",
    )(gk, h1, ysel, ysel, ysel, ysel, lng, lnb)


def _row(v, width=None):
    v = v.reshape(1, -1).astype(F32)
    if width is not None and v.shape[1] < width:
        v = jnp.pad(v, ((0, 0), (0, width - v.shape[1])))
    return v


def _encode_all(x_a, x_b, meta_tokens, ln_emb_g, ln_emb_b, w_in, conv_w, conv_b, dt_bias, a_log,
                d_skip, ssd_norm_g, i_bias, f_bias, mlstm_norm_g, w_out, ln1_g, ln1_b, w_router, b_router,
                w1, b1, w2, b2, ln2_g, ln2_b):
    n_a, seq_len, _ = x_a.shape
    n_seq = n_a + x_b.shape[0]
    rows_a = n_a * seq_len
    assert seq_len % CHUNK == 0
    n_chunks = seq_len // CHUNK + 1
    n_tok = n_seq * seq_len

    sizes = (1024, CONV_CH, 2 * SSD_HEADS, 512, 512, 1024, 1024, 2 * MLSTM_HEADS, 2 * MLSTM_HEADS)
    offs = [0]
    for s in sizes:
        offs.append(offs[-1] + s)
    w_z, w_xbc, w_dt, w_q, w_k, w_v, w_o, w_i, w_f = [w_in[:, offs[j]:offs[j + 1]] for j in range(9)]
    w_big = jnp.concatenate([w_z, w_xbc, w_q, w_k * (MLSTM_DK ** -0.5), w_v, w_o], axis=1).astype(BF16)
    zpad = jnp.zeros((D_MODEL, LANES - GATE_END), F32)
    gate_cols = []
    for d in range(2):
        gate_cols += [w_dt[:, d * SSD_HEADS:(d + 1) * SSD_HEADS],
                      w_i[:, d * MLSTM_HEADS:(d + 1) * MLSTM_HEADS],
                      w_f[:, d * MLSTM_HEADS:(d + 1) * MLSTM_HEADS], zpad]
    w_gates = jnp.concatenate(gate_cols, axis=1).astype(BF16)
    gbias = [_row(jnp.concatenate([dt_bias[d], i_bias[d], f_bias[d]]), LANES) for d in range(2)]
    alog = [_row(a_log[d], LANES) for d in range(2)]
    head_of_col = jnp.arange(D_MODEL, dtype=jnp.int32) // SSD_HEAD_DIM
    expand = (jnp.arange(LANES, dtype=jnp.int32)[:, None] == head_of_col[None, :]).astype(BF16)
    dskip = _row(jnp.repeat(d_skip, SSD_HEAD_DIM))
    conv_w8 = jnp.pad(conv_w.astype(F32), ((0, 8 - CONV_W), (0, 0)))
    meta_tile = jnp.pad(meta_tokens.astype(F32), ((PAD_FRONT, INPROJ_ROWS - CHUNK), (0, 0)))

    z, xbc, q, k, v, o, gates_f, gates_b = _inproj(x_a, x_b, meta_tile, _row(ln_emb_g), _row(ln_emb_b),
                                                   w_big, w_gates)
    xs, bm, cm = _conv(xbc, conv_w8, _row(conv_b), n_chunks)
    yb, hb = _mixer_pass(xs, bm, cm, gates_b, q, k, v, gbias[1], alog[1], expand, n_chunks, reverse=True)
    (ycat,) = _mixer_pass(xs, bm, cm, gates_f, q, k, v, gbias[0], alog[0], expand, n_chunks,
                          reverse=False,
                          final_inputs=(z, o, yb, hb, dskip, _row(ssd_norm_g), _row(mlstm_norm_g)))

    wr = jnp.pad(w_router.astype(F32), ((0, 0), (0, LANES - N_EXPERTS)))
    wrh = wr.astype(BF16)
    wrl = (wr - wrh.astype(F32)).astype(BF16)
    h1, h1p, sel, gates = _epilogue(x_a.reshape(rows_a, D_MODEL), x_b.reshape(n_tok - rows_a, D_MODEL),
                                    ycat.reshape(n_tok, 2 * D_MODEL),
                               w_out.astype(BF16), _row(ln_emb_g), _row(ln_emb_b), _row(ln1_g),
                               _row(ln1_b), wrh, wrl, _row(b_router, LANES))

    n_blocks = n_tok * TOP_K // MOE_BLOCK + N_EXPERTS
    trash_row = n_blocks * MOE_BLOCK
    n_buf_rows = trash_row + MOE_BLOCK
    r_i = jnp.arange(RANK_ROWS, dtype=jnp.int32)
    lstrict = (r_i[None, :] < r_i[:, None]).astype(BF16)
    l_i = jnp.arange(LANES, dtype=jnp.int32)
    ucum = (l_i[:, None] <= l_i[None, :]).astype(BF16)
    dest, gk, stats = _rank(sel, gates, lstrict, ucum, trash_row)
    counts = stats[0, :N_EXPERTS].astype(jnp.int32)
    starts = stats[1, :N_EXPERTS].astype(jnp.int32)
    pends = stats[2, :N_EXPERTS].astype(jnp.int32)
    n_used = pends[N_EXPERTS - 1] // MOE_BLOCK
    blk = jnp.minimum(jnp.arange(n_blocks, dtype=jnp.int32), jnp.maximum(n_used - 1, 0))
    blk_exp = jnp.minimum(jnp.sum((pends[None, :] <= (blk * MOE_BLOCK)[:, None]).astype(jnp.int32), axis=1),
                          N_EXPERTS - 1).astype(jnp.int32)

    blk_last = jnp.take(starts + counts, blk_exp)
    blk_valid = jnp.clip(blk_last - blk * MOE_BLOCK, 0, MOE_BLOCK).astype(jnp.int32)
    dest_km = dest.T.reshape(-1)

    xb = _sc_scatter_rows(h1p, dest_km, n_buf_rows)
    yexp = _ffn(blk, blk_exp, blk_valid, n_used.reshape(1), xb, w1,
                b1.reshape(N_EXPERTS, 1, -1), w2, b2.reshape(N_EXPERTS, 1, -1), n_blocks)
    ysel = _sc_gather_rows(yexp, dest_km)
    return _combine(gk, h1, ysel, _row(ln2_g), _row(ln2_b), rows_a)


def kernel(x_prompt, x_sample, meta_tokens, ln_emb_g, ln_emb_b, w_in, conv_w, conv_b, dt_bias, a_log,
           d_skip, ssd_norm_g, i_bias, f_bias, mlstm_norm_g, w_out, ln1_g, ln1_b, w_router, b_router,
           w1, b1, w2, b2, ln2_g, ln2_b):
    assert x_prompt.shape[1:] == x_sample.shape[1:]
    n_p, seq_len, d = x_prompt.shape
    n_s = x_sample.shape[0]
    y_p, y_s = _encode_all(x_prompt.astype(F32), x_sample.astype(F32), meta_tokens, ln_emb_g, ln_emb_b, w_in[0], conv_w[0],
                           conv_b[0], dt_bias[0], a_log[0], d_skip[0], ssd_norm_g[0], i_bias[0], f_bias[0],
                           mlstm_norm_g[0], w_out[0], ln1_g[0], ln1_b[0], w_router[0], b_router[0],
                           w1[0], b1[0], w2[0], b2[0], ln2_g[0], ln2_b[0])
    return (y_p.reshape(n_p, seq_len, d), y_s.reshape(n_s, seq_len, d))
```

```python
import functools

import jax
import jax.numpy as jnp
from jax import lax
from jax.experimental import pallas as pl
from jax.experimental.pallas import tpu as pltpu
from jax.experimental.pallas import tpu_sc as plsc

F32 = jnp.float32
BF16 = jnp.bfloat16

D_MODEL = 1024
N_META = 16
CHUNK = 128
PAD_FRONT = CHUNK - N_META
SSD_HEADS = 16
SSD_HEAD_DIM = 64
SSD_GROUPS = 4
SSD_STATE = 128
HEADS_PER_GROUP = SSD_HEADS // SSD_GROUPS
GROUP_WIDTH = HEADS_PER_GROUP * SSD_HEAD_DIM
CONV_W = 5
CONV_HALF = CONV_W // 2
CONV_CH = D_MODEL + 2 * SSD_GROUPS * SSD_STATE
MLSTM_HEADS = 8
MLSTM_DK = 64
MLSTM_DV = 128
N_EXPERTS = 32
TOP_K = 4
D_FF = D_MODEL
SWIGLU_LIMIT = 7.0
SWIGLU_ALPHA = 1.702
DEEPNORM_ALPHA = 2.0 ** 0.25
LN_EPS = 1e-5
RMS_EPS = 1e-5
NEG_GATE = -1e30

LANES = 128
BF16_SUBLANES = 16
VMEM_LIMIT_BYTES = 56 * 1024 * 1024

GATE_DT0, GATE_I0, GATE_F0, GATE_END = 0, SSD_HEADS, SSD_HEADS + MLSTM_HEADS, SSD_HEADS + 2 * MLSTM_HEADS

INPROJ_ROWS = 512
EPILOGUE_ROWS = 512
RANK_ROWS = 512
COMBINE_ROWS = 512
MOE_BLOCK = 512

SC_CORES = 2
SC_SUBCORES = 16
SC_WORKERS = SC_CORES * SC_SUBCORES
SC_WINDOW = 32
SC_SCATTER_INFLIGHT = 2
SC_GATHER_INFLIGHT = 4


def _dot(a, b):
    return jnp.dot(a, b, preferred_element_type=F32)


def _dot_nt(a, b):
    return lax.dot_general(a, b, (((1,), (1,)), ((), ())), preferred_element_type=F32)


def _split3(x):
    hi = x.astype(BF16)
    r1 = x - hi.astype(F32)
    mid = r1.astype(BF16)
    lo = (r1 - mid.astype(F32)).astype(BF16)
    return hi, mid, lo


def _dot_exact_lhs(a_bf16, x):
    hi, mid, lo = _split3(x)
    return _dot(a_bf16, hi) + _dot(a_bf16, mid) + _dot(a_bf16, lo)


def _dot_exact_rhs(x, b_bf16):
    hi, mid, lo = _split3(x)
    return _dot(hi, b_bf16) + _dot(mid, b_bf16) + _dot(lo, b_bf16)


def _layer_norm(x, g, b):
    mu = jnp.mean(x, axis=-1, keepdims=True)
    xc = x - mu
    var = jnp.mean(xc * xc, axis=-1, keepdims=True)
    return xc * lax.rsqrt(var + LN_EPS) * g + b


def _sigmoid(x):
    return 1.0 / (1.0 + jnp.exp(-x))


def _log1p_exp_neg_abs(x):
    return jnp.log(1.0 + jnp.exp(-jnp.abs(x)))


def _pack_bf16_pairs(x):
    n = x.shape[1] // 2
    bits = lax.bitcast_convert_type(x.astype(BF16).astype(F32), jnp.uint32)
    return (bits[:, :n] >> 16) | bits[:, n:]


def _unpack_bf16_pairs(words):
    lo = lax.bitcast_convert_type(words << 16, F32)
    hi = lax.bitcast_convert_type(words & jnp.uint32(0xFFFF0000), F32)
    return jnp.concatenate([lo, hi], axis=1)


def _storage_chunk(c, n_chunks):
    return jnp.where(c == 0, n_chunks - 1, c - 1)


def _inproj_kernel(xa_ref, xb_ref, meta_ref, g_ref, b_ref, wbig_ref, wg_ref,
                   z_ref, xbc_ref, q_ref, k_ref, v_ref, o_ref, gf_ref, gb_ref, *, n_x_tiles, n_a):
    x = jnp.where(pl.program_id(0) < n_a, xa_ref[...], xb_ref[...])
    x = jnp.where(pl.program_id(1) == n_x_tiles, meta_ref[...], x)
    h = _layer_norm(x, g_ref[...], b_ref[...]).astype(BF16)

    def mm(c0, c1):
        return _dot(h, wbig_ref[:, c0:c1]).astype(BF16)

    z_ref[...] = mm(0, 1024)
    xbc_ref[:, 0:1024] = mm(1024, 2048)
    xbc_ref[:, 1024:2048] = mm(2048, 3072)
    q_ref[...] = mm(3072, 3584)
    k_ref[...] = mm(3584, 4096)
    v_ref[...] = mm(4096, 5120)
    o_ref[...] = mm(5120, 6144)
    gates = _dot(h, wg_ref[...])
    gf_ref[...] = gates[:, :LANES]
    gb_ref[...] = gates[:, LANES:]


def _inproj(x_a, x_b, meta_tile, ln_g, ln_b, w_big, w_gates):
    n_a, seq_len, _ = x_a.shape
    n_seq = n_a + x_b.shape[0]
    tm = INPROJ_ROWS
    assert seq_len % tm == 0 and tm >= CHUNK
    n_x_tiles = seq_len // tm
    rows = seq_len + CHUNK
    row_spec = lambda n: pl.BlockSpec((None, tm, n), lambda b, i: (b, i, 0))
    const = lambda a: pl.BlockSpec(a.shape, lambda b, i: (0,) * a.ndim)
    resident = lambda a: pl.BlockSpec(a.shape, lambda b, i: (0,) * a.ndim, pipeline_mode=pl.Buffered(1))
    widths = (1024, CONV_CH, 512, 512, 1024, 1024)
    out_shapes = [jax.ShapeDtypeStruct((n_seq, rows, w), BF16) for w in widths]
    out_shapes += [jax.ShapeDtypeStruct((n_seq, rows, LANES), F32)] * 2
    return pl.pallas_call(
        functools.partial(_inproj_kernel, n_x_tiles=n_x_tiles, n_a=n_a),
        grid=(n_seq, n_x_tiles + 1),
        in_specs=[pl.BlockSpec((None, tm, D_MODEL), lambda b, i: (
                      jnp.minimum(b, n_a - 1),
                      jnp.where(b < n_a, jnp.minimum(i, n_x_tiles - 1), n_x_tiles - 1), 0)),
                  pl.BlockSpec((None, tm, D_MODEL), lambda b, i: (
                      jnp.maximum(b - n_a, 0),
                      jnp.where(b < n_a, 0, jnp.minimum(i, n_x_tiles - 1)), 0)),
                  resident(meta_tile), const(ln_g), const(ln_b), resident(w_big), resident(w_gates)],
        out_specs=[row_spec(s.shape[2]) for s in out_shapes],
        out_shape=out_shapes,
        compiler_params=pltpu.CompilerParams(
            dimension_semantics=("arbitrary", "arbitrary"), vmem_limit_bytes=VMEM_LIMIT_BYTES),
        name="inproj",
    )(x_a, x_b, meta_tile, ln_g, ln_b, w_big, w_gates)


def _conv_kernel(prev_ref, main_ref, next_ref, shift_ref, w_ref, b_ref, xs_ref, bm_ref, cm_ref, *, n_chunks):
    c = pl.program_id(1)
    row = lax.broadcasted_iota(jnp.int32, (CHUNK, 1), 0)
    pad_rows = jnp.logical_and(c == 0, row < PAD_FRONT)
    main = jnp.where(pad_rows, 0.0, main_ref[...].astype(F32))
    prev = jnp.where(c == 0, 0.0, prev_ref[...].astype(F32))
    nxt = jnp.where(c == n_chunks - 1, 0.0, next_ref[...].astype(F32))
    xp = jnp.concatenate([prev, main, nxt], axis=0).astype(BF16)
    w = w_ref[...]
    acc = b_ref[...] + main * w[CONV_HALF:CONV_HALF + 1, :]
    shifted = _dot(shift_ref[...], xp)
    for j, t in enumerate(t for t in range(CONV_W) if t != CONV_HALF):
        acc = acc + shifted[j * CHUNK:(j + 1) * CHUNK, :] * w[t:t + 1, :]
    y = acc * _sigmoid(acc)
    y = jnp.where(pad_rows, 0.0, y)
    xs_ref[...] = y[:, :D_MODEL].astype(BF16)
    bm_ref[...] = y[:, D_MODEL:D_MODEL + 512].astype(BF16)
    cm_ref[...] = y[:, D_MODEL + 512:].astype(BF16)


def _conv(xbc, conv_w8, conv_b, n_chunks):
    n_seq, rows, _ = xbc.shape
    halo_blocks_per_chunk = CHUNK // BF16_SUBLANES

    def main_map(b, c):
        return (b, _storage_chunk(c, n_chunks), 0)

    def prev_map(b, c):
        sc = _storage_chunk(jnp.maximum(c - 1, 0), n_chunks)
        return (b, sc * halo_blocks_per_chunk + halo_blocks_per_chunk - 1, 0)

    def next_map(b, c):
        sc = _storage_chunk(jnp.minimum(c + 1, n_chunks - 1), n_chunks)
        return (b, sc * halo_blocks_per_chunk, 0)

    const = lambda a: pl.BlockSpec(a.shape, lambda b, c: (0,) * a.ndim)
    l_i = jnp.arange(CHUNK, dtype=jnp.int32)[:, None]
    j_i = jnp.arange(CHUNK + 2 * BF16_SUBLANES, dtype=jnp.int32)[None, :]
    shifts = jnp.concatenate([(j_i == BF16_SUBLANES + l_i + t - CONV_HALF)
                              for t in range(CONV_W) if t != CONV_HALF], axis=0).astype(BF16)
    out_shapes = [jax.ShapeDtypeStruct((n_seq, rows, D_MODEL), BF16),
                  jax.ShapeDtypeStruct((n_seq, rows, 512), BF16),
                  jax.ShapeDtypeStruct((n_seq, rows, 512), BF16)]
    return pl.pallas_call(
        functools.partial(_conv_kernel, n_chunks=n_chunks),
        grid=(n_seq, n_chunks),
        in_specs=[pl.BlockSpec((None, BF16_SUBLANES, CONV_CH), prev_map),
                  pl.BlockSpec((None, CHUNK, CONV_CH), main_map),
                  pl.BlockSpec((None, BF16_SUBLANES, CONV_CH), next_map),
                  const(shifts), const(conv_w8), const(conv_b)],
        out_specs=[pl.BlockSpec((None, CHUNK, s.shape[2]), main_map) for s in out_shapes],
        out_shape=out_shapes,
        compiler_params=pltpu.CompilerParams(
            dimension_semantics=("arbitrary", "arbitrary"), vmem_limit_bytes=VMEM_LIMIT_BYTES),
        name="conv",
    )(xbc, xbc, xbc, shifts, conv_w8, conv_b)


def _mixer_kernel(*refs, reverse, final, n_chunks, n_seq):
    if final:
        (xs_ref, bm_ref, cm_ref, g_ref, q_ref, k_ref, v_ref, z_ref, o_ref, yb_ref, hb_ref,
         gbias_ref, alog_ref, expand_ref, dskip_ref, ngs_ref, ngm_ref,
         ycat_ref, s_ref, cst_ref, m_ref) = refs
    else:
        (xs_ref, bm_ref, cm_ref, g_ref, q_ref, k_ref, v_ref,
         gbias_ref, alog_ref, expand_ref,
         yout_ref, hout_ref, s_ref, cst_ref, m_ref) = refs

    t = pl.program_id(0)
    c = (n_chunks - 1 - t) if reverse else t
    end = 0 if reverse else CHUNK - 1

    @pl.when(t == 0)
    def _():
        s_ref[...] = jnp.zeros_like(s_ref)
        cst_ref[...] = jnp.zeros_like(cst_ref)
        m_ref[...] = jnp.zeros_like(m_ref)

    row = lax.broadcasted_iota(jnp.int32, (CHUNK, 1), 0)
    col = lax.broadcasted_iota(jnp.int32, (1, CHUNK), 1)
    lane = col
    allowed = (col >= row) if reverse else (col <= row)
    tri = allowed.astype(BF16)
    tri_t = ((row >= col) if reverse else (row <= col)).astype(BF16)
    is_dt = lane < GATE_I0
    is_i = jnp.logical_and(lane >= GATE_I0, lane < GATE_F0)
    is_f = jnp.logical_and(lane >= GATE_F0, lane < GATE_END)
    pad_rows = jnp.logical_and(c == 0, row < PAD_FRONT)
    a_coef = -jnp.exp(alog_ref[...])
    expand = expand_ref[...]
    left_half = lane < SSD_HEAD_DIM
    top_half = row < MLSTM_DK
    ones_blk = jnp.ones((CHUNK, MLSTM_DV), BF16)

    def one_sequence(b):
        gr = g_ref[b] + gbias_ref[...]
        lse = _log1p_exp_neg_abs(gr)
        softplus = jnp.maximum(gr, 0.0) + lse
        logsig = jnp.minimum(gr, 0.0) - lse
        val = jnp.where(is_dt, softplus, jnp.where(is_i, gr, jnp.where(is_f, logsig, 0.0)))
        val = jnp.where(pad_rows, jnp.where(is_i, NEG_GATE, 0.0), val)
        u = jnp.where(is_dt, val * a_coef, jnp.where(is_f, val, 0.0))
        val_t = val.T
        u_t = u.T
        cums = _dot_exact_lhs(tri, u)
        cums_t = _dot_exact_rhs(u_t, tri_t)

        cums_end = cums[end:end + 1, :]
        p1 = jnp.exp(cums)
        p2 = jnp.exp(cums_end - cums) * val
        ex1 = _dot(p1.astype(BF16), expand)
        ex2 = _dot(p2.astype(BF16), expand)
        chunk_decay = _dot_exact_rhs(jnp.broadcast_to(jnp.exp(cums_end), (8, LANES)), expand)[0:1, :]
        xs = xs_ref[b]
        xsf = xs.astype(F32)
        xs_w = (xsf * ex2).astype(BF16)
        y_groups = []
        for g in range(SSD_GROUPS):
            cg = cm_ref[b, :, g * SSD_STATE:(g + 1) * SSD_STATE]
            bg = bm_ref[b, :, g * SSD_STATE:(g + 1) * SSD_STATE]
            cb = _dot_nt(cg, bg)
            ys = []
            for pp in range(HEADS_PER_GROUP // 2):
                pair = g * (HEADS_PER_GROUP // 2) + pp
                xs_pair = xs[:, pair * LANES:(pair + 1) * LANES]
                y_pair = None
                for hh in range(2):
                    h = 2 * pair + hh
                    seg = cums[:, h:h + 1] - cums_t[h:h + 1, :]
                    dec = jnp.exp(jnp.where(allowed, seg, -jnp.inf))
                    m_mat = (cb * dec * val_t[h:h + 1, :]).astype(BF16)
                    keep = left_half if hh == 0 else jnp.logical_not(left_half)
                    part = _dot(m_mat, jnp.where(keep, xs_pair, jnp.zeros_like(xs_pair)))
                    y_pair = part if y_pair is None else y_pair + part
                ys.append(y_pair)
            y_diag = jnp.concatenate(ys, axis=1)
            gs = slice(g * GROUP_WIDTH, (g + 1) * GROUP_WIDTH)
            s_g = s_old[b][g]
            y_off = _dot(cg, s_g.astype(BF16)) * ex1[:, gs]
            y_groups.append(y_diag + y_off)
            bg_t = bg.astype(F32).T.astype(BF16)
            s_new_all[b].append(chunk_decay[:, gs] * s_g + _dot(bg_t, xs_w[:, gs]))
        y_ssd = jnp.concatenate(y_groups, axis=1)

        bcum_t = cums_t[GATE_F0:GATE_END, :]
        ip_t = val_t[GATE_I0:GATE_F0, :]
        g_t = bcum_t[:, end:end + 1]
        a_t = g_t - bcum_t + ip_t
        a_max = jnp.max(a_t, axis=1, keepdims=True)
        w_t = jnp.exp(a_t - a_max)
        m_prev = m_old[b][:, 0:1]
        m_new = jnp.maximum(g_t + m_prev, a_max)
        s_prev = jnp.exp(g_t + m_prev - m_new)
        s_new = jnp.exp(a_max - m_new)
        r_t = ip_t - bcum_t
        h_heads = []
        for pair in range(MLSTM_HEADS // 2):
            h0, h1 = 2 * pair, 2 * pair + 1
            q_pair = q_ref[b, :, pair * LANES:(pair + 1) * LANES]
            k_pair = k_ref[b, :, pair * LANES:(pair + 1) * LANES]
            cst = cst_old[b][pair]
            cst_b = cst.astype(BF16)
            v_pair = []
            for hh, h in enumerate((h0, h1)):
                keep = left_half if hh == 0 else jnp.logical_not(left_half)
                qh = jnp.where(keep, q_pair, jnp.zeros_like(q_pair))
                vh = v_ref[b, :, h * MLSTM_DV:(h + 1) * MLSTM_DV]
                v_pair.append(vh)
                qk = _dot_nt(qh, k_pair)
                bc = cums[:, GATE_F0 + h:GATE_F0 + h + 1]
                dlog = jnp.where(allowed, bc + r_t[h:h + 1, :], -jnp.inf)
                m_intra = jnp.max(dlog, axis=1, keepdims=True)
                m_inter = bc + m_prev[h:h + 1, :]
                m_t = jnp.maximum(m_inter, m_intra)
                s_mat = (qk * jnp.exp(dlog - m_t)).astype(BF16)
                intra = _dot(s_mat, jnp.concatenate([vh, ones_blk], axis=1))
                inter = _dot(qh, cst_b)
                tot = jnp.exp(m_inter - m_t) * inter + intra
                num = tot[:, :MLSTM_DV]
                den = tot[:, MLSTM_DV:]
                h_heads.append(num / jnp.maximum(jnp.abs(den), jnp.exp(-m_t)))
            w_rows = jnp.where(top_half, w_t[h0:h0 + 1, :], w_t[h1:h1 + 1, :])
            kw = (k_pair.astype(F32).T * w_rows).astype(BF16)
            full = _dot(kw, jnp.concatenate([v_pair[0], v_pair[1], ones_blk], axis=1))
            kvn = jnp.concatenate(
                [jnp.where(top_half, full[:, :MLSTM_DV], full[:, MLSTM_DV:2 * MLSTM_DV]),
                 full[:, 2 * MLSTM_DV:]], axis=1)
            sp_rows = jnp.where(top_half, s_prev[h0:h0 + 1, :], s_prev[h1:h1 + 1, :])
            sn_rows = jnp.where(top_half, s_new[h0:h0 + 1, :], s_new[h1:h1 + 1, :])
            cst_new_all[b].append(sp_rows * cst + sn_rows * kvn)
        m_new_all[b] = jnp.broadcast_to(m_new, (MLSTM_HEADS, LANES))
        h_ml = jnp.concatenate(h_heads, axis=1)
        if not final:
            return y_ssd.astype(BF16), h_ml.astype(BF16)

        y_tot = y_ssd + yb_ref[b].astype(F32) + dskip_ref[...] * xsf
        zz = z_ref[b].astype(F32)
        y2 = y_tot * (zz * _sigmoid(zz))
        y_n = y2 * lax.rsqrt(jnp.mean(y2 * y2, axis=-1, keepdims=True) + RMS_EPS) * ngs_ref[...]
        h_tot = h_ml + hb_ref[b].astype(F32)
        segs = []
        for h in range(MLSTM_HEADS):
            seg = h_tot[:, h * MLSTM_DV:(h + 1) * MLSTM_DV]
            segs.append(seg * lax.rsqrt(jnp.mean(seg * seg, axis=-1, keepdims=True) + RMS_EPS))
        h_n = jnp.concatenate(segs, axis=1) * ngm_ref[...]
        y_ml = _sigmoid(o_ref[b].astype(F32)) * h_n
        return y_n.astype(BF16), y_ml.astype(BF16)

    s_old = [[s_ref[b, g] for g in range(SSD_GROUPS)] for b in range(n_seq)]
    cst_old = [[cst_ref[b, p] for p in range(MLSTM_HEADS // 2)] for b in range(n_seq)]
    m_old = [m_ref[b] for b in range(n_seq)]
    s_new_all = [[] for _ in range(n_seq)]
    cst_new_all = [[] for _ in range(n_seq)]
    m_new_all = [None] * n_seq
    results = [one_sequence(b) for b in range(n_seq)]
    for b in range(n_seq):
        for g in range(SSD_GROUPS):
            s_ref[b, g] = s_new_all[b][g]
        for p in range(MLSTM_HEADS // 2):
            cst_ref[b, p] = cst_new_all[b][p]
        m_ref[b] = m_new_all[b]

    @pl.when(c > 0)
    def _():
        for b, (first, second) in enumerate(results):
            if final:
                ycat_ref[b, :, :D_MODEL] = first
                ycat_ref[b, :, D_MODEL:] = second
            else:
                yout_ref[b] = first
                hout_ref[b] = second


def _mixer_pass(xs, bm, cm, gates, q, k, v, gbias, alog, expand, n_chunks, *, reverse, final_inputs=None):
    final = final_inputs is not None
    n_seq = xs.shape[0]
    seq_len = (n_chunks - 1) * CHUNK

    def chunk_of(t):
        return (n_chunks - 1 - t) if reverse else t

    def pad_map(t):
        return (0, _storage_chunk(chunk_of(t), n_chunks), 0)

    def out_map(t):
        return (0, jnp.maximum(chunk_of(t) - 1, 0), 0)

    const = lambda a: pl.BlockSpec(a.shape, lambda t: (0,) * a.ndim)
    pad_spec = lambda n: pl.BlockSpec((n_seq, CHUNK, n), pad_map)
    out_spec = lambda n: pl.BlockSpec((n_seq, CHUNK, n), out_map)
    in_arrays = [xs, bm, cm, gates, q, k, v]
    in_specs = [pad_spec(a.shape[2]) for a in in_arrays]
    if final:
        z, o, yb, hb, dskip, ngs, ngm = final_inputs
        in_arrays += [z, o, yb, hb]
        in_specs += [pad_spec(1024), pad_spec(1024), out_spec(1024), out_spec(1024)]
        in_arrays += [gbias, alog, expand, dskip, ngs, ngm]
        in_specs += [const(a) for a in (gbias, alog, expand, dskip, ngs, ngm)]
        out_shape = [jax.ShapeDtypeStruct((n_seq, seq_len, 2 * D_MODEL), BF16)]
        out_specs = [out_spec(2 * D_MODEL)]
    else:
        in_arrays += [gbias, alog, expand]
        in_specs += [const(a) for a in (gbias, alog, expand)]
        out_shape = [jax.ShapeDtypeStruct((n_seq, seq_len, D_MODEL), BF16),
                     jax.ShapeDtypeStruct((n_seq, seq_len, D_MODEL), BF16)]
        out_specs = [out_spec(D_MODEL), out_spec(D_MODEL)]
    return pl.pallas_call(
        functools.partial(_mixer_kernel, reverse=reverse, final=final, n_chunks=n_chunks, n_seq=n_seq),
        grid=(n_chunks,),
        in_specs=in_specs,
        out_specs=out_specs,
        out_shape=out_shape,
        scratch_shapes=[pltpu.VMEM((n_seq, SSD_GROUPS, SSD_STATE, GROUP_WIDTH), F32),
                        pltpu.VMEM((n_seq, MLSTM_HEADS // 2, 2 * MLSTM_DK, 2 * MLSTM_DV), F32),
                        pltpu.VMEM((n_seq, MLSTM_HEADS, LANES), F32)],
        compiler_params=pltpu.CompilerParams(
            dimension_semantics=("arbitrary",), vmem_limit_bytes=VMEM_LIMIT_BYTES),
        name="mixer_fwd" if final else "mixer_bwd",
    )(*in_arrays)


def _epilogue_kernel(xa_ref, xb_ref, ycat_ref, wout_ref, lng0_ref, lnb0_ref, lng1_ref, lnb1_ref,
                     wrh_ref, wrl_ref, br_ref, h1_ref, h1p_ref, sel_ref, gate_ref, *, n_tiles_a):
    x = jnp.where(pl.program_id(0) < n_tiles_a, xa_ref[...], xb_ref[...])
    h0 = _layer_norm(x, lng0_ref[...], lnb0_ref[...])
    mix = _dot(ycat_ref[...], wout_ref[...])
    h1 = _layer_norm(DEEPNORM_ALPHA * h0 + mix, lng1_ref[...], lnb1_ref[...])
    h1_ref[...] = h1
    h1p_ref[...] = _pack_bf16_pairs(h1)
    hh = h1.astype(BF16)
    hl = (h1 - hh.astype(F32)).astype(BF16)
    wrh = wrh_ref[...]
    logits = _dot(hh, wrh) + _dot(hl, wrh) + _dot(hh, wrl_ref[...]) + br_ref[...]
    lane = lax.broadcasted_iota(jnp.int32, (1, LANES), 1)
    lane_f = lane.astype(F32)
    logits = jnp.where(lane < N_EXPERTS, logits, -jnp.inf)
    work = logits
    sel = jnp.zeros(logits.shape, jnp.bool_)
    top = None
    for _ in range(TOP_K):
        m = jnp.max(work, axis=-1, keepdims=True)
        if top is None:
            top = m
        first = jnp.min(jnp.where(work == m, lane_f, float(LANES)), axis=-1, keepdims=True)
        pick = lane_f == first
        sel = jnp.logical_or(sel, pick)
        work = jnp.where(pick, -jnp.inf, work)
    e = jnp.where(sel, jnp.exp(logits - top), 0.0)
    gate_ref[...] = e / jnp.sum(e, axis=-1, keepdims=True)
    sel_ref[...] = sel.astype(F32)


def _epilogue(x_a, x_b, ycat, w_out, lng0, lnb0, lng1, lnb1, wrh, wrl, br):
    rows = x_a.shape[0] + x_b.shape[0]
    tm = EPILOGUE_ROWS
    assert x_a.shape[0] % tm == 0 and x_b.shape[0] % tm == 0
    n_tiles_a = x_a.shape[0] // tm
    row_spec = lambda n: pl.BlockSpec((tm, n), lambda i: (i, 0))
    const = lambda a: pl.BlockSpec(a.shape, lambda i: (0,) * a.ndim)
    out_shape = [jax.ShapeDtypeStruct((rows, D_MODEL), F32),
                 jax.ShapeDtypeStruct((rows, D_MODEL // 2), jnp.uint32),
                 jax.ShapeDtypeStruct((rows, LANES), F32),
                 jax.ShapeDtypeStruct((rows, LANES), F32)]
    consts = (w_out, lng0, lnb0, lng1, lnb1, wrh, wrl, br)
    return pl.pallas_call(
        functools.partial(_epilogue_kernel, n_tiles_a=n_tiles_a),
        grid=(rows // tm,),
        in_specs=[pl.BlockSpec((tm, D_MODEL), lambda i: (jnp.minimum(i, n_tiles_a - 1), 0)),
                  pl.BlockSpec((tm, D_MODEL), lambda i: (jnp.maximum(i - n_tiles_a, 0), 0)),
                  row_spec(2 * D_MODEL)] + [const(a) for a in consts],
        out_specs=[row_spec(D_MODEL), row_spec(D_MODEL // 2), row_spec(LANES), row_spec(LANES)],
        out_shape=out_shape,
        compiler_params=pltpu.CompilerParams(
            dimension_semantics=("arbitrary",), vmem_limit_bytes=VMEM_LIMIT_BYTES),
        name="epilogue",
    )(x_a, x_b, ycat, *consts)


def _rank_kernel(sel_ref, gate_ref, lstrict_ref, ucum_ref, dest_ref, gk_ref, stats_ref,
                 base_ref, *, trash_row):
    phase = pl.program_id(0)
    i = pl.program_id(1)
    sel = sel_ref[...]
    colsum = jnp.sum(sel, axis=0, keepdims=True)

    @pl.when(jnp.logical_and(phase == 0, i == 0))
    def _():
        base_ref[...] = jnp.zeros_like(base_ref)

    @pl.when(phase == 0)
    def _():
        base_ref[0:1, :] = base_ref[0:1, :] + colsum

    @pl.when(jnp.logical_and(phase == 1, i == 0))
    def _():
        counts = base_ref[0:1, :]
        padded = jnp.ceil(counts / MOE_BLOCK) * MOE_BLOCK
        pend = _dot_exact_rhs(jnp.broadcast_to(padded, (8, LANES)), ucum_ref[...])[0:1, :]
        stats_ref[0:1, :] = counts
        stats_ref[1:2, :] = pend - padded
        stats_ref[2:3, :] = pend
        stats_ref[3:8, :] = jnp.zeros((5, LANES), F32)
        base_ref[1:2, :] = pend - padded

    @pl.when(phase == 1)
    def _():
        before = _dot(lstrict_ref[...], sel.astype(BF16))
        pos = base_ref[1:2, :] + before
        base_ref[1:2, :] = base_ref[1:2, :] + colsum
        work = jnp.where(sel > 0.0, pos + 1.0, 0.0)
        gates = gate_ref[...]
        for kk in range(TOP_K):
            m = jnp.max(work, axis=-1, keepdims=True)
            pick = jnp.logical_and(work == m, m > 0.0)
            gk_ref[:, kk:kk + 1] = jnp.sum(jnp.where(pick, gates, 0.0), axis=-1, keepdims=True)
            dest_ref[:, kk:kk + 1] = jnp.where(m > 0.0, m - 1.0, float(trash_row)).astype(jnp.int32)
            work = jnp.where(pick, 0.0, work)


def _rank(sel, gates, lstrict, ucum, trash_row):
    rows = sel.shape[0]
    tm = RANK_ROWS
    assert rows % tm == 0
    row_spec = lambda n: pl.BlockSpec((tm, n), lambda p, i: (i, 0))
    out_row_spec = lambda n: pl.BlockSpec((tm, n), lambda p, i: (i * p, 0))
    const = lambda a: pl.BlockSpec(a.shape, lambda p, i: (0,) * a.ndim)
    return pl.pallas_call(
        functools.partial(_rank_kernel, trash_row=trash_row),
        grid=(2, rows // tm),
        in_specs=[row_spec(LANES), row_spec(LANES), const(lstrict), const(ucum)],
        out_specs=[out_row_spec(TOP_K), out_row_spec(TOP_K),
                   pl.BlockSpec((8, LANES), lambda p, i: (0, 0))],
        out_shape=[jax.ShapeDtypeStruct((rows, TOP_K), jnp.int32),
                   jax.ShapeDtypeStruct((rows, TOP_K), F32),
                   jax.ShapeDtypeStruct((8, LANES), F32)],
        scratch_shapes=[pltpu.VMEM((8, LANES), F32)],
        compiler_params=pltpu.CompilerParams(
            dimension_semantics=("arbitrary", "arbitrary"), vmem_limit_bytes=VMEM_LIMIT_BYTES),
        name="rank",
    )(sel, gates, lstrict, ucum)


def _sc_mesh():
    return plsc.VectorSubcoreMesh(core_axis_name="c", subcore_axis_name="s",
                                  num_cores=SC_CORES, num_subcores=SC_SUBCORES)


def _sc_scatter_rows(src, idx, n_out_rows):
    n_src, d = src.shape
    w, k = SC_WINDOW, SC_SCATTER_INFLIGHT
    assert idx.shape == (n_src // w, TOP_K, w) and n_src % (w * k * SC_WORKERS) == 0
    per_worker = n_src // (w * SC_WORKERS)
    idx2d = idx.reshape(n_src // w * TOP_K, w)

    @functools.partial(
        pl.kernel, mesh=_sc_mesh(),
        out_type=jax.ShapeDtypeStruct((n_out_rows, d), src.dtype),
        scratch_types=[pltpu.VMEM((k * TOP_K, w), jnp.int32), pltpu.VMEM((k, w, d), src.dtype),
                       pltpu.SemaphoreType.DMA((k,)), pltpu.SemaphoreType.DMA((k,))],
        name="sc_scatter_rows")
    def body(src_hbm, idx_hbm, out_hbm, idx_v, rows_v, sem_load, sem_scatter):
        wid = lax.axis_index("s") * SC_CORES + lax.axis_index("c")

        @pl.loop(0, per_worker // k)
        def _(i):
            win0 = wid * per_worker + i * k
            loads = [pltpu.async_copy(src_hbm.at[pl.ds(pl.multiple_of((win0 + b) * w, w), w)], rows_v.at[b],
                                      sem_load.at[b]) for b in range(k)]
            pltpu.sync_copy(idx_hbm.at[pl.ds(pl.multiple_of(win0 * TOP_K, TOP_K), k * TOP_K)], idx_v)
            scatters = []
            for b in range(k):
                loads[b].wait()
                scatters += [pltpu.async_copy(rows_v.at[b], out_hbm.at[idx_v.at[b * TOP_K + kk]],
                                              sem_scatter.at[b]) for kk in range(TOP_K)]
            for copy in scatters:
                copy.wait()

    return body(src, idx2d)


def _sc_gather_rows(table, idx):
    d = table.shape[1]
    w, k = SC_WINDOW, SC_GATHER_INFLIGHT
    assert idx.shape[0] % (w * k * SC_WORKERS) == 0
    per_worker = idx.shape[0] // (w * SC_WORKERS)
    idx2d = idx.reshape(-1, w)

    @functools.partial(
        pl.kernel, mesh=_sc_mesh(),
        out_type=jax.ShapeDtypeStruct((idx.shape[0], d), table.dtype),
        scratch_types=[pltpu.VMEM((k, w), jnp.int32), pltpu.VMEM((k, w, d), table.dtype),
                       pltpu.SemaphoreType.DMA((k,)), pltpu.SemaphoreType.DMA((k,))],
        name="sc_gather_rows")
    def body(table_hbm, idx_hbm, out_hbm, idx_v, rows_v, sem_gather, sem_store):
        wid = lax.axis_index("s") * SC_CORES + lax.axis_index("c")

        @pl.loop(0, per_worker // k)
        def _(i):
            win0 = wid * per_worker + i * k
            pltpu.sync_copy(idx_hbm.at[pl.ds(pl.multiple_of(win0, k), k)], idx_v)
            gathers = [pltpu.async_copy(table_hbm.at[idx_v.at[b]], rows_v.at[b], sem_gather.at[b])
                       for b in range(k)]
            stores = []
            for b in range(k):
                gathers[b].wait()
                stores.append(pltpu.async_copy(
                    rows_v.at[b], out_hbm.at[pl.ds(pl.multiple_of((win0 + b) * w, w), w)], sem_store.at[b]))
            for copy in stores:
                copy.wait()

    return body(table, idx2d)


def _ffn_kernel(blk_ref, exp_ref, valid_ref, nused_ref, xb_ref, w1_ref, b1_ref, w2_ref, b2_ref, yb_ref,
                w1b_ref, w2b_ref):
    j = pl.program_id(0)
    active = j < nused_ref[0]

    @pl.when(jnp.logical_and(active, jnp.logical_or(j == 0, exp_ref[j] != exp_ref[jnp.maximum(j - 1, 0)])))
    def _():
        w1b_ref[...] = w1_ref[0].astype(BF16)
        w2b_ref[...] = w2_ref[0].astype(BF16)

    @pl.when(active)
    def _():
        row = lax.broadcasted_iota(jnp.int32, (MOE_BLOCK, 1), 0)
        x = jnp.where(row < valid_ref[j], _unpack_bf16_pairs(xb_ref[...]), 0.0).astype(BF16)
        hc = _dot(x, w1b_ref[...]) + b1_ref[0]
        gate = jnp.minimum(hc[:, :D_FF], SWIGLU_LIMIT)
        up = jnp.clip(hc[:, D_FF:], -SWIGLU_LIMIT, SWIGLU_LIMIT)
        act = (up + 1.0) * gate * _sigmoid(SWIGLU_ALPHA * gate)
        yb_ref[...] = _pack_bf16_pairs(_dot(act.astype(BF16), w2b_ref[...]) + b2_ref[0])


def _ffn(blk_idx, blk_exp, blk_valid, n_used, xb, w1, b1, w2, b2, n_blocks):
    bm = MOE_BLOCK
    grid_spec = pltpu.PrefetchScalarGridSpec(
        num_scalar_prefetch=4,
        grid=(n_blocks,),
        in_specs=[pl.BlockSpec((bm, D_MODEL // 2), lambda j, bi, be, bv, nu: (bi[j], 0)),
                  pl.BlockSpec((1, D_MODEL, 2 * D_FF), lambda j, bi, be, bv, nu: (be[j], 0, 0)),
                  pl.BlockSpec((1, 1, 2 * D_FF), lambda j, bi, be, bv, nu: (be[j], 0, 0)),
                  pl.BlockSpec((1, D_FF, D_MODEL), lambda j, bi, be, bv, nu: (be[j], 0, 0)),
                  pl.BlockSpec((1, 1, D_MODEL), lambda j, bi, be, bv, nu: (be[j], 0, 0))],
        out_specs=pl.BlockSpec((bm, D_MODEL // 2), lambda j, bi, be, bv, nu: (bi[j], 0)),
        scratch_shapes=[pltpu.VMEM((D_MODEL, 2 * D_FF), BF16), pltpu.VMEM((D_FF, D_MODEL), BF16)],
    )
    return pl.pallas_call(
        _ffn_kernel,
        grid_spec=grid_spec,
        out_shape=jax.ShapeDtypeStruct(xb.shape, jnp.uint32),
        compiler_params=pltpu.CompilerParams(
            dimension_semantics=("arbitrary",), vmem_limit_bytes=VMEM_LIMIT_BYTES),
        name="expert_ffn",
    )(blk_idx, blk_exp, blk_valid, n_used, xb, w1, b1, w2, b2)


def _combine_kernel(gk_ref, h1_ref, y0_ref, y1_ref, y2_ref, y3_ref, lng_ref, lnb_ref, outa_ref, outb_ref,
                    *, n_tiles_a):
    gk = gk_ref[...]
    ffn = gk[:, 0:1] * _unpack_bf16_pairs(y0_ref[...])
    for kk, y_ref in enumerate((y1_ref, y2_ref, y3_ref), start=1):
        ffn = ffn + gk[:, kk:kk + 1] * _unpack_bf16_pairs(y_ref[...])
    out = _layer_norm(DEEPNORM_ALPHA * h1_ref[...] + ffn, lng_ref[...], lnb_ref[...])
    i = pl.program_id(0)

    @pl.when(i < n_tiles_a)
    def _():
        outa_ref[...] = out

    @pl.when(i >= n_tiles_a)
    def _():
        outb_ref[...] = out


def _combine(gk, h1, ysel, lng, lnb, rows_a):
    rows = h1.shape[0]
    tm = COMBINE_ROWS
    assert rows % tm == 0 and rows_a % tm == 0 and 0 < rows_a < rows
    n_tiles_a = rows_a // tm
    n_tiles = rows // tm
    const = lambda a: pl.BlockSpec(a.shape, lambda i: (0,) * a.ndim)
    ksel = lambda kk: pl.BlockSpec((tm, D_MODEL // 2), lambda i: (kk * n_tiles + i, 0))
    return pl.pallas_call(
        functools.partial(_combine_kernel, n_tiles_a=n_tiles_a),
        grid=(n_tiles,),
        in_specs=[pl.BlockSpec((tm, TOP_K), lambda i: (i, 0)),
                  pl.BlockSpec((tm, D_MODEL), lambda i: (i, 0)),
                  ksel(0), ksel(1), ksel(2), ksel(3),
                  const(lng), const(lnb)],
        out_specs=[pl.BlockSpec((tm, D_MODEL), lambda i: (jnp.minimum(i, n_tiles_a - 1), 0)),
                   pl.BlockSpec((tm, D_MODEL), lambda i: (jnp.maximum(i - n_tiles_a, 0), 0))],
        out_shape=[jax.ShapeDtypeStruct((rows_a, D_MODEL), F32),
                   jax.ShapeDtypeStruct((rows - rows_a, D_MODEL), F32)],
        compiler_params=pltpu.CompilerParams(
            dimension_semantics=("arbitrary",), vmem_limit_bytes=VMEM_LIMIT_BYTES),
        name="combine",
    )(gk, h1, ysel, ysel, ysel, ysel, lng, lnb)


def _row(v, width=None):
    v = v.reshape(1, -1).astype(F32)
    if width is not None and v.shape[1] < width:
        v = jnp.pad(v, ((0, 0), (0, width - v.shape[1])))
    return v


def _encode_all(x_a, x_b, meta_tokens, ln_emb_g, ln_emb_b, w_in, conv_w, conv_b, dt_bias, a_log,
                d_skip, ssd_norm_g, i_bias, f_bias, mlstm_norm_g, w_out, ln1_g, ln1_b, w_router, b_router,
                w1, b1, w2, b2, ln2_g, ln2_b):
    n_a, seq_len, _ = x_a.shape
    n_seq = n_a + x_b.shape[0]
    rows_a = n_a * seq_len
    assert seq_len % CHUNK == 0
    n_chunks = seq_len // CHUNK + 1
    n_tok = n_seq * seq_len

    sizes = (1024, CONV_CH, 2 * SSD_HEADS, 512, 512, 1024, 1024, 2 * MLSTM_HEADS, 2 * MLSTM_HEADS)
    offs = [0]
    for s in sizes:
        offs.append(offs[-1] + s)
    w_z, w_xbc, w_dt, w_q, w_k, w_v, w_o, w_i, w_f = [w_in[:, offs[j]:offs[j + 1]] for j in range(9)]
    w_big = jnp.concatenate([w_z, w_xbc, w_q, w_k * (MLSTM_DK ** -0.5), w_v, w_o], axis=1).astype(BF16)
    zpad = jnp.zeros((D_MODEL, LANES - GATE_END), F32)
    gate_cols = []
    for d in range(2):
        gate_cols += [w_dt[:, d * SSD_HEADS:(d + 1) * SSD_HEADS],
                      w_i[:, d * MLSTM_HEADS:(d + 1) * MLSTM_HEADS],
                      w_f[:, d * MLSTM_HEADS:(d + 1) * MLSTM_HEADS], zpad]
    w_gates = jnp.concatenate(gate_cols, axis=1).astype(BF16)
    gbias = [_row(jnp.concatenate([dt_bias[d], i_bias[d], f_bias[d]]), LANES) for d in range(2)]
    alog = [_row(a_log[d], LANES) for d in range(2)]
    head_of_col = jnp.arange(D_MODEL, dtype=jnp.int32) // SSD_HEAD_DIM
    expand = (jnp.arange(LANES, dtype=jnp.int32)[:, None] == head_of_col[None, :]).astype(BF16)
    dskip = _row(jnp.repeat(d_skip, SSD_HEAD_DIM))
    conv_w8 = jnp.pad(conv_w.astype(F32), ((0, 8 - CONV_W), (0, 0)))
    meta_tile = jnp.pad(meta_tokens.astype(F32), ((PAD_FRONT, INPROJ_ROWS - CHUNK), (0, 0)))

    z, xbc, q, k, v, o, gates_f, gates_b = _inproj(x_a, x_b, meta_tile, _row(ln_emb_g), _row(ln_emb_b),
                                                   w_big, w_gates)
    xs, bm, cm = _conv(xbc, conv_w8, _row(conv_b), n_chunks)
    yb, hb = _mixer_pass(xs, bm, cm, gates_b, q, k, v, gbias[1], alog[1], expand, n_chunks, reverse=True)
    (ycat,) = _mixer_pass(xs, bm, cm, gates_f, q, k, v, gbias[0], alog[0], expand, n_chunks,
                          reverse=False,
                          final_inputs=(z, o, yb, hb, dskip, _row(ssd_norm_g), _row(mlstm_norm_g)))

    wr = jnp.pad(w_router.astype(F32), ((0, 0), (0, LANES - N_EXPERTS)))
    wrh = wr.astype(BF16)
    wrl = (wr - wrh.astype(F32)).astype(BF16)
    h1, h1p, sel, gates = _epilogue(x_a.reshape(rows_a, D_MODEL), x_b.reshape(n_tok - rows_a, D_MODEL),
                                    ycat.reshape(n_tok, 2 * D_MODEL),
                               w_out.astype(BF16), _row(ln_emb_g), _row(ln_emb_b), _row(ln1_g),
                               _row(ln1_b), wrh, wrl, _row(b_router, LANES))

    n_blocks = n_tok * TOP_K // MOE_BLOCK + N_EXPERTS
    trash_row = n_blocks * MOE_BLOCK
    n_buf_rows = trash_row + MOE_BLOCK
    r_i = jnp.arange(RANK_ROWS, dtype=jnp.int32)
    lstrict = (r_i[None, :] < r_i[:, None]).astype(BF16)
    l_i = jnp.arange(LANES, dtype=jnp.int32)
    ucum = (l_i[:, None] <= l_i[None, :]).astype(BF16)
    dest, gk, stats = _rank(sel, gates, lstrict, ucum, trash_row)
    counts = stats[0, :N_EXPERTS].astype(jnp.int32)
    starts = stats[1, :N_EXPERTS].astype(jnp.int32)
    pends = stats[2, :N_EXPERTS].astype(jnp.int32)
    n_used = pends[N_EXPERTS - 1] // MOE_BLOCK
    blk = jnp.minimum(jnp.arange(n_blocks, dtype=jnp.int32), jnp.maximum(n_used - 1, 0))
    blk_exp = jnp.minimum(jnp.sum((pends[None, :] <= (blk * MOE_BLOCK)[:, None]).astype(jnp.int32), axis=1),
                          N_EXPERTS - 1).astype(jnp.int32)

    blk_last = jnp.take(starts + counts, blk_exp)
    blk_valid = jnp.clip(blk_last - blk * MOE_BLOCK, 0, MOE_BLOCK).astype(jnp.int32)
    dest_km = dest.T.reshape(-1)

    dest_wm = dest.reshape(n_tok // SC_WINDOW, SC_WINDOW, TOP_K).transpose(0, 2, 1)
    xb = _sc_scatter_rows(h1p, dest_wm, n_buf_rows)
    yexp = _ffn(blk, blk_exp, blk_valid, n_used.reshape(1), xb, w1,
                b1.reshape(N_EXPERTS, 1, -1), w2, b2.reshape(N_EXPERTS, 1, -1), n_blocks)
    ysel = _sc_gather_rows(yexp, dest_km)
    return _combine(gk, h1, ysel, _row(ln2_g), _row(ln2_b), rows_a)


def kernel(x_prompt, x_sample, meta_tokens, ln_emb_g, ln_emb_b, w_in, conv_w, conv_b, dt_bias, a_log,
           d_skip, ssd_norm_g, i_bias, f_bias, mlstm_norm_g, w_out, ln1_g, ln1_b, w_router, b_router,
           w1, b1, w2, b2, ln2_g, ln2_b):
    assert x_prompt.shape[1:] == x_sample.shape[1:]
    n_p, seq_len, d = x_prompt.shape
    n_s = x_sample.shape[0]
    y_p, y_s = _encode_all(x_prompt.astype(F32), x_sample.astype(F32), meta_tokens, ln_emb_g, ln_emb_b, w_in[0], conv_w[0],
                           conv_b[0], dt_bias[0], a_log[0], d_skip[0], ssd_norm_g[0], i_bias[0], f_bias[0],
                           mlstm_norm_g[0], w_out[0], ln1_g[0], ln1_b[0], w_router[0], b_router[0],
                           w1[0], b1[0], w2[0], b2[0], ln2_g[0], ln2_b[0])
    return (y_p.reshape(n_p, seq_len, d), y_s.reshape(n_s, seq_len, d))
```

```python
import functools

import jax
import jax.numpy as jnp
from jax import lax
from jax.experimental import pallas as pl
from jax.experimental.pallas import tpu as pltpu
from jax.experimental.pallas import tpu_sc as plsc

F32 = jnp.float32
BF16 = jnp.bfloat16

D_MODEL = 1024
N_META = 16
CHUNK = 128
PAD_FRONT = CHUNK - N_META
SSD_HEADS = 16
SSD_HEAD_DIM = 64
SSD_GROUPS = 4
SSD_STATE = 128
HEADS_PER_GROUP = SSD_HEADS // SSD_GROUPS
GROUP_WIDTH = HEADS_PER_GROUP * SSD_HEAD_DIM
CONV_W = 5
CONV_HALF = CONV_W // 2
CONV_CH = D_MODEL + 2 * SSD_GROUPS * SSD_STATE
MLSTM_HEADS = 8
MLSTM_DK = 64
MLSTM_DV = 128
N_EXPERTS = 32
TOP_K = 4
D_FF = D_MODEL
SWIGLU_LIMIT = 7.0
SWIGLU_ALPHA = 1.702
DEEPNORM_ALPHA = 2.0 ** 0.25
LN_EPS = 1e-5
RMS_EPS = 1e-5
NEG_GATE = -1e30

LANES = 128
BF16_SUBLANES = 16
VMEM_LIMIT_BYTES = 56 * 1024 * 1024

GATE_DT0, GATE_I0, GATE_F0, GATE_END = 0, SSD_HEADS, SSD_HEADS + MLSTM_HEADS, SSD_HEADS + 2 * MLSTM_HEADS

INPROJ_ROWS = 512
EPILOGUE_ROWS = 512
RANK_ROWS = 512
COMBINE_ROWS = 512
MOE_BLOCK = 512

SC_CORES = 2
SC_SUBCORES = 16
SC_WORKERS = SC_CORES * SC_SUBCORES
SC_WINDOW = 32
SC_SCATTER_INFLIGHT = 2
SC_GATHER_INFLIGHT = 4


def _dot(a, b):
    return jnp.dot(a, b, preferred_element_type=F32)


def _dot_nt(a, b):
    return lax.dot_general(a, b, (((1,), (1,)), ((), ())), preferred_element_type=F32)


def _split3(x):
    hi = x.astype(BF16)
    r1 = x - hi.astype(F32)
    mid = r1.astype(BF16)
    lo = (r1 - mid.astype(F32)).astype(BF16)
    return hi, mid, lo


def _dot_exact_lhs(a_bf16, x):
    hi, mid, lo = _split3(x)
    return _dot(a_bf16, hi) + _dot(a_bf16, mid) + _dot(a_bf16, lo)


def _dot_exact_rhs(x, b_bf16):
    hi, mid, lo = _split3(x)
    return _dot(hi, b_bf16) + _dot(mid, b_bf16) + _dot(lo, b_bf16)


def _layer_norm(x, g, b):
    mu = jnp.mean(x, axis=-1, keepdims=True)
    xc = x - mu
    var = jnp.mean(xc * xc, axis=-1, keepdims=True)
    return xc * lax.rsqrt(var + LN_EPS) * g + b


def _sigmoid(x):
    return 1.0 / (1.0 + jnp.exp(-x))


def _log1p_exp_neg_abs(x):
    return jnp.log(1.0 + jnp.exp(-jnp.abs(x)))


def _pack_bf16_pairs(x):
    n = x.shape[1] // 2
    bits = lax.bitcast_convert_type(x.astype(BF16).astype(F32), jnp.uint32)
    return (bits[:, :n] >> 16) | bits[:, n:]


def _unpack_bf16_pairs(words):
    lo = lax.bitcast_convert_type(words << 16, F32)
    hi = lax.bitcast_convert_type(words & jnp.uint32(0xFFFF0000), F32)
    return jnp.concatenate([lo, hi], axis=1)


def _storage_chunk(c, n_chunks):
    return jnp.where(c == 0, n_chunks - 1, c - 1)


def _inproj_kernel(xa_ref, xb_ref, meta_ref, g_ref, b_ref, wbig_ref, wg_ref,
                   z_ref, xbc_ref, q_ref, k_ref, v_ref, o_ref, gf_ref, gb_ref, *, n_x_tiles, n_a):
    x = jnp.where(pl.program_id(0) < n_a, xa_ref[...], xb_ref[...])
    x = jnp.where(pl.program_id(1) == n_x_tiles, meta_ref[...], x)
    h = _layer_norm(x, g_ref[...], b_ref[...]).astype(BF16)

    def mm(c0, c1):
        return _dot(h, wbig_ref[:, c0:c1]).astype(BF16)

    z_ref[...] = mm(0, 1024)
    xbc_ref[:, 0:1024] = mm(1024, 2048)
    xbc_ref[:, 1024:2048] = mm(2048, 3072)
    q_ref[...] = mm(3072, 3584)
    k_ref[...] = mm(3584, 4096)
    v_ref[...] = mm(4096, 5120)
    o_ref[...] = mm(5120, 6144)
    gates = _dot(h, wg_ref[...])
    gf_ref[...] = gates[:, :LANES]
    gb_ref[...] = gates[:, LANES:]


def _inproj(x_a, x_b, meta_tile, ln_g, ln_b, w_big, w_gates):
    n_a, seq_len, _ = x_a.shape
    n_seq = n_a + x_b.shape[0]
    tm = INPROJ_ROWS
    assert seq_len % tm == 0 and tm >= CHUNK
    n_x_tiles = seq_len // tm
    rows = seq_len + CHUNK
    row_spec = lambda n: pl.BlockSpec((None, tm, n), lambda b, i: (b, i, 0))
    const = lambda a: pl.BlockSpec(a.shape, lambda b, i: (0,) * a.ndim)
    resident = lambda a: pl.BlockSpec(a.shape, lambda b, i: (0,) * a.ndim, pipeline_mode=pl.Buffered(1))
    widths = (1024, CONV_CH, 512, 512, 1024, 1024)
    out_shapes = [jax.ShapeDtypeStruct((n_seq, rows, w), BF16) for w in widths]
    out_shapes += [jax.ShapeDtypeStruct((n_seq, rows, LANES), F32)] * 2
    return pl.pallas_call(
        functools.partial(_inproj_kernel, n_x_tiles=n_x_tiles, n_a=n_a),
        grid=(n_seq, n_x_tiles + 1),
        in_specs=[pl.BlockSpec((None, tm, D_MODEL), lambda b, i: (
                      jnp.minimum(b, n_a - 1),
                      jnp.where(b < n_a, jnp.minimum(i, n_x_tiles - 1), n_x_tiles - 1), 0)),
                  pl.BlockSpec((None, tm, D_MODEL), lambda b, i: (
                      jnp.maximum(b - n_a, 0),
                      jnp.where(b < n_a, 0, jnp.minimum(i, n_x_tiles - 1)), 0)),
                  resident(meta_tile), const(ln_g), const(ln_b), resident(w_big), resident(w_gates)],
        out_specs=[row_spec(s.shape[2]) for s in out_shapes],
        out_shape=out_shapes,
        compiler_params=pltpu.CompilerParams(
            dimension_semantics=("arbitrary", "arbitrary"), vmem_limit_bytes=VMEM_LIMIT_BYTES),
        name="inproj",
    )(x_a, x_b, meta_tile, ln_g, ln_b, w_big, w_gates)


def _conv_kernel(prev_ref, main_ref, next_ref, shift_ref, w_ref, b_ref, xs_ref, bm_ref, cm_ref, *, n_chunks):
    c = pl.program_id(1)
    row = lax.broadcasted_iota(jnp.int32, (CHUNK, 1), 0)
    pad_rows = jnp.logical_and(c == 0, row < PAD_FRONT)
    main = jnp.where(pad_rows, 0.0, main_ref[...].astype(F32))
    prev = jnp.where(c == 0, 0.0, prev_ref[...].astype(F32))
    nxt = jnp.where(c == n_chunks - 1, 0.0, next_ref[...].astype(F32))
    xp = jnp.concatenate([prev, main, nxt], axis=0).astype(BF16)
    w = w_ref[...]
    acc = b_ref[...] + main * w[CONV_HALF:CONV_HALF + 1, :]
    shifted = _dot(shift_ref[...], xp)
    for j, t in enumerate(t for t in range(CONV_W) if t != CONV_HALF):
        acc = acc + shifted[j * CHUNK:(j + 1) * CHUNK, :] * w[t:t + 1, :]
    y = acc * _sigmoid(acc)
    y = jnp.where(pad_rows, 0.0, y)
    xs_ref[...] = y[:, :D_MODEL].astype(BF16)
    bm_ref[...] = y[:, D_MODEL:D_MODEL + 512].astype(BF16)
    cm_ref[...] = y[:, D_MODEL + 512:].astype(BF16)


def _conv(xbc, conv_w8, conv_b, n_chunks):
    n_seq, rows, _ = xbc.shape
    halo_blocks_per_chunk = CHUNK // BF16_SUBLANES

    def main_map(b, c):
        return (b, _storage_chunk(c, n_chunks), 0)

    def prev_map(b, c):
        sc = _storage_chunk(jnp.maximum(c - 1, 0), n_chunks)
        return (b, sc * halo_blocks_per_chunk + halo_blocks_per_chunk - 1, 0)

    def next_map(b, c):
        sc = _storage_chunk(jnp.minimum(c + 1, n_chunks - 1), n_chunks)
        return (b, sc * halo_blocks_per_chunk, 0)

    const = lambda a: pl.BlockSpec(a.shape, lambda b, c: (0,) * a.ndim)
    l_i = jnp.arange(CHUNK, dtype=jnp.int32)[:, None]
    j_i = jnp.arange(CHUNK + 2 * BF16_SUBLANES, dtype=jnp.int32)[None, :]
    shifts = jnp.concatenate([(j_i == BF16_SUBLANES + l_i + t - CONV_HALF)
                              for t in range(CONV_W) if t != CONV_HALF], axis=0).astype(BF16)
    out_shapes = [jax.ShapeDtypeStruct((n_seq, rows, D_MODEL), BF16),
                  jax.ShapeDtypeStruct((n_seq, rows, 512), BF16),
                  jax.ShapeDtypeStruct((n_seq, rows, 512), BF16)]
    return pl.pallas_call(
        functools.partial(_conv_kernel, n_chunks=n_chunks),
        grid=(n_seq, n_chunks),
        in_specs=[pl.BlockSpec((None, BF16_SUBLANES, CONV_CH), prev_map),
                  pl.BlockSpec((None, CHUNK, CONV_CH), main_map),
                  pl.BlockSpec((None, BF16_SUBLANES, CONV_CH), next_map),
                  const(shifts), const(conv_w8), const(conv_b)],
        out_specs=[pl.BlockSpec((None, CHUNK, s.shape[2]), main_map) for s in out_shapes],
        out_shape=out_shapes,
        compiler_params=pltpu.CompilerParams(
            dimension_semantics=("arbitrary", "arbitrary"), vmem_limit_bytes=VMEM_LIMIT_BYTES),
        name="conv",
    )(xbc, xbc, xbc, shifts, conv_w8, conv_b)


def _mixer_kernel(*refs, reverse, final, n_chunks, n_seq):
    if final:
        (xs_ref, bm_ref, cm_ref, g_ref, q_ref, k_ref, v_ref, z_ref, o_ref, yb_ref, hb_ref,
         gbias_ref, alog_ref, expand_ref, dskip_ref, ngs_ref, ngm_ref,
         ycat_ref, s_ref, cst_ref, m_ref) = refs
    else:
        (xs_ref, bm_ref, cm_ref, g_ref, q_ref, k_ref, v_ref,
         gbias_ref, alog_ref, expand_ref,
         yout_ref, hout_ref, s_ref, cst_ref, m_ref) = refs

    t = pl.program_id(0)
    c = (n_chunks - 1 - t) if reverse else t
    end = 0 if reverse else CHUNK - 1

    @pl.when(t == 0)
    def _():
        s_ref[...] = jnp.zeros_like(s_ref)
        cst_ref[...] = jnp.zeros_like(cst_ref)
        m_ref[...] = jnp.zeros_like(m_ref)

    row = lax.broadcasted_iota(jnp.int32, (CHUNK, 1), 0)
    col = lax.broadcasted_iota(jnp.int32, (1, CHUNK), 1)
    lane = col
    allowed = (col >= row) if reverse else (col <= row)
    tri = allowed.astype(BF16)
    tri_t = ((row >= col) if reverse else (row <= col)).astype(BF16)
    feat = lax.broadcasted_iota(jnp.int32, (GATE_END, 1), 0)
    is_dt = feat < GATE_I0
    is_i = jnp.logical_and(feat >= GATE_I0, feat < GATE_F0)
    is_f = feat >= GATE_F0
    pad_cols = jnp.logical_and(c == 0, col < PAD_FRONT)
    a_coef = -jnp.exp(alog_ref[...])
    expand = expand_ref[...]
    left_half = lane < SSD_HEAD_DIM
    right_half = jnp.logical_not(left_half)
    top_half = row < MLSTM_DK
    ones_blk = jnp.ones((CHUNK, MLSTM_DV), BF16)
    full = (CHUNK, LANES)

    def one_sequence(b):
        gr = (g_ref[b] + gbias_ref[...]).T[0:GATE_END, :]
        lse = _log1p_exp_neg_abs(gr)
        val_t = jnp.where(is_dt, jnp.maximum(gr, 0.0) + lse, jnp.where(is_i, gr, jnp.minimum(gr, 0.0) - lse))
        val_t = jnp.where(pad_cols, jnp.where(is_i, NEG_GATE, 0.0), val_t)
        u_t = jnp.where(is_dt, val_t * a_coef, jnp.where(is_f, val_t, 0.0))
        cums_t = _dot_exact_rhs(u_t, tri_t)
        cums_end = jnp.broadcast_to(cums_t[:, end:end + 1], cums_t.shape)
        p1_t = jnp.exp(cums_t)
        p2_t = jnp.exp(cums_end - cums_t) * val_t
        packed = jnp.concatenate([cums_t, p1_t, p2_t, val_t], axis=0).T
        ex = _dot(packed.astype(BF16), expand)
        ex1 = ex[:, :D_MODEL]
        ex2 = ex[:, D_MODEL:]
        chunk_decay = _dot_exact_rhs(jnp.broadcast_to(packed[end:end + 1, :], (8, LANES)),
                                     expand[:, :D_MODEL])[0:1, :]

        xs = xs_ref[b]
        xsf = xs.astype(F32)
        xs_w = (xsf * ex2).astype(BF16)
        y_groups = []
        for g in range(SSD_GROUPS):
            cg = cm_ref[b, :, g * SSD_STATE:(g + 1) * SSD_STATE]
            bg = bm_ref[b, :, g * SSD_STATE:(g + 1) * SSD_STATE]
            cb = _dot_nt(cg, bg)
            ys = []
            for pp in range(HEADS_PER_GROUP // 2):
                pair = g * (HEADS_PER_GROUP // 2) + pp
                xs_pair = xs[:, pair * LANES:(pair + 1) * LANES]
                zero_pair = jnp.zeros_like(xs_pair)
                m_mats = []
                for h in (2 * pair, 2 * pair + 1):
                    seg = jnp.broadcast_to(packed[:, h:h + 1], full) - cums_t[h:h + 1, :]
                    dec = jnp.exp(jnp.where(allowed, seg, -jnp.inf))
                    m_mats.append((cb * dec * val_t[h:h + 1, :]).astype(BF16))
                rhs = jnp.concatenate([jnp.where(left_half, xs_pair, zero_pair),
                                       jnp.where(right_half, xs_pair, zero_pair)], axis=0)
                ys.append(_dot(jnp.concatenate(m_mats, axis=1), rhs))
            y_diag = jnp.concatenate(ys, axis=1)
            gs = slice(g * GROUP_WIDTH, (g + 1) * GROUP_WIDTH)
            s_g = s_old[b][g]
            y_off = _dot(cg, s_g.astype(BF16)) * ex1[:, gs]
            y_groups.append(y_diag + y_off)
            bg_t = bg.astype(F32).T.astype(BF16)
            s_new_all[b].append(chunk_decay[:, gs] * s_g + _dot(bg_t, xs_w[:, gs]))
        y_ssd = jnp.concatenate(y_groups, axis=1)

        bcum_t = cums_t[GATE_F0:GATE_END, :]
        ip_t = val_t[GATE_I0:GATE_F0, :]
        rep = bcum_t.shape
        g_rep = jnp.broadcast_to(bcum_t[:, end:end + 1], rep)
        a_t = g_rep - bcum_t + ip_t
        a_max = jnp.broadcast_to(jnp.max(a_t, axis=1, keepdims=True), rep)
        w_t = jnp.exp(a_t - a_max)
        m_prev = m_old[b]
        m_new = jnp.maximum(g_rep + m_prev, a_max)
        s_prev = jnp.exp(g_rep + m_prev - m_new)
        s_new = jnp.exp(a_max - m_new)
        r_t = ip_t - bcum_t
        h_heads = []
        for pair in range(MLSTM_HEADS // 2):
            h0, h1 = 2 * pair, 2 * pair + 1
            q_pair = q_ref[b, :, pair * LANES:(pair + 1) * LANES]
            k_pair = k_ref[b, :, pair * LANES:(pair + 1) * LANES]
            q_pair_f = q_pair.astype(F32)
            cst = cst_old[b][pair]
            cst_b = cst.astype(BF16)
            v_pair = []
            for hh, h in enumerate((h0, h1)):
                keep = left_half if hh == 0 else right_half
                vh = v_ref[b, :, h * MLSTM_DV:(h + 1) * MLSTM_DV]
                v_pair.append(vh)
                qk = _dot_nt(jnp.where(keep, q_pair, jnp.zeros_like(q_pair)), k_pair)
                bc = jnp.broadcast_to(packed[:, GATE_F0 + h:GATE_F0 + h + 1], full)
                dlog = jnp.where(allowed, bc + r_t[h:h + 1, :], -jnp.inf)
                m_intra = jnp.broadcast_to(jnp.max(dlog, axis=1, keepdims=True), full)
                m_inter = bc + m_prev[h:h + 1, :]
                m_t = jnp.maximum(m_inter, m_intra)
                s_mat = (qk * jnp.exp(dlog - m_t)).astype(BF16)
                qs = (jnp.where(keep, q_pair_f, 0.0) * jnp.exp(m_inter - m_t)).astype(BF16)
                tot = _dot(jnp.concatenate([s_mat, qs], axis=1),
                           jnp.concatenate([jnp.concatenate([vh, ones_blk], axis=1), cst_b], axis=0))
                num = tot[:, :MLSTM_DV]
                den = tot[:, MLSTM_DV:]
                h_heads.append(num / jnp.maximum(jnp.abs(den), jnp.exp(-m_t)))
            w_rows = jnp.where(top_half, w_t[h0:h0 + 1, :], w_t[h1:h1 + 1, :])
            kw = (k_pair.astype(F32).T * w_rows).astype(BF16)
            full_kv = _dot(kw, jnp.concatenate([v_pair[0], v_pair[1], ones_blk], axis=1))
            kvn = jnp.concatenate(
                [jnp.where(top_half, full_kv[:, :MLSTM_DV], full_kv[:, MLSTM_DV:2 * MLSTM_DV]),
                 full_kv[:, 2 * MLSTM_DV:]], axis=1)
            sp_rows = jnp.where(top_half, s_prev[h0:h0 + 1, :], s_prev[h1:h1 + 1, :])
            sn_rows = jnp.where(top_half, s_new[h0:h0 + 1, :], s_new[h1:h1 + 1, :])
            cst_new_all[b].append(jnp.concatenate([sp_rows, sp_rows], axis=1) * cst
                                  + jnp.concatenate([sn_rows, sn_rows], axis=1) * kvn)
        m_new_all[b] = m_new
        h_ml = jnp.concatenate(h_heads, axis=1)
        if not final:
            return y_ssd.astype(BF16), h_ml.astype(BF16)

        y_tot = y_ssd + yb_ref[b].astype(F32) + dskip_ref[...] * xsf
        zz = z_ref[b].astype(F32)
        y2 = y_tot * (zz * _sigmoid(zz))
        y_n = y2 * lax.rsqrt(jnp.mean(y2 * y2, axis=-1, keepdims=True) + RMS_EPS) * ngs_ref[...]
        h_tot = h_ml + hb_ref[b].astype(F32)
        segs = []
        for h in range(MLSTM_HEADS):
            seg = h_tot[:, h * MLSTM_DV:(h + 1) * MLSTM_DV]
            segs.append(seg * lax.rsqrt(jnp.mean(seg * seg, axis=-1, keepdims=True) + RMS_EPS))
        h_n = jnp.concatenate(segs, axis=1) * ngm_ref[...]
        y_ml = _sigmoid(o_ref[b].astype(F32)) * h_n
        return y_n.astype(BF16), y_ml.astype(BF16)

    s_old = [[s_ref[b, g] for g in range(SSD_GROUPS)] for b in range(n_seq)]
    cst_old = [[cst_ref[b, p] for p in range(MLSTM_HEADS // 2)] for b in range(n_seq)]
    m_old = [m_ref[b] for b in range(n_seq)]
    s_new_all = [[] for _ in range(n_seq)]
    cst_new_all = [[] for _ in range(n_seq)]
    m_new_all = [None] * n_seq
    results = [one_sequence(b) for b in range(n_seq)]
    for b in range(n_seq):
        for g in range(SSD_GROUPS):
            s_ref[b, g] = s_new_all[b][g]
        for p in range(MLSTM_HEADS // 2):
            cst_ref[b, p] = cst_new_all[b][p]
        m_ref[b] = m_new_all[b]

    @pl.when(c > 0)
    def _():
        for b, (first, second) in enumerate(results):
            if final:
                ycat_ref[b, :, :D_MODEL] = first
                ycat_ref[b, :, D_MODEL:] = second
            else:
                yout_ref[b] = first
                hout_ref[b] = second


def _mixer_pass(xs, bm, cm, gates, q, k, v, gbias, alog, expand, n_chunks, *, reverse, final_inputs=None):
    final = final_inputs is not None
    n_seq = xs.shape[0]
    seq_len = (n_chunks - 1) * CHUNK

    def chunk_of(t):
        return (n_chunks - 1 - t) if reverse else t

    def pad_map(t):
        return (0, _storage_chunk(chunk_of(t), n_chunks), 0)

    def out_map(t):
        return (0, jnp.maximum(chunk_of(t) - 1, 0), 0)

    const = lambda a: pl.BlockSpec(a.shape, lambda t: (0,) * a.ndim)
    pad_spec = lambda n: pl.BlockSpec((n_seq, CHUNK, n), pad_map)
    out_spec = lambda n: pl.BlockSpec((n_seq, CHUNK, n), out_map)
    in_arrays = [xs, bm, cm, gates, q, k, v]
    in_specs = [pad_spec(a.shape[2]) for a in in_arrays]
    if final:
        z, o, yb, hb, dskip, ngs, ngm = final_inputs
        in_arrays += [z, o, yb, hb]
        in_specs += [pad_spec(1024), pad_spec(1024), out_spec(1024), out_spec(1024)]
        in_arrays += [gbias, alog, expand, dskip, ngs, ngm]
        in_specs += [const(a) for a in (gbias, alog, expand, dskip, ngs, ngm)]
        out_shape = [jax.ShapeDtypeStruct((n_seq, seq_len, 2 * D_MODEL), BF16)]
        out_specs = [out_spec(2 * D_MODEL)]
    else:
        in_arrays += [gbias, alog, expand]
        in_specs += [const(a) for a in (gbias, alog, expand)]
        out_shape = [jax.ShapeDtypeStruct((n_seq, seq_len, D_MODEL), BF16),
                     jax.ShapeDtypeStruct((n_seq, seq_len, D_MODEL), BF16)]
        out_specs = [out_spec(D_MODEL), out_spec(D_MODEL)]
    return pl.pallas_call(
        functools.partial(_mixer_kernel, reverse=reverse, final=final, n_chunks=n_chunks, n_seq=n_seq),
        grid=(n_chunks,),
        in_specs=in_specs,
        out_specs=out_specs,
        out_shape=out_shape,
        scratch_shapes=[pltpu.VMEM((n_seq, SSD_GROUPS, SSD_STATE, GROUP_WIDTH), F32),
                        pltpu.VMEM((n_seq, MLSTM_HEADS // 2, 2 * MLSTM_DK, 2 * MLSTM_DV), F32),
                        pltpu.VMEM((n_seq, MLSTM_HEADS, LANES), F32)],
        compiler_params=pltpu.CompilerParams(
            dimension_semantics=("arbitrary",), vmem_limit_bytes=VMEM_LIMIT_BYTES),
        name="mixer_fwd" if final else "mixer_bwd",
    )(*in_arrays)


def _epilogue_kernel(xa_ref, xb_ref, ycat_ref, wout_ref, lng0_ref, lnb0_ref, lng1_ref, lnb1_ref,
                     wrh_ref, wrl_ref, br_ref, h1_ref, h1p_ref, sel_ref, gate_ref, *, n_tiles_a):
    x = jnp.where(pl.program_id(0) < n_tiles_a, xa_ref[...], xb_ref[...])
    h0 = _layer_norm(x, lng0_ref[...], lnb0_ref[...])
    mix = _dot(ycat_ref[...], wout_ref[...])
    h1 = _layer_norm(DEEPNORM_ALPHA * h0 + mix, lng1_ref[...], lnb1_ref[...])
    h1_ref[...] = h1
    h1p_ref[...] = _pack_bf16_pairs(h1)
    hh = h1.astype(BF16)
    hl = (h1 - hh.astype(F32)).astype(BF16)
    wrh = wrh_ref[...]
    logits = _dot(hh, wrh) + _dot(hl, wrh) + _dot(hh, wrl_ref[...]) + br_ref[...]
    lane = lax.broadcasted_iota(jnp.int32, (1, LANES), 1)
    lane_f = lane.astype(F32)
    logits = jnp.where(lane < N_EXPERTS, logits, -jnp.inf)
    work = logits
    sel = jnp.zeros(logits.shape, jnp.bool_)
    top = None
    for _ in range(TOP_K):
        m = jnp.max(work, axis=-1, keepdims=True)
        if top is None:
            top = m
        first = jnp.min(jnp.where(work == m, lane_f, float(LANES)), axis=-1, keepdims=True)
        pick = lane_f == first
        sel = jnp.logical_or(sel, pick)
        work = jnp.where(pick, -jnp.inf, work)
    e = jnp.where(sel, jnp.exp(logits - top), 0.0)
    gate_ref[...] = e / jnp.sum(e, axis=-1, keepdims=True)
    sel_ref[...] = sel.astype(F32)


def _epilogue(x_a, x_b, ycat, w_out, lng0, lnb0, lng1, lnb1, wrh, wrl, br):
    rows = x_a.shape[0] + x_b.shape[0]
    tm = EPILOGUE_ROWS
    assert x_a.shape[0] % tm == 0 and x_b.shape[0] % tm == 0
    n_tiles_a = x_a.shape[0] // tm
    row_spec = lambda n: pl.BlockSpec((tm, n), lambda i: (i, 0))
    const = lambda a: pl.BlockSpec(a.shape, lambda i: (0,) * a.ndim)
    out_shape = [jax.ShapeDtypeStruct((rows, D_MODEL), F32),
                 jax.ShapeDtypeStruct((rows, D_MODEL // 2), jnp.uint32),
                 jax.ShapeDtypeStruct((rows, LANES), F32),
                 jax.ShapeDtypeStruct((rows, LANES), F32)]
    consts = (w_out, lng0, lnb0, lng1, lnb1, wrh, wrl, br)
    return pl.pallas_call(
        functools.partial(_epilogue_kernel, n_tiles_a=n_tiles_a),
        grid=(rows // tm,),
        in_specs=[pl.BlockSpec((tm, D_MODEL), lambda i: (jnp.minimum(i, n_tiles_a - 1), 0)),
                  pl.BlockSpec((tm, D_MODEL), lambda i: (jnp.maximum(i - n_tiles_a, 0), 0)),
                  row_spec(2 * D_MODEL)] + [const(a) for a in consts],
        out_specs=[row_spec(D_MODEL), row_spec(D_MODEL // 2), row_spec(LANES), row_spec(LANES)],
        out_shape=out_shape,
        compiler_params=pltpu.CompilerParams(
            dimension_semantics=("arbitrary",), vmem_limit_bytes=VMEM_LIMIT_BYTES),
        name="epilogue",
    )(x_a, x_b, ycat, *consts)


def _rank_kernel(sel_ref, gate_ref, lstrict_ref, ucum_ref, dest_ref, gk_ref, stats_ref,
                 base_ref, *, trash_row):
    phase = pl.program_id(0)
    i = pl.program_id(1)
    sel = sel_ref[...]
    colsum = jnp.sum(sel, axis=0, keepdims=True)

    @pl.when(jnp.logical_and(phase == 0, i == 0))
    def _():
        base_ref[...] = jnp.zeros_like(base_ref)

    @pl.when(phase == 0)
    def _():
        base_ref[0:1, :] = base_ref[0:1, :] + colsum

    @pl.when(jnp.logical_and(phase == 1, i == 0))
    def _():
        counts = base_ref[0:1, :]
        padded = jnp.ceil(counts / MOE_BLOCK) * MOE_BLOCK
        pend = _dot_exact_rhs(jnp.broadcast_to(padded, (8, LANES)), ucum_ref[...])[0:1, :]
        stats_ref[0:1, :] = counts
        stats_ref[1:2, :] = pend - padded
        stats_ref[2:3, :] = pend
        stats_ref[3:8, :] = jnp.zeros((5, LANES), F32)
        base_ref[1:2, :] = pend - padded

    @pl.when(phase == 1)
    def _():
        before = _dot(lstrict_ref[...], sel.astype(BF16))
        pos = base_ref[1:2, :] + before
        base_ref[1:2, :] = base_ref[1:2, :] + colsum
        work = jnp.where(sel > 0.0, pos + 1.0, 0.0)
        gates = gate_ref[...]
        for kk in range(TOP_K):
            m = jnp.max(work, axis=-1, keepdims=True)
            pick = jnp.logical_and(work == m, m > 0.0)
            gk_ref[:, kk:kk + 1] = jnp.sum(jnp.where(pick, gates, 0.0), axis=-1, keepdims=True)
            dest_ref[:, kk:kk + 1] = jnp.where(m > 0.0, m - 1.0, float(trash_row)).astype(jnp.int32)
            work = jnp.where(pick, 0.0, work)


def _rank(sel, gates, lstrict, ucum, trash_row):
    rows = sel.shape[0]
    tm = RANK_ROWS
    assert rows % tm == 0
    row_spec = lambda n: pl.BlockSpec((tm, n), lambda p, i: (i, 0))
    out_row_spec = lambda n: pl.BlockSpec((tm, n), lambda p, i: (i * p, 0))
    const = lambda a: pl.BlockSpec(a.shape, lambda p, i: (0,) * a.ndim)
    return pl.pallas_call(
        functools.partial(_rank_kernel, trash_row=trash_row),
        grid=(2, rows // tm),
        in_specs=[row_spec(LANES), row_spec(LANES), const(lstrict), const(ucum)],
        out_specs=[out_row_spec(TOP_K), out_row_spec(TOP_K),
                   pl.BlockSpec((8, LANES), lambda p, i: (0, 0))],
        out_shape=[jax.ShapeDtypeStruct((rows, TOP_K), jnp.int32),
                   jax.ShapeDtypeStruct((rows, TOP_K), F32),
                   jax.ShapeDtypeStruct((8, LANES), F32)],
        scratch_shapes=[pltpu.VMEM((8, LANES), F32)],
        compiler_params=pltpu.CompilerParams(
            dimension_semantics=("arbitrary", "arbitrary"), vmem_limit_bytes=VMEM_LIMIT_BYTES),
        name="rank",
    )(sel, gates, lstrict, ucum)


def _sc_mesh():
    return plsc.VectorSubcoreMesh(core_axis_name="c", subcore_axis_name="s",
                                  num_cores=SC_CORES, num_subcores=SC_SUBCORES)


def _sc_scatter_rows(src, idx, n_out_rows):
    n_src, d = src.shape
    w, k = SC_WINDOW, SC_SCATTER_INFLIGHT
    assert idx.shape == (n_src // w, TOP_K, w) and n_src % (w * k * SC_WORKERS) == 0
    per_worker = n_src // (w * SC_WORKERS)
    idx2d = idx.reshape(n_src // w * TOP_K, w)

    @functools.partial(
        pl.kernel, mesh=_sc_mesh(),
        out_type=jax.ShapeDtypeStruct((n_out_rows, d), src.dtype),
        scratch_types=[pltpu.VMEM((k * TOP_K, w), jnp.int32), pltpu.VMEM((k, w, d), src.dtype),
                       pltpu.SemaphoreType.DMA((k,)), pltpu.SemaphoreType.DMA((k,))],
        name="sc_scatter_rows")
    def body(src_hbm, idx_hbm, out_hbm, idx_v, rows_v, sem_load, sem_scatter):
        wid = lax.axis_index("s") * SC_CORES + lax.axis_index("c")

        @pl.loop(0, per_worker // k)
        def _(i):
            win0 = wid * per_worker + i * k
            loads = [pltpu.async_copy(src_hbm.at[pl.ds(pl.multiple_of((win0 + b) * w, w), w)], rows_v.at[b],
                                      sem_load.at[b]) for b in range(k)]
            pltpu.sync_copy(idx_hbm.at[pl.ds(pl.multiple_of(win0 * TOP_K, TOP_K), k * TOP_K)], idx_v)
            scatters = []
            for b in range(k):
                loads[b].wait()
                scatters += [pltpu.async_copy(rows_v.at[b], out_hbm.at[idx_v.at[b * TOP_K + kk]],
                                              sem_scatter.at[b]) for kk in range(TOP_K)]
            for copy in scatters:
                copy.wait()

    return body(src, idx2d)


def _sc_gather_rows(table, idx):
    d = table.shape[1]
    w, k = SC_WINDOW, SC_GATHER_INFLIGHT
    assert idx.shape[0] % (w * k * SC_WORKERS) == 0
    per_worker = idx.shape[0] // (w * SC_WORKERS)
    idx2d = idx.reshape(-1, w)

    @functools.partial(
        pl.kernel, mesh=_sc_mesh(),
        out_type=jax.ShapeDtypeStruct((idx.shape[0], d), table.dtype),
        scratch_types=[pltpu.VMEM((k, w), jnp.int32), pltpu.VMEM((k, w, d), table.dtype),
                       pltpu.SemaphoreType.DMA((k,)), pltpu.SemaphoreType.DMA((k,))],
        name="sc_gather_rows")
    def body(table_hbm, idx_hbm, out_hbm, idx_v, rows_v, sem_gather, sem_store):
        wid = lax.axis_index("s") * SC_CORES + lax.axis_index("c")

        @pl.loop(0, per_worker // k)
        def _(i):
            win0 = wid * per_worker + i * k
            pltpu.sync_copy(idx_hbm.at[pl.ds(pl.multiple_of(win0, k), k)], idx_v)
            gathers = [pltpu.async_copy(table_hbm.at[idx_v.at[b]], rows_v.at[b], sem_gather.at[b])
                       for b in range(k)]
            stores = []
            for b in range(k):
                gathers[b].wait()
                stores.append(pltpu.async_copy(
                    rows_v.at[b], out_hbm.at[pl.ds(pl.multiple_of((win0 + b) * w, w), w)], sem_store.at[b]))
            for copy in stores:
                copy.wait()

    return body(table, idx2d)


def _ffn_kernel(blk_ref, exp_ref, valid_ref, nused_ref, xb_ref, w1_ref, b1_ref, w2_ref, b2_ref, yb_ref,
                w1b_ref, w2b_ref):
    j = pl.program_id(0)
    active = j < nused_ref[0]

    @pl.when(jnp.logical_and(active, jnp.logical_or(j == 0, exp_ref[j] != exp_ref[jnp.maximum(j - 1, 0)])))
    def _():
        w1b_ref[...] = w1_ref[0].astype(BF16)
        w2b_ref[...] = w2_ref[0].astype(BF16)

    @pl.when(active)
    def _():
        row = lax.broadcasted_iota(jnp.int32, (MOE_BLOCK, 1), 0)
        x = jnp.where(row < valid_ref[j], _unpack_bf16_pairs(xb_ref[...]), 0.0).astype(BF16)
        hc = _dot(x, w1b_ref[...]) + b1_ref[0]
        gate = jnp.minimum(hc[:, :D_FF], SWIGLU_LIMIT)
        up = jnp.clip(hc[:, D_FF:], -SWIGLU_LIMIT, SWIGLU_LIMIT)
        act = (up + 1.0) * gate * _sigmoid(SWIGLU_ALPHA * gate)
        yb_ref[...] = _pack_bf16_pairs(_dot(act.astype(BF16), w2b_ref[...]) + b2_ref[0])


def _ffn(blk_idx, blk_exp, blk_valid, n_used, xb, w1, b1, w2, b2, n_blocks):
    bm = MOE_BLOCK
    grid_spec = pltpu.PrefetchScalarGridSpec(
        num_scalar_prefetch=4,
        grid=(n_blocks,),
        in_specs=[pl.BlockSpec((bm, D_MODEL // 2), lambda j, bi, be, bv, nu: (bi[j], 0)),
                  pl.BlockSpec((1, D_MODEL, 2 * D_FF), lambda j, bi, be, bv, nu: (be[j], 0, 0)),
                  pl.BlockSpec((1, 1, 2 * D_FF), lambda j, bi, be, bv, nu: (be[j], 0, 0)),
                  pl.BlockSpec((1, D_FF, D_MODEL), lambda j, bi, be, bv, nu: (be[j], 0, 0)),
                  pl.BlockSpec((1, 1, D_MODEL), lambda j, bi, be, bv, nu: (be[j], 0, 0))],
        out_specs=pl.BlockSpec((bm, D_MODEL // 2), lambda j, bi, be, bv, nu: (bi[j], 0)),
        scratch_shapes=[pltpu.VMEM((D_MODEL, 2 * D_FF), BF16), pltpu.VMEM((D_FF, D_MODEL), BF16)],
    )
    return pl.pallas_call(
        _ffn_kernel,
        grid_spec=grid_spec,
        out_shape=jax.ShapeDtypeStruct(xb.shape, jnp.uint32),
        compiler_params=pltpu.CompilerParams(
            dimension_semantics=("arbitrary",), vmem_limit_bytes=VMEM_LIMIT_BYTES),
        name="expert_ffn",
    )(blk_idx, blk_exp, blk_valid, n_used, xb, w1, b1, w2, b2)


def _combine_kernel(gk_ref, h1_ref, y0_ref, y1_ref, y2_ref, y3_ref, lng_ref, lnb_ref, outa_ref, outb_ref,
                    *, n_tiles_a):
    gk = gk_ref[...]
    ffn = gk[:, 0:1] * _unpack_bf16_pairs(y0_ref[...])
    for kk, y_ref in enumerate((y1_ref, y2_ref, y3_ref), start=1):
        ffn = ffn + gk[:, kk:kk + 1] * _unpack_bf16_pairs(y_ref[...])
    out = _layer_norm(DEEPNORM_ALPHA * h1_ref[...] + ffn, lng_ref[...], lnb_ref[...])
    i = pl.program_id(0)

    @pl.when(i < n_tiles_a)
    def _():
        outa_ref[...] = out

    @pl.when(i >= n_tiles_a)
    def _():
        outb_ref[...] = out


def _combine(gk, h1, ysel, lng, lnb, rows_a):
    rows = h1.shape[0]
    tm = COMBINE_ROWS
    assert rows % tm == 0 and rows_a % tm == 0 and 0 < rows_a < rows
    n_tiles_a = rows_a // tm
    n_tiles = rows // tm
    const = lambda a: pl.BlockSpec(a.shape, lambda i: (0,) * a.ndim)
    ksel = lambda kk: pl.BlockSpec((tm, D_MODEL // 2), lambda i: (kk * n_tiles + i, 0))
    return pl.pallas_call(
        functools.partial(_combine_kernel, n_tiles_a=n_tiles_a),
        grid=(n_tiles,),
        in_specs=[pl.BlockSpec((tm, TOP_K), lambda i: (i, 0)),
                  pl.BlockSpec((tm, D_MODEL), lambda i: (i, 0)),
                  ksel(0), ksel(1), ksel(2), ksel(3),
                  const(lng), const(lnb)],
        out_specs=[pl.BlockSpec((tm, D_MODEL), lambda i: (jnp.minimum(i, n_tiles_a - 1), 0)),
                   pl.BlockSpec((tm, D_MODEL), lambda i: (jnp.maximum(i - n_tiles_a, 0), 0))],
        out_shape=[jax.ShapeDtypeStruct((rows_a, D_MODEL), F32),
                   jax.ShapeDtypeStruct((rows - rows_a, D_MODEL), F32)],
        compiler_params=pltpu.CompilerParams(
            dimension_semantics=("arbitrary",), vmem_limit_bytes=VMEM_LIMIT_BYTES),
        name="combine",
    )(gk, h1, ysel, ysel, ysel, ysel, lng, lnb)


def _row(v, width=None):
    v = v.reshape(1, -1).astype(F32)
    if width is not None and v.shape[1] < width:
        v = jnp.pad(v, ((0, 0), (0, width - v.shape[1])))
    return v


def _encode_all(x_a, x_b, meta_tokens, ln_emb_g, ln_emb_b, w_in, conv_w, conv_b, dt_bias, a_log,
                d_skip, ssd_norm_g, i_bias, f_bias, mlstm_norm_g, w_out, ln1_g, ln1_b, w_router, b_router,
                w1, b1, w2, b2, ln2_g, ln2_b):
    n_a, seq_len, _ = x_a.shape
    n_seq = n_a + x_b.shape[0]
    rows_a = n_a * seq_len
    assert seq_len % CHUNK == 0
    n_chunks = seq_len // CHUNK + 1
    n_tok = n_seq * seq_len

    sizes = (1024, CONV_CH, 2 * SSD_HEADS, 512, 512, 1024, 1024, 2 * MLSTM_HEADS, 2 * MLSTM_HEADS)
    offs = [0]
    for s in sizes:
        offs.append(offs[-1] + s)
    w_z, w_xbc, w_dt, w_q, w_k, w_v, w_o, w_i, w_f = [w_in[:, offs[j]:offs[j + 1]] for j in range(9)]
    w_big = jnp.concatenate([w_z, w_xbc, w_q, w_k * (MLSTM_DK ** -0.5), w_v, w_o], axis=1).astype(BF16)
    zpad = jnp.zeros((D_MODEL, LANES - GATE_END), F32)
    gate_cols = []
    for d in range(2):
        gate_cols += [w_dt[:, d * SSD_HEADS:(d + 1) * SSD_HEADS],
                      w_i[:, d * MLSTM_HEADS:(d + 1) * MLSTM_HEADS],
                      w_f[:, d * MLSTM_HEADS:(d + 1) * MLSTM_HEADS], zpad]
    w_gates = jnp.concatenate(gate_cols, axis=1).astype(BF16)
    gbias = [_row(jnp.concatenate([dt_bias[d], i_bias[d], f_bias[d]]), LANES) for d in range(2)]
    alog = [jnp.pad(jnp.broadcast_to(a_log[d].astype(F32)[:, None], (SSD_HEADS, LANES)),
                    ((0, GATE_END - SSD_HEADS), (0, 0))) for d in range(2)]
    head_of_col = jnp.arange(D_MODEL, dtype=jnp.int32) // SSD_HEAD_DIM
    lane_id = jnp.arange(LANES, dtype=jnp.int32)[:, None]
    expand = jnp.concatenate([lane_id == GATE_END + head_of_col[None, :],
                              lane_id == 2 * GATE_END + head_of_col[None, :]], axis=1).astype(BF16)
    dskip = _row(jnp.repeat(d_skip, SSD_HEAD_DIM))
    conv_w8 = jnp.pad(conv_w.astype(F32), ((0, 8 - CONV_W), (0, 0)))
    meta_tile = jnp.pad(meta_tokens.astype(F32), ((PAD_FRONT, INPROJ_ROWS - CHUNK), (0, 0)))

    z, xbc, q, k, v, o, gates_f, gates_b = _inproj(x_a, x_b, meta_tile, _row(ln_emb_g), _row(ln_emb_b),
                                                   w_big, w_gates)
    xs, bm, cm = _conv(xbc, conv_w8, _row(conv_b), n_chunks)
    yb, hb = _mixer_pass(xs, bm, cm, gates_b, q, k, v, gbias[1], alog[1], expand, n_chunks, reverse=True)
    (ycat,) = _mixer_pass(xs, bm, cm, gates_f, q, k, v, gbias[0], alog[0], expand, n_chunks,
                          reverse=False,
                          final_inputs=(z, o, yb, hb, dskip, _row(ssd_norm_g), _row(mlstm_norm_g)))

    wr = jnp.pad(w_router.astype(F32), ((0, 0), (0, LANES - N_EXPERTS)))
    wrh = wr.astype(BF16)
    wrl = (wr - wrh.astype(F32)).astype(BF16)
    h1, h1p, sel, gates = _epilogue(x_a.reshape(rows_a, D_MODEL), x_b.reshape(n_tok - rows_a, D_MODEL),
                                    ycat.reshape(n_tok, 2 * D_MODEL),
                               w_out.astype(BF16), _row(ln_emb_g), _row(ln_emb_b), _row(ln1_g),
                               _row(ln1_b), wrh, wrl, _row(b_router, LANES))

    n_blocks = n_tok * TOP_K // MOE_BLOCK + N_EXPERTS
    trash_row = n_blocks * MOE_BLOCK
    n_buf_rows = trash_row + MOE_BLOCK
    r_i = jnp.arange(RANK_ROWS, dtype=jnp.int32)
    lstrict = (r_i[None, :] < r_i[:, None]).astype(BF16)
    l_i = jnp.arange(LANES, dtype=jnp.int32)
    ucum = (l_i[:, None] <= l_i[None, :]).astype(BF16)
    dest, gk, stats = _rank(sel, gates, lstrict, ucum, trash_row)
    counts = stats[0, :N_EXPERTS].astype(jnp.int32)
    starts = stats[1, :N_EXPERTS].astype(jnp.int32)
    pends = stats[2, :N_EXPERTS].astype(jnp.int32)
    n_used = pends[N_EXPERTS - 1] // MOE_BLOCK
    blk = jnp.minimum(jnp.arange(n_blocks, dtype=jnp.int32), jnp.maximum(n_used - 1, 0))
    blk_exp = jnp.minimum(jnp.sum((pends[None, :] <= (blk * MOE_BLOCK)[:, None]).astype(jnp.int32), axis=1),
                          N_EXPERTS - 1).astype(jnp.int32)

    blk_last = jnp.take(starts + counts, blk_exp)
    blk_valid = jnp.clip(blk_last - blk * MOE_BLOCK, 0, MOE_BLOCK).astype(jnp.int32)
    dest_km = dest.T.reshape(-1)

    dest_wm = dest.reshape(n_tok // SC_WINDOW, SC_WINDOW, TOP_K).transpose(0, 2, 1)
    xb = _sc_scatter_rows(h1p, dest_wm, n_buf_rows)
    yexp = _ffn(blk, blk_exp, blk_valid, n_used.reshape(1), xb, w1,
                b1.reshape(N_EXPERTS, 1, -1), w2, b2.reshape(N_EXPERTS, 1, -1), n_blocks)
    ysel = _sc_gather_rows(yexp, dest_km)
    return _combine(gk, h1, ysel, _row(ln2_g), _row(ln2_b), rows_a)


def kernel(x_prompt, x_sample, meta_tokens, ln_emb_g, ln_emb_b, w_in, conv_w, conv_b, dt_bias, a_log,
           d_skip, ssd_norm_g, i_bias, f_bias, mlstm_norm_g, w_out, ln1_g, ln1_b, w_router, b_router,
           w1, b1, w2, b2, ln2_g, ln2_b):
    assert x_prompt.shape[1:] == x_sample.shape[1:]
    n_p, seq_len, d = x_prompt.shape
    n_s = x_sample.shape[0]
    y_p, y_s = _encode_all(x_prompt.astype(F32), x_sample.astype(F32), meta_tokens, ln_emb_g, ln_emb_b, w_in[0], conv_w[0],
                           conv_b[0], dt_bias[0], a_log[0], d_skip[0], ssd_norm_g[0], i_bias[0], f_bias[0],
                           mlstm_norm_g[0], w_out[0], ln1_g[0], ln1_b[0], w_router[0], b_router[0],
                           w1[0], b1[0], w2[0], b2[0], ln2_g[0], ln2_b[0])
    return (y_p.reshape(n_p, seq_len, d), y_s.reshape(n_s, seq_len, d))
```

```python
import functools

import jax
import jax.numpy as jnp
from jax import lax
from jax.experimental import pallas as pl
from jax.experimental.pallas import tpu as pltpu
from jax.experimental.pallas import tpu_sc as plsc

F32 = jnp.float32
BF16 = jnp.bfloat16

D_MODEL = 1024
N_META = 16
CHUNK = 128
PAD_FRONT = CHUNK - N_META
SSD_HEADS = 16
SSD_HEAD_DIM = 64
SSD_GROUPS = 4
SSD_STATE = 128
HEADS_PER_GROUP = SSD_HEADS // SSD_GROUPS
GROUP_WIDTH = HEADS_PER_GROUP * SSD_HEAD_DIM
CONV_W = 5
CONV_HALF = CONV_W // 2
CONV_CH = D_MODEL + 2 * SSD_GROUPS * SSD_STATE
MLSTM_HEADS = 8
MLSTM_DK = 64
MLSTM_DV = 128
N_EXPERTS = 32
TOP_K = 4
D_FF = D_MODEL
SWIGLU_LIMIT = 7.0
SWIGLU_ALPHA = 1.702
DEEPNORM_ALPHA = 2.0 ** 0.25
LN_EPS = 1e-5
RMS_EPS = 1e-5
NEG_GATE = -1e30

LANES = 128
BF16_SUBLANES = 16
VMEM_LIMIT_BYTES = 56 * 1024 * 1024

GATE_DT0, GATE_I0, GATE_F0, GATE_END = 0, SSD_HEADS, SSD_HEADS + MLSTM_HEADS, SSD_HEADS + 2 * MLSTM_HEADS

INPROJ_ROWS = 512
EPILOGUE_ROWS = 512
RANK_ROWS = 512
COMBINE_ROWS = 512
MOE_BLOCK = 512

SC_CORES = 2
SC_SUBCORES = 16
SC_WORKERS = SC_CORES * SC_SUBCORES
SC_WINDOW = 32
SC_SCATTER_INFLIGHT = 2
SC_GATHER_INFLIGHT = 4


def _dot(a, b):
    return jnp.dot(a, b, preferred_element_type=F32)


def _dot_nt(a, b):
    return lax.dot_general(a, b, (((1,), (1,)), ((), ())), preferred_element_type=F32)


def _split3(x):
    hi = x.astype(BF16)
    r1 = x - hi.astype(F32)
    mid = r1.astype(BF16)
    lo = (r1 - mid.astype(F32)).astype(BF16)
    return hi, mid, lo


def _dot_exact_lhs(a_bf16, x):
    hi, mid, lo = _split3(x)
    return _dot(a_bf16, hi) + _dot(a_bf16, mid) + _dot(a_bf16, lo)


def _dot_exact_rhs(x, b_bf16):
    hi, mid, lo = _split3(x)
    return _dot(hi, b_bf16) + _dot(mid, b_bf16) + _dot(lo, b_bf16)


def _layer_norm(x, g, b):
    mu = jnp.mean(x, axis=-1, keepdims=True)
    xc = x - mu
    var = jnp.mean(xc * xc, axis=-1, keepdims=True)
    return xc * lax.rsqrt(var + LN_EPS) * g + b


def _sigmoid(x):
    return 1.0 / (1.0 + jnp.exp(-x))


def _log1p_exp_neg_abs(x):
    return jnp.log(1.0 + jnp.exp(-jnp.abs(x)))


def _pack_bf16_pairs(x):
    n = x.shape[1] // 2
    bits = lax.bitcast_convert_type(x.astype(BF16).astype(F32), jnp.uint32)
    return (bits[:, :n] >> 16) | bits[:, n:]


def _unpack_bf16_pairs(words):
    lo = lax.bitcast_convert_type(words << 16, F32)
    hi = lax.bitcast_convert_type(words & jnp.uint32(0xFFFF0000), F32)
    return jnp.concatenate([lo, hi], axis=1)


def _storage_chunk(c, n_chunks):
    return jnp.where(c == 0, n_chunks - 1, c - 1)


def _inproj_kernel(xa_ref, xb_ref, meta_ref, g_ref, b_ref, wbig_ref, wg_ref,
                   z_ref, xbc_ref, q_ref, k_ref, v_ref, o_ref, gf_ref, gb_ref, *, n_x_tiles, n_a):
    x = jnp.where(pl.program_id(0) < n_a, xa_ref[...], xb_ref[...])
    x = jnp.where(pl.program_id(1) == n_x_tiles, meta_ref[...], x)
    h = _layer_norm(x, g_ref[...], b_ref[...]).astype(BF16)

    def mm(c0, c1):
        return _dot(h, wbig_ref[:, c0:c1]).astype(BF16)

    z_ref[...] = mm(0, 1024)
    xbc_ref[:, 0:1024] = mm(1024, 2048)
    xbc_ref[:, 1024:2048] = mm(2048, 3072)
    q_ref[...] = mm(3072, 3584)
    k_ref[...] = mm(3584, 4096)
    v_ref[...] = mm(4096, 5120)
    o_ref[...] = mm(5120, 6144)
    gates = _dot(h, wg_ref[...])
    gf_ref[...] = gates[:, :LANES]
    gb_ref[...] = gates[:, LANES:]


def _inproj(x_a, x_b, meta_tile, ln_g, ln_b, w_big, w_gates):
    n_a, seq_len, _ = x_a.shape
    n_seq = n_a + x_b.shape[0]
    tm = INPROJ_ROWS
    assert seq_len % tm == 0 and tm >= CHUNK
    n_x_tiles = seq_len // tm
    rows = seq_len + CHUNK
    row_spec = lambda n: pl.BlockSpec((None, tm, n), lambda b, i: (b, i, 0))
    const = lambda a: pl.BlockSpec(a.shape, lambda b, i: (0,) * a.ndim)
    resident = lambda a: pl.BlockSpec(a.shape, lambda b, i: (0,) * a.ndim, pipeline_mode=pl.Buffered(1))
    widths = (1024, CONV_CH, 512, 512, 1024, 1024)
    out_shapes = [jax.ShapeDtypeStruct((n_seq, rows, w), BF16) for w in widths]
    out_shapes += [jax.ShapeDtypeStruct((n_seq, rows, LANES), F32)] * 2
    return pl.pallas_call(
        functools.partial(_inproj_kernel, n_x_tiles=n_x_tiles, n_a=n_a),
        grid=(n_seq, n_x_tiles + 1),
        in_specs=[pl.BlockSpec((None, tm, D_MODEL), lambda b, i: (
                      jnp.minimum(b, n_a - 1),
                      jnp.where(b < n_a, jnp.minimum(i, n_x_tiles - 1), n_x_tiles - 1), 0)),
                  pl.BlockSpec((None, tm, D_MODEL), lambda b, i: (
                      jnp.maximum(b - n_a, 0),
                      jnp.where(b < n_a, 0, jnp.minimum(i, n_x_tiles - 1)), 0)),
                  resident(meta_tile), const(ln_g), const(ln_b), resident(w_big), resident(w_gates)],
        out_specs=[row_spec(s.shape[2]) for s in out_shapes],
        out_shape=out_shapes,
        compiler_params=pltpu.CompilerParams(
            dimension_semantics=("arbitrary", "arbitrary"), vmem_limit_bytes=VMEM_LIMIT_BYTES),
        name="inproj",
    )(x_a, x_b, meta_tile, ln_g, ln_b, w_big, w_gates)


def _conv_kernel(prev_ref, main_ref, next_ref, shift_ref, w_ref, b_ref, xs_ref, bm_ref, cm_ref, *, n_chunks):
    c = pl.program_id(1)
    row = lax.broadcasted_iota(jnp.int32, (CHUNK, 1), 0)
    pad_rows = jnp.logical_and(c == 0, row < PAD_FRONT)
    main = jnp.where(pad_rows, 0.0, main_ref[...].astype(F32))
    prev = jnp.where(c == 0, 0.0, prev_ref[...].astype(F32))
    nxt = jnp.where(c == n_chunks - 1, 0.0, next_ref[...].astype(F32))
    xp = jnp.concatenate([prev, main, nxt], axis=0).astype(BF16)
    w = w_ref[...]
    acc = b_ref[...] + main * w[CONV_HALF:CONV_HALF + 1, :]
    shifted = _dot(shift_ref[...], xp)
    for j, t in enumerate(t for t in range(CONV_W) if t != CONV_HALF):
        acc = acc + shifted[j * CHUNK:(j + 1) * CHUNK, :] * w[t:t + 1, :]
    y = acc * _sigmoid(acc)
    y = jnp.where(pad_rows, 0.0, y)
    xs_ref[...] = y[:, :D_MODEL].astype(BF16)
    bm_ref[...] = y[:, D_MODEL:D_MODEL + 512].astype(BF16)
    cm_ref[...] = y[:, D_MODEL + 512:].astype(BF16)


def _conv(xbc, conv_w8, conv_b, n_chunks):
    n_seq, rows, _ = xbc.shape
    halo_blocks_per_chunk = CHUNK // BF16_SUBLANES

    def main_map(b, c):
        return (b, _storage_chunk(c, n_chunks), 0)

    def prev_map(b, c):
        sc = _storage_chunk(jnp.maximum(c - 1, 0), n_chunks)
        return (b, sc * halo_blocks_per_chunk + halo_blocks_per_chunk - 1, 0)

    def next_map(b, c):
        sc = _storage_chunk(jnp.minimum(c + 1, n_chunks - 1), n_chunks)
        return (b, sc * halo_blocks_per_chunk, 0)

    const = lambda a: pl.BlockSpec(a.shape, lambda b, c: (0,) * a.ndim)
    l_i = jnp.arange(CHUNK, dtype=jnp.int32)[:, None]
    j_i = jnp.arange(CHUNK + 2 * BF16_SUBLANES, dtype=jnp.int32)[None, :]
    shifts = jnp.concatenate([(j_i == BF16_SUBLANES + l_i + t - CONV_HALF)
                              for t in range(CONV_W) if t != CONV_HALF], axis=0).astype(BF16)
    out_shapes = [jax.ShapeDtypeStruct((n_seq, rows, D_MODEL), BF16),
                  jax.ShapeDtypeStruct((n_seq, rows, 512), BF16),
                  jax.ShapeDtypeStruct((n_seq, rows, 512), BF16)]
    return pl.pallas_call(
        functools.partial(_conv_kernel, n_chunks=n_chunks),
        grid=(n_seq, n_chunks),
        in_specs=[pl.BlockSpec((None, BF16_SUBLANES, CONV_CH), prev_map),
                  pl.BlockSpec((None, CHUNK, CONV_CH), main_map),
                  pl.BlockSpec((None, BF16_SUBLANES, CONV_CH), next_map),
                  const(shifts), const(conv_w8), const(conv_b)],
        out_specs=[pl.BlockSpec((None, CHUNK, s.shape[2]), main_map) for s in out_shapes],
        out_shape=out_shapes,
        compiler_params=pltpu.CompilerParams(
            dimension_semantics=("arbitrary", "arbitrary"), vmem_limit_bytes=VMEM_LIMIT_BYTES),
        name="conv",
    )(xbc, xbc, xbc, shifts, conv_w8, conv_b)


def _mixer_kernel(*refs, reverse, final, n_chunks, n_seq):
    if final:
        (xs_ref, bm_ref, cm_ref, g_ref, q_ref, k_ref, v_ref, z_ref, o_ref, yb_ref, hb_ref,
         gbias_ref, alog_ref, expand_ref, dskip_ref, ngs_ref, ngm_ref,
         ycat_ref, s_ref, cst_ref, m_ref) = refs
    else:
        (xs_ref, bm_ref, cm_ref, g_ref, q_ref, k_ref, v_ref,
         gbias_ref, alog_ref, expand_ref,
         yout_ref, hout_ref, s_ref, cst_ref, m_ref) = refs

    t = pl.program_id(0)
    c = (n_chunks - 1 - t) if reverse else t
    end = 0 if reverse else CHUNK - 1

    @pl.when(t == 0)
    def _():
        s_ref[...] = jnp.zeros_like(s_ref)
        cst_ref[...] = jnp.zeros_like(cst_ref)
        m_ref[...] = jnp.zeros_like(m_ref)

    row = lax.broadcasted_iota(jnp.int32, (CHUNK, 1), 0)
    col = lax.broadcasted_iota(jnp.int32, (1, CHUNK), 1)
    lane = col
    allowed = (col >= row) if reverse else (col <= row)
    tri = allowed.astype(BF16)
    tri_t = ((row >= col) if reverse else (row <= col)).astype(BF16)
    feat = lax.broadcasted_iota(jnp.int32, (GATE_END, 1), 0)
    is_dt = feat < GATE_I0
    is_i = jnp.logical_and(feat >= GATE_I0, feat < GATE_F0)
    is_f = feat >= GATE_F0
    pad_cols = jnp.logical_and(c == 0, col < PAD_FRONT)
    a_coef = -jnp.exp(alog_ref[...])
    expand = expand_ref[...]
    left_half = lane < SSD_HEAD_DIM
    right_half = jnp.logical_not(left_half)
    top_half = row < MLSTM_DK
    ones_blk = jnp.ones((CHUNK, MLSTM_DV), BF16)
    full = (CHUNK, LANES)

    def one_sequence(b):
        gr = (g_ref[b] + gbias_ref[...]).T[0:GATE_END, :]
        lse = _log1p_exp_neg_abs(gr)
        val_t = jnp.where(is_dt, jnp.maximum(gr, 0.0) + lse, jnp.where(is_i, gr, jnp.minimum(gr, 0.0) - lse))
        val_t = jnp.where(pad_cols, jnp.where(is_i, NEG_GATE, 0.0), val_t)
        u_t = jnp.where(is_dt, val_t * a_coef, jnp.where(is_f, val_t, 0.0))
        cums_t = _dot_exact_rhs(u_t, tri_t)
        cums_end = jnp.broadcast_to(cums_t[:, end:end + 1], cums_t.shape)
        p1_t = jnp.exp(cums_t)
        p2_t = jnp.exp(cums_end - cums_t) * val_t
        packed = jnp.concatenate([cums_t, p1_t, p2_t, val_t], axis=0).T
        ex = _dot(packed.astype(BF16), expand)
        ex1 = ex[:, :D_MODEL]
        ex2 = ex[:, D_MODEL:]
        chunk_decay = _dot_exact_rhs(jnp.broadcast_to(packed[end:end + 1, :], (8, LANES)),
                                     expand[:, :D_MODEL])[0:1, :]

        xs = xs_ref[b]
        xsf = xs.astype(F32)
        xs_w = (xsf * ex2).astype(BF16)
        y_groups = []
        for g in range(SSD_GROUPS):
            cg = cm_ref[b, :, g * SSD_STATE:(g + 1) * SSD_STATE]
            bg = bm_ref[b, :, g * SSD_STATE:(g + 1) * SSD_STATE]
            cb = _dot_nt(cg, bg)
            ys = []
            for pp in range(HEADS_PER_GROUP // 2):
                pair = g * (HEADS_PER_GROUP // 2) + pp
                xs_pair = xs[:, pair * LANES:(pair + 1) * LANES]
                zero_pair = jnp.zeros_like(xs_pair)
                m_mats = []
                for h in (2 * pair, 2 * pair + 1):
                    seg = jnp.broadcast_to(packed[:, h:h + 1], full) - cums_t[h:h + 1, :]
                    dec = jnp.exp(jnp.where(allowed, seg, -jnp.inf))
                    m_mats.append((cb * dec * val_t[h:h + 1, :]).astype(BF16))
                rhs = jnp.concatenate([jnp.where(left_half, xs_pair, zero_pair),
                                       jnp.where(right_half, xs_pair, zero_pair)], axis=0)
                ys.append(_dot(jnp.concatenate(m_mats, axis=1), rhs))
            y_diag = jnp.concatenate(ys, axis=1)
            gs = slice(g * GROUP_WIDTH, (g + 1) * GROUP_WIDTH)
            s_g = s_old[b][g]
            y_off = _dot(cg, s_g.astype(BF16)) * ex1[:, gs]
            y_groups.append(y_diag + y_off)
            bg_t = bg.astype(F32).T.astype(BF16)
            s_new_all[b].append(chunk_decay[:, gs] * s_g + _dot(bg_t, xs_w[:, gs]))
        y_ssd = jnp.concatenate(y_groups, axis=1)

        bcum_t = cums_t[GATE_F0:GATE_END, :]
        ip_t = val_t[GATE_I0:GATE_F0, :]
        rep = bcum_t.shape
        g_rep = jnp.broadcast_to(bcum_t[:, end:end + 1], rep)
        a_t = g_rep - bcum_t + ip_t
        a_max = jnp.broadcast_to(jnp.max(a_t, axis=1, keepdims=True), rep)
        w_t = jnp.exp(a_t - a_max)
        m_prev = m_old[b]
        m_new = jnp.maximum(g_rep + m_prev, a_max)
        s_prev = jnp.exp(g_rep + m_prev - m_new)
        s_new = jnp.exp(a_max - m_new)
        r_t = ip_t - bcum_t
        h_heads = []
        for pair in range(MLSTM_HEADS // 2):
            h0, h1 = 2 * pair, 2 * pair + 1
            q_pair = q_ref[b, :, pair * LANES:(pair + 1) * LANES]
            k_pair = k_ref[b, :, pair * LANES:(pair + 1) * LANES]
            q_pair_f = q_pair.astype(F32)
            cst = cst_old[b][pair]
            cst_b = cst.astype(BF16)
            v_pair = []
            for hh, h in enumerate((h0, h1)):
                keep = left_half if hh == 0 else right_half
                vh = v_ref[b, :, h * MLSTM_DV:(h + 1) * MLSTM_DV]
                v_pair.append(vh)
                qk = _dot_nt(jnp.where(keep, q_pair, jnp.zeros_like(q_pair)), k_pair)
                bc = jnp.broadcast_to(packed[:, GATE_F0 + h:GATE_F0 + h + 1], full)
                dlog = jnp.where(allowed, bc + r_t[h:h + 1, :], -jnp.inf)
                m_intra = jnp.broadcast_to(jnp.max(dlog, axis=1, keepdims=True), full)
                m_inter = bc + m_prev[h:h + 1, :]
                m_t = jnp.maximum(m_inter, m_intra)
                s_mat = (qk * jnp.exp(dlog - m_t)).astype(BF16)
                qs = (jnp.where(keep, q_pair_f, 0.0) * jnp.exp(m_inter - m_t)).astype(BF16)
                tot = _dot(jnp.concatenate([s_mat, qs], axis=1),
                           jnp.concatenate([jnp.concatenate([vh, ones_blk], axis=1), cst_b], axis=0))
                num = tot[:, :MLSTM_DV]
                den = tot[:, MLSTM_DV:]
                h_heads.append(num / jnp.maximum(jnp.abs(den), jnp.exp(-m_t)))
            w_rows = jnp.where(top_half, w_t[h0:h0 + 1, :], w_t[h1:h1 + 1, :])
            kw = (k_pair.astype(F32).T * w_rows).astype(BF16)
            full_kv = _dot(kw, jnp.concatenate([v_pair[0], v_pair[1], ones_blk], axis=1))
            kvn = jnp.concatenate(
                [jnp.where(top_half, full_kv[:, :MLSTM_DV], full_kv[:, MLSTM_DV:2 * MLSTM_DV]),
                 full_kv[:, 2 * MLSTM_DV:]], axis=1)
            sp_rows = jnp.where(top_half, s_prev[h0:h0 + 1, :], s_prev[h1:h1 + 1, :])
            sn_rows = jnp.where(top_half, s_new[h0:h0 + 1, :], s_new[h1:h1 + 1, :])
            cst_new_all[b].append(jnp.concatenate([sp_rows, sp_rows], axis=1) * cst
                                  + jnp.concatenate([sn_rows, sn_rows], axis=1) * kvn)
        m_new_all[b] = m_new
        h_ml = jnp.concatenate(h_heads, axis=1)
        if not final:
            return y_ssd.astype(BF16), h_ml.astype(BF16)

        y_tot = y_ssd + yb_ref[b].astype(F32) + dskip_ref[...] * xsf
        zz = z_ref[b].astype(F32)
        y2 = y_tot * (zz * _sigmoid(zz))
        y_n = y2 * lax.rsqrt(jnp.mean(y2 * y2, axis=-1, keepdims=True) + RMS_EPS) * ngs_ref[...]
        h_tot = h_ml + hb_ref[b].astype(F32)
        segs = []
        for h in range(MLSTM_HEADS):
            seg = h_tot[:, h * MLSTM_DV:(h + 1) * MLSTM_DV]
            segs.append(seg * lax.rsqrt(jnp.mean(seg * seg, axis=-1, keepdims=True) + RMS_EPS))
        h_n = jnp.concatenate(segs, axis=1) * ngm_ref[...]
        y_ml = _sigmoid(o_ref[b].astype(F32)) * h_n
        return y_n.astype(BF16), y_ml.astype(BF16)

    s_old = [[s_ref[b, g] for g in range(SSD_GROUPS)] for b in range(n_seq)]
    cst_old = [[cst_ref[b, p] for p in range(MLSTM_HEADS // 2)] for b in range(n_seq)]
    m_old = [m_ref[b] for b in range(n_seq)]
    s_new_all = [[] for _ in range(n_seq)]
    cst_new_all = [[] for _ in range(n_seq)]
    m_new_all = [None] * n_seq
    results = [one_sequence(b) for b in range(n_seq)]
    for b in range(n_seq):
        for g in range(SSD_GROUPS):
            s_ref[b, g] = s_new_all[b][g]
        for p in range(MLSTM_HEADS // 2):
            cst_ref[b, p] = cst_new_all[b][p]
        m_ref[b] = m_new_all[b]

    @pl.when(c > 0)
    def _():
        for b, (first, second) in enumerate(results):
            if final:
                ycat_ref[b, :, :D_MODEL] = first
                ycat_ref[b, :, D_MODEL:] = second
            else:
                yout_ref[b] = first
                hout_ref[b] = second


def _mixer_pass(xs, bm, cm, gates, q, k, v, gbias, alog, expand, n_chunks, *, reverse, final_inputs=None):
    final = final_inputs is not None
    n_seq = xs.shape[0]
    seq_len = (n_chunks - 1) * CHUNK

    def chunk_of(t):
        return (n_chunks - 1 - t) if reverse else t

    def pad_map(t):
        return (0, _storage_chunk(chunk_of(t), n_chunks), 0)

    def out_map(t):
        return (0, jnp.maximum(chunk_of(t) - 1, 0), 0)

    const = lambda a: pl.BlockSpec(a.shape, lambda t: (0,) * a.ndim)
    pad_spec = lambda n: pl.BlockSpec((n_seq, CHUNK, n), pad_map)
    out_spec = lambda n: pl.BlockSpec((n_seq, CHUNK, n), out_map)
    in_arrays = [xs, bm, cm, gates, q, k, v]
    in_specs = [pad_spec(a.shape[2]) for a in in_arrays]
    if final:
        z, o, yb, hb, dskip, ngs, ngm = final_inputs
        in_arrays += [z, o, yb, hb]
        in_specs += [pad_spec(1024), pad_spec(1024), out_spec(1024), out_spec(1024)]
        in_arrays += [gbias, alog, expand, dskip, ngs, ngm]
        in_specs += [const(a) for a in (gbias, alog, expand, dskip, ngs, ngm)]
        out_shape = [jax.ShapeDtypeStruct((n_seq, seq_len, 2 * D_MODEL), BF16)]
        out_specs = [out_spec(2 * D_MODEL)]
    else:
        in_arrays += [gbias, alog, expand]
        in_specs += [const(a) for a in (gbias, alog, expand)]
        out_shape = [jax.ShapeDtypeStruct((n_seq, seq_len, D_MODEL), BF16),
                     jax.ShapeDtypeStruct((n_seq, seq_len, D_MODEL), BF16)]
        out_specs = [out_spec(D_MODEL), out_spec(D_MODEL)]
    return pl.pallas_call(
        functools.partial(_mixer_kernel, reverse=reverse, final=final, n_chunks=n_chunks, n_seq=n_seq),
        grid=(n_chunks,),
        in_specs=in_specs,
        out_specs=out_specs,
        out_shape=out_shape,
        scratch_shapes=[pltpu.VMEM((n_seq, SSD_GROUPS, SSD_STATE, GROUP_WIDTH), F32),
                        pltpu.VMEM((n_seq, MLSTM_HEADS // 2, 2 * MLSTM_DK, 2 * MLSTM_DV), F32),
                        pltpu.VMEM((n_seq, MLSTM_HEADS, LANES), F32)],
        compiler_params=pltpu.CompilerParams(
            dimension_semantics=("arbitrary",), vmem_limit_bytes=VMEM_LIMIT_BYTES),
        name="mixer_fwd" if final else "mixer_bwd",
    )(*in_arrays)


def _epilogue_kernel(x_ref, ycat_ref, wout_ref, lng0_ref, lnb0_ref, lng1_ref, lnb1_ref,
                     wrh_ref, wrl_ref, br_ref, h1_ref, h1p_ref, sel_ref, gate_ref):
    h0 = _layer_norm(x_ref[...], lng0_ref[...], lnb0_ref[...])
    mix = _dot(ycat_ref[...], wout_ref[...])
    h1 = _layer_norm(DEEPNORM_ALPHA * h0 + mix, lng1_ref[...], lnb1_ref[...])
    h1_ref[...] = h1
    h1p_ref[...] = _pack_bf16_pairs(h1)
    hh = h1.astype(BF16)
    hl = (h1 - hh.astype(F32)).astype(BF16)
    wrh = wrh_ref[...]
    logits = _dot(hh, wrh) + _dot(hl, wrh) + _dot(hh, wrl_ref[...]) + br_ref[...]
    lane = lax.broadcasted_iota(jnp.int32, (1, LANES), 1)
    lane_f = lane.astype(F32)
    logits = jnp.where(lane < N_EXPERTS, logits, -jnp.inf)
    work = logits
    sel = jnp.zeros(logits.shape, jnp.bool_)
    top = None
    for _ in range(TOP_K):
        m = jnp.max(work, axis=-1, keepdims=True)
        if top is None:
            top = m
        first = jnp.min(jnp.where(work == m, lane_f, float(LANES)), axis=-1, keepdims=True)
        pick = lane_f == first
        sel = jnp.logical_or(sel, pick)
        work = jnp.where(pick, -jnp.inf, work)
    e = jnp.where(sel, jnp.exp(logits - top), 0.0)
    gate_ref[...] = e / jnp.sum(e, axis=-1, keepdims=True)
    sel_ref[...] = sel.astype(F32)


def _epilogue(x, ycat, ycat_row0, w_out, lng0, lnb0, lng1, lnb1, wrh, wrl, br):
    rows = x.shape[0]
    tm = EPILOGUE_ROWS
    assert rows % tm == 0 and ycat_row0 % tm == 0
    tile0 = ycat_row0 // tm
    row_spec = lambda n: pl.BlockSpec((tm, n), lambda i: (i, 0))
    const = lambda a: pl.BlockSpec(a.shape, lambda i: (0,) * a.ndim)
    out_shape = [jax.ShapeDtypeStruct((rows, D_MODEL), F32),
                 jax.ShapeDtypeStruct((rows, D_MODEL // 2), jnp.uint32),
                 jax.ShapeDtypeStruct((rows, LANES), F32),
                 jax.ShapeDtypeStruct((rows, LANES), F32)]
    consts = (w_out, lng0, lnb0, lng1, lnb1, wrh, wrl, br)
    return pl.pallas_call(
        _epilogue_kernel,
        grid=(rows // tm,),
        in_specs=[row_spec(D_MODEL), pl.BlockSpec((tm, 2 * D_MODEL), lambda i: (tile0 + i, 0))]
                 + [const(a) for a in consts],
        out_specs=[row_spec(D_MODEL), row_spec(D_MODEL // 2), row_spec(LANES), row_spec(LANES)],
        out_shape=out_shape,
        compiler_params=pltpu.CompilerParams(
            dimension_semantics=("arbitrary",), vmem_limit_bytes=VMEM_LIMIT_BYTES),
        name="epilogue",
    )(x, ycat, *consts)


def _rank_kernel(sel_ref, gate_ref, lstrict_ref, ucum_ref, dest_ref, gk_ref, stats_ref,
                 base_ref, *, trash_row):
    phase = pl.program_id(0)
    i = pl.program_id(1)
    sel = sel_ref[...]
    colsum = jnp.sum(sel, axis=0, keepdims=True)

    @pl.when(jnp.logical_and(phase == 0, i == 0))
    def _():
        base_ref[...] = jnp.zeros_like(base_ref)

    @pl.when(phase == 0)
    def _():
        base_ref[0:1, :] = base_ref[0:1, :] + colsum

    @pl.when(jnp.logical_and(phase == 1, i == 0))
    def _():
        counts = base_ref[0:1, :]
        padded = jnp.ceil(counts / MOE_BLOCK) * MOE_BLOCK
        pend = _dot_exact_rhs(jnp.broadcast_to(padded, (8, LANES)), ucum_ref[...])[0:1, :]
        stats_ref[0:1, :] = counts
        stats_ref[1:2, :] = pend - padded
        stats_ref[2:3, :] = pend
        stats_ref[3:8, :] = jnp.zeros((5, LANES), F32)
        base_ref[1:2, :] = pend - padded

    @pl.when(phase == 1)
    def _():
        before = _dot(lstrict_ref[...], sel.astype(BF16))
        pos = base_ref[1:2, :] + before
        base_ref[1:2, :] = base_ref[1:2, :] + colsum
        work = jnp.where(sel > 0.0, pos + 1.0, 0.0)
        gates = gate_ref[...]
        for kk in range(TOP_K):
            m = jnp.max(work, axis=-1, keepdims=True)
            pick = jnp.logical_and(work == m, m > 0.0)
            gk_ref[:, kk:kk + 1] = jnp.sum(jnp.where(pick, gates, 0.0), axis=-1, keepdims=True)
            dest_ref[:, kk:kk + 1] = jnp.where(m > 0.0, m - 1.0, float(trash_row)).astype(jnp.int32)
            work = jnp.where(pick, 0.0, work)


def _rank(sel, gates, lstrict, ucum, trash_row):
    rows = sel.shape[0]
    tm = RANK_ROWS
    assert rows % tm == 0
    row_spec = lambda n: pl.BlockSpec((tm, n), lambda p, i: (i, 0))
    out_row_spec = lambda n: pl.BlockSpec((tm, n), lambda p, i: (i * p, 0))
    const = lambda a: pl.BlockSpec(a.shape, lambda p, i: (0,) * a.ndim)
    return pl.pallas_call(
        functools.partial(_rank_kernel, trash_row=trash_row),
        grid=(2, rows // tm),
        in_specs=[row_spec(LANES), row_spec(LANES), const(lstrict), const(ucum)],
        out_specs=[out_row_spec(TOP_K), out_row_spec(TOP_K),
                   pl.BlockSpec((8, LANES), lambda p, i: (0, 0))],
        out_shape=[jax.ShapeDtypeStruct((rows, TOP_K), jnp.int32),
                   jax.ShapeDtypeStruct((rows, TOP_K), F32),
                   jax.ShapeDtypeStruct((8, LANES), F32)],
        scratch_shapes=[pltpu.VMEM((8, LANES), F32)],
        compiler_params=pltpu.CompilerParams(
            dimension_semantics=("arbitrary", "arbitrary"), vmem_limit_bytes=VMEM_LIMIT_BYTES),
        name="rank",
    )(sel, gates, lstrict, ucum)


def _sc_mesh():
    return plsc.VectorSubcoreMesh(core_axis_name="c", subcore_axis_name="s",
                                  num_cores=SC_CORES, num_subcores=SC_SUBCORES)


def _sc_scatter_rows(src, idx, n_out_rows):
    n_src, d = src.shape
    w, k = SC_WINDOW, SC_SCATTER_INFLIGHT
    assert idx.shape == (n_src // w, TOP_K, w) and n_src % (w * k * SC_WORKERS) == 0
    per_worker = n_src // (w * SC_WORKERS)
    idx2d = idx.reshape(n_src // w * TOP_K, w)

    @functools.partial(
        pl.kernel, mesh=_sc_mesh(),
        out_type=jax.ShapeDtypeStruct((n_out_rows, d), src.dtype),
        scratch_types=[pltpu.VMEM((k * TOP_K, w), jnp.int32), pltpu.VMEM((k, w, d), src.dtype),
                       pltpu.SemaphoreType.DMA((k,)), pltpu.SemaphoreType.DMA((k,))],
        name="sc_scatter_rows")
    def body(src_hbm, idx_hbm, out_hbm, idx_v, rows_v, sem_load, sem_scatter):
        wid = lax.axis_index("s") * SC_CORES + lax.axis_index("c")

        @pl.loop(0, per_worker // k)
        def _(i):
            win0 = wid * per_worker + i * k
            loads = [pltpu.async_copy(src_hbm.at[pl.ds(pl.multiple_of((win0 + b) * w, w), w)], rows_v.at[b],
                                      sem_load.at[b]) for b in range(k)]
            pltpu.sync_copy(idx_hbm.at[pl.ds(pl.multiple_of(win0 * TOP_K, TOP_K), k * TOP_K)], idx_v)
            scatters = []
            for b in range(k):
                loads[b].wait()
                scatters += [pltpu.async_copy(rows_v.at[b], out_hbm.at[idx_v.at[b * TOP_K + kk]],
                                              sem_scatter.at[b]) for kk in range(TOP_K)]
            for copy in scatters:
                copy.wait()

    return body(src, idx2d)


def _sc_gather_rows(table, idx):
    d = table.shape[1]
    w, k = SC_WINDOW, SC_GATHER_INFLIGHT
    assert idx.shape[0] % (w * k * SC_WORKERS) == 0
    per_worker = idx.shape[0] // (w * SC_WORKERS)
    idx2d = idx.reshape(-1, w)

    @functools.partial(
        pl.kernel, mesh=_sc_mesh(),
        out_type=jax.ShapeDtypeStruct((idx.shape[0], d), table.dtype),
        scratch_types=[pltpu.VMEM((k, w), jnp.int32), pltpu.VMEM((k, w, d), table.dtype),
                       pltpu.SemaphoreType.DMA((k,)), pltpu.SemaphoreType.DMA((k,))],
        name="sc_gather_rows")
    def body(table_hbm, idx_hbm, out_hbm, idx_v, rows_v, sem_gather, sem_store):
        wid = lax.axis_index("s") * SC_CORES + lax.axis_index("c")

        @pl.loop(0, per_worker // k)
        def _(i):
            win0 = wid * per_worker + i * k
            pltpu.sync_copy(idx_hbm.at[pl.ds(pl.multiple_of(win0, k), k)], idx_v)
            gathers = [pltpu.async_copy(table_hbm.at[idx_v.at[b]], rows_v.at[b], sem_gather.at[b])
                       for b in range(k)]
            stores = []
            for b in range(k):
                gathers[b].wait()
                stores.append(pltpu.async_copy(
                    rows_v.at[b], out_hbm.at[pl.ds(pl.multiple_of((win0 + b) * w, w), w)], sem_store.at[b]))
            for copy in stores:
                copy.wait()

    return body(table, idx2d)


def _ffn_kernel(blk_ref, exp_ref, valid_ref, nused_ref, xb_ref, w1_ref, b1_ref, w2_ref, b2_ref, yb_ref,
                w1b_ref, w2b_ref):
    j = pl.program_id(0)
    active = j < nused_ref[0]

    @pl.when(jnp.logical_and(active, jnp.logical_or(j == 0, exp_ref[j] != exp_ref[jnp.maximum(j - 1, 0)])))
    def _():
        w1b_ref[...] = w1_ref[0].astype(BF16)
        w2b_ref[...] = w2_ref[0].astype(BF16)

    @pl.when(active)
    def _():
        row = lax.broadcasted_iota(jnp.int32, (MOE_BLOCK, 1), 0)
        x = jnp.where(row < valid_ref[j], _unpack_bf16_pairs(xb_ref[...]), 0.0).astype(BF16)
        hc = _dot(x, w1b_ref[...]) + b1_ref[0]
        gate = jnp.minimum(hc[:, :D_FF], SWIGLU_LIMIT)
        up = jnp.clip(hc[:, D_FF:], -SWIGLU_LIMIT, SWIGLU_LIMIT)
        act = (up + 1.0) * gate * _sigmoid(SWIGLU_ALPHA * gate)
        yb_ref[...] = _pack_bf16_pairs(_dot(act.astype(BF16), w2b_ref[...]) + b2_ref[0])


def _ffn(blk_idx, blk_exp, blk_valid, n_used, xb, w1, b1, w2, b2, n_blocks):
    bm = MOE_BLOCK
    grid_spec = pltpu.PrefetchScalarGridSpec(
        num_scalar_prefetch=4,
        grid=(n_blocks,),
        in_specs=[pl.BlockSpec((bm, D_MODEL // 2), lambda j, bi, be, bv, nu: (bi[j], 0)),
                  pl.BlockSpec((1, D_MODEL, 2 * D_FF), lambda j, bi, be, bv, nu: (be[j], 0, 0)),
                  pl.BlockSpec((1, 1, 2 * D_FF), lambda j, bi, be, bv, nu: (be[j], 0, 0)),
                  pl.BlockSpec((1, D_FF, D_MODEL), lambda j, bi, be, bv, nu: (be[j], 0, 0)),
                  pl.BlockSpec((1, 1, D_MODEL), lambda j, bi, be, bv, nu: (be[j], 0, 0))],
        out_specs=pl.BlockSpec((bm, D_MODEL // 2), lambda j, bi, be, bv, nu: (bi[j], 0)),
        scratch_shapes=[pltpu.VMEM((D_MODEL, 2 * D_FF), BF16), pltpu.VMEM((D_FF, D_MODEL), BF16)],
    )
    return pl.pallas_call(
        _ffn_kernel,
        grid_spec=grid_spec,
        out_shape=jax.ShapeDtypeStruct(xb.shape, jnp.uint32),
        compiler_params=pltpu.CompilerParams(
            dimension_semantics=("arbitrary",), vmem_limit_bytes=VMEM_LIMIT_BYTES),
        name="expert_ffn",
    )(blk_idx, blk_exp, blk_valid, n_used, xb, w1, b1, w2, b2)


def _combine_kernel(gk_ref, h1_ref, y0_ref, y1_ref, y2_ref, y3_ref, lng_ref, lnb_ref, out_ref):
    gk = gk_ref[...]
    ffn = gk[:, 0:1] * _unpack_bf16_pairs(y0_ref[...])
    for kk, y_ref in enumerate((y1_ref, y2_ref, y3_ref), start=1):
        ffn = ffn + gk[:, kk:kk + 1] * _unpack_bf16_pairs(y_ref[...])
    out_ref[...] = _layer_norm(DEEPNORM_ALPHA * h1_ref[...] + ffn, lng_ref[...], lnb_ref[...])


def _combine(gk, h1, ysel, lng, lnb):
    rows = h1.shape[0]
    tm = COMBINE_ROWS
    assert rows % tm == 0
    n_tiles = rows // tm
    const = lambda a: pl.BlockSpec(a.shape, lambda i: (0,) * a.ndim)
    ksel = lambda kk: pl.BlockSpec((tm, D_MODEL // 2), lambda i: (kk * n_tiles + i, 0))
    return pl.pallas_call(
        _combine_kernel,
        grid=(n_tiles,),
        in_specs=[pl.BlockSpec((tm, TOP_K), lambda i: (i, 0)),
                  pl.BlockSpec((tm, D_MODEL), lambda i: (i, 0)),
                  ksel(0), ksel(1), ksel(2), ksel(3),
                  const(lng), const(lnb)],
        out_specs=pl.BlockSpec((tm, D_MODEL), lambda i: (i, 0)),
        out_shape=jax.ShapeDtypeStruct((rows, D_MODEL), F32),
        compiler_params=pltpu.CompilerParams(
            dimension_semantics=("arbitrary",), vmem_limit_bytes=VMEM_LIMIT_BYTES),
        name="combine",
    )(gk, h1, ysel, ysel, ysel, ysel, lng, lnb)


def _row(v, width=None):
    v = v.reshape(1, -1).astype(F32)
    if width is not None and v.shape[1] < width:
        v = jnp.pad(v, ((0, 0), (0, width - v.shape[1])))
    return v


def _encode_all(x_a, x_b, meta_tokens, ln_emb_g, ln_emb_b, w_in, conv_w, conv_b, dt_bias, a_log,
                d_skip, ssd_norm_g, i_bias, f_bias, mlstm_norm_g, w_out, ln1_g, ln1_b, w_router, b_router,
                w1, b1, w2, b2, ln2_g, ln2_b):
    n_a, seq_len, _ = x_a.shape
    n_seq = n_a + x_b.shape[0]
    rows_a = n_a * seq_len
    assert seq_len % CHUNK == 0
    n_chunks = seq_len // CHUNK + 1
    n_tok = n_seq * seq_len

    sizes = (1024, CONV_CH, 2 * SSD_HEADS, 512, 512, 1024, 1024, 2 * MLSTM_HEADS, 2 * MLSTM_HEADS)
    offs = [0]
    for s in sizes:
        offs.append(offs[-1] + s)
    w_z, w_xbc, w_dt, w_q, w_k, w_v, w_o, w_i, w_f = [w_in[:, offs[j]:offs[j + 1]] for j in range(9)]
    w_big = jnp.concatenate([w_z, w_xbc, w_q, w_k * (MLSTM_DK ** -0.5), w_v, w_o], axis=1).astype(BF16)
    zpad = jnp.zeros((D_MODEL, LANES - GATE_END), F32)
    gate_cols = []
    for d in range(2):
        gate_cols += [w_dt[:, d * SSD_HEADS:(d + 1) * SSD_HEADS],
                      w_i[:, d * MLSTM_HEADS:(d + 1) * MLSTM_HEADS],
                      w_f[:, d * MLSTM_HEADS:(d + 1) * MLSTM_HEADS], zpad]
    w_gates = jnp.concatenate(gate_cols, axis=1).astype(BF16)
    gbias = [_row(jnp.concatenate([dt_bias[d], i_bias[d], f_bias[d]]), LANES) for d in range(2)]
    alog = [jnp.pad(jnp.broadcast_to(a_log[d].astype(F32)[:, None], (SSD_HEADS, LANES)),
                    ((0, GATE_END - SSD_HEADS), (0, 0))) for d in range(2)]
    head_of_col = jnp.arange(D_MODEL, dtype=jnp.int32) // SSD_HEAD_DIM
    lane_id = jnp.arange(LANES, dtype=jnp.int32)[:, None]
    expand = jnp.concatenate([lane_id == GATE_END + head_of_col[None, :],
                              lane_id == 2 * GATE_END + head_of_col[None, :]], axis=1).astype(BF16)
    dskip = _row(jnp.repeat(d_skip, SSD_HEAD_DIM))
    conv_w8 = jnp.pad(conv_w.astype(F32), ((0, 8 - CONV_W), (0, 0)))
    meta_tile = jnp.pad(meta_tokens.astype(F32), ((PAD_FRONT, INPROJ_ROWS - CHUNK), (0, 0)))

    z, xbc, q, k, v, o, gates_f, gates_b = _inproj(x_a, x_b, meta_tile, _row(ln_emb_g), _row(ln_emb_b),
                                                   w_big, w_gates)
    xs, bm, cm = _conv(xbc, conv_w8, _row(conv_b), n_chunks)
    yb, hb = _mixer_pass(xs, bm, cm, gates_b, q, k, v, gbias[1], alog[1], expand, n_chunks, reverse=True)
    (ycat,) = _mixer_pass(xs, bm, cm, gates_f, q, k, v, gbias[0], alog[0], expand, n_chunks,
                          reverse=False,
                          final_inputs=(z, o, yb, hb, dskip, _row(ssd_norm_g), _row(mlstm_norm_g)))

    wr = jnp.pad(w_router.astype(F32), ((0, 0), (0, LANES - N_EXPERTS)))
    wrh = wr.astype(BF16)
    wrl = (wr - wrh.astype(F32)).astype(BF16)
    w_out_b = w_out.astype(BF16)
    r_i = jnp.arange(RANK_ROWS, dtype=jnp.int32)
    lstrict = (r_i[None, :] < r_i[:, None]).astype(BF16)
    l_i = jnp.arange(LANES, dtype=jnp.int32)
    ucum = (l_i[:, None] <= l_i[None, :]).astype(BF16)
    ycat2d = ycat.reshape(n_tok, 2 * D_MODEL)
    outs = []
    for x_part, row0 in ((x_a, 0), (x_b, rows_a)):
        n_part = x_part.shape[0] * seq_len
        h1, h1p, sel, gates = _epilogue(x_part.reshape(n_part, D_MODEL), ycat2d, row0, w_out_b,
                                        _row(ln_emb_g), _row(ln_emb_b), _row(ln1_g), _row(ln1_b),
                                        wrh, wrl, _row(b_router, LANES))
        n_blocks = n_part * TOP_K // MOE_BLOCK + N_EXPERTS
        trash_row = n_blocks * MOE_BLOCK
        dest, gk, stats = _rank(sel, gates, lstrict, ucum, trash_row)
        counts = stats[0, :N_EXPERTS].astype(jnp.int32)
        starts = stats[1, :N_EXPERTS].astype(jnp.int32)
        pends = stats[2, :N_EXPERTS].astype(jnp.int32)
        n_used = pends[N_EXPERTS - 1] // MOE_BLOCK
        blk = jnp.minimum(jnp.arange(n_blocks, dtype=jnp.int32), jnp.maximum(n_used - 1, 0))
        blk_exp = jnp.minimum(
            jnp.sum((pends[None, :] <= (blk * MOE_BLOCK)[:, None]).astype(jnp.int32), axis=1),
            N_EXPERTS - 1).astype(jnp.int32)
        blk_last = jnp.take(starts + counts, blk_exp)
        blk_valid = jnp.clip(blk_last - blk * MOE_BLOCK, 0, MOE_BLOCK).astype(jnp.int32)
        dest_km = dest.T.reshape(-1)
        dest_wm = dest.reshape(n_part // SC_WINDOW, SC_WINDOW, TOP_K).transpose(0, 2, 1)
        xb = _sc_scatter_rows(h1p, dest_wm, trash_row + MOE_BLOCK)
        yexp = _ffn(blk, blk_exp, blk_valid, n_used.reshape(1), xb, w1,
                    b1.reshape(N_EXPERTS, 1, -1), w2, b2.reshape(N_EXPERTS, 1, -1), n_blocks)
        ysel = _sc_gather_rows(yexp, dest_km)
        outs.append(_combine(gk, h1, ysel, _row(ln2_g), _row(ln2_b)))
    return outs


def kernel(x_prompt, x_sample, meta_tokens, ln_emb_g, ln_emb_b, w_in, conv_w, conv_b, dt_bias, a_log,
           d_skip, ssd_norm_g, i_bias, f_bias, mlstm_norm_g, w_out, ln1_g, ln1_b, w_router, b_router,
           w1, b1, w2, b2, ln2_g, ln2_b):
    assert x_prompt.shape[1:] == x_sample.shape[1:]
    n_p, seq_len, d = x_prompt.shape
    n_s = x_sample.shape[0]
    y_p, y_s = _encode_all(x_prompt.astype(F32), x_sample.astype(F32), meta_tokens, ln_emb_g, ln_emb_b, w_in[0], conv_w[0],
                           conv_b[0], dt_bias[0], a_log[0], d_skip[0], ssd_norm_g[0], i_bias[0], f_bias[0],
                           mlstm_norm_g[0], w_out[0], ln1_g[0], ln1_b[0], w_router[0], b_router[0],
                           w1[0], b1[0], w2[0], b2[0], ln2_g[0], ln2_b[0])
    return (y_p.reshape(n_p, seq_len, d), y_s.reshape(n_s, seq_len, d))
```

```python
import functools

import jax
import jax.numpy as jnp
from jax import lax
from jax.experimental import pallas as pl
from jax.experimental.pallas import tpu as pltpu
from jax.experimental.pallas import tpu_sc as plsc

F32 = jnp.float32
BF16 = jnp.bfloat16

D_MODEL = 1024
N_META = 16
CHUNK = 128
PAD_FRONT = CHUNK - N_META
SSD_HEADS = 16
SSD_HEAD_DIM = 64
SSD_GROUPS = 4
SSD_STATE = 128
HEADS_PER_GROUP = SSD_HEADS // SSD_GROUPS
GROUP_WIDTH = HEADS_PER_GROUP * SSD_HEAD_DIM
CONV_W = 5
CONV_HALF = CONV_W // 2
CONV_CH = D_MODEL + 2 * SSD_GROUPS * SSD_STATE
MLSTM_HEADS = 8
MLSTM_DK = 64
MLSTM_DV = 128
N_EXPERTS = 32
TOP_K = 4
D_FF = D_MODEL
SWIGLU_LIMIT = 7.0
SWIGLU_ALPHA = 1.702
DEEPNORM_ALPHA = 2.0 ** 0.25
LN_EPS = 1e-5
RMS_EPS = 1e-5
NEG_GATE = -1e30
LOG2E = 1.4426950408889634

LANES = 128
BF16_SUBLANES = 16
VMEM_LIMIT_BYTES = 56 * 1024 * 1024

GATE_DT0, GATE_I0, GATE_F0, GATE_END = 0, SSD_HEADS, SSD_HEADS + MLSTM_HEADS, SSD_HEADS + 2 * MLSTM_HEADS

INPROJ_ROWS = 512
EPILOGUE_ROWS = 512
RANK_ROWS = 512
COMBINE_ROWS = 512
MOE_BLOCK = 512

SC_CORES = 2
SC_SUBCORES = 16
SC_WORKERS = SC_CORES * SC_SUBCORES
SC_WINDOW = 32
SC_SCATTER_INFLIGHT = 2
SC_GATHER_INFLIGHT = 4


def _dot(a, b):
    return jnp.dot(a, b, preferred_element_type=F32)


def _dot_nt(a, b):
    return lax.dot_general(a, b, (((1,), (1,)), ((), ())), preferred_element_type=F32)


def _split3(x):
    hi = x.astype(BF16)
    r1 = x - hi.astype(F32)
    mid = r1.astype(BF16)
    lo = (r1 - mid.astype(F32)).astype(BF16)
    return hi, mid, lo


def _dot_exact_lhs(a_bf16, x):
    hi, mid, lo = _split3(x)
    return _dot(a_bf16, hi) + _dot(a_bf16, mid) + _dot(a_bf16, lo)


def _dot_exact_rhs(x, b_bf16):
    hi, mid, lo = _split3(x)
    return _dot(hi, b_bf16) + _dot(mid, b_bf16) + _dot(lo, b_bf16)


def _layer_norm(x, g, b):
    mu = jnp.mean(x, axis=-1, keepdims=True)
    xc = x - mu
    var = jnp.mean(xc * xc, axis=-1, keepdims=True)
    return xc * lax.rsqrt(var + LN_EPS) * g + b


def _sigmoid(x):
    return 1.0 / (1.0 + jnp.exp(-x))


def _log1p_exp_neg_abs(x):
    return jnp.log(1.0 + jnp.exp(-jnp.abs(x)))


def _pack_bf16_pairs(x):
    n = x.shape[1] // 2
    bits = lax.bitcast_convert_type(x.astype(BF16).astype(F32), jnp.uint32)
    return (bits[:, :n] >> 16) | bits[:, n:]


def _unpack_bf16_pairs(words):
    lo = lax.bitcast_convert_type(words << 16, F32)
    hi = lax.bitcast_convert_type(words & jnp.uint32(0xFFFF0000), F32)
    return jnp.concatenate([lo, hi], axis=1)


def _storage_chunk(c, n_chunks):
    return jnp.where(c == 0, n_chunks - 1, c - 1)


def _inproj_kernel(xa_ref, xb_ref, meta_ref, g_ref, b_ref, wbig_ref, wg_ref,
                   z_ref, xbc_ref, q_ref, k_ref, v_ref, o_ref, gf_ref, gb_ref, *, n_x_tiles, n_a):
    x = jnp.where(pl.program_id(0) < n_a, xa_ref[...], xb_ref[...])
    x = jnp.where(pl.program_id(1) == n_x_tiles, meta_ref[...], x)
    h = _layer_norm(x, g_ref[...], b_ref[...]).astype(BF16)

    def mm(c0, c1):
        return _dot(h, wbig_ref[:, c0:c1]).astype(BF16)

    z_ref[...] = mm(0, 1024)
    xbc_ref[:, 0:1024] = mm(1024, 2048)
    xbc_ref[:, 1024:2048] = mm(2048, 3072)
    q_ref[...] = mm(3072, 3584)
    k_ref[...] = mm(3584, 4096)
    v_ref[...] = mm(4096, 5120)
    o_ref[...] = mm(5120, 6144)
    gates = _dot(h, wg_ref[...])
    gf_ref[...] = gates[:, :LANES]
    gb_ref[...] = gates[:, LANES:]


def _inproj(x_a, x_b, meta_tile, ln_g, ln_b, w_big, w_gates):
    n_a, seq_len, _ = x_a.shape
    n_seq = n_a + x_b.shape[0]
    tm = INPROJ_ROWS
    assert seq_len % tm == 0 and tm >= CHUNK
    n_x_tiles = seq_len // tm
    rows = seq_len + CHUNK
    row_spec = lambda n: pl.BlockSpec((None, tm, n), lambda b, i: (b, i, 0))
    const = lambda a: pl.BlockSpec(a.shape, lambda b, i: (0,) * a.ndim)
    resident = lambda a: pl.BlockSpec(a.shape, lambda b, i: (0,) * a.ndim, pipeline_mode=pl.Buffered(1))
    widths = (1024, CONV_CH, 512, 512, 1024, 1024)
    out_shapes = [jax.ShapeDtypeStruct((n_seq, rows, w), BF16) for w in widths]
    out_shapes += [jax.ShapeDtypeStruct((n_seq, rows, LANES), F32)] * 2
    return pl.pallas_call(
        functools.partial(_inproj_kernel, n_x_tiles=n_x_tiles, n_a=n_a),
        grid=(n_seq, n_x_tiles + 1),
        in_specs=[pl.BlockSpec((None, tm, D_MODEL), lambda b, i: (
                      jnp.minimum(b, n_a - 1),
                      jnp.where(b < n_a, jnp.minimum(i, n_x_tiles - 1), n_x_tiles - 1), 0)),
                  pl.BlockSpec((None, tm, D_MODEL), lambda b, i: (
                      jnp.maximum(b - n_a, 0),
                      jnp.where(b < n_a, 0, jnp.minimum(i, n_x_tiles - 1)), 0)),
                  resident(meta_tile), const(ln_g), const(ln_b), resident(w_big), resident(w_gates)],
        out_specs=[row_spec(s.shape[2]) for s in out_shapes],
        out_shape=out_shapes,
        compiler_params=pltpu.CompilerParams(
            dimension_semantics=("arbitrary", "arbitrary"), vmem_limit_bytes=VMEM_LIMIT_BYTES),
        name="inproj",
    )(x_a, x_b, meta_tile, ln_g, ln_b, w_big, w_gates)


def _conv_kernel(prev_ref, main_ref, next_ref, shift_ref, w_ref, b_ref, xs_ref, bm_ref, cm_ref,
                 *, n_x_tiles, chunks_per_tile):
    i = pl.program_id(1)
    w = w_ref[...]
    bias = b_ref[...]
    shift = shift_ref[...]

    def conv_chunk(before, rows, after, pad_rows, j):
        shifted = _dot(shift, jnp.concatenate([before, rows, after], axis=0).astype(BF16))
        acc = bias + rows * w[CONV_HALF:CONV_HALF + 1, :]
        for jj, t in enumerate(t for t in range(CONV_W) if t != CONV_HALF):
            acc = acc + shifted[jj * CHUNK:(jj + 1) * CHUNK, :] * w[t:t + 1, :]
        y = acc * _sigmoid(acc)
        if pad_rows is not None:
            y = jnp.where(pad_rows, 0.0, y)
        r = slice(j * CHUNK, (j + 1) * CHUNK)
        xs_ref[r, :] = y[:, :D_MODEL].astype(BF16)
        bm_ref[r, :] = y[:, D_MODEL:D_MODEL + 512].astype(BF16)
        cm_ref[r, :] = y[:, D_MODEL + 512:].astype(BF16)

    @pl.when(i < n_x_tiles)
    def _():
        tile = main_ref[...].astype(F32)
        for j in range(chunks_per_tile):
            lo, hi = j * CHUNK, (j + 1) * CHUNK
            before = prev_ref[...].astype(F32) if j == 0 else tile[lo - BF16_SUBLANES:lo, :]
            if j == chunks_per_tile - 1:
                after = jnp.where(i == n_x_tiles - 1, 0.0, next_ref[...].astype(F32))
            else:
                after = tile[hi:hi + BF16_SUBLANES, :]
            conv_chunk(before, tile[lo:hi, :], after, None, j)

    @pl.when(i == n_x_tiles)
    def _():
        row = lax.broadcasted_iota(jnp.int32, (CHUNK, 1), 0)
        pad_rows = row < PAD_FRONT
        rows = jnp.where(pad_rows, 0.0, main_ref[0:CHUNK, :].astype(F32))
        conv_chunk(jnp.zeros((BF16_SUBLANES, CONV_CH), F32), rows, next_ref[...].astype(F32), pad_rows, 0)


def _conv(xbc, conv_w8, conv_b, n_chunks):
    n_seq, rows, _ = xbc.shape
    tm = INPROJ_ROWS
    seq_len = (n_chunks - 1) * CHUNK
    assert seq_len % tm == 0 and tm % CHUNK == 0
    n_x_tiles = seq_len // tm
    halo_per_tile = tm // BF16_SUBLANES
    meta_last_halo = rows // BF16_SUBLANES - 1

    def prev_map(b, i):
        before_tile = jnp.maximum(jnp.minimum(i, n_x_tiles - 1) * halo_per_tile - 1, 0)
        return (b, jnp.where(i == 0, meta_last_halo, before_tile), 0)

    def next_map(b, i):
        return (b, jnp.where(i >= n_x_tiles - 1, 0, (i + 1) * halo_per_tile) , 0)

    const = lambda a: pl.BlockSpec(a.shape, lambda b, i: (0,) * a.ndim)
    l_i = jnp.arange(CHUNK, dtype=jnp.int32)[:, None]
    j_i = jnp.arange(CHUNK + 2 * BF16_SUBLANES, dtype=jnp.int32)[None, :]
    shifts = jnp.concatenate([(j_i == BF16_SUBLANES + l_i + t - CONV_HALF)
                              for t in range(CONV_W) if t != CONV_HALF], axis=0).astype(BF16)
    out_shapes = [jax.ShapeDtypeStruct((n_seq, rows, D_MODEL), BF16),
                  jax.ShapeDtypeStruct((n_seq, rows, 512), BF16),
                  jax.ShapeDtypeStruct((n_seq, rows, 512), BF16)]
    tile_spec = lambda n: pl.BlockSpec((None, tm, n), lambda b, i: (b, i, 0))
    return pl.pallas_call(
        functools.partial(_conv_kernel, n_x_tiles=n_x_tiles, chunks_per_tile=tm // CHUNK),
        grid=(n_seq, n_x_tiles + 1),
        in_specs=[pl.BlockSpec((None, BF16_SUBLANES, CONV_CH), prev_map),
                  tile_spec(CONV_CH),
                  pl.BlockSpec((None, BF16_SUBLANES, CONV_CH), next_map),
                  const(shifts), const(conv_w8), const(conv_b)],
        out_specs=[tile_spec(s.shape[2]) for s in out_shapes],
        out_shape=out_shapes,
        compiler_params=pltpu.CompilerParams(
            dimension_semantics=("arbitrary", "arbitrary"), vmem_limit_bytes=VMEM_LIMIT_BYTES),
        name="conv",
    )(xbc, xbc, xbc, shifts, conv_w8, conv_b)


def _mixer_kernel(*refs, reverse, final, n_chunks, n_seq):
    if final:
        (xs_ref, bm_ref, cm_ref, g_ref, q_ref, k_ref, v_ref, z_ref, o_ref, yb_ref, hb_ref,
         gbias_ref, alog_ref, expand_ref, dskip_ref, ngs_ref, ngm_ref,
         ycat_ref, s_ref, cst_ref, m_ref) = refs
    else:
        (xs_ref, bm_ref, cm_ref, g_ref, q_ref, k_ref, v_ref,
         gbias_ref, alog_ref, expand_ref,
         yout_ref, hout_ref, s_ref, cst_ref, m_ref) = refs

    t = pl.program_id(0)
    c = (n_chunks - 1 - t) if reverse else t
    end = 0 if reverse else CHUNK - 1

    @pl.when(t == 0)
    def _():
        s_ref[...] = jnp.zeros_like(s_ref)
        cst_ref[...] = jnp.zeros_like(cst_ref)
        m_ref[...] = jnp.zeros_like(m_ref)

    row = lax.broadcasted_iota(jnp.int32, (CHUNK, 1), 0)
    col = lax.broadcasted_iota(jnp.int32, (1, CHUNK), 1)
    lane = col
    allowed = (col >= row) if reverse else (col <= row)
    tri = allowed.astype(BF16)
    tri_t = ((row >= col) if reverse else (row <= col)).astype(BF16)
    feat = lax.broadcasted_iota(jnp.int32, (GATE_END, 1), 0)
    is_dt = feat < GATE_I0
    is_i = jnp.logical_and(feat >= GATE_I0, feat < GATE_F0)
    is_f = feat >= GATE_F0
    pad_cols = jnp.logical_and(c == 0, col < PAD_FRONT)
    a_coef = -jnp.exp(alog_ref[...]) * LOG2E
    expand = expand_ref[...]
    left_half = lane < SSD_HEAD_DIM
    right_half = jnp.logical_not(left_half)
    top_half = row < MLSTM_DK
    ones_blk = jnp.ones((CHUNK, MLSTM_DV), BF16)
    full = (CHUNK, LANES)

    def one_sequence(b):
        gr = (g_ref[b] + gbias_ref[...]).T[0:GATE_END, :]
        lse = _log1p_exp_neg_abs(gr)
        val_t = jnp.where(is_dt, jnp.maximum(gr, 0.0) + lse, jnp.where(is_i, gr, jnp.minimum(gr, 0.0) - lse))
        val_t = jnp.where(pad_cols, jnp.where(is_i, NEG_GATE, 0.0), val_t)
        u_t = jnp.where(is_dt, val_t * a_coef, jnp.where(is_f, val_t * LOG2E, 0.0))
        cums_t = _dot_exact_rhs(u_t, tri_t)
        cums_end = jnp.broadcast_to(cums_t[:, end:end + 1], cums_t.shape)
        p1_t = jnp.exp2(cums_t)
        p2_t = jnp.exp2(cums_end - cums_t) * val_t
        packed = jnp.concatenate([cums_t, p1_t, p2_t, val_t], axis=0).T
        ex = _dot(packed.astype(BF16), expand)
        ex1 = ex[:, :D_MODEL]
        ex2 = ex[:, D_MODEL:]
        chunk_decay = _dot_exact_rhs(jnp.broadcast_to(packed[end:end + 1, :], (8, LANES)),
                                     expand[:, :D_MODEL])[0:1, :]

        xs = xs_ref[b]
        xsf = xs.astype(F32)
        xs_w = (xsf * ex2).astype(BF16)
        y_groups = []
        for g in range(SSD_GROUPS):
            cg = cm_ref[b, :, g * SSD_STATE:(g + 1) * SSD_STATE]
            bg = bm_ref[b, :, g * SSD_STATE:(g + 1) * SSD_STATE]
            cb = _dot_nt(cg, bg)
            ys = []
            for pp in range(HEADS_PER_GROUP // 2):
                pair = g * (HEADS_PER_GROUP // 2) + pp
                xs_pair = xs[:, pair * LANES:(pair + 1) * LANES]
                zero_pair = jnp.zeros_like(xs_pair)
                m_mats = []
                for h in (2 * pair, 2 * pair + 1):
                    seg = jnp.broadcast_to(packed[:, h:h + 1], full) - cums_t[h:h + 1, :]
                    dec = jnp.exp2(jnp.where(allowed, seg, -jnp.inf))
                    m_mats.append((cb * dec * val_t[h:h + 1, :]).astype(BF16))
                rhs = jnp.concatenate([jnp.where(left_half, xs_pair, zero_pair),
                                       jnp.where(right_half, xs_pair, zero_pair)], axis=0)
                ys.append(_dot(jnp.concatenate(m_mats, axis=1), rhs))
            y_diag = jnp.concatenate(ys, axis=1)
            gs = slice(g * GROUP_WIDTH, (g + 1) * GROUP_WIDTH)
            s_g = s_old[b][g]
            y_off = _dot(cg, s_g.astype(BF16)) * ex1[:, gs]
            y_groups.append(y_diag + y_off)
            bg_t = bg.astype(F32).T.astype(BF16)
            s_new_all[b].append(chunk_decay[:, gs] * s_g + _dot(bg_t, xs_w[:, gs]))
        y_ssd = jnp.concatenate(y_groups, axis=1)

        bcum_t = cums_t[GATE_F0:GATE_END, :]
        ip_t = val_t[GATE_I0:GATE_F0, :] * LOG2E
        rep = bcum_t.shape
        g_rep = jnp.broadcast_to(bcum_t[:, end:end + 1], rep)
        a_t = g_rep - bcum_t + ip_t
        a_max = jnp.broadcast_to(jnp.max(a_t, axis=1, keepdims=True), rep)
        w_t = jnp.exp2(a_t - a_max)
        m_prev = m_old[b]
        m_new = jnp.maximum(g_rep + m_prev, a_max)
        s_prev = jnp.exp2(g_rep + m_prev - m_new)
        s_new = jnp.exp2(a_max - m_new)
        r_t = ip_t - bcum_t
        h_heads = []
        for pair in range(MLSTM_HEADS // 2):
            h0, h1 = 2 * pair, 2 * pair + 1
            q_pair = q_ref[b, :, pair * LANES:(pair + 1) * LANES]
            k_pair = k_ref[b, :, pair * LANES:(pair + 1) * LANES]
            q_pair_f = q_pair.astype(F32)
            cst = cst_old[b][pair]
            cst_b = cst.astype(BF16)
            v_pair = []
            for hh, h in enumerate((h0, h1)):
                keep = left_half if hh == 0 else right_half
                vh = v_ref[b, :, h * MLSTM_DV:(h + 1) * MLSTM_DV]
                v_pair.append(vh)
                qk = _dot_nt(jnp.where(keep, q_pair, jnp.zeros_like(q_pair)), k_pair)
                bc = jnp.broadcast_to(packed[:, GATE_F0 + h:GATE_F0 + h + 1], full)
                dlog = jnp.where(allowed, bc + r_t[h:h + 1, :], -jnp.inf)
                m_intra = jnp.broadcast_to(jnp.max(dlog, axis=1, keepdims=True), full)
                m_inter = bc + m_prev[h:h + 1, :]
                m_t = jnp.maximum(m_inter, m_intra)
                s_mat = (qk * jnp.exp2(dlog - m_t)).astype(BF16)
                qs = (jnp.where(keep, q_pair_f, 0.0) * jnp.exp2(m_inter - m_t)).astype(BF16)
                tot = _dot(jnp.concatenate([s_mat, qs], axis=1),
                           jnp.concatenate([jnp.concatenate([vh, ones_blk], axis=1), cst_b], axis=0))
                num = tot[:, :MLSTM_DV]
                den = tot[:, MLSTM_DV:]
                h_heads.append(num / jnp.maximum(jnp.abs(den), jnp.exp2(-m_t)))
            w_rows = jnp.where(top_half, w_t[h0:h0 + 1, :], w_t[h1:h1 + 1, :])
            kw = (k_pair.astype(F32).T * w_rows).astype(BF16)
            full_kv = _dot(kw, jnp.concatenate([v_pair[0], v_pair[1], ones_blk], axis=1))
            kvn = jnp.concatenate(
                [jnp.where(top_half, full_kv[:, :MLSTM_DV], full_kv[:, MLSTM_DV:2 * MLSTM_DV]),
                 full_kv[:, 2 * MLSTM_DV:]], axis=1)
            sp_rows = jnp.where(top_half, s_prev[h0:h0 + 1, :], s_prev[h1:h1 + 1, :])
            sn_rows = jnp.where(top_half, s_new[h0:h0 + 1, :], s_new[h1:h1 + 1, :])
            cst_new_all[b].append(jnp.concatenate([sp_rows, sp_rows], axis=1) * cst
                                  + jnp.concatenate([sn_rows, sn_rows], axis=1) * kvn)
        m_new_all[b] = m_new
        h_ml = jnp.concatenate(h_heads, axis=1)
        if not final:
            return y_ssd.astype(BF16), h_ml.astype(BF16)

        y_tot = y_ssd + yb_ref[b].astype(F32) + dskip_ref[...] * xsf
        zz = z_ref[b].astype(F32)
        y2 = y_tot * (zz * _sigmoid(zz))
        y_n = y2 * lax.rsqrt(jnp.mean(y2 * y2, axis=-1, keepdims=True) + RMS_EPS) * ngs_ref[...]
        h_tot = h_ml + hb_ref[b].astype(F32)
        segs = []
        for h in range(MLSTM_HEADS):
            seg = h_tot[:, h * MLSTM_DV:(h + 1) * MLSTM_DV]
            segs.append(seg * lax.rsqrt(jnp.mean(seg * seg, axis=-1, keepdims=True) + RMS_EPS))
        h_n = jnp.concatenate(segs, axis=1) * ngm_ref[...]
        y_ml = _sigmoid(o_ref[b].astype(F32)) * h_n
        return y_n.astype(BF16), y_ml.astype(BF16)

    s_old = [[s_ref[b, g] for g in range(SSD_GROUPS)] for b in range(n_seq)]
    cst_old = [[cst_ref[b, p] for p in range(MLSTM_HEADS // 2)] for b in range(n_seq)]
    m_old = [m_ref[b] for b in range(n_seq)]
    s_new_all = [[] for _ in range(n_seq)]
    cst_new_all = [[] for _ in range(n_seq)]
    m_new_all = [None] * n_seq
    results = [one_sequence(b) for b in range(n_seq)]
    for b in range(n_seq):
        for g in range(SSD_GROUPS):
            s_ref[b, g] = s_new_all[b][g]
        for p in range(MLSTM_HEADS // 2):
            cst_ref[b, p] = cst_new_all[b][p]
        m_ref[b] = m_new_all[b]

    @pl.when(c > 0)
    def _():
        for b, (first, second) in enumerate(results):
            if final:
                ycat_ref[b, :, :D_MODEL] = first
                ycat_ref[b, :, D_MODEL:] = second
            else:
                yout_ref[b] = first
                hout_ref[b] = second


def _mixer_pass(xs, bm, cm, gates, q, k, v, gbias, alog, expand, n_chunks, *, reverse, final_inputs=None):
    final = final_inputs is not None
    n_seq = xs.shape[0]
    seq_len = (n_chunks - 1) * CHUNK

    def chunk_of(t):
        return (n_chunks - 1 - t) if reverse else t

    def pad_map(t):
        return (0, _storage_chunk(chunk_of(t), n_chunks), 0)

    def out_map(t):
        return (0, jnp.maximum(chunk_of(t) - 1, 0), 0)

    const = lambda a: pl.BlockSpec(a.shape, lambda t: (0,) * a.ndim)
    pad_spec = lambda n: pl.BlockSpec((n_seq, CHUNK, n), pad_map)
    out_spec = lambda n: pl.BlockSpec((n_seq, CHUNK, n), out_map)
    in_arrays = [xs, bm, cm, gates, q, k, v]
    in_specs = [pad_spec(a.shape[2]) for a in in_arrays]
    if final:
        z, o, yb, hb, dskip, ngs, ngm = final_inputs
        in_arrays += [z, o, yb, hb]
        in_specs += [pad_spec(1024), pad_spec(1024), out_spec(1024), out_spec(1024)]
        in_arrays += [gbias, alog, expand, dskip, ngs, ngm]
        in_specs += [const(a) for a in (gbias, alog, expand, dskip, ngs, ngm)]
        out_shape = [jax.ShapeDtypeStruct((n_seq, seq_len, 2 * D_MODEL), BF16)]
        out_specs = [out_spec(2 * D_MODEL)]
    else:
        in_arrays += [gbias, alog, expand]
        in_specs += [const(a) for a in (gbias, alog, expand)]
        out_shape = [jax.ShapeDtypeStruct((n_seq, seq_len, D_MODEL), BF16),
                     jax.ShapeDtypeStruct((n_seq, seq_len, D_MODEL), BF16)]
        out_specs = [out_spec(D_MODEL), out_spec(D_MODEL)]
    return pl.pallas_call(
        functools.partial(_mixer_kernel, reverse=reverse, final=final, n_chunks=n_chunks, n_seq=n_seq),
        grid=(n_chunks,),
        in_specs=in_specs,
        out_specs=out_specs,
        out_shape=out_shape,
        scratch_shapes=[pltpu.VMEM((n_seq, SSD_GROUPS, SSD_STATE, GROUP_WIDTH), F32),
                        pltpu.VMEM((n_seq, MLSTM_HEADS // 2, 2 * MLSTM_DK, 2 * MLSTM_DV), F32),
                        pltpu.VMEM((n_seq, MLSTM_HEADS, LANES), F32)],
        compiler_params=pltpu.CompilerParams(
            dimension_semantics=("arbitrary",), vmem_limit_bytes=VMEM_LIMIT_BYTES),
        name="mixer_fwd" if final else "mixer_bwd",
    )(*in_arrays)


def _epilogue_kernel(x_ref, ycat_ref, wout_ref, lng0_ref, lnb0_ref, lng1_ref, lnb1_ref,
                     wrh_ref, wrl_ref, br_ref, h1_ref, h1p_ref, sel_ref, gate_ref):
    h0 = _layer_norm(x_ref[...], lng0_ref[...], lnb0_ref[...])
    mix = _dot(ycat_ref[...], wout_ref[...])
    h1 = _layer_norm(DEEPNORM_ALPHA * h0 + mix, lng1_ref[...], lnb1_ref[...])
    h1_ref[...] = h1
    h1p_ref[...] = _pack_bf16_pairs(h1)
    hh = h1.astype(BF16)
    hl = (h1 - hh.astype(F32)).astype(BF16)
    wrh = wrh_ref[...]
    logits = _dot(hh, wrh) + _dot(hl, wrh) + _dot(hh, wrl_ref[...]) + br_ref[...]
    lane = lax.broadcasted_iota(jnp.int32, (1, LANES), 1)
    lane_f = lane.astype(F32)
    logits = jnp.where(lane < N_EXPERTS, logits, -jnp.inf)
    work = logits
    sel = jnp.zeros(logits.shape, jnp.bool_)
    top = None
    for _ in range(TOP_K):
        m = jnp.max(work, axis=-1, keepdims=True)
        if top is None:
            top = m
        first = jnp.min(jnp.where(work == m, lane_f, float(LANES)), axis=-1, keepdims=True)
        pick = lane_f == first
        sel = jnp.logical_or(sel, pick)
        work = jnp.where(pick, -jnp.inf, work)
    e = jnp.where(sel, jnp.exp(logits - top), 0.0)
    gate_ref[...] = e / jnp.sum(e, axis=-1, keepdims=True)
    sel_ref[...] = sel.astype(F32)


def _epilogue(x, ycat, ycat_row0, w_out, lng0, lnb0, lng1, lnb1, wrh, wrl, br):
    rows = x.shape[0]
    tm = EPILOGUE_ROWS
    assert rows % tm == 0 and ycat_row0 % tm == 0
    tile0 = ycat_row0 // tm
    row_spec = lambda n: pl.BlockSpec((tm, n), lambda i: (i, 0))
    const = lambda a: pl.BlockSpec(a.shape, lambda i: (0,) * a.ndim)
    out_shape = [jax.ShapeDtypeStruct((rows, D_MODEL), F32),
                 jax.ShapeDtypeStruct((rows, D_MODEL // 2), jnp.uint32),
                 jax.ShapeDtypeStruct((rows, LANES), F32),
                 jax.ShapeDtypeStruct((rows, LANES), F32)]
    consts = (w_out, lng0, lnb0, lng1, lnb1, wrh, wrl, br)
    return pl.pallas_call(
        _epilogue_kernel,
        grid=(rows // tm,),
        in_specs=[row_spec(D_MODEL), pl.BlockSpec((tm, 2 * D_MODEL), lambda i: (tile0 + i, 0))]
                 + [const(a) for a in consts],
        out_specs=[row_spec(D_MODEL), row_spec(D_MODEL // 2), row_spec(LANES), row_spec(LANES)],
        out_shape=out_shape,
        compiler_params=pltpu.CompilerParams(
            dimension_semantics=("arbitrary",), vmem_limit_bytes=VMEM_LIMIT_BYTES),
        name="epilogue",
    )(x, ycat, *consts)


def _rank_kernel(sel_ref, gate_ref, lstrict_ref, ucum_ref, dest_ref, gk_ref, stats_ref,
                 base_ref, *, trash_row):
    phase = pl.program_id(0)
    i = pl.program_id(1)
    sel = sel_ref[...]
    colsum = jnp.sum(sel, axis=0, keepdims=True)

    @pl.when(jnp.logical_and(phase == 0, i == 0))
    def _():
        base_ref[...] = jnp.zeros_like(base_ref)

    @pl.when(phase == 0)
    def _():
        base_ref[0:1, :] = base_ref[0:1, :] + colsum

    @pl.when(jnp.logical_and(phase == 1, i == 0))
    def _():
        counts = base_ref[0:1, :]
        padded = jnp.ceil(counts / MOE_BLOCK) * MOE_BLOCK
        pend = _dot_exact_rhs(jnp.broadcast_to(padded, (8, LANES)), ucum_ref[...])[0:1, :]
        stats_ref[0:1, :] = counts
        stats_ref[1:2, :] = pend - padded
        stats_ref[2:3, :] = pend
        stats_ref[3:8, :] = jnp.zeros((5, LANES), F32)
        base_ref[1:2, :] = pend - padded

    @pl.when(phase == 1)
    def _():
        before = _dot(lstrict_ref[...], sel.astype(BF16))
        pos = base_ref[1:2, :] + before
        base_ref[1:2, :] = base_ref[1:2, :] + colsum
        work = jnp.where(sel > 0.0, pos + 1.0, 0.0)
        gates = gate_ref[...]
        for kk in range(TOP_K):
            m = jnp.max(work, axis=-1, keepdims=True)
            pick = jnp.logical_and(work == m, m > 0.0)
            gk_ref[:, kk:kk + 1] = jnp.sum(jnp.where(pick, gates, 0.0), axis=-1, keepdims=True)
            dest_ref[:, kk:kk + 1] = jnp.where(m > 0.0, m - 1.0, float(trash_row)).astype(jnp.int32)
            work = jnp.where(pick, 0.0, work)


def _rank(sel, gates, lstrict, ucum, trash_row):
    rows = sel.shape[0]
    tm = RANK_ROWS
    assert rows % tm == 0
    row_spec = lambda n: pl.BlockSpec((tm, n), lambda p, i: (i, 0))
    out_row_spec = lambda n: pl.BlockSpec((tm, n), lambda p, i: (i * p, 0))
    const = lambda a: pl.BlockSpec(a.shape, lambda p, i: (0,) * a.ndim)
    return pl.pallas_call(
        functools.partial(_rank_kernel, trash_row=trash_row),
        grid=(2, rows // tm),
        in_specs=[row_spec(LANES), row_spec(LANES), const(lstrict), const(ucum)],
        out_specs=[out_row_spec(TOP_K), out_row_spec(TOP_K),
                   pl.BlockSpec((8, LANES), lambda p, i: (0, 0))],
        out_shape=[jax.ShapeDtypeStruct((rows, TOP_K), jnp.int32),
                   jax.ShapeDtypeStruct((rows, TOP_K), F32),
                   jax.ShapeDtypeStruct((8, LANES), F32)],
        scratch_shapes=[pltpu.VMEM((8, LANES), F32)],
        compiler_params=pltpu.CompilerParams(
            dimension_semantics=("arbitrary", "arbitrary"), vmem_limit_bytes=VMEM_LIMIT_BYTES),
        name="rank",
    )(sel, gates, lstrict, ucum)


def _sc_mesh():
    return plsc.VectorSubcoreMesh(core_axis_name="c", subcore_axis_name="s",
                                  num_cores=SC_CORES, num_subcores=SC_SUBCORES)


def _sc_scatter_rows(src, idx, n_out_rows):
    n_src, d = src.shape
    w, k = SC_WINDOW, SC_SCATTER_INFLIGHT
    assert idx.shape == (n_src // w, TOP_K, w) and n_src % (w * k * SC_WORKERS) == 0
    per_worker = n_src // (w * SC_WORKERS)
    idx2d = idx.reshape(n_src // w * TOP_K, w)

    @functools.partial(
        pl.kernel, mesh=_sc_mesh(),
        out_type=jax.ShapeDtypeStruct((n_out_rows, d), src.dtype),
        scratch_types=[pltpu.VMEM((k * TOP_K, w), jnp.int32), pltpu.VMEM((k, w, d), src.dtype),
                       pltpu.SemaphoreType.DMA((k,)), pltpu.SemaphoreType.DMA((k,))],
        name="sc_scatter_rows")
    def body(src_hbm, idx_hbm, out_hbm, idx_v, rows_v, sem_load, sem_scatter):
        wid = lax.axis_index("s") * SC_CORES + lax.axis_index("c")

        @pl.loop(0, per_worker // k)
        def _(i):
            win0 = wid * per_worker + i * k
            loads = [pltpu.async_copy(src_hbm.at[pl.ds(pl.multiple_of((win0 + b) * w, w), w)], rows_v.at[b],
                                      sem_load.at[b]) for b in range(k)]
            pltpu.sync_copy(idx_hbm.at[pl.ds(pl.multiple_of(win0 * TOP_K, TOP_K), k * TOP_K)], idx_v)
            scatters = []
            for b in range(k):
                loads[b].wait()
                scatters += [pltpu.async_copy(rows_v.at[b], out_hbm.at[idx_v.at[b * TOP_K + kk]],
                                              sem_scatter.at[b]) for kk in range(TOP_K)]
            for copy in scatters:
                copy.wait()

    return body(src, idx2d)


def _sc_gather_rows(table, idx):
    d = table.shape[1]
    w, k = SC_WINDOW, SC_GATHER_INFLIGHT
    assert idx.shape[0] % (w * k * SC_WORKERS) == 0
    per_worker = idx.shape[0] // (w * SC_WORKERS)
    idx2d = idx.reshape(-1, w)

    @functools.partial(
        pl.kernel, mesh=_sc_mesh(),
        out_type=jax.ShapeDtypeStruct((idx.shape[0], d), table.dtype),
        scratch_types=[pltpu.VMEM((k, w), jnp.int32), pltpu.VMEM((k, w, d), table.dtype),
                       pltpu.SemaphoreType.DMA((k,)), pltpu.SemaphoreType.DMA((k,))],
        name="sc_gather_rows")
    def body(table_hbm, idx_hbm, out_hbm, idx_v, rows_v, sem_gather, sem_store):
        wid = lax.axis_index("s") * SC_CORES + lax.axis_index("c")

        @pl.loop(0, per_worker // k)
        def _(i):
            win0 = wid * per_worker + i * k
            pltpu.sync_copy(idx_hbm.at[pl.ds(pl.multiple_of(win0, k), k)], idx_v)
            gathers = [pltpu.async_copy(table_hbm.at[idx_v.at[b]], rows_v.at[b], sem_gather.at[b])
                       for b in range(k)]
            stores = []
            for b in range(k):
                gathers[b].wait()
                stores.append(pltpu.async_copy(
                    rows_v.at[b], out_hbm.at[pl.ds(pl.multiple_of((win0 + b) * w, w), w)], sem_store.at[b]))
            for copy in stores:
                copy.wait()

    return body(table, idx2d)


def _ffn_kernel(blk_ref, exp_ref, valid_ref, nused_ref, xb_ref, w1_ref, b1_ref, w2_ref, b2_ref, yb_ref,
                w1b_ref, w2b_ref):
    j = pl.program_id(0)
    active = j < nused_ref[0]

    @pl.when(jnp.logical_and(active, jnp.logical_or(j == 0, exp_ref[j] != exp_ref[jnp.maximum(j - 1, 0)])))
    def _():
        w1b_ref[...] = w1_ref[0].astype(BF16)
        w2b_ref[...] = w2_ref[0].astype(BF16)

    @pl.when(active)
    def _():
        row = lax.broadcasted_iota(jnp.int32, (MOE_BLOCK, 1), 0)
        x = jnp.where(row < valid_ref[j], _unpack_bf16_pairs(xb_ref[...]), 0.0).astype(BF16)
        hc = _dot(x, w1b_ref[...]) + b1_ref[0]
        gate = jnp.minimum(hc[:, :D_FF], SWIGLU_LIMIT)
        up = jnp.clip(hc[:, D_FF:], -SWIGLU_LIMIT, SWIGLU_LIMIT)
        act = (up + 1.0) * gate * _sigmoid(SWIGLU_ALPHA * gate)
        yb_ref[...] = _pack_bf16_pairs(_dot(act.astype(BF16), w2b_ref[...]) + b2_ref[0])


def _ffn(blk_idx, blk_exp, blk_valid, n_used, xb, w1, b1, w2, b2, n_blocks):
    bm = MOE_BLOCK
    grid_spec = pltpu.PrefetchScalarGridSpec(
        num_scalar_prefetch=4,
        grid=(n_blocks,),
        in_specs=[pl.BlockSpec((bm, D_MODEL // 2), lambda j, bi, be, bv, nu: (bi[j], 0)),
                  pl.BlockSpec((1, D_MODEL, 2 * D_FF), lambda j, bi, be, bv, nu: (be[j], 0, 0)),
                  pl.BlockSpec((1, 1, 2 * D_FF), lambda j, bi, be, bv, nu: (be[j], 0, 0)),
                  pl.BlockSpec((1, D_FF, D_MODEL), lambda j, bi, be, bv, nu: (be[j], 0, 0)),
                  pl.BlockSpec((1, 1, D_MODEL), lambda j, bi, be, bv, nu: (be[j], 0, 0))],
        out_specs=pl.BlockSpec((bm, D_MODEL // 2), lambda j, bi, be, bv, nu: (bi[j], 0)),
        scratch_shapes=[pltpu.VMEM((D_MODEL, 2 * D_FF), BF16), pltpu.VMEM((D_FF, D_MODEL), BF16)],
    )
    return pl.pallas_call(
        _ffn_kernel,
        grid_spec=grid_spec,
        out_shape=jax.ShapeDtypeStruct(xb.shape, jnp.uint32),
        compiler_params=pltpu.CompilerParams(
            dimension_semantics=("arbitrary",), vmem_limit_bytes=VMEM_LIMIT_BYTES),
        name="expert_ffn",
    )(blk_idx, blk_exp, blk_valid, n_used, xb, w1, b1, w2, b2)


def _combine_kernel(gk_ref, h1_ref, y0_ref, y1_ref, y2_ref, y3_ref, lng_ref, lnb_ref, out_ref):
    gk = gk_ref[...]
    ffn = gk[:, 0:1] * _unpack_bf16_pairs(y0_ref[...])
    for kk, y_ref in enumerate((y1_ref, y2_ref, y3_ref), start=1):
        ffn = ffn + gk[:, kk:kk + 1] * _unpack_bf16_pairs(y_ref[...])
    out_ref[...] = _layer_norm(DEEPNORM_ALPHA * h1_ref[...] + ffn, lng_ref[...], lnb_ref[...])


def _combine(gk, h1, ysel, lng, lnb):
    rows = h1.shape[0]
    tm = COMBINE_ROWS
    assert rows % tm == 0
    n_tiles = rows // tm
    const = lambda a: pl.BlockSpec(a.shape, lambda i: (0,) * a.ndim)
    ksel = lambda kk: pl.BlockSpec((tm, D_MODEL // 2), lambda i: (kk * n_tiles + i, 0))
    return pl.pallas_call(
        _combine_kernel,
        grid=(n_tiles,),
        in_specs=[pl.BlockSpec((tm, TOP_K), lambda i: (i, 0)),
                  pl.BlockSpec((tm, D_MODEL), lambda i: (i, 0)),
                  ksel(0), ksel(1), ksel(2), ksel(3),
                  const(lng), const(lnb)],
        out_specs=pl.BlockSpec((tm, D_MODEL), lambda i: (i, 0)),
        out_shape=jax.ShapeDtypeStruct((rows, D_MODEL), F32),
        compiler_params=pltpu.CompilerParams(
            dimension_semantics=("arbitrary",), vmem_limit_bytes=VMEM_LIMIT_BYTES),
        name="combine",
    )(gk, h1, ysel, ysel, ysel, ysel, lng, lnb)


def _row(v, width=None):
    v = v.reshape(1, -1).astype(F32)
    if width is not None and v.shape[1] < width:
        v = jnp.pad(v, ((0, 0), (0, width - v.shape[1])))
    return v


def _encode_all(x_a, x_b, meta_tokens, ln_emb_g, ln_emb_b, w_in, conv_w, conv_b, dt_bias, a_log,
                d_skip, ssd_norm_g, i_bias, f_bias, mlstm_norm_g, w_out, ln1_g, ln1_b, w_router, b_router,
                w1, b1, w2, b2, ln2_g, ln2_b):
    n_a, seq_len, _ = x_a.shape
    n_seq = n_a + x_b.shape[0]
    rows_a = n_a * seq_len
    assert seq_len % CHUNK == 0
    n_chunks = seq_len // CHUNK + 1
    n_tok = n_seq * seq_len

    sizes = (1024, CONV_CH, 2 * SSD_HEADS, 512, 512, 1024, 1024, 2 * MLSTM_HEADS, 2 * MLSTM_HEADS)
    offs = [0]
    for s in sizes:
        offs.append(offs[-1] + s)
    w_z, w_xbc, w_dt, w_q, w_k, w_v, w_o, w_i, w_f = [w_in[:, offs[j]:offs[j + 1]] for j in range(9)]
    w_big = jnp.concatenate([w_z, w_xbc, w_q, w_k * (MLSTM_DK ** -0.5), w_v, w_o], axis=1).astype(BF16)
    zpad = jnp.zeros((D_MODEL, LANES - GATE_END), F32)
    gate_cols = []
    for d in range(2):
        gate_cols += [w_dt[:, d * SSD_HEADS:(d + 1) * SSD_HEADS],
                      w_i[:, d * MLSTM_HEADS:(d + 1) * MLSTM_HEADS],
                      w_f[:, d * MLSTM_HEADS:(d + 1) * MLSTM_HEADS], zpad]
    w_gates = jnp.concatenate(gate_cols, axis=1).astype(BF16)
    gbias = [_row(jnp.concatenate([dt_bias[d], i_bias[d], f_bias[d]]), LANES) for d in range(2)]
    alog = [jnp.pad(jnp.broadcast_to(a_log[d].astype(F32)[:, None], (SSD_HEADS, LANES)),
                    ((0, GATE_END - SSD_HEADS), (0, 0))) for d in range(2)]
    head_of_col = jnp.arange(D_MODEL, dtype=jnp.int32) // SSD_HEAD_DIM
    lane_id = jnp.arange(LANES, dtype=jnp.int32)[:, None]
    expand = jnp.concatenate([lane_id == GATE_END + head_of_col[None, :],
                              lane_id == 2 * GATE_END + head_of_col[None, :]], axis=1).astype(BF16)
    dskip = _row(jnp.repeat(d_skip, SSD_HEAD_DIM))
    conv_w8 = jnp.pad(conv_w.astype(F32), ((0, 8 - CONV_W), (0, 0)))
    meta_tile = jnp.pad(meta_tokens.astype(F32), ((PAD_FRONT, INPROJ_ROWS - CHUNK), (0, 0)))

    z, xbc, q, k, v, o, gates_f, gates_b = _inproj(x_a, x_b, meta_tile, _row(ln_emb_g), _row(ln_emb_b),
                                                   w_big, w_gates)
    xs, bm, cm = _conv(xbc, conv_w8, _row(conv_b), n_chunks)
    yb, hb = _mixer_pass(xs, bm, cm, gates_b, q, k, v, gbias[1], alog[1], expand, n_chunks, reverse=True)
    (ycat,) = _mixer_pass(xs, bm, cm, gates_f, q, k, v, gbias[0], alog[0], expand, n_chunks,
                          reverse=False,
                          final_inputs=(z, o, yb, hb, dskip, _row(ssd_norm_g), _row(mlstm_norm_g)))

    wr = jnp.pad(w_router.astype(F32), ((0, 0), (0, LANES - N_EXPERTS)))
    wrh = wr.astype(BF16)
    wrl = (wr - wrh.astype(F32)).astype(BF16)
    w_out_b = w_out.astype(BF16)
    r_i = jnp.arange(RANK_ROWS, dtype=jnp.int32)
    lstrict = (r_i[None, :] < r_i[:, None]).astype(BF16)
    l_i = jnp.arange(LANES, dtype=jnp.int32)
    ucum = (l_i[:, None] <= l_i[None, :]).astype(BF16)
    ycat2d = ycat.reshape(n_tok, 2 * D_MODEL)
    outs = []
    for x_part, row0 in ((x_a, 0), (x_b, rows_a)):
        n_part = x_part.shape[0] * seq_len
        h1, h1p, sel, gates = _epilogue(x_part.reshape(n_part, D_MODEL), ycat2d, row0, w_out_b,
                                        _row(ln_emb_g), _row(ln_emb_b), _row(ln1_g), _row(ln1_b),
                                        wrh, wrl, _row(b_router, LANES))
        n_blocks = n_part * TOP_K // MOE_BLOCK + N_EXPERTS
        trash_row = n_blocks * MOE_BLOCK
        dest, gk, stats = _rank(sel, gates, lstrict, ucum, trash_row)
        counts = stats[0, :N_EXPERTS].astype(jnp.int32)
        starts = stats[1, :N_EXPERTS].astype(jnp.int32)
        pends = stats[2, :N_EXPERTS].astype(jnp.int32)
        n_used = pends[N_EXPERTS - 1] // MOE_BLOCK
        blk = jnp.minimum(jnp.arange(n_blocks, dtype=jnp.int32), jnp.maximum(n_used - 1, 0))
        blk_exp = jnp.minimum(
            jnp.sum((pends[None, :] <= (blk * MOE_BLOCK)[:, None]).astype(jnp.int32), axis=1),
            N_EXPERTS - 1).astype(jnp.int32)
        blk_last = jnp.take(starts + counts, blk_exp)
        blk_valid = jnp.clip(blk_last - blk * MOE_BLOCK, 0, MOE_BLOCK).astype(jnp.int32)
        dest_km = dest.T.reshape(-1)
        dest_wm = dest.reshape(n_part // SC_WINDOW, SC_WINDOW, TOP_K).transpose(0, 2, 1)
        xb = _sc_scatter_rows(h1p, dest_wm, trash_row + MOE_BLOCK)
        yexp = _ffn(blk, blk_exp, blk_valid, n_used.reshape(1), xb, w1,
                    b1.reshape(N_EXPERTS, 1, -1), w2, b2.reshape(N_EXPERTS, 1, -1), n_blocks)
        ysel = _sc_gather_rows(yexp, dest_km)
        outs.append(_combine(gk, h1, ysel, _row(ln2_g), _row(ln2_b)))
    return outs


def kernel(x_prompt, x_sample, meta_tokens, ln_emb_g, ln_emb_b, w_in, conv_w, conv_b, dt_bias, a_log,
           d_skip, ssd_norm_g, i_bias, f_bias, mlstm_norm_g, w_out, ln1_g, ln1_b, w_router, b_router,
           w1, b1, w2, b2, ln2_g, ln2_b):
    assert x_prompt.shape[1:] == x_sample.shape[1:]
    n_p, seq_len, d = x_prompt.shape
    n_s = x_sample.shape[0]
    y_p, y_s = _encode_all(x_prompt.astype(F32), x_sample.astype(F32), meta_tokens, ln_emb_g, ln_emb_b, w_in[0], conv_w[0],
                           conv_b[0], dt_bias[0], a_log[0], d_skip[0], ssd_norm_g[0], i_bias[0], f_bias[0],
                           mlstm_norm_g[0], w_out[0], ln1_g[0], ln1_b[0], w_router[0], b_router[0],
                           w1[0], b1[0], w2[0], b2[0], ln2_g[0], ln2_b[0])
    return (y_p.reshape(n_p, seq_len, d), y_s.reshape(n_s, seq_len, d))
```

```python
import functools

import jax
import jax.numpy as jnp
from jax import lax
from jax.experimental import pallas as pl
from jax.experimental.pallas import tpu as pltpu
from jax.experimental.pallas import tpu_sc as plsc

F32 = jnp.float32
BF16 = jnp.bfloat16

D_MODEL = 1024
N_META = 16
CHUNK = 128
PAD_FRONT = CHUNK - N_META
SSD_HEADS = 16
SSD_HEAD_DIM = 64
SSD_GROUPS = 4
SSD_STATE = 128
HEADS_PER_GROUP = SSD_HEADS // SSD_GROUPS
GROUP_WIDTH = HEADS_PER_GROUP * SSD_HEAD_DIM
CONV_W = 5
CONV_HALF = CONV_W // 2
CONV_CH = D_MODEL + 2 * SSD_GROUPS * SSD_STATE
MLSTM_HEADS = 8
MLSTM_DK = 64
MLSTM_DV = 128
N_EXPERTS = 32
TOP_K = 4
D_FF = D_MODEL
SWIGLU_LIMIT = 7.0
SWIGLU_ALPHA = 1.702
DEEPNORM_ALPHA = 2.0 ** 0.25
LN_EPS = 1e-5
RMS_EPS = 1e-5
NEG_GATE = -1e30
LOG2E = 1.4426950408889634

LANES = 128
BF16_SUBLANES = 16
VMEM_LIMIT_BYTES = 56 * 1024 * 1024

GATE_DT0, GATE_I0, GATE_F0, GATE_END = 0, SSD_HEADS, SSD_HEADS + MLSTM_HEADS, SSD_HEADS + 2 * MLSTM_HEADS

INPROJ_ROWS = 512
EPILOGUE_ROWS = 512
RANK_ROWS = 512
COMBINE_ROWS = 512
MOE_BLOCK = 512

SC_CORES = 2
SC_SUBCORES = 16
SC_WORKERS = SC_CORES * SC_SUBCORES
SC_WINDOW = 32
SC_SCATTER_INFLIGHT = 2
SC_GATHER_INFLIGHT = 4


def _dot(a, b):
    return jnp.dot(a, b, preferred_element_type=F32)


def _dot_nt(a, b):
    return lax.dot_general(a, b, (((1,), (1,)), ((), ())), preferred_element_type=F32)


def _split3(x):
    hi = x.astype(BF16)
    r1 = x - hi.astype(F32)
    mid = r1.astype(BF16)
    lo = (r1 - mid.astype(F32)).astype(BF16)
    return hi, mid, lo


def _dot_exact_lhs(a_bf16, x):
    hi, mid, lo = _split3(x)
    return _dot(a_bf16, hi) + _dot(a_bf16, mid) + _dot(a_bf16, lo)


def _dot_exact_rhs(x, b_bf16):
    hi, mid, lo = _split3(x)
    return _dot(hi, b_bf16) + _dot(mid, b_bf16) + _dot(lo, b_bf16)


def _layer_norm(x, g, b):
    mu = jnp.mean(x, axis=-1, keepdims=True)
    xc = x - mu
    var = jnp.mean(xc * xc, axis=-1, keepdims=True)
    return xc * lax.rsqrt(var + LN_EPS) * g + b


def _sigmoid(x):
    return 1.0 / (1.0 + jnp.exp(-x))


def _log1p_exp_neg_abs(x):
    return jnp.log(1.0 + jnp.exp(-jnp.abs(x)))


def _pack_bf16_pairs(x):
    n = x.shape[1] // 2
    bits = lax.bitcast_convert_type(x.astype(BF16).astype(F32), jnp.uint32)
    return (bits[:, :n] >> 16) | bits[:, n:]


def _unpack_bf16_pairs(words):
    lo = lax.bitcast_convert_type(words << 16, F32)
    hi = lax.bitcast_convert_type(words & jnp.uint32(0xFFFF0000), F32)
    return jnp.concatenate([lo, hi], axis=1)


def _storage_chunk(c, n_chunks):
    return jnp.where(c == 0, n_chunks - 1, c - 1)


def _inproj_kernel(xa_ref, xb_ref, meta_ref, g_ref, b_ref, wbig_ref, wg_ref,
                   z_ref, xbc_ref, q_ref, k_ref, v_ref, o_ref, gf_ref, gb_ref, *, n_x_tiles, n_a):
    x = jnp.where(pl.program_id(0) < n_a, xa_ref[...], xb_ref[...])
    x = jnp.where(pl.program_id(1) == n_x_tiles, meta_ref[...], x)
    h = _layer_norm(x, g_ref[...], b_ref[...]).astype(BF16)

    def mm(c0, c1):
        return _dot(h, wbig_ref[:, c0:c1]).astype(BF16)

    z_ref[...] = mm(0, 1024)
    xbc_ref[:, 0:1024] = mm(1024, 2048)
    xbc_ref[:, 1024:2048] = mm(2048, 3072)
    q_ref[...] = mm(3072, 3584)
    k_ref[...] = mm(3584, 4096)
    v_ref[...] = mm(4096, 5120)
    o_ref[...] = mm(5120, 6144)
    gates = _dot(h, wg_ref[...])
    gf_ref[...] = gates[:, :LANES]
    gb_ref[...] = gates[:, LANES:]


def _inproj(x_a, x_b, meta_tile, ln_g, ln_b, w_big, w_gates):
    n_a, seq_len, _ = x_a.shape
    n_seq = n_a + x_b.shape[0]
    tm = INPROJ_ROWS
    assert seq_len % tm == 0 and tm >= CHUNK
    n_x_tiles = seq_len // tm
    rows = seq_len + CHUNK
    row_spec = lambda n: pl.BlockSpec((None, tm, n), lambda b, i: (b, i, 0))
    const = lambda a: pl.BlockSpec(a.shape, lambda b, i: (0,) * a.ndim)
    resident = lambda a: pl.BlockSpec(a.shape, lambda b, i: (0,) * a.ndim, pipeline_mode=pl.Buffered(1))
    widths = (1024, CONV_CH, 512, 512, 1024, 1024)
    out_shapes = [jax.ShapeDtypeStruct((n_seq, rows, w), BF16) for w in widths]
    out_shapes += [jax.ShapeDtypeStruct((n_seq, rows, LANES), F32)] * 2
    return pl.pallas_call(
        functools.partial(_inproj_kernel, n_x_tiles=n_x_tiles, n_a=n_a),
        grid=(n_seq, n_x_tiles + 1),
        in_specs=[pl.BlockSpec((None, tm, D_MODEL), lambda b, i: (
                      jnp.minimum(b, n_a - 1),
                      jnp.where(b < n_a, jnp.minimum(i, n_x_tiles - 1), n_x_tiles - 1), 0)),
                  pl.BlockSpec((None, tm, D_MODEL), lambda b, i: (
                      jnp.maximum(b - n_a, 0),
                      jnp.where(b < n_a, 0, jnp.minimum(i, n_x_tiles - 1)), 0)),
                  resident(meta_tile), const(ln_g), const(ln_b), resident(w_big), resident(w_gates)],
        out_specs=[row_spec(s.shape[2]) for s in out_shapes],
        out_shape=out_shapes,
        compiler_params=pltpu.CompilerParams(
            dimension_semantics=("arbitrary", "arbitrary"), vmem_limit_bytes=VMEM_LIMIT_BYTES),
        name="inproj",
    )(x_a, x_b, meta_tile, ln_g, ln_b, w_big, w_gates)


def _conv_kernel(prev_ref, main_ref, next_ref, shift_ref, w_ref, b_ref, xs_ref, bm_ref, cm_ref,
                 *, n_x_tiles, chunks_per_tile):
    i = pl.program_id(1)
    w = w_ref[...]
    bias = b_ref[...]
    shift = shift_ref[...]

    def conv_chunk(before, rows, after, pad_rows, j):
        shifted = _dot(shift, jnp.concatenate([before, rows, after], axis=0).astype(BF16))
        acc = bias + rows * w[CONV_HALF:CONV_HALF + 1, :]
        for jj, t in enumerate(t for t in range(CONV_W) if t != CONV_HALF):
            acc = acc + shifted[jj * CHUNK:(jj + 1) * CHUNK, :] * w[t:t + 1, :]
        y = acc * _sigmoid(acc)
        if pad_rows is not None:
            y = jnp.where(pad_rows, 0.0, y)
        r = slice(j * CHUNK, (j + 1) * CHUNK)
        xs_ref[r, :] = y[:, :D_MODEL].astype(BF16)
        bm_ref[r, :] = y[:, D_MODEL:D_MODEL + 512].astype(BF16)
        cm_ref[r, :] = y[:, D_MODEL + 512:].astype(BF16)

    @pl.when(i < n_x_tiles)
    def _():
        tile = main_ref[...].astype(F32)
        for j in range(chunks_per_tile):
            lo, hi = j * CHUNK, (j + 1) * CHUNK
            before = prev_ref[...].astype(F32) if j == 0 else tile[lo - BF16_SUBLANES:lo, :]
            if j == chunks_per_tile - 1:
                after = jnp.where(i == n_x_tiles - 1, 0.0, next_ref[...].astype(F32))
            else:
                after = tile[hi:hi + BF16_SUBLANES, :]
            conv_chunk(before, tile[lo:hi, :], after, None, j)

    @pl.when(i == n_x_tiles)
    def _():
        row = lax.broadcasted_iota(jnp.int32, (CHUNK, 1), 0)
        pad_rows = row < PAD_FRONT
        rows = jnp.where(pad_rows, 0.0, main_ref[0:CHUNK, :].astype(F32))
        conv_chunk(jnp.zeros((BF16_SUBLANES, CONV_CH), F32), rows, next_ref[...].astype(F32), pad_rows, 0)


def _conv(xbc, conv_w8, conv_b, n_chunks):
    n_seq, rows, _ = xbc.shape
    tm = INPROJ_ROWS
    seq_len = (n_chunks - 1) * CHUNK
    assert seq_len % tm == 0 and tm % CHUNK == 0
    n_x_tiles = seq_len // tm
    halo_per_tile = tm // BF16_SUBLANES
    meta_last_halo = rows // BF16_SUBLANES - 1

    def prev_map(b, i):
        before_tile = jnp.maximum(jnp.minimum(i, n_x_tiles - 1) * halo_per_tile - 1, 0)
        return (b, jnp.where(i == 0, meta_last_halo, before_tile), 0)

    def next_map(b, i):
        return (b, jnp.where(i >= n_x_tiles - 1, 0, (i + 1) * halo_per_tile) , 0)

    const = lambda a: pl.BlockSpec(a.shape, lambda b, i: (0,) * a.ndim)
    l_i = jnp.arange(CHUNK, dtype=jnp.int32)[:, None]
    j_i = jnp.arange(CHUNK + 2 * BF16_SUBLANES, dtype=jnp.int32)[None, :]
    shifts = jnp.concatenate([(j_i == BF16_SUBLANES + l_i + t - CONV_HALF)
                              for t in range(CONV_W) if t != CONV_HALF], axis=0).astype(BF16)
    out_shapes = [jax.ShapeDtypeStruct((n_seq, rows, D_MODEL), BF16),
                  jax.ShapeDtypeStruct((n_seq, rows, 512), BF16),
                  jax.ShapeDtypeStruct((n_seq, rows, 512), BF16)]
    tile_spec = lambda n: pl.BlockSpec((None, tm, n), lambda b, i: (b, i, 0))
    return pl.pallas_call(
        functools.partial(_conv_kernel, n_x_tiles=n_x_tiles, chunks_per_tile=tm // CHUNK),
        grid=(n_seq, n_x_tiles + 1),
        in_specs=[pl.BlockSpec((None, BF16_SUBLANES, CONV_CH), prev_map),
                  tile_spec(CONV_CH),
                  pl.BlockSpec((None, BF16_SUBLANES, CONV_CH), next_map),
                  const(shifts), const(conv_w8), const(conv_b)],
        out_specs=[tile_spec(s.shape[2]) for s in out_shapes],
        out_shape=out_shapes,
        compiler_params=pltpu.CompilerParams(
            dimension_semantics=("arbitrary", "arbitrary"), vmem_limit_bytes=VMEM_LIMIT_BYTES),
        name="conv",
    )(xbc, xbc, xbc, shifts, conv_w8, conv_b)


def _mixer_kernel(*refs, reverse, final, n_chunks, n_seq):
    if final:
        (xs_ref, bm_ref, cm_ref, g_ref, q_ref, k_ref, v_ref, z_ref, o_ref, yb_ref, hb_ref,
         gbias_ref, alog_ref, expand_ref, dskip_ref, ngs_ref, ngm_ref,
         ycat_ref, s_ref, cst_ref, m_ref) = refs
    else:
        (xs_ref, bm_ref, cm_ref, g_ref, q_ref, k_ref, v_ref,
         gbias_ref, alog_ref, expand_ref,
         yout_ref, hout_ref, s_ref, cst_ref, m_ref) = refs

    t = pl.program_id(0)
    c = (n_chunks - 1 - t) if reverse else t
    end = 0 if reverse else CHUNK - 1

    @pl.when(t == 0)
    def _():
        s_ref[...] = jnp.zeros_like(s_ref)
        cst_ref[...] = jnp.zeros_like(cst_ref)
        m_ref[...] = jnp.zeros_like(m_ref)

    row = lax.broadcasted_iota(jnp.int32, (CHUNK, 1), 0)
    col = lax.broadcasted_iota(jnp.int32, (1, CHUNK), 1)
    lane = col
    allowed = (col >= row) if reverse else (col <= row)
    tri = allowed.astype(BF16)
    tri_t = ((row >= col) if reverse else (row <= col)).astype(BF16)
    feat = lax.broadcasted_iota(jnp.int32, (GATE_END, 1), 0)
    is_dt = feat < GATE_I0
    is_i = jnp.logical_and(feat >= GATE_I0, feat < GATE_F0)
    is_f = feat >= GATE_F0
    pad_cols = jnp.logical_and(c == 0, col < PAD_FRONT)
    a_coef = -jnp.exp(alog_ref[...]) * LOG2E
    expand = expand_ref[...]
    left_half = lane < SSD_HEAD_DIM
    right_half = jnp.logical_not(left_half)
    top_half = row < MLSTM_DK
    ones_blk = jnp.ones((CHUNK, MLSTM_DV), BF16)
    full = (CHUNK, LANES)

    def one_sequence(b):
        gr = (g_ref[b] + gbias_ref[...]).T[0:GATE_END, :]
        lse = _log1p_exp_neg_abs(gr)
        val_t = jnp.where(is_dt, jnp.maximum(gr, 0.0) + lse, jnp.where(is_i, gr, jnp.minimum(gr, 0.0) - lse))
        val_t = jnp.where(pad_cols, jnp.where(is_i, NEG_GATE, 0.0), val_t)
        u_t = jnp.where(is_dt, val_t * a_coef, jnp.where(is_f, val_t * LOG2E, 0.0))
        cums_t = _dot_exact_rhs(u_t, tri_t)
        cums_end = jnp.broadcast_to(cums_t[:, end:end + 1], cums_t.shape)
        p1_t = jnp.exp2(cums_t)
        p2_t = jnp.exp2(cums_end - cums_t) * val_t
        packed = jnp.concatenate([cums_t, p1_t, p2_t, val_t], axis=0).T
        ex = _dot(packed.astype(BF16), expand)
        ex1 = ex[:, :D_MODEL]
        ex2 = ex[:, D_MODEL:]
        chunk_decay = _dot_exact_rhs(jnp.broadcast_to(packed[end:end + 1, :], (8, LANES)),
                                     expand[:, :D_MODEL])[0:1, :]

        xs = xs_ref[b]
        xsf = xs.astype(F32)
        xs_w = (xsf * ex2).astype(BF16)
        y_groups = []
        for g in range(SSD_GROUPS):
            cg = cm_ref[b, :, g * SSD_STATE:(g + 1) * SSD_STATE]
            bg = bm_ref[b, :, g * SSD_STATE:(g + 1) * SSD_STATE]
            cb = _dot_nt(cg, bg)
            ys = []
            for pp in range(HEADS_PER_GROUP // 2):
                pair = g * (HEADS_PER_GROUP // 2) + pp
                xs_pair = xs[:, pair * LANES:(pair + 1) * LANES]
                zero_pair = jnp.zeros_like(xs_pair)
                m_mats = []
                for h in (2 * pair, 2 * pair + 1):
                    seg = jnp.broadcast_to(packed[:, h:h + 1], full) - cums_t[h:h + 1, :]
                    dec = jnp.exp2(jnp.where(allowed, seg, -jnp.inf))
                    m_mats.append((cb * dec * val_t[h:h + 1, :]).astype(BF16))
                rhs = jnp.concatenate([jnp.where(left_half, xs_pair, zero_pair),
                                       jnp.where(right_half, xs_pair, zero_pair)], axis=0)
                ys.append(_dot(jnp.concatenate(m_mats, axis=1), rhs))
            y_diag = jnp.concatenate(ys, axis=1)
            gs = slice(g * GROUP_WIDTH, (g + 1) * GROUP_WIDTH)
            s_g = s_old[b][g]
            y_off = _dot(cg, s_g.astype(BF16)) * ex1[:, gs]
            y_groups.append(y_diag + y_off)
            bg_t = bg.astype(F32).T.astype(BF16)
            s_new_all[b].append(chunk_decay[:, gs] * s_g + _dot(bg_t, xs_w[:, gs]))
        y_ssd = jnp.concatenate(y_groups, axis=1)

        bcum_t = cums_t[GATE_F0:GATE_END, :]
        ip_t = val_t[GATE_I0:GATE_F0, :] * LOG2E
        rep = bcum_t.shape
        g_rep = jnp.broadcast_to(bcum_t[:, end:end + 1], rep)
        a_t = g_rep - bcum_t + ip_t
        a_max = jnp.broadcast_to(jnp.max(a_t, axis=1, keepdims=True), rep)
        w_t = jnp.exp2(a_t - a_max)
        m_prev = m_old[b]
        m_new = jnp.maximum(g_rep + m_prev, a_max)
        s_prev = jnp.exp2(g_rep + m_prev - m_new)
        s_new = jnp.exp2(a_max - m_new)
        r_t = ip_t - bcum_t
        h_heads = []
        for pair in range(MLSTM_HEADS // 2):
            h0, h1 = 2 * pair, 2 * pair + 1
            q_pair = q_ref[b, :, pair * LANES:(pair + 1) * LANES]
            k_pair = k_ref[b, :, pair * LANES:(pair + 1) * LANES]
            q_pair_f = q_pair.astype(F32)
            cst = cst_old[b][pair]
            cst_b = cst.astype(BF16)
            v_pair = []
            for hh, h in enumerate((h0, h1)):
                keep = left_half if hh == 0 else right_half
                vh = v_ref[b, :, h * MLSTM_DV:(h + 1) * MLSTM_DV]
                v_pair.append(vh)
                qk = _dot_nt(jnp.where(keep, q_pair, jnp.zeros_like(q_pair)), k_pair)
                bc = jnp.broadcast_to(packed[:, GATE_F0 + h:GATE_F0 + h + 1], full)
                dlog = jnp.where(allowed, bc + r_t[h:h + 1, :], -jnp.inf)
                m_intra = jnp.broadcast_to(jnp.max(dlog, axis=1, keepdims=True), full)
                m_inter = bc + m_prev[h:h + 1, :]
                m_t = jnp.maximum(m_inter, m_intra)
                s_mat = (qk * jnp.exp2(dlog - m_t)).astype(BF16)
                qs = (jnp.where(keep, q_pair_f, 0.0) * jnp.exp2(m_inter - m_t)).astype(BF16)
                tot = _dot(jnp.concatenate([s_mat, qs], axis=1),
                           jnp.concatenate([jnp.concatenate([vh, ones_blk], axis=1), cst_b], axis=0))
                num = tot[:, :MLSTM_DV]
                den = tot[:, MLSTM_DV:]
                h_heads.append(num / jnp.maximum(jnp.abs(den), jnp.exp2(-m_t)))
            w_rows = jnp.where(top_half, w_t[h0:h0 + 1, :], w_t[h1:h1 + 1, :])
            kw = (k_pair.astype(F32).T * w_rows).astype(BF16)
            full_kv = _dot(kw, jnp.concatenate([v_pair[0], v_pair[1], ones_blk], axis=1))
            kvn = jnp.concatenate(
                [jnp.where(top_half, full_kv[:, :MLSTM_DV], full_kv[:, MLSTM_DV:2 * MLSTM_DV]),
                 full_kv[:, 2 * MLSTM_DV:]], axis=1)
            sp_rows = jnp.where(top_half, s_prev[h0:h0 + 1, :], s_prev[h1:h1 + 1, :])
            sn_rows = jnp.where(top_half, s_new[h0:h0 + 1, :], s_new[h1:h1 + 1, :])
            cst_new_all[b].append(jnp.concatenate([sp_rows, sp_rows], axis=1) * cst
                                  + jnp.concatenate([sn_rows, sn_rows], axis=1) * kvn)
        m_new_all[b] = m_new
        h_ml = jnp.concatenate(h_heads, axis=1)
        if not final:
            return y_ssd.astype(BF16), h_ml.astype(BF16)

        y_tot = y_ssd + yb_ref[b].astype(F32) + dskip_ref[...] * xsf
        zz = z_ref[b].astype(F32)
        y2 = y_tot * (zz * _sigmoid(zz))
        y_n = y2 * lax.rsqrt(jnp.mean(y2 * y2, axis=-1, keepdims=True) + RMS_EPS) * ngs_ref[...]
        h_tot = h_ml + hb_ref[b].astype(F32)
        segs = []
        for h in range(MLSTM_HEADS):
            seg = h_tot[:, h * MLSTM_DV:(h + 1) * MLSTM_DV]
            segs.append(seg * lax.rsqrt(jnp.mean(seg * seg, axis=-1, keepdims=True) + RMS_EPS))
        h_n = jnp.concatenate(segs, axis=1) * ngm_ref[...]
        y_ml = _sigmoid(o_ref[b].astype(F32)) * h_n
        return y_n.astype(BF16), y_ml.astype(BF16)

    s_old = [[s_ref[b, g] for g in range(SSD_GROUPS)] for b in range(n_seq)]
    cst_old = [[cst_ref[b, p] for p in range(MLSTM_HEADS // 2)] for b in range(n_seq)]
    m_old = [m_ref[b] for b in range(n_seq)]
    s_new_all = [[] for _ in range(n_seq)]
    cst_new_all = [[] for _ in range(n_seq)]
    m_new_all = [None] * n_seq
    results = [one_sequence(b) for b in range(n_seq)]
    for b in range(n_seq):
        for g in range(SSD_GROUPS):
            s_ref[b, g] = s_new_all[b][g]
        for p in range(MLSTM_HEADS // 2):
            cst_ref[b, p] = cst_new_all[b][p]
        m_ref[b] = m_new_all[b]

    @pl.when(c > 0)
    def _():
        for b, (first, second) in enumerate(results):
            if final:
                ycat_ref[b, :, :D_MODEL] = first
                ycat_ref[b, :, D_MODEL:] = second
            else:
                yout_ref[b] = first
                hout_ref[b] = second


def _mixer_pass(xs, bm, cm, gates, q, k, v, gbias, alog, expand, n_chunks, *, reverse, final_inputs=None):
    final = final_inputs is not None
    n_seq = xs.shape[0]
    seq_len = (n_chunks - 1) * CHUNK

    def chunk_of(t):
        return (n_chunks - 1 - t) if reverse else t

    def pad_map(t):
        return (0, _storage_chunk(chunk_of(t), n_chunks), 0)

    def out_map(t):
        return (0, jnp.maximum(chunk_of(t) - 1, 0), 0)

    const = lambda a: pl.BlockSpec(a.shape, lambda t: (0,) * a.ndim)
    pad_spec = lambda n: pl.BlockSpec((n_seq, CHUNK, n), pad_map)
    out_spec = lambda n: pl.BlockSpec((n_seq, CHUNK, n), out_map)
    in_arrays = [xs, bm, cm, gates, q, k, v]
    in_specs = [pad_spec(a.shape[2]) for a in in_arrays]
    if final:
        z, o, yb, hb, dskip, ngs, ngm = final_inputs
        in_arrays += [z, o, yb, hb]
        in_specs += [pad_spec(1024), pad_spec(1024), out_spec(1024), out_spec(1024)]
        in_arrays += [gbias, alog, expand, dskip, ngs, ngm]
        in_specs += [const(a) for a in (gbias, alog, expand, dskip, ngs, ngm)]
        out_shape = [jax.ShapeDtypeStruct((n_seq, seq_len, 2 * D_MODEL), BF16)]
        out_specs = [out_spec(2 * D_MODEL)]
    else:
        in_arrays += [gbias, alog, expand]
        in_specs += [const(a) for a in (gbias, alog, expand)]
        out_shape = [jax.ShapeDtypeStruct((n_seq, seq_len, D_MODEL), BF16),
                     jax.ShapeDtypeStruct((n_seq, seq_len, D_MODEL), BF16)]
        out_specs = [out_spec(D_MODEL), out_spec(D_MODEL)]
    return pl.pallas_call(
        functools.partial(_mixer_kernel, reverse=reverse, final=final, n_chunks=n_chunks, n_seq=n_seq),
        grid=(n_chunks,),
        in_specs=in_specs,
        out_specs=out_specs,
        out_shape=out_shape,
        scratch_shapes=[pltpu.VMEM((n_seq, SSD_GROUPS, SSD_STATE, GROUP_WIDTH), F32),
                        pltpu.VMEM((n_seq, MLSTM_HEADS // 2, 2 * MLSTM_DK, 2 * MLSTM_DV), F32),
                        pltpu.VMEM((n_seq, MLSTM_HEADS, LANES), F32)],
        compiler_params=pltpu.CompilerParams(
            dimension_semantics=("arbitrary",), vmem_limit_bytes=VMEM_LIMIT_BYTES),
        name="mixer_fwd" if final else "mixer_bwd",
    )(*in_arrays)


def _epilogue_kernel(x_ref, ycat_ref, wout_ref, lng0_ref, lnb0_ref, lng1_ref, lnb1_ref,
                     wrh_ref, wrl_ref, br_ref, h1_ref, h1p_ref, sel_ref, gate_ref):
    h0 = _layer_norm(x_ref[...], lng0_ref[...], lnb0_ref[...])
    mix = _dot(ycat_ref[...], wout_ref[...])
    h1 = _layer_norm(DEEPNORM_ALPHA * h0 + mix, lng1_ref[...], lnb1_ref[...])
    h1_ref[...] = h1
    h1p_ref[...] = _pack_bf16_pairs(h1)
    hh = h1.astype(BF16)
    hl = (h1 - hh.astype(F32)).astype(BF16)
    wrh = wrh_ref[...]
    logits = _dot(hh, wrh) + _dot(hl, wrh) + _dot(hh, wrl_ref[...]) + br_ref[...]
    lane = lax.broadcasted_iota(jnp.int32, (1, LANES), 1)
    lane_f = lane.astype(F32)
    logits = jnp.where(lane < N_EXPERTS, logits, -jnp.inf)
    work = logits
    sel = jnp.zeros(logits.shape, jnp.bool_)
    top = None
    for _ in range(TOP_K):
        m = jnp.max(work, axis=-1, keepdims=True)
        if top is None:
            top = m
        first = jnp.min(jnp.where(work == m, lane_f, float(LANES)), axis=-1, keepdims=True)
        pick = lane_f == first
        sel = jnp.logical_or(sel, pick)
        work = jnp.where(pick, -jnp.inf, work)
    e = jnp.where(sel, jnp.exp(logits - top), 0.0)
    gate_ref[...] = e / jnp.sum(e, axis=-1, keepdims=True)
    sel_ref[...] = sel.astype(F32)


def _epilogue(x, ycat, ycat_row0, w_out, lng0, lnb0, lng1, lnb1, wrh, wrl, br):
    rows = x.shape[0]
    tm = EPILOGUE_ROWS
    assert rows % tm == 0 and ycat_row0 % tm == 0
    tile0 = ycat_row0 // tm
    row_spec = lambda n: pl.BlockSpec((tm, n), lambda i: (i, 0))
    const = lambda a: pl.BlockSpec(a.shape, lambda i: (0,) * a.ndim)
    out_shape = [jax.ShapeDtypeStruct((rows, D_MODEL), F32),
                 jax.ShapeDtypeStruct((rows, D_MODEL // 2), jnp.uint32),
                 jax.ShapeDtypeStruct((rows, LANES), F32),
                 jax.ShapeDtypeStruct((rows, LANES), F32)]
    consts = (w_out, lng0, lnb0, lng1, lnb1, wrh, wrl, br)
    return pl.pallas_call(
        _epilogue_kernel,
        grid=(rows // tm,),
        in_specs=[row_spec(D_MODEL), pl.BlockSpec((tm, 2 * D_MODEL), lambda i: (tile0 + i, 0))]
                 + [const(a) for a in consts],
        out_specs=[row_spec(D_MODEL), row_spec(D_MODEL // 2), row_spec(LANES), row_spec(LANES)],
        out_shape=out_shape,
        compiler_params=pltpu.CompilerParams(
            dimension_semantics=("arbitrary",), vmem_limit_bytes=VMEM_LIMIT_BYTES),
        name="epilogue",
    )(x, ycat, *consts)


def _rank_kernel(sel_ref, gate_ref, lstrict_ref, ucum_ref, dest_ref, gk_ref, stats_ref,
                 base_ref, *, trash_row):
    phase = pl.program_id(0)
    i = pl.program_id(1)
    sel = sel_ref[...]
    colsum = jnp.sum(sel, axis=0, keepdims=True)

    @pl.when(jnp.logical_and(phase == 0, i == 0))
    def _():
        base_ref[...] = jnp.zeros_like(base_ref)

    @pl.when(phase == 0)
    def _():
        base_ref[0:1, :] = base_ref[0:1, :] + colsum

    @pl.when(jnp.logical_and(phase == 1, i == 0))
    def _():
        counts = base_ref[0:1, :]
        padded = jnp.ceil(counts / MOE_BLOCK) * MOE_BLOCK
        pend = _dot_exact_rhs(jnp.broadcast_to(padded, (8, LANES)), ucum_ref[...])[0:1, :]
        stats_ref[0:1, :] = counts
        stats_ref[1:2, :] = pend - padded
        stats_ref[2:3, :] = pend
        stats_ref[3:8, :] = jnp.zeros((5, LANES), F32)
        base_ref[1:2, :] = pend - padded

    @pl.when(phase == 1)
    def _():
        before = _dot(lstrict_ref[...], sel.astype(BF16))
        pos = base_ref[1:2, :] + before
        base_ref[1:2, :] = base_ref[1:2, :] + colsum
        work = jnp.where(sel > 0.0, pos + 1.0, 0.0)
        gates = gate_ref[...]
        for kk in range(TOP_K):
            m = jnp.max(work, axis=-1, keepdims=True)
            pick = jnp.logical_and(work == m, m > 0.0)
            gk_ref[:, kk:kk + 1] = jnp.sum(jnp.where(pick, gates, 0.0), axis=-1, keepdims=True)
            dest_ref[:, kk:kk + 1] = jnp.where(m > 0.0, m - 1.0, float(trash_row)).astype(jnp.int32)
            work = jnp.where(pick, 0.0, work)


def _rank(sel, gates, lstrict, ucum, trash_row):
    rows = sel.shape[0]
    tm = RANK_ROWS
    assert rows % tm == 0
    row_spec = lambda n: pl.BlockSpec((tm, n), lambda p, i: (i, 0))
    out_row_spec = lambda n: pl.BlockSpec((tm, n), lambda p, i: (i * p, 0))
    const = lambda a: pl.BlockSpec(a.shape, lambda p, i: (0,) * a.ndim)
    return pl.pallas_call(
        functools.partial(_rank_kernel, trash_row=trash_row),
        grid=(2, rows // tm),
        in_specs=[row_spec(LANES), row_spec(LANES), const(lstrict), const(ucum)],
        out_specs=[out_row_spec(TOP_K), out_row_spec(TOP_K),
                   pl.BlockSpec((8, LANES), lambda p, i: (0, 0))],
        out_shape=[jax.ShapeDtypeStruct((rows, TOP_K), jnp.int32),
                   jax.ShapeDtypeStruct((rows, TOP_K), F32),
                   jax.ShapeDtypeStruct((8, LANES), F32)],
        scratch_shapes=[pltpu.VMEM((8, LANES), F32)],
        compiler_params=pltpu.CompilerParams(
            dimension_semantics=("arbitrary", "arbitrary"), vmem_limit_bytes=VMEM_LIMIT_BYTES),
        name="rank",
    )(sel, gates, lstrict, ucum)


def _sc_mesh():
    return plsc.VectorSubcoreMesh(core_axis_name="c", subcore_axis_name="s",
                                  num_cores=SC_CORES, num_subcores=SC_SUBCORES)


def _sc_scatter_rows(src, idx, n_out_rows):
    n_src, d = src.shape
    w, k = SC_WINDOW, SC_SCATTER_INFLIGHT
    assert idx.shape == (n_src // w, TOP_K, w) and n_src % (w * k * SC_WORKERS) == 0
    per_worker = n_src // (w * SC_WORKERS)
    idx2d = idx.reshape(n_src // w * TOP_K, w)

    @functools.partial(
        pl.kernel, mesh=_sc_mesh(),
        out_type=jax.ShapeDtypeStruct((n_out_rows, d), src.dtype),
        scratch_types=[pltpu.VMEM((k * TOP_K, w), jnp.int32), pltpu.VMEM((k, w, d), src.dtype),
                       pltpu.SemaphoreType.DMA((k,)), pltpu.SemaphoreType.DMA((k,))],
        name="sc_scatter_rows")
    def body(src_hbm, idx_hbm, out_hbm, idx_v, rows_v, sem_load, sem_scatter):
        wid = lax.axis_index("s") * SC_CORES + lax.axis_index("c")

        @pl.loop(0, per_worker // k)
        def _(i):
            win0 = wid * per_worker + i * k
            loads = [pltpu.async_copy(src_hbm.at[pl.ds(pl.multiple_of((win0 + b) * w, w), w)], rows_v.at[b],
                                      sem_load.at[b]) for b in range(k)]
            pltpu.sync_copy(idx_hbm.at[pl.ds(pl.multiple_of(win0 * TOP_K, TOP_K), k * TOP_K)], idx_v)
            scatters = []
            for b in range(k):
                loads[b].wait()
                scatters += [pltpu.async_copy(rows_v.at[b], out_hbm.at[idx_v.at[b * TOP_K + kk]],
                                              sem_scatter.at[b]) for kk in range(TOP_K)]
            for copy in scatters:
                copy.wait()

    return body(src, idx2d)


def _sc_gather_rows(table, idx):
    d = table.shape[1]
    w, k = SC_WINDOW, SC_GATHER_INFLIGHT
    assert idx.shape[0] % (w * k * SC_WORKERS) == 0
    per_worker = idx.shape[0] // (w * SC_WORKERS)
    idx2d = idx.reshape(-1, w)

    @functools.partial(
        pl.kernel, mesh=_sc_mesh(),
        out_type=jax.ShapeDtypeStruct((idx.shape[0], d), table.dtype),
        scratch_types=[pltpu.VMEM((k, w), jnp.int32), pltpu.VMEM((k, w, d), table.dtype),
                       pltpu.SemaphoreType.DMA((k,)), pltpu.SemaphoreType.DMA((k,))],
        name="sc_gather_rows")
    def body(table_hbm, idx_hbm, out_hbm, idx_v, rows_v, sem_gather, sem_store):
        wid = lax.axis_index("s") * SC_CORES + lax.axis_index("c")

        @pl.loop(0, per_worker // k)
        def _(i):
            win0 = wid * per_worker + i * k
            pltpu.sync_copy(idx_hbm.at[pl.ds(pl.multiple_of(win0, k), k)], idx_v)
            gathers = [pltpu.async_copy(table_hbm.at[idx_v.at[b]], rows_v.at[b], sem_gather.at[b])
                       for b in range(k)]
            stores = []
            for b in range(k):
                gathers[b].wait()
                stores.append(pltpu.async_copy(
                    rows_v.at[b], out_hbm.at[pl.ds(pl.multiple_of((win0 + b) * w, w), w)], sem_store.at[b]))
            for copy in stores:
                copy.wait()

    return body(table, idx2d)


def _ffn_kernel(blk_ref, exp_ref, valid_ref, first_ref, next_ref, slot_ref, nused_ref,
                xb_ref, w1_hbm, b1_ref, w2_hbm, b2_ref, yb_ref, w1f_ref, w2f_ref, w1b_ref, w2b_ref, sems):
    j = pl.program_id(0)
    active = j < nused_ref[0]

    def weight_copies(e, slot):
        return (pltpu.make_async_copy(w1_hbm.at[e], w1f_ref.at[slot], sems.at[0, slot]),
                pltpu.make_async_copy(w2_hbm.at[e], w2f_ref.at[slot], sems.at[1, slot]))

    @pl.when(jnp.logical_and(active, first_ref[j] == 1))
    def _():
        e = exp_ref[j]
        slot = slot_ref[j]
        nxt = next_ref[j]

        @pl.when(j == 0)
        def _():
            for copy in weight_copies(e, slot):
                copy.start()

        @pl.when(nxt >= 0)
        def _():
            for copy in weight_copies(nxt, 1 - slot):
                copy.start()

        for copy in weight_copies(e, slot):
            copy.wait()
        w1b_ref[...] = w1f_ref[slot].astype(BF16)
        w2b_ref[...] = w2f_ref[slot].astype(BF16)

    @pl.when(active)
    def _():
        row = lax.broadcasted_iota(jnp.int32, (MOE_BLOCK, 1), 0)
        x = jnp.where(row < valid_ref[j], _unpack_bf16_pairs(xb_ref[...]), 0.0).astype(BF16)
        hc = _dot(x, w1b_ref[...]) + b1_ref[0]
        gate = jnp.minimum(hc[:, :D_FF], SWIGLU_LIMIT)
        up = jnp.clip(hc[:, D_FF:], -SWIGLU_LIMIT, SWIGLU_LIMIT)
        act = (up + 1.0) * gate * _sigmoid(SWIGLU_ALPHA * gate)
        yb_ref[...] = _pack_bf16_pairs(_dot(act.astype(BF16), w2b_ref[...]) + b2_ref[0])


def _ffn(blk_idx, blk_exp, blk_valid, blk_first, blk_next, blk_slot, n_used, xb, w1, b1, w2, b2, n_blocks):
    bm = MOE_BLOCK
    grid_spec = pltpu.PrefetchScalarGridSpec(
        num_scalar_prefetch=7,
        grid=(n_blocks,),
        in_specs=[pl.BlockSpec((bm, D_MODEL // 2), lambda j, bi, be, *_: (bi[j], 0)),
                  pl.BlockSpec(memory_space=pl.ANY),
                  pl.BlockSpec((1, 1, 2 * D_FF), lambda j, bi, be, *_: (be[j], 0, 0)),
                  pl.BlockSpec(memory_space=pl.ANY),
                  pl.BlockSpec((1, 1, D_MODEL), lambda j, bi, be, *_: (be[j], 0, 0))],
        out_specs=pl.BlockSpec((bm, D_MODEL // 2), lambda j, bi, be, *_: (bi[j], 0)),
        scratch_shapes=[pltpu.VMEM((2, D_MODEL, 2 * D_FF), F32), pltpu.VMEM((2, D_FF, D_MODEL), F32),
                        pltpu.VMEM((D_MODEL, 2 * D_FF), BF16), pltpu.VMEM((D_FF, D_MODEL), BF16),
                        pltpu.SemaphoreType.DMA((2, 2))],
    )
    return pl.pallas_call(
        _ffn_kernel,
        grid_spec=grid_spec,
        out_shape=jax.ShapeDtypeStruct(xb.shape, jnp.uint32),
        compiler_params=pltpu.CompilerParams(
            dimension_semantics=("arbitrary",), vmem_limit_bytes=VMEM_LIMIT_BYTES),
        name="expert_ffn",
    )(blk_idx, blk_exp, blk_valid, blk_first, blk_next, blk_slot, n_used, xb, w1, b1, w2, b2)


def _combine_kernel(gk_ref, h1_ref, y0_ref, y1_ref, y2_ref, y3_ref, lng_ref, lnb_ref, out_ref):
    gk = gk_ref[...]
    ffn = gk[:, 0:1] * _unpack_bf16_pairs(y0_ref[...])
    for kk, y_ref in enumerate((y1_ref, y2_ref, y3_ref), start=1):
        ffn = ffn + gk[:, kk:kk + 1] * _unpack_bf16_pairs(y_ref[...])
    out_ref[...] = _layer_norm(DEEPNORM_ALPHA * h1_ref[...] + ffn, lng_ref[...], lnb_ref[...])


def _combine(gk, h1, ysel, lng, lnb):
    rows = h1.shape[0]
    tm = COMBINE_ROWS
    assert rows % tm == 0
    n_tiles = rows // tm
    const = lambda a: pl.BlockSpec(a.shape, lambda i: (0,) * a.ndim)
    ksel = lambda kk: pl.BlockSpec((tm, D_MODEL // 2), lambda i: (kk * n_tiles + i, 0))
    return pl.pallas_call(
        _combine_kernel,
        grid=(n_tiles,),
        in_specs=[pl.BlockSpec((tm, TOP_K), lambda i: (i, 0)),
                  pl.BlockSpec((tm, D_MODEL), lambda i: (i, 0)),
                  ksel(0), ksel(1), ksel(2), ksel(3),
                  const(lng), const(lnb)],
        out_specs=pl.BlockSpec((tm, D_MODEL), lambda i: (i, 0)),
        out_shape=jax.ShapeDtypeStruct((rows, D_MODEL), F32),
        compiler_params=pltpu.CompilerParams(
            dimension_semantics=("arbitrary",), vmem_limit_bytes=VMEM_LIMIT_BYTES),
        name="combine",
    )(gk, h1, ysel, ysel, ysel, ysel, lng, lnb)


def _row(v, width=None):
    v = v.reshape(1, -1).astype(F32)
    if width is not None and v.shape[1] < width:
        v = jnp.pad(v, ((0, 0), (0, width - v.shape[1])))
    return v


def _encode_all(x_a, x_b, meta_tokens, ln_emb_g, ln_emb_b, w_in, conv_w, conv_b, dt_bias, a_log,
                d_skip, ssd_norm_g, i_bias, f_bias, mlstm_norm_g, w_out, ln1_g, ln1_b, w_router, b_router,
                w1, b1, w2, b2, ln2_g, ln2_b):
    n_a, seq_len, _ = x_a.shape
    n_seq = n_a + x_b.shape[0]
    rows_a = n_a * seq_len
    assert seq_len % CHUNK == 0
    n_chunks = seq_len // CHUNK + 1
    n_tok = n_seq * seq_len

    sizes = (1024, CONV_CH, 2 * SSD_HEADS, 512, 512, 1024, 1024, 2 * MLSTM_HEADS, 2 * MLSTM_HEADS)
    offs = [0]
    for s in sizes:
        offs.append(offs[-1] + s)
    w_z, w_xbc, w_dt, w_q, w_k, w_v, w_o, w_i, w_f = [w_in[:, offs[j]:offs[j + 1]] for j in range(9)]
    w_big = jnp.concatenate([w_z, w_xbc, w_q, w_k * (MLSTM_DK ** -0.5), w_v, w_o], axis=1).astype(BF16)
    zpad = jnp.zeros((D_MODEL, LANES - GATE_END), F32)
    gate_cols = []
    for d in range(2):
        gate_cols += [w_dt[:, d * SSD_HEADS:(d + 1) * SSD_HEADS],
                      w_i[:, d * MLSTM_HEADS:(d + 1) * MLSTM_HEADS],
                      w_f[:, d * MLSTM_HEADS:(d + 1) * MLSTM_HEADS], zpad]
    w_gates = jnp.concatenate(gate_cols, axis=1).astype(BF16)
    gbias = [_row(jnp.concatenate([dt_bias[d], i_bias[d], f_bias[d]]), LANES) for d in range(2)]
    alog = [jnp.pad(jnp.broadcast_to(a_log[d].astype(F32)[:, None], (SSD_HEADS, LANES)),
                    ((0, GATE_END - SSD_HEADS), (0, 0))) for d in range(2)]
    head_of_col = jnp.arange(D_MODEL, dtype=jnp.int32) // SSD_HEAD_DIM
    lane_id = jnp.arange(LANES, dtype=jnp.int32)[:, None]
    expand = jnp.concatenate([lane_id == GATE_END + head_of_col[None, :],
                              lane_id == 2 * GATE_END + head_of_col[None, :]], axis=1).astype(BF16)
    dskip = _row(jnp.repeat(d_skip, SSD_HEAD_DIM))
    conv_w8 = jnp.pad(conv_w.astype(F32), ((0, 8 - CONV_W), (0, 0)))
    meta_tile = jnp.pad(meta_tokens.astype(F32), ((PAD_FRONT, INPROJ_ROWS - CHUNK), (0, 0)))

    z, xbc, q, k, v, o, gates_f, gates_b = _inproj(x_a, x_b, meta_tile, _row(ln_emb_g), _row(ln_emb_b),
                                                   w_big, w_gates)
    xs, bm, cm = _conv(xbc, conv_w8, _row(conv_b), n_chunks)
    yb, hb = _mixer_pass(xs, bm, cm, gates_b, q, k, v, gbias[1], alog[1], expand, n_chunks, reverse=True)
    (ycat,) = _mixer_pass(xs, bm, cm, gates_f, q, k, v, gbias[0], alog[0], expand, n_chunks,
                          reverse=False,
                          final_inputs=(z, o, yb, hb, dskip, _row(ssd_norm_g), _row(mlstm_norm_g)))

    wr = jnp.pad(w_router.astype(F32), ((0, 0), (0, LANES - N_EXPERTS)))
    wrh = wr.astype(BF16)
    wrl = (wr - wrh.astype(F32)).astype(BF16)
    w_out_b = w_out.astype(BF16)
    r_i = jnp.arange(RANK_ROWS, dtype=jnp.int32)
    lstrict = (r_i[None, :] < r_i[:, None]).astype(BF16)
    l_i = jnp.arange(LANES, dtype=jnp.int32)
    ucum = (l_i[:, None] <= l_i[None, :]).astype(BF16)
    ycat2d = ycat.reshape(n_tok, 2 * D_MODEL)
    outs = []
    for x_part, row0 in ((x_a, 0), (x_b, rows_a)):
        n_part = x_part.shape[0] * seq_len
        h1, h1p, sel, gates = _epilogue(x_part.reshape(n_part, D_MODEL), ycat2d, row0, w_out_b,
                                        _row(ln_emb_g), _row(ln_emb_b), _row(ln1_g), _row(ln1_b),
                                        wrh, wrl, _row(b_router, LANES))
        n_blocks = n_part * TOP_K // MOE_BLOCK + N_EXPERTS
        trash_row = n_blocks * MOE_BLOCK
        dest, gk, stats = _rank(sel, gates, lstrict, ucum, trash_row)
        counts = stats[0, :N_EXPERTS].astype(jnp.int32)
        starts = stats[1, :N_EXPERTS].astype(jnp.int32)
        pends = stats[2, :N_EXPERTS].astype(jnp.int32)
        n_used = pends[N_EXPERTS - 1] // MOE_BLOCK
        blk = jnp.minimum(jnp.arange(n_blocks, dtype=jnp.int32), jnp.maximum(n_used - 1, 0))
        blk_exp = jnp.minimum(
            jnp.sum((pends[None, :] <= (blk * MOE_BLOCK)[:, None]).astype(jnp.int32), axis=1),
            N_EXPERTS - 1).astype(jnp.int32)
        blk_last = jnp.take(starts + counts, blk_exp)
        blk_valid = jnp.clip(blk_last - blk * MOE_BLOCK, 0, MOE_BLOCK).astype(jnp.int32)
        dest_km = dest.T.reshape(-1)
        dest_wm = dest.reshape(n_part // SC_WINDOW, SC_WINDOW, TOP_K).transpose(0, 2, 1)
        xb = _sc_scatter_rows(h1p, dest_wm, trash_row + MOE_BLOCK)
        j_i = jnp.arange(n_blocks, dtype=jnp.int32)
        blk_first = jnp.logical_and(j_i < n_used, jnp.logical_or(j_i == 0, blk_exp != jnp.roll(blk_exp, 1)))
        blk_slot = ((jnp.cumsum(blk_first.astype(jnp.int32)) - 1) % 2).astype(jnp.int32)
        run_end = jnp.take(pends, blk_exp) // MOE_BLOCK
        blk_next = jnp.where(run_end < n_used, jnp.take(blk_exp, jnp.minimum(run_end, n_blocks - 1)),
                             -1).astype(jnp.int32)
        yexp = _ffn(blk, blk_exp, blk_valid, blk_first.astype(jnp.int32), blk_next, blk_slot,
                    n_used.reshape(1), xb, w1, b1.reshape(N_EXPERTS, 1, -1), w2,
                    b2.reshape(N_EXPERTS, 1, -1), n_blocks)
        ysel = _sc_gather_rows(yexp, dest_km)
        outs.append(_combine(gk, h1, ysel, _row(ln2_g), _row(ln2_b)))
    return outs


def kernel(x_prompt, x_sample, meta_tokens, ln_emb_g, ln_emb_b, w_in, conv_w, conv_b, dt_bias, a_log,
           d_skip, ssd_norm_g, i_bias, f_bias, mlstm_norm_g, w_out, ln1_g, ln1_b, w_router, b_router,
           w1, b1, w2, b2, ln2_g, ln2_b):
    assert x_prompt.shape[1:] == x_sample.shape[1:]
    n_p, seq_len, d = x_prompt.shape
    n_s = x_sample.shape[0]
    y_p, y_s = _encode_all(x_prompt.astype(F32), x_sample.astype(F32), meta_tokens, ln_emb_g, ln_emb_b, w_in[0], conv_w[0],
                           conv_b[0], dt_bias[0], a_log[0], d_skip[0], ssd_norm_g[0], i_bias[0], f_bias[0],
                           mlstm_norm_g[0], w_out[0], ln1_g[0], ln1_b[0], w_router[0], b_router[0],
                           w1[0], b1[0], w2[0], b2[0], ln2_g[0], ln2_b[0])
    return (y_p.reshape(n_p, seq_len, d), y_s.reshape(n_s, seq_len, d))
```

```python
import functools

import jax
import jax.numpy as jnp
from jax import lax
from jax.experimental import pallas as pl
from jax.experimental.pallas import tpu as pltpu
from jax.experimental.pallas import tpu_sc as plsc

F32 = jnp.float32
BF16 = jnp.bfloat16

D_MODEL = 1024
N_META = 16
CHUNK = 128
PAD_FRONT = CHUNK - N_META
SSD_HEADS = 16
SSD_HEAD_DIM = 64
SSD_GROUPS = 4
SSD_STATE = 128
HEADS_PER_GROUP = SSD_HEADS // SSD_GROUPS
GROUP_WIDTH = HEADS_PER_GROUP * SSD_HEAD_DIM
CONV_W = 5
CONV_HALF = CONV_W // 2
CONV_CH = D_MODEL + 2 * SSD_GROUPS * SSD_STATE
MLSTM_HEADS = 8
MLSTM_DK = 64
MLSTM_DV = 128
N_EXPERTS = 32
TOP_K = 4
D_FF = D_MODEL
SWIGLU_LIMIT = 7.0
SWIGLU_ALPHA = 1.702
DEEPNORM_ALPHA = 2.0 ** 0.25
LN_EPS = 1e-5
RMS_EPS = 1e-5
NEG_GATE = -1e30
LOG2E = 1.4426950408889634

LANES = 128
BF16_SUBLANES = 16
VMEM_LIMIT_BYTES = 56 * 1024 * 1024

GATE_DT0, GATE_I0, GATE_F0, GATE_END = 0, SSD_HEADS, SSD_HEADS + MLSTM_HEADS, SSD_HEADS + 2 * MLSTM_HEADS

INPROJ_ROWS = 512
EPILOGUE_ROWS = 512
RANK_ROWS = 512
COMBINE_ROWS = 512
MOE_BLOCK = 512

SC_CORES = 2
SC_SUBCORES = 16
SC_WORKERS = SC_CORES * SC_SUBCORES
SC_WINDOW = 32
SC_SCATTER_INFLIGHT = 2
SC_GATHER_INFLIGHT = 4


def _dot(a, b):
    return jnp.dot(a, b, preferred_element_type=F32)


def _dot_nt(a, b):
    return lax.dot_general(a, b, (((1,), (1,)), ((), ())), preferred_element_type=F32)


def _split3(x):
    hi = x.astype(BF16)
    r1 = x - hi.astype(F32)
    mid = r1.astype(BF16)
    lo = (r1 - mid.astype(F32)).astype(BF16)
    return hi, mid, lo


def _dot_exact_lhs(a_bf16, x):
    hi, mid, lo = _split3(x)
    return _dot(a_bf16, hi) + _dot(a_bf16, mid) + _dot(a_bf16, lo)


def _dot_exact_rhs(x, b_bf16):
    hi, mid, lo = _split3(x)
    return _dot(hi, b_bf16) + _dot(mid, b_bf16) + _dot(lo, b_bf16)


def _layer_norm(x, g, b):
    mu = jnp.mean(x, axis=-1, keepdims=True)
    xc = x - mu
    var = jnp.mean(xc * xc, axis=-1, keepdims=True)
    return xc * lax.rsqrt(var + LN_EPS) * g + b


def _sigmoid(x):
    return 1.0 / (1.0 + jnp.exp(-x))


def _log1p_exp_neg_abs(x):
    return jnp.log(1.0 + jnp.exp(-jnp.abs(x)))


def _pack_bf16_pairs(x):
    n = x.shape[1] // 2
    bits = lax.bitcast_convert_type(x.astype(BF16).astype(F32), jnp.uint32)
    return (bits[:, :n] >> 16) | bits[:, n:]


def _unpack_bf16_pairs(words):
    lo = lax.bitcast_convert_type(words << 16, F32)
    hi = lax.bitcast_convert_type(words & jnp.uint32(0xFFFF0000), F32)
    return jnp.concatenate([lo, hi], axis=1)


def _storage_chunk(c, n_chunks):
    return jnp.where(c == 0, n_chunks - 1, c - 1)


def _inproj_kernel(xa_ref, xb_ref, meta_ref, g_ref, b_ref, wbig_ref, wg_ref,
                   z_ref, xbc_ref, q_ref, k_ref, v_ref, o_ref, gf_ref, gb_ref, *, n_x_tiles, n_a):
    x = jnp.where(pl.program_id(0) < n_a, xa_ref[...], xb_ref[...])
    x = jnp.where(pl.program_id(1) == n_x_tiles, meta_ref[...], x)
    h = _layer_norm(x, g_ref[...], b_ref[...]).astype(BF16)

    def mm(c0, c1):
        return _dot(h, wbig_ref[:, c0:c1]).astype(BF16)

    z_ref[...] = mm(0, 1024)
    xbc_ref[:, 0:1024] = mm(1024, 2048)
    xbc_ref[:, 1024:2048] = mm(2048, 3072)
    q_ref[...] = mm(3072, 3584)
    k_ref[...] = mm(3584, 4096)
    v_ref[...] = mm(4096, 5120)
    o_ref[...] = mm(5120, 6144)
    gates = _dot(h, wg_ref[...])
    gf_ref[...] = gates[:, :LANES]
    gb_ref[...] = gates[:, LANES:]


def _inproj(x_a, x_b, meta_tile, ln_g, ln_b, w_big, w_gates):
    n_a, seq_len, _ = x_a.shape
    n_seq = n_a + x_b.shape[0]
    tm = INPROJ_ROWS
    assert seq_len % tm == 0 and tm >= CHUNK
    n_x_tiles = seq_len // tm
    rows = seq_len + CHUNK
    row_spec = lambda n: pl.BlockSpec((None, tm, n), lambda b, i: (b, i, 0))
    const = lambda a: pl.BlockSpec(a.shape, lambda b, i: (0,) * a.ndim)
    resident = lambda a: pl.BlockSpec(a.shape, lambda b, i: (0,) * a.ndim, pipeline_mode=pl.Buffered(1))
    widths = (1024, CONV_CH, 512, 512, 1024, 1024)
    out_shapes = [jax.ShapeDtypeStruct((n_seq, rows, w), BF16) for w in widths]
    out_shapes += [jax.ShapeDtypeStruct((n_seq, rows, LANES), F32)] * 2
    return pl.pallas_call(
        functools.partial(_inproj_kernel, n_x_tiles=n_x_tiles, n_a=n_a),
        grid=(n_seq, n_x_tiles + 1),
        in_specs=[pl.BlockSpec((None, tm, D_MODEL), lambda b, i: (
                      jnp.minimum(b, n_a - 1),
                      jnp.where(b < n_a, jnp.minimum(i, n_x_tiles - 1), n_x_tiles - 1), 0)),
                  pl.BlockSpec((None, tm, D_MODEL), lambda b, i: (
                      jnp.maximum(b - n_a, 0),
                      jnp.where(b < n_a, 0, jnp.minimum(i, n_x_tiles - 1)), 0)),
                  resident(meta_tile), const(ln_g), const(ln_b), resident(w_big), resident(w_gates)],
        out_specs=[row_spec(s.shape[2]) for s in out_shapes],
        out_shape=out_shapes,
        compiler_params=pltpu.CompilerParams(
            dimension_semantics=("arbitrary", "arbitrary"), vmem_limit_bytes=VMEM_LIMIT_BYTES),
        name="inproj",
    )(x_a, x_b, meta_tile, ln_g, ln_b, w_big, w_gates)


def _conv_kernel(prev_ref, main_ref, next_ref, shift_ref, w_ref, b_ref, xs_ref, bm_ref, cm_ref,
                 *, n_x_tiles, chunks_per_tile):
    i = pl.program_id(1)
    w = w_ref[...]
    bias = b_ref[...]
    shift = shift_ref[...]

    def conv_chunk(before, rows, after, pad_rows, j):
        shifted = _dot(shift, jnp.concatenate([before, rows, after], axis=0).astype(BF16))
        acc = bias + rows * w[CONV_HALF:CONV_HALF + 1, :]
        for jj, t in enumerate(t for t in range(CONV_W) if t != CONV_HALF):
            acc = acc + shifted[jj * CHUNK:(jj + 1) * CHUNK, :] * w[t:t + 1, :]
        y = acc * _sigmoid(acc)
        if pad_rows is not None:
            y = jnp.where(pad_rows, 0.0, y)
        r = slice(j * CHUNK, (j + 1) * CHUNK)
        xs_ref[r, :] = y[:, :D_MODEL].astype(BF16)
        bm_ref[r, :] = y[:, D_MODEL:D_MODEL + 512].astype(BF16)
        cm_ref[r, :] = y[:, D_MODEL + 512:].astype(BF16)

    @pl.when(i < n_x_tiles)
    def _():
        tile = main_ref[...].astype(F32)
        for j in range(chunks_per_tile):
            lo, hi = j * CHUNK, (j + 1) * CHUNK
            before = prev_ref[...].astype(F32) if j == 0 else tile[lo - BF16_SUBLANES:lo, :]
            if j == chunks_per_tile - 1:
                after = jnp.where(i == n_x_tiles - 1, 0.0, next_ref[...].astype(F32))
            else:
                after = tile[hi:hi + BF16_SUBLANES, :]
            conv_chunk(before, tile[lo:hi, :], after, None, j)

    @pl.when(i == n_x_tiles)
    def _():
        row = lax.broadcasted_iota(jnp.int32, (CHUNK, 1), 0)
        pad_rows = row < PAD_FRONT
        rows = jnp.where(pad_rows, 0.0, main_ref[0:CHUNK, :].astype(F32))
        conv_chunk(jnp.zeros((BF16_SUBLANES, CONV_CH), F32), rows, next_ref[...].astype(F32), pad_rows, 0)


def _conv(xbc, conv_w8, conv_b, n_chunks):
    n_seq, rows, _ = xbc.shape
    tm = INPROJ_ROWS
    seq_len = (n_chunks - 1) * CHUNK
    assert seq_len % tm == 0 and tm % CHUNK == 0
    n_x_tiles = seq_len // tm
    halo_per_tile = tm // BF16_SUBLANES
    meta_last_halo = rows // BF16_SUBLANES - 1

    def prev_map(b, i):
        before_tile = jnp.maximum(jnp.minimum(i, n_x_tiles - 1) * halo_per_tile - 1, 0)
        return (b, jnp.where(i == 0, meta_last_halo, before_tile), 0)

    def next_map(b, i):
        return (b, jnp.where(i >= n_x_tiles - 1, 0, (i + 1) * halo_per_tile) , 0)

    const = lambda a: pl.BlockSpec(a.shape, lambda b, i: (0,) * a.ndim)
    l_i = jnp.arange(CHUNK, dtype=jnp.int32)[:, None]
    j_i = jnp.arange(CHUNK + 2 * BF16_SUBLANES, dtype=jnp.int32)[None, :]
    shifts = jnp.concatenate([(j_i == BF16_SUBLANES + l_i + t - CONV_HALF)
                              for t in range(CONV_W) if t != CONV_HALF], axis=0).astype(BF16)
    out_shapes = [jax.ShapeDtypeStruct((n_seq, rows, D_MODEL), BF16),
                  jax.ShapeDtypeStruct((n_seq, rows, 512), BF16),
                  jax.ShapeDtypeStruct((n_seq, rows, 512), BF16)]
    tile_spec = lambda n: pl.BlockSpec((None, tm, n), lambda b, i: (b, i, 0))
    return pl.pallas_call(
        functools.partial(_conv_kernel, n_x_tiles=n_x_tiles, chunks_per_tile=tm // CHUNK),
        grid=(n_seq, n_x_tiles + 1),
        in_specs=[pl.BlockSpec((None, BF16_SUBLANES, CONV_CH), prev_map),
                  tile_spec(CONV_CH),
                  pl.BlockSpec((None, BF16_SUBLANES, CONV_CH), next_map),
                  const(shifts), const(conv_w8), const(conv_b)],
        out_specs=[tile_spec(s.shape[2]) for s in out_shapes],
        out_shape=out_shapes,
        compiler_params=pltpu.CompilerParams(
            dimension_semantics=("arbitrary", "arbitrary"), vmem_limit_bytes=VMEM_LIMIT_BYTES),
        name="conv",
    )(xbc, xbc, xbc, shifts, conv_w8, conv_b)


def _mixer_kernel(*refs, reverse, final, n_chunks, n_seq):
    if final:
        (xs_ref, bm_ref, cm_ref, g_ref, q_ref, k_ref, v_ref, z_ref, o_ref, yb_ref, hb_ref,
         gbias_ref, alog_ref, expand_ref, dskip_ref, ngs_ref, ngm_ref,
         ycat_ref, s_ref, cst_ref, m_ref) = refs
    else:
        (xs_ref, bm_ref, cm_ref, g_ref, q_ref, k_ref, v_ref,
         gbias_ref, alog_ref, expand_ref,
         yout_ref, hout_ref, s_ref, cst_ref, m_ref) = refs

    t = pl.program_id(0)
    c = (n_chunks - 1 - t) if reverse else t
    end = 0 if reverse else CHUNK - 1

    @pl.when(t == 0)
    def _():
        s_ref[...] = jnp.zeros_like(s_ref)
        cst_ref[...] = jnp.zeros_like(cst_ref)
        m_ref[...] = jnp.zeros_like(m_ref)

    row = lax.broadcasted_iota(jnp.int32, (CHUNK, 1), 0)
    col = lax.broadcasted_iota(jnp.int32, (1, CHUNK), 1)
    lane = col
    allowed = (col >= row) if reverse else (col <= row)
    tri = allowed.astype(BF16)
    tri_t = ((row >= col) if reverse else (row <= col)).astype(BF16)
    feat = lax.broadcasted_iota(jnp.int32, (GATE_END, 1), 0)
    is_dt = feat < GATE_I0
    is_i = jnp.logical_and(feat >= GATE_I0, feat < GATE_F0)
    is_f = feat >= GATE_F0
    pad_cols = jnp.logical_and(c == 0, col < PAD_FRONT)
    a_coef = -jnp.exp(alog_ref[...]) * LOG2E
    expand = expand_ref[...]
    left_half = lane < SSD_HEAD_DIM
    right_half = jnp.logical_not(left_half)
    top_half = row < MLSTM_DK
    ones_blk = jnp.ones((CHUNK, MLSTM_DV), BF16)
    full = (CHUNK, LANES)

    def one_sequence(b):
        gr = (g_ref[b] + gbias_ref[...]).T[0:GATE_END, :]
        lse = _log1p_exp_neg_abs(gr)
        val_t = jnp.where(is_dt, jnp.maximum(gr, 0.0) + lse, jnp.where(is_i, gr, jnp.minimum(gr, 0.0) - lse))
        val_t = jnp.where(pad_cols, jnp.where(is_i, NEG_GATE, 0.0), val_t)
        u_t = jnp.where(is_dt, val_t * a_coef, jnp.where(is_f, val_t * LOG2E, 0.0))
        cums_t = _dot_exact_rhs(u_t, tri_t)
        cums_end = jnp.broadcast_to(cums_t[:, end:end + 1], cums_t.shape)
        p1_t = jnp.exp2(cums_t)
        p2_t = jnp.exp2(cums_end - cums_t) * val_t
        packed = jnp.concatenate([cums_t, p1_t, p2_t, val_t], axis=0).T
        ex = _dot(packed.astype(BF16), expand)
        ex1 = ex[:, :D_MODEL]
        ex2 = ex[:, D_MODEL:]
        chunk_decay = _dot_exact_rhs(jnp.broadcast_to(packed[end:end + 1, :], (8, LANES)),
                                     expand[:, :D_MODEL])[0:1, :]

        xs = xs_ref[b]
        xsf = xs.astype(F32)
        xs_w = (xsf * ex2).astype(BF16)
        src_term = jnp.log(val_t[0:SSD_HEADS, :]) * LOG2E - cums_t[0:SSD_HEADS, :]
        y_groups = []
        for g in range(SSD_GROUPS):
            cg = cm_ref[b, :, g * SSD_STATE:(g + 1) * SSD_STATE]
            bg = bm_ref[b, :, g * SSD_STATE:(g + 1) * SSD_STATE]
            cb = _dot_nt(cg, bg)
            ys = []
            for pp in range(HEADS_PER_GROUP // 2):
                pair = g * (HEADS_PER_GROUP // 2) + pp
                xs_pair = xs[:, pair * LANES:(pair + 1) * LANES]
                zero_pair = jnp.zeros_like(xs_pair)
                m_mats = []
                for h in (2 * pair, 2 * pair + 1):
                    seg = jnp.broadcast_to(packed[:, h:h + 1], full) + src_term[h:h + 1, :]
                    m_mats.append((cb * jnp.exp2(jnp.where(allowed, seg, -jnp.inf))).astype(BF16))
                rhs = jnp.concatenate([jnp.where(left_half, xs_pair, zero_pair),
                                       jnp.where(right_half, xs_pair, zero_pair)], axis=0)
                ys.append(_dot(jnp.concatenate(m_mats, axis=1), rhs))
            y_diag = jnp.concatenate(ys, axis=1)
            gs = slice(g * GROUP_WIDTH, (g + 1) * GROUP_WIDTH)
            s_g = s_old[b][g]
            y_off = _dot(cg, s_g.astype(BF16)) * ex1[:, gs]
            y_groups.append(y_diag + y_off)
            bg_t = bg.astype(F32).T.astype(BF16)
            s_new_all[b].append(chunk_decay[:, gs] * s_g + _dot(bg_t, xs_w[:, gs]))
        y_ssd = jnp.concatenate(y_groups, axis=1)

        bcum_t = cums_t[GATE_F0:GATE_END, :]
        ip_t = val_t[GATE_I0:GATE_F0, :] * LOG2E
        rep = bcum_t.shape
        g_rep = jnp.broadcast_to(bcum_t[:, end:end + 1], rep)
        a_t = g_rep - bcum_t + ip_t
        a_max = jnp.broadcast_to(jnp.max(a_t, axis=1, keepdims=True), rep)
        w_t = jnp.exp2(a_t - a_max)
        m_prev = m_old[b]
        m_new = jnp.maximum(g_rep + m_prev, a_max)
        s_prev = jnp.exp2(g_rep + m_prev - m_new)
        s_new = jnp.exp2(a_max - m_new)
        r_t = ip_t - bcum_t
        h_heads = []
        for pair in range(MLSTM_HEADS // 2):
            h0, h1 = 2 * pair, 2 * pair + 1
            q_pair = q_ref[b, :, pair * LANES:(pair + 1) * LANES]
            k_pair = k_ref[b, :, pair * LANES:(pair + 1) * LANES]
            q_pair_f = q_pair.astype(F32)
            cst = cst_old[b][pair]
            cst_b = cst.astype(BF16)
            v_pair = []
            for hh, h in enumerate((h0, h1)):
                keep = left_half if hh == 0 else right_half
                vh = v_ref[b, :, h * MLSTM_DV:(h + 1) * MLSTM_DV]
                v_pair.append(vh)
                qk = _dot_nt(jnp.where(keep, q_pair, jnp.zeros_like(q_pair)), k_pair)
                bc = jnp.broadcast_to(packed[:, GATE_F0 + h:GATE_F0 + h + 1], full)
                dlog = jnp.where(allowed, bc + r_t[h:h + 1, :], -jnp.inf)
                m_intra = jnp.broadcast_to(jnp.max(dlog, axis=1, keepdims=True), full)
                m_inter = bc + m_prev[h:h + 1, :]
                m_t = jnp.maximum(m_inter, m_intra)
                s_mat = (qk * jnp.exp2(dlog - m_t)).astype(BF16)
                qs = (jnp.where(keep, q_pair_f, 0.0) * jnp.exp2(m_inter - m_t)).astype(BF16)
                tot = _dot(jnp.concatenate([s_mat, qs], axis=1),
                           jnp.concatenate([jnp.concatenate([vh, ones_blk], axis=1), cst_b], axis=0))
                num = tot[:, :MLSTM_DV]
                den = tot[:, MLSTM_DV:]
                h_heads.append(num / jnp.maximum(jnp.abs(den), jnp.exp2(-m_t)))
            w_rows = jnp.where(top_half, w_t[h0:h0 + 1, :], w_t[h1:h1 + 1, :])
            kw = (k_pair.astype(F32).T * w_rows).astype(BF16)
            full_kv = _dot(kw, jnp.concatenate([v_pair[0], v_pair[1], ones_blk], axis=1))
            kvn = jnp.concatenate(
                [jnp.where(top_half, full_kv[:, :MLSTM_DV], full_kv[:, MLSTM_DV:2 * MLSTM_DV]),
                 full_kv[:, 2 * MLSTM_DV:]], axis=1)
            sp_rows = jnp.where(top_half, s_prev[h0:h0 + 1, :], s_prev[h1:h1 + 1, :])
            sn_rows = jnp.where(top_half, s_new[h0:h0 + 1, :], s_new[h1:h1 + 1, :])
            cst_new_all[b].append(jnp.concatenate([sp_rows, sp_rows], axis=1) * cst
                                  + jnp.concatenate([sn_rows, sn_rows], axis=1) * kvn)
        m_new_all[b] = m_new
        h_ml = jnp.concatenate(h_heads, axis=1)
        if not final:
            return y_ssd.astype(BF16), h_ml.astype(BF16)

        y_tot = y_ssd + yb_ref[b].astype(F32) + dskip_ref[...] * xsf
        zz = z_ref[b].astype(F32)
        y2 = y_tot * (zz * _sigmoid(zz))
        y_n = y2 * lax.rsqrt(jnp.mean(y2 * y2, axis=-1, keepdims=True) + RMS_EPS) * ngs_ref[...]
        h_tot = h_ml + hb_ref[b].astype(F32)
        segs = []
        for h in range(MLSTM_HEADS):
            seg = h_tot[:, h * MLSTM_DV:(h + 1) * MLSTM_DV]
            segs.append(seg * lax.rsqrt(jnp.mean(seg * seg, axis=-1, keepdims=True) + RMS_EPS))
        h_n = jnp.concatenate(segs, axis=1) * ngm_ref[...]
        y_ml = _sigmoid(o_ref[b].astype(F32)) * h_n
        return y_n.astype(BF16), y_ml.astype(BF16)

    s_old = [[s_ref[b, g] for g in range(SSD_GROUPS)] for b in range(n_seq)]
    cst_old = [[cst_ref[b, p] for p in range(MLSTM_HEADS // 2)] for b in range(n_seq)]
    m_old = [m_ref[b] for b in range(n_seq)]
    s_new_all = [[] for _ in range(n_seq)]
    cst_new_all = [[] for _ in range(n_seq)]
    m_new_all = [None] * n_seq
    results = [one_sequence(b) for b in range(n_seq)]
    for b in range(n_seq):
        for g in range(SSD_GROUPS):
            s_ref[b, g] = s_new_all[b][g]
        for p in range(MLSTM_HEADS // 2):
            cst_ref[b, p] = cst_new_all[b][p]
        m_ref[b] = m_new_all[b]

    @pl.when(c > 0)
    def _():
        for b, (first, second) in enumerate(results):
            if final:
                ycat_ref[b, :, :D_MODEL] = first
                ycat_ref[b, :, D_MODEL:] = second
            else:
                yout_ref[b] = first
                hout_ref[b] = second


def _mixer_pass(xs, bm, cm, gates, q, k, v, gbias, alog, expand, n_chunks, *, reverse, final_inputs=None):
    final = final_inputs is not None
    n_seq = xs.shape[0]
    seq_len = (n_chunks - 1) * CHUNK

    def chunk_of(t):
        return (n_chunks - 1 - t) if reverse else t

    def pad_map(t):
        return (0, _storage_chunk(chunk_of(t), n_chunks), 0)

    def out_map(t):
        return (0, jnp.maximum(chunk_of(t) - 1, 0), 0)

    const = lambda a: pl.BlockSpec(a.shape, lambda t: (0,) * a.ndim)
    pad_spec = lambda n: pl.BlockSpec((n_seq, CHUNK, n), pad_map)
    out_spec = lambda n: pl.BlockSpec((n_seq, CHUNK, n), out_map)
    in_arrays = [xs, bm, cm, gates, q, k, v]
    in_specs = [pad_spec(a.shape[2]) for a in in_arrays]
    if final:
        z, o, yb, hb, dskip, ngs, ngm = final_inputs
        in_arrays += [z, o, yb, hb]
        in_specs += [pad_spec(1024), pad_spec(1024), out_spec(1024), out_spec(1024)]
        in_arrays += [gbias, alog, expand, dskip, ngs, ngm]
        in_specs += [const(a) for a in (gbias, alog, expand, dskip, ngs, ngm)]
        out_shape = [jax.ShapeDtypeStruct((n_seq, seq_len, 2 * D_MODEL), BF16)]
        out_specs = [out_spec(2 * D_MODEL)]
    else:
        in_arrays += [gbias, alog, expand]
        in_specs += [const(a) for a in (gbias, alog, expand)]
        out_shape = [jax.ShapeDtypeStruct((n_seq, seq_len, D_MODEL), BF16),
                     jax.ShapeDtypeStruct((n_seq, seq_len, D_MODEL), BF16)]
        out_specs = [out_spec(D_MODEL), out_spec(D_MODEL)]
    return pl.pallas_call(
        functools.partial(_mixer_kernel, reverse=reverse, final=final, n_chunks=n_chunks, n_seq=n_seq),
        grid=(n_chunks,),
        in_specs=in_specs,
        out_specs=out_specs,
        out_shape=out_shape,
        scratch_shapes=[pltpu.VMEM((n_seq, SSD_GROUPS, SSD_STATE, GROUP_WIDTH), F32),
                        pltpu.VMEM((n_seq, MLSTM_HEADS // 2, 2 * MLSTM_DK, 2 * MLSTM_DV), F32),
                        pltpu.VMEM((n_seq, MLSTM_HEADS, LANES), F32)],
        compiler_params=pltpu.CompilerParams(
            dimension_semantics=("arbitrary",), vmem_limit_bytes=VMEM_LIMIT_BYTES),
        name="mixer_fwd" if final else "mixer_bwd",
    )(*in_arrays)


def _epilogue_kernel(x_ref, ycat_ref, wout_ref, lng0_ref, lnb0_ref, lng1_ref, lnb1_ref,
                     wrh_ref, wrl_ref, br_ref, h1_ref, h1p_ref, sel_ref, gate_ref, cnt_ref):
    h0 = _layer_norm(x_ref[...], lng0_ref[...], lnb0_ref[...])
    mix = _dot(ycat_ref[...], wout_ref[...])
    h1 = _layer_norm(DEEPNORM_ALPHA * h0 + mix, lng1_ref[...], lnb1_ref[...])
    h1_ref[...] = h1
    h1p_ref[...] = _pack_bf16_pairs(h1)
    hh = h1.astype(BF16)
    hl = (h1 - hh.astype(F32)).astype(BF16)
    wrh = wrh_ref[...]
    logits = _dot(hh, wrh) + _dot(hl, wrh) + _dot(hh, wrl_ref[...]) + br_ref[...]
    lane = lax.broadcasted_iota(jnp.int32, (1, LANES), 1)
    lane_f = lane.astype(F32)
    logits = jnp.where(lane < N_EXPERTS, logits, -jnp.inf)
    work = logits
    sel = jnp.zeros(logits.shape, jnp.bool_)
    top = None
    for _ in range(TOP_K):
        m = jnp.max(work, axis=-1, keepdims=True)
        if top is None:
            top = m
        first = jnp.min(jnp.where(work == m, lane_f, float(LANES)), axis=-1, keepdims=True)
        pick = lane_f == first
        sel = jnp.logical_or(sel, pick)
        work = jnp.where(pick, -jnp.inf, work)
    e = jnp.where(sel, jnp.exp(logits - top), 0.0)
    gate_ref[...] = e / jnp.sum(e, axis=-1, keepdims=True)
    sel_f = sel.astype(F32)
    sel_ref[...] = sel_f

    @pl.when(pl.program_id(0) == 0)
    def _():
        cnt_ref[...] = jnp.zeros_like(cnt_ref)

    cnt_ref[0:1, :] = cnt_ref[0:1, :] + jnp.sum(sel_f, axis=0, keepdims=True)


def _epilogue(x, ycat, ycat_row0, w_out, lng0, lnb0, lng1, lnb1, wrh, wrl, br):
    rows = x.shape[0]
    tm = EPILOGUE_ROWS
    assert rows % tm == 0 and ycat_row0 % tm == 0
    tile0 = ycat_row0 // tm
    row_spec = lambda n: pl.BlockSpec((tm, n), lambda i: (i, 0))
    const = lambda a: pl.BlockSpec(a.shape, lambda i: (0,) * a.ndim)
    out_shape = [jax.ShapeDtypeStruct((rows, D_MODEL), F32),
                 jax.ShapeDtypeStruct((rows, D_MODEL // 2), jnp.uint32),
                 jax.ShapeDtypeStruct((rows, LANES), F32),
                 jax.ShapeDtypeStruct((rows, LANES), F32),
                 jax.ShapeDtypeStruct((8, LANES), F32)]
    consts = (w_out, lng0, lnb0, lng1, lnb1, wrh, wrl, br)
    return pl.pallas_call(
        _epilogue_kernel,
        grid=(rows // tm,),
        in_specs=[row_spec(D_MODEL), pl.BlockSpec((tm, 2 * D_MODEL), lambda i: (tile0 + i, 0))]
                 + [const(a) for a in consts],
        out_specs=[row_spec(D_MODEL), row_spec(D_MODEL // 2), row_spec(LANES), row_spec(LANES),
                   pl.BlockSpec((8, LANES), lambda i: (0, 0))],
        out_shape=out_shape,
        compiler_params=pltpu.CompilerParams(
            dimension_semantics=("arbitrary",), vmem_limit_bytes=VMEM_LIMIT_BYTES),
        name="epilogue",
    )(x, ycat, *consts)


def _rank_kernel(sel_ref, gate_ref, cnt_ref, lstrict_ref, ucum_ref, dest_ref, gk_ref, stats_ref,
                 base_ref, *, trash_row):
    i = pl.program_id(0)
    sel = sel_ref[...]
    colsum = jnp.sum(sel, axis=0, keepdims=True)

    @pl.when(i == 0)
    def _():
        counts = cnt_ref[0:1, :]
        padded = jnp.ceil(counts / MOE_BLOCK) * MOE_BLOCK
        pend = _dot_exact_rhs(jnp.broadcast_to(padded, (8, LANES)), ucum_ref[...])[0:1, :]
        stats_ref[0:1, :] = counts
        stats_ref[1:2, :] = pend - padded
        stats_ref[2:3, :] = pend
        stats_ref[3:8, :] = jnp.zeros((5, LANES), F32)
        base_ref[...] = jnp.broadcast_to(pend - padded, base_ref.shape)

    before = _dot(lstrict_ref[...], sel.astype(BF16))
    pos = base_ref[0:1, :] + before
    base_ref[0:1, :] = base_ref[0:1, :] + colsum
    work = jnp.where(sel > 0.0, pos + 1.0, 0.0)
    gates = gate_ref[...]
    for kk in range(TOP_K):
        m = jnp.max(work, axis=-1, keepdims=True)
        pick = jnp.logical_and(work == m, m > 0.0)
        gk_ref[:, kk:kk + 1] = jnp.sum(jnp.where(pick, gates, 0.0), axis=-1, keepdims=True)
        dest_ref[:, kk:kk + 1] = jnp.where(m > 0.0, m - 1.0, float(trash_row)).astype(jnp.int32)
        work = jnp.where(pick, 0.0, work)


def _rank(sel, gates, counts, lstrict, ucum, trash_row):
    rows = sel.shape[0]
    tm = RANK_ROWS
    assert rows % tm == 0
    row_spec = lambda n: pl.BlockSpec((tm, n), lambda i: (i, 0))
    const = lambda a: pl.BlockSpec(a.shape, lambda i: (0,) * a.ndim)
    return pl.pallas_call(
        functools.partial(_rank_kernel, trash_row=trash_row),
        grid=(rows // tm,),
        in_specs=[row_spec(LANES), row_spec(LANES), const(counts), const(lstrict), const(ucum)],
        out_specs=[row_spec(TOP_K), row_spec(TOP_K), pl.BlockSpec((8, LANES), lambda i: (0, 0))],
        out_shape=[jax.ShapeDtypeStruct((rows, TOP_K), jnp.int32),
                   jax.ShapeDtypeStruct((rows, TOP_K), F32),
                   jax.ShapeDtypeStruct((8, LANES), F32)],
        scratch_shapes=[pltpu.VMEM((8, LANES), F32)],
        compiler_params=pltpu.CompilerParams(
            dimension_semantics=("arbitrary",), vmem_limit_bytes=VMEM_LIMIT_BYTES),
        name="rank",
    )(sel, gates, counts, lstrict, ucum)


def _sc_mesh():
    return plsc.VectorSubcoreMesh(core_axis_name="c", subcore_axis_name="s",
                                  num_cores=SC_CORES, num_subcores=SC_SUBCORES)


def _sc_scatter_rows(src, idx, n_out_rows):
    n_src, d = src.shape
    w, k = SC_WINDOW, SC_SCATTER_INFLIGHT
    assert idx.shape == (n_src // w, TOP_K, w) and n_src % (w * k * SC_WORKERS) == 0
    per_worker = n_src // (w * SC_WORKERS)
    idx2d = idx.reshape(n_src // w * TOP_K, w)

    @functools.partial(
        pl.kernel, mesh=_sc_mesh(),
        out_type=jax.ShapeDtypeStruct((n_out_rows, d), src.dtype),
        scratch_types=[pltpu.VMEM((k * TOP_K, w), jnp.int32), pltpu.VMEM((k, w, d), src.dtype),
                       pltpu.SemaphoreType.DMA((k,)), pltpu.SemaphoreType.DMA((k,))],
        name="sc_scatter_rows")
    def body(src_hbm, idx_hbm, out_hbm, idx_v, rows_v, sem_load, sem_scatter):
        wid = lax.axis_index("s") * SC_CORES + lax.axis_index("c")

        @pl.loop(0, per_worker // k)
        def _(i):
            win0 = wid * per_worker + i * k
            loads = [pltpu.async_copy(src_hbm.at[pl.ds(pl.multiple_of((win0 + b) * w, w), w)], rows_v.at[b],
                                      sem_load.at[b]) for b in range(k)]
            pltpu.sync_copy(idx_hbm.at[pl.ds(pl.multiple_of(win0 * TOP_K, TOP_K), k * TOP_K)], idx_v)
            scatters = []
            for b in range(k):
                loads[b].wait()
                scatters += [pltpu.async_copy(rows_v.at[b], out_hbm.at[idx_v.at[b * TOP_K + kk]],
                                              sem_scatter.at[b]) for kk in range(TOP_K)]
            for copy in scatters:
                copy.wait()

    return body(src, idx2d)


def _sc_gather_rows(table, idx):
    d = table.shape[1]
    w, k = SC_WINDOW, SC_GATHER_INFLIGHT
    assert idx.shape[0] % (w * k * SC_WORKERS) == 0
    per_worker = idx.shape[0] // (w * SC_WORKERS)
    idx2d = idx.reshape(-1, w)

    @functools.partial(
        pl.kernel, mesh=_sc_mesh(),
        out_type=jax.ShapeDtypeStruct((idx.shape[0], d), table.dtype),
        scratch_types=[pltpu.VMEM((k, w), jnp.int32), pltpu.VMEM((k, w, d), table.dtype),
                       pltpu.SemaphoreType.DMA((k,)), pltpu.SemaphoreType.DMA((k,))],
        name="sc_gather_rows")
    def body(table_hbm, idx_hbm, out_hbm, idx_v, rows_v, sem_gather, sem_store):
        wid = lax.axis_index("s") * SC_CORES + lax.axis_index("c")

        @pl.loop(0, per_worker // k)
        def _(i):
            win0 = wid * per_worker + i * k
            pltpu.sync_copy(idx_hbm.at[pl.ds(pl.multiple_of(win0, k), k)], idx_v)
            gathers = [pltpu.async_copy(table_hbm.at[idx_v.at[b]], rows_v.at[b], sem_gather.at[b])
                       for b in range(k)]
            stores = []
            for b in range(k):
                gathers[b].wait()
                stores.append(pltpu.async_copy(
                    rows_v.at[b], out_hbm.at[pl.ds(pl.multiple_of((win0 + b) * w, w), w)], sem_store.at[b]))
            for copy in stores:
                copy.wait()

    return body(table, idx2d)


def _ffn_kernel(blk_ref, exp_ref, valid_ref, first_ref, next_ref, slot_ref, nused_ref,
                xb_ref, w1_hbm, b1_ref, w2_hbm, b2_ref, yb_ref, w1f_ref, w2f_ref, w1b_ref, w2b_ref, sems):
    j = pl.program_id(0)
    active = j < nused_ref[0]

    def weight_copies(e, slot):
        return (pltpu.make_async_copy(w1_hbm.at[e], w1f_ref.at[slot], sems.at[0, slot]),
                pltpu.make_async_copy(w2_hbm.at[e], w2f_ref.at[slot], sems.at[1, slot]))

    @pl.when(jnp.logical_and(active, first_ref[j] == 1))
    def _():
        e = exp_ref[j]
        slot = slot_ref[j]
        nxt = next_ref[j]

        @pl.when(j == 0)
        def _():
            for copy in weight_copies(e, slot):
                copy.start()

        @pl.when(nxt >= 0)
        def _():
            for copy in weight_copies(nxt, 1 - slot):
                copy.start()

        for copy in weight_copies(e, slot):
            copy.wait()
        w1b_ref[...] = w1f_ref[slot].astype(BF16)
        w2b_ref[...] = w2f_ref[slot].astype(BF16)

    @pl.when(active)
    def _():
        row = lax.broadcasted_iota(jnp.int32, (MOE_BLOCK, 1), 0)
        x = jnp.where(row < valid_ref[j], _unpack_bf16_pairs(xb_ref[...]), 0.0).astype(BF16)
        hc = _dot(x, w1b_ref[...]) + b1_ref[0]
        gate = jnp.minimum(hc[:, :D_FF], SWIGLU_LIMIT)
        up = jnp.clip(hc[:, D_FF:], -SWIGLU_LIMIT, SWIGLU_LIMIT)
        act = (up + 1.0) * gate * _sigmoid(SWIGLU_ALPHA * gate)
        yb_ref[...] = _pack_bf16_pairs(_dot(act.astype(BF16), w2b_ref[...]) + b2_ref[0])


def _ffn(blk_idx, blk_exp, blk_valid, blk_first, blk_next, blk_slot, n_used, xb, w1, b1, w2, b2, n_blocks):
    bm = MOE_BLOCK
    grid_spec = pltpu.PrefetchScalarGridSpec(
        num_scalar_prefetch=7,
        grid=(n_blocks,),
        in_specs=[pl.BlockSpec((bm, D_MODEL // 2), lambda j, bi, be, *_: (bi[j], 0)),
                  pl.BlockSpec(memory_space=pl.ANY),
                  pl.BlockSpec((1, 1, 2 * D_FF), lambda j, bi, be, *_: (be[j], 0, 0)),
                  pl.BlockSpec(memory_space=pl.ANY),
                  pl.BlockSpec((1, 1, D_MODEL), lambda j, bi, be, *_: (be[j], 0, 0))],
        out_specs=pl.BlockSpec((bm, D_MODEL // 2), lambda j, bi, be, *_: (bi[j], 0)),
        scratch_shapes=[pltpu.VMEM((2, D_MODEL, 2 * D_FF), F32), pltpu.VMEM((2, D_FF, D_MODEL), F32),
                        pltpu.VMEM((D_MODEL, 2 * D_FF), BF16), pltpu.VMEM((D_FF, D_MODEL), BF16),
                        pltpu.SemaphoreType.DMA((2, 2))],
    )
    return pl.pallas_call(
        _ffn_kernel,
        grid_spec=grid_spec,
        out_shape=jax.ShapeDtypeStruct(xb.shape, jnp.uint32),
        compiler_params=pltpu.CompilerParams(
            dimension_semantics=("arbitrary",), vmem_limit_bytes=VMEM_LIMIT_BYTES),
        name="expert_ffn",
    )(blk_idx, blk_exp, blk_valid, blk_first, blk_next, blk_slot, n_used, xb, w1, b1, w2, b2)


def _combine_kernel(gk_ref, h1_ref, y0_ref, y1_ref, y2_ref, y3_ref, lng_ref, lnb_ref, out_ref):
    gk = gk_ref[...]
    ffn = gk[:, 0:1] * _unpack_bf16_pairs(y0_ref[...])
    for kk, y_ref in enumerate((y1_ref, y2_ref, y3_ref), start=1):
        ffn = ffn + gk[:, kk:kk + 1] * _unpack_bf16_pairs(y_ref[...])
    out_ref[...] = _layer_norm(DEEPNORM_ALPHA * h1_ref[...] + ffn, lng_ref[...], lnb_ref[...])


def _combine(gk, h1, ysel, lng, lnb):
    rows = h1.shape[0]
    tm = COMBINE_ROWS
    assert rows % tm == 0
    n_tiles = rows // tm
    const = lambda a: pl.BlockSpec(a.shape, lambda i: (0,) * a.ndim)
    ksel = lambda kk: pl.BlockSpec((tm, D_MODEL // 2), lambda i: (kk * n_tiles + i, 0))
    return pl.pallas_call(
        _combine_kernel,
        grid=(n_tiles,),
        in_specs=[pl.BlockSpec((tm, TOP_K), lambda i: (i, 0)),
                  pl.BlockSpec((tm, D_MODEL), lambda i: (i, 0)),
                  ksel(0), ksel(1), ksel(2), ksel(3),
                  const(lng), const(lnb)],
        out_specs=pl.BlockSpec((tm, D_MODEL), lambda i: (i, 0)),
        out_shape=jax.ShapeDtypeStruct((rows, D_MODEL), F32),
        compiler_params=pltpu.CompilerParams(
            dimension_semantics=("arbitrary",), vmem_limit_bytes=VMEM_LIMIT_BYTES),
        name="combine",
    )(gk, h1, ysel, ysel, ysel, ysel, lng, lnb)


def _row(v, width=None):
    v = v.reshape(1, -1).astype(F32)
    if width is not None and v.shape[1] < width:
        v = jnp.pad(v, ((0, 0), (0, width - v.shape[1])))
    return v


def _encode_all(x_a, x_b, meta_tokens, ln_emb_g, ln_emb_b, w_in, conv_w, conv_b, dt_bias, a_log,
                d_skip, ssd_norm_g, i_bias, f_bias, mlstm_norm_g, w_out, ln1_g, ln1_b, w_router, b_router,
                w1, b1, w2, b2, ln2_g, ln2_b):
    n_a, seq_len, _ = x_a.shape
    n_seq = n_a + x_b.shape[0]
    rows_a = n_a * seq_len
    assert seq_len % CHUNK == 0
    n_chunks = seq_len // CHUNK + 1
    n_tok = n_seq * seq_len

    sizes = (1024, CONV_CH, 2 * SSD_HEADS, 512, 512, 1024, 1024, 2 * MLSTM_HEADS, 2 * MLSTM_HEADS)
    offs = [0]
    for s in sizes:
        offs.append(offs[-1] + s)
    w_z, w_xbc, w_dt, w_q, w_k, w_v, w_o, w_i, w_f = [w_in[:, offs[j]:offs[j + 1]] for j in range(9)]
    w_big = jnp.concatenate([w_z, w_xbc, w_q, w_k * (MLSTM_DK ** -0.5), w_v, w_o], axis=1).astype(BF16)
    zpad = jnp.zeros((D_MODEL, LANES - GATE_END), F32)
    gate_cols = []
    for d in range(2):
        gate_cols += [w_dt[:, d * SSD_HEADS:(d + 1) * SSD_HEADS],
                      w_i[:, d * MLSTM_HEADS:(d + 1) * MLSTM_HEADS],
                      w_f[:, d * MLSTM_HEADS:(d + 1) * MLSTM_HEADS], zpad]
    w_gates = jnp.concatenate(gate_cols, axis=1).astype(BF16)
    gbias = [_row(jnp.concatenate([dt_bias[d], i_bias[d], f_bias[d]]), LANES) for d in range(2)]
    alog = [jnp.pad(jnp.broadcast_to(a_log[d].astype(F32)[:, None], (SSD_HEADS, LANES)),
                    ((0, GATE_END - SSD_HEADS), (0, 0))) for d in range(2)]
    head_of_col = jnp.arange(D_MODEL, dtype=jnp.int32) // SSD_HEAD_DIM
    lane_id = jnp.arange(LANES, dtype=jnp.int32)[:, None]
    expand = jnp.concatenate([lane_id == GATE_END + head_of_col[None, :],
                              lane_id == 2 * GATE_END + head_of_col[None, :]], axis=1).astype(BF16)
    dskip = _row(jnp.repeat(d_skip, SSD_HEAD_DIM))
    conv_w8 = jnp.pad(conv_w.astype(F32), ((0, 8 - CONV_W), (0, 0)))
    meta_tile = jnp.pad(meta_tokens.astype(F32), ((PAD_FRONT, INPROJ_ROWS - CHUNK), (0, 0)))

    z, xbc, q, k, v, o, gates_f, gates_b = _inproj(x_a, x_b, meta_tile, _row(ln_emb_g), _row(ln_emb_b),
                                                   w_big, w_gates)
    xs, bm, cm = _conv(xbc, conv_w8, _row(conv_b), n_chunks)
    yb, hb = _mixer_pass(xs, bm, cm, gates_b, q, k, v, gbias[1], alog[1], expand, n_chunks, reverse=True)
    (ycat,) = _mixer_pass(xs, bm, cm, gates_f, q, k, v, gbias[0], alog[0], expand, n_chunks,
                          reverse=False,
                          final_inputs=(z, o, yb, hb, dskip, _row(ssd_norm_g), _row(mlstm_norm_g)))

    wr = jnp.pad(w_router.astype(F32), ((0, 0), (0, LANES - N_EXPERTS)))
    wrh = wr.astype(BF16)
    wrl = (wr - wrh.astype(F32)).astype(BF16)
    w_out_b = w_out.astype(BF16)
    r_i = jnp.arange(RANK_ROWS, dtype=jnp.int32)
    lstrict = (r_i[None, :] < r_i[:, None]).astype(BF16)
    l_i = jnp.arange(LANES, dtype=jnp.int32)
    ucum = (l_i[:, None] <= l_i[None, :]).astype(BF16)
    ycat2d = ycat.reshape(n_tok, 2 * D_MODEL)
    outs = []
    for x_part, row0 in ((x_a, 0), (x_b, rows_a)):
        n_part = x_part.shape[0] * seq_len
        h1, h1p, sel, gates, expert_counts = _epilogue(x_part.reshape(n_part, D_MODEL), ycat2d, row0, w_out_b,
                                        _row(ln_emb_g), _row(ln_emb_b), _row(ln1_g), _row(ln1_b),
                                        wrh, wrl, _row(b_router, LANES))
        n_blocks = n_part * TOP_K // MOE_BLOCK + N_EXPERTS
        trash_row = n_blocks * MOE_BLOCK
        dest, gk, stats = _rank(sel, gates, expert_counts, lstrict, ucum, trash_row)
        counts = stats[0, :N_EXPERTS].astype(jnp.int32)
        starts = stats[1, :N_EXPERTS].astype(jnp.int32)
        pends = stats[2, :N_EXPERTS].astype(jnp.int32)
        n_used = pends[N_EXPERTS - 1] // MOE_BLOCK
        blk = jnp.minimum(jnp.arange(n_blocks, dtype=jnp.int32), jnp.maximum(n_used - 1, 0))
        blk_exp = jnp.minimum(
            jnp.sum((pends[None, :] <= (blk * MOE_BLOCK)[:, None]).astype(jnp.int32), axis=1),
            N_EXPERTS - 1).astype(jnp.int32)
        blk_last = jnp.take(starts + counts, blk_exp)
        blk_valid = jnp.clip(blk_last - blk * MOE_BLOCK, 0, MOE_BLOCK).astype(jnp.int32)
        dest_km = dest.T.reshape(-1)
        dest_wm = dest.reshape(n_part // SC_WINDOW, SC_WINDOW, TOP_K).transpose(0, 2, 1)
        xb = _sc_scatter_rows(h1p, dest_wm, trash_row + MOE_BLOCK)
        j_i = jnp.arange(n_blocks, dtype=jnp.int32)
        blk_first = jnp.logical_and(j_i < n_used, jnp.logical_or(j_i == 0, blk_exp != jnp.roll(blk_exp, 1)))
        blk_slot = ((jnp.cumsum(blk_first.astype(jnp.int32)) - 1) % 2).astype(jnp.int32)
        run_end = jnp.take(pends, blk_exp) // MOE_BLOCK
        blk_next = jnp.where(run_end < n_used, jnp.take(blk_exp, jnp.minimum(run_end, n_blocks - 1)),
                             -1).astype(jnp.int32)
        yexp = _ffn(blk, blk_exp, blk_valid, blk_first.astype(jnp.int32), blk_next, blk_slot,
                    n_used.reshape(1), xb, w1, b1.reshape(N_EXPERTS, 1, -1), w2,
                    b2.reshape(N_EXPERTS, 1, -1), n_blocks)
        ysel = _sc_gather_rows(yexp, dest_km)
        outs.append(_combine(gk, h1, ysel, _row(ln2_g), _row(ln2_b)))
    return outs


def kernel(x_prompt, x_sample, meta_tokens, ln_emb_g, ln_emb_b, w_in, conv_w, conv_b, dt_bias, a_log,
           d_skip, ssd_norm_g, i_bias, f_bias, mlstm_norm_g, w_out, ln1_g, ln1_b, w_router, b_router,
           w1, b1, w2, b2, ln2_g, ln2_b):
    assert x_prompt.shape[1:] == x_sample.shape[1:]
    n_p, seq_len, d = x_prompt.shape
    n_s = x_sample.shape[0]
    y_p, y_s = _encode_all(x_prompt.astype(F32), x_sample.astype(F32), meta_tokens, ln_emb_g, ln_emb_b, w_in[0], conv_w[0],
                           conv_b[0], dt_bias[0], a_log[0], d_skip[0], ssd_norm_g[0], i_bias[0], f_bias[0],
                           mlstm_norm_g[0], w_out[0], ln1_g[0], ln1_b[0], w_router[0], b_router[0],
                           w1[0], b1[0], w2[0], b2[0], ln2_g[0], ln2_b[0])
    return (y_p.reshape(n_p, seq_len, d), y_s.reshape(n_s, seq_len, d))
```

```python
import functools

import jax
import jax.numpy as jnp
from jax import lax
from jax.experimental import pallas as pl
from jax.experimental.pallas import tpu as pltpu
from jax.experimental.pallas import tpu_sc as plsc

F32 = jnp.float32
BF16 = jnp.bfloat16

D_MODEL = 1024
N_META = 16
CHUNK = 128
PAD_FRONT = CHUNK - N_META
SSD_HEADS = 16
SSD_HEAD_DIM = 64
SSD_GROUPS = 4
SSD_STATE = 128
HEADS_PER_GROUP = SSD_HEADS // SSD_GROUPS
GROUP_WIDTH = HEADS_PER_GROUP * SSD_HEAD_DIM
CONV_W = 5
CONV_HALF = CONV_W // 2
CONV_CH = D_MODEL + 2 * SSD_GROUPS * SSD_STATE
MLSTM_HEADS = 8
MLSTM_DK = 64
MLSTM_DV = 128
N_EXPERTS = 32
TOP_K = 4
D_FF = D_MODEL
SWIGLU_LIMIT = 7.0
SWIGLU_ALPHA = 1.702
DEEPNORM_ALPHA = 2.0 ** 0.25
LN_EPS = 1e-5
RMS_EPS = 1e-5
NEG_GATE = -1e30
LOG2E = 1.4426950408889634

LANES = 128
BF16_SUBLANES = 16
VMEM_LIMIT_BYTES = 56 * 1024 * 1024

GATE_DT0, GATE_I0, GATE_F0, GATE_END = 0, SSD_HEADS, SSD_HEADS + MLSTM_HEADS, SSD_HEADS + 2 * MLSTM_HEADS

INPROJ_ROWS = 512
EPILOGUE_ROWS = 512
RANK_ROWS = 512
COMBINE_ROWS = 512
MOE_BLOCK = 512

SC_CORES = 2
SC_SUBCORES = 16
SC_WORKERS = SC_CORES * SC_SUBCORES
SC_WINDOW = 32
SC_SCATTER_INFLIGHT = 2
SC_GATHER_INFLIGHT = 4


def _dot(a, b):
    return jnp.dot(a, b, preferred_element_type=F32)


def _dot_nt(a, b):
    return lax.dot_general(a, b, (((1,), (1,)), ((), ())), preferred_element_type=F32)


def _split3(x):
    hi = x.astype(BF16)
    r1 = x - hi.astype(F32)
    mid = r1.astype(BF16)
    lo = (r1 - mid.astype(F32)).astype(BF16)
    return hi, mid, lo


def _dot_exact_lhs(a_bf16, x):
    hi, mid, lo = _split3(x)
    return _dot(a_bf16, hi) + _dot(a_bf16, mid) + _dot(a_bf16, lo)


def _dot_exact_rhs(x, b_bf16):
    hi, mid, lo = _split3(x)
    return _dot(hi, b_bf16) + _dot(mid, b_bf16) + _dot(lo, b_bf16)


def _layer_norm(x, g, b):
    mu = jnp.mean(x, axis=-1, keepdims=True)
    xc = x - mu
    var = jnp.mean(xc * xc, axis=-1, keepdims=True)
    return xc * lax.rsqrt(var + LN_EPS) * g + b


def _sigmoid(x):
    return 1.0 / (1.0 + jnp.exp(-x))


def _log1p_exp_neg_abs(x):
    return jnp.log(1.0 + jnp.exp(-jnp.abs(x)))


def _pack_bf16_pairs(x):
    n = x.shape[1] // 2
    bits = lax.bitcast_convert_type(x.astype(BF16).astype(F32), jnp.uint32)
    return (bits[:, :n] >> 16) | bits[:, n:]


def _unpack_bf16_pairs(words):
    lo = lax.bitcast_convert_type(words << 16, F32)
    hi = lax.bitcast_convert_type(words & jnp.uint32(0xFFFF0000), F32)
    return jnp.concatenate([lo, hi], axis=1)


def _storage_chunk(c, n_chunks):
    return jnp.where(c == 0, n_chunks - 1, c - 1)


def _inproj_kernel(xa_ref, xb_ref, meta_ref, g_ref, b_ref, wbig_ref, wg_ref,
                   z_ref, xbc_ref, q_ref, k_ref, v_ref, o_ref, gf_ref, gb_ref, *, n_x_tiles, n_a):
    x = jnp.where(pl.program_id(0) < n_a, xa_ref[...], xb_ref[...])
    x = jnp.where(pl.program_id(1) == n_x_tiles, meta_ref[...], x)
    h = _layer_norm(x, g_ref[...], b_ref[...]).astype(BF16)

    def mm(c0, c1):
        return _dot(h, wbig_ref[:, c0:c1]).astype(BF16)

    z_ref[...] = mm(0, 1024)
    xbc_ref[:, 0:1024] = mm(1024, 2048)
    xbc_ref[:, 1024:2048] = mm(2048, 3072)
    q_ref[...] = mm(3072, 3584)
    k_ref[...] = mm(3584, 4096)
    v_ref[...] = mm(4096, 5120)
    o_ref[...] = mm(5120, 6144)
    gates = _dot(h, wg_ref[...])
    gf_ref[...] = gates[:, :LANES]
    gb_ref[...] = gates[:, LANES:]


def _inproj(x_a, x_b, meta_tile, ln_g, ln_b, w_big, w_gates):
    n_a, seq_len, _ = x_a.shape
    n_seq = n_a + x_b.shape[0]
    tm = INPROJ_ROWS
    assert seq_len % tm == 0 and tm >= CHUNK
    n_x_tiles = seq_len // tm
    rows = seq_len + CHUNK
    row_spec = lambda n: pl.BlockSpec((None, tm, n), lambda b, i: (b, i, 0))
    const = lambda a: pl.BlockSpec(a.shape, lambda b, i: (0,) * a.ndim)
    resident = lambda a: pl.BlockSpec(a.shape, lambda b, i: (0,) * a.ndim, pipeline_mode=pl.Buffered(1))
    widths = (1024, CONV_CH, 512, 512, 1024, 1024)
    out_shapes = [jax.ShapeDtypeStruct((n_seq, rows, w), BF16) for w in widths]
    out_shapes += [jax.ShapeDtypeStruct((n_seq, rows, LANES), F32)] * 2
    return pl.pallas_call(
        functools.partial(_inproj_kernel, n_x_tiles=n_x_tiles, n_a=n_a),
        grid=(n_seq, n_x_tiles + 1),
        in_specs=[pl.BlockSpec((None, tm, D_MODEL), lambda b, i: (
                      jnp.minimum(b, n_a - 1),
                      jnp.where(b < n_a, jnp.minimum(i, n_x_tiles - 1), n_x_tiles - 1), 0)),
                  pl.BlockSpec((None, tm, D_MODEL), lambda b, i: (
                      jnp.maximum(b - n_a, 0),
                      jnp.where(b < n_a, 0, jnp.minimum(i, n_x_tiles - 1)), 0)),
                  resident(meta_tile), const(ln_g), const(ln_b), resident(w_big), resident(w_gates)],
        out_specs=[row_spec(s.shape[2]) for s in out_shapes],
        out_shape=out_shapes,
        compiler_params=pltpu.CompilerParams(
            dimension_semantics=("arbitrary", "arbitrary"), vmem_limit_bytes=VMEM_LIMIT_BYTES),
        name="inproj",
    )(x_a, x_b, meta_tile, ln_g, ln_b, w_big, w_gates)


def _conv_kernel(prev_ref, main_ref, next_ref, shift_ref, w_ref, b_ref, xs_ref, bm_ref, cm_ref,
                 *, n_x_tiles, chunks_per_tile):
    i = pl.program_id(1)
    w = w_ref[...]
    bias = b_ref[...]
    shift = shift_ref[...]

    def conv_chunk(before, rows, after, pad_rows, j):
        shifted = _dot(shift, jnp.concatenate([before, rows, after], axis=0).astype(BF16))
        acc = bias + rows * w[CONV_HALF:CONV_HALF + 1, :]
        for jj, t in enumerate(t for t in range(CONV_W) if t != CONV_HALF):
            acc = acc + shifted[jj * CHUNK:(jj + 1) * CHUNK, :] * w[t:t + 1, :]
        y = acc * _sigmoid(acc)
        if pad_rows is not None:
            y = jnp.where(pad_rows, 0.0, y)
        r = slice(j * CHUNK, (j + 1) * CHUNK)
        xs_ref[r, :] = y[:, :D_MODEL].astype(BF16)
        bm_ref[r, :] = y[:, D_MODEL:D_MODEL + 512].astype(BF16)
        cm_ref[r, :] = y[:, D_MODEL + 512:].astype(BF16)

    @pl.when(i < n_x_tiles)
    def _():
        tile = main_ref[...].astype(F32)
        for j in range(chunks_per_tile):
            lo, hi = j * CHUNK, (j + 1) * CHUNK
            before = prev_ref[...].astype(F32) if j == 0 else tile[lo - BF16_SUBLANES:lo, :]
            if j == chunks_per_tile - 1:
                after = jnp.where(i == n_x_tiles - 1, 0.0, next_ref[...].astype(F32))
            else:
                after = tile[hi:hi + BF16_SUBLANES, :]
            conv_chunk(before, tile[lo:hi, :], after, None, j)

    @pl.when(i == n_x_tiles)
    def _():
        row = lax.broadcasted_iota(jnp.int32, (CHUNK, 1), 0)
        pad_rows = row < PAD_FRONT
        rows = jnp.where(pad_rows, 0.0, main_ref[0:CHUNK, :].astype(F32))
        conv_chunk(jnp.zeros((BF16_SUBLANES, CONV_CH), F32), rows, next_ref[...].astype(F32), pad_rows, 0)


def _conv(xbc, conv_w8, conv_b, n_chunks):
    n_seq, rows, _ = xbc.shape
    tm = INPROJ_ROWS
    seq_len = (n_chunks - 1) * CHUNK
    assert seq_len % tm == 0 and tm % CHUNK == 0
    n_x_tiles = seq_len // tm
    halo_per_tile = tm // BF16_SUBLANES
    meta_last_halo = rows // BF16_SUBLANES - 1

    def prev_map(b, i):
        before_tile = jnp.maximum(jnp.minimum(i, n_x_tiles - 1) * halo_per_tile - 1, 0)
        return (b, jnp.where(i == 0, meta_last_halo, before_tile), 0)

    def next_map(b, i):
        return (b, jnp.where(i >= n_x_tiles - 1, 0, (i + 1) * halo_per_tile) , 0)

    const = lambda a: pl.BlockSpec(a.shape, lambda b, i: (0,) * a.ndim)
    l_i = jnp.arange(CHUNK, dtype=jnp.int32)[:, None]
    j_i = jnp.arange(CHUNK + 2 * BF16_SUBLANES, dtype=jnp.int32)[None, :]
    shifts = jnp.concatenate([(j_i == BF16_SUBLANES + l_i + t - CONV_HALF)
                              for t in range(CONV_W) if t != CONV_HALF], axis=0).astype(BF16)
    out_shapes = [jax.ShapeDtypeStruct((n_seq, rows, D_MODEL), BF16),
                  jax.ShapeDtypeStruct((n_seq, rows, 512), BF16),
                  jax.ShapeDtypeStruct((n_seq, rows, 512), BF16)]
    tile_spec = lambda n: pl.BlockSpec((None, tm, n), lambda b, i: (b, i, 0))
    return pl.pallas_call(
        functools.partial(_conv_kernel, n_x_tiles=n_x_tiles, chunks_per_tile=tm // CHUNK),
        grid=(n_seq, n_x_tiles + 1),
        in_specs=[pl.BlockSpec((None, BF16_SUBLANES, CONV_CH), prev_map),
                  tile_spec(CONV_CH),
                  pl.BlockSpec((None, BF16_SUBLANES, CONV_CH), next_map),
                  const(shifts), const(conv_w8), const(conv_b)],
        out_specs=[tile_spec(s.shape[2]) for s in out_shapes],
        out_shape=out_shapes,
        compiler_params=pltpu.CompilerParams(
            dimension_semantics=("arbitrary", "arbitrary"), vmem_limit_bytes=VMEM_LIMIT_BYTES),
        name="conv",
    )(xbc, xbc, xbc, shifts, conv_w8, conv_b)


def _mixer_kernel(*refs, reverse, final, n_chunks, n_seq):
    if final:
        (xs_ref, bm_ref, cm_ref, g_ref, q_ref, k_ref, v_ref, z_ref, o_ref, yb_ref, hb_ref,
         gbias_ref, alog_ref, expand_ref, dskip_ref, ngs_ref, ngm_ref,
         ycat_ref, s_ref, cst_ref, m_ref) = refs
    else:
        (xs_ref, bm_ref, cm_ref, g_ref, q_ref, k_ref, v_ref,
         gbias_ref, alog_ref, expand_ref,
         yout_ref, hout_ref, s_ref, cst_ref, m_ref) = refs

    t = pl.program_id(0)
    c = (n_chunks - 1 - t) if reverse else t
    end = 0 if reverse else CHUNK - 1

    @pl.when(t == 0)
    def _():
        s_ref[...] = jnp.zeros_like(s_ref)
        cst_ref[...] = jnp.zeros_like(cst_ref)
        m_ref[...] = jnp.zeros_like(m_ref)

    row = lax.broadcasted_iota(jnp.int32, (CHUNK, 1), 0)
    col = lax.broadcasted_iota(jnp.int32, (1, CHUNK), 1)
    lane = col
    allowed = (col >= row) if reverse else (col <= row)
    tri = allowed.astype(BF16)
    tri_t = ((row >= col) if reverse else (row <= col)).astype(BF16)
    feat = lax.broadcasted_iota(jnp.int32, (GATE_END, 1), 0)
    is_dt = feat < GATE_I0
    is_i = jnp.logical_and(feat >= GATE_I0, feat < GATE_F0)
    is_f = feat >= GATE_F0
    pad_cols = jnp.logical_and(c == 0, col < PAD_FRONT)
    a_coef = -jnp.exp(alog_ref[...]) * LOG2E
    expand = expand_ref[...]
    left_half = lane < SSD_HEAD_DIM
    right_half = jnp.logical_not(left_half)
    top_half = row < MLSTM_DK
    ones_blk = jnp.ones((CHUNK, MLSTM_DV), BF16)
    full = (CHUNK, LANES)

    def one_sequence(b):
        cgs = [cm_ref[b, :, g * SSD_STATE:(g + 1) * SSD_STATE] for g in range(SSD_GROUPS)]
        bgs = [bm_ref[b, :, g * SSD_STATE:(g + 1) * SSD_STATE] for g in range(SSD_GROUPS)]
        cbs = [_dot_nt(cgs[g], bgs[g]) for g in range(SSD_GROUPS)]
        bg_ts = [bgs[g].astype(F32).T.astype(BF16) for g in range(SSD_GROUPS)]
        q_pairs = [q_ref[b, :, p * LANES:(p + 1) * LANES] for p in range(MLSTM_HEADS // 2)]
        k_pairs = [k_ref[b, :, p * LANES:(p + 1) * LANES] for p in range(MLSTM_HEADS // 2)]
        qks = [_dot_nt(jnp.where(left_half if h % 2 == 0 else right_half, q_pairs[h // 2],
                                 jnp.zeros_like(q_pairs[h // 2])), k_pairs[h // 2])
               for h in range(MLSTM_HEADS)]
        k_pair_ts = [k_pairs[p].astype(F32).T for p in range(MLSTM_HEADS // 2)]
        if final:
            zz = z_ref[b].astype(F32)
            z_gate = zz * _sigmoid(zz)
            o_gate = _sigmoid(o_ref[b].astype(F32))

        gr = (g_ref[b] + gbias_ref[...]).T[0:GATE_END, :]
        lse = _log1p_exp_neg_abs(gr)
        val_t = jnp.where(is_dt, jnp.maximum(gr, 0.0) + lse, jnp.where(is_i, gr, jnp.minimum(gr, 0.0) - lse))
        val_t = jnp.where(pad_cols, jnp.where(is_i, NEG_GATE, 0.0), val_t)
        u_t = jnp.where(is_dt, val_t * a_coef, jnp.where(is_f, val_t * LOG2E, 0.0))
        cums_t = _dot_exact_rhs(u_t, tri_t)
        cums_end = jnp.broadcast_to(cums_t[:, end:end + 1], cums_t.shape)
        p1_t = jnp.exp2(cums_t)
        p2_t = jnp.exp2(cums_end - cums_t) * val_t
        packed = jnp.concatenate([cums_t, p1_t, p2_t, val_t], axis=0).T
        ex = _dot(packed.astype(BF16), expand)
        ex1 = ex[:, :D_MODEL]
        ex2 = ex[:, D_MODEL:]
        chunk_decay = _dot_exact_rhs(jnp.broadcast_to(packed[end:end + 1, :], (8, LANES)),
                                     expand[:, :D_MODEL])[0:1, :]

        xs = xs_ref[b]
        xsf = xs.astype(F32)
        xs_w = (xsf * ex2).astype(BF16)
        src_term = jnp.log(val_t[0:SSD_HEADS, :]) * LOG2E - cums_t[0:SSD_HEADS, :]
        y_diags = []
        for g in range(SSD_GROUPS):
            cb = cbs[g]
            ys = []
            for pp in range(HEADS_PER_GROUP // 2):
                pair = g * (HEADS_PER_GROUP // 2) + pp
                xs_pair = xs[:, pair * LANES:(pair + 1) * LANES]
                zero_pair = jnp.zeros_like(xs_pair)
                m_mats = []
                for h in (2 * pair, 2 * pair + 1):
                    seg = jnp.broadcast_to(packed[:, h:h + 1], full) + src_term[h:h + 1, :]
                    m_mats.append((cb * jnp.exp2(jnp.where(allowed, seg, -jnp.inf))).astype(BF16))
                rhs = jnp.concatenate([jnp.where(left_half, xs_pair, zero_pair),
                                       jnp.where(right_half, xs_pair, zero_pair)], axis=0)
                ys.append(_dot(jnp.concatenate(m_mats, axis=1), rhs))
            gs = slice(g * GROUP_WIDTH, (g + 1) * GROUP_WIDTH)
            s_g = s_old[b][g]
            y_off = _dot(cgs[g], s_g.astype(BF16)) * ex1[:, gs]
            y_diags.append(jnp.concatenate(ys, axis=1) + y_off)
            s_new_all[b].append(chunk_decay[:, gs] * s_g + _dot(bg_ts[g], xs_w[:, gs]))
        y_ssd = jnp.concatenate(y_diags, axis=1)

        bcum_t = cums_t[GATE_F0:GATE_END, :]
        ip_t = val_t[GATE_I0:GATE_F0, :] * LOG2E
        rep = bcum_t.shape
        g_rep = jnp.broadcast_to(bcum_t[:, end:end + 1], rep)
        a_t = g_rep - bcum_t + ip_t
        a_max = jnp.broadcast_to(jnp.max(a_t, axis=1, keepdims=True), rep)
        w_t = jnp.exp2(a_t - a_max)
        m_prev = m_old[b]
        m_new = jnp.maximum(g_rep + m_prev, a_max)
        s_prev = jnp.exp2(g_rep + m_prev - m_new)
        s_new = jnp.exp2(a_max - m_new)
        r_t = ip_t - bcum_t
        h_heads = []
        for pair in range(MLSTM_HEADS // 2):
            h0, h1 = 2 * pair, 2 * pair + 1
            q_pair_f = q_pairs[pair].astype(F32)
            cst = cst_old[b][pair]
            cst_b = cst.astype(BF16)
            v_pair = []
            for hh, h in enumerate((h0, h1)):
                keep = left_half if hh == 0 else right_half
                vh = v_ref[b, :, h * MLSTM_DV:(h + 1) * MLSTM_DV]
                v_pair.append(vh)
                qk = qks[h]
                bc = jnp.broadcast_to(packed[:, GATE_F0 + h:GATE_F0 + h + 1], full)
                dlog = jnp.where(allowed, bc + r_t[h:h + 1, :], -jnp.inf)
                m_intra = jnp.broadcast_to(jnp.max(dlog, axis=1, keepdims=True), full)
                m_inter = bc + m_prev[h:h + 1, :]
                m_t = jnp.maximum(m_inter, m_intra)
                s_mat = (qk * jnp.exp2(dlog - m_t)).astype(BF16)
                qs = (jnp.where(keep, q_pair_f, 0.0) * jnp.exp2(m_inter - m_t)).astype(BF16)
                tot = _dot(jnp.concatenate([s_mat, qs], axis=1),
                           jnp.concatenate([jnp.concatenate([vh, ones_blk], axis=1), cst_b], axis=0))
                num = tot[:, :MLSTM_DV]
                den = tot[:, MLSTM_DV:]
                h_heads.append(num / jnp.maximum(jnp.abs(den), jnp.exp2(-m_t)))
            w_rows = jnp.where(top_half, w_t[h0:h0 + 1, :], w_t[h1:h1 + 1, :])
            kw = (k_pair_ts[pair] * w_rows).astype(BF16)
            full_kv = _dot(kw, jnp.concatenate([v_pair[0], v_pair[1], ones_blk], axis=1))
            kvn = jnp.concatenate(
                [jnp.where(top_half, full_kv[:, :MLSTM_DV], full_kv[:, MLSTM_DV:2 * MLSTM_DV]),
                 full_kv[:, 2 * MLSTM_DV:]], axis=1)
            sp_rows = jnp.where(top_half, s_prev[h0:h0 + 1, :], s_prev[h1:h1 + 1, :])
            sn_rows = jnp.where(top_half, s_new[h0:h0 + 1, :], s_new[h1:h1 + 1, :])
            cst_new_all[b].append(jnp.concatenate([sp_rows, sp_rows], axis=1) * cst
                                  + jnp.concatenate([sn_rows, sn_rows], axis=1) * kvn)
        m_new_all[b] = m_new
        h_ml = jnp.concatenate(h_heads, axis=1)
        if not final:
            return y_ssd.astype(BF16), h_ml.astype(BF16)

        y_tot = y_ssd + yb_ref[b].astype(F32) + dskip_ref[...] * xsf
        y2 = y_tot * z_gate
        y_n = y2 * lax.rsqrt(jnp.mean(y2 * y2, axis=-1, keepdims=True) + RMS_EPS) * ngs_ref[...]
        h_tot = h_ml + hb_ref[b].astype(F32)
        segs = []
        for h in range(MLSTM_HEADS):
            seg = h_tot[:, h * MLSTM_DV:(h + 1) * MLSTM_DV]
            segs.append(seg * lax.rsqrt(jnp.mean(seg * seg, axis=-1, keepdims=True) + RMS_EPS))
        h_n = jnp.concatenate(segs, axis=1) * ngm_ref[...]
        y_ml = o_gate * h_n
        return y_n.astype(BF16), y_ml.astype(BF16)

    s_old = [[s_ref[b, g] for g in range(SSD_GROUPS)] for b in range(n_seq)]
    cst_old = [[cst_ref[b, p] for p in range(MLSTM_HEADS // 2)] for b in range(n_seq)]
    m_old = [m_ref[b] for b in range(n_seq)]
    s_new_all = [[] for _ in range(n_seq)]
    cst_new_all = [[] for _ in range(n_seq)]
    m_new_all = [None] * n_seq
    results = [one_sequence(b) for b in range(n_seq)]
    for b in range(n_seq):
        for g in range(SSD_GROUPS):
            s_ref[b, g] = s_new_all[b][g]
        for p in range(MLSTM_HEADS // 2):
            cst_ref[b, p] = cst_new_all[b][p]
        m_ref[b] = m_new_all[b]

    @pl.when(c > 0)
    def _():
        for b, (first, second) in enumerate(results):
            if final:
                ycat_ref[b, :, :D_MODEL] = first
                ycat_ref[b, :, D_MODEL:] = second
            else:
                yout_ref[b] = first
                hout_ref[b] = second


def _mixer_pass(xs, bm, cm, gates, q, k, v, gbias, alog, expand, n_chunks, *, reverse, final_inputs=None):
    final = final_inputs is not None
    n_seq = xs.shape[0]
    seq_len = (n_chunks - 1) * CHUNK

    def chunk_of(t):
        return (n_chunks - 1 - t) if reverse else t

    def pad_map(t):
        return (0, _storage_chunk(chunk_of(t), n_chunks), 0)

    def out_map(t):
        return (0, jnp.maximum(chunk_of(t) - 1, 0), 0)

    const = lambda a: pl.BlockSpec(a.shape, lambda t: (0,) * a.ndim)
    pad_spec = lambda n: pl.BlockSpec((n_seq, CHUNK, n), pad_map)
    out_spec = lambda n: pl.BlockSpec((n_seq, CHUNK, n), out_map)
    in_arrays = [xs, bm, cm, gates, q, k, v]
    in_specs = [pad_spec(a.shape[2]) for a in in_arrays]
    if final:
        z, o, yb, hb, dskip, ngs, ngm = final_inputs
        in_arrays += [z, o, yb, hb]
        in_specs += [pad_spec(1024), pad_spec(1024), out_spec(1024), out_spec(1024)]
        in_arrays += [gbias, alog, expand, dskip, ngs, ngm]
        in_specs += [const(a) for a in (gbias, alog, expand, dskip, ngs, ngm)]
        out_shape = [jax.ShapeDtypeStruct((n_seq, seq_len, 2 * D_MODEL), BF16)]
        out_specs = [out_spec(2 * D_MODEL)]
    else:
        in_arrays += [gbias, alog, expand]
        in_specs += [const(a) for a in (gbias, alog, expand)]
        out_shape = [jax.ShapeDtypeStruct((n_seq, seq_len, D_MODEL), BF16),
                     jax.ShapeDtypeStruct((n_seq, seq_len, D_MODEL), BF16)]
        out_specs = [out_spec(D_MODEL), out_spec(D_MODEL)]
    return pl.pallas_call(
        functools.partial(_mixer_kernel, reverse=reverse, final=final, n_chunks=n_chunks, n_seq=n_seq),
        grid=(n_chunks,),
        in_specs=in_specs,
        out_specs=out_specs,
        out_shape=out_shape,
        scratch_shapes=[pltpu.VMEM((n_seq, SSD_GROUPS, SSD_STATE, GROUP_WIDTH), F32),
                        pltpu.VMEM((n_seq, MLSTM_HEADS // 2, 2 * MLSTM_DK, 2 * MLSTM_DV), F32),
                        pltpu.VMEM((n_seq, MLSTM_HEADS, LANES), F32)],
        compiler_params=pltpu.CompilerParams(
            dimension_semantics=("arbitrary",), vmem_limit_bytes=VMEM_LIMIT_BYTES),
        name="mixer_fwd" if final else "mixer_bwd",
    )(*in_arrays)


def _epilogue_kernel(x_ref, ycat_ref, wout_ref, lng0_ref, lnb0_ref, lng1_ref, lnb1_ref,
                     wrh_ref, wrl_ref, br_ref, h1_ref, h1p_ref, sel_ref, gate_ref, cnt_ref):
    h0 = _layer_norm(x_ref[...], lng0_ref[...], lnb0_ref[...])
    mix = _dot(ycat_ref[...], wout_ref[...])
    h1 = _layer_norm(DEEPNORM_ALPHA * h0 + mix, lng1_ref[...], lnb1_ref[...])
    h1_ref[...] = h1
    h1p_ref[...] = _pack_bf16_pairs(h1)
    hh = h1.astype(BF16)
    hl = (h1 - hh.astype(F32)).astype(BF16)
    wrh = wrh_ref[...]
    logits = _dot(hh, wrh) + _dot(hl, wrh) + _dot(hh, wrl_ref[...]) + br_ref[...]
    lane = lax.broadcasted_iota(jnp.int32, (1, LANES), 1)
    lane_f = lane.astype(F32)
    logits = jnp.where(lane < N_EXPERTS, logits, -jnp.inf)
    work = logits
    sel = jnp.zeros(logits.shape, jnp.bool_)
    top = None
    for _ in range(TOP_K):
        m = jnp.max(work, axis=-1, keepdims=True)
        if top is None:
            top = m
        first = jnp.min(jnp.where(work == m, lane_f, float(LANES)), axis=-1, keepdims=True)
        pick = lane_f == first
        sel = jnp.logical_or(sel, pick)
        work = jnp.where(pick, -jnp.inf, work)
    e = jnp.where(sel, jnp.exp(logits - top), 0.0)
    gate_ref[...] = e / jnp.sum(e, axis=-1, keepdims=True)
    sel_f = sel.astype(F32)
    sel_ref[...] = sel_f

    @pl.when(pl.program_id(0) == 0)
    def _():
        cnt_ref[...] = jnp.zeros_like(cnt_ref)

    cnt_ref[0:1, :] = cnt_ref[0:1, :] + jnp.sum(sel_f, axis=0, keepdims=True)


def _epilogue(x, ycat, ycat_row0, w_out, lng0, lnb0, lng1, lnb1, wrh, wrl, br):
    rows = x.shape[0]
    tm = EPILOGUE_ROWS
    assert rows % tm == 0 and ycat_row0 % tm == 0
    tile0 = ycat_row0 // tm
    row_spec = lambda n: pl.BlockSpec((tm, n), lambda i: (i, 0))
    const = lambda a: pl.BlockSpec(a.shape, lambda i: (0,) * a.ndim)
    out_shape = [jax.ShapeDtypeStruct((rows, D_MODEL), F32),
                 jax.ShapeDtypeStruct((rows, D_MODEL // 2), jnp.uint32),
                 jax.ShapeDtypeStruct((rows, LANES), F32),
                 jax.ShapeDtypeStruct((rows, LANES), F32),
                 jax.ShapeDtypeStruct((8, LANES), F32)]
    consts = (w_out, lng0, lnb0, lng1, lnb1, wrh, wrl, br)
    return pl.pallas_call(
        _epilogue_kernel,
        grid=(rows // tm,),
        in_specs=[row_spec(D_MODEL), pl.BlockSpec((tm, 2 * D_MODEL), lambda i: (tile0 + i, 0))]
                 + [const(a) for a in consts],
        out_specs=[row_spec(D_MODEL), row_spec(D_MODEL // 2), row_spec(LANES), row_spec(LANES),
                   pl.BlockSpec((8, LANES), lambda i: (0, 0))],
        out_shape=out_shape,
        compiler_params=pltpu.CompilerParams(
            dimension_semantics=("arbitrary",), vmem_limit_bytes=VMEM_LIMIT_BYTES),
        name="epilogue",
    )(x, ycat, *consts)


def _rank_kernel(sel_ref, gate_ref, cnt_ref, lstrict_ref, ucum_ref, dest_ref, gk_ref, stats_ref,
                 base_ref, *, trash_row):
    i = pl.program_id(0)
    sel = sel_ref[...]
    colsum = jnp.sum(sel, axis=0, keepdims=True)

    @pl.when(i == 0)
    def _():
        counts = cnt_ref[0:1, :]
        padded = jnp.ceil(counts / MOE_BLOCK) * MOE_BLOCK
        pend = _dot_exact_rhs(jnp.broadcast_to(padded, (8, LANES)), ucum_ref[...])[0:1, :]
        stats_ref[0:1, :] = counts
        stats_ref[1:2, :] = pend - padded
        stats_ref[2:3, :] = pend
        stats_ref[3:8, :] = jnp.zeros((5, LANES), F32)
        base_ref[...] = jnp.broadcast_to(pend - padded, base_ref.shape)

    before = _dot(lstrict_ref[...], sel.astype(BF16))
    pos = base_ref[0:1, :] + before
    base_ref[0:1, :] = base_ref[0:1, :] + colsum
    work = jnp.where(sel > 0.0, pos + 1.0, 0.0)
    gates = gate_ref[...]
    for kk in range(TOP_K):
        m = jnp.max(work, axis=-1, keepdims=True)
        pick = jnp.logical_and(work == m, m > 0.0)
        gk_ref[:, kk:kk + 1] = jnp.sum(jnp.where(pick, gates, 0.0), axis=-1, keepdims=True)
        dest_ref[:, kk:kk + 1] = jnp.where(m > 0.0, m - 1.0, float(trash_row)).astype(jnp.int32)
        work = jnp.where(pick, 0.0, work)


def _rank(sel, gates, counts, lstrict, ucum, trash_row):
    rows = sel.shape[0]
    tm = RANK_ROWS
    assert rows % tm == 0
    row_spec = lambda n: pl.BlockSpec((tm, n), lambda i: (i, 0))
    const = lambda a: pl.BlockSpec(a.shape, lambda i: (0,) * a.ndim)
    return pl.pallas_call(
        functools.partial(_rank_kernel, trash_row=trash_row),
        grid=(rows // tm,),
        in_specs=[row_spec(LANES), row_spec(LANES), const(counts), const(lstrict), const(ucum)],
        out_specs=[row_spec(TOP_K), row_spec(TOP_K), pl.BlockSpec((8, LANES), lambda i: (0, 0))],
        out_shape=[jax.ShapeDtypeStruct((rows, TOP_K), jnp.int32),
                   jax.ShapeDtypeStruct((rows, TOP_K), F32),
                   jax.ShapeDtypeStruct((8, LANES), F32)],
        scratch_shapes=[pltpu.VMEM((8, LANES), F32)],
        compiler_params=pltpu.CompilerParams(
            dimension_semantics=("arbitrary",), vmem_limit_bytes=VMEM_LIMIT_BYTES),
        name="rank",
    )(sel, gates, counts, lstrict, ucum)


def _sc_mesh():
    return plsc.VectorSubcoreMesh(core_axis_name="c", subcore_axis_name="s",
                                  num_cores=SC_CORES, num_subcores=SC_SUBCORES)


def _sc_scatter_rows(src, idx, n_out_rows):
    n_src, d = src.shape
    w, k = SC_WINDOW, SC_SCATTER_INFLIGHT
    assert idx.shape == (n_src // w, TOP_K, w) and n_src % (w * k * SC_WORKERS) == 0
    per_worker = n_src // (w * SC_WORKERS)
    idx2d = idx.reshape(n_src // w * TOP_K, w)

    @functools.partial(
        pl.kernel, mesh=_sc_mesh(),
        out_type=jax.ShapeDtypeStruct((n_out_rows, d), src.dtype),
        scratch_types=[pltpu.VMEM((k * TOP_K, w), jnp.int32), pltpu.VMEM((k, w, d), src.dtype),
                       pltpu.SemaphoreType.DMA((k,)), pltpu.SemaphoreType.DMA((k,))],
        name="sc_scatter_rows")
    def body(src_hbm, idx_hbm, out_hbm, idx_v, rows_v, sem_load, sem_scatter):
        wid = lax.axis_index("s") * SC_CORES + lax.axis_index("c")

        @pl.loop(0, per_worker // k)
        def _(i):
            win0 = wid * per_worker + i * k
            loads = [pltpu.async_copy(src_hbm.at[pl.ds(pl.multiple_of((win0 + b) * w, w), w)], rows_v.at[b],
                                      sem_load.at[b]) for b in range(k)]
            pltpu.sync_copy(idx_hbm.at[pl.ds(pl.multiple_of(win0 * TOP_K, TOP_K), k * TOP_K)], idx_v)
            scatters = []
            for b in range(k):
                loads[b].wait()
                scatters += [pltpu.async_copy(rows_v.at[b], out_hbm.at[idx_v.at[b * TOP_K + kk]],
                                              sem_scatter.at[b]) for kk in range(TOP_K)]
            for copy in scatters:
                copy.wait()

    return body(src, idx2d)


def _sc_gather_rows(table, idx):
    d = table.shape[1]
    w, k = SC_WINDOW, SC_GATHER_INFLIGHT
    assert idx.shape[0] % (w * k * SC_WORKERS) == 0
    per_worker = idx.shape[0] // (w * SC_WORKERS)
    idx2d = idx.reshape(-1, w)

    @functools.partial(
        pl.kernel, mesh=_sc_mesh(),
        out_type=jax.ShapeDtypeStruct((idx.shape[0], d), table.dtype),
        scratch_types=[pltpu.VMEM((k, w), jnp.int32), pltpu.VMEM((k, w, d), table.dtype),
                       pltpu.SemaphoreType.DMA((k,)), pltpu.SemaphoreType.DMA((k,))],
        name="sc_gather_rows")
    def body(table_hbm, idx_hbm, out_hbm, idx_v, rows_v, sem_gather, sem_store):
        wid = lax.axis_index("s") * SC_CORES + lax.axis_index("c")

        @pl.loop(0, per_worker // k)
        def _(i):
            win0 = wid * per_worker + i * k
            pltpu.sync_copy(idx_hbm.at[pl.ds(pl.multiple_of(win0, k), k)], idx_v)
            gathers = [pltpu.async_copy(table_hbm.at[idx_v.at[b]], rows_v.at[b], sem_gather.at[b])
                       for b in range(k)]
            stores = []
            for b in range(k):
                gathers[b].wait()
                stores.append(pltpu.async_copy(
                    rows_v.at[b], out_hbm.at[pl.ds(pl.multiple_of((win0 + b) * w, w), w)], sem_store.at[b]))
            for copy in stores:
                copy.wait()

    return body(table, idx2d)


def _ffn_kernel(blk_ref, exp_ref, valid_ref, first_ref, next_ref, slot_ref, nused_ref,
                xb_ref, w1_hbm, b1_ref, w2_hbm, b2_ref, yb_ref, w1f_ref, w2f_ref, w1b_ref, w2b_ref, sems):
    j = pl.program_id(0)
    active = j < nused_ref[0]

    def weight_copies(e, slot):
        return (pltpu.make_async_copy(w1_hbm.at[e], w1f_ref.at[slot], sems.at[0, slot]),
                pltpu.make_async_copy(w2_hbm.at[e], w2f_ref.at[slot], sems.at[1, slot]))

    @pl.when(jnp.logical_and(active, first_ref[j] == 1))
    def _():
        e = exp_ref[j]
        slot = slot_ref[j]
        nxt = next_ref[j]

        @pl.when(j == 0)
        def _():
            for copy in weight_copies(e, slot):
                copy.start()

        @pl.when(nxt >= 0)
        def _():
            for copy in weight_copies(nxt, 1 - slot):
                copy.start()

        for copy in weight_copies(e, slot):
            copy.wait()
        w1b_ref[...] = w1f_ref[slot].astype(BF16)
        w2b_ref[...] = w2f_ref[slot].astype(BF16)

    @pl.when(active)
    def _():
        row = lax.broadcasted_iota(jnp.int32, (MOE_BLOCK, 1), 0)
        x = jnp.where(row < valid_ref[j], _unpack_bf16_pairs(xb_ref[...]), 0.0).astype(BF16)
        hc = _dot(x, w1b_ref[...]) + b1_ref[0]
        gate = jnp.minimum(hc[:, :D_FF], SWIGLU_LIMIT)
        up = jnp.clip(hc[:, D_FF:], -SWIGLU_LIMIT, SWIGLU_LIMIT)
        act = (up + 1.0) * gate * _sigmoid(SWIGLU_ALPHA * gate)
        yb_ref[...] = _pack_bf16_pairs(_dot(act.astype(BF16), w2b_ref[...]) + b2_ref[0])


def _ffn(blk_idx, blk_exp, blk_valid, blk_first, blk_next, blk_slot, n_used, xb, w1, b1, w2, b2, n_blocks):
    bm = MOE_BLOCK
    grid_spec = pltpu.PrefetchScalarGridSpec(
        num_scalar_prefetch=7,
        grid=(n_blocks,),
        in_specs=[pl.BlockSpec((bm, D_MODEL // 2), lambda j, bi, be, *_: (bi[j], 0)),
                  pl.BlockSpec(memory_space=pl.ANY),
                  pl.BlockSpec((1, 1, 2 * D_FF), lambda j, bi, be, *_: (be[j], 0, 0)),
                  pl.BlockSpec(memory_space=pl.ANY),
                  pl.BlockSpec((1, 1, D_MODEL), lambda j, bi, be, *_: (be[j], 0, 0))],
        out_specs=pl.BlockSpec((bm, D_MODEL // 2), lambda j, bi, be, *_: (bi[j], 0)),
        scratch_shapes=[pltpu.VMEM((2, D_MODEL, 2 * D_FF), F32), pltpu.VMEM((2, D_FF, D_MODEL), F32),
                        pltpu.VMEM((D_MODEL, 2 * D_FF), BF16), pltpu.VMEM((D_FF, D_MODEL), BF16),
                        pltpu.SemaphoreType.DMA((2, 2))],
    )
    return pl.pallas_call(
        _ffn_kernel,
        grid_spec=grid_spec,
        out_shape=jax.ShapeDtypeStruct(xb.shape, jnp.uint32),
        compiler_params=pltpu.CompilerParams(
            dimension_semantics=("arbitrary",), vmem_limit_bytes=VMEM_LIMIT_BYTES),
        name="expert_ffn",
    )(blk_idx, blk_exp, blk_valid, blk_first, blk_next, blk_slot, n_used, xb, w1, b1, w2, b2)


def _combine_kernel(gk_ref, h1_ref, y0_ref, y1_ref, y2_ref, y3_ref, lng_ref, lnb_ref, out_ref):
    gk = gk_ref[...]
    ffn = gk[:, 0:1] * _unpack_bf16_pairs(y0_ref[...])
    for kk, y_ref in enumerate((y1_ref, y2_ref, y3_ref), start=1):
        ffn = ffn + gk[:, kk:kk + 1] * _unpack_bf16_pairs(y_ref[...])
    out_ref[...] = _layer_norm(DEEPNORM_ALPHA * h1_ref[...] + ffn, lng_ref[...], lnb_ref[...])


def _combine(gk, h1, ysel, lng, lnb):
    rows = h1.shape[0]
    tm = COMBINE_ROWS
    assert rows % tm == 0
    n_tiles = rows // tm
    const = lambda a: pl.BlockSpec(a.shape, lambda i: (0,) * a.ndim)
    ksel = lambda kk: pl.BlockSpec((tm, D_MODEL // 2), lambda i: (kk * n_tiles + i, 0))
    return pl.pallas_call(
        _combine_kernel,
        grid=(n_tiles,),
        in_specs=[pl.BlockSpec((tm, TOP_K), lambda i: (i, 0)),
                  pl.BlockSpec((tm, D_MODEL), lambda i: (i, 0)),
                  ksel(0), ksel(1), ksel(2), ksel(3),
                  const(lng), const(lnb)],
        out_specs=pl.BlockSpec((tm, D_MODEL), lambda i: (i, 0)),
        out_shape=jax.ShapeDtypeStruct((rows, D_MODEL), F32),
        compiler_params=pltpu.CompilerParams(
            dimension_semantics=("arbitrary",), vmem_limit_bytes=VMEM_LIMIT_BYTES),
        name="combine",
    )(gk, h1, ysel, ysel, ysel, ysel, lng, lnb)


def _row(v, width=None):
    v = v.reshape(1, -1).astype(F32)
    if width is not None and v.shape[1] < width:
        v = jnp.pad(v, ((0, 0), (0, width - v.shape[1])))
    return v


def _encode_all(x_a, x_b, meta_tokens, ln_emb_g, ln_emb_b, w_in, conv_w, conv_b, dt_bias, a_log,
                d_skip, ssd_norm_g, i_bias, f_bias, mlstm_norm_g, w_out, ln1_g, ln1_b, w_router, b_router,
                w1, b1, w2, b2, ln2_g, ln2_b):
    n_a, seq_len, _ = x_a.shape
    n_seq = n_a + x_b.shape[0]
    rows_a = n_a * seq_len
    assert seq_len % CHUNK == 0
    n_chunks = seq_len // CHUNK + 1
    n_tok = n_seq * seq_len

    sizes = (1024, CONV_CH, 2 * SSD_HEADS, 512, 512, 1024, 1024, 2 * MLSTM_HEADS, 2 * MLSTM_HEADS)
    offs = [0]
    for s in sizes:
        offs.append(offs[-1] + s)
    w_z, w_xbc, w_dt, w_q, w_k, w_v, w_o, w_i, w_f = [w_in[:, offs[j]:offs[j + 1]] for j in range(9)]
    w_big = jnp.concatenate([w_z, w_xbc, w_q, w_k * (MLSTM_DK ** -0.5), w_v, w_o], axis=1).astype(BF16)
    zpad = jnp.zeros((D_MODEL, LANES - GATE_END), F32)
    gate_cols = []
    for d in range(2):
        gate_cols += [w_dt[:, d * SSD_HEADS:(d + 1) * SSD_HEADS],
                      w_i[:, d * MLSTM_HEADS:(d + 1) * MLSTM_HEADS],
                      w_f[:, d * MLSTM_HEADS:(d + 1) * MLSTM_HEADS], zpad]
    w_gates = jnp.concatenate(gate_cols, axis=1).astype(BF16)
    gbias = [_row(jnp.concatenate([dt_bias[d], i_bias[d], f_bias[d]]), LANES) for d in range(2)]
    alog = [jnp.pad(jnp.broadcast_to(a_log[d].astype(F32)[:, None], (SSD_HEADS, LANES)),
                    ((0, GATE_END - SSD_HEADS), (0, 0))) for d in range(2)]
    head_of_col = jnp.arange(D_MODEL, dtype=jnp.int32) // SSD_HEAD_DIM
    lane_id = jnp.arange(LANES, dtype=jnp.int32)[:, None]
    expand = jnp.concatenate([lane_id == GATE_END + head_of_col[None, :],
                              lane_id == 2 * GATE_END + head_of_col[None, :]], axis=1).astype(BF16)
    dskip = _row(jnp.repeat(d_skip, SSD_HEAD_DIM))
    conv_w8 = jnp.pad(conv_w.astype(F32), ((0, 8 - CONV_W), (0, 0)))
    meta_tile = jnp.pad(meta_tokens.astype(F32), ((PAD_FRONT, INPROJ_ROWS - CHUNK), (0, 0)))

    z, xbc, q, k, v, o, gates_f, gates_b = _inproj(x_a, x_b, meta_tile, _row(ln_emb_g), _row(ln_emb_b),
                                                   w_big, w_gates)
    xs, bm, cm = _conv(xbc, conv_w8, _row(conv_b), n_chunks)
    yb, hb = _mixer_pass(xs, bm, cm, gates_b, q, k, v, gbias[1], alog[1], expand, n_chunks, reverse=True)
    (ycat,) = _mixer_pass(xs, bm, cm, gates_f, q, k, v, gbias[0], alog[0], expand, n_chunks,
                          reverse=False,
                          final_inputs=(z, o, yb, hb, dskip, _row(ssd_norm_g), _row(mlstm_norm_g)))

    wr = jnp.pad(w_router.astype(F32), ((0, 0), (0, LANES - N_EXPERTS)))
    wrh = wr.astype(BF16)
    wrl = (wr - wrh.astype(F32)).astype(BF16)
    w_out_b = w_out.astype(BF16)
    r_i = jnp.arange(RANK_ROWS, dtype=jnp.int32)
    lstrict = (r_i[None, :] < r_i[:, None]).astype(BF16)
    l_i = jnp.arange(LANES, dtype=jnp.int32)
    ucum = (l_i[:, None] <= l_i[None, :]).astype(BF16)
    ycat2d = ycat.reshape(n_tok, 2 * D_MODEL)
    outs = []
    for x_part, row0 in ((x_a, 0), (x_b, rows_a)):
        n_part = x_part.shape[0] * seq_len
        h1, h1p, sel, gates, expert_counts = _epilogue(x_part.reshape(n_part, D_MODEL), ycat2d, row0, w_out_b,
                                        _row(ln_emb_g), _row(ln_emb_b), _row(ln1_g), _row(ln1_b),
                                        wrh, wrl, _row(b_router, LANES))
        n_blocks = n_part * TOP_K // MOE_BLOCK + N_EXPERTS
        trash_row = n_blocks * MOE_BLOCK
        dest, gk, stats = _rank(sel, gates, expert_counts, lstrict, ucum, trash_row)
        counts = stats[0, :N_EXPERTS].astype(jnp.int32)
        starts = stats[1, :N_EXPERTS].astype(jnp.int32)
        pends = stats[2, :N_EXPERTS].astype(jnp.int32)
        n_used = pends[N_EXPERTS - 1] // MOE_BLOCK
        blk = jnp.minimum(jnp.arange(n_blocks, dtype=jnp.int32), jnp.maximum(n_used - 1, 0))
        blk_exp = jnp.minimum(
            jnp.sum((pends[None, :] <= (blk * MOE_BLOCK)[:, None]).astype(jnp.int32), axis=1),
            N_EXPERTS - 1).astype(jnp.int32)
        blk_last = jnp.take(starts + counts, blk_exp)
        blk_valid = jnp.clip(blk_last - blk * MOE_BLOCK, 0, MOE_BLOCK).astype(jnp.int32)
        dest_km = dest.T.reshape(-1)
        dest_wm = dest.reshape(n_part // SC_WINDOW, SC_WINDOW, TOP_K).transpose(0, 2, 1)
        xb = _sc_scatter_rows(h1p, dest_wm, trash_row + MOE_BLOCK)
        j_i = jnp.arange(n_blocks, dtype=jnp.int32)
        blk_first = jnp.logical_and(j_i < n_used, jnp.logical_or(j_i == 0, blk_exp != jnp.roll(blk_exp, 1)))
        blk_slot = ((jnp.cumsum(blk_first.astype(jnp.int32)) - 1) % 2).astype(jnp.int32)
        run_end = jnp.take(pends, blk_exp) // MOE_BLOCK
        blk_next = jnp.where(run_end < n_used, jnp.take(blk_exp, jnp.minimum(run_end, n_blocks - 1)),
                             -1).astype(jnp.int32)
        yexp = _ffn(blk, blk_exp, blk_valid, blk_first.astype(jnp.int32), blk_next, blk_slot,
                    n_used.reshape(1), xb, w1, b1.reshape(N_EXPERTS, 1, -1), w2,
                    b2.reshape(N_EXPERTS, 1, -1), n_blocks)
        ysel = _sc_gather_rows(yexp, dest_km)
        outs.append(_combine(gk, h1, ysel, _row(ln2_g), _row(ln2_b)))
    return outs


def kernel(x_prompt, x_sample, meta_tokens, ln_emb_g, ln_emb_b, w_in, conv_w, conv_b, dt_bias, a_log,
           d_skip, ssd_norm_g, i_bias, f_bias, mlstm_norm_g, w_out, ln1_g, ln1_b, w_router, b_router,
           w1, b1, w2, b2, ln2_g, ln2_b):
    assert x_prompt.shape[1:] == x_sample.shape[1:]
    n_p, seq_len, d = x_prompt.shape
    n_s = x_sample.shape[0]
    y_p, y_s = _encode_all(x_prompt.astype(F32), x_sample.astype(F32), meta_tokens, ln_emb_g, ln_emb_b, w_in[0], conv_w[0],
                           conv_b[0], dt_bias[0], a_log[0], d_skip[0], ssd_norm_g[0], i_bias[0], f_bias[0],
                           mlstm_norm_g[0], w_out[0], ln1_g[0], ln1_b[0], w_router[0], b_router[0],
                           w1[0], b1[0], w2[0], b2[0], ln2_g[0], ln2_b[0])
    return (y_p.reshape(n_p, seq_len, d), y_s.reshape(n_s, seq_len, d))
```

```python
import functools

import jax
import jax.numpy as jnp
from jax import lax
from jax.experimental import pallas as pl
from jax.experimental.pallas import tpu as pltpu
from jax.experimental.pallas import tpu_sc as plsc

F32 = jnp.float32
BF16 = jnp.bfloat16

D_MODEL = 1024
N_META = 16
CHUNK = 128
PAD_FRONT = CHUNK - N_META
SSD_HEADS = 16
SSD_HEAD_DIM = 64
SSD_GROUPS = 4
SSD_STATE = 128
HEADS_PER_GROUP = SSD_HEADS // SSD_GROUPS
GROUP_WIDTH = HEADS_PER_GROUP * SSD_HEAD_DIM
CONV_W = 5
CONV_HALF = CONV_W // 2
CONV_CH = D_MODEL + 2 * SSD_GROUPS * SSD_STATE
MLSTM_HEADS = 8
MLSTM_DK = 64
MLSTM_DV = 128
N_EXPERTS = 32
TOP_K = 4
D_FF = D_MODEL
SWIGLU_LIMIT = 7.0
SWIGLU_ALPHA = 1.702
DEEPNORM_ALPHA = 2.0 ** 0.25
LN_EPS = 1e-5
RMS_EPS = 1e-5
NEG_GATE = -1e30
LOG2E = 1.4426950408889634

LANES = 128
BF16_SUBLANES = 16
VMEM_LIMIT_BYTES = 56 * 1024 * 1024

GATE_DT0, GATE_I0, GATE_F0, GATE_END = 0, SSD_HEADS, SSD_HEADS + MLSTM_HEADS, SSD_HEADS + 2 * MLSTM_HEADS

INPROJ_ROWS = 512
EPILOGUE_ROWS = 512
RANK_ROWS = 512
COMBINE_ROWS = 512
MOE_BLOCK = 512

SC_CORES = 2
SC_SUBCORES = 16
SC_WORKERS = SC_CORES * SC_SUBCORES
SC_WINDOW = 32
SC_SCATTER_INFLIGHT = 2
SC_GATHER_INFLIGHT = 4


def _dot(a, b):
    return jnp.dot(a, b, preferred_element_type=F32)


def _dot_nt(a, b):
    return lax.dot_general(a, b, (((1,), (1,)), ((), ())), preferred_element_type=F32)


def _split3(x):
    hi = x.astype(BF16)
    r1 = x - hi.astype(F32)
    mid = r1.astype(BF16)
    lo = (r1 - mid.astype(F32)).astype(BF16)
    return hi, mid, lo


def _dot_exact_lhs(a_bf16, x):
    hi, mid, lo = _split3(x)
    return _dot(a_bf16, hi) + _dot(a_bf16, mid) + _dot(a_bf16, lo)


def _dot_exact_rhs(x, b_bf16):
    hi, mid, lo = _split3(x)
    return _dot(hi, b_bf16) + _dot(mid, b_bf16) + _dot(lo, b_bf16)


def _layer_norm(x, g, b):
    mu = jnp.mean(x, axis=-1, keepdims=True)
    xc = x - mu
    var = jnp.mean(xc * xc, axis=-1, keepdims=True)
    return xc * lax.rsqrt(var + LN_EPS) * g + b


def _sigmoid(x):
    return 1.0 / (1.0 + jnp.exp(-x))


def _log1p_exp_neg_abs(x):
    return jnp.log(1.0 + jnp.exp(-jnp.abs(x)))


def _pack_bf16_pairs(x):
    n = x.shape[1] // 2
    bits = lax.bitcast_convert_type(x.astype(BF16).astype(F32), jnp.uint32)
    return (bits[:, :n] >> 16) | bits[:, n:]


def _unpack_bf16_pairs(words):
    lo = lax.bitcast_convert_type(words << 16, F32)
    hi = lax.bitcast_convert_type(words & jnp.uint32(0xFFFF0000), F32)
    return jnp.concatenate([lo, hi], axis=1)


def _storage_chunk(c, n_chunks):
    return jnp.where(c == 0, n_chunks - 1, c - 1)


def _inproj_kernel(xa_ref, xb_ref, meta_ref, g_ref, b_ref, wbig_ref, wg_ref,
                   z_ref, xbc_ref, q_ref, k_ref, v_ref, o_ref, gf_ref, gb_ref, *, n_x_tiles, n_a):
    x = jnp.where(pl.program_id(0) < n_a, xa_ref[...], xb_ref[...])
    x = jnp.where(pl.program_id(1) == n_x_tiles, meta_ref[...], x)
    h = _layer_norm(x, g_ref[...], b_ref[...]).astype(BF16)

    def mm(c0, c1):
        return _dot(h, wbig_ref[:, c0:c1]).astype(BF16)

    z_ref[...] = mm(0, 1024)
    xbc_ref[:, 0:1024] = mm(1024, 2048)
    xbc_ref[:, 1024:2048] = mm(2048, 3072)
    q_ref[...] = mm(3072, 3584)
    k_ref[...] = mm(3584, 4096)
    v_ref[...] = mm(4096, 5120)
    o_ref[...] = mm(5120, 6144)
    gates = _dot(h, wg_ref[...])
    gf_ref[...] = gates[:, :LANES]
    gb_ref[...] = gates[:, LANES:]


def _inproj(x_a, x_b, meta_tile, ln_g, ln_b, w_big, w_gates):
    n_a, seq_len, _ = x_a.shape
    n_seq = n_a + x_b.shape[0]
    tm = INPROJ_ROWS
    assert seq_len % tm == 0 and tm >= CHUNK
    n_x_tiles = seq_len // tm
    rows = seq_len + CHUNK
    row_spec = lambda n: pl.BlockSpec((None, tm, n), lambda b, i: (b, i, 0))
    const = lambda a: pl.BlockSpec(a.shape, lambda b, i: (0,) * a.ndim)
    resident = lambda a: pl.BlockSpec(a.shape, lambda b, i: (0,) * a.ndim, pipeline_mode=pl.Buffered(1))
    widths = (1024, CONV_CH, 512, 512, 1024, 1024)
    out_shapes = [jax.ShapeDtypeStruct((n_seq, rows, w), BF16) for w in widths]
    out_shapes += [jax.ShapeDtypeStruct((n_seq, rows, LANES), F32)] * 2
    return pl.pallas_call(
        functools.partial(_inproj_kernel, n_x_tiles=n_x_tiles, n_a=n_a),
        grid=(n_seq, n_x_tiles + 1),
        in_specs=[pl.BlockSpec((None, tm, D_MODEL), lambda b, i: (
                      jnp.minimum(b, n_a - 1),
                      jnp.where(b < n_a, jnp.minimum(i, n_x_tiles - 1), n_x_tiles - 1), 0)),
                  pl.BlockSpec((None, tm, D_MODEL), lambda b, i: (
                      jnp.maximum(b - n_a, 0),
                      jnp.where(b < n_a, 0, jnp.minimum(i, n_x_tiles - 1)), 0)),
                  resident(meta_tile), const(ln_g), const(ln_b), resident(w_big), resident(w_gates)],
        out_specs=[row_spec(s.shape[2]) for s in out_shapes],
        out_shape=out_shapes,
        compiler_params=pltpu.CompilerParams(
            dimension_semantics=("arbitrary", "arbitrary"), vmem_limit_bytes=VMEM_LIMIT_BYTES),
        name="inproj",
    )(x_a, x_b, meta_tile, ln_g, ln_b, w_big, w_gates)


def _conv_kernel(prev_ref, main_ref, next_ref, shift_ref, w_ref, b_ref, xs_ref, bm_ref, cm_ref,
                 *, n_x_tiles, chunks_per_tile):
    i = pl.program_id(1)
    w = w_ref[...]
    bias = b_ref[...]
    shift = shift_ref[...]

    def conv_chunk(before, rows, after, pad_rows, j):
        shifted = _dot(shift, jnp.concatenate([before, rows, after], axis=0).astype(BF16))
        acc = bias + rows * w[CONV_HALF:CONV_HALF + 1, :]
        for jj, t in enumerate(t for t in range(CONV_W) if t != CONV_HALF):
            acc = acc + shifted[jj * CHUNK:(jj + 1) * CHUNK, :] * w[t:t + 1, :]
        y = acc * _sigmoid(acc)
        if pad_rows is not None:
            y = jnp.where(pad_rows, 0.0, y)
        r = slice(j * CHUNK, (j + 1) * CHUNK)
        xs_ref[r, :] = y[:, :D_MODEL].astype(BF16)
        bm_ref[r, :] = y[:, D_MODEL:D_MODEL + 512].astype(BF16)
        cm_ref[r, :] = y[:, D_MODEL + 512:].astype(BF16)

    @pl.when(i < n_x_tiles)
    def _():
        tile = main_ref[...].astype(F32)
        for j in range(chunks_per_tile):
            lo, hi = j * CHUNK, (j + 1) * CHUNK
            before = prev_ref[...].astype(F32) if j == 0 else tile[lo - BF16_SUBLANES:lo, :]
            if j == chunks_per_tile - 1:
                after = jnp.where(i == n_x_tiles - 1, 0.0, next_ref[...].astype(F32))
            else:
                after = tile[hi:hi + BF16_SUBLANES, :]
            conv_chunk(before, tile[lo:hi, :], after, None, j)

    @pl.when(i == n_x_tiles)
    def _():
        row = lax.broadcasted_iota(jnp.int32, (CHUNK, 1), 0)
        pad_rows = row < PAD_FRONT
        rows = jnp.where(pad_rows, 0.0, main_ref[0:CHUNK, :].astype(F32))
        conv_chunk(jnp.zeros((BF16_SUBLANES, CONV_CH), F32), rows, next_ref[...].astype(F32), pad_rows, 0)


def _conv(xbc, conv_w8, conv_b, n_chunks):
    n_seq, rows, _ = xbc.shape
    tm = INPROJ_ROWS
    seq_len = (n_chunks - 1) * CHUNK
    assert seq_len % tm == 0 and tm % CHUNK == 0
    n_x_tiles = seq_len // tm
    halo_per_tile = tm // BF16_SUBLANES
    meta_last_halo = rows // BF16_SUBLANES - 1

    def prev_map(b, i):
        before_tile = jnp.maximum(jnp.minimum(i, n_x_tiles - 1) * halo_per_tile - 1, 0)
        return (b, jnp.where(i == 0, meta_last_halo, before_tile), 0)

    def next_map(b, i):
        return (b, jnp.where(i >= n_x_tiles - 1, 0, (i + 1) * halo_per_tile) , 0)

    const = lambda a: pl.BlockSpec(a.shape, lambda b, i: (0,) * a.ndim)
    l_i = jnp.arange(CHUNK, dtype=jnp.int32)[:, None]
    j_i = jnp.arange(CHUNK + 2 * BF16_SUBLANES, dtype=jnp.int32)[None, :]
    shifts = jnp.concatenate([(j_i == BF16_SUBLANES + l_i + t - CONV_HALF)
                              for t in range(CONV_W) if t != CONV_HALF], axis=0).astype(BF16)
    out_shapes = [jax.ShapeDtypeStruct((n_seq, rows, D_MODEL), BF16),
                  jax.ShapeDtypeStruct((n_seq, rows, 512), BF16),
                  jax.ShapeDtypeStruct((n_seq, rows, 512), BF16)]
    tile_spec = lambda n: pl.BlockSpec((None, tm, n), lambda b, i: (b, i, 0))
    return pl.pallas_call(
        functools.partial(_conv_kernel, n_x_tiles=n_x_tiles, chunks_per_tile=tm // CHUNK),
        grid=(n_seq, n_x_tiles + 1),
        in_specs=[pl.BlockSpec((None, BF16_SUBLANES, CONV_CH), prev_map),
                  tile_spec(CONV_CH),
                  pl.BlockSpec((None, BF16_SUBLANES, CONV_CH), next_map),
                  const(shifts), const(conv_w8), const(conv_b)],
        out_specs=[tile_spec(s.shape[2]) for s in out_shapes],
        out_shape=out_shapes,
        compiler_params=pltpu.CompilerParams(
            dimension_semantics=("arbitrary", "arbitrary"), vmem_limit_bytes=VMEM_LIMIT_BYTES),
        name="conv",
    )(xbc, xbc, xbc, shifts, conv_w8, conv_b)


def _mixer_kernel(*refs, reverse, final, n_chunks, n_seq):
    if final:
        (xs_ref, bm_ref, cm_ref, g_ref, q_ref, k_ref, v_ref, z_ref, o_ref, yb_ref, hb_ref,
         gbias_ref, alog_ref, expand_ref, dskip_ref, ngs_ref, ngm_ref,
         ycat_ref, s_ref, cst_ref, m_ref) = refs
    else:
        (xs_ref, bm_ref, cm_ref, g_ref, q_ref, k_ref, v_ref,
         gbias_ref, alog_ref, expand_ref,
         yout_ref, hout_ref, s_ref, cst_ref, m_ref) = refs

    t = pl.program_id(0)
    c = (n_chunks - 1 - t) if reverse else t
    end = 0 if reverse else CHUNK - 1

    @pl.when(t == 0)
    def _():
        s_ref[...] = jnp.zeros_like(s_ref)
        cst_ref[...] = jnp.zeros_like(cst_ref)
        m_ref[...] = jnp.zeros_like(m_ref)

    row = lax.broadcasted_iota(jnp.int32, (CHUNK, 1), 0)
    col = lax.broadcasted_iota(jnp.int32, (1, CHUNK), 1)
    lane = col
    allowed = (col >= row) if reverse else (col <= row)
    tri = allowed.astype(BF16)
    tri_t = ((row >= col) if reverse else (row <= col)).astype(BF16)
    feat = lax.broadcasted_iota(jnp.int32, (GATE_END, 1), 0)
    is_dt = feat < GATE_I0
    is_i = jnp.logical_and(feat >= GATE_I0, feat < GATE_F0)
    is_f = feat >= GATE_F0
    pad_cols = jnp.logical_and(c == 0, col < PAD_FRONT)
    a_coef = -jnp.exp(alog_ref[...]) * LOG2E
    expand = expand_ref[...]
    left_half = lane < SSD_HEAD_DIM
    right_half = jnp.logical_not(left_half)
    top_half = row < MLSTM_DK
    ones_blk = jnp.ones((CHUNK, MLSTM_DV), BF16)
    full = (CHUNK, LANES)

    def independent_products(b):
        cgs = [cm_ref[b, :, g * SSD_STATE:(g + 1) * SSD_STATE] for g in range(SSD_GROUPS)]
        bgs = [bm_ref[b, :, g * SSD_STATE:(g + 1) * SSD_STATE] for g in range(SSD_GROUPS)]
        cbs = [_dot_nt(cgs[g], bgs[g]) for g in range(SSD_GROUPS)]
        bg_ts = [bgs[g].astype(F32).T.astype(BF16) for g in range(SSD_GROUPS)]
        q_pairs = [q_ref[b, :, p * LANES:(p + 1) * LANES] for p in range(MLSTM_HEADS // 2)]
        k_pairs = [k_ref[b, :, p * LANES:(p + 1) * LANES] for p in range(MLSTM_HEADS // 2)]
        qks = [_dot_nt(jnp.where(left_half if h % 2 == 0 else right_half, q_pairs[h // 2],
                                 jnp.zeros_like(q_pairs[h // 2])), k_pairs[h // 2])
               for h in range(MLSTM_HEADS)]
        k_pair_ts = [k_pairs[p].astype(F32).T for p in range(MLSTM_HEADS // 2)]
        xs = xs_ref[b]
        xs_rhs = []
        for p in range(SSD_HEADS // 2):
            xs_pair = xs[:, p * LANES:(p + 1) * LANES]
            zero_pair = jnp.zeros_like(xs_pair)
            xs_rhs.append(jnp.concatenate([jnp.where(left_half, xs_pair, zero_pair),
                                           jnp.where(right_half, xs_pair, zero_pair)], axis=0))
        z_gate = o_gate = None
        if final:
            zz = z_ref[b].astype(F32)
            z_gate = zz * _sigmoid(zz)
            o_gate = _sigmoid(o_ref[b].astype(F32))
        return cgs, cbs, bg_ts, q_pairs, qks, k_pair_ts, xs, xs_rhs, z_gate, o_gate

    def one_sequence(b, products):
        cgs, cbs, bg_ts, q_pairs, qks, k_pair_ts, xs, xs_rhs, z_gate, o_gate = products

        gr = (g_ref[b] + gbias_ref[...]).T[0:GATE_END, :]
        lse = _log1p_exp_neg_abs(gr)
        val_t = jnp.where(is_dt, jnp.maximum(gr, 0.0) + lse, jnp.where(is_i, gr, jnp.minimum(gr, 0.0) - lse))
        val_t = jnp.where(pad_cols, jnp.where(is_i, NEG_GATE, 0.0), val_t)
        u_t = jnp.where(is_dt, val_t * a_coef, jnp.where(is_f, val_t * LOG2E, 0.0))
        cums_t = _dot_exact_rhs(u_t, tri_t)
        cums_end = jnp.broadcast_to(cums_t[:, end:end + 1], cums_t.shape)
        p1_t = jnp.exp2(cums_t)
        p2_t = jnp.exp2(cums_end - cums_t) * val_t
        packed = jnp.concatenate([cums_t, p1_t, p2_t, val_t], axis=0).T
        ex = _dot(packed.astype(BF16), expand)
        ex1 = ex[:, :D_MODEL]
        ex2 = ex[:, D_MODEL:]
        chunk_decay = _dot_exact_rhs(jnp.broadcast_to(packed[end:end + 1, :], (8, LANES)),
                                     expand[:, :D_MODEL])[0:1, :]

        xsf = xs.astype(F32)
        xs_w = (xsf * ex2).astype(BF16)
        src_term = jnp.log(val_t[0:SSD_HEADS, :]) * LOG2E - cums_t[0:SSD_HEADS, :]
        for g in range(SSD_GROUPS):
            gs = slice(g * GROUP_WIDTH, (g + 1) * GROUP_WIDTH)
            s_new_all[b].append(chunk_decay[:, gs] * s_old[b][g] + _dot(bg_ts[g], xs_w[:, gs]))
        pair_lhs = []
        for pair in range(SSD_HEADS // 2):
            cb = cbs[pair // (HEADS_PER_GROUP // 2)]
            m_mats = []
            for h in (2 * pair, 2 * pair + 1):
                seg = jnp.broadcast_to(packed[:, h:h + 1], full) + src_term[h:h + 1, :]
                m_mats.append((cb * jnp.exp2(jnp.where(allowed, seg, -jnp.inf))).astype(BF16))
            pair_lhs.append(jnp.concatenate(m_mats, axis=1))
        y_diag = jnp.concatenate([_dot(pair_lhs[p], xs_rhs[p]) for p in range(SSD_HEADS // 2)], axis=1)
        y_off = jnp.concatenate([_dot(cgs[g], s_old[b][g].astype(BF16)) for g in range(SSD_GROUPS)], axis=1)
        y_ssd = y_diag + y_off * ex1

        bcum_t = cums_t[GATE_F0:GATE_END, :]
        ip_t = val_t[GATE_I0:GATE_F0, :] * LOG2E
        rep = bcum_t.shape
        g_rep = jnp.broadcast_to(bcum_t[:, end:end + 1], rep)
        a_t = g_rep - bcum_t + ip_t
        a_max = jnp.broadcast_to(jnp.max(a_t, axis=1, keepdims=True), rep)
        w_t = jnp.exp2(a_t - a_max)
        m_prev = m_old[b]
        m_new = jnp.maximum(g_rep + m_prev, a_max)
        s_prev = jnp.exp2(g_rep + m_prev - m_new)
        s_new = jnp.exp2(a_max - m_new)
        r_t = ip_t - bcum_t
        v_heads = [v_ref[b, :, h * MLSTM_DV:(h + 1) * MLSTM_DV] for h in range(MLSTM_HEADS)]
        for pair in range(MLSTM_HEADS // 2):
            h0, h1 = 2 * pair, 2 * pair + 1
            cst = cst_old[b][pair]
            w_rows = jnp.where(top_half, w_t[h0:h0 + 1, :], w_t[h1:h1 + 1, :])
            kw = (k_pair_ts[pair] * w_rows).astype(BF16)
            full_kv = _dot(kw, jnp.concatenate([v_heads[h0], v_heads[h1], ones_blk], axis=1))
            kvn = jnp.concatenate(
                [jnp.where(top_half, full_kv[:, :MLSTM_DV], full_kv[:, MLSTM_DV:2 * MLSTM_DV]),
                 full_kv[:, 2 * MLSTM_DV:]], axis=1)
            sp_rows = jnp.where(top_half, s_prev[h0:h0 + 1, :], s_prev[h1:h1 + 1, :])
            sn_rows = jnp.where(top_half, s_new[h0:h0 + 1, :], s_new[h1:h1 + 1, :])
            cst_new_all[b].append(jnp.concatenate([sp_rows, sp_rows], axis=1) * cst
                                  + jnp.concatenate([sn_rows, sn_rows], axis=1) * kvn)
        h_heads = []
        for pair in range(MLSTM_HEADS // 2):
            h0, h1 = 2 * pair, 2 * pair + 1
            q_pair_f = q_pairs[pair].astype(F32)
            cst_b = cst_old[b][pair].astype(BF16)
            for hh, h in enumerate((h0, h1)):
                keep = left_half if hh == 0 else right_half
                vh = v_heads[h]
                qk = qks[h]
                bc = jnp.broadcast_to(packed[:, GATE_F0 + h:GATE_F0 + h + 1], full)
                dlog = jnp.where(allowed, bc + r_t[h:h + 1, :], -jnp.inf)
                m_intra = jnp.broadcast_to(jnp.max(dlog, axis=1, keepdims=True), full)
                m_inter = bc + m_prev[h:h + 1, :]
                m_t = jnp.maximum(m_inter, m_intra)
                s_mat = (qk * jnp.exp2(dlog - m_t)).astype(BF16)
                qs = (jnp.where(keep, q_pair_f, 0.0) * jnp.exp2(m_inter - m_t)).astype(BF16)
                tot = _dot(jnp.concatenate([s_mat, qs], axis=1),
                           jnp.concatenate([jnp.concatenate([vh, ones_blk], axis=1), cst_b], axis=0))
                num = tot[:, :MLSTM_DV]
                den = tot[:, MLSTM_DV:]
                h_heads.append(num / jnp.maximum(jnp.abs(den), jnp.exp2(-m_t)))
        m_new_all[b] = m_new
        h_ml = jnp.concatenate(h_heads, axis=1)
        if not final:
            return y_ssd.astype(BF16), h_ml.astype(BF16)

        y_tot = y_ssd + yb_ref[b].astype(F32) + dskip_ref[...] * xsf
        y2 = y_tot * z_gate
        y_n = y2 * lax.rsqrt(jnp.mean(y2 * y2, axis=-1, keepdims=True) + RMS_EPS) * ngs_ref[...]
        h_tot = h_ml + hb_ref[b].astype(F32)
        segs = []
        for h in range(MLSTM_HEADS):
            seg = h_tot[:, h * MLSTM_DV:(h + 1) * MLSTM_DV]
            segs.append(seg * lax.rsqrt(jnp.mean(seg * seg, axis=-1, keepdims=True) + RMS_EPS))
        h_n = jnp.concatenate(segs, axis=1) * ngm_ref[...]
        y_ml = o_gate * h_n
        return y_n.astype(BF16), y_ml.astype(BF16)

    s_old = [[s_ref[b, g] for g in range(SSD_GROUPS)] for b in range(n_seq)]
    cst_old = [[cst_ref[b, p] for p in range(MLSTM_HEADS // 2)] for b in range(n_seq)]
    m_old = [m_ref[b] for b in range(n_seq)]
    s_new_all = [[] for _ in range(n_seq)]
    cst_new_all = [[] for _ in range(n_seq)]
    m_new_all = [None] * n_seq
    results = [one_sequence(b, independent_products(b)) for b in range(n_seq)]
    for b in range(n_seq):
        for g in range(SSD_GROUPS):
            s_ref[b, g] = s_new_all[b][g]
        for p in range(MLSTM_HEADS // 2):
            cst_ref[b, p] = cst_new_all[b][p]
        m_ref[b] = m_new_all[b]

    @pl.when(c > 0)
    def _():
        for b, (first, second) in enumerate(results):
            if final:
                ycat_ref[b, :, :D_MODEL] = first
                ycat_ref[b, :, D_MODEL:] = second
            else:
                yout_ref[b] = first
                hout_ref[b] = second


def _mixer_pass(xs, bm, cm, gates, q, k, v, gbias, alog, expand, n_chunks, *, reverse, final_inputs=None):
    final = final_inputs is not None
    n_seq = xs.shape[0]
    seq_len = (n_chunks - 1) * CHUNK

    def chunk_of(t):
        return (n_chunks - 1 - t) if reverse else t

    def pad_map(t):
        return (0, _storage_chunk(chunk_of(t), n_chunks), 0)

    def out_map(t):
        return (0, jnp.maximum(chunk_of(t) - 1, 0), 0)

    const = lambda a: pl.BlockSpec(a.shape, lambda t: (0,) * a.ndim)
    pad_spec = lambda n: pl.BlockSpec((n_seq, CHUNK, n), pad_map)
    out_spec = lambda n: pl.BlockSpec((n_seq, CHUNK, n), out_map)
    in_arrays = [xs, bm, cm, gates, q, k, v]
    in_specs = [pad_spec(a.shape[2]) for a in in_arrays]
    if final:
        z, o, yb, hb, dskip, ngs, ngm = final_inputs
        in_arrays += [z, o, yb, hb]
        in_specs += [pad_spec(1024), pad_spec(1024), out_spec(1024), out_spec(1024)]
        in_arrays += [gbias, alog, expand, dskip, ngs, ngm]
        in_specs += [const(a) for a in (gbias, alog, expand, dskip, ngs, ngm)]
        out_shape = [jax.ShapeDtypeStruct((n_seq, seq_len, 2 * D_MODEL), BF16)]
        out_specs = [out_spec(2 * D_MODEL)]
    else:
        in_arrays += [gbias, alog, expand]
        in_specs += [const(a) for a in (gbias, alog, expand)]
        out_shape = [jax.ShapeDtypeStruct((n_seq, seq_len, D_MODEL), BF16),
                     jax.ShapeDtypeStruct((n_seq, seq_len, D_MODEL), BF16)]
        out_specs = [out_spec(D_MODEL), out_spec(D_MODEL)]
    return pl.pallas_call(
        functools.partial(_mixer_kernel, reverse=reverse, final=final, n_chunks=n_chunks, n_seq=n_seq),
        grid=(n_chunks,),
        in_specs=in_specs,
        out_specs=out_specs,
        out_shape=out_shape,
        scratch_shapes=[pltpu.VMEM((n_seq, SSD_GROUPS, SSD_STATE, GROUP_WIDTH), F32),
                        pltpu.VMEM((n_seq, MLSTM_HEADS // 2, 2 * MLSTM_DK, 2 * MLSTM_DV), F32),
                        pltpu.VMEM((n_seq, MLSTM_HEADS, LANES), F32)],
        compiler_params=pltpu.CompilerParams(
            dimension_semantics=("arbitrary",), vmem_limit_bytes=VMEM_LIMIT_BYTES),
        name="mixer_fwd" if final else "mixer_bwd",
    )(*in_arrays)


def _epilogue_kernel(x_ref, ycat_ref, wout_ref, lng0_ref, lnb0_ref, lng1_ref, lnb1_ref,
                     wrh_ref, wrl_ref, br_ref, h1_ref, h1p_ref, sel_ref, gate_ref, cnt_ref):
    h0 = _layer_norm(x_ref[...], lng0_ref[...], lnb0_ref[...])
    mix = _dot(ycat_ref[...], wout_ref[...])
    h1 = _layer_norm(DEEPNORM_ALPHA * h0 + mix, lng1_ref[...], lnb1_ref[...])
    h1_ref[...] = h1
    h1p_ref[...] = _pack_bf16_pairs(h1)
    hh = h1.astype(BF16)
    hl = (h1 - hh.astype(F32)).astype(BF16)
    wrh = wrh_ref[...]
    logits = _dot(hh, wrh) + _dot(hl, wrh) + _dot(hh, wrl_ref[...]) + br_ref[...]
    lane = lax.broadcasted_iota(jnp.int32, (1, LANES), 1)
    lane_f = lane.astype(F32)
    logits = jnp.where(lane < N_EXPERTS, logits, -jnp.inf)
    work = logits
    sel = jnp.zeros(logits.shape, jnp.bool_)
    top = None
    for _ in range(TOP_K):
        m = jnp.max(work, axis=-1, keepdims=True)
        if top is None:
            top = m
        first = jnp.min(jnp.where(work == m, lane_f, float(LANES)), axis=-1, keepdims=True)
        pick = lane_f == first
        sel = jnp.logical_or(sel, pick)
        work = jnp.where(pick, -jnp.inf, work)
    e = jnp.where(sel, jnp.exp(logits - top), 0.0)
    gate_ref[...] = e / jnp.sum(e, axis=-1, keepdims=True)
    sel_f = sel.astype(F32)
    sel_ref[...] = sel_f

    @pl.when(pl.program_id(0) == 0)
    def _():
        cnt_ref[...] = jnp.zeros_like(cnt_ref)

    cnt_ref[0:1, :] = cnt_ref[0:1, :] + jnp.sum(sel_f, axis=0, keepdims=True)


def _epilogue(x, ycat, ycat_row0, w_out, lng0, lnb0, lng1, lnb1, wrh, wrl, br):
    rows = x.shape[0]
    tm = EPILOGUE_ROWS
    assert rows % tm == 0 and ycat_row0 % tm == 0
    tile0 = ycat_row0 // tm
    row_spec = lambda n: pl.BlockSpec((tm, n), lambda i: (i, 0))
    const = lambda a: pl.BlockSpec(a.shape, lambda i: (0,) * a.ndim)
    out_shape = [jax.ShapeDtypeStruct((rows, D_MODEL), F32),
                 jax.ShapeDtypeStruct((rows, D_MODEL // 2), jnp.uint32),
                 jax.ShapeDtypeStruct((rows, LANES), F32),
                 jax.ShapeDtypeStruct((rows, LANES), F32),
                 jax.ShapeDtypeStruct((8, LANES), F32)]
    consts = (w_out, lng0, lnb0, lng1, lnb1, wrh, wrl, br)
    return pl.pallas_call(
        _epilogue_kernel,
        grid=(rows // tm,),
        in_specs=[row_spec(D_MODEL), pl.BlockSpec((tm, 2 * D_MODEL), lambda i: (tile0 + i, 0))]
                 + [const(a) for a in consts],
        out_specs=[row_spec(D_MODEL), row_spec(D_MODEL // 2), row_spec(LANES), row_spec(LANES),
                   pl.BlockSpec((8, LANES), lambda i: (0, 0))],
        out_shape=out_shape,
        compiler_params=pltpu.CompilerParams(
            dimension_semantics=("arbitrary",), vmem_limit_bytes=VMEM_LIMIT_BYTES),
        name="epilogue",
    )(x, ycat, *consts)


def _rank_kernel(sel_ref, gate_ref, cnt_ref, lstrict_ref, ucum_ref, dest_ref, gk_ref, stats_ref,
                 base_ref, *, trash_row):
    i = pl.program_id(0)
    sel = sel_ref[...]
    colsum = jnp.sum(sel, axis=0, keepdims=True)

    @pl.when(i == 0)
    def _():
        counts = cnt_ref[0:1, :]
        padded = jnp.ceil(counts / MOE_BLOCK) * MOE_BLOCK
        pend = _dot_exact_rhs(jnp.broadcast_to(padded, (8, LANES)), ucum_ref[...])[0:1, :]
        stats_ref[0:1, :] = counts
        stats_ref[1:2, :] = pend - padded
        stats_ref[2:3, :] = pend
        stats_ref[3:8, :] = jnp.zeros((5, LANES), F32)
        base_ref[...] = jnp.broadcast_to(pend - padded, base_ref.shape)

    before = _dot(lstrict_ref[...], sel.astype(BF16))
    pos = base_ref[0:1, :] + before
    base_ref[0:1, :] = base_ref[0:1, :] + colsum
    work = jnp.where(sel > 0.0, pos + 1.0, 0.0)
    gates = gate_ref[...]
    for kk in range(TOP_K):
        m = jnp.max(work, axis=-1, keepdims=True)
        pick = jnp.logical_and(work == m, m > 0.0)
        gk_ref[:, kk:kk + 1] = jnp.sum(jnp.where(pick, gates, 0.0), axis=-1, keepdims=True)
        dest_ref[:, kk:kk + 1] = jnp.where(m > 0.0, m - 1.0, float(trash_row)).astype(jnp.int32)
        work = jnp.where(pick, 0.0, work)


def _rank(sel, gates, counts, lstrict, ucum, trash_row):
    rows = sel.shape[0]
    tm = RANK_ROWS
    assert rows % tm == 0
    row_spec = lambda n: pl.BlockSpec((tm, n), lambda i: (i, 0))
    const = lambda a: pl.BlockSpec(a.shape, lambda i: (0,) * a.ndim)
    return pl.pallas_call(
        functools.partial(_rank_kernel, trash_row=trash_row),
        grid=(rows // tm,),
        in_specs=[row_spec(LANES), row_spec(LANES), const(counts), const(lstrict), const(ucum)],
        out_specs=[row_spec(TOP_K), row_spec(TOP_K), pl.BlockSpec((8, LANES), lambda i: (0, 0))],
        out_shape=[jax.ShapeDtypeStruct((rows, TOP_K), jnp.int32),
                   jax.ShapeDtypeStruct((rows, TOP_K), F32),
                   jax.ShapeDtypeStruct((8, LANES), F32)],
        scratch_shapes=[pltpu.VMEM((8, LANES), F32)],
        compiler_params=pltpu.CompilerParams(
            dimension_semantics=("arbitrary",), vmem_limit_bytes=VMEM_LIMIT_BYTES),
        name="rank",
    )(sel, gates, counts, lstrict, ucum)


def _sc_mesh():
    return plsc.VectorSubcoreMesh(core_axis_name="c", subcore_axis_name="s",
                                  num_cores=SC_CORES, num_subcores=SC_SUBCORES)


def _sc_scatter_rows(src, idx, n_out_rows):
    n_src, d = src.shape
    w, k = SC_WINDOW, SC_SCATTER_INFLIGHT
    assert idx.shape == (n_src // w, TOP_K, w) and n_src % (w * k * SC_WORKERS) == 0
    per_worker = n_src // (w * SC_WORKERS)
    idx2d = idx.reshape(n_src // w * TOP_K, w)

    @functools.partial(
        pl.kernel, mesh=_sc_mesh(),
        out_type=jax.ShapeDtypeStruct((n_out_rows, d), src.dtype),
        scratch_types=[pltpu.VMEM((k * TOP_K, w), jnp.int32), pltpu.VMEM((k, w, d), src.dtype),
                       pltpu.SemaphoreType.DMA((k,)), pltpu.SemaphoreType.DMA((k,))],
        name="sc_scatter_rows")
    def body(src_hbm, idx_hbm, out_hbm, idx_v, rows_v, sem_load, sem_scatter):
        wid = lax.axis_index("s") * SC_CORES + lax.axis_index("c")

        @pl.loop(0, per_worker // k)
        def _(i):
            win0 = wid * per_worker + i * k
            loads = [pltpu.async_copy(src_hbm.at[pl.ds(pl.multiple_of((win0 + b) * w, w), w)], rows_v.at[b],
                                      sem_load.at[b]) for b in range(k)]
            pltpu.sync_copy(idx_hbm.at[pl.ds(pl.multiple_of(win0 * TOP_K, TOP_K), k * TOP_K)], idx_v)
            scatters = []
            for b in range(k):
                loads[b].wait()
                scatters += [pltpu.async_copy(rows_v.at[b], out_hbm.at[idx_v.at[b * TOP_K + kk]],
                                              sem_scatter.at[b]) for kk in range(TOP_K)]
            for copy in scatters:
                copy.wait()

    return body(src, idx2d)


def _sc_gather_rows(table, idx):
    d = table.shape[1]
    w, k = SC_WINDOW, SC_GATHER_INFLIGHT
    assert idx.shape[0] % (w * k * SC_WORKERS) == 0
    per_worker = idx.shape[0] // (w * SC_WORKERS)
    idx2d = idx.reshape(-1, w)

    @functools.partial(
        pl.kernel, mesh=_sc_mesh(),
        out_type=jax.ShapeDtypeStruct((idx.shape[0], d), table.dtype),
        scratch_types=[pltpu.VMEM((k, w), jnp.int32), pltpu.VMEM((k, w, d), table.dtype),
                       pltpu.SemaphoreType.DMA((k,)), pltpu.SemaphoreType.DMA((k,))],
        name="sc_gather_rows")
    def body(table_hbm, idx_hbm, out_hbm, idx_v, rows_v, sem_gather, sem_store):
        wid = lax.axis_index("s") * SC_CORES + lax.axis_index("c")

        @pl.loop(0, per_worker // k)
        def _(i):
            win0 = wid * per_worker + i * k
            pltpu.sync_copy(idx_hbm.at[pl.ds(pl.multiple_of(win0, k), k)], idx_v)
            gathers = [pltpu.async_copy(table_hbm.at[idx_v.at[b]], rows_v.at[b], sem_gather.at[b])
                       for b in range(k)]
            stores = []
            for b in range(k):
                gathers[b].wait()
                stores.append(pltpu.async_copy(
                    rows_v.at[b], out_hbm.at[pl.ds(pl.multiple_of((win0 + b) * w, w), w)], sem_store.at[b]))
            for copy in stores:
                copy.wait()

    return body(table, idx2d)


def _ffn_kernel(blk_ref, exp_ref, valid_ref, first_ref, next_ref, slot_ref, nused_ref,
                xb_ref, w1_hbm, b1_ref, w2_hbm, b2_ref, yb_ref, w1f_ref, w2f_ref, w1b_ref, w2b_ref, sems):
    j = pl.program_id(0)
    active = j < nused_ref[0]

    def weight_copies(e, slot):
        return (pltpu.make_async_copy(w1_hbm.at[e], w1f_ref.at[slot], sems.at[0, slot]),
                pltpu.make_async_copy(w2_hbm.at[e], w2f_ref.at[slot], sems.at[1, slot]))

    @pl.when(jnp.logical_and(active, first_ref[j] == 1))
    def _():
        e = exp_ref[j]
        slot = slot_ref[j]
        nxt = next_ref[j]

        @pl.when(j == 0)
        def _():
            for copy in weight_copies(e, slot):
                copy.start()

        @pl.when(nxt >= 0)
        def _():
            for copy in weight_copies(nxt, 1 - slot):
                copy.start()

        for copy in weight_copies(e, slot):
            copy.wait()
        w1b_ref[...] = w1f_ref[slot].astype(BF16)
        w2b_ref[...] = w2f_ref[slot].astype(BF16)

    @pl.when(active)
    def _():
        row = lax.broadcasted_iota(jnp.int32, (MOE_BLOCK, 1), 0)
        x = jnp.where(row < valid_ref[j], _unpack_bf16_pairs(xb_ref[...]), 0.0).astype(BF16)
        hc = _dot(x, w1b_ref[...]) + b1_ref[0]
        gate = jnp.minimum(hc[:, :D_FF], SWIGLU_LIMIT)
        up = jnp.clip(hc[:, D_FF:], -SWIGLU_LIMIT, SWIGLU_LIMIT)
        act = (up + 1.0) * gate * _sigmoid(SWIGLU_ALPHA * gate)
        yb_ref[...] = _pack_bf16_pairs(_dot(act.astype(BF16), w2b_ref[...]) + b2_ref[0])


def _ffn(blk_idx, blk_exp, blk_valid, blk_first, blk_next, blk_slot, n_used, xb, w1, b1, w2, b2, n_blocks):
    bm = MOE_BLOCK
    grid_spec = pltpu.PrefetchScalarGridSpec(
        num_scalar_prefetch=7,
        grid=(n_blocks,),
        in_specs=[pl.BlockSpec((bm, D_MODEL // 2), lambda j, bi, be, *_: (bi[j], 0)),
                  pl.BlockSpec(memory_space=pl.ANY),
                  pl.BlockSpec((1, 1, 2 * D_FF), lambda j, bi, be, *_: (be[j], 0, 0)),
                  pl.BlockSpec(memory_space=pl.ANY),
                  pl.BlockSpec((1, 1, D_MODEL), lambda j, bi, be, *_: (be[j], 0, 0))],
        out_specs=pl.BlockSpec((bm, D_MODEL // 2), lambda j, bi, be, *_: (bi[j], 0)),
        scratch_shapes=[pltpu.VMEM((2, D_MODEL, 2 * D_FF), F32), pltpu.VMEM((2, D_FF, D_MODEL), F32),
                        pltpu.VMEM((D_MODEL, 2 * D_FF), BF16), pltpu.VMEM((D_FF, D_MODEL), BF16),
                        pltpu.SemaphoreType.DMA((2, 2))],
    )
    return pl.pallas_call(
        _ffn_kernel,
        grid_spec=grid_spec,
        out_shape=jax.ShapeDtypeStruct(xb.shape, jnp.uint32),
        compiler_params=pltpu.CompilerParams(
            dimension_semantics=("arbitrary",), vmem_limit_bytes=VMEM_LIMIT_BYTES),
        name="expert_ffn",
    )(blk_idx, blk_exp, blk_valid, blk_first, blk_next, blk_slot, n_used, xb, w1, b1, w2, b2)


def _combine_kernel(gk_ref, h1_ref, y0_ref, y1_ref, y2_ref, y3_ref, lng_ref, lnb_ref, out_ref):
    gk = gk_ref[...]
    ffn = gk[:, 0:1] * _unpack_bf16_pairs(y0_ref[...])
    for kk, y_ref in enumerate((y1_ref, y2_ref, y3_ref), start=1):
        ffn = ffn + gk[:, kk:kk + 1] * _unpack_bf16_pairs(y_ref[...])
    out_ref[...] = _layer_norm(DEEPNORM_ALPHA * h1_ref[...] + ffn, lng_ref[...], lnb_ref[...])


def _combine(gk, h1, ysel, lng, lnb):
    rows = h1.shape[0]
    tm = COMBINE_ROWS
    assert rows % tm == 0
    n_tiles = rows // tm
    const = lambda a: pl.BlockSpec(a.shape, lambda i: (0,) * a.ndim)
    ksel = lambda kk: pl.BlockSpec((tm, D_MODEL // 2), lambda i: (kk * n_tiles + i, 0))
    return pl.pallas_call(
        _combine_kernel,
        grid=(n_tiles,),
        in_specs=[pl.BlockSpec((tm, TOP_K), lambda i: (i, 0)),
                  pl.BlockSpec((tm, D_MODEL), lambda i: (i, 0)),
                  ksel(0), ksel(1), ksel(2), ksel(3),
                  const(lng), const(lnb)],
        out_specs=pl.BlockSpec((tm, D_MODEL), lambda i: (i, 0)),
        out_shape=jax.ShapeDtypeStruct((rows, D_MODEL), F32),
        compiler_params=pltpu.CompilerParams(
            dimension_semantics=("arbitrary",), vmem_limit_bytes=VMEM_LIMIT_BYTES),
        name="combine",
    )(gk, h1, ysel, ysel, ysel, ysel, lng, lnb)


def _row(v, width=None):
    v = v.reshape(1, -1).astype(F32)
    if width is not None and v.shape[1] < width:
        v = jnp.pad(v, ((0, 0), (0, width - v.shape[1])))
    return v


def _encode_all(x_a, x_b, meta_tokens, ln_emb_g, ln_emb_b, w_in, conv_w, conv_b, dt_bias, a_log,
                d_skip, ssd_norm_g, i_bias, f_bias, mlstm_norm_g, w_out, ln1_g, ln1_b, w_router, b_router,
                w1, b1, w2, b2, ln2_g, ln2_b):
    n_a, seq_len, _ = x_a.shape
    n_seq = n_a + x_b.shape[0]
    rows_a = n_a * seq_len
    assert seq_len % CHUNK == 0
    n_chunks = seq_len // CHUNK + 1
    n_tok = n_seq * seq_len

    sizes = (1024, CONV_CH, 2 * SSD_HEADS, 512, 512, 1024, 1024, 2 * MLSTM_HEADS, 2 * MLSTM_HEADS)
    offs = [0]
    for s in sizes:
        offs.append(offs[-1] + s)
    w_z, w_xbc, w_dt, w_q, w_k, w_v, w_o, w_i, w_f = [w_in[:, offs[j]:offs[j + 1]] for j in range(9)]
    w_big = jnp.concatenate([w_z, w_xbc, w_q, w_k * (MLSTM_DK ** -0.5), w_v, w_o], axis=1).astype(BF16)
    zpad = jnp.zeros((D_MODEL, LANES - GATE_END), F32)
    gate_cols = []
    for d in range(2):
        gate_cols += [w_dt[:, d * SSD_HEADS:(d + 1) * SSD_HEADS],
                      w_i[:, d * MLSTM_HEADS:(d + 1) * MLSTM_HEADS],
                      w_f[:, d * MLSTM_HEADS:(d + 1) * MLSTM_HEADS], zpad]
    w_gates = jnp.concatenate(gate_cols, axis=1).astype(BF16)
    gbias = [_row(jnp.concatenate([dt_bias[d], i_bias[d], f_bias[d]]), LANES) for d in range(2)]
    alog = [jnp.pad(jnp.broadcast_to(a_log[d].astype(F32)[:, None], (SSD_HEADS, LANES)),
                    ((0, GATE_END - SSD_HEADS), (0, 0))) for d in range(2)]
    head_of_col = jnp.arange(D_MODEL, dtype=jnp.int32) // SSD_HEAD_DIM
    lane_id = jnp.arange(LANES, dtype=jnp.int32)[:, None]
    expand = jnp.concatenate([lane_id == GATE_END + head_of_col[None, :],
                              lane_id == 2 * GATE_END + head_of_col[None, :]], axis=1).astype(BF16)
    dskip = _row(jnp.repeat(d_skip, SSD_HEAD_DIM))
    conv_w8 = jnp.pad(conv_w.astype(F32), ((0, 8 - CONV_W), (0, 0)))
    meta_tile = jnp.pad(meta_tokens.astype(F32), ((PAD_FRONT, INPROJ_ROWS - CHUNK), (0, 0)))

    z, xbc, q, k, v, o, gates_f, gates_b = _inproj(x_a, x_b, meta_tile, _row(ln_emb_g), _row(ln_emb_b),
                                                   w_big, w_gates)
    xs, bm, cm = _conv(xbc, conv_w8, _row(conv_b), n_chunks)
    yb, hb = _mixer_pass(xs, bm, cm, gates_b, q, k, v, gbias[1], alog[1], expand, n_chunks, reverse=True)
    (ycat,) = _mixer_pass(xs, bm, cm, gates_f, q, k, v, gbias[0], alog[0], expand, n_chunks,
                          reverse=False,
                          final_inputs=(z, o, yb, hb, dskip, _row(ssd_norm_g), _row(mlstm_norm_g)))

    wr = jnp.pad(w_router.astype(F32), ((0, 0), (0, LANES - N_EXPERTS)))
    wrh = wr.astype(BF16)
    wrl = (wr - wrh.astype(F32)).astype(BF16)
    w_out_b = w_out.astype(BF16)
    r_i = jnp.arange(RANK_ROWS, dtype=jnp.int32)
    lstrict = (r_i[None, :] < r_i[:, None]).astype(BF16)
    l_i = jnp.arange(LANES, dtype=jnp.int32)
    ucum = (l_i[:, None] <= l_i[None, :]).astype(BF16)
    ycat2d = ycat.reshape(n_tok, 2 * D_MODEL)
    outs = []
    for x_part, row0 in ((x_a, 0), (x_b, rows_a)):
        n_part = x_part.shape[0] * seq_len
        h1, h1p, sel, gates, expert_counts = _epilogue(x_part.reshape(n_part, D_MODEL), ycat2d, row0, w_out_b,
                                        _row(ln_emb_g), _row(ln_emb_b), _row(ln1_g), _row(ln1_b),
                                        wrh, wrl, _row(b_router, LANES))
        n_blocks = n_part * TOP_K // MOE_BLOCK + N_EXPERTS
        trash_row = n_blocks * MOE_BLOCK
        dest, gk, stats = _rank(sel, gates, expert_counts, lstrict, ucum, trash_row)
        counts = stats[0, :N_EXPERTS].astype(jnp.int32)
        starts = stats[1, :N_EXPERTS].astype(jnp.int32)
        pends = stats[2, :N_EXPERTS].astype(jnp.int32)
        n_used = pends[N_EXPERTS - 1] // MOE_BLOCK
        blk = jnp.minimum(jnp.arange(n_blocks, dtype=jnp.int32), jnp.maximum(n_used - 1, 0))
        blk_exp = jnp.minimum(
            jnp.sum((pends[None, :] <= (blk * MOE_BLOCK)[:, None]).astype(jnp.int32), axis=1),
            N_EXPERTS - 1).astype(jnp.int32)
        blk_last = jnp.take(starts + counts, blk_exp)
        blk_valid = jnp.clip(blk_last - blk * MOE_BLOCK, 0, MOE_BLOCK).astype(jnp.int32)
        dest_km = dest.T.reshape(-1)
        dest_wm = dest.reshape(n_part // SC_WINDOW, SC_WINDOW, TOP_K).transpose(0, 2, 1)
        xb = _sc_scatter_rows(h1p, dest_wm, trash_row + MOE_BLOCK)
        j_i = jnp.arange(n_blocks, dtype=jnp.int32)
        blk_first = jnp.logical_and(j_i < n_used, jnp.logical_or(j_i == 0, blk_exp != jnp.roll(blk_exp, 1)))
        blk_slot = ((jnp.cumsum(blk_first.astype(jnp.int32)) - 1) % 2).astype(jnp.int32)
        run_end = jnp.take(pends, blk_exp) // MOE_BLOCK
        blk_next = jnp.where(run_end < n_used, jnp.take(blk_exp, jnp.minimum(run_end, n_blocks - 1)),
                             -1).astype(jnp.int32)
        yexp = _ffn(blk, blk_exp, blk_valid, blk_first.astype(jnp.int32), blk_next, blk_slot,
                    n_used.reshape(1), xb, w1, b1.reshape(N_EXPERTS, 1, -1), w2,
                    b2.reshape(N_EXPERTS, 1, -1), n_blocks)
        ysel = _sc_gather_rows(yexp, dest_km)
        outs.append(_combine(gk, h1, ysel, _row(ln2_g), _row(ln2_b)))
    return outs


def kernel(x_prompt, x_sample, meta_tokens, ln_emb_g, ln_emb_b, w_in, conv_w, conv_b, dt_bias, a_log,
           d_skip, ssd_norm_g, i_bias, f_bias, mlstm_norm_g, w_out, ln1_g, ln1_b, w_router, b_router,
           w1, b1, w2, b2, ln2_g, ln2_b):
    assert x_prompt.shape[1:] == x_sample.shape[1:]
    n_p, seq_len, d = x_prompt.shape
    n_s = x_sample.shape[0]
    y_p, y_s = _encode_all(x_prompt.astype(F32), x_sample.astype(F32), meta_tokens, ln_emb_g, ln_emb_b, w_in[0], conv_w[0],
                           conv_b[0], dt_bias[0], a_log[0], d_skip[0], ssd_norm_g[0], i_bias[0], f_bias[0],
                           mlstm_norm_g[0], w_out[0], ln1_g[0], ln1_b[0], w_router[0], b_router[0],
                           w1[0], b1[0], w2[0], b2[0], ln2_g[0], ln2_b[0])
    return (y_p.reshape(n_p, seq_len, d), y_s.reshape(n_s, seq_len, d))
```

```python
import functools

import jax
import jax.numpy as jnp
from jax import lax
from jax.experimental import pallas as pl
from jax.experimental.pallas import tpu as pltpu
from jax.experimental.pallas import tpu_sc as plsc

F32 = jnp.float32
BF16 = jnp.bfloat16

D_MODEL = 1024
N_META = 16
CHUNK = 128
PAD_FRONT = CHUNK - N_META
SSD_HEADS = 16
SSD_HEAD_DIM = 64
SSD_GROUPS = 4
SSD_STATE = 128
HEADS_PER_GROUP = SSD_HEADS // SSD_GROUPS
GROUP_WIDTH = HEADS_PER_GROUP * SSD_HEAD_DIM
CONV_W = 5
CONV_HALF = CONV_W // 2
CONV_CH = D_MODEL + 2 * SSD_GROUPS * SSD_STATE
MLSTM_HEADS = 8
MLSTM_DK = 64
MLSTM_DV = 128
N_EXPERTS = 32
TOP_K = 4
D_FF = D_MODEL
SWIGLU_LIMIT = 7.0
SWIGLU_ALPHA = 1.702
DEEPNORM_ALPHA = 2.0 ** 0.25
LN_EPS = 1e-5
RMS_EPS = 1e-5
NEG_GATE = -1e30
LOG2E = 1.4426950408889634

LANES = 128
BF16_SUBLANES = 16
VMEM_LIMIT_BYTES = 56 * 1024 * 1024

GATE_DT0, GATE_I0, GATE_F0, GATE_END = 0, SSD_HEADS, SSD_HEADS + MLSTM_HEADS, SSD_HEADS + 2 * MLSTM_HEADS

INPROJ_ROWS = 512
EPILOGUE_ROWS = 512
RANK_ROWS = 512
COMBINE_ROWS = 512
MOE_BLOCK = 512

SC_CORES = 2
SC_SUBCORES = 16
SC_WORKERS = SC_CORES * SC_SUBCORES
SC_WINDOW = 32
SC_SCATTER_INFLIGHT = 2
SC_GATHER_INFLIGHT = 4


def _dot(a, b):
    return jnp.dot(a, b, preferred_element_type=F32)


def _dot_nt(a, b):
    return lax.dot_general(a, b, (((1,), (1,)), ((), ())), preferred_element_type=F32)


def _split3(x):
    hi = x.astype(BF16)
    r1 = x - hi.astype(F32)
    mid = r1.astype(BF16)
    lo = (r1 - mid.astype(F32)).astype(BF16)
    return hi, mid, lo


def _dot_exact_lhs(a_bf16, x):
    hi, mid, lo = _split3(x)
    return _dot(a_bf16, hi) + _dot(a_bf16, mid) + _dot(a_bf16, lo)


def _dot_exact_rhs(x, b_bf16):
    hi, mid, lo = _split3(x)
    return _dot(hi, b_bf16) + _dot(mid, b_bf16) + _dot(lo, b_bf16)


def _layer_norm(x, g, b):
    mu = jnp.mean(x, axis=-1, keepdims=True)
    xc = x - mu
    var = jnp.mean(xc * xc, axis=-1, keepdims=True)
    return xc * lax.rsqrt(var + LN_EPS) * g + b


def _sigmoid(x):
    return 1.0 / (1.0 + jnp.exp(-x))


def _log1p_exp_neg_abs(x):
    return jnp.log(1.0 + jnp.exp(-jnp.abs(x)))


def _pack_bf16_pairs(x):
    n = x.shape[1] // 2
    bits = lax.bitcast_convert_type(x.astype(BF16).astype(F32), jnp.uint32)
    return (bits[:, :n] >> 16) | bits[:, n:]


def _unpack_bf16_pairs(words):
    lo = lax.bitcast_convert_type(words << 16, F32)
    hi = lax.bitcast_convert_type(words & jnp.uint32(0xFFFF0000), F32)
    return jnp.concatenate([lo, hi], axis=1)


def _storage_chunk(c, n_chunks):
    return jnp.where(c == 0, n_chunks - 1, c - 1)


def _inproj_kernel(xa_ref, xb_ref, meta_ref, g_ref, b_ref, wbig_ref, wg_ref,
                   z_ref, xbc_ref, q_ref, k_ref, v_ref, o_ref, gf_ref, gb_ref, *, n_x_tiles, n_a):
    x = jnp.where(pl.program_id(0) < n_a, xa_ref[...], xb_ref[...])
    x = jnp.where(pl.program_id(1) == n_x_tiles, meta_ref[...], x)
    h = _layer_norm(x, g_ref[...], b_ref[...]).astype(BF16)

    def mm(c0, c1):
        return _dot(h, wbig_ref[:, c0:c1]).astype(BF16)

    z_ref[...] = mm(0, 1024)
    xbc_ref[:, 0:1024] = mm(1024, 2048)
    xbc_ref[:, 1024:2048] = mm(2048, 3072)
    q_ref[...] = mm(3072, 3584)
    k_ref[...] = mm(3584, 4096)
    v_ref[...] = mm(4096, 5120)
    o_ref[...] = mm(5120, 6144)
    gates = _dot(h, wg_ref[...])
    gf_ref[...] = gates[:, :LANES]
    gb_ref[...] = gates[:, LANES:]


def _inproj(x_a, x_b, meta_tile, ln_g, ln_b, w_big, w_gates):
    n_a, seq_len, _ = x_a.shape
    n_seq = n_a + x_b.shape[0]
    tm = INPROJ_ROWS
    assert seq_len % tm == 0 and tm >= CHUNK
    n_x_tiles = seq_len // tm
    rows = seq_len + CHUNK
    row_spec = lambda n: pl.BlockSpec((None, tm, n), lambda b, i: (b, i, 0))
    const = lambda a: pl.BlockSpec(a.shape, lambda b, i: (0,) * a.ndim)
    resident = lambda a: pl.BlockSpec(a.shape, lambda b, i: (0,) * a.ndim, pipeline_mode=pl.Buffered(1))
    widths = (1024, CONV_CH, 512, 512, 1024, 1024)
    out_shapes = [jax.ShapeDtypeStruct((n_seq, rows, w), BF16) for w in widths]
    out_shapes += [jax.ShapeDtypeStruct((n_seq, rows, LANES), F32)] * 2
    return pl.pallas_call(
        functools.partial(_inproj_kernel, n_x_tiles=n_x_tiles, n_a=n_a),
        grid=(n_seq, n_x_tiles + 1),
        in_specs=[pl.BlockSpec((None, tm, D_MODEL), lambda b, i: (
                      jnp.minimum(b, n_a - 1),
                      jnp.where(b < n_a, jnp.minimum(i, n_x_tiles - 1), n_x_tiles - 1), 0)),
                  pl.BlockSpec((None, tm, D_MODEL), lambda b, i: (
                      jnp.maximum(b - n_a, 0),
                      jnp.where(b < n_a, 0, jnp.minimum(i, n_x_tiles - 1)), 0)),
                  resident(meta_tile), const(ln_g), const(ln_b), resident(w_big), resident(w_gates)],
        out_specs=[row_spec(s.shape[2]) for s in out_shapes],
        out_shape=out_shapes,
        compiler_params=pltpu.CompilerParams(
            dimension_semantics=("arbitrary", "arbitrary"), vmem_limit_bytes=VMEM_LIMIT_BYTES),
        name="inproj",
    )(x_a, x_b, meta_tile, ln_g, ln_b, w_big, w_gates)


def _conv_kernel(prev_ref, main_ref, next_ref, shift_ref, w_ref, b_ref, xs_ref, bm_ref, cm_ref,
                 *, n_x_tiles, chunks_per_tile):
    i = pl.program_id(1)
    w = w_ref[...]
    bias = b_ref[...]
    shift = shift_ref[...]

    def conv_chunk(before, rows, after, pad_rows, j):
        shifted = _dot(shift, jnp.concatenate([before, rows, after], axis=0).astype(BF16))
        acc = bias + rows * w[CONV_HALF:CONV_HALF + 1, :]
        for jj, t in enumerate(t for t in range(CONV_W) if t != CONV_HALF):
            acc = acc + shifted[jj * CHUNK:(jj + 1) * CHUNK, :] * w[t:t + 1, :]
        y = acc * _sigmoid(acc)
        if pad_rows is not None:
            y = jnp.where(pad_rows, 0.0, y)
        r = slice(j * CHUNK, (j + 1) * CHUNK)
        xs_ref[r, :] = y[:, :D_MODEL].astype(BF16)
        bm_ref[r, :] = y[:, D_MODEL:D_MODEL + 512].astype(BF16)
        cm_ref[r, :] = y[:, D_MODEL + 512:].astype(BF16)

    @pl.when(i < n_x_tiles)
    def _():
        tile = main_ref[...].astype(F32)
        for j in range(chunks_per_tile):
            lo, hi = j * CHUNK, (j + 1) * CHUNK
            before = prev_ref[...].astype(F32) if j == 0 else tile[lo - BF16_SUBLANES:lo, :]
            if j == chunks_per_tile - 1:
                after = jnp.where(i == n_x_tiles - 1, 0.0, next_ref[...].astype(F32))
            else:
                after = tile[hi:hi + BF16_SUBLANES, :]
            conv_chunk(before, tile[lo:hi, :], after, None, j)

    @pl.when(i == n_x_tiles)
    def _():
        row = lax.broadcasted_iota(jnp.int32, (CHUNK, 1), 0)
        pad_rows = row < PAD_FRONT
        rows = jnp.where(pad_rows, 0.0, main_ref[0:CHUNK, :].astype(F32))
        conv_chunk(jnp.zeros((BF16_SUBLANES, CONV_CH), F32), rows, next_ref[...].astype(F32), pad_rows, 0)


def _conv(xbc, conv_w8, conv_b, n_chunks):
    n_seq, rows, _ = xbc.shape
    tm = INPROJ_ROWS
    seq_len = (n_chunks - 1) * CHUNK
    assert seq_len % tm == 0 and tm % CHUNK == 0
    n_x_tiles = seq_len // tm
    halo_per_tile = tm // BF16_SUBLANES
    meta_last_halo = rows // BF16_SUBLANES - 1

    def prev_map(b, i):
        before_tile = jnp.maximum(jnp.minimum(i, n_x_tiles - 1) * halo_per_tile - 1, 0)
        return (b, jnp.where(i == 0, meta_last_halo, before_tile), 0)

    def next_map(b, i):
        return (b, jnp.where(i >= n_x_tiles - 1, 0, (i + 1) * halo_per_tile) , 0)

    const = lambda a: pl.BlockSpec(a.shape, lambda b, i: (0,) * a.ndim)
    l_i = jnp.arange(CHUNK, dtype=jnp.int32)[:, None]
    j_i = jnp.arange(CHUNK + 2 * BF16_SUBLANES, dtype=jnp.int32)[None, :]
    shifts = jnp.concatenate([(j_i == BF16_SUBLANES + l_i + t - CONV_HALF)
                              for t in range(CONV_W) if t != CONV_HALF], axis=0).astype(BF16)
    out_shapes = [jax.ShapeDtypeStruct((n_seq, rows, D_MODEL), BF16),
                  jax.ShapeDtypeStruct((n_seq, rows, 512), BF16),
                  jax.ShapeDtypeStruct((n_seq, rows, 512), BF16)]
    tile_spec = lambda n: pl.BlockSpec((None, tm, n), lambda b, i: (b, i, 0))
    return pl.pallas_call(
        functools.partial(_conv_kernel, n_x_tiles=n_x_tiles, chunks_per_tile=tm // CHUNK),
        grid=(n_seq, n_x_tiles + 1),
        in_specs=[pl.BlockSpec((None, BF16_SUBLANES, CONV_CH), prev_map),
                  tile_spec(CONV_CH),
                  pl.BlockSpec((None, BF16_SUBLANES, CONV_CH), next_map),
                  const(shifts), const(conv_w8), const(conv_b)],
        out_specs=[tile_spec(s.shape[2]) for s in out_shapes],
        out_shape=out_shapes,
        compiler_params=pltpu.CompilerParams(
            dimension_semantics=("arbitrary", "arbitrary"), vmem_limit_bytes=VMEM_LIMIT_BYTES),
        name="conv",
    )(xbc, xbc, xbc, shifts, conv_w8, conv_b)


def _mixer_kernel(*refs, reverse, final, n_chunks, n_seq):
    if final:
        (xs_ref, bm_ref, cm_ref, g_ref, q_ref, k_ref, v_ref, z_ref, o_ref, yb_ref, hb_ref,
         gbias_ref, alog_ref, expand_ref, dskip_ref, ngs_ref, ngm_ref,
         ycat_ref, s_ref, cst_ref, m_ref) = refs
    else:
        (xs_ref, bm_ref, cm_ref, g_ref, q_ref, k_ref, v_ref,
         gbias_ref, alog_ref, expand_ref,
         yout_ref, hout_ref, s_ref, cst_ref, m_ref) = refs

    t = pl.program_id(0)
    c = (n_chunks - 1 - t) if reverse else t
    end = 0 if reverse else CHUNK - 1

    @pl.when(t == 0)
    def _():
        s_ref[...] = jnp.zeros_like(s_ref)
        cst_ref[...] = jnp.zeros_like(cst_ref)
        m_ref[...] = jnp.zeros_like(m_ref)

    row = lax.broadcasted_iota(jnp.int32, (CHUNK, 1), 0)
    col = lax.broadcasted_iota(jnp.int32, (1, CHUNK), 1)
    lane = col
    allowed = (col >= row) if reverse else (col <= row)
    tri = allowed.astype(BF16)
    tri_t = ((row >= col) if reverse else (row <= col)).astype(BF16)
    feat = lax.broadcasted_iota(jnp.int32, (GATE_END, 1), 0)
    is_dt = feat < GATE_I0
    is_i = jnp.logical_and(feat >= GATE_I0, feat < GATE_F0)
    is_f = feat >= GATE_F0
    pad_cols = jnp.logical_and(c == 0, col < PAD_FRONT)
    a_coef = -jnp.exp(alog_ref[...]) * LOG2E
    expand = expand_ref[...]
    left_half = lane < SSD_HEAD_DIM
    right_half = jnp.logical_not(left_half)
    top_half = row < MLSTM_DK
    ones_blk = jnp.ones((CHUNK, MLSTM_DV), BF16)
    full = (CHUNK, LANES)

    def independent_products(b):
        cgs = [cm_ref[b, :, g * SSD_STATE:(g + 1) * SSD_STATE] for g in range(SSD_GROUPS)]
        bgs = [bm_ref[b, :, g * SSD_STATE:(g + 1) * SSD_STATE] for g in range(SSD_GROUPS)]
        cbs = [_dot_nt(cgs[g], bgs[g]) for g in range(SSD_GROUPS)]
        bg_ts = [bgs[g].astype(F32).T.astype(BF16) for g in range(SSD_GROUPS)]
        q_pairs = [q_ref[b, :, p * LANES:(p + 1) * LANES] for p in range(MLSTM_HEADS // 2)]
        k_pairs = [k_ref[b, :, p * LANES:(p + 1) * LANES] for p in range(MLSTM_HEADS // 2)]
        qks = [_dot_nt(jnp.where(left_half if h % 2 == 0 else right_half, q_pairs[h // 2],
                                 jnp.zeros_like(q_pairs[h // 2])), k_pairs[h // 2])
               for h in range(MLSTM_HEADS)]
        k_pair_ts = [k_pairs[p].astype(F32).T for p in range(MLSTM_HEADS // 2)]
        xs = xs_ref[b]
        xs_rhs = []
        for p in range(SSD_HEADS // 2):
            xs_pair = xs[:, p * LANES:(p + 1) * LANES]
            zero_pair = jnp.zeros_like(xs_pair)
            xs_rhs.append(jnp.concatenate([jnp.where(left_half, xs_pair, zero_pair),
                                           jnp.where(right_half, xs_pair, zero_pair)], axis=0))
        z_gate = o_gate = None
        if final:
            zz = z_ref[b].astype(F32)
            z_gate = zz * _sigmoid(zz)
            o_gate = _sigmoid(o_ref[b].astype(F32))
        return cgs, cbs, bg_ts, q_pairs, qks, k_pair_ts, xs, xs_rhs, z_gate, o_gate

    def one_sequence(b, products):
        cgs, cbs, bg_ts, q_pairs, qks, k_pair_ts, xs, xs_rhs, z_gate, o_gate = products

        gr = (g_ref[b] + gbias_ref[...]).T[0:GATE_END, :]
        lse = _log1p_exp_neg_abs(gr)
        val_t = jnp.where(is_dt, jnp.maximum(gr, 0.0) + lse, jnp.where(is_i, gr, jnp.minimum(gr, 0.0) - lse))
        val_t = jnp.where(pad_cols, jnp.where(is_i, NEG_GATE, 0.0), val_t)
        u_t = jnp.where(is_dt, val_t * a_coef, jnp.where(is_f, val_t * LOG2E, 0.0))
        cums_t = _dot_exact_rhs(u_t, tri_t)
        cums_end = jnp.broadcast_to(cums_t[:, end:end + 1], cums_t.shape)
        p1_t = jnp.exp2(cums_t)
        p2_t = jnp.exp2(cums_end - cums_t) * val_t
        packed = jnp.concatenate([cums_t, p1_t, p2_t, val_t], axis=0).T
        ex = _dot(packed.astype(BF16), expand)
        ex1 = ex[:, :D_MODEL]
        ex2 = ex[:, D_MODEL:]
        chunk_decay = _dot_exact_rhs(jnp.broadcast_to(packed[end:end + 1, :], (8, LANES)),
                                     expand[:, :D_MODEL])[0:1, :]

        xsf = xs.astype(F32)
        xs_w = (xsf * ex2).astype(BF16)
        src_term = jnp.log(val_t[0:SSD_HEADS, :]) * LOG2E - cums_t[0:SSD_HEADS, :]
        for g in range(SSD_GROUPS):
            gs = slice(g * GROUP_WIDTH, (g + 1) * GROUP_WIDTH)
            s_new_all[b].append(chunk_decay[:, gs] * s_old[b][g] + _dot(bg_ts[g], xs_w[:, gs]))
        pair_lhs = []
        for pair in range(SSD_HEADS // 2):
            cb = cbs[pair // (HEADS_PER_GROUP // 2)]
            m_mats = []
            for h in (2 * pair, 2 * pair + 1):
                seg = jnp.broadcast_to(packed[:, h:h + 1], full) + src_term[h:h + 1, :]
                m_mats.append((cb * jnp.exp2(jnp.where(allowed, seg, -jnp.inf))).astype(BF16))
            pair_lhs.append(jnp.concatenate(m_mats, axis=1))
        y_diag = jnp.concatenate([_dot(pair_lhs[p], xs_rhs[p]) for p in range(SSD_HEADS // 2)], axis=1)
        y_off = jnp.concatenate([_dot(cgs[g], s_old[b][g].astype(BF16)) for g in range(SSD_GROUPS)], axis=1)
        y_ssd = y_diag + y_off * ex1

        bcum_t = cums_t[GATE_F0:GATE_END, :]
        ip_t = val_t[GATE_I0:GATE_F0, :] * LOG2E
        rep = bcum_t.shape
        g_rep = jnp.broadcast_to(bcum_t[:, end:end + 1], rep)
        a_t = g_rep - bcum_t + ip_t
        a_max = jnp.broadcast_to(jnp.max(a_t, axis=1, keepdims=True), rep)
        w_t = jnp.exp2(a_t - a_max)
        m_prev = m_old[b]
        m_new = jnp.maximum(g_rep + m_prev, a_max)
        s_prev = jnp.exp2(g_rep + m_prev - m_new)
        s_new = jnp.exp2(a_max - m_new)
        r_t = ip_t - bcum_t
        v_heads = [v_ref[b, :, h * MLSTM_DV:(h + 1) * MLSTM_DV] for h in range(MLSTM_HEADS)]
        for pair in range(MLSTM_HEADS // 2):
            h0, h1 = 2 * pair, 2 * pair + 1
            cst = cst_old[b][pair]
            w_rows = jnp.where(top_half, w_t[h0:h0 + 1, :], w_t[h1:h1 + 1, :])
            kw = (k_pair_ts[pair] * w_rows).astype(BF16)
            full_kv = _dot(kw, jnp.concatenate([v_heads[h0], v_heads[h1], ones_blk], axis=1))
            kvn = jnp.concatenate(
                [jnp.where(top_half, full_kv[:, :MLSTM_DV], full_kv[:, MLSTM_DV:2 * MLSTM_DV]),
                 full_kv[:, 2 * MLSTM_DV:]], axis=1)
            sp_rows = jnp.where(top_half, s_prev[h0:h0 + 1, :], s_prev[h1:h1 + 1, :])
            sn_rows = jnp.where(top_half, s_new[h0:h0 + 1, :], s_new[h1:h1 + 1, :])
            cst_new_all[b].append(jnp.concatenate([sp_rows, sp_rows], axis=1) * cst
                                  + jnp.concatenate([sn_rows, sn_rows], axis=1) * kvn)
        h_heads = []
        for pair in range(MLSTM_HEADS // 2):
            h0, h1 = 2 * pair, 2 * pair + 1
            q_pair_f = q_pairs[pair].astype(F32)
            cst_b = cst_old[b][pair].astype(BF16)
            for hh, h in enumerate((h0, h1)):
                keep = left_half if hh == 0 else right_half
                vh = v_heads[h]
                qk = qks[h]
                bc = jnp.broadcast_to(packed[:, GATE_F0 + h:GATE_F0 + h + 1], full)
                dlog = jnp.where(allowed, bc + r_t[h:h + 1, :], -jnp.inf)
                m_intra = jnp.broadcast_to(jnp.max(dlog, axis=1, keepdims=True), full)
                m_inter = bc + m_prev[h:h + 1, :]
                m_t = jnp.maximum(m_inter, m_intra)
                s_mat = (qk * jnp.exp2(dlog - m_t)).astype(BF16)
                qs = (jnp.where(keep, q_pair_f, 0.0) * jnp.exp2(m_inter - m_t)).astype(BF16)
                tot = _dot(jnp.concatenate([s_mat, qs], axis=1),
                           jnp.concatenate([jnp.concatenate([vh, ones_blk], axis=1), cst_b], axis=0))
                num = tot[:, :MLSTM_DV]
                den = tot[:, MLSTM_DV:]
                h_heads.append(num / jnp.maximum(jnp.abs(den), jnp.exp2(-m_t)))
        m_new_all[b] = m_new
        h_ml = jnp.concatenate(h_heads, axis=1)
        if not final:
            return y_ssd.astype(BF16), h_ml.astype(BF16)

        y_tot = y_ssd + yb_ref[b].astype(F32) + dskip_ref[...] * xsf
        y2 = y_tot * z_gate
        y_n = y2 * lax.rsqrt(jnp.mean(y2 * y2, axis=-1, keepdims=True) + RMS_EPS) * ngs_ref[...]
        h_tot = h_ml + hb_ref[b].astype(F32)
        segs = []
        for h in range(MLSTM_HEADS):
            seg = h_tot[:, h * MLSTM_DV:(h + 1) * MLSTM_DV]
            segs.append(seg * lax.rsqrt(jnp.mean(seg * seg, axis=-1, keepdims=True) + RMS_EPS))
        h_n = jnp.concatenate(segs, axis=1) * ngm_ref[...]
        y_ml = o_gate * h_n
        return y_n.astype(BF16), y_ml.astype(BF16)

    s_old = [[s_ref[b, g] for g in range(SSD_GROUPS)] for b in range(n_seq)]
    cst_old = [[cst_ref[b, p] for p in range(MLSTM_HEADS // 2)] for b in range(n_seq)]
    m_old = [m_ref[b] for b in range(n_seq)]
    s_new_all = [[] for _ in range(n_seq)]
    cst_new_all = [[] for _ in range(n_seq)]
    m_new_all = [None] * n_seq
    results = [one_sequence(b, independent_products(b)) for b in range(n_seq)]
    for b in range(n_seq):
        for g in range(SSD_GROUPS):
            s_ref[b, g] = s_new_all[b][g]
        for p in range(MLSTM_HEADS // 2):
            cst_ref[b, p] = cst_new_all[b][p]
        m_ref[b] = m_new_all[b]

    @pl.when(c > 0)
    def _():
        for b, (first, second) in enumerate(results):
            if final:
                ycat_ref[b, :, :D_MODEL] = first
                ycat_ref[b, :, D_MODEL:] = second
            else:
                yout_ref[b] = first
                hout_ref[b] = second


def _mixer_pass(xs, bm, cm, gates, q, k, v, gbias, alog, expand, n_chunks, *, reverse, final_inputs=None):
    final = final_inputs is not None
    n_seq = xs.shape[0]
    seq_len = (n_chunks - 1) * CHUNK

    def chunk_of(t):
        return (n_chunks - 1 - t) if reverse else t

    def pad_map(t):
        return (0, _storage_chunk(chunk_of(t), n_chunks), 0)

    def out_map(t):
        return (0, jnp.maximum(chunk_of(t) - 1, 0), 0)

    const = lambda a: pl.BlockSpec(a.shape, lambda t: (0,) * a.ndim)
    pad_spec = lambda n: pl.BlockSpec((n_seq, CHUNK, n), pad_map)
    out_spec = lambda n: pl.BlockSpec((n_seq, CHUNK, n), out_map)
    in_arrays = [xs, bm, cm, gates, q, k, v]
    in_specs = [pad_spec(a.shape[2]) for a in in_arrays]
    if final:
        z, o, yb, hb, dskip, ngs, ngm = final_inputs
        in_arrays += [z, o, yb, hb]
        in_specs += [pad_spec(1024), pad_spec(1024), out_spec(1024), out_spec(1024)]
        in_arrays += [gbias, alog, expand, dskip, ngs, ngm]
        in_specs += [const(a) for a in (gbias, alog, expand, dskip, ngs, ngm)]
        out_shape = [jax.ShapeDtypeStruct((n_seq, seq_len, 2 * D_MODEL), BF16)]
        out_specs = [out_spec(2 * D_MODEL)]
    else:
        in_arrays += [gbias, alog, expand]
        in_specs += [const(a) for a in (gbias, alog, expand)]
        out_shape = [jax.ShapeDtypeStruct((n_seq, seq_len, D_MODEL), BF16),
                     jax.ShapeDtypeStruct((n_seq, seq_len, D_MODEL), BF16)]
        out_specs = [out_spec(D_MODEL), out_spec(D_MODEL)]
    return pl.pallas_call(
        functools.partial(_mixer_kernel, reverse=reverse, final=final, n_chunks=n_chunks, n_seq=n_seq),
        grid=(n_chunks,),
        in_specs=in_specs,
        out_specs=out_specs,
        out_shape=out_shape,
        scratch_shapes=[pltpu.VMEM((n_seq, SSD_GROUPS, SSD_STATE, GROUP_WIDTH), F32),
                        pltpu.VMEM((n_seq, MLSTM_HEADS // 2, 2 * MLSTM_DK, 2 * MLSTM_DV), F32),
                        pltpu.VMEM((n_seq, MLSTM_HEADS, LANES), F32)],
        compiler_params=pltpu.CompilerParams(
            dimension_semantics=("arbitrary",), vmem_limit_bytes=VMEM_LIMIT_BYTES),
        name="mixer_fwd" if final else "mixer_bwd",
    )(*in_arrays)


def _epilogue_kernel(x_ref, ycat_ref, wout_ref, lng0_ref, lnb0_ref, lng1_ref, lnb1_ref,
                     wrh_ref, wrl_ref, br_ref, h1_ref, h1p_ref, sel_ref, gate_ref, cnt_ref):
    h0 = _layer_norm(x_ref[...], lng0_ref[...], lnb0_ref[...])
    mix = _dot(ycat_ref[...], wout_ref[...])
    h1 = _layer_norm(DEEPNORM_ALPHA * h0 + mix, lng1_ref[...], lnb1_ref[...])
    h1_ref[...] = h1
    h1p_ref[...] = _pack_bf16_pairs(h1)
    hh = h1.astype(BF16)
    hl = (h1 - hh.astype(F32)).astype(BF16)
    both = _dot(hh, jnp.concatenate([wrh_ref[...], wrl_ref[...]], axis=1))
    logits = both[:, :LANES] + both[:, LANES:] + _dot(hl, wrh_ref[...]) + br_ref[...]
    lane = lax.broadcasted_iota(jnp.int32, (1, LANES), 1)
    lane_f = lane.astype(F32)
    logits = jnp.where(lane < N_EXPERTS, logits, -jnp.inf)
    work = logits
    sel = jnp.zeros(logits.shape, jnp.bool_)
    top = None
    for _ in range(TOP_K):
        m = jnp.max(work, axis=-1, keepdims=True)
        if top is None:
            top = m
        first = jnp.min(jnp.where(work == m, lane_f, float(LANES)), axis=-1, keepdims=True)
        pick = lane_f == first
        sel = jnp.logical_or(sel, pick)
        work = jnp.where(pick, -jnp.inf, work)
    e = jnp.where(sel, jnp.exp(logits - top), 0.0)
    gate_ref[...] = e / jnp.sum(e, axis=-1, keepdims=True)
    sel_f = sel.astype(F32)
    sel_ref[...] = sel_f

    @pl.when(pl.program_id(0) == 0)
    def _():
        cnt_ref[...] = jnp.zeros_like(cnt_ref)

    cnt_ref[0:1, :] = cnt_ref[0:1, :] + jnp.sum(sel_f, axis=0, keepdims=True)


def _epilogue(x, ycat, ycat_row0, w_out, lng0, lnb0, lng1, lnb1, wrh, wrl, br):
    rows = x.shape[0]
    tm = EPILOGUE_ROWS
    assert rows % tm == 0 and ycat_row0 % tm == 0
    tile0 = ycat_row0 // tm
    row_spec = lambda n: pl.BlockSpec((tm, n), lambda i: (i, 0))
    const = lambda a: pl.BlockSpec(a.shape, lambda i: (0,) * a.ndim)
    out_shape = [jax.ShapeDtypeStruct((rows, D_MODEL), F32),
                 jax.ShapeDtypeStruct((rows, D_MODEL // 2), jnp.uint32),
                 jax.ShapeDtypeStruct((rows, LANES), F32),
                 jax.ShapeDtypeStruct((rows, LANES), F32),
                 jax.ShapeDtypeStruct((8, LANES), F32)]
    consts = (w_out, lng0, lnb0, lng1, lnb1, wrh, wrl, br)
    return pl.pallas_call(
        _epilogue_kernel,
        grid=(rows // tm,),
        in_specs=[row_spec(D_MODEL), pl.BlockSpec((tm, 2 * D_MODEL), lambda i: (tile0 + i, 0))]
                 + [const(a) for a in consts],
        out_specs=[row_spec(D_MODEL), row_spec(D_MODEL // 2), row_spec(LANES), row_spec(LANES),
                   pl.BlockSpec((8, LANES), lambda i: (0, 0))],
        out_shape=out_shape,
        compiler_params=pltpu.CompilerParams(
            dimension_semantics=("arbitrary",), vmem_limit_bytes=VMEM_LIMIT_BYTES),
        name="epilogue",
    )(x, ycat, *consts)


def _rank_kernel(sel_ref, gate_ref, cnt_ref, lstrict_ref, ucum_ref, dest_ref, gk_ref, stats_ref,
                 base_ref, *, trash_row):
    i = pl.program_id(0)
    sel = sel_ref[...]
    colsum = jnp.sum(sel, axis=0, keepdims=True)

    @pl.when(i == 0)
    def _():
        counts = cnt_ref[0:1, :]
        padded = jnp.ceil(counts / MOE_BLOCK) * MOE_BLOCK
        pend = _dot_exact_rhs(jnp.broadcast_to(padded, (8, LANES)), ucum_ref[...])[0:1, :]
        stats_ref[0:1, :] = counts
        stats_ref[1:2, :] = pend - padded
        stats_ref[2:3, :] = pend
        stats_ref[3:8, :] = jnp.zeros((5, LANES), F32)
        base_ref[...] = jnp.broadcast_to(pend - padded, base_ref.shape)

    before = _dot(lstrict_ref[...], sel.astype(BF16))
    pos = base_ref[0:1, :] + before
    base_ref[0:1, :] = base_ref[0:1, :] + colsum
    work = jnp.where(sel > 0.0, pos + 1.0, 0.0)
    gates = gate_ref[...]
    for kk in range(TOP_K):
        m = jnp.max(work, axis=-1, keepdims=True)
        pick = jnp.logical_and(work == m, m > 0.0)
        gk_ref[:, kk:kk + 1] = jnp.sum(jnp.where(pick, gates, 0.0), axis=-1, keepdims=True)
        dest_ref[:, kk:kk + 1] = jnp.where(m > 0.0, m - 1.0, float(trash_row)).astype(jnp.int32)
        work = jnp.where(pick, 0.0, work)


def _rank(sel, gates, counts, lstrict, ucum, trash_row):
    rows = sel.shape[0]
    tm = RANK_ROWS
    assert rows % tm == 0
    row_spec = lambda n: pl.BlockSpec((tm, n), lambda i: (i, 0))
    const = lambda a: pl.BlockSpec(a.shape, lambda i: (0,) * a.ndim)
    return pl.pallas_call(
        functools.partial(_rank_kernel, trash_row=trash_row),
        grid=(rows // tm,),
        in_specs=[row_spec(LANES), row_spec(LANES), const(counts), const(lstrict), const(ucum)],
        out_specs=[row_spec(TOP_K), row_spec(TOP_K), pl.BlockSpec((8, LANES), lambda i: (0, 0))],
        out_shape=[jax.ShapeDtypeStruct((rows, TOP_K), jnp.int32),
                   jax.ShapeDtypeStruct((rows, TOP_K), F32),
                   jax.ShapeDtypeStruct((8, LANES), F32)],
        scratch_shapes=[pltpu.VMEM((8, LANES), F32)],
        compiler_params=pltpu.CompilerParams(
            dimension_semantics=("arbitrary",), vmem_limit_bytes=VMEM_LIMIT_BYTES),
        name="rank",
    )(sel, gates, counts, lstrict, ucum)


def _sc_mesh():
    return plsc.VectorSubcoreMesh(core_axis_name="c", subcore_axis_name="s",
                                  num_cores=SC_CORES, num_subcores=SC_SUBCORES)


def _sc_scatter_rows(src, idx, n_out_rows):
    n_src, d = src.shape
    w, k = SC_WINDOW, SC_SCATTER_INFLIGHT
    assert idx.shape == (n_src // w, TOP_K, w) and n_src % (w * k * SC_WORKERS) == 0
    per_worker = n_src // (w * SC_WORKERS)
    idx2d = idx.reshape(n_src // w * TOP_K, w)

    @functools.partial(
        pl.kernel, mesh=_sc_mesh(),
        out_type=jax.ShapeDtypeStruct((n_out_rows, d), src.dtype),
        scratch_types=[pltpu.VMEM((k * TOP_K, w), jnp.int32), pltpu.VMEM((k, w, d), src.dtype),
                       pltpu.SemaphoreType.DMA((k,)), pltpu.SemaphoreType.DMA((k,))],
        name="sc_scatter_rows")
    def body(src_hbm, idx_hbm, out_hbm, idx_v, rows_v, sem_load, sem_scatter):
        wid = lax.axis_index("s") * SC_CORES + lax.axis_index("c")

        @pl.loop(0, per_worker // k)
        def _(i):
            win0 = wid * per_worker + i * k
            loads = [pltpu.async_copy(src_hbm.at[pl.ds(pl.multiple_of((win0 + b) * w, w), w)], rows_v.at[b],
                                      sem_load.at[b]) for b in range(k)]
            pltpu.sync_copy(idx_hbm.at[pl.ds(pl.multiple_of(win0 * TOP_K, TOP_K), k * TOP_K)], idx_v)
            scatters = []
            for b in range(k):
                loads[b].wait()
                scatters += [pltpu.async_copy(rows_v.at[b], out_hbm.at[idx_v.at[b * TOP_K + kk]],
                                              sem_scatter.at[b]) for kk in range(TOP_K)]
            for copy in scatters:
                copy.wait()

    return body(src, idx2d)


def _sc_gather_rows(table, idx):
    d = table.shape[1]
    w, k = SC_WINDOW, SC_GATHER_INFLIGHT
    assert idx.shape[0] % (w * k * SC_WORKERS) == 0
    per_worker = idx.shape[0] // (w * SC_WORKERS)
    idx2d = idx.reshape(-1, w)

    @functools.partial(
        pl.kernel, mesh=_sc_mesh(),
        out_type=jax.ShapeDtypeStruct((idx.shape[0], d), table.dtype),
        scratch_types=[pltpu.VMEM((k, w), jnp.int32), pltpu.VMEM((k, w, d), table.dtype),
                       pltpu.SemaphoreType.DMA((k,)), pltpu.SemaphoreType.DMA((k,))],
        name="sc_gather_rows")
    def body(table_hbm, idx_hbm, out_hbm, idx_v, rows_v, sem_gather, sem_store):
        wid = lax.axis_index("s") * SC_CORES + lax.axis_index("c")

        @pl.loop(0, per_worker // k)
        def _(i):
            win0 = wid * per_worker + i * k
            pltpu.sync_copy(idx_hbm.at[pl.ds(pl.multiple_of(win0, k), k)], idx_v)
            gathers = [pltpu.async_copy(table_hbm.at[idx_v.at[b]], rows_v.at[b], sem_gather.at[b])
                       for b in range(k)]
            stores = []
            for b in range(k):
                gathers[b].wait()
                stores.append(pltpu.async_copy(
                    rows_v.at[b], out_hbm.at[pl.ds(pl.multiple_of((win0 + b) * w, w), w)], sem_store.at[b]))
            for copy in stores:
                copy.wait()

    return body(table, idx2d)


def _ffn_kernel(blk_ref, exp_ref, valid_ref, first_ref, next_ref, slot_ref, nused_ref,
                xb_ref, w1_hbm, b1_ref, w2_hbm, b2_ref, yb_ref, w1f_ref, w2f_ref, w1b_ref, w2b_ref, sems):
    j = pl.program_id(0)
    active = j < nused_ref[0]

    def weight_copies(e, slot):
        return (pltpu.make_async_copy(w1_hbm.at[e], w1f_ref.at[slot], sems.at[0, slot]),
                pltpu.make_async_copy(w2_hbm.at[e], w2f_ref.at[slot], sems.at[1, slot]))

    @pl.when(jnp.logical_and(active, first_ref[j] == 1))
    def _():
        e = exp_ref[j]
        slot = slot_ref[j]
        nxt = next_ref[j]

        @pl.when(j == 0)
        def _():
            for copy in weight_copies(e, slot):
                copy.start()

        @pl.when(nxt >= 0)
        def _():
            for copy in weight_copies(nxt, 1 - slot):
                copy.start()

        for copy in weight_copies(e, slot):
            copy.wait()
        w1b_ref[...] = w1f_ref[slot].astype(BF16)
        w2b_ref[...] = w2f_ref[slot].astype(BF16)

    @pl.when(active)
    def _():
        row = lax.broadcasted_iota(jnp.int32, (MOE_BLOCK, 1), 0)
        x = jnp.where(row < valid_ref[j], _unpack_bf16_pairs(xb_ref[...]), 0.0).astype(BF16)
        hc = _dot(x, w1b_ref[...]) + b1_ref[0]
        gate = jnp.minimum(hc[:, :D_FF], SWIGLU_LIMIT)
        up = jnp.clip(hc[:, D_FF:], -SWIGLU_LIMIT, SWIGLU_LIMIT)
        act = (up + 1.0) * gate * _sigmoid(SWIGLU_ALPHA * gate)
        yb_ref[...] = _pack_bf16_pairs(_dot(act.astype(BF16), w2b_ref[...]) + b2_ref[0])


def _ffn(blk_idx, blk_exp, blk_valid, blk_first, blk_next, blk_slot, n_used, xb, w1, b1, w2, b2, n_blocks):
    bm = MOE_BLOCK
    grid_spec = pltpu.PrefetchScalarGridSpec(
        num_scalar_prefetch=7,
        grid=(n_blocks,),
        in_specs=[pl.BlockSpec((bm, D_MODEL // 2), lambda j, bi, be, *_: (bi[j], 0)),
                  pl.BlockSpec(memory_space=pl.ANY),
                  pl.BlockSpec((1, 1, 2 * D_FF), lambda j, bi, be, *_: (be[j], 0, 0)),
                  pl.BlockSpec(memory_space=pl.ANY),
                  pl.BlockSpec((1, 1, D_MODEL), lambda j, bi, be, *_: (be[j], 0, 0))],
        out_specs=pl.BlockSpec((bm, D_MODEL // 2), lambda j, bi, be, *_: (bi[j], 0)),
        scratch_shapes=[pltpu.VMEM((2, D_MODEL, 2 * D_FF), F32), pltpu.VMEM((2, D_FF, D_MODEL), F32),
                        pltpu.VMEM((D_MODEL, 2 * D_FF), BF16), pltpu.VMEM((D_FF, D_MODEL), BF16),
                        pltpu.SemaphoreType.DMA((2, 2))],
    )
    return pl.pallas_call(
        _ffn_kernel,
        grid_spec=grid_spec,
        out_shape=jax.ShapeDtypeStruct(xb.shape, jnp.uint32),
        compiler_params=pltpu.CompilerParams(
            dimension_semantics=("arbitrary",), vmem_limit_bytes=VMEM_LIMIT_BYTES),
        name="expert_ffn",
    )(blk_idx, blk_exp, blk_valid, blk_first, blk_next, blk_slot, n_used, xb, w1, b1, w2, b2)


def _combine_kernel(gk_ref, h1_ref, y0_ref, y1_ref, y2_ref, y3_ref, lng_ref, lnb_ref, out_ref):
    gk = gk_ref[...]
    ffn = gk[:, 0:1] * _unpack_bf16_pairs(y0_ref[...])
    for kk, y_ref in enumerate((y1_ref, y2_ref, y3_ref), start=1):
        ffn = ffn + gk[:, kk:kk + 1] * _unpack_bf16_pairs(y_ref[...])
    out_ref[...] = _layer_norm(DEEPNORM_ALPHA * h1_ref[...] + ffn, lng_ref[...], lnb_ref[...])


def _combine(gk, h1, ysel, lng, lnb):
    rows = h1.shape[0]
    tm = COMBINE_ROWS
    assert rows % tm == 0
    n_tiles = rows // tm
    const = lambda a: pl.BlockSpec(a.shape, lambda i: (0,) * a.ndim)
    ksel = lambda kk: pl.BlockSpec((tm, D_MODEL // 2), lambda i: (kk * n_tiles + i, 0))
    return pl.pallas_call(
        _combine_kernel,
        grid=(n_tiles,),
        in_specs=[pl.BlockSpec((tm, TOP_K), lambda i: (i, 0)),
                  pl.BlockSpec((tm, D_MODEL), lambda i: (i, 0)),
                  ksel(0), ksel(1), ksel(2), ksel(3),
                  const(lng), const(lnb)],
        out_specs=pl.BlockSpec((tm, D_MODEL), lambda i: (i, 0)),
        out_shape=jax.ShapeDtypeStruct((rows, D_MODEL), F32),
        compiler_params=pltpu.CompilerParams(
            dimension_semantics=("arbitrary",), vmem_limit_bytes=VMEM_LIMIT_BYTES),
        name="combine",
    )(gk, h1, ysel, ysel, ysel, ysel, lng, lnb)


def _row(v, width=None):
    v = v.reshape(1, -1).astype(F32)
    if width is not None and v.shape[1] < width:
        v = jnp.pad(v, ((0, 0), (0, width - v.shape[1])))
    return v


def _encode_all(x_a, x_b, meta_tokens, ln_emb_g, ln_emb_b, w_in, conv_w, conv_b, dt_bias, a_log,
                d_skip, ssd_norm_g, i_bias, f_bias, mlstm_norm_g, w_out, ln1_g, ln1_b, w_router, b_router,
                w1, b1, w2, b2, ln2_g, ln2_b):
    n_a, seq_len, _ = x_a.shape
    n_seq = n_a + x_b.shape[0]
    rows_a = n_a * seq_len
    assert seq_len % CHUNK == 0
    n_chunks = seq_len // CHUNK + 1
    n_tok = n_seq * seq_len

    sizes = (1024, CONV_CH, 2 * SSD_HEADS, 512, 512, 1024, 1024, 2 * MLSTM_HEADS, 2 * MLSTM_HEADS)
    offs = [0]
    for s in sizes:
        offs.append(offs[-1] + s)
    w_z, w_xbc, w_dt, w_q, w_k, w_v, w_o, w_i, w_f = [w_in[:, offs[j]:offs[j + 1]] for j in range(9)]
    w_big = jnp.concatenate([w_z, w_xbc, w_q, w_k * (MLSTM_DK ** -0.5), w_v, w_o], axis=1).astype(BF16)
    zpad = jnp.zeros((D_MODEL, LANES - GATE_END), F32)
    gate_cols = []
    for d in range(2):
        gate_cols += [w_dt[:, d * SSD_HEADS:(d + 1) * SSD_HEADS],
                      w_i[:, d * MLSTM_HEADS:(d + 1) * MLSTM_HEADS],
                      w_f[:, d * MLSTM_HEADS:(d + 1) * MLSTM_HEADS], zpad]
    w_gates = jnp.concatenate(gate_cols, axis=1).astype(BF16)
    gbias = [_row(jnp.concatenate([dt_bias[d], i_bias[d], f_bias[d]]), LANES) for d in range(2)]
    alog = [jnp.pad(jnp.broadcast_to(a_log[d].astype(F32)[:, None], (SSD_HEADS, LANES)),
                    ((0, GATE_END - SSD_HEADS), (0, 0))) for d in range(2)]
    head_of_col = jnp.arange(D_MODEL, dtype=jnp.int32) // SSD_HEAD_DIM
    lane_id = jnp.arange(LANES, dtype=jnp.int32)[:, None]
    expand = jnp.concatenate([lane_id == GATE_END + head_of_col[None, :],
                              lane_id == 2 * GATE_END + head_of_col[None, :]], axis=1).astype(BF16)
    dskip = _row(jnp.repeat(d_skip, SSD_HEAD_DIM))
    conv_w8 = jnp.pad(conv_w.astype(F32), ((0, 8 - CONV_W), (0, 0)))
    meta_tile = jnp.pad(meta_tokens.astype(F32), ((PAD_FRONT, INPROJ_ROWS - CHUNK), (0, 0)))

    z, xbc, q, k, v, o, gates_f, gates_b = _inproj(x_a, x_b, meta_tile, _row(ln_emb_g), _row(ln_emb_b),
                                                   w_big, w_gates)
    xs, bm, cm = _conv(xbc, conv_w8, _row(conv_b), n_chunks)
    yb, hb = _mixer_pass(xs, bm, cm, gates_b, q, k, v, gbias[1], alog[1], expand, n_chunks, reverse=True)
    (ycat,) = _mixer_pass(xs, bm, cm, gates_f, q, k, v, gbias[0], alog[0], expand, n_chunks,
                          reverse=False,
                          final_inputs=(z, o, yb, hb, dskip, _row(ssd_norm_g), _row(mlstm_norm_g)))

    wr = jnp.pad(w_router.astype(F32), ((0, 0), (0, LANES - N_EXPERTS)))
    wrh = wr.astype(BF16)
    wrl = (wr - wrh.astype(F32)).astype(BF16)
    w_out_b = w_out.astype(BF16)
    r_i = jnp.arange(RANK_ROWS, dtype=jnp.int32)
    lstrict = (r_i[None, :] < r_i[:, None]).astype(BF16)
    l_i = jnp.arange(LANES, dtype=jnp.int32)
    ucum = (l_i[:, None] <= l_i[None, :]).astype(BF16)
    ycat2d = ycat.reshape(n_tok, 2 * D_MODEL)
    outs = []
    for x_part, row0 in ((x_a, 0), (x_b, rows_a)):
        n_part = x_part.shape[0] * seq_len
        h1, h1p, sel, gates, expert_counts = _epilogue(x_part.reshape(n_part, D_MODEL), ycat2d, row0, w_out_b,
                                        _row(ln_emb_g), _row(ln_emb_b), _row(ln1_g), _row(ln1_b),
                                        wrh, wrl, _row(b_router, LANES))
        n_blocks = n_part * TOP_K // MOE_BLOCK + N_EXPERTS
        trash_row = n_blocks * MOE_BLOCK
        dest, gk, stats = _rank(sel, gates, expert_counts, lstrict, ucum, trash_row)
        counts = stats[0, :N_EXPERTS].astype(jnp.int32)
        starts = stats[1, :N_EXPERTS].astype(jnp.int32)
        pends = stats[2, :N_EXPERTS].astype(jnp.int32)
        n_used = pends[N_EXPERTS - 1] // MOE_BLOCK
        blk = jnp.minimum(jnp.arange(n_blocks, dtype=jnp.int32), jnp.maximum(n_used - 1, 0))
        blk_exp = jnp.minimum(
            jnp.sum((pends[None, :] <= (blk * MOE_BLOCK)[:, None]).astype(jnp.int32), axis=1),
            N_EXPERTS - 1).astype(jnp.int32)
        blk_last = jnp.take(starts + counts, blk_exp)
        blk_valid = jnp.clip(blk_last - blk * MOE_BLOCK, 0, MOE_BLOCK).astype(jnp.int32)
        dest_km = dest.T.reshape(-1)
        dest_wm = dest.reshape(n_part // SC_WINDOW, SC_WINDOW, TOP_K).transpose(0, 2, 1)
        xb = _sc_scatter_rows(h1p, dest_wm, trash_row + MOE_BLOCK)
        j_i = jnp.arange(n_blocks, dtype=jnp.int32)
        blk_first = jnp.logical_and(j_i < n_used, jnp.logical_or(j_i == 0, blk_exp != jnp.roll(blk_exp, 1)))
        blk_slot = ((jnp.cumsum(blk_first.astype(jnp.int32)) - 1) % 2).astype(jnp.int32)
        run_end = jnp.take(pends, blk_exp) // MOE_BLOCK
        blk_next = jnp.where(run_end < n_used, jnp.take(blk_exp, jnp.minimum(run_end, n_blocks - 1)),
                             -1).astype(jnp.int32)
        yexp = _ffn(blk, blk_exp, blk_valid, blk_first.astype(jnp.int32), blk_next, blk_slot,
                    n_used.reshape(1), xb, w1, b1.reshape(N_EXPERTS, 1, -1), w2,
                    b2.reshape(N_EXPERTS, 1, -1), n_blocks)
        ysel = _sc_gather_rows(yexp, dest_km)
        outs.append(_combine(gk, h1, ysel, _row(ln2_g), _row(ln2_b)))
    return outs


def kernel(x_prompt, x_sample, meta_tokens, ln_emb_g, ln_emb_b, w_in, conv_w, conv_b, dt_bias, a_log,
           d_skip, ssd_norm_g, i_bias, f_bias, mlstm_norm_g, w_out, ln1_g, ln1_b, w_router, b_router,
           w1, b1, w2, b2, ln2_g, ln2_b):
    assert x_prompt.shape[1:] == x_sample.shape[1:]
    n_p, seq_len, d = x_prompt.shape
    n_s = x_sample.shape[0]
    y_p, y_s = _encode_all(x_prompt.astype(F32), x_sample.astype(F32), meta_tokens, ln_emb_g, ln_emb_b, w_in[0], conv_w[0],
                           conv_b[0], dt_bias[0], a_log[0], d_skip[0], ssd_norm_g[0], i_bias[0], f_bias[0],
                           mlstm_norm_g[0], w_out[0], ln1_g[0], ln1_b[0], w_router[0], b_router[0],
                           w1[0], b1[0], w2[0], b2[0], ln2_g[0], ln2_b[0])
    return (y_p.reshape(n_p, seq_len, d), y_s.reshape(n_s, seq_len, d))
```

```python
import functools

import jax
import jax.numpy as jnp
from jax import lax
from jax.experimental import pallas as pl
from jax.experimental.pallas import tpu as pltpu
from jax.experimental.pallas import tpu_sc as plsc

F32 = jnp.float32
BF16 = jnp.bfloat16

D_MODEL = 1024
N_META = 16
CHUNK = 128
PAD_FRONT = CHUNK - N_META
SSD_HEADS = 16
SSD_HEAD_DIM = 64
SSD_GROUPS = 4
SSD_STATE = 128
HEADS_PER_GROUP = SSD_HEADS // SSD_GROUPS
GROUP_WIDTH = HEADS_PER_GROUP * SSD_HEAD_DIM
CONV_W = 5
CONV_HALF = CONV_W // 2
CONV_CH = D_MODEL + 2 * SSD_GROUPS * SSD_STATE
MLSTM_HEADS = 8
MLSTM_DK = 64
MLSTM_DV = 128
N_EXPERTS = 32
TOP_K = 4
D_FF = D_MODEL
SWIGLU_LIMIT = 7.0
SWIGLU_ALPHA = 1.702
DEEPNORM_ALPHA = 2.0 ** 0.25
LN_EPS = 1e-5
RMS_EPS = 1e-5
NEG_GATE = -1e30
LOG2E = 1.4426950408889634

LANES = 128
BF16_SUBLANES = 16
VMEM_LIMIT_BYTES = 56 * 1024 * 1024

GATE_DT0, GATE_I0, GATE_F0, GATE_END = 0, SSD_HEADS, SSD_HEADS + MLSTM_HEADS, SSD_HEADS + 2 * MLSTM_HEADS

INPROJ_ROWS = 512
EPILOGUE_ROWS = 512
RANK_ROWS = 512
COMBINE_ROWS = 512
MOE_BLOCK = 512

SC_CORES = 2
SC_SUBCORES = 16
SC_WORKERS = SC_CORES * SC_SUBCORES
SC_WINDOW = 32
SC_SCATTER_INFLIGHT = 2
SC_GATHER_INFLIGHT = 4


def _dot(a, b):
    return jnp.dot(a, b, preferred_element_type=F32)


def _dot_nt(a, b):
    return lax.dot_general(a, b, (((1,), (1,)), ((), ())), preferred_element_type=F32)


def _split3(x):
    hi = x.astype(BF16)
    r1 = x - hi.astype(F32)
    mid = r1.astype(BF16)
    lo = (r1 - mid.astype(F32)).astype(BF16)
    return hi, mid, lo


def _dot_exact_lhs(a_bf16, x):
    hi, mid, lo = _split3(x)
    return _dot(a_bf16, hi) + _dot(a_bf16, mid) + _dot(a_bf16, lo)


def _dot_exact_rhs(x, b_bf16):
    hi, mid, lo = _split3(x)
    return _dot(hi, b_bf16) + _dot(mid, b_bf16) + _dot(lo, b_bf16)


def _layer_norm(x, g, b):
    mu = jnp.mean(x, axis=-1, keepdims=True)
    xc = x - mu
    var = jnp.mean(xc * xc, axis=-1, keepdims=True)
    return xc * lax.rsqrt(var + LN_EPS) * g + b


def _sigmoid(x):
    return 1.0 / (1.0 + jnp.exp(-x))


def _log1p_exp_neg_abs(x):
    return jnp.log(1.0 + jnp.exp(-jnp.abs(x)))


def _pack_bf16_pairs(x):
    n = x.shape[1] // 2
    bits = lax.bitcast_convert_type(x.astype(BF16).astype(F32), jnp.uint32)
    return (bits[:, :n] >> 16) | bits[:, n:]


def _unpack_bf16_pairs(words):
    lo = lax.bitcast_convert_type(words << 16, F32)
    hi = lax.bitcast_convert_type(words & jnp.uint32(0xFFFF0000), F32)
    return jnp.concatenate([lo, hi], axis=1)


def _storage_chunk(c, n_chunks):
    return jnp.where(c == 0, n_chunks - 1, c - 1)


def _inproj_kernel(xa_ref, xb_ref, meta_ref, g_ref, b_ref, wbig_ref, wg_ref,
                   z_ref, xbc_ref, q_ref, k_ref, v_ref, o_ref, gf_ref, gb_ref, *, n_x_tiles, n_a):
    x = jnp.where(pl.program_id(0) < n_a, xa_ref[...], xb_ref[...])
    x = jnp.where(pl.program_id(1) == n_x_tiles, meta_ref[...], x)
    h = _layer_norm(x, g_ref[...], b_ref[...]).astype(BF16)

    def mm(c0, c1):
        return _dot(h, wbig_ref[:, c0:c1]).astype(BF16)

    z_ref[...] = mm(0, 1024)
    xbc_ref[:, 0:1024] = mm(1024, 2048)
    xbc_ref[:, 1024:2048] = mm(2048, 3072)
    q_ref[...] = mm(3072, 3584)
    k_ref[...] = mm(3584, 4096)
    v_ref[...] = mm(4096, 5120)
    o_ref[...] = mm(5120, 6144)
    gates = _dot(h, wg_ref[...])
    gf_ref[...] = gates[:, :LANES]
    gb_ref[...] = gates[:, LANES:]


def _inproj(x_a, x_b, meta_tile, ln_g, ln_b, w_big, w_gates):
    n_a, seq_len, _ = x_a.shape
    n_seq = n_a + x_b.shape[0]
    tm = INPROJ_ROWS
    assert seq_len % tm == 0 and tm >= CHUNK
    n_x_tiles = seq_len // tm
    rows = seq_len + CHUNK
    row_spec = lambda n: pl.BlockSpec((None, tm, n), lambda b, i: (b, i, 0))
    const = lambda a: pl.BlockSpec(a.shape, lambda b, i: (0,) * a.ndim)
    resident = lambda a: pl.BlockSpec(a.shape, lambda b, i: (0,) * a.ndim, pipeline_mode=pl.Buffered(1))
    widths = (1024, CONV_CH, 512, 512, 1024, 1024)
    out_shapes = [jax.ShapeDtypeStruct((n_seq, rows, w), BF16) for w in widths]
    out_shapes += [jax.ShapeDtypeStruct((n_seq, rows, LANES), F32)] * 2
    return pl.pallas_call(
        functools.partial(_inproj_kernel, n_x_tiles=n_x_tiles, n_a=n_a),
        grid=(n_seq, n_x_tiles + 1),
        in_specs=[pl.BlockSpec((None, tm, D_MODEL), lambda b, i: (
                      jnp.minimum(b, n_a - 1),
                      jnp.where(b < n_a, jnp.minimum(i, n_x_tiles - 1), n_x_tiles - 1), 0)),
                  pl.BlockSpec((None, tm, D_MODEL), lambda b, i: (
                      jnp.maximum(b - n_a, 0),
                      jnp.where(b < n_a, 0, jnp.minimum(i, n_x_tiles - 1)), 0)),
                  resident(meta_tile), const(ln_g), const(ln_b), resident(w_big), resident(w_gates)],
        out_specs=[row_spec(s.shape[2]) for s in out_shapes],
        out_shape=out_shapes,
        compiler_params=pltpu.CompilerParams(
            dimension_semantics=("arbitrary", "arbitrary"), vmem_limit_bytes=VMEM_LIMIT_BYTES),
        name="inproj",
    )(x_a, x_b, meta_tile, ln_g, ln_b, w_big, w_gates)


def _conv_kernel(prev_ref, main_ref, next_ref, shift_ref, w_ref, b_ref, xs_ref, bm_ref, cm_ref,
                 *, n_x_tiles, chunks_per_tile):
    i = pl.program_id(1)
    w = w_ref[...]
    bias = b_ref[...]
    shift = shift_ref[...]

    def conv_chunk(before, rows, after, pad_rows, j):
        shifted = _dot(shift, jnp.concatenate([before, rows, after], axis=0).astype(BF16))
        acc = bias + rows * w[CONV_HALF:CONV_HALF + 1, :]
        for jj, t in enumerate(t for t in range(CONV_W) if t != CONV_HALF):
            acc = acc + shifted[jj * CHUNK:(jj + 1) * CHUNK, :] * w[t:t + 1, :]
        y = acc * _sigmoid(acc)
        if pad_rows is not None:
            y = jnp.where(pad_rows, 0.0, y)
        r = slice(j * CHUNK, (j + 1) * CHUNK)
        xs_ref[r, :] = y[:, :D_MODEL].astype(BF16)
        bm_ref[r, :] = y[:, D_MODEL:D_MODEL + 512].astype(BF16)
        cm_ref[r, :] = y[:, D_MODEL + 512:].astype(BF16)

    @pl.when(i < n_x_tiles)
    def _():
        tile = main_ref[...].astype(F32)
        for j in range(chunks_per_tile):
            lo, hi = j * CHUNK, (j + 1) * CHUNK
            before = prev_ref[...].astype(F32) if j == 0 else tile[lo - BF16_SUBLANES:lo, :]
            if j == chunks_per_tile - 1:
                after = jnp.where(i == n_x_tiles - 1, 0.0, next_ref[...].astype(F32))
            else:
                after = tile[hi:hi + BF16_SUBLANES, :]
            conv_chunk(before, tile[lo:hi, :], after, None, j)

    @pl.when(i == n_x_tiles)
    def _():
        row = lax.broadcasted_iota(jnp.int32, (CHUNK, 1), 0)
        pad_rows = row < PAD_FRONT
        rows = jnp.where(pad_rows, 0.0, main_ref[0:CHUNK, :].astype(F32))
        conv_chunk(jnp.zeros((BF16_SUBLANES, CONV_CH), F32), rows, next_ref[...].astype(F32), pad_rows, 0)


def _conv(xbc, conv_w8, conv_b, n_chunks):
    n_seq, rows, _ = xbc.shape
    tm = INPROJ_ROWS
    seq_len = (n_chunks - 1) * CHUNK
    assert seq_len % tm == 0 and tm % CHUNK == 0
    n_x_tiles = seq_len // tm
    halo_per_tile = tm // BF16_SUBLANES
    meta_last_halo = rows // BF16_SUBLANES - 1

    def prev_map(b, i):
        before_tile = jnp.maximum(jnp.minimum(i, n_x_tiles - 1) * halo_per_tile - 1, 0)
        return (b, jnp.where(i == 0, meta_last_halo, before_tile), 0)

    def next_map(b, i):
        return (b, jnp.where(i >= n_x_tiles - 1, 0, (i + 1) * halo_per_tile) , 0)

    const = lambda a: pl.BlockSpec(a.shape, lambda b, i: (0,) * a.ndim)
    l_i = jnp.arange(CHUNK, dtype=jnp.int32)[:, None]
    j_i = jnp.arange(CHUNK + 2 * BF16_SUBLANES, dtype=jnp.int32)[None, :]
    shifts = jnp.concatenate([(j_i == BF16_SUBLANES + l_i + t - CONV_HALF)
                              for t in range(CONV_W) if t != CONV_HALF], axis=0).astype(BF16)
    out_shapes = [jax.ShapeDtypeStruct((n_seq, rows, D_MODEL), BF16),
                  jax.ShapeDtypeStruct((n_seq, rows, 512), BF16),
                  jax.ShapeDtypeStruct((n_seq, rows, 512), BF16)]
    tile_spec = lambda n: pl.BlockSpec((None, tm, n), lambda b, i: (b, i, 0))
    return pl.pallas_call(
        functools.partial(_conv_kernel, n_x_tiles=n_x_tiles, chunks_per_tile=tm // CHUNK),
        grid=(n_seq, n_x_tiles + 1),
        in_specs=[pl.BlockSpec((None, BF16_SUBLANES, CONV_CH), prev_map),
                  tile_spec(CONV_CH),
                  pl.BlockSpec((None, BF16_SUBLANES, CONV_CH), next_map),
                  const(shifts), const(conv_w8), const(conv_b)],
        out_specs=[tile_spec(s.shape[2]) for s in out_shapes],
        out_shape=out_shapes,
        compiler_params=pltpu.CompilerParams(
            dimension_semantics=("arbitrary", "arbitrary"), vmem_limit_bytes=VMEM_LIMIT_BYTES),
        name="conv",
    )(xbc, xbc, xbc, shifts, conv_w8, conv_b)


def _mixer_kernel(*refs, reverse, final, n_chunks, n_seq):
    if final:
        (xs_ref, bm_ref, cm_ref, g_ref, q_ref, k_ref, v_ref, z_ref, o_ref, yb_ref, hb_ref,
         gbias_ref, alog_ref, expand_ref, dskip_ref, ngs_ref, ngm_ref,
         ycat_ref, s_ref, cst_ref, m_ref) = refs
    else:
        (xs_ref, bm_ref, cm_ref, g_ref, q_ref, k_ref, v_ref,
         gbias_ref, alog_ref, expand_ref,
         yout_ref, hout_ref, s_ref, cst_ref, m_ref) = refs

    t = pl.program_id(0)
    c = (n_chunks - 1 - t) if reverse else t
    end = 0 if reverse else CHUNK - 1

    @pl.when(t == 0)
    def _():
        s_ref[...] = jnp.zeros_like(s_ref)
        cst_ref[...] = jnp.zeros_like(cst_ref)
        m_ref[...] = jnp.zeros_like(m_ref)

    row = lax.broadcasted_iota(jnp.int32, (CHUNK, 1), 0)
    col = lax.broadcasted_iota(jnp.int32, (1, CHUNK), 1)
    lane = col
    allowed = (col >= row) if reverse else (col <= row)
    tri = allowed.astype(BF16)
    tri_t = ((row >= col) if reverse else (row <= col)).astype(BF16)
    feat = lax.broadcasted_iota(jnp.int32, (GATE_END, 1), 0)
    is_dt = feat < GATE_I0
    is_i = jnp.logical_and(feat >= GATE_I0, feat < GATE_F0)
    is_f = feat >= GATE_F0
    pad_cols = jnp.logical_and(c == 0, col < PAD_FRONT)
    a_coef = -jnp.exp(alog_ref[...]) * LOG2E
    expand = expand_ref[...]
    left_half = lane < SSD_HEAD_DIM
    right_half = jnp.logical_not(left_half)
    top_half = row < MLSTM_DK
    ones_blk = jnp.ones((CHUNK, MLSTM_DV), BF16)
    full = (CHUNK, LANES)

    def independent_products(b):
        cgs = [cm_ref[b, :, g * SSD_STATE:(g + 1) * SSD_STATE] for g in range(SSD_GROUPS)]
        bgs = [bm_ref[b, :, g * SSD_STATE:(g + 1) * SSD_STATE] for g in range(SSD_GROUPS)]
        cbs = [_dot_nt(cgs[g], bgs[g]) for g in range(SSD_GROUPS)]
        bg_ts = [bgs[g].astype(F32).T.astype(BF16) for g in range(SSD_GROUPS)]
        q_pairs = [q_ref[b, :, p * LANES:(p + 1) * LANES] for p in range(MLSTM_HEADS // 2)]
        k_pairs = [k_ref[b, :, p * LANES:(p + 1) * LANES] for p in range(MLSTM_HEADS // 2)]
        qks = [_dot_nt(jnp.where(left_half if h % 2 == 0 else right_half, q_pairs[h // 2],
                                 jnp.zeros_like(q_pairs[h // 2])), k_pairs[h // 2])
               for h in range(MLSTM_HEADS)]
        k_pair_ts = [k_pairs[p].astype(F32).T for p in range(MLSTM_HEADS // 2)]
        xs = xs_ref[b]
        xs_rhs = []
        for p in range(SSD_HEADS // 2):
            xs_pair = xs[:, p * LANES:(p + 1) * LANES]
            zero_pair = jnp.zeros_like(xs_pair)
            xs_rhs.append(jnp.concatenate([jnp.where(left_half, xs_pair, zero_pair),
                                           jnp.where(right_half, xs_pair, zero_pair)], axis=0))
        z_gate = o_gate = None
        if final:
            zz = z_ref[b].astype(F32)
            z_gate = zz * _sigmoid(zz)
            o_gate = _sigmoid(o_ref[b].astype(F32))
        return cgs, cbs, bg_ts, q_pairs, qks, k_pair_ts, xs, xs_rhs, z_gate, o_gate

    def one_sequence(b, products):
        cgs, cbs, bg_ts, q_pairs, qks, k_pair_ts, xs, xs_rhs, z_gate, o_gate = products

        gr = (g_ref[b] + gbias_ref[...]).T[0:GATE_END, :]
        lse = _log1p_exp_neg_abs(gr)
        val_t = jnp.where(is_dt, jnp.maximum(gr, 0.0) + lse, jnp.where(is_i, gr, jnp.minimum(gr, 0.0) - lse))
        val_t = jnp.where(pad_cols, jnp.where(is_i, NEG_GATE, 0.0), val_t)
        u_t = jnp.where(is_dt, val_t * a_coef, jnp.where(is_f, val_t * LOG2E, 0.0))
        cums_t = _dot_exact_rhs(u_t, tri_t)
        cums_end = jnp.broadcast_to(cums_t[:, end:end + 1], cums_t.shape)
        p1_t = jnp.exp2(cums_t)
        p2_t = jnp.exp2(cums_end - cums_t) * val_t
        packed = jnp.concatenate([cums_t, p1_t, p2_t, val_t], axis=0).T
        ex = _dot(packed.astype(BF16), expand)
        ex1 = ex[:, :D_MODEL]
        ex2 = ex[:, D_MODEL:]
        chunk_decay = _dot_exact_rhs(jnp.broadcast_to(packed[end:end + 1, :], (8, LANES)),
                                     expand[:, :D_MODEL])[0:1, :]

        xsf = xs.astype(F32)
        xs_w = (xsf * ex2).astype(BF16)
        src_term = jnp.log(val_t[0:SSD_HEADS, :]) * LOG2E - cums_t[0:SSD_HEADS, :]
        for g in range(SSD_GROUPS):
            gs = slice(g * GROUP_WIDTH, (g + 1) * GROUP_WIDTH)
            s_new_all[b].append(chunk_decay[:, gs] * s_old[b][g] + _dot(bg_ts[g], xs_w[:, gs]))
        pair_lhs = []
        for pair in range(SSD_HEADS // 2):
            cb = cbs[pair // (HEADS_PER_GROUP // 2)]
            m_mats = []
            for h in (2 * pair, 2 * pair + 1):
                seg = jnp.broadcast_to(packed[:, h:h + 1], full) + src_term[h:h + 1, :]
                m_mats.append((cb * jnp.exp2(jnp.where(allowed, seg, -jnp.inf))).astype(BF16))
            pair_lhs.append(jnp.concatenate(m_mats, axis=1))
        y_diag = jnp.concatenate([_dot(pair_lhs[p], xs_rhs[p]) for p in range(SSD_HEADS // 2)], axis=1)
        y_off = jnp.concatenate([_dot(cgs[g], s_old[b][g].astype(BF16)) for g in range(SSD_GROUPS)], axis=1)
        y_ssd = y_diag + y_off * ex1

        bcum_t = cums_t[GATE_F0:GATE_END, :]
        ip_t = val_t[GATE_I0:GATE_F0, :] * LOG2E
        rep = bcum_t.shape
        g_rep = jnp.broadcast_to(bcum_t[:, end:end + 1], rep)
        a_t = g_rep - bcum_t + ip_t
        a_max = jnp.broadcast_to(jnp.max(a_t, axis=1, keepdims=True), rep)
        w_t = jnp.exp2(a_t - a_max)
        m_prev = m_old[b]
        m_new = jnp.maximum(g_rep + m_prev, a_max)
        s_prev = jnp.exp2(g_rep + m_prev - m_new)
        s_new = jnp.exp2(a_max - m_new)
        r_t = ip_t - bcum_t
        v_heads = [v_ref[b, :, h * MLSTM_DV:(h + 1) * MLSTM_DV] for h in range(MLSTM_HEADS)]
        for pair in range(MLSTM_HEADS // 2):
            h0, h1 = 2 * pair, 2 * pair + 1
            cst = cst_old[b][pair]
            w_rows = jnp.where(top_half, w_t[h0:h0 + 1, :], w_t[h1:h1 + 1, :])
            kw = (k_pair_ts[pair] * w_rows).astype(BF16)
            full_kv = _dot(kw, jnp.concatenate([v_heads[h0], v_heads[h1], ones_blk], axis=1))
            kvn = jnp.concatenate(
                [jnp.where(top_half, full_kv[:, :MLSTM_DV], full_kv[:, MLSTM_DV:2 * MLSTM_DV]),
                 full_kv[:, 2 * MLSTM_DV:]], axis=1)
            sp_rows = jnp.where(top_half, s_prev[h0:h0 + 1, :], s_prev[h1:h1 + 1, :])
            sn_rows = jnp.where(top_half, s_new[h0:h0 + 1, :], s_new[h1:h1 + 1, :])
            cst_new_all[b].append(jnp.concatenate([sp_rows, sp_rows], axis=1) * cst
                                  + jnp.concatenate([sn_rows, sn_rows], axis=1) * kvn)
        h_heads = []
        for pair in range(MLSTM_HEADS // 2):
            h0, h1 = 2 * pair, 2 * pair + 1
            q_pair_f = q_pairs[pair].astype(F32)
            cst_b = cst_old[b][pair].astype(BF16)
            for hh, h in enumerate((h0, h1)):
                keep = left_half if hh == 0 else right_half
                vh = v_heads[h]
                qk = qks[h]
                bc = jnp.broadcast_to(packed[:, GATE_F0 + h:GATE_F0 + h + 1], full)
                dlog = jnp.where(allowed, bc + r_t[h:h + 1, :], -jnp.inf)
                m_intra = jnp.broadcast_to(jnp.max(dlog, axis=1, keepdims=True), full)
                m_inter = bc + m_prev[h:h + 1, :]
                m_t = jnp.maximum(m_inter, m_intra)
                s_mat = (qk * jnp.exp2(dlog - m_t)).astype(BF16)
                qs = (jnp.where(keep, q_pair_f, 0.0) * jnp.exp2(m_inter - m_t)).astype(BF16)
                tot = _dot(jnp.concatenate([s_mat, qs], axis=1),
                           jnp.concatenate([jnp.concatenate([vh, ones_blk], axis=1), cst_b], axis=0))
                num = tot[:, :MLSTM_DV]
                den = tot[:, MLSTM_DV:]
                h_heads.append(num / jnp.maximum(jnp.abs(den), jnp.exp2(-m_t)))
        m_new_all[b] = m_new
        h_ml = jnp.concatenate(h_heads, axis=1)
        if not final:
            return y_ssd.astype(BF16), h_ml.astype(BF16)

        y_tot = y_ssd + yb_ref[b].astype(F32) + dskip_ref[...] * xsf
        y2 = y_tot * z_gate
        y_n = y2 * lax.rsqrt(jnp.mean(y2 * y2, axis=-1, keepdims=True) + RMS_EPS) * ngs_ref[...]
        h_tot = h_ml + hb_ref[b].astype(F32)
        segs = []
        for h in range(MLSTM_HEADS):
            seg = h_tot[:, h * MLSTM_DV:(h + 1) * MLSTM_DV]
            segs.append(seg * lax.rsqrt(jnp.mean(seg * seg, axis=-1, keepdims=True) + RMS_EPS))
        h_n = jnp.concatenate(segs, axis=1) * ngm_ref[...]
        y_ml = o_gate * h_n
        return y_n.astype(BF16), y_ml.astype(BF16)

    s_old = [[s_ref[b, g] for g in range(SSD_GROUPS)] for b in range(n_seq)]
    cst_old = [[cst_ref[b, p] for p in range(MLSTM_HEADS // 2)] for b in range(n_seq)]
    m_old = [m_ref[b] for b in range(n_seq)]
    s_new_all = [[] for _ in range(n_seq)]
    cst_new_all = [[] for _ in range(n_seq)]
    m_new_all = [None] * n_seq
    results = [one_sequence(b, independent_products(b)) for b in range(n_seq)]
    for b in range(n_seq):
        for g in range(SSD_GROUPS):
            s_ref[b, g] = s_new_all[b][g]
        for p in range(MLSTM_HEADS // 2):
            cst_ref[b, p] = cst_new_all[b][p]
        m_ref[b] = m_new_all[b]

    @pl.when(c > 0)
    def _():
        for b, (first, second) in enumerate(results):
            if final:
                ycat_ref[b, :, :D_MODEL] = first
                ycat_ref[b, :, D_MODEL:] = second
            else:
                yout_ref[b] = first
                hout_ref[b] = second


def _mixer_pass(xs, bm, cm, gates, q, k, v, gbias, alog, expand, n_chunks, *, reverse, final_inputs=None):
    final = final_inputs is not None
    n_seq = xs.shape[0]
    seq_len = (n_chunks - 1) * CHUNK

    def chunk_of(t):
        return (n_chunks - 1 - t) if reverse else t

    def pad_map(t):
        return (0, _storage_chunk(chunk_of(t), n_chunks), 0)

    def out_map(t):
        return (0, jnp.maximum(chunk_of(t) - 1, 0), 0)

    const = lambda a: pl.BlockSpec(a.shape, lambda t: (0,) * a.ndim)
    pad_spec = lambda n: pl.BlockSpec((n_seq, CHUNK, n), pad_map)
    out_spec = lambda n: pl.BlockSpec((n_seq, CHUNK, n), out_map)
    in_arrays = [xs, bm, cm, gates, q, k, v]
    in_specs = [pad_spec(a.shape[2]) for a in in_arrays]
    if final:
        z, o, yb, hb, dskip, ngs, ngm = final_inputs
        in_arrays += [z, o, yb, hb]
        in_specs += [pad_spec(1024), pad_spec(1024), out_spec(1024), out_spec(1024)]
        in_arrays += [gbias, alog, expand, dskip, ngs, ngm]
        in_specs += [const(a) for a in (gbias, alog, expand, dskip, ngs, ngm)]
        out_shape = [jax.ShapeDtypeStruct((n_seq, seq_len, 2 * D_MODEL), BF16)]
        out_specs = [out_spec(2 * D_MODEL)]
    else:
        in_arrays += [gbias, alog, expand]
        in_specs += [const(a) for a in (gbias, alog, expand)]
        out_shape = [jax.ShapeDtypeStruct((n_seq, seq_len, D_MODEL), BF16),
                     jax.ShapeDtypeStruct((n_seq, seq_len, D_MODEL), BF16)]
        out_specs = [out_spec(D_MODEL), out_spec(D_MODEL)]
    return pl.pallas_call(
        functools.partial(_mixer_kernel, reverse=reverse, final=final, n_chunks=n_chunks, n_seq=n_seq),
        grid=(n_chunks,),
        in_specs=in_specs,
        out_specs=out_specs,
        out_shape=out_shape,
        scratch_shapes=[pltpu.VMEM((n_seq, SSD_GROUPS, SSD_STATE, GROUP_WIDTH), F32),
                        pltpu.VMEM((n_seq, MLSTM_HEADS // 2, 2 * MLSTM_DK, 2 * MLSTM_DV), F32),
                        pltpu.VMEM((n_seq, MLSTM_HEADS, LANES), F32)],
        compiler_params=pltpu.CompilerParams(
            dimension_semantics=("arbitrary",), vmem_limit_bytes=VMEM_LIMIT_BYTES),
        name="mixer_fwd" if final else "mixer_bwd",
    )(*in_arrays)


def _epilogue_kernel(x_ref, ycat_ref, wout_ref, lng0_ref, lnb0_ref, lng1_ref, lnb1_ref,
                     wrh_ref, wrl_ref, br_ref, after_ref, h1_ref, h1p_ref, sel_ref, gate_ref, cnt_ref):
    del after_ref
    h0 = _layer_norm(x_ref[...], lng0_ref[...], lnb0_ref[...])
    mix = _dot(ycat_ref[...], wout_ref[...])
    h1 = _layer_norm(DEEPNORM_ALPHA * h0 + mix, lng1_ref[...], lnb1_ref[...])
    h1_ref[...] = h1
    h1p_ref[...] = _pack_bf16_pairs(h1)
    hh = h1.astype(BF16)
    hl = (h1 - hh.astype(F32)).astype(BF16)
    both = _dot(hh, jnp.concatenate([wrh_ref[...], wrl_ref[...]], axis=1))
    logits = both[:, :LANES] + both[:, LANES:] + _dot(hl, wrh_ref[...]) + br_ref[...]
    lane = lax.broadcasted_iota(jnp.int32, (1, LANES), 1)
    lane_f = lane.astype(F32)
    logits = jnp.where(lane < N_EXPERTS, logits, -jnp.inf)
    work = logits
    sel = jnp.zeros(logits.shape, jnp.bool_)
    top = None
    for _ in range(TOP_K):
        m = jnp.max(work, axis=-1, keepdims=True)
        if top is None:
            top = m
        first = jnp.min(jnp.where(work == m, lane_f, float(LANES)), axis=-1, keepdims=True)
        pick = lane_f == first
        sel = jnp.logical_or(sel, pick)
        work = jnp.where(pick, -jnp.inf, work)
    e = jnp.where(sel, jnp.exp(logits - top), 0.0)
    gate_ref[...] = e / jnp.sum(e, axis=-1, keepdims=True)
    sel_f = sel.astype(F32)
    sel_ref[...] = sel_f

    @pl.when(pl.program_id(0) == 0)
    def _():
        cnt_ref[...] = jnp.zeros_like(cnt_ref)

    cnt_ref[0:1, :] = cnt_ref[0:1, :] + jnp.sum(sel_f, axis=0, keepdims=True)


def _epilogue(x, ycat, ycat_row0, w_out, lng0, lnb0, lng1, lnb1, wrh, wrl, br, after):
    rows = x.shape[0]
    tm = EPILOGUE_ROWS
    assert rows % tm == 0 and ycat_row0 % tm == 0
    tile0 = ycat_row0 // tm
    row_spec = lambda n: pl.BlockSpec((tm, n), lambda i: (i, 0))
    const = lambda a: pl.BlockSpec(a.shape, lambda i: (0,) * a.ndim)
    out_shape = [jax.ShapeDtypeStruct((rows, D_MODEL), F32),
                 jax.ShapeDtypeStruct((rows, D_MODEL // 2), jnp.uint32),
                 jax.ShapeDtypeStruct((rows, LANES), F32),
                 jax.ShapeDtypeStruct((rows, LANES), F32),
                 jax.ShapeDtypeStruct((8, LANES), F32)]
    consts = (w_out, lng0, lnb0, lng1, lnb1, wrh, wrl, br, after)
    return pl.pallas_call(
        _epilogue_kernel,
        grid=(rows // tm,),
        in_specs=[row_spec(D_MODEL), pl.BlockSpec((tm, 2 * D_MODEL), lambda i: (tile0 + i, 0))]
                 + [const(a) for a in consts],
        out_specs=[row_spec(D_MODEL), row_spec(D_MODEL // 2), row_spec(LANES), row_spec(LANES),
                   pl.BlockSpec((8, LANES), lambda i: (0, 0))],
        out_shape=out_shape,
        compiler_params=pltpu.CompilerParams(
            dimension_semantics=("arbitrary",), vmem_limit_bytes=VMEM_LIMIT_BYTES),
        name="epilogue",
    )(x, ycat, *consts)


def _rank_kernel(sel_ref, gate_ref, cnt_ref, lstrict_ref, ucum_ref, dest_ref, gk_ref, stats_ref,
                 base_ref, *, trash_row):
    i = pl.program_id(0)
    sel = sel_ref[...]
    colsum = jnp.sum(sel, axis=0, keepdims=True)

    @pl.when(i == 0)
    def _():
        counts = cnt_ref[0:1, :]
        padded = jnp.ceil(counts / MOE_BLOCK) * MOE_BLOCK
        pend = _dot_exact_rhs(jnp.broadcast_to(padded, (8, LANES)), ucum_ref[...])[0:1, :]
        stats_ref[0:1, :] = counts
        stats_ref[1:2, :] = pend - padded
        stats_ref[2:3, :] = pend
        stats_ref[3:8, :] = jnp.zeros((5, LANES), F32)
        base_ref[...] = jnp.broadcast_to(pend - padded, base_ref.shape)

    before = _dot(lstrict_ref[...], sel.astype(BF16))
    pos = base_ref[0:1, :] + before
    base_ref[0:1, :] = base_ref[0:1, :] + colsum
    work = jnp.where(sel > 0.0, pos + 1.0, 0.0)
    gates = gate_ref[...]
    for kk in range(TOP_K):
        m = jnp.max(work, axis=-1, keepdims=True)
        pick = jnp.logical_and(work == m, m > 0.0)
        gk_ref[:, kk:kk + 1] = jnp.sum(jnp.where(pick, gates, 0.0), axis=-1, keepdims=True)
        dest_ref[:, kk:kk + 1] = jnp.where(m > 0.0, m - 1.0, float(trash_row)).astype(jnp.int32)
        work = jnp.where(pick, 0.0, work)


def _rank(sel, gates, counts, lstrict, ucum, trash_row):
    rows = sel.shape[0]
    tm = RANK_ROWS
    assert rows % tm == 0
    row_spec = lambda n: pl.BlockSpec((tm, n), lambda i: (i, 0))
    const = lambda a: pl.BlockSpec(a.shape, lambda i: (0,) * a.ndim)
    return pl.pallas_call(
        functools.partial(_rank_kernel, trash_row=trash_row),
        grid=(rows // tm,),
        in_specs=[row_spec(LANES), row_spec(LANES), const(counts), const(lstrict), const(ucum)],
        out_specs=[row_spec(TOP_K), row_spec(TOP_K), pl.BlockSpec((8, LANES), lambda i: (0, 0))],
        out_shape=[jax.ShapeDtypeStruct((rows, TOP_K), jnp.int32),
                   jax.ShapeDtypeStruct((rows, TOP_K), F32),
                   jax.ShapeDtypeStruct((8, LANES), F32)],
        scratch_shapes=[pltpu.VMEM((8, LANES), F32)],
        compiler_params=pltpu.CompilerParams(
            dimension_semantics=("arbitrary",), vmem_limit_bytes=VMEM_LIMIT_BYTES),
        name="rank",
    )(sel, gates, counts, lstrict, ucum)


def _sc_mesh():
    return plsc.VectorSubcoreMesh(core_axis_name="c", subcore_axis_name="s",
                                  num_cores=SC_CORES, num_subcores=SC_SUBCORES)


def _sc_scatter_rows(src, idx, n_out_rows):
    n_src, d = src.shape
    w, k = SC_WINDOW, SC_SCATTER_INFLIGHT
    assert idx.shape == (n_src // w, TOP_K, w) and n_src % (w * k * SC_WORKERS) == 0
    per_worker = n_src // (w * SC_WORKERS)
    idx2d = idx.reshape(n_src // w * TOP_K, w)

    @functools.partial(
        pl.kernel, mesh=_sc_mesh(),
        out_type=jax.ShapeDtypeStruct((n_out_rows, d), src.dtype),
        scratch_types=[pltpu.VMEM((k * TOP_K, w), jnp.int32), pltpu.VMEM((k, w, d), src.dtype),
                       pltpu.SemaphoreType.DMA((k,)), pltpu.SemaphoreType.DMA((k,))],
        name="sc_scatter_rows")
    def body(src_hbm, idx_hbm, out_hbm, idx_v, rows_v, sem_load, sem_scatter):
        wid = lax.axis_index("s") * SC_CORES + lax.axis_index("c")

        @pl.loop(0, per_worker // k)
        def _(i):
            win0 = wid * per_worker + i * k
            loads = [pltpu.async_copy(src_hbm.at[pl.ds(pl.multiple_of((win0 + b) * w, w), w)], rows_v.at[b],
                                      sem_load.at[b]) for b in range(k)]
            pltpu.sync_copy(idx_hbm.at[pl.ds(pl.multiple_of(win0 * TOP_K, TOP_K), k * TOP_K)], idx_v)
            scatters = []
            for b in range(k):
                loads[b].wait()
                scatters += [pltpu.async_copy(rows_v.at[b], out_hbm.at[idx_v.at[b * TOP_K + kk]],
                                              sem_scatter.at[b]) for kk in range(TOP_K)]
            for copy in scatters:
                copy.wait()

    return body(src, idx2d)


def _sc_gather_rows(table, idx):
    d = table.shape[1]
    w, k = SC_WINDOW, SC_GATHER_INFLIGHT
    assert idx.shape[0] % (w * k * SC_WORKERS) == 0
    per_worker = idx.shape[0] // (w * SC_WORKERS)
    idx2d = idx.reshape(-1, w)

    @functools.partial(
        pl.kernel, mesh=_sc_mesh(),
        out_type=jax.ShapeDtypeStruct((idx.shape[0], d), table.dtype),
        scratch_types=[pltpu.VMEM((k, w), jnp.int32), pltpu.VMEM((k, w, d), table.dtype),
                       pltpu.SemaphoreType.DMA((k,)), pltpu.SemaphoreType.DMA((k,))],
        name="sc_gather_rows")
    def body(table_hbm, idx_hbm, out_hbm, idx_v, rows_v, sem_gather, sem_store):
        wid = lax.axis_index("s") * SC_CORES + lax.axis_index("c")

        @pl.loop(0, per_worker // k)
        def _(i):
            win0 = wid * per_worker + i * k
            pltpu.sync_copy(idx_hbm.at[pl.ds(pl.multiple_of(win0, k), k)], idx_v)
            gathers = [pltpu.async_copy(table_hbm.at[idx_v.at[b]], rows_v.at[b], sem_gather.at[b])
                       for b in range(k)]
            stores = []
            for b in range(k):
                gathers[b].wait()
                stores.append(pltpu.async_copy(
                    rows_v.at[b], out_hbm.at[pl.ds(pl.multiple_of((win0 + b) * w, w), w)], sem_store.at[b]))
            for copy in stores:
                copy.wait()

    return body(table, idx2d)


def _ffn_kernel(blk_ref, exp_ref, valid_ref, first_ref, next_ref, slot_ref, nused_ref,
                xb_ref, w1_hbm, b1_ref, w2_hbm, b2_ref, yb_ref, w1f_ref, w2f_ref, w1b_ref, w2b_ref, sems):
    j = pl.program_id(0)
    active = j < nused_ref[0]

    def weight_copies(e, slot):
        return (pltpu.make_async_copy(w1_hbm.at[e], w1f_ref.at[slot], sems.at[0, slot]),
                pltpu.make_async_copy(w2_hbm.at[e], w2f_ref.at[slot], sems.at[1, slot]))

    @pl.when(jnp.logical_and(active, first_ref[j] == 1))
    def _():
        e = exp_ref[j]
        slot = slot_ref[j]
        nxt = next_ref[j]

        @pl.when(j == 0)
        def _():
            for copy in weight_copies(e, slot):
                copy.start()

        @pl.when(nxt >= 0)
        def _():
            for copy in weight_copies(nxt, 1 - slot):
                copy.start()

        for copy in weight_copies(e, slot):
            copy.wait()
        w1b_ref[...] = w1f_ref[slot].astype(BF16)
        w2b_ref[...] = w2f_ref[slot].astype(BF16)

    @pl.when(active)
    def _():
        row = lax.broadcasted_iota(jnp.int32, (MOE_BLOCK, 1), 0)
        x = jnp.where(row < valid_ref[j], _unpack_bf16_pairs(xb_ref[...]), 0.0).astype(BF16)
        hc = _dot(x, w1b_ref[...]) + b1_ref[0]
        gate = jnp.minimum(hc[:, :D_FF], SWIGLU_LIMIT)
        up = jnp.clip(hc[:, D_FF:], -SWIGLU_LIMIT, SWIGLU_LIMIT)
        act = (up + 1.0) * gate * _sigmoid(SWIGLU_ALPHA * gate)
        yb_ref[...] = _pack_bf16_pairs(_dot(act.astype(BF16), w2b_ref[...]) + b2_ref[0])


def _ffn(blk_idx, blk_exp, blk_valid, blk_first, blk_next, blk_slot, n_used, xb, w1, b1, w2, b2, n_blocks):
    bm = MOE_BLOCK
    grid_spec = pltpu.PrefetchScalarGridSpec(
        num_scalar_prefetch=7,
        grid=(n_blocks,),
        in_specs=[pl.BlockSpec((bm, D_MODEL // 2), lambda j, bi, be, *_: (bi[j], 0)),
                  pl.BlockSpec(memory_space=pl.ANY),
                  pl.BlockSpec((1, 1, 2 * D_FF), lambda j, bi, be, *_: (be[j], 0, 0)),
                  pl.BlockSpec(memory_space=pl.ANY),
                  pl.BlockSpec((1, 1, D_MODEL), lambda j, bi, be, *_: (be[j], 0, 0))],
        out_specs=pl.BlockSpec((bm, D_MODEL // 2), lambda j, bi, be, *_: (bi[j], 0)),
        scratch_shapes=[pltpu.VMEM((2, D_MODEL, 2 * D_FF), F32), pltpu.VMEM((2, D_FF, D_MODEL), F32),
                        pltpu.VMEM((D_MODEL, 2 * D_FF), BF16), pltpu.VMEM((D_FF, D_MODEL), BF16),
                        pltpu.SemaphoreType.DMA((2, 2))],
    )
    return pl.pallas_call(
        _ffn_kernel,
        grid_spec=grid_spec,
        out_shape=jax.ShapeDtypeStruct(xb.shape, jnp.uint32),
        compiler_params=pltpu.CompilerParams(
            dimension_semantics=("arbitrary",), vmem_limit_bytes=VMEM_LIMIT_BYTES),
        name="expert_ffn",
    )(blk_idx, blk_exp, blk_valid, blk_first, blk_next, blk_slot, n_used, xb, w1, b1, w2, b2)


def _combine_kernel(gk_ref, h1_ref, y0_ref, y1_ref, y2_ref, y3_ref, lng_ref, lnb_ref, out_ref):
    gk = gk_ref[...]
    ffn = gk[:, 0:1] * _unpack_bf16_pairs(y0_ref[...])
    for kk, y_ref in enumerate((y1_ref, y2_ref, y3_ref), start=1):
        ffn = ffn + gk[:, kk:kk + 1] * _unpack_bf16_pairs(y_ref[...])
    out_ref[...] = _layer_norm(DEEPNORM_ALPHA * h1_ref[...] + ffn, lng_ref[...], lnb_ref[...])


def _combine(gk, h1, ysel, lng, lnb):
    rows = h1.shape[0]
    tm = COMBINE_ROWS
    assert rows % tm == 0
    n_tiles = rows // tm
    const = lambda a: pl.BlockSpec(a.shape, lambda i: (0,) * a.ndim)
    ksel = lambda kk: pl.BlockSpec((tm, D_MODEL // 2), lambda i: (kk * n_tiles + i, 0))
    return pl.pallas_call(
        _combine_kernel,
        grid=(n_tiles,),
        in_specs=[pl.BlockSpec((tm, TOP_K), lambda i: (i, 0)),
                  pl.BlockSpec((tm, D_MODEL), lambda i: (i, 0)),
                  ksel(0), ksel(1), ksel(2), ksel(3),
                  const(lng), const(lnb)],
        out_specs=pl.BlockSpec((tm, D_MODEL), lambda i: (i, 0)),
        out_shape=jax.ShapeDtypeStruct((rows, D_MODEL), F32),
        compiler_params=pltpu.CompilerParams(
            dimension_semantics=("arbitrary",), vmem_limit_bytes=VMEM_LIMIT_BYTES),
        name="combine",
    )(gk, h1, ysel, ysel, ysel, ysel, lng, lnb)


def _row(v, width=None):
    v = v.reshape(1, -1).astype(F32)
    if width is not None and v.shape[1] < width:
        v = jnp.pad(v, ((0, 0), (0, width - v.shape[1])))
    return v


def _encode_all(x_a, x_b, meta_tokens, ln_emb_g, ln_emb_b, w_in, conv_w, conv_b, dt_bias, a_log,
                d_skip, ssd_norm_g, i_bias, f_bias, mlstm_norm_g, w_out, ln1_g, ln1_b, w_router, b_router,
                w1, b1, w2, b2, ln2_g, ln2_b):
    n_a, seq_len, _ = x_a.shape
    n_seq = n_a + x_b.shape[0]
    rows_a = n_a * seq_len
    assert seq_len % CHUNK == 0
    n_chunks = seq_len // CHUNK + 1
    n_tok = n_seq * seq_len

    sizes = (1024, CONV_CH, 2 * SSD_HEADS, 512, 512, 1024, 1024, 2 * MLSTM_HEADS, 2 * MLSTM_HEADS)
    offs = [0]
    for s in sizes:
        offs.append(offs[-1] + s)
    w_z, w_xbc, w_dt, w_q, w_k, w_v, w_o, w_i, w_f = [w_in[:, offs[j]:offs[j + 1]] for j in range(9)]
    w_big = jnp.concatenate([w_z, w_xbc, w_q, w_k * (MLSTM_DK ** -0.5), w_v, w_o], axis=1).astype(BF16)
    zpad = jnp.zeros((D_MODEL, LANES - GATE_END), F32)
    gate_cols = []
    for d in range(2):
        gate_cols += [w_dt[:, d * SSD_HEADS:(d + 1) * SSD_HEADS],
                      w_i[:, d * MLSTM_HEADS:(d + 1) * MLSTM_HEADS],
                      w_f[:, d * MLSTM_HEADS:(d + 1) * MLSTM_HEADS], zpad]
    w_gates = jnp.concatenate(gate_cols, axis=1).astype(BF16)
    gbias = [_row(jnp.concatenate([dt_bias[d], i_bias[d], f_bias[d]]), LANES) for d in range(2)]
    alog = [jnp.pad(jnp.broadcast_to(a_log[d].astype(F32)[:, None], (SSD_HEADS, LANES)),
                    ((0, GATE_END - SSD_HEADS), (0, 0))) for d in range(2)]
    head_of_col = jnp.arange(D_MODEL, dtype=jnp.int32) // SSD_HEAD_DIM
    lane_id = jnp.arange(LANES, dtype=jnp.int32)[:, None]
    expand = jnp.concatenate([lane_id == GATE_END + head_of_col[None, :],
                              lane_id == 2 * GATE_END + head_of_col[None, :]], axis=1).astype(BF16)
    dskip = _row(jnp.repeat(d_skip, SSD_HEAD_DIM))
    conv_w8 = jnp.pad(conv_w.astype(F32), ((0, 8 - CONV_W), (0, 0)))
    meta_tile = jnp.pad(meta_tokens.astype(F32), ((PAD_FRONT, INPROJ_ROWS - CHUNK), (0, 0)))

    z, xbc, q, k, v, o, gates_f, gates_b = _inproj(x_a, x_b, meta_tile, _row(ln_emb_g), _row(ln_emb_b),
                                                   w_big, w_gates)
    xs, bm, cm = _conv(xbc, conv_w8, _row(conv_b), n_chunks)
    yb, hb = _mixer_pass(xs, bm, cm, gates_b, q, k, v, gbias[1], alog[1], expand, n_chunks, reverse=True)
    (ycat,) = _mixer_pass(xs, bm, cm, gates_f, q, k, v, gbias[0], alog[0], expand, n_chunks,
                          reverse=False,
                          final_inputs=(z, o, yb, hb, dskip, _row(ssd_norm_g), _row(mlstm_norm_g)))

    wr = jnp.pad(w_router.astype(F32), ((0, 0), (0, LANES - N_EXPERTS)))
    wrh = wr.astype(BF16)
    wrl = (wr - wrh.astype(F32)).astype(BF16)
    w_out_b = w_out.astype(BF16)
    r_i = jnp.arange(RANK_ROWS, dtype=jnp.int32)
    lstrict = (r_i[None, :] < r_i[:, None]).astype(BF16)
    l_i = jnp.arange(LANES, dtype=jnp.int32)
    ucum = (l_i[:, None] <= l_i[None, :]).astype(BF16)
    ycat2d = ycat.reshape(n_tok, 2 * D_MODEL)
    outs = []
    after = jnp.zeros((8, LANES), F32)
    for x_part, row0 in ((x_a, 0), (x_b, rows_a)):
        n_part = x_part.shape[0] * seq_len
        h1, h1p, sel, gates, expert_counts = _epilogue(x_part.reshape(n_part, D_MODEL), ycat2d, row0, w_out_b,
                                        _row(ln_emb_g), _row(ln_emb_b), _row(ln1_g), _row(ln1_b),
                                        wrh, wrl, _row(b_router, LANES), after)
        n_blocks = n_part * TOP_K // MOE_BLOCK + N_EXPERTS
        trash_row = n_blocks * MOE_BLOCK
        dest, gk, stats = _rank(sel, gates, expert_counts, lstrict, ucum, trash_row)
        after = stats
        counts = stats[0, :N_EXPERTS].astype(jnp.int32)
        starts = stats[1, :N_EXPERTS].astype(jnp.int32)
        pends = stats[2, :N_EXPERTS].astype(jnp.int32)
        n_used = pends[N_EXPERTS - 1] // MOE_BLOCK
        blk = jnp.minimum(jnp.arange(n_blocks, dtype=jnp.int32), jnp.maximum(n_used - 1, 0))
        blk_exp = jnp.minimum(
            jnp.sum((pends[None, :] <= (blk * MOE_BLOCK)[:, None]).astype(jnp.int32), axis=1),
            N_EXPERTS - 1).astype(jnp.int32)
        blk_last = jnp.take(starts + counts, blk_exp)
        blk_valid = jnp.clip(blk_last - blk * MOE_BLOCK, 0, MOE_BLOCK).astype(jnp.int32)
        dest_km = dest.T.reshape(-1)
        dest_wm = dest.reshape(n_part // SC_WINDOW, SC_WINDOW, TOP_K).transpose(0, 2, 1)
        xb = _sc_scatter_rows(h1p, dest_wm, trash_row + MOE_BLOCK)
        j_i = jnp.arange(n_blocks, dtype=jnp.int32)
        blk_first = jnp.logical_and(j_i < n_used, jnp.logical_or(j_i == 0, blk_exp != jnp.roll(blk_exp, 1)))
        blk_slot = ((jnp.cumsum(blk_first.astype(jnp.int32)) - 1) % 2).astype(jnp.int32)
        run_end = jnp.take(pends, blk_exp) // MOE_BLOCK
        blk_next = jnp.where(run_end < n_used, jnp.take(blk_exp, jnp.minimum(run_end, n_blocks - 1)),
                             -1).astype(jnp.int32)
        yexp = _ffn(blk, blk_exp, blk_valid, blk_first.astype(jnp.int32), blk_next, blk_slot,
                    n_used.reshape(1), xb, w1, b1.reshape(N_EXPERTS, 1, -1), w2,
                    b2.reshape(N_EXPERTS, 1, -1), n_blocks)
        ysel = _sc_gather_rows(yexp, dest_km)
        outs.append(_combine(gk, h1, ysel, _row(ln2_g), _row(ln2_b)))
    return outs


def kernel(x_prompt, x_sample, meta_tokens, ln_emb_g, ln_emb_b, w_in, conv_w, conv_b, dt_bias, a_log,
           d_skip, ssd_norm_g, i_bias, f_bias, mlstm_norm_g, w_out, ln1_g, ln1_b, w_router, b_router,
           w1, b1, w2, b2, ln2_g, ln2_b):
    assert x_prompt.shape[1:] == x_sample.shape[1:]
    n_p, seq_len, d = x_prompt.shape
    n_s = x_sample.shape[0]
    y_p, y_s = _encode_all(x_prompt.astype(F32), x_sample.astype(F32), meta_tokens, ln_emb_g, ln_emb_b, w_in[0], conv_w[0],
                           conv_b[0], dt_bias[0], a_log[0], d_skip[0], ssd_norm_g[0], i_bias[0], f_bias[0],
                           mlstm_norm_g[0], w_out[0], ln1_g[0], ln1_b[0], w_router[0], b_router[0],
                           w1[0], b1[0], w2[0], b2[0], ln2_g[0], ln2_b[0])
    return (y_p.reshape(n_p, seq_len, d), y_s.reshape(n_s, seq_len, d))
```

```python
import functools

import jax
import jax.numpy as jnp
from jax import lax
from jax.experimental import pallas as pl
from jax.experimental.pallas import tpu as pltpu
from jax.experimental.pallas import tpu_sc as plsc

F32 = jnp.float32
BF16 = jnp.bfloat16

D_MODEL = 1024
N_META = 16
CHUNK = 128
PAD_FRONT = CHUNK - N_META
SSD_HEADS = 16
SSD_HEAD_DIM = 64
SSD_GROUPS = 4
SSD_STATE = 128
HEADS_PER_GROUP = SSD_HEADS // SSD_GROUPS
GROUP_WIDTH = HEADS_PER_GROUP * SSD_HEAD_DIM
CONV_W = 5
CONV_HALF = CONV_W // 2
CONV_CH = D_MODEL + 2 * SSD_GROUPS * SSD_STATE
MLSTM_HEADS = 8
MLSTM_DK = 64
MLSTM_DV = 128
N_EXPERTS = 32
TOP_K = 4
D_FF = D_MODEL
SWIGLU_LIMIT = 7.0
SWIGLU_ALPHA = 1.702
DEEPNORM_ALPHA = 2.0 ** 0.25
LN_EPS = 1e-5
RMS_EPS = 1e-5
NEG_GATE = -1e30
LOG2E = 1.4426950408889634

LANES = 128
BF16_SUBLANES = 16
VMEM_LIMIT_BYTES = 56 * 1024 * 1024

GATE_DT0, GATE_I0, GATE_F0, GATE_END = 0, SSD_HEADS, SSD_HEADS + MLSTM_HEADS, SSD_HEADS + 2 * MLSTM_HEADS

INPROJ_ROWS = 512
EPILOGUE_ROWS = 512
RANK_ROWS = 512
COMBINE_ROWS = 512
MOE_BLOCK = 512

SC_CORES = 2
SC_SUBCORES = 16
SC_WORKERS = SC_CORES * SC_SUBCORES
SC_WINDOW = 32
SC_SCATTER_INFLIGHT = 2
SC_GATHER_INFLIGHT = 4


def _dot(a, b):
    return jnp.dot(a, b, preferred_element_type=F32)


def _dot_nt(a, b):
    return lax.dot_general(a, b, (((1,), (1,)), ((), ())), preferred_element_type=F32)


def _split3(x):
    hi = x.astype(BF16)
    r1 = x - hi.astype(F32)
    mid = r1.astype(BF16)
    lo = (r1 - mid.astype(F32)).astype(BF16)
    return hi, mid, lo


def _dot_exact_lhs(a_bf16, x):
    hi, mid, lo = _split3(x)
    return _dot(a_bf16, hi) + _dot(a_bf16, mid) + _dot(a_bf16, lo)


def _dot_exact_rhs(x, b_bf16):
    hi, mid, lo = _split3(x)
    return _dot(hi, b_bf16) + _dot(mid, b_bf16) + _dot(lo, b_bf16)


def _layer_norm(x, g, b):
    mu = jnp.mean(x, axis=-1, keepdims=True)
    xc = x - mu
    var = jnp.mean(xc * xc, axis=-1, keepdims=True)
    return xc * lax.rsqrt(var + LN_EPS) * g + b


def _sigmoid(x):
    return 1.0 / (1.0 + jnp.exp(-x))


def _log1p_exp_neg_abs(x):
    return jnp.log(1.0 + jnp.exp(-jnp.abs(x)))


def _pack_bf16_pairs(x):
    n = x.shape[1] // 2
    bits = lax.bitcast_convert_type(x.astype(BF16).astype(F32), jnp.uint32)
    return (bits[:, :n] >> 16) | bits[:, n:]


def _unpack_bf16_pairs(words):
    lo = lax.bitcast_convert_type(words << 16, F32)
    hi = lax.bitcast_convert_type(words & jnp.uint32(0xFFFF0000), F32)
    return jnp.concatenate([lo, hi], axis=1)


def _storage_chunk(c, n_chunks):
    return jnp.where(c == 0, n_chunks - 1, c - 1)


def _inproj_kernel(xa_ref, xb_ref, meta_ref, g_ref, b_ref, wbig_ref, wg_ref,
                   z_ref, xbc_ref, q_ref, k_ref, v_ref, o_ref, gf_ref, gb_ref, *, n_x_tiles, n_a):
    x = jnp.where(pl.program_id(0) < n_a, xa_ref[...], xb_ref[...])
    x = jnp.where(pl.program_id(1) == n_x_tiles, meta_ref[...], x)
    h = _layer_norm(x, g_ref[...], b_ref[...]).astype(BF16)

    def mm(c0, c1):
        return _dot(h, wbig_ref[:, c0:c1]).astype(BF16)

    z_ref[...] = mm(0, 1024)
    xbc_ref[:, 0:1024] = mm(1024, 2048)
    xbc_ref[:, 1024:2048] = mm(2048, 3072)
    q_ref[...] = mm(3072, 3584)
    k_ref[...] = mm(3584, 4096)
    v_ref[...] = mm(4096, 5120)
    o_ref[...] = mm(5120, 6144)
    gates = _dot(h, wg_ref[...])
    gf_ref[...] = gates[:, :LANES]
    gb_ref[...] = gates[:, LANES:]


def _inproj(x_a, x_b, meta_tile, ln_g, ln_b, w_big, w_gates):
    n_a, seq_len, _ = x_a.shape
    n_seq = n_a + x_b.shape[0]
    tm = INPROJ_ROWS
    assert seq_len % tm == 0 and tm >= CHUNK
    n_x_tiles = seq_len // tm
    rows = seq_len + CHUNK
    row_spec = lambda n: pl.BlockSpec((None, tm, n), lambda b, i: (b, i, 0))
    const = lambda a: pl.BlockSpec(a.shape, lambda b, i: (0,) * a.ndim)
    resident = lambda a: pl.BlockSpec(a.shape, lambda b, i: (0,) * a.ndim, pipeline_mode=pl.Buffered(1))
    widths = (1024, CONV_CH, 512, 512, 1024, 1024)
    out_shapes = [jax.ShapeDtypeStruct((n_seq, rows, w), BF16) for w in widths]
    out_shapes += [jax.ShapeDtypeStruct((n_seq, rows, LANES), F32)] * 2
    return pl.pallas_call(
        functools.partial(_inproj_kernel, n_x_tiles=n_x_tiles, n_a=n_a),
        grid=(n_seq, n_x_tiles + 1),
        in_specs=[pl.BlockSpec((None, tm, D_MODEL), lambda b, i: (
                      jnp.minimum(b, n_a - 1),
                      jnp.where(b < n_a, jnp.minimum(i, n_x_tiles - 1), n_x_tiles - 1), 0)),
                  pl.BlockSpec((None, tm, D_MODEL), lambda b, i: (
                      jnp.maximum(b - n_a, 0),
                      jnp.where(b < n_a, 0, jnp.minimum(i, n_x_tiles - 1)), 0)),
                  resident(meta_tile), const(ln_g), const(ln_b), resident(w_big), resident(w_gates)],
        out_specs=[row_spec(s.shape[2]) for s in out_shapes],
        out_shape=out_shapes,
        compiler_params=pltpu.CompilerParams(
            dimension_semantics=("arbitrary", "arbitrary"), vmem_limit_bytes=VMEM_LIMIT_BYTES),
        name="inproj",
    )(x_a, x_b, meta_tile, ln_g, ln_b, w_big, w_gates)


def _conv_kernel(prev_ref, main_ref, next_ref, shift_ref, w_ref, b_ref, xs_ref, bm_ref, cm_ref,
                 *, n_x_tiles, chunks_per_tile):
    i = pl.program_id(1)
    w = w_ref[...]
    bias = b_ref[...]
    shift = shift_ref[...]

    def conv_chunk(before, rows, after, pad_rows, j):
        shifted = _dot(shift, jnp.concatenate([before, rows, after], axis=0).astype(BF16))
        acc = bias + rows * w[CONV_HALF:CONV_HALF + 1, :]
        for jj, t in enumerate(t for t in range(CONV_W) if t != CONV_HALF):
            acc = acc + shifted[jj * CHUNK:(jj + 1) * CHUNK, :] * w[t:t + 1, :]
        y = acc * _sigmoid(acc)
        if pad_rows is not None:
            y = jnp.where(pad_rows, 0.0, y)
        r = slice(j * CHUNK, (j + 1) * CHUNK)
        xs_ref[r, :] = y[:, :D_MODEL].astype(BF16)
        bm_ref[r, :] = y[:, D_MODEL:D_MODEL + 512].astype(BF16)
        cm_ref[r, :] = y[:, D_MODEL + 512:].astype(BF16)

    @pl.when(i < n_x_tiles)
    def _():
        tile = main_ref[...].astype(F32)
        for j in range(chunks_per_tile):
            lo, hi = j * CHUNK, (j + 1) * CHUNK
            before = prev_ref[...].astype(F32) if j == 0 else tile[lo - BF16_SUBLANES:lo, :]
            if j == chunks_per_tile - 1:
                after = jnp.where(i == n_x_tiles - 1, 0.0, next_ref[...].astype(F32))
            else:
                after = tile[hi:hi + BF16_SUBLANES, :]
            conv_chunk(before, tile[lo:hi, :], after, None, j)

    @pl.when(i == n_x_tiles)
    def _():
        row = lax.broadcasted_iota(jnp.int32, (CHUNK, 1), 0)
        pad_rows = row < PAD_FRONT
        rows = jnp.where(pad_rows, 0.0, main_ref[0:CHUNK, :].astype(F32))
        conv_chunk(jnp.zeros((BF16_SUBLANES, CONV_CH), F32), rows, next_ref[...].astype(F32), pad_rows, 0)


def _conv(xbc, conv_w8, conv_b, n_chunks):
    n_seq, rows, _ = xbc.shape
    tm = INPROJ_ROWS
    seq_len = (n_chunks - 1) * CHUNK
    assert seq_len % tm == 0 and tm % CHUNK == 0
    n_x_tiles = seq_len // tm
    halo_per_tile = tm // BF16_SUBLANES
    meta_last_halo = rows // BF16_SUBLANES - 1

    def prev_map(b, i):
        before_tile = jnp.maximum(jnp.minimum(i, n_x_tiles - 1) * halo_per_tile - 1, 0)
        return (b, jnp.where(i == 0, meta_last_halo, before_tile), 0)

    def next_map(b, i):
        return (b, jnp.where(i >= n_x_tiles - 1, 0, (i + 1) * halo_per_tile) , 0)

    const = lambda a: pl.BlockSpec(a.shape, lambda b, i: (0,) * a.ndim)
    l_i = jnp.arange(CHUNK, dtype=jnp.int32)[:, None]
    j_i = jnp.arange(CHUNK + 2 * BF16_SUBLANES, dtype=jnp.int32)[None, :]
    shifts = jnp.concatenate([(j_i == BF16_SUBLANES + l_i + t - CONV_HALF)
                              for t in range(CONV_W) if t != CONV_HALF], axis=0).astype(BF16)
    out_shapes = [jax.ShapeDtypeStruct((n_seq, rows, D_MODEL), BF16),
                  jax.ShapeDtypeStruct((n_seq, rows, 512), BF16),
                  jax.ShapeDtypeStruct((n_seq, rows, 512), BF16)]
    tile_spec = lambda n: pl.BlockSpec((None, tm, n), lambda b, i: (b, i, 0))
    return pl.pallas_call(
        functools.partial(_conv_kernel, n_x_tiles=n_x_tiles, chunks_per_tile=tm // CHUNK),
        grid=(n_seq, n_x_tiles + 1),
        in_specs=[pl.BlockSpec((None, BF16_SUBLANES, CONV_CH), prev_map),
                  tile_spec(CONV_CH),
                  pl.BlockSpec((None, BF16_SUBLANES, CONV_CH), next_map),
                  const(shifts), const(conv_w8), const(conv_b)],
        out_specs=[tile_spec(s.shape[2]) for s in out_shapes],
        out_shape=out_shapes,
        compiler_params=pltpu.CompilerParams(
            dimension_semantics=("arbitrary", "arbitrary"), vmem_limit_bytes=VMEM_LIMIT_BYTES),
        name="conv",
    )(xbc, xbc, xbc, shifts, conv_w8, conv_b)


def _mixer_kernel(*refs, reverse, final, n_chunks, n_seq):
    if final:
        (xs_ref, bm_ref, cm_ref, g_ref, q_ref, k_ref, v_ref, z_ref, o_ref, yb_ref, hb_ref,
         gbias_ref, alog_ref, expand_ref, dskip_ref, ngs_ref, ngm_ref,
         ycat_ref, s_ref, cst_ref, m_ref) = refs
    else:
        (xs_ref, bm_ref, cm_ref, g_ref, q_ref, k_ref, v_ref,
         gbias_ref, alog_ref, expand_ref,
         yout_ref, hout_ref, s_ref, cst_ref, m_ref) = refs

    t = pl.program_id(0)
    c = (n_chunks - 1 - t) if reverse else t
    end = 0 if reverse else CHUNK - 1

    @pl.when(t == 0)
    def _():
        s_ref[...] = jnp.zeros_like(s_ref)
        cst_ref[...] = jnp.zeros_like(cst_ref)
        m_ref[...] = jnp.zeros_like(m_ref)

    row = lax.broadcasted_iota(jnp.int32, (CHUNK, 1), 0)
    col = lax.broadcasted_iota(jnp.int32, (1, CHUNK), 1)
    lane = col
    allowed = (col >= row) if reverse else (col <= row)
    tri = allowed.astype(BF16)
    tri_t = ((row >= col) if reverse else (row <= col)).astype(BF16)
    feat = lax.broadcasted_iota(jnp.int32, (GATE_END, 1), 0)
    is_dt = feat < GATE_I0
    is_i = jnp.logical_and(feat >= GATE_I0, feat < GATE_F0)
    is_f = feat >= GATE_F0
    pad_cols = jnp.logical_and(c == 0, col < PAD_FRONT)
    a_coef = -jnp.exp(alog_ref[...]) * LOG2E
    expand = expand_ref[...]
    left_half = lane < SSD_HEAD_DIM
    right_half = jnp.logical_not(left_half)
    top_half = row < MLSTM_DK
    ones_blk = jnp.ones((CHUNK, MLSTM_DV), BF16)
    full = (CHUNK, LANES)

    def independent_products(b):
        cgs = [cm_ref[b, :, g * SSD_STATE:(g + 1) * SSD_STATE] for g in range(SSD_GROUPS)]
        bgs = [bm_ref[b, :, g * SSD_STATE:(g + 1) * SSD_STATE] for g in range(SSD_GROUPS)]
        cbs = [_dot_nt(cgs[g], bgs[g]) for g in range(SSD_GROUPS)]
        bg_ts = [bgs[g].astype(F32).T.astype(BF16) for g in range(SSD_GROUPS)]
        q_pairs = [q_ref[b, :, p * LANES:(p + 1) * LANES] for p in range(MLSTM_HEADS // 2)]
        k_pairs = [k_ref[b, :, p * LANES:(p + 1) * LANES] for p in range(MLSTM_HEADS // 2)]
        qks = [_dot_nt(jnp.where(left_half if h % 2 == 0 else right_half, q_pairs[h // 2],
                                 jnp.zeros_like(q_pairs[h // 2])), k_pairs[h // 2])
               for h in range(MLSTM_HEADS)]
        k_pair_ts = [k_pairs[p].astype(F32).T for p in range(MLSTM_HEADS // 2)]
        xs = xs_ref[b]
        xs_rhs = []
        for p in range(SSD_HEADS // 2):
            xs_pair = xs[:, p * LANES:(p + 1) * LANES]
            zero_pair = jnp.zeros_like(xs_pair)
            xs_rhs.append(jnp.concatenate([jnp.where(left_half, xs_pair, zero_pair),
                                           jnp.where(right_half, xs_pair, zero_pair)], axis=0))
        z_gate = o_gate = None
        if final:
            zz = z_ref[b].astype(F32)
            z_gate = zz * _sigmoid(zz)
            o_gate = _sigmoid(o_ref[b].astype(F32))
        return cgs, cbs, bg_ts, q_pairs, qks, k_pair_ts, xs, xs_rhs, z_gate, o_gate

    def one_sequence(b, products):
        cgs, cbs, bg_ts, q_pairs, qks, k_pair_ts, xs, xs_rhs, z_gate, o_gate = products

        gr = (g_ref[b] + gbias_ref[...]).T[0:GATE_END, :]
        lse = _log1p_exp_neg_abs(gr)
        val_t = jnp.where(is_dt, jnp.maximum(gr, 0.0) + lse, jnp.where(is_i, gr, jnp.minimum(gr, 0.0) - lse))
        val_t = jnp.where(pad_cols, jnp.where(is_i, NEG_GATE, 0.0), val_t)
        u_t = jnp.where(is_dt, val_t * a_coef, jnp.where(is_f, val_t * LOG2E, 0.0))
        cums_t = _dot_exact_rhs(u_t, tri_t)
        cums_end = jnp.broadcast_to(cums_t[:, end:end + 1], cums_t.shape)
        p1_t = jnp.exp2(cums_t)
        p2_t = jnp.exp2(cums_end - cums_t) * val_t
        packed = jnp.concatenate([cums_t, p1_t, p2_t, val_t], axis=0).T
        ex = _dot(packed.astype(BF16), expand)
        ex1 = ex[:, :D_MODEL]
        ex2 = ex[:, D_MODEL:]
        chunk_decay = _dot_exact_rhs(jnp.broadcast_to(packed[end:end + 1, :], (8, LANES)),
                                     expand[:, :D_MODEL])[0:1, :]

        xsf = xs.astype(F32)
        xs_w = (xsf * ex2).astype(BF16)
        src_term = jnp.log(val_t[0:SSD_HEADS, :]) * LOG2E - cums_t[0:SSD_HEADS, :]
        for g in range(SSD_GROUPS):
            gs = slice(g * GROUP_WIDTH, (g + 1) * GROUP_WIDTH)
            s_new_all[b].append(chunk_decay[:, gs] * s_old[b][g] + _dot(bg_ts[g], xs_w[:, gs]))
        pair_lhs = []
        for pair in range(SSD_HEADS // 2):
            cb = cbs[pair // (HEADS_PER_GROUP // 2)]
            m_mats = []
            for h in (2 * pair, 2 * pair + 1):
                seg = jnp.broadcast_to(packed[:, h:h + 1], full) + src_term[h:h + 1, :]
                m_mats.append((cb * jnp.exp2(jnp.where(allowed, seg, -jnp.inf))).astype(BF16))
            pair_lhs.append(jnp.concatenate(m_mats, axis=1))
        y_diag = jnp.concatenate([_dot(pair_lhs[p], xs_rhs[p]) for p in range(SSD_HEADS // 2)], axis=1)
        y_off = jnp.concatenate([_dot(cgs[g], s_old[b][g].astype(BF16)) for g in range(SSD_GROUPS)], axis=1)
        y_ssd = y_diag + y_off * ex1

        bcum_t = cums_t[GATE_F0:GATE_END, :]
        ip_t = val_t[GATE_I0:GATE_F0, :] * LOG2E
        rep = bcum_t.shape
        g_rep = jnp.broadcast_to(bcum_t[:, end:end + 1], rep)
        a_t = g_rep - bcum_t + ip_t
        a_max = jnp.broadcast_to(jnp.max(a_t, axis=1, keepdims=True), rep)
        w_t = jnp.exp2(a_t - a_max)
        m_prev = m_old[b]
        m_new = jnp.maximum(g_rep + m_prev, a_max)
        s_prev = jnp.exp2(g_rep + m_prev - m_new)
        s_new = jnp.exp2(a_max - m_new)
        r_t = ip_t - bcum_t
        v_heads = [v_ref[b, :, h * MLSTM_DV:(h + 1) * MLSTM_DV] for h in range(MLSTM_HEADS)]
        for pair in range(MLSTM_HEADS // 2):
            h0, h1 = 2 * pair, 2 * pair + 1
            cst = cst_old[b][pair]
            w_rows = jnp.where(top_half, w_t[h0:h0 + 1, :], w_t[h1:h1 + 1, :])
            kw = (k_pair_ts[pair] * w_rows).astype(BF16)
            full_kv = _dot(kw, jnp.concatenate([v_heads[h0], v_heads[h1], ones_blk], axis=1))
            kvn = jnp.concatenate(
                [jnp.where(top_half, full_kv[:, :MLSTM_DV], full_kv[:, MLSTM_DV:2 * MLSTM_DV]),
                 full_kv[:, 2 * MLSTM_DV:]], axis=1)
            sp_rows = jnp.where(top_half, s_prev[h0:h0 + 1, :], s_prev[h1:h1 + 1, :])
            sn_rows = jnp.where(top_half, s_new[h0:h0 + 1, :], s_new[h1:h1 + 1, :])
            cst_new_all[b].append(jnp.concatenate([sp_rows, sp_rows], axis=1) * cst
                                  + jnp.concatenate([sn_rows, sn_rows], axis=1) * kvn)
        h_heads = []
        for pair in range(MLSTM_HEADS // 2):
            h0, h1 = 2 * pair, 2 * pair + 1
            q_pair_f = q_pairs[pair].astype(F32)
            cst_b = cst_old[b][pair].astype(BF16)
            for hh, h in enumerate((h0, h1)):
                keep = left_half if hh == 0 else right_half
                vh = v_heads[h]
                qk = qks[h]
                bc = jnp.broadcast_to(packed[:, GATE_F0 + h:GATE_F0 + h + 1], full)
                dlog = jnp.where(allowed, bc + r_t[h:h + 1, :], -jnp.inf)
                m_intra = jnp.broadcast_to(jnp.max(dlog, axis=1, keepdims=True), full)
                m_inter = bc + m_prev[h:h + 1, :]
                m_t = jnp.maximum(m_inter, m_intra)
                s_mat = (qk * jnp.exp2(dlog - m_t)).astype(BF16)
                qs = (jnp.where(keep, q_pair_f, 0.0) * jnp.exp2(m_inter - m_t)).astype(BF16)
                tot = _dot(jnp.concatenate([s_mat, qs], axis=1),
                           jnp.concatenate([jnp.concatenate([vh, ones_blk], axis=1), cst_b], axis=0))
                num = tot[:, :MLSTM_DV]
                den = tot[:, MLSTM_DV:]
                h_heads.append(num / jnp.maximum(jnp.abs(den), jnp.exp2(-m_t)))
        m_new_all[b] = m_new
        h_ml = jnp.concatenate(h_heads, axis=1)
        if not final:
            return y_ssd.astype(BF16), h_ml.astype(BF16)

        y_tot = y_ssd + yb_ref[b].astype(F32) + dskip_ref[...] * xsf
        y2 = y_tot * z_gate
        y_n = y2 * lax.rsqrt(jnp.mean(y2 * y2, axis=-1, keepdims=True) + RMS_EPS) * ngs_ref[...]
        h_tot = h_ml + hb_ref[b].astype(F32)
        segs = []
        for h in range(MLSTM_HEADS):
            seg = h_tot[:, h * MLSTM_DV:(h + 1) * MLSTM_DV]
            segs.append(seg * lax.rsqrt(jnp.mean(seg * seg, axis=-1, keepdims=True) + RMS_EPS))
        h_n = jnp.concatenate(segs, axis=1) * ngm_ref[...]
        y_ml = o_gate * h_n
        return y_n.astype(BF16), y_ml.astype(BF16)

    s_old = [[s_ref[b, g] for g in range(SSD_GROUPS)] for b in range(n_seq)]
    cst_old = [[cst_ref[b, p] for p in range(MLSTM_HEADS // 2)] for b in range(n_seq)]
    m_old = [m_ref[b] for b in range(n_seq)]
    s_new_all = [[] for _ in range(n_seq)]
    cst_new_all = [[] for _ in range(n_seq)]
    m_new_all = [None] * n_seq
    results = [one_sequence(b, independent_products(b)) for b in range(n_seq)]
    for b in range(n_seq):
        for g in range(SSD_GROUPS):
            s_ref[b, g] = s_new_all[b][g]
        for p in range(MLSTM_HEADS // 2):
            cst_ref[b, p] = cst_new_all[b][p]
        m_ref[b] = m_new_all[b]

    @pl.when(c > 0)
    def _():
        for b, (first, second) in enumerate(results):
            if final:
                ycat_ref[b, :, :D_MODEL] = first
                ycat_ref[b, :, D_MODEL:] = second
            else:
                yout_ref[b] = first
                hout_ref[b] = second


def _mixer_pass(xs, bm, cm, gates, q, k, v, gbias, alog, expand, n_chunks, *, reverse, final_inputs=None):
    final = final_inputs is not None
    n_seq = xs.shape[0]
    seq_len = (n_chunks - 1) * CHUNK

    def chunk_of(t):
        return (n_chunks - 1 - t) if reverse else t

    def pad_map(t):
        return (0, _storage_chunk(chunk_of(t), n_chunks), 0)

    def out_map(t):
        return (0, jnp.maximum(chunk_of(t) - 1, 0), 0)

    const = lambda a: pl.BlockSpec(a.shape, lambda t: (0,) * a.ndim)
    pad_spec = lambda n: pl.BlockSpec((n_seq, CHUNK, n), pad_map)
    out_spec = lambda n: pl.BlockSpec((n_seq, CHUNK, n), out_map)
    in_arrays = [xs, bm, cm, gates, q, k, v]
    in_specs = [pad_spec(a.shape[2]) for a in in_arrays]
    if final:
        z, o, yb, hb, dskip, ngs, ngm = final_inputs
        in_arrays += [z, o, yb, hb]
        in_specs += [pad_spec(1024), pad_spec(1024), out_spec(1024), out_spec(1024)]
        in_arrays += [gbias, alog, expand, dskip, ngs, ngm]
        in_specs += [const(a) for a in (gbias, alog, expand, dskip, ngs, ngm)]
        out_shape = [jax.ShapeDtypeStruct((n_seq, seq_len, 2 * D_MODEL), BF16)]
        out_specs = [out_spec(2 * D_MODEL)]
    else:
        in_arrays += [gbias, alog, expand]
        in_specs += [const(a) for a in (gbias, alog, expand)]
        out_shape = [jax.ShapeDtypeStruct((n_seq, seq_len, D_MODEL), BF16),
                     jax.ShapeDtypeStruct((n_seq, seq_len, D_MODEL), BF16)]
        out_specs = [out_spec(D_MODEL), out_spec(D_MODEL)]
    return pl.pallas_call(
        functools.partial(_mixer_kernel, reverse=reverse, final=final, n_chunks=n_chunks, n_seq=n_seq),
        grid=(n_chunks,),
        in_specs=in_specs,
        out_specs=out_specs,
        out_shape=out_shape,
        scratch_shapes=[pltpu.VMEM((n_seq, SSD_GROUPS, SSD_STATE, GROUP_WIDTH), F32),
                        pltpu.VMEM((n_seq, MLSTM_HEADS // 2, 2 * MLSTM_DK, 2 * MLSTM_DV), F32),
                        pltpu.VMEM((n_seq, MLSTM_HEADS, LANES), F32)],
        compiler_params=pltpu.CompilerParams(
            dimension_semantics=("arbitrary",), vmem_limit_bytes=VMEM_LIMIT_BYTES),
        name="mixer_fwd" if final else "mixer_bwd",
    )(*in_arrays)


def _epilogue_kernel(x_ref, ycat_ref, wout_ref, lng0_ref, lnb0_ref, lng1_ref, lnb1_ref,
                     wrh_ref, wrl_ref, br_ref, after_ref, h1_ref, h1p_ref, sel_ref, gate_ref, cnt_ref):
    del after_ref
    h0 = _layer_norm(x_ref[...], lng0_ref[...], lnb0_ref[...])
    mix = _dot(ycat_ref[...], wout_ref[...])
    h1 = _layer_norm(DEEPNORM_ALPHA * h0 + mix, lng1_ref[...], lnb1_ref[...])
    h1_ref[...] = h1
    h1p_ref[...] = _pack_bf16_pairs(h1)
    hh = h1.astype(BF16)
    hl = (h1 - hh.astype(F32)).astype(BF16)
    both = _dot(hh, jnp.concatenate([wrh_ref[...], wrl_ref[...]], axis=1))
    logits = both[:, :LANES] + both[:, LANES:] + _dot(hl, wrh_ref[...]) + br_ref[...]
    lane = lax.broadcasted_iota(jnp.int32, (1, LANES), 1)
    lane_f = lane.astype(F32)
    logits = jnp.where(lane < N_EXPERTS, logits, -jnp.inf)
    work = logits
    sel = jnp.zeros(logits.shape, jnp.bool_)
    top = None
    for _ in range(TOP_K):
        m = jnp.max(work, axis=-1, keepdims=True)
        if top is None:
            top = m
        first = jnp.min(jnp.where(work == m, lane_f, float(LANES)), axis=-1, keepdims=True)
        pick = lane_f == first
        sel = jnp.logical_or(sel, pick)
        work = jnp.where(pick, -jnp.inf, work)
    e = jnp.where(sel, jnp.exp(logits - top), 0.0)
    gate_ref[...] = e / jnp.sum(e, axis=-1, keepdims=True)
    sel_f = sel.astype(F32)
    sel_ref[...] = sel_f

    @pl.when(pl.program_id(0) == 0)
    def _():
        cnt_ref[...] = jnp.zeros_like(cnt_ref)

    cnt_ref[0:1, :] = cnt_ref[0:1, :] + jnp.sum(sel_f, axis=0, keepdims=True)


def _epilogue(x, ycat, ycat_row0, w_out, lng0, lnb0, lng1, lnb1, wrh, wrl, br, after):
    rows = x.shape[0]
    tm = EPILOGUE_ROWS
    assert rows % tm == 0 and ycat_row0 % tm == 0
    tile0 = ycat_row0 // tm
    row_spec = lambda n: pl.BlockSpec((tm, n), lambda i: (i, 0))
    const = lambda a: pl.BlockSpec(a.shape, lambda i: (0,) * a.ndim)
    out_shape = [jax.ShapeDtypeStruct((rows, D_MODEL), F32),
                 jax.ShapeDtypeStruct((rows, D_MODEL // 2), jnp.uint32),
                 jax.ShapeDtypeStruct((rows, LANES), F32),
                 jax.ShapeDtypeStruct((rows, LANES), F32),
                 jax.ShapeDtypeStruct((8, LANES), F32)]
    consts = (w_out, lng0, lnb0, lng1, lnb1, wrh, wrl, br, after)
    return pl.pallas_call(
        _epilogue_kernel,
        grid=(rows // tm,),
        in_specs=[row_spec(D_MODEL), pl.BlockSpec((tm, 2 * D_MODEL), lambda i: (tile0 + i, 0))]
                 + [const(a) for a in consts],
        out_specs=[row_spec(D_MODEL), row_spec(D_MODEL // 2), row_spec(LANES), row_spec(LANES),
                   pl.BlockSpec((8, LANES), lambda i: (0, 0))],
        out_shape=out_shape,
        compiler_params=pltpu.CompilerParams(
            dimension_semantics=("arbitrary",), vmem_limit_bytes=VMEM_LIMIT_BYTES),
        name="epilogue",
    )(x, ycat, *consts)


def _rank_kernel(sel_ref, gate_ref, cnt_ref, lstrict_ref, ucum_ref, dest_ref, gk_ref, stats_ref,
                 base_ref, *, trash_row):
    i = pl.program_id(0)
    sel = sel_ref[...]
    colsum = jnp.sum(sel, axis=0, keepdims=True)

    @pl.when(i == 0)
    def _():
        counts = cnt_ref[0:1, :]
        padded = jnp.ceil(counts / MOE_BLOCK) * MOE_BLOCK
        pend = _dot_exact_rhs(jnp.broadcast_to(padded, (8, LANES)), ucum_ref[...])[0:1, :]
        stats_ref[0:1, :] = counts
        stats_ref[1:2, :] = pend - padded
        stats_ref[2:3, :] = pend
        stats_ref[3:8, :] = jnp.zeros((5, LANES), F32)
        base_ref[...] = jnp.broadcast_to(pend - padded, base_ref.shape)

    before = _dot(lstrict_ref[...], sel.astype(BF16))
    pos = base_ref[0:1, :] + before
    base_ref[0:1, :] = base_ref[0:1, :] + colsum
    work = jnp.where(sel > 0.0, pos + 1.0, 0.0)
    gates = gate_ref[...]
    for kk in range(TOP_K):
        m = jnp.max(work, axis=-1, keepdims=True)
        pick = jnp.logical_and(work == m, m > 0.0)
        gk_ref[:, kk:kk + 1] = jnp.sum(jnp.where(pick, gates, 0.0), axis=-1, keepdims=True)
        dest_ref[:, kk:kk + 1] = jnp.where(m > 0.0, m - 1.0, float(trash_row)).astype(jnp.int32)
        work = jnp.where(pick, 0.0, work)


def _rank(sel, gates, counts, lstrict, ucum, trash_row):
    rows = sel.shape[0]
    tm = RANK_ROWS
    assert rows % tm == 0
    row_spec = lambda n: pl.BlockSpec((tm, n), lambda i: (i, 0))
    const = lambda a: pl.BlockSpec(a.shape, lambda i: (0,) * a.ndim)
    return pl.pallas_call(
        functools.partial(_rank_kernel, trash_row=trash_row),
        grid=(rows // tm,),
        in_specs=[row_spec(LANES), row_spec(LANES), const(counts), const(lstrict), const(ucum)],
        out_specs=[row_spec(TOP_K), row_spec(TOP_K), pl.BlockSpec((8, LANES), lambda i: (0, 0))],
        out_shape=[jax.ShapeDtypeStruct((rows, TOP_K), jnp.int32),
                   jax.ShapeDtypeStruct((rows, TOP_K), F32),
                   jax.ShapeDtypeStruct((8, LANES), F32)],
        scratch_shapes=[pltpu.VMEM((8, LANES), F32)],
        compiler_params=pltpu.CompilerParams(
            dimension_semantics=("arbitrary",), vmem_limit_bytes=VMEM_LIMIT_BYTES),
        name="rank",
    )(sel, gates, counts, lstrict, ucum)


def _sc_mesh():
    return plsc.VectorSubcoreMesh(core_axis_name="c", subcore_axis_name="s",
                                  num_cores=SC_CORES, num_subcores=SC_SUBCORES)


def _sc_scatter_rows(src, idx, n_out_rows):
    n_src, d = src.shape
    w, k = SC_WINDOW, SC_SCATTER_INFLIGHT
    assert idx.shape == (n_src // w, TOP_K, w) and n_src % (w * k * SC_WORKERS) == 0
    per_worker = n_src // (w * SC_WORKERS)
    idx2d = idx.reshape(n_src // w * TOP_K, w)

    @functools.partial(
        pl.kernel, mesh=_sc_mesh(),
        out_type=jax.ShapeDtypeStruct((n_out_rows, d), src.dtype),
        scratch_types=[pltpu.VMEM((k * TOP_K, w), jnp.int32), pltpu.VMEM((k, w, d), src.dtype),
                       pltpu.SemaphoreType.DMA((k,)), pltpu.SemaphoreType.DMA((k,))],
        name="sc_scatter_rows")
    def body(src_hbm, idx_hbm, out_hbm, idx_v, rows_v, sem_load, sem_scatter):
        wid = lax.axis_index("s") * SC_CORES + lax.axis_index("c")

        @pl.loop(0, per_worker // k)
        def _(i):
            win0 = wid * per_worker + i * k
            loads = [pltpu.async_copy(src_hbm.at[pl.ds(pl.multiple_of((win0 + b) * w, w), w)], rows_v.at[b],
                                      sem_load.at[b]) for b in range(k)]
            pltpu.sync_copy(idx_hbm.at[pl.ds(pl.multiple_of(win0 * TOP_K, TOP_K), k * TOP_K)], idx_v)
            scatters = []
            for b in range(k):
                loads[b].wait()
                scatters += [pltpu.async_copy(rows_v.at[b], out_hbm.at[idx_v.at[b * TOP_K + kk]],
                                              sem_scatter.at[b]) for kk in range(TOP_K)]
            for copy in scatters:
                copy.wait()

    return body(src, idx2d)


def _sc_gather_rows(table, idx):
    d = table.shape[1]
    w, k = SC_WINDOW, SC_GATHER_INFLIGHT
    assert idx.shape[0] % (w * k * SC_WORKERS) == 0
    per_worker = idx.shape[0] // (w * SC_WORKERS)
    idx2d = idx.reshape(-1, w)

    @functools.partial(
        pl.kernel, mesh=_sc_mesh(),
        out_type=jax.ShapeDtypeStruct((idx.shape[0], d), table.dtype),
        scratch_types=[pltpu.VMEM((k, w), jnp.int32), pltpu.VMEM((k, w, d), table.dtype),
                       pltpu.SemaphoreType.DMA((k,)), pltpu.SemaphoreType.DMA((k,))],
        name="sc_gather_rows")
    def body(table_hbm, idx_hbm, out_hbm, idx_v, rows_v, sem_gather, sem_store):
        wid = lax.axis_index("s") * SC_CORES + lax.axis_index("c")

        @pl.loop(0, per_worker // k)
        def _(i):
            win0 = wid * per_worker + i * k
            pltpu.sync_copy(idx_hbm.at[pl.ds(pl.multiple_of(win0, k), k)], idx_v)
            gathers = [pltpu.async_copy(table_hbm.at[idx_v.at[b]], rows_v.at[b], sem_gather.at[b])
                       for b in range(k)]
            stores = []
            for b in range(k):
                gathers[b].wait()
                stores.append(pltpu.async_copy(
                    rows_v.at[b], out_hbm.at[pl.ds(pl.multiple_of((win0 + b) * w, w), w)], sem_store.at[b]))
            for copy in stores:
                copy.wait()

    return body(table, idx2d)


def _ffn_kernel(blk_ref, exp_ref, valid_ref, first_ref, next_ref, slot_ref, nused_ref,
                xb_ref, w1_hbm, b1_ref, w2_hbm, b2_ref, after_hbm, yb_ref, w1f_ref, w2f_ref, w1b_ref, w2b_ref,
                sems):
    del after_hbm
    j = pl.program_id(0)
    active = j < nused_ref[0]

    def weight_copies(e, slot):
        return (pltpu.make_async_copy(w1_hbm.at[e], w1f_ref.at[slot], sems.at[0, slot]),
                pltpu.make_async_copy(w2_hbm.at[e], w2f_ref.at[slot], sems.at[1, slot]))

    @pl.when(jnp.logical_and(active, first_ref[j] == 1))
    def _():
        e = exp_ref[j]
        slot = slot_ref[j]
        nxt = next_ref[j]

        @pl.when(j == 0)
        def _():
            for copy in weight_copies(e, slot):
                copy.start()

        @pl.when(nxt >= 0)
        def _():
            for copy in weight_copies(nxt, 1 - slot):
                copy.start()

        for copy in weight_copies(e, slot):
            copy.wait()
        w1b_ref[...] = w1f_ref[slot].astype(BF16)
        w2b_ref[...] = w2f_ref[slot].astype(BF16)

    @pl.when(active)
    def _():
        row = lax.broadcasted_iota(jnp.int32, (MOE_BLOCK, 1), 0)
        x = jnp.where(row < valid_ref[j], _unpack_bf16_pairs(xb_ref[...]), 0.0).astype(BF16)
        hc = _dot(x, w1b_ref[...]) + b1_ref[0]
        gate = jnp.minimum(hc[:, :D_FF], SWIGLU_LIMIT)
        up = jnp.clip(hc[:, D_FF:], -SWIGLU_LIMIT, SWIGLU_LIMIT)
        act = (up + 1.0) * gate * _sigmoid(SWIGLU_ALPHA * gate)
        yb_ref[...] = _pack_bf16_pairs(_dot(act.astype(BF16), w2b_ref[...]) + b2_ref[0])


def _ffn(blk_idx, blk_exp, blk_valid, blk_first, blk_next, blk_slot, n_used, xb, w1, b1, w2, b2, after,
         n_blocks):
    bm = MOE_BLOCK
    grid_spec = pltpu.PrefetchScalarGridSpec(
        num_scalar_prefetch=7,
        grid=(n_blocks,),
        in_specs=[pl.BlockSpec((bm, D_MODEL // 2), lambda j, bi, be, *_: (bi[j], 0)),
                  pl.BlockSpec(memory_space=pl.ANY),
                  pl.BlockSpec((1, 1, 2 * D_FF), lambda j, bi, be, *_: (be[j], 0, 0)),
                  pl.BlockSpec(memory_space=pl.ANY),
                  pl.BlockSpec((1, 1, D_MODEL), lambda j, bi, be, *_: (be[j], 0, 0)),
                  pl.BlockSpec(memory_space=pl.ANY)],
        out_specs=pl.BlockSpec((bm, D_MODEL // 2), lambda j, bi, be, *_: (bi[j], 0)),
        scratch_shapes=[pltpu.VMEM((2, D_MODEL, 2 * D_FF), F32), pltpu.VMEM((2, D_FF, D_MODEL), F32),
                        pltpu.VMEM((D_MODEL, 2 * D_FF), BF16), pltpu.VMEM((D_FF, D_MODEL), BF16),
                        pltpu.SemaphoreType.DMA((2, 2))],
    )
    return pl.pallas_call(
        _ffn_kernel,
        grid_spec=grid_spec,
        out_shape=jax.ShapeDtypeStruct(xb.shape, jnp.uint32),
        compiler_params=pltpu.CompilerParams(
            dimension_semantics=("arbitrary",), vmem_limit_bytes=VMEM_LIMIT_BYTES),
        name="expert_ffn",
    )(blk_idx, blk_exp, blk_valid, blk_first, blk_next, blk_slot, n_used, xb, w1, b1, w2, b2, after)


def _combine_kernel(gk_ref, h1_ref, y0_ref, y1_ref, y2_ref, y3_ref, lng_ref, lnb_ref, out_ref):
    gk = gk_ref[...]
    ffn = gk[:, 0:1] * _unpack_bf16_pairs(y0_ref[...])
    for kk, y_ref in enumerate((y1_ref, y2_ref, y3_ref), start=1):
        ffn = ffn + gk[:, kk:kk + 1] * _unpack_bf16_pairs(y_ref[...])
    out_ref[...] = _layer_norm(DEEPNORM_ALPHA * h1_ref[...] + ffn, lng_ref[...], lnb_ref[...])


def _combine(gk, h1, ysel, lng, lnb):
    rows = h1.shape[0]
    tm = COMBINE_ROWS
    assert rows % tm == 0
    n_tiles = rows // tm
    const = lambda a: pl.BlockSpec(a.shape, lambda i: (0,) * a.ndim)
    ksel = lambda kk: pl.BlockSpec((tm, D_MODEL // 2), lambda i: (kk * n_tiles + i, 0))
    return pl.pallas_call(
        _combine_kernel,
        grid=(n_tiles,),
        in_specs=[pl.BlockSpec((tm, TOP_K), lambda i: (i, 0)),
                  pl.BlockSpec((tm, D_MODEL), lambda i: (i, 0)),
                  ksel(0), ksel(1), ksel(2), ksel(3),
                  const(lng), const(lnb)],
        out_specs=pl.BlockSpec((tm, D_MODEL), lambda i: (i, 0)),
        out_shape=jax.ShapeDtypeStruct((rows, D_MODEL), F32),
        compiler_params=pltpu.CompilerParams(
            dimension_semantics=("arbitrary",), vmem_limit_bytes=VMEM_LIMIT_BYTES),
        name="combine",
    )(gk, h1, ysel, ysel, ysel, ysel, lng, lnb)


def _row(v, width=None):
    v = v.reshape(1, -1).astype(F32)
    if width is not None and v.shape[1] < width:
        v = jnp.pad(v, ((0, 0), (0, width - v.shape[1])))
    return v


def _encode_all(x_a, x_b, meta_tokens, ln_emb_g, ln_emb_b, w_in, conv_w, conv_b, dt_bias, a_log,
                d_skip, ssd_norm_g, i_bias, f_bias, mlstm_norm_g, w_out, ln1_g, ln1_b, w_router, b_router,
                w1, b1, w2, b2, ln2_g, ln2_b):
    n_a, seq_len, _ = x_a.shape
    n_seq = n_a + x_b.shape[0]
    rows_a = n_a * seq_len
    assert seq_len % CHUNK == 0
    n_chunks = seq_len // CHUNK + 1
    n_tok = n_seq * seq_len

    sizes = (1024, CONV_CH, 2 * SSD_HEADS, 512, 512, 1024, 1024, 2 * MLSTM_HEADS, 2 * MLSTM_HEADS)
    offs = [0]
    for s in sizes:
        offs.append(offs[-1] + s)
    w_z, w_xbc, w_dt, w_q, w_k, w_v, w_o, w_i, w_f = [w_in[:, offs[j]:offs[j + 1]] for j in range(9)]
    w_big = jnp.concatenate([w_z, w_xbc, w_q, w_k * (MLSTM_DK ** -0.5), w_v, w_o], axis=1).astype(BF16)
    zpad = jnp.zeros((D_MODEL, LANES - GATE_END), F32)
    gate_cols = []
    for d in range(2):
        gate_cols += [w_dt[:, d * SSD_HEADS:(d + 1) * SSD_HEADS],
                      w_i[:, d * MLSTM_HEADS:(d + 1) * MLSTM_HEADS],
                      w_f[:, d * MLSTM_HEADS:(d + 1) * MLSTM_HEADS], zpad]
    w_gates = jnp.concatenate(gate_cols, axis=1).astype(BF16)
    gbias = [_row(jnp.concatenate([dt_bias[d], i_bias[d], f_bias[d]]), LANES) for d in range(2)]
    alog = [jnp.pad(jnp.broadcast_to(a_log[d].astype(F32)[:, None], (SSD_HEADS, LANES)),
                    ((0, GATE_END - SSD_HEADS), (0, 0))) for d in range(2)]
    head_of_col = jnp.arange(D_MODEL, dtype=jnp.int32) // SSD_HEAD_DIM
    lane_id = jnp.arange(LANES, dtype=jnp.int32)[:, None]
    expand = jnp.concatenate([lane_id == GATE_END + head_of_col[None, :],
                              lane_id == 2 * GATE_END + head_of_col[None, :]], axis=1).astype(BF16)
    dskip = _row(jnp.repeat(d_skip, SSD_HEAD_DIM))
    conv_w8 = jnp.pad(conv_w.astype(F32), ((0, 8 - CONV_W), (0, 0)))
    meta_tile = jnp.pad(meta_tokens.astype(F32), ((PAD_FRONT, INPROJ_ROWS - CHUNK), (0, 0)))

    z, xbc, q, k, v, o, gates_f, gates_b = _inproj(x_a, x_b, meta_tile, _row(ln_emb_g), _row(ln_emb_b),
                                                   w_big, w_gates)
    xs, bm, cm = _conv(xbc, conv_w8, _row(conv_b), n_chunks)
    yb, hb = _mixer_pass(xs, bm, cm, gates_b, q, k, v, gbias[1], alog[1], expand, n_chunks, reverse=True)
    (ycat,) = _mixer_pass(xs, bm, cm, gates_f, q, k, v, gbias[0], alog[0], expand, n_chunks,
                          reverse=False,
                          final_inputs=(z, o, yb, hb, dskip, _row(ssd_norm_g), _row(mlstm_norm_g)))

    wr = jnp.pad(w_router.astype(F32), ((0, 0), (0, LANES - N_EXPERTS)))
    wrh = wr.astype(BF16)
    wrl = (wr - wrh.astype(F32)).astype(BF16)
    w_out_b = w_out.astype(BF16)
    r_i = jnp.arange(RANK_ROWS, dtype=jnp.int32)
    lstrict = (r_i[None, :] < r_i[:, None]).astype(BF16)
    l_i = jnp.arange(LANES, dtype=jnp.int32)
    ucum = (l_i[:, None] <= l_i[None, :]).astype(BF16)
    ycat2d = ycat.reshape(n_tok, 2 * D_MODEL)
    outs = []
    after = jnp.zeros((8, LANES), F32)
    after_ffn = after
    for x_part, row0 in ((x_a, 0), (x_b, rows_a)):
        n_part = x_part.shape[0] * seq_len
        h1, h1p, sel, gates, expert_counts = _epilogue(x_part.reshape(n_part, D_MODEL), ycat2d, row0, w_out_b,
                                        _row(ln_emb_g), _row(ln_emb_b), _row(ln1_g), _row(ln1_b),
                                        wrh, wrl, _row(b_router, LANES), after)
        n_blocks = n_part * TOP_K // MOE_BLOCK + N_EXPERTS
        trash_row = n_blocks * MOE_BLOCK
        dest, gk, stats = _rank(sel, gates, expert_counts, lstrict, ucum, trash_row)
        after = stats
        counts = stats[0, :N_EXPERTS].astype(jnp.int32)
        starts = stats[1, :N_EXPERTS].astype(jnp.int32)
        pends = stats[2, :N_EXPERTS].astype(jnp.int32)
        n_used = pends[N_EXPERTS - 1] // MOE_BLOCK
        blk = jnp.minimum(jnp.arange(n_blocks, dtype=jnp.int32), jnp.maximum(n_used - 1, 0))
        blk_exp = jnp.minimum(
            jnp.sum((pends[None, :] <= (blk * MOE_BLOCK)[:, None]).astype(jnp.int32), axis=1),
            N_EXPERTS - 1).astype(jnp.int32)
        blk_last = jnp.take(starts + counts, blk_exp)
        blk_valid = jnp.clip(blk_last - blk * MOE_BLOCK, 0, MOE_BLOCK).astype(jnp.int32)
        dest_km = dest.T.reshape(-1)
        dest_wm = dest.reshape(n_part // SC_WINDOW, SC_WINDOW, TOP_K).transpose(0, 2, 1)
        xb = _sc_scatter_rows(h1p, dest_wm, trash_row + MOE_BLOCK)
        j_i = jnp.arange(n_blocks, dtype=jnp.int32)
        blk_first = jnp.logical_and(j_i < n_used, jnp.logical_or(j_i == 0, blk_exp != jnp.roll(blk_exp, 1)))
        blk_slot = ((jnp.cumsum(blk_first.astype(jnp.int32)) - 1) % 2).astype(jnp.int32)
        run_end = jnp.take(pends, blk_exp) // MOE_BLOCK
        blk_next = jnp.where(run_end < n_used, jnp.take(blk_exp, jnp.minimum(run_end, n_blocks - 1)),
                             -1).astype(jnp.int32)
        yexp = _ffn(blk, blk_exp, blk_valid, blk_first.astype(jnp.int32), blk_next, blk_slot,
                    n_used.reshape(1), xb, w1, b1.reshape(N_EXPERTS, 1, -1), w2,
                    b2.reshape(N_EXPERTS, 1, -1), after_ffn, n_blocks)
        after_ffn = yexp
        ysel = _sc_gather_rows(yexp, dest_km)
        outs.append(_combine(gk, h1, ysel, _row(ln2_g), _row(ln2_b)))
    return outs


def kernel(x_prompt, x_sample, meta_tokens, ln_emb_g, ln_emb_b, w_in, conv_w, conv_b, dt_bias, a_log,
           d_skip, ssd_norm_g, i_bias, f_bias, mlstm_norm_g, w_out, ln1_g, ln1_b, w_router, b_router,
           w1, b1, w2, b2, ln2_g, ln2_b):
    assert x_prompt.shape[1:] == x_sample.shape[1:]
    n_p, seq_len, d = x_prompt.shape
    n_s = x_sample.shape[0]
    y_p, y_s = _encode_all(x_prompt.astype(F32), x_sample.astype(F32), meta_tokens, ln_emb_g, ln_emb_b, w_in[0], conv_w[0],
                           conv_b[0], dt_bias[0], a_log[0], d_skip[0], ssd_norm_g[0], i_bias[0], f_bias[0],
                           mlstm_norm_g[0], w_out[0], ln1_g[0], ln1_b[0], w_router[0], b_router[0],
                           w1[0], b1[0], w2[0], b2[0], ln2_g[0], ln2_b[0])
    return (y_p.reshape(n_p, seq_len, d), y_s.reshape(n_s, seq_len, d))
```

```python
import functools

import jax
import jax.numpy as jnp
from jax import lax
from jax.experimental import pallas as pl
from jax.experimental.pallas import tpu as pltpu
from jax.experimental.pallas import tpu_sc as plsc

F32 = jnp.float32
BF16 = jnp.bfloat16

D_MODEL = 1024
N_META = 16
CHUNK = 128
PAD_FRONT = CHUNK - N_META
SSD_HEADS = 16
SSD_HEAD_DIM = 64
SSD_GROUPS = 4
SSD_STATE = 128
HEADS_PER_GROUP = SSD_HEADS // SSD_GROUPS
GROUP_WIDTH = HEADS_PER_GROUP * SSD_HEAD_DIM
CONV_W = 5
CONV_HALF = CONV_W // 2
CONV_CH = D_MODEL + 2 * SSD_GROUPS * SSD_STATE
MLSTM_HEADS = 8
MLSTM_DK = 64
MLSTM_DV = 128
N_EXPERTS = 32
TOP_K = 4
D_FF = D_MODEL
SWIGLU_LIMIT = 7.0
SWIGLU_ALPHA = 1.702
DEEPNORM_ALPHA = 2.0 ** 0.25
LN_EPS = 1e-5
RMS_EPS = 1e-5
NEG_GATE = -1e30
LOG2E = 1.4426950408889634

LANES = 128
BF16_SUBLANES = 16
VMEM_LIMIT_BYTES = 56 * 1024 * 1024

GATE_DT0, GATE_I0, GATE_F0, GATE_END = 0, SSD_HEADS, SSD_HEADS + MLSTM_HEADS, SSD_HEADS + 2 * MLSTM_HEADS

INPROJ_ROWS = 512
EPILOGUE_ROWS = 512
RANK_ROWS = 512
COMBINE_ROWS = 512
MOE_BLOCK = 512

SC_CORES = 2
SC_SUBCORES = 16
SC_WORKERS = SC_CORES * SC_SUBCORES
SC_WINDOW = 32
SC_SCATTER_INFLIGHT = 2
SC_GATHER_INFLIGHT = 4


def _dot(a, b):
    return jnp.dot(a, b, preferred_element_type=F32)


def _dot_nt(a, b):
    return lax.dot_general(a, b, (((1,), (1,)), ((), ())), preferred_element_type=F32)


def _split3(x):
    hi = x.astype(BF16)
    r1 = x - hi.astype(F32)
    mid = r1.astype(BF16)
    lo = (r1 - mid.astype(F32)).astype(BF16)
    return hi, mid, lo


def _dot_exact_lhs(a_bf16, x):
    hi, mid, lo = _split3(x)
    return _dot(a_bf16, hi) + _dot(a_bf16, mid) + _dot(a_bf16, lo)


def _dot_exact_rhs(x, b_bf16):
    hi, mid, lo = _split3(x)
    return _dot(hi, b_bf16) + _dot(mid, b_bf16) + _dot(lo, b_bf16)


def _layer_norm(x, g, b):
    mu = jnp.mean(x, axis=-1, keepdims=True)
    xc = x - mu
    var = jnp.mean(xc * xc, axis=-1, keepdims=True)
    return xc * lax.rsqrt(var + LN_EPS) * g + b


def _sigmoid(x):
    return 1.0 / (1.0 + jnp.exp(-x))


def _log1p_exp_neg_abs(x):
    return jnp.log(1.0 + jnp.exp(-jnp.abs(x)))


def _pack_bf16_pairs(x):
    n = x.shape[1] // 2
    bits = lax.bitcast_convert_type(x.astype(BF16).astype(F32), jnp.uint32)
    return (bits[:, :n] >> 16) | bits[:, n:]


def _unpack_bf16_pairs(words):
    lo = lax.bitcast_convert_type(words << 16, F32)
    hi = lax.bitcast_convert_type(words & jnp.uint32(0xFFFF0000), F32)
    return jnp.concatenate([lo, hi], axis=1)


def _storage_chunk(c, n_chunks):
    return jnp.where(c == 0, n_chunks - 1, c - 1)


def _inproj_kernel(xa_ref, xb_ref, meta_ref, g_ref, b_ref, wbig_ref, wg_ref,
                   z_ref, xbc_ref, q_ref, k_ref, v_ref, o_ref, gf_ref, gb_ref, *, n_x_tiles, n_a):
    x = jnp.where(pl.program_id(0) < n_a, xa_ref[...], xb_ref[...])
    x = jnp.where(pl.program_id(1) == n_x_tiles, meta_ref[...], x)
    h = _layer_norm(x, g_ref[...], b_ref[...]).astype(BF16)

    def mm(c0, c1):
        return _dot(h, wbig_ref[:, c0:c1]).astype(BF16)

    z_ref[...] = mm(0, 1024)
    xbc_ref[:, 0:1024] = mm(1024, 2048)
    xbc_ref[:, 1024:2048] = mm(2048, 3072)
    q_ref[...] = mm(3072, 3584)
    k_ref[...] = mm(3584, 4096)
    v_ref[...] = mm(4096, 5120)
    o_ref[...] = mm(5120, 6144)
    gates = _dot(h, wg_ref[...])
    gf_ref[...] = gates[:, :LANES]
    gb_ref[...] = gates[:, LANES:]


def _inproj(x_a, x_b, meta_tile, ln_g, ln_b, w_big, w_gates):
    n_a, seq_len, _ = x_a.shape
    n_seq = n_a + x_b.shape[0]
    tm = INPROJ_ROWS
    assert seq_len % tm == 0 and tm >= CHUNK
    n_x_tiles = seq_len // tm
    rows = seq_len + CHUNK
    row_spec = lambda n: pl.BlockSpec((None, tm, n), lambda b, i: (b, i, 0))
    const = lambda a: pl.BlockSpec(a.shape, lambda b, i: (0,) * a.ndim)
    resident = lambda a: pl.BlockSpec(a.shape, lambda b, i: (0,) * a.ndim, pipeline_mode=pl.Buffered(1))
    widths = (1024, CONV_CH, 512, 512, 1024, 1024)
    out_shapes = [jax.ShapeDtypeStruct((n_seq, rows, w), BF16) for w in widths]
    out_shapes += [jax.ShapeDtypeStruct((n_seq, rows, LANES), F32)] * 2
    return pl.pallas_call(
        functools.partial(_inproj_kernel, n_x_tiles=n_x_tiles, n_a=n_a),
        grid=(n_seq, n_x_tiles + 1),
        in_specs=[pl.BlockSpec((None, tm, D_MODEL), lambda b, i: (
                      jnp.minimum(b, n_a - 1),
                      jnp.where(b < n_a, jnp.minimum(i, n_x_tiles - 1), n_x_tiles - 1), 0)),
                  pl.BlockSpec((None, tm, D_MODEL), lambda b, i: (
                      jnp.maximum(b - n_a, 0),
                      jnp.where(b < n_a, 0, jnp.minimum(i, n_x_tiles - 1)), 0)),
                  resident(meta_tile), const(ln_g), const(ln_b), resident(w_big), resident(w_gates)],
        out_specs=[row_spec(s.shape[2]) for s in out_shapes],
        out_shape=out_shapes,
        compiler_params=pltpu.CompilerParams(
            dimension_semantics=("arbitrary", "arbitrary"), vmem_limit_bytes=VMEM_LIMIT_BYTES),
        name="inproj",
    )(x_a, x_b, meta_tile, ln_g, ln_b, w_big, w_gates)


def _conv_kernel(prev_ref, main_ref, next_ref, shift_ref, w_ref, b_ref, xs_ref, bm_ref, cm_ref,
                 *, n_x_tiles, chunks_per_tile):
    i = pl.program_id(1)
    w = w_ref[...]
    bias = b_ref[...]
    shift = shift_ref[...]

    def conv_chunk(before, rows, after, pad_rows, j):
        shifted = _dot(shift, jnp.concatenate([before, rows, after], axis=0).astype(BF16))
        acc = bias + rows * w[CONV_HALF:CONV_HALF + 1, :]
        for jj, t in enumerate(t for t in range(CONV_W) if t != CONV_HALF):
            acc = acc + shifted[jj * CHUNK:(jj + 1) * CHUNK, :] * w[t:t + 1, :]
        y = acc * _sigmoid(acc)
        if pad_rows is not None:
            y = jnp.where(pad_rows, 0.0, y)
        r = slice(j * CHUNK, (j + 1) * CHUNK)
        xs_ref[r, :] = y[:, :D_MODEL].astype(BF16)
        bm_ref[r, :] = y[:, D_MODEL:D_MODEL + 512].astype(BF16)
        cm_ref[r, :] = y[:, D_MODEL + 512:].astype(BF16)

    @pl.when(i < n_x_tiles)
    def _():
        tile = main_ref[...].astype(F32)
        for j in range(chunks_per_tile):
            lo, hi = j * CHUNK, (j + 1) * CHUNK
            before = prev_ref[...].astype(F32) if j == 0 else tile[lo - BF16_SUBLANES:lo, :]
            if j == chunks_per_tile - 1:
                after = jnp.where(i == n_x_tiles - 1, 0.0, next_ref[...].astype(F32))
            else:
                after = tile[hi:hi + BF16_SUBLANES, :]
            conv_chunk(before, tile[lo:hi, :], after, None, j)

    @pl.when(i == n_x_tiles)
    def _():
        row = lax.broadcasted_iota(jnp.int32, (CHUNK, 1), 0)
        pad_rows = row < PAD_FRONT
        rows = jnp.where(pad_rows, 0.0, main_ref[0:CHUNK, :].astype(F32))
        conv_chunk(jnp.zeros((BF16_SUBLANES, CONV_CH), F32), rows, next_ref[...].astype(F32), pad_rows, 0)


def _conv(xbc, conv_w8, conv_b, n_chunks):
    n_seq, rows, _ = xbc.shape
    tm = INPROJ_ROWS
    seq_len = (n_chunks - 1) * CHUNK
    assert seq_len % tm == 0 and tm % CHUNK == 0
    n_x_tiles = seq_len // tm
    halo_per_tile = tm // BF16_SUBLANES
    meta_last_halo = rows // BF16_SUBLANES - 1

    def prev_map(b, i):
        before_tile = jnp.maximum(jnp.minimum(i, n_x_tiles - 1) * halo_per_tile - 1, 0)
        return (b, jnp.where(i == 0, meta_last_halo, before_tile), 0)

    def next_map(b, i):
        return (b, jnp.where(i >= n_x_tiles - 1, 0, (i + 1) * halo_per_tile) , 0)

    const = lambda a: pl.BlockSpec(a.shape, lambda b, i: (0,) * a.ndim)
    l_i = jnp.arange(CHUNK, dtype=jnp.int32)[:, None]
    j_i = jnp.arange(CHUNK + 2 * BF16_SUBLANES, dtype=jnp.int32)[None, :]
    shifts = jnp.concatenate([(j_i == BF16_SUBLANES + l_i + t - CONV_HALF)
                              for t in range(CONV_W) if t != CONV_HALF], axis=0).astype(BF16)
    out_shapes = [jax.ShapeDtypeStruct((n_seq, rows, D_MODEL), BF16),
                  jax.ShapeDtypeStruct((n_seq, rows, 512), BF16),
                  jax.ShapeDtypeStruct((n_seq, rows, 512), BF16)]
    tile_spec = lambda n: pl.BlockSpec((None, tm, n), lambda b, i: (b, i, 0))
    return pl.pallas_call(
        functools.partial(_conv_kernel, n_x_tiles=n_x_tiles, chunks_per_tile=tm // CHUNK),
        grid=(n_seq, n_x_tiles + 1),
        in_specs=[pl.BlockSpec((None, BF16_SUBLANES, CONV_CH), prev_map),
                  tile_spec(CONV_CH),
                  pl.BlockSpec((None, BF16_SUBLANES, CONV_CH), next_map),
                  const(shifts), const(conv_w8), const(conv_b)],
        out_specs=[tile_spec(s.shape[2]) for s in out_shapes],
        out_shape=out_shapes,
        compiler_params=pltpu.CompilerParams(
            dimension_semantics=("arbitrary", "arbitrary"), vmem_limit_bytes=VMEM_LIMIT_BYTES),
        name="conv",
    )(xbc, xbc, xbc, shifts, conv_w8, conv_b)


def _mixer_kernel(*refs, reverse, final, n_chunks, n_seq):
    if final:
        (xs_ref, bm_ref, cm_ref, g_ref, q_ref, k_ref, v_ref, z_ref, o_ref, yb_ref, hb_ref,
         gbias_ref, alog_ref, expand_ref, dskip_ref, ngs_ref, ngm_ref,
         ycat_ref, s_ref, cst_ref, m_ref) = refs
    else:
        (xs_ref, bm_ref, cm_ref, g_ref, q_ref, k_ref, v_ref,
         gbias_ref, alog_ref, expand_ref,
         yout_ref, hout_ref, s_ref, cst_ref, m_ref) = refs

    t = pl.program_id(0)
    c = (n_chunks - 1 - t) if reverse else t
    end = 0 if reverse else CHUNK - 1

    @pl.when(t == 0)
    def _():
        s_ref[...] = jnp.zeros_like(s_ref)
        cst_ref[...] = jnp.zeros_like(cst_ref)
        m_ref[...] = jnp.zeros_like(m_ref)

    row = lax.broadcasted_iota(jnp.int32, (CHUNK, 1), 0)
    col = lax.broadcasted_iota(jnp.int32, (1, CHUNK), 1)
    lane = col
    allowed = (col >= row) if reverse else (col <= row)
    tri = allowed.astype(BF16)
    tri_t = ((row >= col) if reverse else (row <= col)).astype(BF16)
    feat = lax.broadcasted_iota(jnp.int32, (GATE_END, 1), 0)
    is_dt = feat < GATE_I0
    is_i = jnp.logical_and(feat >= GATE_I0, feat < GATE_F0)
    is_f = feat >= GATE_F0
    pad_cols = jnp.logical_and(c == 0, col < PAD_FRONT)
    a_coef = -jnp.exp(alog_ref[...]) * LOG2E
    expand = expand_ref[...]
    left_half = lane < SSD_HEAD_DIM
    right_half = jnp.logical_not(left_half)
    top_half = row < MLSTM_DK
    ones_blk = jnp.ones((CHUNK, MLSTM_DV), BF16)
    full = (CHUNK, LANES)

    def independent_products(b):
        cgs = [cm_ref[b, :, g * SSD_STATE:(g + 1) * SSD_STATE] for g in range(SSD_GROUPS)]
        bgs = [bm_ref[b, :, g * SSD_STATE:(g + 1) * SSD_STATE] for g in range(SSD_GROUPS)]
        cbs = [_dot_nt(cgs[g], bgs[g]) for g in range(SSD_GROUPS)]
        bg_ts = [bgs[g].astype(F32).T.astype(BF16) for g in range(SSD_GROUPS)]
        q_pairs = [q_ref[b, :, p * LANES:(p + 1) * LANES] for p in range(MLSTM_HEADS // 2)]
        k_pairs = [k_ref[b, :, p * LANES:(p + 1) * LANES] for p in range(MLSTM_HEADS // 2)]
        qks = [_dot_nt(jnp.where(left_half if h % 2 == 0 else right_half, q_pairs[h // 2],
                                 jnp.zeros_like(q_pairs[h // 2])), k_pairs[h // 2])
               for h in range(MLSTM_HEADS)]
        k_pair_ts = [k_pairs[p].astype(F32).T for p in range(MLSTM_HEADS // 2)]
        xs = xs_ref[b]
        xs_rhs = []
        for p in range(SSD_HEADS // 2):
            xs_pair = xs[:, p * LANES:(p + 1) * LANES]
            zero_pair = jnp.zeros_like(xs_pair)
            xs_rhs.append(jnp.concatenate([jnp.where(left_half, xs_pair, zero_pair),
                                           jnp.where(right_half, xs_pair, zero_pair)], axis=0))
        z_gate = o_gate = None
        if final:
            zz = z_ref[b].astype(F32)
            z_gate = zz * _sigmoid(zz)
            o_gate = _sigmoid(o_ref[b].astype(F32))
        return cgs, cbs, bg_ts, q_pairs, qks, k_pair_ts, xs, xs_rhs, z_gate, o_gate

    def one_sequence(b, products):
        cgs, cbs, bg_ts, q_pairs, qks, k_pair_ts, xs, xs_rhs, z_gate, o_gate = products

        gr = (g_ref[b] + gbias_ref[...]).T[0:GATE_END, :]
        lse = _log1p_exp_neg_abs(gr)
        val_t = jnp.where(is_dt, jnp.maximum(gr, 0.0) + lse, jnp.where(is_i, gr, jnp.minimum(gr, 0.0) - lse))
        val_t = jnp.where(pad_cols, jnp.where(is_i, NEG_GATE, 0.0), val_t)
        u_t = jnp.where(is_dt, val_t * a_coef, jnp.where(is_f, val_t * LOG2E, 0.0))
        cums_t = _dot_exact_rhs(u_t, tri_t)
        cums_end = jnp.broadcast_to(cums_t[:, end:end + 1], cums_t.shape)
        p1_t = jnp.exp2(cums_t)
        p2_t = jnp.exp2(cums_end - cums_t) * val_t
        packed = jnp.concatenate([cums_t, p1_t, p2_t, val_t], axis=0).T
        ex = _dot(packed.astype(BF16), expand)
        ex1 = ex[:, :D_MODEL]
        ex2 = ex[:, D_MODEL:]
        chunk_decay = _dot_exact_rhs(jnp.broadcast_to(packed[end:end + 1, :], (8, LANES)),
                                     expand[:, :D_MODEL])[0:1, :]

        xsf = xs.astype(F32)
        xs_w = (xsf * ex2).astype(BF16)
        src_term = jnp.log(val_t[0:SSD_HEADS, :]) * LOG2E - cums_t[0:SSD_HEADS, :]
        for g in range(SSD_GROUPS):
            gs = slice(g * GROUP_WIDTH, (g + 1) * GROUP_WIDTH)
            s_new_all[b].append(chunk_decay[:, gs] * s_old[b][g] + _dot(bg_ts[g], xs_w[:, gs]))
        pair_lhs = []
        for pair in range(SSD_HEADS // 2):
            cb = cbs[pair // (HEADS_PER_GROUP // 2)]
            m_mats = []
            for h in (2 * pair, 2 * pair + 1):
                seg = jnp.broadcast_to(packed[:, h:h + 1], full) + src_term[h:h + 1, :]
                m_mats.append((cb * jnp.exp2(jnp.where(allowed, seg, -jnp.inf))).astype(BF16))
            pair_lhs.append(jnp.concatenate(m_mats, axis=1))
        y_diag = jnp.concatenate([_dot(pair_lhs[p], xs_rhs[p]) for p in range(SSD_HEADS // 2)], axis=1)
        y_off = jnp.concatenate([_dot(cgs[g], s_old[b][g].astype(BF16)) for g in range(SSD_GROUPS)], axis=1)
        y_ssd = y_diag + y_off * ex1

        bcum_t = cums_t[GATE_F0:GATE_END, :]
        ip_t = val_t[GATE_I0:GATE_F0, :] * LOG2E
        rep = bcum_t.shape
        g_rep = jnp.broadcast_to(bcum_t[:, end:end + 1], rep)
        a_t = g_rep - bcum_t + ip_t
        a_max = jnp.broadcast_to(jnp.max(a_t, axis=1, keepdims=True), rep)
        w_t = jnp.exp2(a_t - a_max)
        m_prev = m_old[b]
        m_new = jnp.maximum(g_rep + m_prev, a_max)
        s_prev = jnp.exp2(g_rep + m_prev - m_new)
        s_new = jnp.exp2(a_max - m_new)
        r_t = ip_t - bcum_t
        v_heads = [v_ref[b, :, h * MLSTM_DV:(h + 1) * MLSTM_DV] for h in range(MLSTM_HEADS)]
        for pair in range(MLSTM_HEADS // 2):
            h0, h1 = 2 * pair, 2 * pair + 1
            cst = cst_old[b][pair]
            w_rows = jnp.where(top_half, w_t[h0:h0 + 1, :], w_t[h1:h1 + 1, :])
            kw = (k_pair_ts[pair] * w_rows).astype(BF16)
            full_kv = _dot(kw, jnp.concatenate([v_heads[h0], v_heads[h1], ones_blk], axis=1))
            kvn = jnp.concatenate(
                [jnp.where(top_half, full_kv[:, :MLSTM_DV], full_kv[:, MLSTM_DV:2 * MLSTM_DV]),
                 full_kv[:, 2 * MLSTM_DV:]], axis=1)
            sp_rows = jnp.where(top_half, s_prev[h0:h0 + 1, :], s_prev[h1:h1 + 1, :])
            sn_rows = jnp.where(top_half, s_new[h0:h0 + 1, :], s_new[h1:h1 + 1, :])
            cst_new_all[b].append(jnp.concatenate([sp_rows, sp_rows], axis=1) * cst
                                  + jnp.concatenate([sn_rows, sn_rows], axis=1) * kvn)
        h_heads = []
        for pair in range(MLSTM_HEADS // 2):
            h0, h1 = 2 * pair, 2 * pair + 1
            q_pair_f = q_pairs[pair].astype(F32)
            cst_b = cst_old[b][pair].astype(BF16)
            for hh, h in enumerate((h0, h1)):
                keep = left_half if hh == 0 else right_half
                vh = v_heads[h]
                qk = qks[h]
                bc = jnp.broadcast_to(packed[:, GATE_F0 + h:GATE_F0 + h + 1], full)
                dlog = jnp.where(allowed, bc + r_t[h:h + 1, :], -jnp.inf)
                m_intra = jnp.broadcast_to(jnp.max(dlog, axis=1, keepdims=True), full)
                m_inter = bc + m_prev[h:h + 1, :]
                m_t = jnp.maximum(m_inter, m_intra)
                s_mat = (qk * jnp.exp2(dlog - m_t)).astype(BF16)
                qs = (jnp.where(keep, q_pair_f, 0.0) * jnp.exp2(m_inter - m_t)).astype(BF16)
                tot = _dot(jnp.concatenate([s_mat, qs], axis=1),
                           jnp.concatenate([jnp.concatenate([vh, ones_blk], axis=1), cst_b], axis=0))
                num = tot[:, :MLSTM_DV]
                den = tot[:, MLSTM_DV:]
                h_heads.append(num / jnp.maximum(jnp.abs(den), jnp.exp2(-m_t)))
        m_new_all[b] = m_new
        h_ml = jnp.concatenate(h_heads, axis=1)
        if not final:
            return y_ssd.astype(BF16), h_ml.astype(BF16)

        y_tot = y_ssd + yb_ref[b].astype(F32) + dskip_ref[...] * xsf
        y2 = y_tot * z_gate
        y_n = y2 * lax.rsqrt(jnp.mean(y2 * y2, axis=-1, keepdims=True) + RMS_EPS) * ngs_ref[...]
        h_tot = h_ml + hb_ref[b].astype(F32)
        segs = []
        for h in range(MLSTM_HEADS):
            seg = h_tot[:, h * MLSTM_DV:(h + 1) * MLSTM_DV]
            segs.append(seg * lax.rsqrt(jnp.mean(seg * seg, axis=-1, keepdims=True) + RMS_EPS))
        h_n = jnp.concatenate(segs, axis=1) * ngm_ref[...]
        y_ml = o_gate * h_n
        return y_n.astype(BF16), y_ml.astype(BF16)

    s_old = [[s_ref[b, g] for g in range(SSD_GROUPS)] for b in range(n_seq)]
    cst_old = [[cst_ref[b, p] for p in range(MLSTM_HEADS // 2)] for b in range(n_seq)]
    m_old = [m_ref[b] for b in range(n_seq)]
    s_new_all = [[] for _ in range(n_seq)]
    cst_new_all = [[] for _ in range(n_seq)]
    m_new_all = [None] * n_seq
    results = [one_sequence(b, independent_products(b)) for b in range(n_seq)]
    for b in range(n_seq):
        for g in range(SSD_GROUPS):
            s_ref[b, g] = s_new_all[b][g]
        for p in range(MLSTM_HEADS // 2):
            cst_ref[b, p] = cst_new_all[b][p]
        m_ref[b] = m_new_all[b]

    @pl.when(c > 0)
    def _():
        for b, (first, second) in enumerate(results):
            if final:
                ycat_ref[b, :, :D_MODEL] = first
                ycat_ref[b, :, D_MODEL:] = second
            else:
                yout_ref[b] = first
                hout_ref[b] = second


def _mixer_pass(xs, bm, cm, gates, q, k, v, gbias, alog, expand, n_chunks, *, reverse, final_inputs=None):
    final = final_inputs is not None
    n_seq = xs.shape[0]
    seq_len = (n_chunks - 1) * CHUNK

    def chunk_of(t):
        return (n_chunks - 1 - t) if reverse else t

    def pad_map(t):
        return (0, _storage_chunk(chunk_of(t), n_chunks), 0)

    def out_map(t):
        return (0, jnp.maximum(chunk_of(t) - 1, 0), 0)

    const = lambda a: pl.BlockSpec(a.shape, lambda t: (0,) * a.ndim)
    pad_spec = lambda n: pl.BlockSpec((n_seq, CHUNK, n), pad_map)
    out_spec = lambda n: pl.BlockSpec((n_seq, CHUNK, n), out_map)
    in_arrays = [xs, bm, cm, gates, q, k, v]
    in_specs = [pad_spec(a.shape[2]) for a in in_arrays]
    if final:
        z, o, yb, hb, dskip, ngs, ngm = final_inputs
        in_arrays += [z, o, yb, hb]
        in_specs += [pad_spec(1024), pad_spec(1024), out_spec(1024), out_spec(1024)]
        in_arrays += [gbias, alog, expand, dskip, ngs, ngm]
        in_specs += [const(a) for a in (gbias, alog, expand, dskip, ngs, ngm)]
        out_shape = [jax.ShapeDtypeStruct((n_seq, seq_len, 2 * D_MODEL), BF16)]
        out_specs = [out_spec(2 * D_MODEL)]
    else:
        in_arrays += [gbias, alog, expand]
        in_specs += [const(a) for a in (gbias, alog, expand)]
        out_shape = [jax.ShapeDtypeStruct((n_seq, seq_len, D_MODEL), BF16),
                     jax.ShapeDtypeStruct((n_seq, seq_len, D_MODEL), BF16)]
        out_specs = [out_spec(D_MODEL), out_spec(D_MODEL)]
    return pl.pallas_call(
        functools.partial(_mixer_kernel, reverse=reverse, final=final, n_chunks=n_chunks, n_seq=n_seq),
        grid=(n_chunks,),
        in_specs=in_specs,
        out_specs=out_specs,
        out_shape=out_shape,
        scratch_shapes=[pltpu.VMEM((n_seq, SSD_GROUPS, SSD_STATE, GROUP_WIDTH), F32),
                        pltpu.VMEM((n_seq, MLSTM_HEADS // 2, 2 * MLSTM_DK, 2 * MLSTM_DV), F32),
                        pltpu.VMEM((n_seq, MLSTM_HEADS, LANES), F32)],
        compiler_params=pltpu.CompilerParams(
            dimension_semantics=("arbitrary",), vmem_limit_bytes=VMEM_LIMIT_BYTES),
        name="mixer_fwd" if final else "mixer_bwd",
    )(*in_arrays)


def _epilogue_kernel(x_ref, ycat_ref, wout_ref, lng0_ref, lnb0_ref, lng1_ref, lnb1_ref,
                     wrh_ref, wrl_ref, br_ref, h1_ref, h1p_ref, sel_ref, gate_ref, cnt_ref):
    h0 = _layer_norm(x_ref[...], lng0_ref[...], lnb0_ref[...])
    mix = _dot(ycat_ref[...], wout_ref[...])
    h1 = _layer_norm(DEEPNORM_ALPHA * h0 + mix, lng1_ref[...], lnb1_ref[...])
    h1_ref[...] = h1
    h1p_ref[...] = _pack_bf16_pairs(h1)
    hh = h1.astype(BF16)
    hl = (h1 - hh.astype(F32)).astype(BF16)
    both = _dot(hh, jnp.concatenate([wrh_ref[...], wrl_ref[...]], axis=1))
    logits = both[:, :LANES] + both[:, LANES:] + _dot(hl, wrh_ref[...]) + br_ref[...]
    lane = lax.broadcasted_iota(jnp.int32, (1, LANES), 1)
    lane_f = lane.astype(F32)
    logits = jnp.where(lane < N_EXPERTS, logits, -jnp.inf)
    work = logits
    sel = jnp.zeros(logits.shape, jnp.bool_)
    top = None
    for _ in range(TOP_K):
        m = jnp.max(work, axis=-1, keepdims=True)
        if top is None:
            top = m
        first = jnp.min(jnp.where(work == m, lane_f, float(LANES)), axis=-1, keepdims=True)
        pick = lane_f == first
        sel = jnp.logical_or(sel, pick)
        work = jnp.where(pick, -jnp.inf, work)
    e = jnp.where(sel, jnp.exp(logits - top), 0.0)
    gate_ref[...] = e / jnp.sum(e, axis=-1, keepdims=True)
    sel_f = sel.astype(F32)
    sel_ref[...] = sel_f

    @pl.when(pl.program_id(0) == 0)
    def _():
        cnt_ref[...] = jnp.zeros_like(cnt_ref)

    cnt_ref[0:1, :] = cnt_ref[0:1, :] + jnp.sum(sel_f, axis=0, keepdims=True)


def _epilogue(x, ycat, ycat_row0, w_out, lng0, lnb0, lng1, lnb1, wrh, wrl, br):
    rows = x.shape[0]
    tm = EPILOGUE_ROWS
    assert rows % tm == 0 and ycat_row0 % tm == 0
    tile0 = ycat_row0 // tm
    row_spec = lambda n: pl.BlockSpec((tm, n), lambda i: (i, 0))
    const = lambda a: pl.BlockSpec(a.shape, lambda i: (0,) * a.ndim)
    out_shape = [jax.ShapeDtypeStruct((rows, D_MODEL), F32),
                 jax.ShapeDtypeStruct((rows, D_MODEL // 2), jnp.uint32),
                 jax.ShapeDtypeStruct((rows, LANES), F32),
                 jax.ShapeDtypeStruct((rows, LANES), F32),
                 jax.ShapeDtypeStruct((8, LANES), F32)]
    consts = (w_out, lng0, lnb0, lng1, lnb1, wrh, wrl, br)
    return pl.pallas_call(
        _epilogue_kernel,
        grid=(rows // tm,),
        in_specs=[row_spec(D_MODEL), pl.BlockSpec((tm, 2 * D_MODEL), lambda i: (tile0 + i, 0))]
                 + [const(a) for a in consts],
        out_specs=[row_spec(D_MODEL), row_spec(D_MODEL // 2), row_spec(LANES), row_spec(LANES),
                   pl.BlockSpec((8, LANES), lambda i: (0, 0))],
        out_shape=out_shape,
        compiler_params=pltpu.CompilerParams(
            dimension_semantics=("arbitrary",), vmem_limit_bytes=VMEM_LIMIT_BYTES),
        name="epilogue",
    )(x, ycat, *consts)


def _rank_kernel(sel_ref, gate_ref, cnt_ref, lstrict_ref, ucum_ref, dest_ref, gk_ref, stats_ref,
                 base_ref, *, trash_row):
    i = pl.program_id(0)
    sel = sel_ref[...]
    colsum = jnp.sum(sel, axis=0, keepdims=True)

    @pl.when(i == 0)
    def _():
        counts = cnt_ref[0:1, :]
        padded = jnp.ceil(counts / MOE_BLOCK) * MOE_BLOCK
        pend = _dot_exact_rhs(jnp.broadcast_to(padded, (8, LANES)), ucum_ref[...])[0:1, :]
        stats_ref[0:1, :] = counts
        stats_ref[1:2, :] = pend - padded
        stats_ref[2:3, :] = pend
        stats_ref[3:8, :] = jnp.zeros((5, LANES), F32)
        base_ref[...] = jnp.broadcast_to(pend - padded, base_ref.shape)

    before = _dot(lstrict_ref[...], sel.astype(BF16))
    pos = base_ref[0:1, :] + before
    base_ref[0:1, :] = base_ref[0:1, :] + colsum
    work = jnp.where(sel > 0.0, pos + 1.0, 0.0)
    gates = gate_ref[...]
    lane = lax.broadcasted_iota(jnp.int32, (1, LANES), 1)
    dest_cols = jnp.zeros(sel.shape, F32)
    for kk in range(TOP_K):
        m = jnp.max(work, axis=-1, keepdims=True)
        pick = jnp.logical_and(work == m, m > 0.0)
        gk_ref[:, kk:kk + 1] = jnp.sum(jnp.where(pick, gates, 0.0), axis=-1, keepdims=True)
        dest_cols = jnp.where(lane == kk, jnp.where(m > 0.0, m - 1.0, float(trash_row)), dest_cols)
        work = jnp.where(pick, 0.0, work)
    dest_ref[...] = dest_cols.T[0:8, :].astype(jnp.int32)


def _rank(sel, gates, counts, lstrict, ucum, trash_row):
    rows = sel.shape[0]
    tm = RANK_ROWS
    assert rows % tm == 0
    row_spec = lambda n: pl.BlockSpec((tm, n), lambda i: (i, 0))
    const = lambda a: pl.BlockSpec(a.shape, lambda i: (0,) * a.ndim)
    return pl.pallas_call(
        functools.partial(_rank_kernel, trash_row=trash_row),
        grid=(rows // tm,),
        in_specs=[row_spec(LANES), row_spec(LANES), const(counts), const(lstrict), const(ucum)],
        out_specs=[pl.BlockSpec((None, 8, tm), lambda i: (i, 0, 0)), row_spec(TOP_K),
                   pl.BlockSpec((8, LANES), lambda i: (0, 0))],
        out_shape=[jax.ShapeDtypeStruct((rows // tm, 8, tm), jnp.int32),
                   jax.ShapeDtypeStruct((rows, TOP_K), F32),
                   jax.ShapeDtypeStruct((8, LANES), F32)],
        scratch_shapes=[pltpu.VMEM((8, LANES), F32)],
        compiler_params=pltpu.CompilerParams(
            dimension_semantics=("arbitrary",), vmem_limit_bytes=VMEM_LIMIT_BYTES),
        name="rank",
    )(sel, gates, counts, lstrict, ucum)


def _sc_mesh():
    return plsc.VectorSubcoreMesh(core_axis_name="c", subcore_axis_name="s",
                                  num_cores=SC_CORES, num_subcores=SC_SUBCORES)


def _sc_scatter_rows(src, idx, n_out_rows):
    n_src, d = src.shape
    w, k = SC_WINDOW, SC_SCATTER_INFLIGHT
    assert idx.shape == (n_src // w, TOP_K, w) and n_src % (w * k * SC_WORKERS) == 0
    per_worker = n_src // (w * SC_WORKERS)
    idx2d = idx.reshape(n_src // w * TOP_K, w)

    @functools.partial(
        pl.kernel, mesh=_sc_mesh(),
        out_type=jax.ShapeDtypeStruct((n_out_rows, d), src.dtype),
        scratch_types=[pltpu.VMEM((k * TOP_K, w), jnp.int32), pltpu.VMEM((k, w, d), src.dtype),
                       pltpu.SemaphoreType.DMA((k,)), pltpu.SemaphoreType.DMA((k,))],
        name="sc_scatter_rows")
    def body(src_hbm, idx_hbm, out_hbm, idx_v, rows_v, sem_load, sem_scatter):
        wid = lax.axis_index("s") * SC_CORES + lax.axis_index("c")

        @pl.loop(0, per_worker // k)
        def _(i):
            win0 = wid * per_worker + i * k
            loads = [pltpu.async_copy(src_hbm.at[pl.ds(pl.multiple_of((win0 + b) * w, w), w)], rows_v.at[b],
                                      sem_load.at[b]) for b in range(k)]
            pltpu.sync_copy(idx_hbm.at[pl.ds(pl.multiple_of(win0 * TOP_K, TOP_K), k * TOP_K)], idx_v)
            scatters = []
            for b in range(k):
                loads[b].wait()
                scatters += [pltpu.async_copy(rows_v.at[b], out_hbm.at[idx_v.at[b * TOP_K + kk]],
                                              sem_scatter.at[b]) for kk in range(TOP_K)]
            for copy in scatters:
                copy.wait()

    return body(src, idx2d)


def _sc_gather_rows(table, idx):
    d = table.shape[1]
    w, k = SC_WINDOW, SC_GATHER_INFLIGHT
    assert idx.shape[0] % (w * k * SC_WORKERS) == 0
    per_worker = idx.shape[0] // (w * SC_WORKERS)
    idx2d = idx.reshape(-1, w)

    @functools.partial(
        pl.kernel, mesh=_sc_mesh(),
        out_type=jax.ShapeDtypeStruct((idx.shape[0], d), table.dtype),
        scratch_types=[pltpu.VMEM((k, w), jnp.int32), pltpu.VMEM((k, w, d), table.dtype),
                       pltpu.SemaphoreType.DMA((k,)), pltpu.SemaphoreType.DMA((k,))],
        name="sc_gather_rows")
    def body(table_hbm, idx_hbm, out_hbm, idx_v, rows_v, sem_gather, sem_store):
        wid = lax.axis_index("s") * SC_CORES + lax.axis_index("c")

        @pl.loop(0, per_worker // k)
        def _(i):
            win0 = wid * per_worker + i * k
            pltpu.sync_copy(idx_hbm.at[pl.ds(pl.multiple_of(win0, k), k)], idx_v)
            gathers = [pltpu.async_copy(table_hbm.at[idx_v.at[b]], rows_v.at[b], sem_gather.at[b])
                       for b in range(k)]
            stores = []
            for b in range(k):
                gathers[b].wait()
                stores.append(pltpu.async_copy(
                    rows_v.at[b], out_hbm.at[pl.ds(pl.multiple_of((win0 + b) * w, w), w)], sem_store.at[b]))
            for copy in stores:
                copy.wait()

    return body(table, idx2d)


def _ffn_kernel(blk_ref, exp_ref, valid_ref, first_ref, next_ref, slot_ref, nused_ref,
                xb_ref, w1_hbm, b1_ref, w2_hbm, b2_ref, yb_ref, w1f_ref, w2f_ref, w1b_ref, w2b_ref, sems):
    j = pl.program_id(0)
    active = j < nused_ref[0]

    def weight_copies(e, slot):
        return (pltpu.make_async_copy(w1_hbm.at[e], w1f_ref.at[slot], sems.at[0, slot]),
                pltpu.make_async_copy(w2_hbm.at[e], w2f_ref.at[slot], sems.at[1, slot]))

    @pl.when(jnp.logical_and(active, first_ref[j] == 1))
    def _():
        e = exp_ref[j]
        slot = slot_ref[j]
        nxt = next_ref[j]

        @pl.when(j == 0)
        def _():
            for copy in weight_copies(e, slot):
                copy.start()

        @pl.when(nxt >= 0)
        def _():
            for copy in weight_copies(nxt, 1 - slot):
                copy.start()

        for copy in weight_copies(e, slot):
            copy.wait()
        w1b_ref[...] = w1f_ref[slot].astype(BF16)
        w2b_ref[...] = w2f_ref[slot].astype(BF16)

    @pl.when(active)
    def _():
        row = lax.broadcasted_iota(jnp.int32, (MOE_BLOCK, 1), 0)
        x = jnp.where(row < valid_ref[j], _unpack_bf16_pairs(xb_ref[...]), 0.0).astype(BF16)
        hc = _dot(x, w1b_ref[...]) + b1_ref[0]
        gate = jnp.minimum(hc[:, :D_FF], SWIGLU_LIMIT)
        up = jnp.clip(hc[:, D_FF:], -SWIGLU_LIMIT, SWIGLU_LIMIT)
        act = (up + 1.0) * gate * _sigmoid(SWIGLU_ALPHA * gate)
        yb_ref[...] = _pack_bf16_pairs(_dot(act.astype(BF16), w2b_ref[...]) + b2_ref[0])


def _ffn(blk_idx, blk_exp, blk_valid, blk_first, blk_next, blk_slot, n_used, xb, w1, b1, w2, b2, n_blocks):
    bm = MOE_BLOCK
    grid_spec = pltpu.PrefetchScalarGridSpec(
        num_scalar_prefetch=7,
        grid=(n_blocks,),
        in_specs=[pl.BlockSpec((bm, D_MODEL // 2), lambda j, bi, be, *_: (bi[j], 0)),
                  pl.BlockSpec(memory_space=pl.ANY),
                  pl.BlockSpec((1, 1, 2 * D_FF), lambda j, bi, be, *_: (be[j], 0, 0)),
                  pl.BlockSpec(memory_space=pl.ANY),
                  pl.BlockSpec((1, 1, D_MODEL), lambda j, bi, be, *_: (be[j], 0, 0))],
        out_specs=pl.BlockSpec((bm, D_MODEL // 2), lambda j, bi, be, *_: (bi[j], 0)),
        scratch_shapes=[pltpu.VMEM((2, D_MODEL, 2 * D_FF), F32), pltpu.VMEM((2, D_FF, D_MODEL), F32),
                        pltpu.VMEM((D_MODEL, 2 * D_FF), BF16), pltpu.VMEM((D_FF, D_MODEL), BF16),
                        pltpu.SemaphoreType.DMA((2, 2))],
    )
    return pl.pallas_call(
        _ffn_kernel,
        grid_spec=grid_spec,
        out_shape=jax.ShapeDtypeStruct(xb.shape, jnp.uint32),
        compiler_params=pltpu.CompilerParams(
            dimension_semantics=("arbitrary",), vmem_limit_bytes=VMEM_LIMIT_BYTES),
        name="expert_ffn",
    )(blk_idx, blk_exp, blk_valid, blk_first, blk_next, blk_slot, n_used, xb, w1, b1, w2, b2)


def _combine_kernel(gk_ref, h1_ref, y0_ref, y1_ref, y2_ref, y3_ref, lng_ref, lnb_ref, out_ref):
    gk = gk_ref[...]
    ffn = gk[:, 0:1] * _unpack_bf16_pairs(y0_ref[...])
    for kk, y_ref in enumerate((y1_ref, y2_ref, y3_ref), start=1):
        ffn = ffn + gk[:, kk:kk + 1] * _unpack_bf16_pairs(y_ref[...])
    out_ref[...] = _layer_norm(DEEPNORM_ALPHA * h1_ref[...] + ffn, lng_ref[...], lnb_ref[...])


def _combine(gk, h1, ysel, lng, lnb):
    rows = h1.shape[0]
    tm = COMBINE_ROWS
    assert rows % tm == 0
    n_tiles = rows // tm
    const = lambda a: pl.BlockSpec(a.shape, lambda i: (0,) * a.ndim)
    ksel = lambda kk: pl.BlockSpec((tm, D_MODEL // 2), lambda i: (kk * n_tiles + i, 0))
    return pl.pallas_call(
        _combine_kernel,
        grid=(n_tiles,),
        in_specs=[pl.BlockSpec((tm, TOP_K), lambda i: (i, 0)),
                  pl.BlockSpec((tm, D_MODEL), lambda i: (i, 0)),
                  ksel(0), ksel(1), ksel(2), ksel(3),
                  const(lng), const(lnb)],
        out_specs=pl.BlockSpec((tm, D_MODEL), lambda i: (i, 0)),
        out_shape=jax.ShapeDtypeStruct((rows, D_MODEL), F32),
        compiler_params=pltpu.CompilerParams(
            dimension_semantics=("arbitrary",), vmem_limit_bytes=VMEM_LIMIT_BYTES),
        name="combine",
    )(gk, h1, ysel, ysel, ysel, ysel, lng, lnb)


def _row(v, width=None):
    v = v.reshape(1, -1).astype(F32)
    if width is not None and v.shape[1] < width:
        v = jnp.pad(v, ((0, 0), (0, width - v.shape[1])))
    return v


def _encode_all(x_a, x_b, meta_tokens, ln_emb_g, ln_emb_b, w_in, conv_w, conv_b, dt_bias, a_log,
                d_skip, ssd_norm_g, i_bias, f_bias, mlstm_norm_g, w_out, ln1_g, ln1_b, w_router, b_router,
                w1, b1, w2, b2, ln2_g, ln2_b):
    n_a, seq_len, _ = x_a.shape
    n_seq = n_a + x_b.shape[0]
    rows_a = n_a * seq_len
    assert seq_len % CHUNK == 0
    n_chunks = seq_len // CHUNK + 1
    n_tok = n_seq * seq_len

    sizes = (1024, CONV_CH, 2 * SSD_HEADS, 512, 512, 1024, 1024, 2 * MLSTM_HEADS, 2 * MLSTM_HEADS)
    offs = [0]
    for s in sizes:
        offs.append(offs[-1] + s)
    w_z, w_xbc, w_dt, w_q, w_k, w_v, w_o, w_i, w_f = [w_in[:, offs[j]:offs[j + 1]] for j in range(9)]
    w_big = jnp.concatenate([w_z, w_xbc, w_q, w_k * (MLSTM_DK ** -0.5), w_v, w_o], axis=1).astype(BF16)
    zpad = jnp.zeros((D_MODEL, LANES - GATE_END), F32)
    gate_cols = []
    for d in range(2):
        gate_cols += [w_dt[:, d * SSD_HEADS:(d + 1) * SSD_HEADS],
                      w_i[:, d * MLSTM_HEADS:(d + 1) * MLSTM_HEADS],
                      w_f[:, d * MLSTM_HEADS:(d + 1) * MLSTM_HEADS], zpad]
    w_gates = jnp.concatenate(gate_cols, axis=1).astype(BF16)
    gbias = [_row(jnp.concatenate([dt_bias[d], i_bias[d], f_bias[d]]), LANES) for d in range(2)]
    alog = [jnp.pad(jnp.broadcast_to(a_log[d].astype(F32)[:, None], (SSD_HEADS, LANES)),
                    ((0, GATE_END - SSD_HEADS), (0, 0))) for d in range(2)]
    head_of_col = jnp.arange(D_MODEL, dtype=jnp.int32) // SSD_HEAD_DIM
    lane_id = jnp.arange(LANES, dtype=jnp.int32)[:, None]
    expand = jnp.concatenate([lane_id == GATE_END + head_of_col[None, :],
                              lane_id == 2 * GATE_END + head_of_col[None, :]], axis=1).astype(BF16)
    dskip = _row(jnp.repeat(d_skip, SSD_HEAD_DIM))
    conv_w8 = jnp.pad(conv_w.astype(F32), ((0, 8 - CONV_W), (0, 0)))
    meta_tile = jnp.pad(meta_tokens.astype(F32), ((PAD_FRONT, INPROJ_ROWS - CHUNK), (0, 0)))

    z, xbc, q, k, v, o, gates_f, gates_b = _inproj(x_a, x_b, meta_tile, _row(ln_emb_g), _row(ln_emb_b),
                                                   w_big, w_gates)
    xs, bm, cm = _conv(xbc, conv_w8, _row(conv_b), n_chunks)
    yb, hb = _mixer_pass(xs, bm, cm, gates_b, q, k, v, gbias[1], alog[1], expand, n_chunks, reverse=True)
    (ycat,) = _mixer_pass(xs, bm, cm, gates_f, q, k, v, gbias[0], alog[0], expand, n_chunks,
                          reverse=False,
                          final_inputs=(z, o, yb, hb, dskip, _row(ssd_norm_g), _row(mlstm_norm_g)))

    wr = jnp.pad(w_router.astype(F32), ((0, 0), (0, LANES - N_EXPERTS)))
    wrh = wr.astype(BF16)
    wrl = (wr - wrh.astype(F32)).astype(BF16)
    w_out_b = w_out.astype(BF16)
    r_i = jnp.arange(RANK_ROWS, dtype=jnp.int32)
    lstrict = (r_i[None, :] < r_i[:, None]).astype(BF16)
    l_i = jnp.arange(LANES, dtype=jnp.int32)
    ucum = (l_i[:, None] <= l_i[None, :]).astype(BF16)
    ycat2d = ycat.reshape(n_tok, 2 * D_MODEL)
    outs = []
    for x_part, row0 in ((x_a, 0), (x_b, rows_a)):
        n_part = x_part.shape[0] * seq_len
        h1, h1p, sel, gates, expert_counts = _epilogue(x_part.reshape(n_part, D_MODEL), ycat2d, row0, w_out_b,
                                        _row(ln_emb_g), _row(ln_emb_b), _row(ln1_g), _row(ln1_b),
                                        wrh, wrl, _row(b_router, LANES))
        n_blocks = n_part * TOP_K // MOE_BLOCK + N_EXPERTS
        trash_row = n_blocks * MOE_BLOCK
        dest, gk, stats = _rank(sel, gates, expert_counts, lstrict, ucum, trash_row)
        counts = stats[0, :N_EXPERTS].astype(jnp.int32)
        starts = stats[1, :N_EXPERTS].astype(jnp.int32)
        pends = stats[2, :N_EXPERTS].astype(jnp.int32)
        n_used = pends[N_EXPERTS - 1] // MOE_BLOCK
        blk = jnp.minimum(jnp.arange(n_blocks, dtype=jnp.int32), jnp.maximum(n_used - 1, 0))
        blk_exp = jnp.minimum(
            jnp.sum((pends[None, :] <= (blk * MOE_BLOCK)[:, None]).astype(jnp.int32), axis=1),
            N_EXPERTS - 1).astype(jnp.int32)
        e_i = jnp.arange(N_EXPERTS, dtype=jnp.int32)[None, :]
        is_exp = blk_exp[:, None] == e_i
        blk_last = jnp.sum(jnp.where(is_exp, (starts + counts)[None, :], 0), axis=1)
        blk_valid = jnp.clip(blk_last - blk * MOE_BLOCK, 0, MOE_BLOCK).astype(jnp.int32)
        d4 = dest[:, :TOP_K, :]
        dest_km = d4.transpose(1, 0, 2).reshape(-1)
        dest_wm = d4.reshape(-1, TOP_K, RANK_ROWS // SC_WINDOW, SC_WINDOW).transpose(0, 2, 1, 3).reshape(
            n_part // SC_WINDOW, TOP_K, SC_WINDOW)
        xb = _sc_scatter_rows(h1p, dest_wm, trash_row + MOE_BLOCK)
        j_i = jnp.arange(n_blocks, dtype=jnp.int32)
        blk_first = jnp.logical_and(j_i < n_used, jnp.logical_or(j_i == 0, blk_exp != jnp.roll(blk_exp, 1)))
        blk_slot = ((jnp.cumsum(blk_first.astype(jnp.int32)) - 1) % 2).astype(jnp.int32)
        later = jnp.where(jnp.logical_and(e_i > blk_exp[:, None], counts[None, :] > 0), e_i, N_EXPERTS)
        blk_next = jnp.min(later, axis=1)
        blk_next = jnp.where(blk_next < N_EXPERTS, blk_next, -1).astype(jnp.int32)
        yexp = _ffn(blk, blk_exp, blk_valid, blk_first.astype(jnp.int32), blk_next, blk_slot,
                    n_used.reshape(1), xb, w1, b1.reshape(N_EXPERTS, 1, -1), w2,
                    b2.reshape(N_EXPERTS, 1, -1), n_blocks)
        ysel = _sc_gather_rows(yexp, dest_km)
        outs.append(_combine(gk, h1, ysel, _row(ln2_g), _row(ln2_b)))
    return outs


def kernel(x_prompt, x_sample, meta_tokens, ln_emb_g, ln_emb_b, w_in, conv_w, conv_b, dt_bias, a_log,
           d_skip, ssd_norm_g, i_bias, f_bias, mlstm_norm_g, w_out, ln1_g, ln1_b, w_router, b_router,
           w1, b1, w2, b2, ln2_g, ln2_b):
    assert x_prompt.shape[1:] == x_sample.shape[1:]
    n_p, seq_len, d = x_prompt.shape
    n_s = x_sample.shape[0]
    y_p, y_s = _encode_all(x_prompt.astype(F32), x_sample.astype(F32), meta_tokens, ln_emb_g, ln_emb_b, w_in[0], conv_w[0],
                           conv_b[0], dt_bias[0], a_log[0], d_skip[0], ssd_norm_g[0], i_bias[0], f_bias[0],
                           mlstm_norm_g[0], w_out[0], ln1_g[0], ln1_b[0], w_router[0], b_router[0],
                           w1[0], b1[0], w2[0], b2[0], ln2_g[0], ln2_b[0])
    return (y_p.reshape(n_p, seq_len, d), y_s.reshape(n_s, seq_len, d))
```

```python
import functools

import jax
import jax.numpy as jnp
from jax import lax
from jax.experimental import pallas as pl
from jax.experimental.pallas import tpu as pltpu
from jax.experimental.pallas import tpu_sc as plsc

F32 = jnp.float32
BF16 = jnp.bfloat16

D_MODEL = 1024
N_META = 16
CHUNK = 128
PAD_FRONT = CHUNK - N_META
SSD_HEADS = 16
SSD_HEAD_DIM = 64
SSD_GROUPS = 4
SSD_STATE = 128
HEADS_PER_GROUP = SSD_HEADS // SSD_GROUPS
GROUP_WIDTH = HEADS_PER_GROUP * SSD_HEAD_DIM
CONV_W = 5
CONV_HALF = CONV_W // 2
CONV_CH = D_MODEL + 2 * SSD_GROUPS * SSD_STATE
MLSTM_HEADS = 8
MLSTM_DK = 64
MLSTM_DV = 128
N_EXPERTS = 32
TOP_K = 4
D_FF = D_MODEL
SWIGLU_LIMIT = 7.0
SWIGLU_ALPHA = 1.702
DEEPNORM_ALPHA = 2.0 ** 0.25
LN_EPS = 1e-5
RMS_EPS = 1e-5
NEG_GATE = -1e30
LOG2E = 1.4426950408889634

LANES = 128
BF16_SUBLANES = 16
VMEM_LIMIT_BYTES = 56 * 1024 * 1024

GATE_DT0, GATE_I0, GATE_F0, GATE_END = 0, SSD_HEADS, SSD_HEADS + MLSTM_HEADS, SSD_HEADS + 2 * MLSTM_HEADS

INPROJ_ROWS = 512
EPILOGUE_ROWS = 512
RANK_ROWS = 512
COMBINE_ROWS = 512
MOE_BLOCK = 512

SC_CORES = 2
SC_SUBCORES = 16
SC_WORKERS = SC_CORES * SC_SUBCORES
SC_WINDOW = 32
SC_SCATTER_INFLIGHT = 2
SC_GATHER_INFLIGHT = 4


def _dot(a, b):
    return jnp.dot(a, b, preferred_element_type=F32)


def _dot_nt(a, b):
    return lax.dot_general(a, b, (((1,), (1,)), ((), ())), preferred_element_type=F32)


def _split3(x):
    hi = x.astype(BF16)
    r1 = x - hi.astype(F32)
    mid = r1.astype(BF16)
    lo = (r1 - mid.astype(F32)).astype(BF16)
    return hi, mid, lo


def _dot_exact_lhs(a_bf16, x):
    hi, mid, lo = _split3(x)
    return _dot(a_bf16, hi) + _dot(a_bf16, mid) + _dot(a_bf16, lo)


def _dot_exact_rhs(x, b_bf16):
    hi, mid, lo = _split3(x)
    return _dot(hi, b_bf16) + _dot(mid, b_bf16) + _dot(lo, b_bf16)


def _layer_norm(x, g, b):
    mu = jnp.mean(x, axis=-1, keepdims=True)
    xc = x - mu
    var = jnp.mean(xc * xc, axis=-1, keepdims=True)
    return xc * lax.rsqrt(var + LN_EPS) * g + b


def _sigmoid(x):
    return 1.0 / (1.0 + jnp.exp(-x))


def _log1p_exp_neg_abs(x):
    return jnp.log(1.0 + jnp.exp(-jnp.abs(x)))


def _pack_bf16_pairs(x):
    n = x.shape[1] // 2
    bits = lax.bitcast_convert_type(x.astype(BF16).astype(F32), jnp.uint32)
    return (bits[:, :n] >> 16) | bits[:, n:]


def _unpack_bf16_pairs(words):
    lo = lax.bitcast_convert_type(words << 16, F32)
    hi = lax.bitcast_convert_type(words & jnp.uint32(0xFFFF0000), F32)
    return jnp.concatenate([lo, hi], axis=1)


def _storage_chunk(c, n_chunks):
    return jnp.where(c == 0, n_chunks - 1, c - 1)


def _inproj_kernel(xa_ref, xb_ref, meta_ref, g_ref, b_ref, wbig_ref, wg_ref,
                   z_ref, xbc_ref, q_ref, k_ref, v_ref, o_ref, gf_ref, gb_ref, *, n_x_tiles, n_a):
    x = jnp.where(pl.program_id(0) < n_a, xa_ref[...], xb_ref[...])
    x = jnp.where(pl.program_id(1) == n_x_tiles, meta_ref[...], x)
    h = _layer_norm(x, g_ref[...], b_ref[...]).astype(BF16)

    def mm(c0, c1):
        return _dot(h, wbig_ref[:, c0:c1]).astype(BF16)

    z_ref[...] = mm(0, 1024)
    xbc_ref[:, 0:1024] = mm(1024, 2048)
    xbc_ref[:, 1024:2048] = mm(2048, 3072)
    q_ref[...] = mm(3072, 3584)
    k_ref[...] = mm(3584, 4096)
    v_ref[...] = mm(4096, 5120)
    o_ref[...] = mm(5120, 6144)
    gates = _dot(h, wg_ref[...])
    gf_ref[...] = gates[:, :LANES]
    gb_ref[...] = gates[:, LANES:]


def _inproj(x_a, x_b, meta_tile, ln_g, ln_b, w_big, w_gates):
    n_a, seq_len, _ = x_a.shape
    n_seq = n_a + x_b.shape[0]
    tm = INPROJ_ROWS
    assert seq_len % tm == 0 and tm >= CHUNK
    n_x_tiles = seq_len // tm
    rows = seq_len + CHUNK
    row_spec = lambda n: pl.BlockSpec((None, tm, n), lambda b, i: (b, i, 0))
    const = lambda a: pl.BlockSpec(a.shape, lambda b, i: (0,) * a.ndim)
    resident = lambda a: pl.BlockSpec(a.shape, lambda b, i: (0,) * a.ndim, pipeline_mode=pl.Buffered(1))
    widths = (1024, CONV_CH, 512, 512, 1024, 1024)
    out_shapes = [jax.ShapeDtypeStruct((n_seq, rows, w), BF16) for w in widths]
    out_shapes += [jax.ShapeDtypeStruct((n_seq, rows, LANES), F32)] * 2
    return pl.pallas_call(
        functools.partial(_inproj_kernel, n_x_tiles=n_x_tiles, n_a=n_a),
        grid=(n_seq, n_x_tiles + 1),
        in_specs=[pl.BlockSpec((None, tm, D_MODEL), lambda b, i: (
                      jnp.minimum(b, n_a - 1),
                      jnp.where(b < n_a, jnp.minimum(i, n_x_tiles - 1), n_x_tiles - 1), 0)),
                  pl.BlockSpec((None, tm, D_MODEL), lambda b, i: (
                      jnp.maximum(b - n_a, 0),
                      jnp.where(b < n_a, 0, jnp.minimum(i, n_x_tiles - 1)), 0)),
                  resident(meta_tile), const(ln_g), const(ln_b), resident(w_big), resident(w_gates)],
        out_specs=[row_spec(s.shape[2]) for s in out_shapes],
        out_shape=out_shapes,
        compiler_params=pltpu.CompilerParams(
            dimension_semantics=("arbitrary", "arbitrary"), vmem_limit_bytes=VMEM_LIMIT_BYTES),
        name="inproj",
    )(x_a, x_b, meta_tile, ln_g, ln_b, w_big, w_gates)


def _conv_kernel(prev_ref, main_ref, next_ref, shift_ref, w_ref, b_ref, xs_ref, bm_ref, cm_ref,
                 *, n_x_tiles, chunks_per_tile):
    i = pl.program_id(1)
    w = w_ref[...]
    bias = b_ref[...]
    shift = shift_ref[...]

    def conv_chunk(before, rows, after, pad_rows, j):
        shifted = _dot(shift, jnp.concatenate([before, rows, after], axis=0).astype(BF16))
        acc = bias + rows * w[CONV_HALF:CONV_HALF + 1, :]
        for jj, t in enumerate(t for t in range(CONV_W) if t != CONV_HALF):
            acc = acc + shifted[jj * CHUNK:(jj + 1) * CHUNK, :] * w[t:t + 1, :]
        y = acc * _sigmoid(acc)
        if pad_rows is not None:
            y = jnp.where(pad_rows, 0.0, y)
        r = slice(j * CHUNK, (j + 1) * CHUNK)
        xs_ref[r, :] = y[:, :D_MODEL].astype(BF16)
        bm_ref[r, :] = y[:, D_MODEL:D_MODEL + 512].astype(BF16)
        cm_ref[r, :] = y[:, D_MODEL + 512:].astype(BF16)

    @pl.when(i < n_x_tiles)
    def _():
        tile = main_ref[...].astype(F32)
        for j in range(chunks_per_tile):
            lo, hi = j * CHUNK, (j + 1) * CHUNK
            before = prev_ref[...].astype(F32) if j == 0 else tile[lo - BF16_SUBLANES:lo, :]
            if j == chunks_per_tile - 1:
                after = jnp.where(i == n_x_tiles - 1, 0.0, next_ref[...].astype(F32))
            else:
                after = tile[hi:hi + BF16_SUBLANES, :]
            conv_chunk(before, tile[lo:hi, :], after, None, j)

    @pl.when(i == n_x_tiles)
    def _():
        row = lax.broadcasted_iota(jnp.int32, (CHUNK, 1), 0)
        pad_rows = row < PAD_FRONT
        rows = jnp.where(pad_rows, 0.0, main_ref[0:CHUNK, :].astype(F32))
        conv_chunk(jnp.zeros((BF16_SUBLANES, CONV_CH), F32), rows, next_ref[...].astype(F32), pad_rows, 0)


def _conv(xbc, conv_w8, conv_b, n_chunks):
    n_seq, rows, _ = xbc.shape
    tm = INPROJ_ROWS
    seq_len = (n_chunks - 1) * CHUNK
    assert seq_len % tm == 0 and tm % CHUNK == 0
    n_x_tiles = seq_len // tm
    halo_per_tile = tm // BF16_SUBLANES
    meta_last_halo = rows // BF16_SUBLANES - 1

    def prev_map(b, i):
        before_tile = jnp.maximum(jnp.minimum(i, n_x_tiles - 1) * halo_per_tile - 1, 0)
        return (b, jnp.where(i == 0, meta_last_halo, before_tile), 0)

    def next_map(b, i):
        return (b, jnp.where(i >= n_x_tiles - 1, 0, (i + 1) * halo_per_tile) , 0)

    const = lambda a: pl.BlockSpec(a.shape, lambda b, i: (0,) * a.ndim)
    l_i = jnp.arange(CHUNK, dtype=jnp.int32)[:, None]
    j_i = jnp.arange(CHUNK + 2 * BF16_SUBLANES, dtype=jnp.int32)[None, :]
    shifts = jnp.concatenate([(j_i == BF16_SUBLANES + l_i + t - CONV_HALF)
                              for t in range(CONV_W) if t != CONV_HALF], axis=0).astype(BF16)
    out_shapes = [jax.ShapeDtypeStruct((n_seq, rows, D_MODEL), BF16),
                  jax.ShapeDtypeStruct((n_seq, rows, 512), BF16),
                  jax.ShapeDtypeStruct((n_seq, rows, 512), BF16)]
    tile_spec = lambda n: pl.BlockSpec((None, tm, n), lambda b, i: (b, i, 0))
    return pl.pallas_call(
        functools.partial(_conv_kernel, n_x_tiles=n_x_tiles, chunks_per_tile=tm // CHUNK),
        grid=(n_seq, n_x_tiles + 1),
        in_specs=[pl.BlockSpec((None, BF16_SUBLANES, CONV_CH), prev_map),
                  tile_spec(CONV_CH),
                  pl.BlockSpec((None, BF16_SUBLANES, CONV_CH), next_map),
                  const(shifts), const(conv_w8), const(conv_b)],
        out_specs=[tile_spec(s.shape[2]) for s in out_shapes],
        out_shape=out_shapes,
        compiler_params=pltpu.CompilerParams(
            dimension_semantics=("arbitrary", "arbitrary"), vmem_limit_bytes=VMEM_LIMIT_BYTES),
        name="conv",
    )(xbc, xbc, xbc, shifts, conv_w8, conv_b)


def _mixer_kernel(*refs, reverse, final, n_chunks, n_seq):
    if final:
        (xs_ref, bm_ref, cm_ref, g_ref, q_ref, k_ref, v_ref, z_ref, o_ref, yb_ref, hb_ref,
         gbias_ref, alog_ref, expand_ref, dskip_ref, ngs_ref, ngm_ref,
         ycat_ref, s_ref, cst_ref, m_ref) = refs
    else:
        (xs_ref, bm_ref, cm_ref, g_ref, q_ref, k_ref, v_ref,
         gbias_ref, alog_ref, expand_ref,
         yout_ref, hout_ref, s_ref, cst_ref, m_ref) = refs

    t = pl.program_id(0)
    c = (n_chunks - 1 - t) if reverse else t
    end = 0 if reverse else CHUNK - 1

    @pl.when(t == 0)
    def _():
        s_ref[...] = jnp.zeros_like(s_ref)
        cst_ref[...] = jnp.zeros_like(cst_ref)
        m_ref[...] = jnp.zeros_like(m_ref)

    row = lax.broadcasted_iota(jnp.int32, (CHUNK, 1), 0)
    col = lax.broadcasted_iota(jnp.int32, (1, CHUNK), 1)
    lane = col
    allowed = (col >= row) if reverse else (col <= row)
    tri = allowed.astype(BF16)
    tri_t = ((row >= col) if reverse else (row <= col)).astype(BF16)
    feat = lax.broadcasted_iota(jnp.int32, (GATE_END, 1), 0)
    is_dt = feat < GATE_I0
    is_i = jnp.logical_and(feat >= GATE_I0, feat < GATE_F0)
    is_f = feat >= GATE_F0
    pad_cols = jnp.logical_and(c == 0, col < PAD_FRONT)
    a_coef = -jnp.exp(alog_ref[...]) * LOG2E
    expand = expand_ref[...]
    left_half = lane < SSD_HEAD_DIM
    right_half = jnp.logical_not(left_half)
    top_half = row < MLSTM_DK
    ones_blk = jnp.ones((CHUNK, MLSTM_DV), BF16)
    full = (CHUNK, LANES)

    def independent_products(b):
        cgs = [cm_ref[b, :, g * SSD_STATE:(g + 1) * SSD_STATE] for g in range(SSD_GROUPS)]
        bgs = [bm_ref[b, :, g * SSD_STATE:(g + 1) * SSD_STATE] for g in range(SSD_GROUPS)]
        cbs = [_dot_nt(cgs[g], bgs[g]) for g in range(SSD_GROUPS)]
        bg_ts = [bgs[g].astype(F32).T.astype(BF16) for g in range(SSD_GROUPS)]
        q_pairs = [q_ref[b, :, p * LANES:(p + 1) * LANES] for p in range(MLSTM_HEADS // 2)]
        k_pairs = [k_ref[b, :, p * LANES:(p + 1) * LANES] for p in range(MLSTM_HEADS // 2)]
        qks = [_dot_nt(jnp.where(left_half if h % 2 == 0 else right_half, q_pairs[h // 2],
                                 jnp.zeros_like(q_pairs[h // 2])), k_pairs[h // 2])
               for h in range(MLSTM_HEADS)]
        k_pair_ts = [k_pairs[p].astype(F32).T for p in range(MLSTM_HEADS // 2)]
        xs = xs_ref[b]
        xs_rhs = []
        for p in range(SSD_HEADS // 2):
            xs_pair = xs[:, p * LANES:(p + 1) * LANES]
            zero_pair = jnp.zeros_like(xs_pair)
            xs_rhs.append(jnp.concatenate([jnp.where(left_half, xs_pair, zero_pair),
                                           jnp.where(right_half, xs_pair, zero_pair)], axis=0))
        z_gate = o_gate = None
        if final:
            zz = z_ref[b].astype(F32)
            z_gate = zz * _sigmoid(zz)
            o_gate = _sigmoid(o_ref[b].astype(F32))
        return cgs, cbs, bg_ts, q_pairs, qks, k_pair_ts, xs, xs_rhs, z_gate, o_gate

    def one_sequence(b, products):
        cgs, cbs, bg_ts, q_pairs, qks, k_pair_ts, xs, xs_rhs, z_gate, o_gate = products

        gr = (g_ref[b] + gbias_ref[...]).T[0:GATE_END, :]
        lse = _log1p_exp_neg_abs(gr)
        val_t = jnp.where(is_dt, jnp.maximum(gr, 0.0) + lse, jnp.where(is_i, gr, jnp.minimum(gr, 0.0) - lse))
        val_t = jnp.where(pad_cols, jnp.where(is_i, NEG_GATE, 0.0), val_t)
        u_t = jnp.where(is_dt, val_t * a_coef, jnp.where(is_f, val_t * LOG2E, 0.0))
        cums_t = _dot_exact_rhs(u_t, tri_t)
        cums_end = jnp.broadcast_to(cums_t[:, end:end + 1], cums_t.shape)
        p1_t = jnp.exp2(cums_t)
        p2_t = jnp.exp2(cums_end - cums_t) * val_t
        packed = jnp.concatenate([cums_t, p1_t, p2_t, val_t], axis=0).T
        ex = _dot(packed.astype(BF16), expand)
        ex1 = ex[:, :D_MODEL]
        ex2 = ex[:, D_MODEL:]
        chunk_decay = _dot_exact_rhs(jnp.broadcast_to(packed[end:end + 1, :], (8, LANES)),
                                     expand[:, :D_MODEL])[0:1, :]

        xsf = xs.astype(F32)
        xs_w = (xsf * ex2).astype(BF16)
        src_term = jnp.log(val_t[0:SSD_HEADS, :]) * LOG2E - cums_t[0:SSD_HEADS, :]
        for g in range(SSD_GROUPS):
            gs = slice(g * GROUP_WIDTH, (g + 1) * GROUP_WIDTH)
            s_new_all[b].append(chunk_decay[:, gs] * s_old[b][g] + _dot(bg_ts[g], xs_w[:, gs]))
        pair_lhs = []
        for pair in range(SSD_HEADS // 2):
            cb = cbs[pair // (HEADS_PER_GROUP // 2)]
            m_mats = []
            for h in (2 * pair, 2 * pair + 1):
                seg = jnp.broadcast_to(packed[:, h:h + 1], full) + src_term[h:h + 1, :]
                m_mats.append((cb * jnp.exp2(jnp.where(allowed, seg, -jnp.inf))).astype(BF16))
            pair_lhs.append(jnp.concatenate(m_mats, axis=1))
        y_diag = jnp.concatenate([_dot(pair_lhs[p], xs_rhs[p]) for p in range(SSD_HEADS // 2)], axis=1)
        y_off = jnp.concatenate([_dot(cgs[g], s_old[b][g].astype(BF16)) for g in range(SSD_GROUPS)], axis=1)
        y_ssd = y_diag + y_off * ex1

        bcum_t = cums_t[GATE_F0:GATE_END, :]
        ip_t = val_t[GATE_I0:GATE_F0, :] * LOG2E
        rep = bcum_t.shape
        g_rep = jnp.broadcast_to(bcum_t[:, end:end + 1], rep)
        a_t = g_rep - bcum_t + ip_t
        a_max = jnp.broadcast_to(jnp.max(a_t, axis=1, keepdims=True), rep)
        w_t = jnp.exp2(a_t - a_max)
        m_prev = m_old[b]
        m_new = jnp.maximum(g_rep + m_prev, a_max)
        s_prev = jnp.exp2(g_rep + m_prev - m_new)
        s_new = jnp.exp2(a_max - m_new)
        r_t = ip_t - bcum_t
        v_heads = [v_ref[b, :, h * MLSTM_DV:(h + 1) * MLSTM_DV] for h in range(MLSTM_HEADS)]
        for pair in range(MLSTM_HEADS // 2):
            h0, h1 = 2 * pair, 2 * pair + 1
            cst = cst_old[b][pair]
            w_rows = jnp.where(top_half, w_t[h0:h0 + 1, :], w_t[h1:h1 + 1, :])
            kw = (k_pair_ts[pair] * w_rows).astype(BF16)
            full_kv = _dot(kw, jnp.concatenate([v_heads[h0], v_heads[h1], ones_blk], axis=1))
            kvn = jnp.concatenate(
                [jnp.where(top_half, full_kv[:, :MLSTM_DV], full_kv[:, MLSTM_DV:2 * MLSTM_DV]),
                 full_kv[:, 2 * MLSTM_DV:]], axis=1)
            sp_rows = jnp.where(top_half, s_prev[h0:h0 + 1, :], s_prev[h1:h1 + 1, :])
            sn_rows = jnp.where(top_half, s_new[h0:h0 + 1, :], s_new[h1:h1 + 1, :])
            cst_new_all[b].append(jnp.concatenate([sp_rows, sp_rows], axis=1) * cst
                                  + jnp.concatenate([sn_rows, sn_rows], axis=1) * kvn)
        h_heads = []
        for pair in range(MLSTM_HEADS // 2):
            h0, h1 = 2 * pair, 2 * pair + 1
            q_pair_f = q_pairs[pair].astype(F32)
            cst_b = cst_old[b][pair].astype(BF16)
            for hh, h in enumerate((h0, h1)):
                keep = left_half if hh == 0 else right_half
                vh = v_heads[h]
                qk = qks[h]
                bc = jnp.broadcast_to(packed[:, GATE_F0 + h:GATE_F0 + h + 1], full)
                dlog = jnp.where(allowed, bc + r_t[h:h + 1, :], -jnp.inf)
                m_intra = jnp.broadcast_to(jnp.max(dlog, axis=1, keepdims=True), full)
                m_inter = bc + m_prev[h:h + 1, :]
                m_t = jnp.maximum(m_inter, m_intra)
                s_mat = (qk * jnp.exp2(dlog - m_t)).astype(BF16)
                qs = (jnp.where(keep, q_pair_f, 0.0) * jnp.exp2(m_inter - m_t)).astype(BF16)
                tot = _dot(jnp.concatenate([s_mat, qs], axis=1),
                           jnp.concatenate([jnp.concatenate([vh, ones_blk], axis=1), cst_b], axis=0))
                num = tot[:, :MLSTM_DV]
                den = tot[:, MLSTM_DV:]
                h_heads.append(num / jnp.maximum(jnp.abs(den), jnp.exp2(-m_t)))
        m_new_all[b] = m_new
        h_ml = jnp.concatenate(h_heads, axis=1)
        if not final:
            return y_ssd.astype(BF16), h_ml.astype(BF16)

        y_tot = y_ssd + yb_ref[b].astype(F32) + dskip_ref[...] * xsf
        y2 = y_tot * z_gate
        y_n = y2 * lax.rsqrt(jnp.mean(y2 * y2, axis=-1, keepdims=True) + RMS_EPS) * ngs_ref[...]
        h_tot = h_ml + hb_ref[b].astype(F32)
        segs = []
        for h in range(MLSTM_HEADS):
            seg = h_tot[:, h * MLSTM_DV:(h + 1) * MLSTM_DV]
            segs.append(seg * lax.rsqrt(jnp.mean(seg * seg, axis=-1, keepdims=True) + RMS_EPS))
        h_n = jnp.concatenate(segs, axis=1) * ngm_ref[...]
        y_ml = o_gate * h_n
        return y_n.astype(BF16), y_ml.astype(BF16)

    s_old = [[s_ref[b, g] for g in range(SSD_GROUPS)] for b in range(n_seq)]
    cst_old = [[cst_ref[b, p] for p in range(MLSTM_HEADS // 2)] for b in range(n_seq)]
    m_old = [m_ref[b] for b in range(n_seq)]
    s_new_all = [[] for _ in range(n_seq)]
    cst_new_all = [[] for _ in range(n_seq)]
    m_new_all = [None] * n_seq
    results = [one_sequence(b, independent_products(b)) for b in range(n_seq)]
    for b in range(n_seq):
        for g in range(SSD_GROUPS):
            s_ref[b, g] = s_new_all[b][g]
        for p in range(MLSTM_HEADS // 2):
            cst_ref[b, p] = cst_new_all[b][p]
        m_ref[b] = m_new_all[b]

    @pl.when(c > 0)
    def _():
        for b, (first, second) in enumerate(results):
            if final:
                ycat_ref[b, :, :D_MODEL] = first
                ycat_ref[b, :, D_MODEL:] = second
            else:
                yout_ref[b] = first
                hout_ref[b] = second


def _mixer_pass(xs, bm, cm, gates, q, k, v, gbias, alog, expand, n_chunks, *, reverse, final_inputs=None):
    final = final_inputs is not None
    n_seq = xs.shape[0]
    seq_len = (n_chunks - 1) * CHUNK

    def chunk_of(t):
        return (n_chunks - 1 - t) if reverse else t

    def pad_map(t):
        return (0, _storage_chunk(chunk_of(t), n_chunks), 0)

    def out_map(t):
        return (0, jnp.maximum(chunk_of(t) - 1, 0), 0)

    const = lambda a: pl.BlockSpec(a.shape, lambda t: (0,) * a.ndim)
    pad_spec = lambda n: pl.BlockSpec((n_seq, CHUNK, n), pad_map)
    out_spec = lambda n: pl.BlockSpec((n_seq, CHUNK, n), out_map)
    in_arrays = [xs, bm, cm, gates, q, k, v]
    in_specs = [pad_spec(a.shape[2]) for a in in_arrays]
    if final:
        z, o, yb, hb, dskip, ngs, ngm = final_inputs
        in_arrays += [z, o, yb, hb]
        in_specs += [pad_spec(1024), pad_spec(1024), out_spec(1024), out_spec(1024)]
        in_arrays += [gbias, alog, expand, dskip, ngs, ngm]
        in_specs += [const(a) for a in (gbias, alog, expand, dskip, ngs, ngm)]
        out_shape = [jax.ShapeDtypeStruct((n_seq, seq_len, 2 * D_MODEL), BF16)]
        out_specs = [out_spec(2 * D_MODEL)]
    else:
        in_arrays += [gbias, alog, expand]
        in_specs += [const(a) for a in (gbias, alog, expand)]
        out_shape = [jax.ShapeDtypeStruct((n_seq, seq_len, D_MODEL), BF16),
                     jax.ShapeDtypeStruct((n_seq, seq_len, D_MODEL), BF16)]
        out_specs = [out_spec(D_MODEL), out_spec(D_MODEL)]
    return pl.pallas_call(
        functools.partial(_mixer_kernel, reverse=reverse, final=final, n_chunks=n_chunks, n_seq=n_seq),
        grid=(n_chunks,),
        in_specs=in_specs,
        out_specs=out_specs,
        out_shape=out_shape,
        scratch_shapes=[pltpu.VMEM((n_seq, SSD_GROUPS, SSD_STATE, GROUP_WIDTH), F32),
                        pltpu.VMEM((n_seq, MLSTM_HEADS // 2, 2 * MLSTM_DK, 2 * MLSTM_DV), F32),
                        pltpu.VMEM((n_seq, MLSTM_HEADS, LANES), F32)],
        compiler_params=pltpu.CompilerParams(
            dimension_semantics=("arbitrary",), vmem_limit_bytes=VMEM_LIMIT_BYTES),
        name="mixer_fwd" if final else "mixer_bwd",
    )(*in_arrays)


def _epilogue_kernel(x_ref, ycat_ref, wout_ref, lng0_ref, lnb0_ref, lng1_ref, lnb1_ref,
                     wrh_ref, wrl_ref, br_ref, h1_ref, h1p_ref, sel_ref, gate_ref, cnt_ref):
    h0 = _layer_norm(x_ref[...], lng0_ref[...], lnb0_ref[...])
    mix = _dot(ycat_ref[...], wout_ref[...])
    h1 = _layer_norm(DEEPNORM_ALPHA * h0 + mix, lng1_ref[...], lnb1_ref[...])
    h1_ref[...] = h1
    h1p_ref[...] = _pack_bf16_pairs(h1)
    hh = h1.astype(BF16)
    hl = (h1 - hh.astype(F32)).astype(BF16)
    both = _dot(hh, jnp.concatenate([wrh_ref[...], wrl_ref[...]], axis=1))
    logits = both[:, :LANES] + both[:, LANES:] + _dot(hl, wrh_ref[...]) + br_ref[...]
    lane = lax.broadcasted_iota(jnp.int32, (1, LANES), 1)
    lane_f = lane.astype(F32)
    logits = jnp.where(lane < N_EXPERTS, logits, -jnp.inf)
    work = logits
    sel = jnp.zeros(logits.shape, jnp.bool_)
    top = None
    for _ in range(TOP_K):
        m = jnp.max(work, axis=-1, keepdims=True)
        if top is None:
            top = m
        first = jnp.min(jnp.where(work == m, lane_f, float(LANES)), axis=-1, keepdims=True)
        pick = lane_f == first
        sel = jnp.logical_or(sel, pick)
        work = jnp.where(pick, -jnp.inf, work)
    e = jnp.where(sel, jnp.exp(logits - top), 0.0)
    gate_ref[...] = e / jnp.sum(e, axis=-1, keepdims=True)
    sel_f = sel.astype(F32)
    sel_ref[...] = sel_f

    @pl.when(pl.program_id(0) == 0)
    def _():
        cnt_ref[...] = jnp.zeros_like(cnt_ref)

    cnt_ref[0:1, :] = cnt_ref[0:1, :] + jnp.sum(sel_f, axis=0, keepdims=True)


def _epilogue(x, x_row0, rows, ycat, ycat_row0, w_out, lng0, lnb0, lng1, lnb1, wrh, wrl, br):
    tm = EPILOGUE_ROWS
    assert rows % tm == 0 and ycat_row0 % tm == 0 and x_row0 % tm == 0
    tile0 = ycat_row0 // tm
    x_tile0 = x_row0 // tm
    row_spec = lambda n: pl.BlockSpec((tm, n), lambda i: (i, 0))
    const = lambda a: pl.BlockSpec(a.shape, lambda i: (0,) * a.ndim)
    out_shape = [jax.ShapeDtypeStruct((rows, D_MODEL), F32),
                 jax.ShapeDtypeStruct((rows, D_MODEL // 2), jnp.uint32),
                 jax.ShapeDtypeStruct((rows, LANES), F32),
                 jax.ShapeDtypeStruct((rows, LANES), F32),
                 jax.ShapeDtypeStruct((8, LANES), F32)]
    consts = (w_out, lng0, lnb0, lng1, lnb1, wrh, wrl, br)
    return pl.pallas_call(
        _epilogue_kernel,
        grid=(rows // tm,),
        in_specs=[pl.BlockSpec((tm, D_MODEL), lambda i: (x_tile0 + i, 0)),
                  pl.BlockSpec((tm, 2 * D_MODEL), lambda i: (tile0 + i, 0))]
                 + [const(a) for a in consts],
        out_specs=[row_spec(D_MODEL), row_spec(D_MODEL // 2), row_spec(LANES), row_spec(LANES),
                   pl.BlockSpec((8, LANES), lambda i: (0, 0))],
        out_shape=out_shape,
        compiler_params=pltpu.CompilerParams(
            dimension_semantics=("arbitrary",), vmem_limit_bytes=VMEM_LIMIT_BYTES),
        name="epilogue",
    )(x, ycat, *consts)


def _rank_kernel(sel_ref, gate_ref, cnt_ref, lstrict_ref, ucum_ref, dest_ref, gk_ref, stats_ref,
                 base_ref, *, trash_row):
    i = pl.program_id(0)
    sel = sel_ref[...]
    colsum = jnp.sum(sel, axis=0, keepdims=True)

    @pl.when(i == 0)
    def _():
        counts = cnt_ref[0:1, :]
        padded = jnp.ceil(counts / MOE_BLOCK) * MOE_BLOCK
        pend = _dot_exact_rhs(jnp.broadcast_to(padded, (8, LANES)), ucum_ref[...])[0:1, :]
        stats_ref[0:1, :] = counts
        stats_ref[1:2, :] = pend - padded
        stats_ref[2:3, :] = pend
        stats_ref[3:8, :] = jnp.zeros((5, LANES), F32)
        base_ref[...] = jnp.broadcast_to(pend - padded, base_ref.shape)

    before = _dot(lstrict_ref[...], sel.astype(BF16))
    pos = base_ref[0:1, :] + before
    base_ref[0:1, :] = base_ref[0:1, :] + colsum
    work = jnp.where(sel > 0.0, pos + 1.0, 0.0)
    gates = gate_ref[...]
    lane = lax.broadcasted_iota(jnp.int32, (1, LANES), 1)
    dest_cols = jnp.zeros(sel.shape, F32)
    for kk in range(TOP_K):
        m = jnp.max(work, axis=-1, keepdims=True)
        pick = jnp.logical_and(work == m, m > 0.0)
        gk_ref[:, kk:kk + 1] = jnp.sum(jnp.where(pick, gates, 0.0), axis=-1, keepdims=True)
        dest_cols = jnp.where(lane == kk, jnp.where(m > 0.0, m - 1.0, float(trash_row)), dest_cols)
        work = jnp.where(pick, 0.0, work)
    dest_ref[...] = dest_cols.T[0:8, :].astype(jnp.int32)


def _rank(sel, gates, counts, lstrict, ucum, trash_row):
    rows = sel.shape[0]
    tm = RANK_ROWS
    assert rows % tm == 0
    row_spec = lambda n: pl.BlockSpec((tm, n), lambda i: (i, 0))
    const = lambda a: pl.BlockSpec(a.shape, lambda i: (0,) * a.ndim)
    return pl.pallas_call(
        functools.partial(_rank_kernel, trash_row=trash_row),
        grid=(rows // tm,),
        in_specs=[row_spec(LANES), row_spec(LANES), const(counts), const(lstrict), const(ucum)],
        out_specs=[pl.BlockSpec((None, 8, tm), lambda i: (i, 0, 0)), row_spec(TOP_K),
                   pl.BlockSpec((8, LANES), lambda i: (0, 0))],
        out_shape=[jax.ShapeDtypeStruct((rows // tm, 8, tm), jnp.int32),
                   jax.ShapeDtypeStruct((rows, TOP_K), F32),
                   jax.ShapeDtypeStruct((8, LANES), F32)],
        scratch_shapes=[pltpu.VMEM((8, LANES), F32)],
        compiler_params=pltpu.CompilerParams(
            dimension_semantics=("arbitrary",), vmem_limit_bytes=VMEM_LIMIT_BYTES),
        name="rank",
    )(sel, gates, counts, lstrict, ucum)


def _sc_mesh():
    return plsc.VectorSubcoreMesh(core_axis_name="c", subcore_axis_name="s",
                                  num_cores=SC_CORES, num_subcores=SC_SUBCORES)


def _sc_scatter_rows(src, idx, n_out_rows):
    n_src, d = src.shape
    w, k = SC_WINDOW, SC_SCATTER_INFLIGHT
    assert idx.shape == (n_src // w, TOP_K, w) and n_src % (w * k * SC_WORKERS) == 0
    per_worker = n_src // (w * SC_WORKERS)
    idx2d = idx.reshape(n_src // w * TOP_K, w)

    @functools.partial(
        pl.kernel, mesh=_sc_mesh(),
        out_type=jax.ShapeDtypeStruct((n_out_rows, d), src.dtype),
        scratch_types=[pltpu.VMEM((k * TOP_K, w), jnp.int32), pltpu.VMEM((k, w, d), src.dtype),
                       pltpu.SemaphoreType.DMA((k,)), pltpu.SemaphoreType.DMA((k,))],
        name="sc_scatter_rows")
    def body(src_hbm, idx_hbm, out_hbm, idx_v, rows_v, sem_load, sem_scatter):
        wid = lax.axis_index("s") * SC_CORES + lax.axis_index("c")

        @pl.loop(0, per_worker // k)
        def _(i):
            win0 = wid * per_worker + i * k
            loads = [pltpu.async_copy(src_hbm.at[pl.ds(pl.multiple_of((win0 + b) * w, w), w)], rows_v.at[b],
                                      sem_load.at[b]) for b in range(k)]
            pltpu.sync_copy(idx_hbm.at[pl.ds(pl.multiple_of(win0 * TOP_K, TOP_K), k * TOP_K)], idx_v)
            scatters = []
            for b in range(k):
                loads[b].wait()
                scatters += [pltpu.async_copy(rows_v.at[b], out_hbm.at[idx_v.at[b * TOP_K + kk]],
                                              sem_scatter.at[b]) for kk in range(TOP_K)]
            for copy in scatters:
                copy.wait()

    return body(src, idx2d)


def _sc_gather_rows(table, idx):
    d = table.shape[1]
    w, k = SC_WINDOW, SC_GATHER_INFLIGHT
    assert idx.shape[0] % (w * k * SC_WORKERS) == 0
    per_worker = idx.shape[0] // (w * SC_WORKERS)
    idx2d = idx.reshape(-1, w)

    @functools.partial(
        pl.kernel, mesh=_sc_mesh(),
        out_type=jax.ShapeDtypeStruct((idx.shape[0], d), table.dtype),
        scratch_types=[pltpu.VMEM((k, w), jnp.int32), pltpu.VMEM((k, w, d), table.dtype),
                       pltpu.SemaphoreType.DMA((k,)), pltpu.SemaphoreType.DMA((k,))],
        name="sc_gather_rows")
    def body(table_hbm, idx_hbm, out_hbm, idx_v, rows_v, sem_gather, sem_store):
        wid = lax.axis_index("s") * SC_CORES + lax.axis_index("c")

        @pl.loop(0, per_worker // k)
        def _(i):
            win0 = wid * per_worker + i * k
            pltpu.sync_copy(idx_hbm.at[pl.ds(pl.multiple_of(win0, k), k)], idx_v)
            gathers = [pltpu.async_copy(table_hbm.at[idx_v.at[b]], rows_v.at[b], sem_gather.at[b])
                       for b in range(k)]
            stores = []
            for b in range(k):
                gathers[b].wait()
                stores.append(pltpu.async_copy(
                    rows_v.at[b], out_hbm.at[pl.ds(pl.multiple_of((win0 + b) * w, w), w)], sem_store.at[b]))
            for copy in stores:
                copy.wait()

    return body(table, idx2d)


def _ffn_kernel(blk_ref, exp_ref, valid_ref, first_ref, next_ref, slot_ref, nused_ref,
                xb_ref, w1_hbm, b1_ref, w2_hbm, b2_ref, yb_ref, w1f_ref, w2f_ref, w1b_ref, w2b_ref, sems):
    j = pl.program_id(0)
    active = j < nused_ref[0]

    def weight_copies(e, slot):
        return (pltpu.make_async_copy(w1_hbm.at[e], w1f_ref.at[slot], sems.at[0, slot]),
                pltpu.make_async_copy(w2_hbm.at[e], w2f_ref.at[slot], sems.at[1, slot]))

    @pl.when(jnp.logical_and(active, first_ref[j] == 1))
    def _():
        e = exp_ref[j]
        slot = slot_ref[j]
        nxt = next_ref[j]

        @pl.when(j == 0)
        def _():
            for copy in weight_copies(e, slot):
                copy.start()

        @pl.when(nxt >= 0)
        def _():
            for copy in weight_copies(nxt, 1 - slot):
                copy.start()

        for copy in weight_copies(e, slot):
            copy.wait()
        w1b_ref[...] = w1f_ref[slot].astype(BF16)
        w2b_ref[...] = w2f_ref[slot].astype(BF16)

    @pl.when(active)
    def _():
        row = lax.broadcasted_iota(jnp.int32, (MOE_BLOCK, 1), 0)
        x = jnp.where(row < valid_ref[j], _unpack_bf16_pairs(xb_ref[...]), 0.0).astype(BF16)
        hc = _dot(x, w1b_ref[...]) + b1_ref[0]
        gate = jnp.minimum(hc[:, :D_FF], SWIGLU_LIMIT)
        up = jnp.clip(hc[:, D_FF:], -SWIGLU_LIMIT, SWIGLU_LIMIT)
        act = (up + 1.0) * gate * _sigmoid(SWIGLU_ALPHA * gate)
        yb_ref[...] = _pack_bf16_pairs(_dot(act.astype(BF16), w2b_ref[...]) + b2_ref[0])


def _ffn(blk_idx, blk_exp, blk_valid, blk_first, blk_next, blk_slot, n_used, xb, w1, b1, w2, b2, n_blocks):
    bm = MOE_BLOCK
    grid_spec = pltpu.PrefetchScalarGridSpec(
        num_scalar_prefetch=7,
        grid=(n_blocks,),
        in_specs=[pl.BlockSpec((bm, D_MODEL // 2), lambda j, bi, be, *_: (bi[j], 0)),
                  pl.BlockSpec(memory_space=pl.ANY),
                  pl.BlockSpec((1, 1, 2 * D_FF), lambda j, bi, be, *_: (be[j], 0, 0)),
                  pl.BlockSpec(memory_space=pl.ANY),
                  pl.BlockSpec((1, 1, D_MODEL), lambda j, bi, be, *_: (be[j], 0, 0))],
        out_specs=pl.BlockSpec((bm, D_MODEL // 2), lambda j, bi, be, *_: (bi[j], 0)),
        scratch_shapes=[pltpu.VMEM((2, D_MODEL, 2 * D_FF), F32), pltpu.VMEM((2, D_FF, D_MODEL), F32),
                        pltpu.VMEM((D_MODEL, 2 * D_FF), BF16), pltpu.VMEM((D_FF, D_MODEL), BF16),
                        pltpu.SemaphoreType.DMA((2, 2))],
    )
    return pl.pallas_call(
        _ffn_kernel,
        grid_spec=grid_spec,
        out_shape=jax.ShapeDtypeStruct(xb.shape, jnp.uint32),
        compiler_params=pltpu.CompilerParams(
            dimension_semantics=("arbitrary",), vmem_limit_bytes=VMEM_LIMIT_BYTES),
        name="expert_ffn",
    )(blk_idx, blk_exp, blk_valid, blk_first, blk_next, blk_slot, n_used, xb, w1, b1, w2, b2)


def _combine_kernel(gk_ref, h1_ref, y0_ref, y1_ref, y2_ref, y3_ref, lng_ref, lnb_ref, *rest):
    out_ref = rest[-1]
    gk = gk_ref[...]
    ffn = gk[:, 0:1] * _unpack_bf16_pairs(y0_ref[...])
    for kk, y_ref in enumerate((y1_ref, y2_ref, y3_ref), start=1):
        ffn = ffn + gk[:, kk:kk + 1] * _unpack_bf16_pairs(y_ref[...])
    out_ref[...] = _layer_norm(DEEPNORM_ALPHA * h1_ref[...] + ffn, lng_ref[...], lnb_ref[...])


def _combine(gk, h1, ysel, lng, lnb, out_rows, out_row0, out_buf=None):
    rows = h1.shape[0]
    tm = COMBINE_ROWS
    assert rows % tm == 0 and out_row0 % tm == 0 and out_row0 + rows <= out_rows
    n_tiles = rows // tm
    out_tile0 = out_row0 // tm
    const = lambda a: pl.BlockSpec(a.shape, lambda i: (0,) * a.ndim)
    ksel = lambda kk: pl.BlockSpec((tm, D_MODEL // 2), lambda i: (kk * n_tiles + i, 0))
    in_specs = [pl.BlockSpec((tm, TOP_K), lambda i: (i, 0)),
                pl.BlockSpec((tm, D_MODEL), lambda i: (i, 0)),
                ksel(0), ksel(1), ksel(2), ksel(3),
                const(lng), const(lnb)]
    operands = [gk, h1, ysel, ysel, ysel, ysel, lng, lnb]
    aliases = {}
    if out_buf is not None:
        assert out_buf.shape == (out_rows, D_MODEL) and out_buf.dtype == F32
        aliases = {len(operands): 0}
        in_specs.append(pl.BlockSpec(memory_space=pl.ANY))
        operands.append(out_buf)
    return pl.pallas_call(
        _combine_kernel,
        grid=(n_tiles,),
        in_specs=in_specs,
        out_specs=pl.BlockSpec((tm, D_MODEL), lambda i: (out_tile0 + i, 0)),
        out_shape=jax.ShapeDtypeStruct((out_rows, D_MODEL), F32),
        input_output_aliases=aliases,
        compiler_params=pltpu.CompilerParams(
            dimension_semantics=("arbitrary",), vmem_limit_bytes=VMEM_LIMIT_BYTES),
        name="combine",
    )(*operands)


def _row(v, width=None):
    v = v.reshape(1, -1).astype(F32)
    if width is not None and v.shape[1] < width:
        v = jnp.pad(v, ((0, 0), (0, width - v.shape[1])))
    return v


def _encode_all(x_a, x_b, meta_tokens, ln_emb_g, ln_emb_b, w_in, conv_w, conv_b, dt_bias, a_log,
                d_skip, ssd_norm_g, i_bias, f_bias, mlstm_norm_g, w_out, ln1_g, ln1_b, w_router, b_router,
                w1, b1, w2, b2, ln2_g, ln2_b):
    n_a, seq_len, _ = x_a.shape
    n_seq = n_a + x_b.shape[0]
    assert seq_len % CHUNK == 0
    n_chunks = seq_len // CHUNK + 1
    n_tok = n_seq * seq_len

    sizes = (1024, CONV_CH, 2 * SSD_HEADS, 512, 512, 1024, 1024, 2 * MLSTM_HEADS, 2 * MLSTM_HEADS)
    offs = [0]
    for s in sizes:
        offs.append(offs[-1] + s)
    w_z, w_xbc, w_dt, w_q, w_k, w_v, w_o, w_i, w_f = [w_in[:, offs[j]:offs[j + 1]] for j in range(9)]
    w_big = jnp.concatenate([w_z, w_xbc, w_q, w_k * (MLSTM_DK ** -0.5), w_v, w_o], axis=1).astype(BF16)
    zpad = jnp.zeros((D_MODEL, LANES - GATE_END), F32)
    gate_cols = []
    for d in range(2):
        gate_cols += [w_dt[:, d * SSD_HEADS:(d + 1) * SSD_HEADS],
                      w_i[:, d * MLSTM_HEADS:(d + 1) * MLSTM_HEADS],
                      w_f[:, d * MLSTM_HEADS:(d + 1) * MLSTM_HEADS], zpad]
    w_gates = jnp.concatenate(gate_cols, axis=1).astype(BF16)
    gbias = [_row(jnp.concatenate([dt_bias[d], i_bias[d], f_bias[d]]), LANES) for d in range(2)]
    alog = [jnp.pad(jnp.broadcast_to(a_log[d].astype(F32)[:, None], (SSD_HEADS, LANES)),
                    ((0, GATE_END - SSD_HEADS), (0, 0))) for d in range(2)]
    head_of_col = jnp.arange(D_MODEL, dtype=jnp.int32) // SSD_HEAD_DIM
    lane_id = jnp.arange(LANES, dtype=jnp.int32)[:, None]
    expand = jnp.concatenate([lane_id == GATE_END + head_of_col[None, :],
                              lane_id == 2 * GATE_END + head_of_col[None, :]], axis=1).astype(BF16)
    dskip = _row(jnp.repeat(d_skip, SSD_HEAD_DIM))
    conv_w8 = jnp.pad(conv_w.astype(F32), ((0, 8 - CONV_W), (0, 0)))
    meta_tile = jnp.pad(meta_tokens.astype(F32), ((PAD_FRONT, INPROJ_ROWS - CHUNK), (0, 0)))

    z, xbc, q, k, v, o, gates_f, gates_b = _inproj(x_a, x_b, meta_tile, _row(ln_emb_g), _row(ln_emb_b),
                                                   w_big, w_gates)
    xs, bm, cm = _conv(xbc, conv_w8, _row(conv_b), n_chunks)
    yb, hb = _mixer_pass(xs, bm, cm, gates_b, q, k, v, gbias[1], alog[1], expand, n_chunks, reverse=True)
    (ycat,) = _mixer_pass(xs, bm, cm, gates_f, q, k, v, gbias[0], alog[0], expand, n_chunks,
                          reverse=False,
                          final_inputs=(z, o, yb, hb, dskip, _row(ssd_norm_g), _row(mlstm_norm_g)))

    wr = jnp.pad(w_router.astype(F32), ((0, 0), (0, LANES - N_EXPERTS)))
    wrh = wr.astype(BF16)
    wrl = (wr - wrh.astype(F32)).astype(BF16)
    w_out_b = w_out.astype(BF16)
    r_i = jnp.arange(RANK_ROWS, dtype=jnp.int32)
    lstrict = (r_i[None, :] < r_i[:, None]).astype(BF16)
    l_i = jnp.arange(LANES, dtype=jnp.int32)
    ucum = (l_i[:, None] <= l_i[None, :]).astype(BF16)
    ycat2d = ycat.reshape(n_tok, 2 * D_MODEL)
    x_2d = [x_a.reshape(-1, D_MODEL), x_b.reshape(-1, D_MODEL)]
    outs = [None, None]
    n_part = seq_len
    for s in range(n_seq):
        part, x_row0 = (0, s * seq_len) if s < n_a else (1, (s - n_a) * seq_len)
        h1, h1p, sel, gates, expert_counts = _epilogue(x_2d[part], x_row0, n_part, ycat2d, s * seq_len, w_out_b,
                                        _row(ln_emb_g), _row(ln_emb_b), _row(ln1_g), _row(ln1_b),
                                        wrh, wrl, _row(b_router, LANES))
        n_blocks = n_part * TOP_K // MOE_BLOCK + N_EXPERTS
        trash_row = n_blocks * MOE_BLOCK
        dest, gk, stats = _rank(sel, gates, expert_counts, lstrict, ucum, trash_row)
        counts = stats[0, :N_EXPERTS].astype(jnp.int32)
        starts = stats[1, :N_EXPERTS].astype(jnp.int32)
        pends = stats[2, :N_EXPERTS].astype(jnp.int32)
        n_used = pends[N_EXPERTS - 1] // MOE_BLOCK
        blk = jnp.minimum(jnp.arange(n_blocks, dtype=jnp.int32), jnp.maximum(n_used - 1, 0))
        blk_exp = jnp.minimum(
            jnp.sum((pends[None, :] <= (blk * MOE_BLOCK)[:, None]).astype(jnp.int32), axis=1),
            N_EXPERTS - 1).astype(jnp.int32)
        e_i = jnp.arange(N_EXPERTS, dtype=jnp.int32)[None, :]
        is_exp = blk_exp[:, None] == e_i
        blk_last = jnp.sum(jnp.where(is_exp, (starts + counts)[None, :], 0), axis=1)
        blk_valid = jnp.clip(blk_last - blk * MOE_BLOCK, 0, MOE_BLOCK).astype(jnp.int32)
        d4 = dest[:, :TOP_K, :]
        dest_km = d4.transpose(1, 0, 2).reshape(-1)
        dest_wm = d4.reshape(-1, TOP_K, RANK_ROWS // SC_WINDOW, SC_WINDOW).transpose(0, 2, 1, 3).reshape(
            n_part // SC_WINDOW, TOP_K, SC_WINDOW)
        xb = _sc_scatter_rows(h1p, dest_wm, trash_row + MOE_BLOCK)
        j_i = jnp.arange(n_blocks, dtype=jnp.int32)
        blk_first = jnp.logical_and(j_i < n_used, jnp.logical_or(j_i == 0, blk_exp != jnp.roll(blk_exp, 1)))
        blk_slot = ((jnp.cumsum(blk_first.astype(jnp.int32)) - 1) % 2).astype(jnp.int32)
        later = jnp.where(jnp.logical_and(e_i > blk_exp[:, None], counts[None, :] > 0), e_i, N_EXPERTS)
        blk_next = jnp.min(later, axis=1)
        blk_next = jnp.where(blk_next < N_EXPERTS, blk_next, -1).astype(jnp.int32)
        yexp = _ffn(blk, blk_exp, blk_valid, blk_first.astype(jnp.int32), blk_next, blk_slot,
                    n_used.reshape(1), xb, w1, b1.reshape(N_EXPERTS, 1, -1), w2,
                    b2.reshape(N_EXPERTS, 1, -1), n_blocks)
        ysel = _sc_gather_rows(yexp, dest_km)
        outs[part] = _combine(gk, h1, ysel, _row(ln2_g), _row(ln2_b), x_2d[part].shape[0], x_row0, outs[part])
    return outs


def kernel(x_prompt, x_sample, meta_tokens, ln_emb_g, ln_emb_b, w_in, conv_w, conv_b, dt_bias, a_log,
           d_skip, ssd_norm_g, i_bias, f_bias, mlstm_norm_g, w_out, ln1_g, ln1_b, w_router, b_router,
           w1, b1, w2, b2, ln2_g, ln2_b):
    assert x_prompt.shape[1:] == x_sample.shape[1:]
    n_p, seq_len, d = x_prompt.shape
    n_s = x_sample.shape[0]
    y_p, y_s = _encode_all(x_prompt.astype(F32), x_sample.astype(F32), meta_tokens, ln_emb_g, ln_emb_b, w_in[0], conv_w[0],
                           conv_b[0], dt_bias[0], a_log[0], d_skip[0], ssd_norm_g[0], i_bias[0], f_bias[0],
                           mlstm_norm_g[0], w_out[0], ln1_g[0], ln1_b[0], w_router[0], b_router[0],
                           w1[0], b1[0], w2[0], b2[0], ln2_g[0], ln2_b[0])
    return (y_p.reshape(n_p, seq_len, d), y_s.reshape(n_s, seq_len, d))
```

```python
import functools

import jax
import jax.numpy as jnp
from jax import lax
from jax.experimental import pallas as pl
from jax.experimental.pallas import tpu as pltpu
from jax.experimental.pallas import tpu_sc as plsc

F32 = jnp.float32
BF16 = jnp.bfloat16

D_MODEL = 1024
N_META = 16
CHUNK = 128
PAD_FRONT = CHUNK - N_META
SSD_HEADS = 16
SSD_HEAD_DIM = 64
SSD_GROUPS = 4
SSD_STATE = 128
HEADS_PER_GROUP = SSD_HEADS // SSD_GROUPS
GROUP_WIDTH = HEADS_PER_GROUP * SSD_HEAD_DIM
CONV_W = 5
CONV_HALF = CONV_W // 2
CONV_CH = D_MODEL + 2 * SSD_GROUPS * SSD_STATE
MLSTM_HEADS = 8
MLSTM_DK = 64
MLSTM_DV = 128
N_EXPERTS = 32
TOP_K = 4
D_FF = D_MODEL
SWIGLU_LIMIT = 7.0
SWIGLU_ALPHA = 1.702
DEEPNORM_ALPHA = 2.0 ** 0.25
LN_EPS = 1e-5
RMS_EPS = 1e-5
NEG_GATE = -1e30
LOG2E = 1.4426950408889634

LANES = 128
BF16_SUBLANES = 16
VMEM_LIMIT_BYTES = 56 * 1024 * 1024

GATE_DT0, GATE_I0, GATE_F0, GATE_END = 0, SSD_HEADS, SSD_HEADS + MLSTM_HEADS, SSD_HEADS + 2 * MLSTM_HEADS

INPROJ_ROWS = 512
EPILOGUE_ROWS = 512
RANK_ROWS = 512
COMBINE_ROWS = 512
EPILOGUE_PARTS = 2
MOE_BLOCK = 512

SC_CORES = 2
SC_SUBCORES = 16
SC_WORKERS = SC_CORES * SC_SUBCORES
SC_WINDOW = 32
SC_SCATTER_INFLIGHT = 2
SC_GATHER_INFLIGHT = 4


def _dot(a, b):
    return jnp.dot(a, b, preferred_element_type=F32)


def _dot_nt(a, b):
    return lax.dot_general(a, b, (((1,), (1,)), ((), ())), preferred_element_type=F32)


def _split3(x):
    hi = x.astype(BF16)
    r1 = x - hi.astype(F32)
    mid = r1.astype(BF16)
    lo = (r1 - mid.astype(F32)).astype(BF16)
    return hi, mid, lo


def _dot_exact_lhs(a_bf16, x):
    hi, mid, lo = _split3(x)
    return _dot(a_bf16, hi) + _dot(a_bf16, mid) + _dot(a_bf16, lo)


def _dot_exact_rhs(x, b_bf16):
    hi, mid, lo = _split3(x)
    return _dot(hi, b_bf16) + _dot(mid, b_bf16) + _dot(lo, b_bf16)


def _layer_norm(x, g, b):
    mu = jnp.mean(x, axis=-1, keepdims=True)
    xc = x - mu
    var = jnp.mean(xc * xc, axis=-1, keepdims=True)
    return xc * lax.rsqrt(var + LN_EPS) * g + b


def _sigmoid(x):
    return 1.0 / (1.0 + jnp.exp(-x))


def _log1p_exp_neg_abs(x):
    return jnp.log(1.0 + jnp.exp(-jnp.abs(x)))


def _pack_bf16_pairs(x):
    n = x.shape[1] // 2
    bits = lax.bitcast_convert_type(x.astype(BF16).astype(F32), jnp.uint32)
    return (bits[:, :n] >> 16) | bits[:, n:]


def _unpack_bf16_pairs(words):
    lo = lax.bitcast_convert_type(words << 16, F32)
    hi = lax.bitcast_convert_type(words & jnp.uint32(0xFFFF0000), F32)
    return jnp.concatenate([lo, hi], axis=1)


def _storage_chunk(c, n_chunks):
    return jnp.where(c == 0, n_chunks - 1, c - 1)


def _inproj_kernel(xa_ref, xb_ref, meta_ref, g_ref, b_ref, wbig_ref, wg_ref,
                   z_ref, xbc_ref, q_ref, k_ref, v_ref, o_ref, gf_ref, gb_ref, *, n_x_tiles, n_a):
    x = jnp.where(pl.program_id(0) < n_a, xa_ref[...], xb_ref[...])
    x = jnp.where(pl.program_id(1) == n_x_tiles, meta_ref[...], x)
    h = _layer_norm(x, g_ref[...], b_ref[...]).astype(BF16)

    def mm(c0, c1):
        return _dot(h, wbig_ref[:, c0:c1]).astype(BF16)

    z_ref[...] = mm(0, 1024)
    xbc_ref[:, 0:1024] = mm(1024, 2048)
    xbc_ref[:, 1024:2048] = mm(2048, 3072)
    q_ref[...] = mm(3072, 3584)
    k_ref[...] = mm(3584, 4096)
    v_ref[...] = mm(4096, 5120)
    o_ref[...] = mm(5120, 6144)
    gates = _dot(h, wg_ref[...])
    gf_ref[...] = gates[:, :LANES]
    gb_ref[...] = gates[:, LANES:]


def _inproj(x_a, x_b, meta_tile, ln_g, ln_b, w_big, w_gates):
    n_a, seq_len, _ = x_a.shape
    n_seq = n_a + x_b.shape[0]
    tm = INPROJ_ROWS
    assert seq_len % tm == 0 and tm >= CHUNK
    n_x_tiles = seq_len // tm
    rows = seq_len + CHUNK
    row_spec = lambda n: pl.BlockSpec((None, tm, n), lambda b, i: (b, i, 0))
    const = lambda a: pl.BlockSpec(a.shape, lambda b, i: (0,) * a.ndim)
    resident = lambda a: pl.BlockSpec(a.shape, lambda b, i: (0,) * a.ndim, pipeline_mode=pl.Buffered(1))
    widths = (1024, CONV_CH, 512, 512, 1024, 1024)
    out_shapes = [jax.ShapeDtypeStruct((n_seq, rows, w), BF16) for w in widths]
    out_shapes += [jax.ShapeDtypeStruct((n_seq, rows, LANES), F32)] * 2
    return pl.pallas_call(
        functools.partial(_inproj_kernel, n_x_tiles=n_x_tiles, n_a=n_a),
        grid=(n_seq, n_x_tiles + 1),
        in_specs=[pl.BlockSpec((None, tm, D_MODEL), lambda b, i: (
                      jnp.minimum(b, n_a - 1),
                      jnp.where(b < n_a, jnp.minimum(i, n_x_tiles - 1), n_x_tiles - 1), 0)),
                  pl.BlockSpec((None, tm, D_MODEL), lambda b, i: (
                      jnp.maximum(b - n_a, 0),
                      jnp.where(b < n_a, 0, jnp.minimum(i, n_x_tiles - 1)), 0)),
                  resident(meta_tile), const(ln_g), const(ln_b), resident(w_big), resident(w_gates)],
        out_specs=[row_spec(s.shape[2]) for s in out_shapes],
        out_shape=out_shapes,
        compiler_params=pltpu.CompilerParams(
            dimension_semantics=("arbitrary", "arbitrary"), vmem_limit_bytes=VMEM_LIMIT_BYTES),
        name="inproj",
    )(x_a, x_b, meta_tile, ln_g, ln_b, w_big, w_gates)


def _conv_kernel(prev_ref, main_ref, next_ref, shift_ref, w_ref, b_ref, xs_ref, bm_ref, cm_ref,
                 *, n_x_tiles, chunks_per_tile):
    i = pl.program_id(1)
    w = w_ref[...]
    bias = b_ref[...]
    shift = shift_ref[...]

    def conv_chunk(before, rows, after, pad_rows, j):
        shifted = _dot(shift, jnp.concatenate([before, rows, after], axis=0).astype(BF16))
        acc = bias + rows * w[CONV_HALF:CONV_HALF + 1, :]
        for jj, t in enumerate(t for t in range(CONV_W) if t != CONV_HALF):
            acc = acc + shifted[jj * CHUNK:(jj + 1) * CHUNK, :] * w[t:t + 1, :]
        y = acc * _sigmoid(acc)
        if pad_rows is not None:
            y = jnp.where(pad_rows, 0.0, y)
        r = slice(j * CHUNK, (j + 1) * CHUNK)
        xs_ref[r, :] = y[:, :D_MODEL].astype(BF16)
        bm_ref[r, :] = y[:, D_MODEL:D_MODEL + 512].astype(BF16)
        cm_ref[r, :] = y[:, D_MODEL + 512:].astype(BF16)

    @pl.when(i < n_x_tiles)
    def _():
        tile = main_ref[...].astype(F32)
        for j in range(chunks_per_tile):
            lo, hi = j * CHUNK, (j + 1) * CHUNK
            before = prev_ref[...].astype(F32) if j == 0 else tile[lo - BF16_SUBLANES:lo, :]
            if j == chunks_per_tile - 1:
                after = jnp.where(i == n_x_tiles - 1, 0.0, next_ref[...].astype(F32))
            else:
                after = tile[hi:hi + BF16_SUBLANES, :]
            conv_chunk(before, tile[lo:hi, :], after, None, j)

    @pl.when(i == n_x_tiles)
    def _():
        row = lax.broadcasted_iota(jnp.int32, (CHUNK, 1), 0)
        pad_rows = row < PAD_FRONT
        rows = jnp.where(pad_rows, 0.0, main_ref[0:CHUNK, :].astype(F32))
        conv_chunk(jnp.zeros((BF16_SUBLANES, CONV_CH), F32), rows, next_ref[...].astype(F32), pad_rows, 0)


def _conv(xbc, conv_w8, conv_b, n_chunks):
    n_seq, rows, _ = xbc.shape
    tm = INPROJ_ROWS
    seq_len = (n_chunks - 1) * CHUNK
    assert seq_len % tm == 0 and tm % CHUNK == 0
    n_x_tiles = seq_len // tm
    halo_per_tile = tm // BF16_SUBLANES
    meta_last_halo = rows // BF16_SUBLANES - 1

    def prev_map(b, i):
        before_tile = jnp.maximum(jnp.minimum(i, n_x_tiles - 1) * halo_per_tile - 1, 0)
        return (b, jnp.where(i == 0, meta_last_halo, before_tile), 0)

    def next_map(b, i):
        return (b, jnp.where(i >= n_x_tiles - 1, 0, (i + 1) * halo_per_tile) , 0)

    const = lambda a: pl.BlockSpec(a.shape, lambda b, i: (0,) * a.ndim)
    l_i = jnp.arange(CHUNK, dtype=jnp.int32)[:, None]
    j_i = jnp.arange(CHUNK + 2 * BF16_SUBLANES, dtype=jnp.int32)[None, :]
    shifts = jnp.concatenate([(j_i == BF16_SUBLANES + l_i + t - CONV_HALF)
                              for t in range(CONV_W) if t != CONV_HALF], axis=0).astype(BF16)
    out_shapes = [jax.ShapeDtypeStruct((n_seq, rows, D_MODEL), BF16),
                  jax.ShapeDtypeStruct((n_seq, rows, 512), BF16),
                  jax.ShapeDtypeStruct((n_seq, rows, 512), BF16)]
    tile_spec = lambda n: pl.BlockSpec((None, tm, n), lambda b, i: (b, i, 0))
    return pl.pallas_call(
        functools.partial(_conv_kernel, n_x_tiles=n_x_tiles, chunks_per_tile=tm // CHUNK),
        grid=(n_seq, n_x_tiles + 1),
        in_specs=[pl.BlockSpec((None, BF16_SUBLANES, CONV_CH), prev_map),
                  tile_spec(CONV_CH),
                  pl.BlockSpec((None, BF16_SUBLANES, CONV_CH), next_map),
                  const(shifts), const(conv_w8), const(conv_b)],
        out_specs=[tile_spec(s.shape[2]) for s in out_shapes],
        out_shape=out_shapes,
        compiler_params=pltpu.CompilerParams(
            dimension_semantics=("arbitrary", "arbitrary"), vmem_limit_bytes=VMEM_LIMIT_BYTES),
        name="conv",
    )(xbc, xbc, xbc, shifts, conv_w8, conv_b)


def _mixer_kernel(*refs, reverse, final, n_chunks, n_seq):
    if final:
        (xs_ref, bm_ref, cm_ref, g_ref, q_ref, k_ref, v_ref, z_ref, o_ref, yb_ref, hb_ref,
         gbias_ref, alog_ref, expand_ref, dskip_ref, ngs_ref, ngm_ref,
         ycat_ref, s_ref, cst_ref, m_ref) = refs
    else:
        (xs_ref, bm_ref, cm_ref, g_ref, q_ref, k_ref, v_ref,
         gbias_ref, alog_ref, expand_ref,
         yout_ref, hout_ref, s_ref, cst_ref, m_ref) = refs

    t = pl.program_id(0)
    c = (n_chunks - 1 - t) if reverse else t
    end = 0 if reverse else CHUNK - 1

    @pl.when(t == 0)
    def _():
        s_ref[...] = jnp.zeros_like(s_ref)
        cst_ref[...] = jnp.zeros_like(cst_ref)
        m_ref[...] = jnp.zeros_like(m_ref)

    row = lax.broadcasted_iota(jnp.int32, (CHUNK, 1), 0)
    col = lax.broadcasted_iota(jnp.int32, (1, CHUNK), 1)
    lane = col
    allowed = (col >= row) if reverse else (col <= row)
    tri = allowed.astype(BF16)
    tri_t = ((row >= col) if reverse else (row <= col)).astype(BF16)
    feat = lax.broadcasted_iota(jnp.int32, (GATE_END, 1), 0)
    is_dt = feat < GATE_I0
    is_i = jnp.logical_and(feat >= GATE_I0, feat < GATE_F0)
    is_f = feat >= GATE_F0
    pad_cols = jnp.logical_and(c == 0, col < PAD_FRONT)
    a_coef = -jnp.exp(alog_ref[...]) * LOG2E
    expand = expand_ref[...]
    left_half = lane < SSD_HEAD_DIM
    right_half = jnp.logical_not(left_half)
    top_half = row < MLSTM_DK
    ones_blk = jnp.ones((CHUNK, MLSTM_DV), BF16)
    full = (CHUNK, LANES)

    def independent_products(b):
        cgs = [cm_ref[b, :, g * SSD_STATE:(g + 1) * SSD_STATE] for g in range(SSD_GROUPS)]
        bgs = [bm_ref[b, :, g * SSD_STATE:(g + 1) * SSD_STATE] for g in range(SSD_GROUPS)]
        cbs = [_dot_nt(cgs[g], bgs[g]) for g in range(SSD_GROUPS)]
        bg_ts = [bgs[g].astype(F32).T.astype(BF16) for g in range(SSD_GROUPS)]
        q_pairs = [q_ref[b, :, p * LANES:(p + 1) * LANES] for p in range(MLSTM_HEADS // 2)]
        k_pairs = [k_ref[b, :, p * LANES:(p + 1) * LANES] for p in range(MLSTM_HEADS // 2)]
        qks = [_dot_nt(jnp.where(left_half if h % 2 == 0 else right_half, q_pairs[h // 2],
                                 jnp.zeros_like(q_pairs[h // 2])), k_pairs[h // 2])
               for h in range(MLSTM_HEADS)]
        k_pair_ts = [k_pairs[p].astype(F32).T for p in range(MLSTM_HEADS // 2)]
        xs = xs_ref[b]
        xs_rhs = []
        for p in range(SSD_HEADS // 2):
            xs_pair = xs[:, p * LANES:(p + 1) * LANES]
            zero_pair = jnp.zeros_like(xs_pair)
            xs_rhs.append(jnp.concatenate([jnp.where(left_half, xs_pair, zero_pair),
                                           jnp.where(right_half, xs_pair, zero_pair)], axis=0))
        z_gate = o_gate = None
        if final:
            zz = z_ref[b].astype(F32)
            z_gate = zz * _sigmoid(zz)
            o_gate = _sigmoid(o_ref[b].astype(F32))
        return cgs, cbs, bg_ts, q_pairs, qks, k_pair_ts, xs, xs_rhs, z_gate, o_gate

    def one_sequence(b, products):
        cgs, cbs, bg_ts, q_pairs, qks, k_pair_ts, xs, xs_rhs, z_gate, o_gate = products

        gr = (g_ref[b] + gbias_ref[...]).T[0:GATE_END, :]
        lse = _log1p_exp_neg_abs(gr)
        val_t = jnp.where(is_dt, jnp.maximum(gr, 0.0) + lse, jnp.where(is_i, gr, jnp.minimum(gr, 0.0) - lse))
        val_t = jnp.where(pad_cols, jnp.where(is_i, NEG_GATE, 0.0), val_t)
        u_t = jnp.where(is_dt, val_t * a_coef, jnp.where(is_f, val_t * LOG2E, 0.0))
        cums_t = _dot_exact_rhs(u_t, tri_t)
        cums_end = jnp.broadcast_to(cums_t[:, end:end + 1], cums_t.shape)
        p1_t = jnp.exp2(cums_t)
        p2_t = jnp.exp2(cums_end - cums_t) * val_t
        packed = jnp.concatenate([cums_t, p1_t, p2_t, val_t], axis=0).T
        ex = _dot(packed.astype(BF16), expand)
        ex1 = ex[:, :D_MODEL]
        ex2 = ex[:, D_MODEL:]
        chunk_decay = _dot_exact_rhs(jnp.broadcast_to(packed[end:end + 1, :], (8, LANES)),
                                     expand[:, :D_MODEL])[0:1, :]

        xsf = xs.astype(F32)
        xs_w = (xsf * ex2).astype(BF16)
        src_term = jnp.log(val_t[0:SSD_HEADS, :]) * LOG2E - cums_t[0:SSD_HEADS, :]
        for g in range(SSD_GROUPS):
            gs = slice(g * GROUP_WIDTH, (g + 1) * GROUP_WIDTH)
            s_new_all[b].append(chunk_decay[:, gs] * s_old[b][g] + _dot(bg_ts[g], xs_w[:, gs]))
        pair_lhs = []
        for pair in range(SSD_HEADS // 2):
            cb = cbs[pair // (HEADS_PER_GROUP // 2)]
            m_mats = []
            for h in (2 * pair, 2 * pair + 1):
                seg = jnp.broadcast_to(packed[:, h:h + 1], full) + src_term[h:h + 1, :]
                m_mats.append((cb * jnp.exp2(jnp.where(allowed, seg, -jnp.inf))).astype(BF16))
            pair_lhs.append(jnp.concatenate(m_mats, axis=1))
        y_diag = jnp.concatenate([_dot(pair_lhs[p], xs_rhs[p]) for p in range(SSD_HEADS // 2)], axis=1)
        y_off = jnp.concatenate([_dot(cgs[g], s_old[b][g].astype(BF16)) for g in range(SSD_GROUPS)], axis=1)
        y_ssd = y_diag + y_off * ex1

        bcum_t = cums_t[GATE_F0:GATE_END, :]
        ip_t = val_t[GATE_I0:GATE_F0, :] * LOG2E
        rep = bcum_t.shape
        g_rep = jnp.broadcast_to(bcum_t[:, end:end + 1], rep)
        a_t = g_rep - bcum_t + ip_t
        a_max = jnp.broadcast_to(jnp.max(a_t, axis=1, keepdims=True), rep)
        w_t = jnp.exp2(a_t - a_max)
        m_prev = m_old[b]
        m_new = jnp.maximum(g_rep + m_prev, a_max)
        s_prev = jnp.exp2(g_rep + m_prev - m_new)
        s_new = jnp.exp2(a_max - m_new)
        r_t = ip_t - bcum_t
        v_heads = [v_ref[b, :, h * MLSTM_DV:(h + 1) * MLSTM_DV] for h in range(MLSTM_HEADS)]
        for pair in range(MLSTM_HEADS // 2):
            h0, h1 = 2 * pair, 2 * pair + 1
            cst = cst_old[b][pair]
            w_rows = jnp.where(top_half, w_t[h0:h0 + 1, :], w_t[h1:h1 + 1, :])
            kw = (k_pair_ts[pair] * w_rows).astype(BF16)
            full_kv = _dot(kw, jnp.concatenate([v_heads[h0], v_heads[h1], ones_blk], axis=1))
            kvn = jnp.concatenate(
                [jnp.where(top_half, full_kv[:, :MLSTM_DV], full_kv[:, MLSTM_DV:2 * MLSTM_DV]),
                 full_kv[:, 2 * MLSTM_DV:]], axis=1)
            sp_rows = jnp.where(top_half, s_prev[h0:h0 + 1, :], s_prev[h1:h1 + 1, :])
            sn_rows = jnp.where(top_half, s_new[h0:h0 + 1, :], s_new[h1:h1 + 1, :])
            cst_new_all[b].append(jnp.concatenate([sp_rows, sp_rows], axis=1) * cst
                                  + jnp.concatenate([sn_rows, sn_rows], axis=1) * kvn)
        h_heads = []
        for pair in range(MLSTM_HEADS // 2):
            h0, h1 = 2 * pair, 2 * pair + 1
            q_pair_f = q_pairs[pair].astype(F32)
            cst_b = cst_old[b][pair].astype(BF16)
            for hh, h in enumerate((h0, h1)):
                keep = left_half if hh == 0 else right_half
                vh = v_heads[h]
                qk = qks[h]
                bc = jnp.broadcast_to(packed[:, GATE_F0 + h:GATE_F0 + h + 1], full)
                dlog = jnp.where(allowed, bc + r_t[h:h + 1, :], -jnp.inf)
                m_intra = jnp.broadcast_to(jnp.max(dlog, axis=1, keepdims=True), full)
                m_inter = bc + m_prev[h:h + 1, :]
                m_t = jnp.maximum(m_inter, m_intra)
                s_mat = (qk * jnp.exp2(dlog - m_t)).astype(BF16)
                qs = (jnp.where(keep, q_pair_f, 0.0) * jnp.exp2(m_inter - m_t)).astype(BF16)
                tot = _dot(jnp.concatenate([s_mat, qs], axis=1),
                           jnp.concatenate([jnp.concatenate([vh, ones_blk], axis=1), cst_b], axis=0))
                num = tot[:, :MLSTM_DV]
                den = tot[:, MLSTM_DV:]
                h_heads.append(num / jnp.maximum(jnp.abs(den), jnp.exp2(-m_t)))
        m_new_all[b] = m_new
        h_ml = jnp.concatenate(h_heads, axis=1)
        if not final:
            return y_ssd.astype(BF16), h_ml.astype(BF16)

        y_tot = y_ssd + yb_ref[b].astype(F32) + dskip_ref[...] * xsf
        y2 = y_tot * z_gate
        y_n = y2 * lax.rsqrt(jnp.mean(y2 * y2, axis=-1, keepdims=True) + RMS_EPS) * ngs_ref[...]
        h_tot = h_ml + hb_ref[b].astype(F32)
        segs = []
        for h in range(MLSTM_HEADS):
            seg = h_tot[:, h * MLSTM_DV:(h + 1) * MLSTM_DV]
            segs.append(seg * lax.rsqrt(jnp.mean(seg * seg, axis=-1, keepdims=True) + RMS_EPS))
        h_n = jnp.concatenate(segs, axis=1) * ngm_ref[...]
        y_ml = o_gate * h_n
        return y_n.astype(BF16), y_ml.astype(BF16)

    s_old = [[s_ref[b, g] for g in range(SSD_GROUPS)] for b in range(n_seq)]
    cst_old = [[cst_ref[b, p] for p in range(MLSTM_HEADS // 2)] for b in range(n_seq)]
    m_old = [m_ref[b] for b in range(n_seq)]
    s_new_all = [[] for _ in range(n_seq)]
    cst_new_all = [[] for _ in range(n_seq)]
    m_new_all = [None] * n_seq
    results = [one_sequence(b, independent_products(b)) for b in range(n_seq)]
    for b in range(n_seq):
        for g in range(SSD_GROUPS):
            s_ref[b, g] = s_new_all[b][g]
        for p in range(MLSTM_HEADS // 2):
            cst_ref[b, p] = cst_new_all[b][p]
        m_ref[b] = m_new_all[b]

    @pl.when(c > 0)
    def _():
        for b, (first, second) in enumerate(results):
            if final:
                ycat_ref[b, :, :D_MODEL] = first
                ycat_ref[b, :, D_MODEL:] = second
            else:
                yout_ref[b] = first
                hout_ref[b] = second


def _mixer_pass(xs, bm, cm, gates, q, k, v, gbias, alog, expand, n_chunks, *, reverse, final_inputs=None):
    final = final_inputs is not None
    n_seq = xs.shape[0]
    seq_len = (n_chunks - 1) * CHUNK

    def chunk_of(t):
        return (n_chunks - 1 - t) if reverse else t

    def pad_map(t):
        return (0, _storage_chunk(chunk_of(t), n_chunks), 0)

    def out_map(t):
        return (0, jnp.maximum(chunk_of(t) - 1, 0), 0)

    const = lambda a: pl.BlockSpec(a.shape, lambda t: (0,) * a.ndim)
    pad_spec = lambda n: pl.BlockSpec((n_seq, CHUNK, n), pad_map)
    out_spec = lambda n: pl.BlockSpec((n_seq, CHUNK, n), out_map)
    in_arrays = [xs, bm, cm, gates, q, k, v]
    in_specs = [pad_spec(a.shape[2]) for a in in_arrays]
    if final:
        z, o, yb, hb, dskip, ngs, ngm = final_inputs
        in_arrays += [z, o, yb, hb]
        in_specs += [pad_spec(1024), pad_spec(1024), out_spec(1024), out_spec(1024)]
        in_arrays += [gbias, alog, expand, dskip, ngs, ngm]
        in_specs += [const(a) for a in (gbias, alog, expand, dskip, ngs, ngm)]
        out_shape = [jax.ShapeDtypeStruct((n_seq, seq_len, 2 * D_MODEL), BF16)]
        out_specs = [out_spec(2 * D_MODEL)]
    else:
        in_arrays += [gbias, alog, expand]
        in_specs += [const(a) for a in (gbias, alog, expand)]
        out_shape = [jax.ShapeDtypeStruct((n_seq, seq_len, D_MODEL), BF16),
                     jax.ShapeDtypeStruct((n_seq, seq_len, D_MODEL), BF16)]
        out_specs = [out_spec(D_MODEL), out_spec(D_MODEL)]
    return pl.pallas_call(
        functools.partial(_mixer_kernel, reverse=reverse, final=final, n_chunks=n_chunks, n_seq=n_seq),
        grid=(n_chunks,),
        in_specs=in_specs,
        out_specs=out_specs,
        out_shape=out_shape,
        scratch_shapes=[pltpu.VMEM((n_seq, SSD_GROUPS, SSD_STATE, GROUP_WIDTH), F32),
                        pltpu.VMEM((n_seq, MLSTM_HEADS // 2, 2 * MLSTM_DK, 2 * MLSTM_DV), F32),
                        pltpu.VMEM((n_seq, MLSTM_HEADS, LANES), F32)],
        compiler_params=pltpu.CompilerParams(
            dimension_semantics=("arbitrary",), vmem_limit_bytes=VMEM_LIMIT_BYTES),
        name="mixer_fwd" if final else "mixer_bwd",
    )(*in_arrays)


def _epilogue_kernel(x_ref, ycat_ref, wout_ref, lng0_ref, lnb0_ref, lng1_ref, lnb1_ref,
                     wrh_ref, wrl_ref, br_ref, h1_ref, h1p_ref, sel_ref, gate_ref, cnt_ref):
    lane = lax.broadcasted_iota(jnp.int32, (1, LANES), 1)
    lane_f = lane.astype(F32)
    part_rows = x_ref.shape[0] // EPILOGUE_PARTS
    mixes = [_dot(ycat_ref[p * part_rows:(p + 1) * part_rows, :], wout_ref[...]) for p in range(EPILOGUE_PARTS)]
    colsum = jnp.zeros((1, LANES), F32)
    for p, mix in enumerate(mixes):
        r = slice(p * part_rows, (p + 1) * part_rows)
        h0 = _layer_norm(x_ref[r, :], lng0_ref[...], lnb0_ref[...])
        h1 = _layer_norm(DEEPNORM_ALPHA * h0 + mix, lng1_ref[...], lnb1_ref[...])
        h1_ref[r, :] = h1
        h1p_ref[r, :] = _pack_bf16_pairs(h1)
        hh = h1.astype(BF16)
        hl = (h1 - hh.astype(F32)).astype(BF16)
        both = _dot(hh, jnp.concatenate([wrh_ref[...], wrl_ref[...]], axis=1))
        logits = both[:, :LANES] + both[:, LANES:] + _dot(hl, wrh_ref[...]) + br_ref[...]
        logits = jnp.where(lane < N_EXPERTS, logits, -jnp.inf)
        work = logits
        sel = jnp.zeros(logits.shape, jnp.bool_)
        top = None
        for _ in range(TOP_K):
            m = jnp.max(work, axis=-1, keepdims=True)
            if top is None:
                top = m
            first = jnp.min(jnp.where(work == m, lane_f, float(LANES)), axis=-1, keepdims=True)
            pick = lane_f == first
            sel = jnp.logical_or(sel, pick)
            work = jnp.where(pick, -jnp.inf, work)
        e = jnp.where(sel, jnp.exp(logits - top), 0.0)
        gate_ref[r, :] = e / jnp.sum(e, axis=-1, keepdims=True)
        sel_f = sel.astype(F32)
        sel_ref[r, :] = sel_f
        colsum = colsum + jnp.sum(sel_f, axis=0, keepdims=True)

    @pl.when(pl.program_id(0) == 0)
    def _():
        cnt_ref[...] = jnp.zeros_like(cnt_ref)

    cnt_ref[0:1, :] = cnt_ref[0:1, :] + colsum


def _epilogue(x, ycat, ycat_row0, w_out, lng0, lnb0, lng1, lnb1, wrh, wrl, br):
    rows = x.shape[0]
    tm = EPILOGUE_ROWS
    assert rows % tm == 0 and ycat_row0 % tm == 0
    tile0 = ycat_row0 // tm
    row_spec = lambda n: pl.BlockSpec((tm, n), lambda i: (i, 0))
    const = lambda a: pl.BlockSpec(a.shape, lambda i: (0,) * a.ndim)
    out_shape = [jax.ShapeDtypeStruct((rows, D_MODEL), F32),
                 jax.ShapeDtypeStruct((rows, D_MODEL // 2), jnp.uint32),
                 jax.ShapeDtypeStruct((rows, LANES), F32),
                 jax.ShapeDtypeStruct((rows, LANES), F32),
                 jax.ShapeDtypeStruct((8, LANES), F32)]
    consts = (w_out, lng0, lnb0, lng1, lnb1, wrh, wrl, br)
    return pl.pallas_call(
        _epilogue_kernel,
        grid=(rows // tm,),
        in_specs=[row_spec(D_MODEL), pl.BlockSpec((tm, 2 * D_MODEL), lambda i: (tile0 + i, 0))]
                 + [const(a) for a in consts],
        out_specs=[row_spec(D_MODEL), row_spec(D_MODEL // 2), row_spec(LANES), row_spec(LANES),
                   pl.BlockSpec((8, LANES), lambda i: (0, 0))],
        out_shape=out_shape,
        compiler_params=pltpu.CompilerParams(
            dimension_semantics=("arbitrary",), vmem_limit_bytes=VMEM_LIMIT_BYTES),
        name="epilogue",
    )(x, ycat, *consts)


def _rank_kernel(sel_ref, gate_ref, cnt_ref, lstrict_ref, ucum_ref, dest_ref, gk_ref, stats_ref,
                 base_ref, *, trash_row):
    i = pl.program_id(0)
    sel = sel_ref[...]
    colsum = jnp.sum(sel, axis=0, keepdims=True)

    @pl.when(i == 0)
    def _():
        counts = cnt_ref[0:1, :]
        padded = jnp.ceil(counts / MOE_BLOCK) * MOE_BLOCK
        pend = _dot_exact_rhs(jnp.broadcast_to(padded, (8, LANES)), ucum_ref[...])[0:1, :]
        stats_ref[0:1, :] = counts
        stats_ref[1:2, :] = pend - padded
        stats_ref[2:3, :] = pend
        stats_ref[3:8, :] = jnp.zeros((5, LANES), F32)
        base_ref[...] = jnp.broadcast_to(pend - padded, base_ref.shape)

    before = _dot(lstrict_ref[...], sel.astype(BF16))
    pos = base_ref[0:1, :] + before
    base_ref[0:1, :] = base_ref[0:1, :] + colsum
    work = jnp.where(sel > 0.0, pos + 1.0, 0.0)
    gates = gate_ref[...]
    lane = lax.broadcasted_iota(jnp.int32, (1, LANES), 1)
    dest_cols = jnp.zeros(sel.shape, F32)
    for kk in range(TOP_K):
        m = jnp.max(work, axis=-1, keepdims=True)
        pick = jnp.logical_and(work == m, m > 0.0)
        gk_ref[:, kk:kk + 1] = jnp.sum(jnp.where(pick, gates, 0.0), axis=-1, keepdims=True)
        dest_cols = jnp.where(lane == kk, jnp.where(m > 0.0, m - 1.0, float(trash_row)), dest_cols)
        work = jnp.where(pick, 0.0, work)
    dest_ref[...] = dest_cols.T[0:8, :].astype(jnp.int32)


def _rank(sel, gates, counts, lstrict, ucum, trash_row):
    rows = sel.shape[0]
    tm = RANK_ROWS
    assert rows % tm == 0
    row_spec = lambda n: pl.BlockSpec((tm, n), lambda i: (i, 0))
    const = lambda a: pl.BlockSpec(a.shape, lambda i: (0,) * a.ndim)
    return pl.pallas_call(
        functools.partial(_rank_kernel, trash_row=trash_row),
        grid=(rows // tm,),
        in_specs=[row_spec(LANES), row_spec(LANES), const(counts), const(lstrict), const(ucum)],
        out_specs=[pl.BlockSpec((None, 8, tm), lambda i: (i, 0, 0)), row_spec(TOP_K),
                   pl.BlockSpec((8, LANES), lambda i: (0, 0))],
        out_shape=[jax.ShapeDtypeStruct((rows // tm, 8, tm), jnp.int32),
                   jax.ShapeDtypeStruct((rows, TOP_K), F32),
                   jax.ShapeDtypeStruct((8, LANES), F32)],
        scratch_shapes=[pltpu.VMEM((8, LANES), F32)],
        compiler_params=pltpu.CompilerParams(
            dimension_semantics=("arbitrary",), vmem_limit_bytes=VMEM_LIMIT_BYTES),
        name="rank",
    )(sel, gates, counts, lstrict, ucum)


def _sc_mesh():
    return plsc.VectorSubcoreMesh(core_axis_name="c", subcore_axis_name="s",
                                  num_cores=SC_CORES, num_subcores=SC_SUBCORES)


def _sc_scatter_rows(src, idx, n_out_rows):
    n_src, d = src.shape
    w, k = SC_WINDOW, SC_SCATTER_INFLIGHT
    assert idx.shape == (n_src // w, TOP_K, w) and n_src % (w * k * SC_WORKERS) == 0
    per_worker = n_src // (w * SC_WORKERS)
    idx2d = idx.reshape(n_src // w * TOP_K, w)

    @functools.partial(
        pl.kernel, mesh=_sc_mesh(),
        out_type=jax.ShapeDtypeStruct((n_out_rows, d), src.dtype),
        scratch_types=[pltpu.VMEM((k * TOP_K, w), jnp.int32), pltpu.VMEM((k, w, d), src.dtype),
                       pltpu.SemaphoreType.DMA((k,)), pltpu.SemaphoreType.DMA((k,))],
        name="sc_scatter_rows")
    def body(src_hbm, idx_hbm, out_hbm, idx_v, rows_v, sem_load, sem_scatter):
        wid = lax.axis_index("s") * SC_CORES + lax.axis_index("c")

        @pl.loop(0, per_worker // k)
        def _(i):
            win0 = wid * per_worker + i * k
            loads = [pltpu.async_copy(src_hbm.at[pl.ds(pl.multiple_of((win0 + b) * w, w), w)], rows_v.at[b],
                                      sem_load.at[b]) for b in range(k)]
            pltpu.sync_copy(idx_hbm.at[pl.ds(pl.multiple_of(win0 * TOP_K, TOP_K), k * TOP_K)], idx_v)
            scatters = []
            for b in range(k):
                loads[b].wait()
                scatters += [pltpu.async_copy(rows_v.at[b], out_hbm.at[idx_v.at[b * TOP_K + kk]],
                                              sem_scatter.at[b]) for kk in range(TOP_K)]
            for copy in scatters:
                copy.wait()

    return body(src, idx2d)


def _sc_gather_rows(table, idx):
    d = table.shape[1]
    w, k = SC_WINDOW, SC_GATHER_INFLIGHT
    assert idx.shape[0] % (w * k * SC_WORKERS) == 0
    per_worker = idx.shape[0] // (w * SC_WORKERS)
    idx2d = idx.reshape(-1, w)

    @functools.partial(
        pl.kernel, mesh=_sc_mesh(),
        out_type=jax.ShapeDtypeStruct((idx.shape[0], d), table.dtype),
        scratch_types=[pltpu.VMEM((k, w), jnp.int32), pltpu.VMEM((k, w, d), table.dtype),
                       pltpu.SemaphoreType.DMA((k,)), pltpu.SemaphoreType.DMA((k,))],
        name="sc_gather_rows")
    def body(table_hbm, idx_hbm, out_hbm, idx_v, rows_v, sem_gather, sem_store):
        wid = lax.axis_index("s") * SC_CORES + lax.axis_index("c")

        @pl.loop(0, per_worker // k)
        def _(i):
            win0 = wid * per_worker + i * k
            pltpu.sync_copy(idx_hbm.at[pl.ds(pl.multiple_of(win0, k), k)], idx_v)
            gathers = [pltpu.async_copy(table_hbm.at[idx_v.at[b]], rows_v.at[b], sem_gather.at[b])
                       for b in range(k)]
            stores = []
            for b in range(k):
                gathers[b].wait()
                stores.append(pltpu.async_copy(
                    rows_v.at[b], out_hbm.at[pl.ds(pl.multiple_of((win0 + b) * w, w), w)], sem_store.at[b]))
            for copy in stores:
                copy.wait()

    return body(table, idx2d)


def _ffn_kernel(blk_ref, exp_ref, valid_ref, first_ref, next_ref, slot_ref, nused_ref,
                xb_ref, w1_hbm, b1_ref, w2_hbm, b2_ref, yb_ref, w1f_ref, w2f_ref, w1b_ref, w2b_ref, sems):
    j = pl.program_id(0)
    active = j < nused_ref[0]

    def weight_copies(e, slot):
        return (pltpu.make_async_copy(w1_hbm.at[e], w1f_ref.at[slot], sems.at[0, slot]),
                pltpu.make_async_copy(w2_hbm.at[e], w2f_ref.at[slot], sems.at[1, slot]))

    @pl.when(jnp.logical_and(active, first_ref[j] == 1))
    def _():
        e = exp_ref[j]
        slot = slot_ref[j]
        nxt = next_ref[j]

        @pl.when(j == 0)
        def _():
            for copy in weight_copies(e, slot):
                copy.start()

        @pl.when(nxt >= 0)
        def _():
            for copy in weight_copies(nxt, 1 - slot):
                copy.start()

        for copy in weight_copies(e, slot):
            copy.wait()
        w1b_ref[...] = w1f_ref[slot].astype(BF16)
        w2b_ref[...] = w2f_ref[slot].astype(BF16)

    @pl.when(active)
    def _():
        row = lax.broadcasted_iota(jnp.int32, (MOE_BLOCK, 1), 0)
        x = jnp.where(row < valid_ref[j], _unpack_bf16_pairs(xb_ref[...]), 0.0).astype(BF16)
        hc = _dot(x, w1b_ref[...]) + b1_ref[0]
        gate = jnp.minimum(hc[:, :D_FF], SWIGLU_LIMIT)
        up = jnp.clip(hc[:, D_FF:], -SWIGLU_LIMIT, SWIGLU_LIMIT)
        act = (up + 1.0) * gate * _sigmoid(SWIGLU_ALPHA * gate)
        yb_ref[...] = _pack_bf16_pairs(_dot(act.astype(BF16), w2b_ref[...]) + b2_ref[0])


def _ffn(blk_idx, blk_exp, blk_valid, blk_first, blk_next, blk_slot, n_used, xb, w1, b1, w2, b2, n_blocks):
    bm = MOE_BLOCK
    grid_spec = pltpu.PrefetchScalarGridSpec(
        num_scalar_prefetch=7,
        grid=(n_blocks,),
        in_specs=[pl.BlockSpec((bm, D_MODEL // 2), lambda j, bi, be, *_: (bi[j], 0)),
                  pl.BlockSpec(memory_space=pl.ANY),
                  pl.BlockSpec((1, 1, 2 * D_FF), lambda j, bi, be, *_: (be[j], 0, 0)),
                  pl.BlockSpec(memory_space=pl.ANY),
                  pl.BlockSpec((1, 1, D_MODEL), lambda j, bi, be, *_: (be[j], 0, 0))],
        out_specs=pl.BlockSpec((bm, D_MODEL // 2), lambda j, bi, be, *_: (bi[j], 0)),
        scratch_shapes=[pltpu.VMEM((2, D_MODEL, 2 * D_FF), F32), pltpu.VMEM((2, D_FF, D_MODEL), F32),
                        pltpu.VMEM((D_MODEL, 2 * D_FF), BF16), pltpu.VMEM((D_FF, D_MODEL), BF16),
                        pltpu.SemaphoreType.DMA((2, 2))],
    )
    return pl.pallas_call(
        _ffn_kernel,
        grid_spec=grid_spec,
        out_shape=jax.ShapeDtypeStruct(xb.shape, jnp.uint32),
        compiler_params=pltpu.CompilerParams(
            dimension_semantics=("arbitrary",), vmem_limit_bytes=VMEM_LIMIT_BYTES),
        name="expert_ffn",
    )(blk_idx, blk_exp, blk_valid, blk_first, blk_next, blk_slot, n_used, xb, w1, b1, w2, b2)


def _combine_kernel(gk_ref, h1_ref, y0_ref, y1_ref, y2_ref, y3_ref, lng_ref, lnb_ref, out_ref):
    gk = gk_ref[...]
    ffn = gk[:, 0:1] * _unpack_bf16_pairs(y0_ref[...])
    for kk, y_ref in enumerate((y1_ref, y2_ref, y3_ref), start=1):
        ffn = ffn + gk[:, kk:kk + 1] * _unpack_bf16_pairs(y_ref[...])
    out_ref[...] = _layer_norm(DEEPNORM_ALPHA * h1_ref[...] + ffn, lng_ref[...], lnb_ref[...])


def _combine(gk, h1, ysel, lng, lnb):
    rows = h1.shape[0]
    tm = COMBINE_ROWS
    assert rows % tm == 0
    n_tiles = rows // tm
    const = lambda a: pl.BlockSpec(a.shape, lambda i: (0,) * a.ndim)
    ksel = lambda kk: pl.BlockSpec((tm, D_MODEL // 2), lambda i: (kk * n_tiles + i, 0))
    return pl.pallas_call(
        _combine_kernel,
        grid=(n_tiles,),
        in_specs=[pl.BlockSpec((tm, TOP_K), lambda i: (i, 0)),
                  pl.BlockSpec((tm, D_MODEL), lambda i: (i, 0)),
                  ksel(0), ksel(1), ksel(2), ksel(3),
                  const(lng), const(lnb)],
        out_specs=pl.BlockSpec((tm, D_MODEL), lambda i: (i, 0)),
        out_shape=jax.ShapeDtypeStruct((rows, D_MODEL), F32),
        compiler_params=pltpu.CompilerParams(
            dimension_semantics=("arbitrary",), vmem_limit_bytes=VMEM_LIMIT_BYTES),
        name="combine",
    )(gk, h1, ysel, ysel, ysel, ysel, lng, lnb)


def _row(v, width=None):
    v = v.reshape(1, -1).astype(F32)
    if width is not None and v.shape[1] < width:
        v = jnp.pad(v, ((0, 0), (0, width - v.shape[1])))
    return v


def _encode_all(x_a, x_b, meta_tokens, ln_emb_g, ln_emb_b, w_in, conv_w, conv_b, dt_bias, a_log,
                d_skip, ssd_norm_g, i_bias, f_bias, mlstm_norm_g, w_out, ln1_g, ln1_b, w_router, b_router,
                w1, b1, w2, b2, ln2_g, ln2_b):
    n_a, seq_len, _ = x_a.shape
    n_seq = n_a + x_b.shape[0]
    rows_a = n_a * seq_len
    assert seq_len % CHUNK == 0
    n_chunks = seq_len // CHUNK + 1
    n_tok = n_seq * seq_len

    sizes = (1024, CONV_CH, 2 * SSD_HEADS, 512, 512, 1024, 1024, 2 * MLSTM_HEADS, 2 * MLSTM_HEADS)
    offs = [0]
    for s in sizes:
        offs.append(offs[-1] + s)
    w_z, w_xbc, w_dt, w_q, w_k, w_v, w_o, w_i, w_f = [w_in[:, offs[j]:offs[j + 1]] for j in range(9)]
    w_big = jnp.concatenate([w_z, w_xbc, w_q, w_k * (MLSTM_DK ** -0.5), w_v, w_o], axis=1).astype(BF16)
    zpad = jnp.zeros((D_MODEL, LANES - GATE_END), F32)
    gate_cols = []
    for d in range(2):
        gate_cols += [w_dt[:, d * SSD_HEADS:(d + 1) * SSD_HEADS],
                      w_i[:, d * MLSTM_HEADS:(d + 1) * MLSTM_HEADS],
                      w_f[:, d * MLSTM_HEADS:(d + 1) * MLSTM_HEADS], zpad]
    w_gates = jnp.concatenate(gate_cols, axis=1).astype(BF16)
    gbias = [_row(jnp.concatenate([dt_bias[d], i_bias[d], f_bias[d]]), LANES) for d in range(2)]
    alog = [jnp.pad(jnp.broadcast_to(a_log[d].astype(F32)[:, None], (SSD_HEADS, LANES)),
                    ((0, GATE_END - SSD_HEADS), (0, 0))) for d in range(2)]
    head_of_col = jnp.arange(D_MODEL, dtype=jnp.int32) // SSD_HEAD_DIM
    lane_id = jnp.arange(LANES, dtype=jnp.int32)[:, None]
    expand = jnp.concatenate([lane_id == GATE_END + head_of_col[None, :],
                              lane_id == 2 * GATE_END + head_of_col[None, :]], axis=1).astype(BF16)
    dskip = _row(jnp.repeat(d_skip, SSD_HEAD_DIM))
    conv_w8 = jnp.pad(conv_w.astype(F32), ((0, 8 - CONV_W), (0, 0)))
    meta_tile = jnp.pad(meta_tokens.astype(F32), ((PAD_FRONT, INPROJ_ROWS - CHUNK), (0, 0)))

    z, xbc, q, k, v, o, gates_f, gates_b = _inproj(x_a, x_b, meta_tile, _row(ln_emb_g), _row(ln_emb_b),
                                                   w_big, w_gates)
    xs, bm, cm = _conv(xbc, conv_w8, _row(conv_b), n_chunks)
    yb, hb = _mixer_pass(xs, bm, cm, gates_b, q, k, v, gbias[1], alog[1], expand, n_chunks, reverse=True)
    (ycat,) = _mixer_pass(xs, bm, cm, gates_f, q, k, v, gbias[0], alog[0], expand, n_chunks,
                          reverse=False,
                          final_inputs=(z, o, yb, hb, dskip, _row(ssd_norm_g), _row(mlstm_norm_g)))

    wr = jnp.pad(w_router.astype(F32), ((0, 0), (0, LANES - N_EXPERTS)))
    wrh = wr.astype(BF16)
    wrl = (wr - wrh.astype(F32)).astype(BF16)
    w_out_b = w_out.astype(BF16)
    r_i = jnp.arange(RANK_ROWS, dtype=jnp.int32)
    lstrict = (r_i[None, :] < r_i[:, None]).astype(BF16)
    l_i = jnp.arange(LANES, dtype=jnp.int32)
    ucum = (l_i[:, None] <= l_i[None, :]).astype(BF16)
    ycat2d = ycat.reshape(n_tok, 2 * D_MODEL)
    outs = []
    for x_part, row0 in ((x_a, 0), (x_b, rows_a)):
        n_part = x_part.shape[0] * seq_len
        h1, h1p, sel, gates, expert_counts = _epilogue(x_part.reshape(n_part, D_MODEL), ycat2d, row0, w_out_b,
                                        _row(ln_emb_g), _row(ln_emb_b), _row(ln1_g), _row(ln1_b),
                                        wrh, wrl, _row(b_router, LANES))
        n_blocks = n_part * TOP_K // MOE_BLOCK + N_EXPERTS
        trash_row = n_blocks * MOE_BLOCK
        dest, gk, stats = _rank(sel, gates, expert_counts, lstrict, ucum, trash_row)
        counts = stats[0, :N_EXPERTS].astype(jnp.int32)
        starts = stats[1, :N_EXPERTS].astype(jnp.int32)
        pends = stats[2, :N_EXPERTS].astype(jnp.int32)
        n_used = pends[N_EXPERTS - 1] // MOE_BLOCK
        blk = jnp.minimum(jnp.arange(n_blocks, dtype=jnp.int32), jnp.maximum(n_used - 1, 0))
        blk_exp = jnp.minimum(
            jnp.sum((pends[None, :] <= (blk * MOE_BLOCK)[:, None]).astype(jnp.int32), axis=1),
            N_EXPERTS - 1).astype(jnp.int32)
        e_i = jnp.arange(N_EXPERTS, dtype=jnp.int32)[None, :]
        is_exp = blk_exp[:, None] == e_i
        blk_last = jnp.sum(jnp.where(is_exp, (starts + counts)[None, :], 0), axis=1)
        blk_valid = jnp.clip(blk_last - blk * MOE_BLOCK, 0, MOE_BLOCK).astype(jnp.int32)
        d4 = dest[:, :TOP_K, :]
        dest_km = d4.transpose(1, 0, 2).reshape(-1)
        dest_wm = d4.reshape(-1, TOP_K, RANK_ROWS // SC_WINDOW, SC_WINDOW).transpose(0, 2, 1, 3).reshape(
            n_part // SC_WINDOW, TOP_K, SC_WINDOW)
        xb = _sc_scatter_rows(h1p, dest_wm, trash_row + MOE_BLOCK)
        j_i = jnp.arange(n_blocks, dtype=jnp.int32)
        blk_first = jnp.logical_and(j_i < n_used, jnp.logical_or(j_i == 0, blk_exp != jnp.roll(blk_exp, 1)))
        blk_slot = ((jnp.cumsum(blk_first.astype(jnp.int32)) - 1) % 2).astype(jnp.int32)
        later = jnp.where(jnp.logical_and(e_i > blk_exp[:, None], counts[None, :] > 0), e_i, N_EXPERTS)
        blk_next = jnp.min(later, axis=1)
        blk_next = jnp.where(blk_next < N_EXPERTS, blk_next, -1).astype(jnp.int32)
        yexp = _ffn(blk, blk_exp, blk_valid, blk_first.astype(jnp.int32), blk_next, blk_slot,
                    n_used.reshape(1), xb, w1, b1.reshape(N_EXPERTS, 1, -1), w2,
                    b2.reshape(N_EXPERTS, 1, -1), n_blocks)
        ysel = _sc_gather_rows(yexp, dest_km)
        outs.append(_combine(gk, h1, ysel, _row(ln2_g), _row(ln2_b)))
    return outs


def kernel(x_prompt, x_sample, meta_tokens, ln_emb_g, ln_emb_b, w_in, conv_w, conv_b, dt_bias, a_log,
           d_skip, ssd_norm_g, i_bias, f_bias, mlstm_norm_g, w_out, ln1_g, ln1_b, w_router, b_router,
           w1, b1, w2, b2, ln2_g, ln2_b):
    assert x_prompt.shape[1:] == x_sample.shape[1:]
    n_p, seq_len, d = x_prompt.shape
    n_s = x_sample.shape[0]
    y_p, y_s = _encode_all(x_prompt.astype(F32), x_sample.astype(F32), meta_tokens, ln_emb_g, ln_emb_b, w_in[0], conv_w[0],
                           conv_b[0], dt_bias[0], a_log[0], d_skip[0], ssd_norm_g[0], i_bias[0], f_bias[0],
                           mlstm_norm_g[0], w_out[0], ln1_g[0], ln1_b[0], w_router[0], b_router[0],
                           w1[0], b1[0], w2[0], b2[0], ln2_g[0], ln2_b[0])
    return (y_p.reshape(n_p, seq_len, d), y_s.reshape(n_s, seq_len, d))
```

```python
import functools

import jax
import jax.numpy as jnp
from jax import lax
from jax.experimental import pallas as pl
from jax.experimental.pallas import tpu as pltpu
from jax.experimental.pallas import tpu_sc as plsc

F32 = jnp.float32
BF16 = jnp.bfloat16

D_MODEL = 1024
N_META = 16
CHUNK = 128
PAD_FRONT = CHUNK - N_META
SSD_HEADS = 16
SSD_HEAD_DIM = 64
SSD_GROUPS = 4
SSD_STATE = 128
HEADS_PER_GROUP = SSD_HEADS // SSD_GROUPS
GROUP_WIDTH = HEADS_PER_GROUP * SSD_HEAD_DIM
CONV_W = 5
CONV_HALF = CONV_W // 2
CONV_CH = D_MODEL + 2 * SSD_GROUPS * SSD_STATE
MLSTM_HEADS = 8
MLSTM_DK = 64
MLSTM_DV = 128
N_EXPERTS = 32
TOP_K = 4
D_FF = D_MODEL
SWIGLU_LIMIT = 7.0
SWIGLU_ALPHA = 1.702
DEEPNORM_ALPHA = 2.0 ** 0.25
LN_EPS = 1e-5
RMS_EPS = 1e-5
NEG_GATE = -1e30
LOG2E = 1.4426950408889634

LANES = 128
BF16_SUBLANES = 16
VMEM_LIMIT_BYTES = 56 * 1024 * 1024

GATE_DT0, GATE_I0, GATE_F0, GATE_END = 0, SSD_HEADS, SSD_HEADS + MLSTM_HEADS, SSD_HEADS + 2 * MLSTM_HEADS

INPROJ_ROWS = 512
EPILOGUE_ROWS = 512
RANK_ROWS = 512
COMBINE_ROWS = 512
EPILOGUE_PARTS = 2
INPROJ_PARTS = 2
MOE_BLOCK = 512

SC_CORES = 2
SC_SUBCORES = 16
SC_WORKERS = SC_CORES * SC_SUBCORES
SC_WINDOW = 32
SC_SCATTER_INFLIGHT = 2
SC_GATHER_INFLIGHT = 4


def _dot(a, b):
    return jnp.dot(a, b, preferred_element_type=F32)


def _dot_nt(a, b):
    return lax.dot_general(a, b, (((1,), (1,)), ((), ())), preferred_element_type=F32)


def _split3(x):
    hi = x.astype(BF16)
    r1 = x - hi.astype(F32)
    mid = r1.astype(BF16)
    lo = (r1 - mid.astype(F32)).astype(BF16)
    return hi, mid, lo


def _dot_exact_lhs(a_bf16, x):
    hi, mid, lo = _split3(x)
    return _dot(a_bf16, hi) + _dot(a_bf16, mid) + _dot(a_bf16, lo)


def _dot_exact_rhs(x, b_bf16):
    hi, mid, lo = _split3(x)
    return _dot(hi, b_bf16) + _dot(mid, b_bf16) + _dot(lo, b_bf16)


def _layer_norm(x, g, b):
    mu = jnp.mean(x, axis=-1, keepdims=True)
    xc = x - mu
    var = jnp.mean(xc * xc, axis=-1, keepdims=True)
    return xc * lax.rsqrt(var + LN_EPS) * g + b


def _sigmoid(x):
    return 1.0 / (1.0 + jnp.exp(-x))


def _log1p_exp_neg_abs(x):
    return jnp.log(1.0 + jnp.exp(-jnp.abs(x)))


def _pack_bf16_pairs(x):
    n = x.shape[1] // 2
    bits = lax.bitcast_convert_type(x.astype(BF16).astype(F32), jnp.uint32)
    return (bits[:, :n] >> 16) | bits[:, n:]


def _unpack_bf16_pairs(words):
    lo = lax.bitcast_convert_type(words << 16, F32)
    hi = lax.bitcast_convert_type(words & jnp.uint32(0xFFFF0000), F32)
    return jnp.concatenate([lo, hi], axis=1)


def _storage_chunk(c, n_chunks):
    return jnp.where(c == 0, n_chunks - 1, c - 1)


def _inproj_kernel(xa_ref, xb_ref, meta_ref, g_ref, b_ref, wbig_ref, wg_ref,
                   z_ref, xbc_ref, q_ref, k_ref, v_ref, o_ref, gf_ref, gb_ref, *, n_x_tiles, n_a):
    part_rows = xa_ref.shape[0] // INPROJ_PARTS

    def normed(p):
        r = slice(p * part_rows, (p + 1) * part_rows)
        x = jnp.where(pl.program_id(0) < n_a, xa_ref[r, :], xb_ref[r, :])
        x = jnp.where(pl.program_id(1) == n_x_tiles, meta_ref[r, :], x)
        return _layer_norm(x, g_ref[...], b_ref[...]).astype(BF16)

    h_next = normed(0)
    for p in range(INPROJ_PARTS):
        r = slice(p * part_rows, (p + 1) * part_rows)
        h = h_next

        def mm(c0, c1):
            return _dot(h, wbig_ref[:, c0:c1]).astype(BF16)

        z_ref[r, :] = mm(0, 1024)
        if p + 1 < INPROJ_PARTS:
            h_next = normed(p + 1)
        xbc_ref[r, 0:1024] = mm(1024, 2048)
        xbc_ref[r, 1024:2048] = mm(2048, 3072)
        q_ref[r, :] = mm(3072, 3584)
        k_ref[r, :] = mm(3584, 4096)
        v_ref[r, :] = mm(4096, 5120)
        o_ref[r, :] = mm(5120, 6144)
        gates = _dot(h, wg_ref[...])
        gf_ref[r, :] = gates[:, :LANES]
        gb_ref[r, :] = gates[:, LANES:]


def _inproj(x_a, x_b, meta_tile, ln_g, ln_b, w_big, w_gates):
    n_a, seq_len, _ = x_a.shape
    n_seq = n_a + x_b.shape[0]
    tm = INPROJ_ROWS
    assert seq_len % tm == 0 and tm >= CHUNK
    n_x_tiles = seq_len // tm
    rows = seq_len + CHUNK
    row_spec = lambda n: pl.BlockSpec((None, tm, n), lambda b, i: (b, i, 0))
    const = lambda a: pl.BlockSpec(a.shape, lambda b, i: (0,) * a.ndim)
    resident = lambda a: pl.BlockSpec(a.shape, lambda b, i: (0,) * a.ndim, pipeline_mode=pl.Buffered(1))
    widths = (1024, CONV_CH, 512, 512, 1024, 1024)
    out_shapes = [jax.ShapeDtypeStruct((n_seq, rows, w), BF16) for w in widths]
    out_shapes += [jax.ShapeDtypeStruct((n_seq, rows, LANES), F32)] * 2
    return pl.pallas_call(
        functools.partial(_inproj_kernel, n_x_tiles=n_x_tiles, n_a=n_a),
        grid=(n_seq, n_x_tiles + 1),
        in_specs=[pl.BlockSpec((None, tm, D_MODEL), lambda b, i: (
                      jnp.minimum(b, n_a - 1),
                      jnp.where(b < n_a, jnp.minimum(i, n_x_tiles - 1), n_x_tiles - 1), 0)),
                  pl.BlockSpec((None, tm, D_MODEL), lambda b, i: (
                      jnp.maximum(b - n_a, 0),
                      jnp.where(b < n_a, 0, jnp.minimum(i, n_x_tiles - 1)), 0)),
                  resident(meta_tile), const(ln_g), const(ln_b), resident(w_big), resident(w_gates)],
        out_specs=[row_spec(s.shape[2]) for s in out_shapes],
        out_shape=out_shapes,
        compiler_params=pltpu.CompilerParams(
            dimension_semantics=("arbitrary", "arbitrary"), vmem_limit_bytes=VMEM_LIMIT_BYTES),
        name="inproj",
    )(x_a, x_b, meta_tile, ln_g, ln_b, w_big, w_gates)


def _conv_kernel(prev_ref, main_ref, next_ref, shift_ref, w_ref, b_ref, xs_ref, bm_ref, cm_ref,
                 *, n_x_tiles, chunks_per_tile):
    i = pl.program_id(1)
    w = w_ref[...]
    bias = b_ref[...]
    shift = shift_ref[...]

    def conv_chunk(before, rows, after, pad_rows, j):
        shifted = _dot(shift, jnp.concatenate([before, rows, after], axis=0).astype(BF16))
        acc = bias + rows * w[CONV_HALF:CONV_HALF + 1, :]
        for jj, t in enumerate(t for t in range(CONV_W) if t != CONV_HALF):
            acc = acc + shifted[jj * CHUNK:(jj + 1) * CHUNK, :] * w[t:t + 1, :]
        y = acc * _sigmoid(acc)
        if pad_rows is not None:
            y = jnp.where(pad_rows, 0.0, y)
        r = slice(j * CHUNK, (j + 1) * CHUNK)
        xs_ref[r, :] = y[:, :D_MODEL].astype(BF16)
        bm_ref[r, :] = y[:, D_MODEL:D_MODEL + 512].astype(BF16)
        cm_ref[r, :] = y[:, D_MODEL + 512:].astype(BF16)

    @pl.when(i < n_x_tiles)
    def _():
        tile = main_ref[...].astype(F32)
        for j in range(chunks_per_tile):
            lo, hi = j * CHUNK, (j + 1) * CHUNK
            before = prev_ref[...].astype(F32) if j == 0 else tile[lo - BF16_SUBLANES:lo, :]
            if j == chunks_per_tile - 1:
                after = jnp.where(i == n_x_tiles - 1, 0.0, next_ref[...].astype(F32))
            else:
                after = tile[hi:hi + BF16_SUBLANES, :]
            conv_chunk(before, tile[lo:hi, :], after, None, j)

    @pl.when(i == n_x_tiles)
    def _():
        row = lax.broadcasted_iota(jnp.int32, (CHUNK, 1), 0)
        pad_rows = row < PAD_FRONT
        rows = jnp.where(pad_rows, 0.0, main_ref[0:CHUNK, :].astype(F32))
        conv_chunk(jnp.zeros((BF16_SUBLANES, CONV_CH), F32), rows, next_ref[...].astype(F32), pad_rows, 0)


def _conv(xbc, conv_w8, conv_b, n_chunks):
    n_seq, rows, _ = xbc.shape
    tm = INPROJ_ROWS
    seq_len = (n_chunks - 1) * CHUNK
    assert seq_len % tm == 0 and tm % CHUNK == 0
    n_x_tiles = seq_len // tm
    halo_per_tile = tm // BF16_SUBLANES
    meta_last_halo = rows // BF16_SUBLANES - 1

    def prev_map(b, i):
        before_tile = jnp.maximum(jnp.minimum(i, n_x_tiles - 1) * halo_per_tile - 1, 0)
        return (b, jnp.where(i == 0, meta_last_halo, before_tile), 0)

    def next_map(b, i):
        return (b, jnp.where(i >= n_x_tiles - 1, 0, (i + 1) * halo_per_tile) , 0)

    const = lambda a: pl.BlockSpec(a.shape, lambda b, i: (0,) * a.ndim)
    l_i = jnp.arange(CHUNK, dtype=jnp.int32)[:, None]
    j_i = jnp.arange(CHUNK + 2 * BF16_SUBLANES, dtype=jnp.int32)[None, :]
    shifts = jnp.concatenate([(j_i == BF16_SUBLANES + l_i + t - CONV_HALF)
                              for t in range(CONV_W) if t != CONV_HALF], axis=0).astype(BF16)
    out_shapes = [jax.ShapeDtypeStruct((n_seq, rows, D_MODEL), BF16),
                  jax.ShapeDtypeStruct((n_seq, rows, 512), BF16),
                  jax.ShapeDtypeStruct((n_seq, rows, 512), BF16)]
    tile_spec = lambda n: pl.BlockSpec((None, tm, n), lambda b, i: (b, i, 0))
    return pl.pallas_call(
        functools.partial(_conv_kernel, n_x_tiles=n_x_tiles, chunks_per_tile=tm // CHUNK),
        grid=(n_seq, n_x_tiles + 1),
        in_specs=[pl.BlockSpec((None, BF16_SUBLANES, CONV_CH), prev_map),
                  tile_spec(CONV_CH),
                  pl.BlockSpec((None, BF16_SUBLANES, CONV_CH), next_map),
                  const(shifts), const(conv_w8), const(conv_b)],
        out_specs=[tile_spec(s.shape[2]) for s in out_shapes],
        out_shape=out_shapes,
        compiler_params=pltpu.CompilerParams(
            dimension_semantics=("arbitrary", "arbitrary"), vmem_limit_bytes=VMEM_LIMIT_BYTES),
        name="conv",
    )(xbc, xbc, xbc, shifts, conv_w8, conv_b)


def _mixer_kernel(*refs, reverse, final, n_chunks, n_seq):
    if final:
        (xs_ref, bm_ref, cm_ref, g_ref, q_ref, k_ref, v_ref, z_ref, o_ref, yb_ref, hb_ref,
         gbias_ref, alog_ref, expand_ref, dskip_ref, ngs_ref, ngm_ref,
         ycat_ref, s_ref, cst_ref, m_ref) = refs
    else:
        (xs_ref, bm_ref, cm_ref, g_ref, q_ref, k_ref, v_ref,
         gbias_ref, alog_ref, expand_ref,
         yout_ref, hout_ref, s_ref, cst_ref, m_ref) = refs

    t = pl.program_id(0)
    c = (n_chunks - 1 - t) if reverse else t
    end = 0 if reverse else CHUNK - 1

    @pl.when(t == 0)
    def _():
        s_ref[...] = jnp.zeros_like(s_ref)
        cst_ref[...] = jnp.zeros_like(cst_ref)
        m_ref[...] = jnp.zeros_like(m_ref)

    row = lax.broadcasted_iota(jnp.int32, (CHUNK, 1), 0)
    col = lax.broadcasted_iota(jnp.int32, (1, CHUNK), 1)
    lane = col
    allowed = (col >= row) if reverse else (col <= row)
    tri = allowed.astype(BF16)
    tri_t = ((row >= col) if reverse else (row <= col)).astype(BF16)
    feat = lax.broadcasted_iota(jnp.int32, (GATE_END, 1), 0)
    is_dt = feat < GATE_I0
    is_i = jnp.logical_and(feat >= GATE_I0, feat < GATE_F0)
    is_f = feat >= GATE_F0
    pad_cols = jnp.logical_and(c == 0, col < PAD_FRONT)
    a_coef = -jnp.exp(alog_ref[...]) * LOG2E
    expand = expand_ref[...]
    left_half = lane < SSD_HEAD_DIM
    right_half = jnp.logical_not(left_half)
    top_half = row < MLSTM_DK
    ones_blk = jnp.ones((CHUNK, MLSTM_DV), BF16)
    full = (CHUNK, LANES)

    def independent_products(b):
        cgs = [cm_ref[b, :, g * SSD_STATE:(g + 1) * SSD_STATE] for g in range(SSD_GROUPS)]
        bgs = [bm_ref[b, :, g * SSD_STATE:(g + 1) * SSD_STATE] for g in range(SSD_GROUPS)]
        cbs = [_dot_nt(cgs[g], bgs[g]) for g in range(SSD_GROUPS)]
        bg_ts = [bgs[g].astype(F32).T.astype(BF16) for g in range(SSD_GROUPS)]
        q_pairs = [q_ref[b, :, p * LANES:(p + 1) * LANES] for p in range(MLSTM_HEADS // 2)]
        k_pairs = [k_ref[b, :, p * LANES:(p + 1) * LANES] for p in range(MLSTM_HEADS // 2)]
        qks = [_dot_nt(jnp.where(left_half if h % 2 == 0 else right_half, q_pairs[h // 2],
                                 jnp.zeros_like(q_pairs[h // 2])), k_pairs[h // 2])
               for h in range(MLSTM_HEADS)]
        k_pair_ts = [k_pairs[p].astype(F32).T for p in range(MLSTM_HEADS // 2)]
        xs = xs_ref[b]
        xs_rhs = []
        for p in range(SSD_HEADS // 2):
            xs_pair = xs[:, p * LANES:(p + 1) * LANES]
            zero_pair = jnp.zeros_like(xs_pair)
            xs_rhs.append(jnp.concatenate([jnp.where(left_half, xs_pair, zero_pair),
                                           jnp.where(right_half, xs_pair, zero_pair)], axis=0))
        z_gate = o_gate = None
        if final:
            zz = z_ref[b].astype(F32)
            z_gate = zz * _sigmoid(zz)
            o_gate = _sigmoid(o_ref[b].astype(F32))
        return cgs, cbs, bg_ts, q_pairs, qks, k_pair_ts, xs, xs_rhs, z_gate, o_gate

    def one_sequence(b, products):
        cgs, cbs, bg_ts, q_pairs, qks, k_pair_ts, xs, xs_rhs, z_gate, o_gate = products

        gr = (g_ref[b] + gbias_ref[...]).T[0:GATE_END, :]
        lse = _log1p_exp_neg_abs(gr)
        val_t = jnp.where(is_dt, jnp.maximum(gr, 0.0) + lse, jnp.where(is_i, gr, jnp.minimum(gr, 0.0) - lse))
        val_t = jnp.where(pad_cols, jnp.where(is_i, NEG_GATE, 0.0), val_t)
        u_t = jnp.where(is_dt, val_t * a_coef, jnp.where(is_f, val_t * LOG2E, 0.0))
        cums_t = _dot_exact_rhs(u_t, tri_t)
        cums_end = jnp.broadcast_to(cums_t[:, end:end + 1], cums_t.shape)
        p1_t = jnp.exp2(cums_t)
        p2_t = jnp.exp2(cums_end - cums_t) * val_t
        packed = jnp.concatenate([cums_t, p1_t, p2_t, val_t], axis=0).T
        ex = _dot(packed.astype(BF16), expand)
        ex1 = ex[:, :D_MODEL]
        ex2 = ex[:, D_MODEL:]
        chunk_decay = _dot_exact_rhs(jnp.broadcast_to(packed[end:end + 1, :], (8, LANES)),
                                     expand[:, :D_MODEL])[0:1, :]

        xsf = xs.astype(F32)
        xs_w = (xsf * ex2).astype(BF16)
        src_term = jnp.log(val_t[0:SSD_HEADS, :]) * LOG2E - cums_t[0:SSD_HEADS, :]
        for g in range(SSD_GROUPS):
            gs = slice(g * GROUP_WIDTH, (g + 1) * GROUP_WIDTH)
            s_new_all[b].append(chunk_decay[:, gs] * s_old[b][g] + _dot(bg_ts[g], xs_w[:, gs]))
        pair_lhs = []
        for pair in range(SSD_HEADS // 2):
            cb = cbs[pair // (HEADS_PER_GROUP // 2)]
            m_mats = []
            for h in (2 * pair, 2 * pair + 1):
                seg = jnp.broadcast_to(packed[:, h:h + 1], full) + src_term[h:h + 1, :]
                m_mats.append((cb * jnp.exp2(jnp.where(allowed, seg, -jnp.inf))).astype(BF16))
            pair_lhs.append(jnp.concatenate(m_mats, axis=1))
        y_diag = jnp.concatenate([_dot(pair_lhs[p], xs_rhs[p]) for p in range(SSD_HEADS // 2)], axis=1)
        y_off = jnp.concatenate([_dot(cgs[g], s_old[b][g].astype(BF16)) for g in range(SSD_GROUPS)], axis=1)
        y_ssd = y_diag + y_off * ex1

        bcum_t = cums_t[GATE_F0:GATE_END, :]
        ip_t = val_t[GATE_I0:GATE_F0, :] * LOG2E
        rep = bcum_t.shape
        g_rep = jnp.broadcast_to(bcum_t[:, end:end + 1], rep)
        a_t = g_rep - bcum_t + ip_t
        a_max = jnp.broadcast_to(jnp.max(a_t, axis=1, keepdims=True), rep)
        w_t = jnp.exp2(a_t - a_max)
        m_prev = m_old[b]
        m_new = jnp.maximum(g_rep + m_prev, a_max)
        s_prev = jnp.exp2(g_rep + m_prev - m_new)
        s_new = jnp.exp2(a_max - m_new)
        r_t = ip_t - bcum_t
        v_heads = [v_ref[b, :, h * MLSTM_DV:(h + 1) * MLSTM_DV] for h in range(MLSTM_HEADS)]
        for pair in range(MLSTM_HEADS // 2):
            h0, h1 = 2 * pair, 2 * pair + 1
            cst = cst_old[b][pair]
            w_rows = jnp.where(top_half, w_t[h0:h0 + 1, :], w_t[h1:h1 + 1, :])
            kw = (k_pair_ts[pair] * w_rows).astype(BF16)
            full_kv = _dot(kw, jnp.concatenate([v_heads[h0], v_heads[h1], ones_blk], axis=1))
            kvn = jnp.concatenate(
                [jnp.where(top_half, full_kv[:, :MLSTM_DV], full_kv[:, MLSTM_DV:2 * MLSTM_DV]),
                 full_kv[:, 2 * MLSTM_DV:]], axis=1)
            sp_rows = jnp.where(top_half, s_prev[h0:h0 + 1, :], s_prev[h1:h1 + 1, :])
            sn_rows = jnp.where(top_half, s_new[h0:h0 + 1, :], s_new[h1:h1 + 1, :])
            cst_new_all[b].append(jnp.concatenate([sp_rows, sp_rows], axis=1) * cst
                                  + jnp.concatenate([sn_rows, sn_rows], axis=1) * kvn)
        h_heads = []
        for pair in range(MLSTM_HEADS // 2):
            h0, h1 = 2 * pair, 2 * pair + 1
            q_pair_f = q_pairs[pair].astype(F32)
            cst_b = cst_old[b][pair].astype(BF16)
            for hh, h in enumerate((h0, h1)):
                keep = left_half if hh == 0 else right_half
                vh = v_heads[h]
                qk = qks[h]
                bc = jnp.broadcast_to(packed[:, GATE_F0 + h:GATE_F0 + h + 1], full)
                dlog = jnp.where(allowed, bc + r_t[h:h + 1, :], -jnp.inf)
                m_intra = jnp.broadcast_to(jnp.max(dlog, axis=1, keepdims=True), full)
                m_inter = bc + m_prev[h:h + 1, :]
                m_t = jnp.maximum(m_inter, m_intra)
                s_mat = (qk * jnp.exp2(dlog - m_t)).astype(BF16)
                qs = (jnp.where(keep, q_pair_f, 0.0) * jnp.exp2(m_inter - m_t)).astype(BF16)
                tot = _dot(jnp.concatenate([s_mat, qs], axis=1),
                           jnp.concatenate([jnp.concatenate([vh, ones_blk], axis=1), cst_b], axis=0))
                num = tot[:, :MLSTM_DV]
                den = tot[:, MLSTM_DV:]
                h_heads.append(num / jnp.maximum(jnp.abs(den), jnp.exp2(-m_t)))
        m_new_all[b] = m_new
        h_ml = jnp.concatenate(h_heads, axis=1)
        if not final:
            return y_ssd.astype(BF16), h_ml.astype(BF16)

        y_tot = y_ssd + yb_ref[b].astype(F32) + dskip_ref[...] * xsf
        y2 = y_tot * z_gate
        y_n = y2 * lax.rsqrt(jnp.mean(y2 * y2, axis=-1, keepdims=True) + RMS_EPS) * ngs_ref[...]
        h_tot = h_ml + hb_ref[b].astype(F32)
        segs = []
        for h in range(MLSTM_HEADS):
            seg = h_tot[:, h * MLSTM_DV:(h + 1) * MLSTM_DV]
            segs.append(seg * lax.rsqrt(jnp.mean(seg * seg, axis=-1, keepdims=True) + RMS_EPS))
        h_n = jnp.concatenate(segs, axis=1) * ngm_ref[...]
        y_ml = o_gate * h_n
        return y_n.astype(BF16), y_ml.astype(BF16)

    s_old = [[s_ref[b, g] for g in range(SSD_GROUPS)] for b in range(n_seq)]
    cst_old = [[cst_ref[b, p] for p in range(MLSTM_HEADS // 2)] for b in range(n_seq)]
    m_old = [m_ref[b] for b in range(n_seq)]
    s_new_all = [[] for _ in range(n_seq)]
    cst_new_all = [[] for _ in range(n_seq)]
    m_new_all = [None] * n_seq
    results = [one_sequence(b, independent_products(b)) for b in range(n_seq)]
    for b in range(n_seq):
        for g in range(SSD_GROUPS):
            s_ref[b, g] = s_new_all[b][g]
        for p in range(MLSTM_HEADS // 2):
            cst_ref[b, p] = cst_new_all[b][p]
        m_ref[b] = m_new_all[b]

    @pl.when(c > 0)
    def _():
        for b, (first, second) in enumerate(results):
            if final:
                ycat_ref[b, :, :D_MODEL] = first
                ycat_ref[b, :, D_MODEL:] = second
            else:
                yout_ref[b] = first
                hout_ref[b] = second


def _mixer_pass(xs, bm, cm, gates, q, k, v, gbias, alog, expand, n_chunks, *, reverse, final_inputs=None):
    final = final_inputs is not None
    n_seq = xs.shape[0]
    seq_len = (n_chunks - 1) * CHUNK

    def chunk_of(t):
        return (n_chunks - 1 - t) if reverse else t

    def pad_map(t):
        return (0, _storage_chunk(chunk_of(t), n_chunks), 0)

    def out_map(t):
        return (0, jnp.maximum(chunk_of(t) - 1, 0), 0)

    const = lambda a: pl.BlockSpec(a.shape, lambda t: (0,) * a.ndim)
    pad_spec = lambda n: pl.BlockSpec((n_seq, CHUNK, n), pad_map)
    out_spec = lambda n: pl.BlockSpec((n_seq, CHUNK, n), out_map)
    in_arrays = [xs, bm, cm, gates, q, k, v]
    in_specs = [pad_spec(a.shape[2]) for a in in_arrays]
    if final:
        z, o, yb, hb, dskip, ngs, ngm = final_inputs
        in_arrays += [z, o, yb, hb]
        in_specs += [pad_spec(1024), pad_spec(1024), out_spec(1024), out_spec(1024)]
        in_arrays += [gbias, alog, expand, dskip, ngs, ngm]
        in_specs += [const(a) for a in (gbias, alog, expand, dskip, ngs, ngm)]
        out_shape = [jax.ShapeDtypeStruct((n_seq, seq_len, 2 * D_MODEL), BF16)]
        out_specs = [out_spec(2 * D_MODEL)]
    else:
        in_arrays += [gbias, alog, expand]
        in_specs += [const(a) for a in (gbias, alog, expand)]
        out_shape = [jax.ShapeDtypeStruct((n_seq, seq_len, D_MODEL), BF16),
                     jax.ShapeDtypeStruct((n_seq, seq_len, D_MODEL), BF16)]
        out_specs = [out_spec(D_MODEL), out_spec(D_MODEL)]
    return pl.pallas_call(
        functools.partial(_mixer_kernel, reverse=reverse, final=final, n_chunks=n_chunks, n_seq=n_seq),
        grid=(n_chunks,),
        in_specs=in_specs,
        out_specs=out_specs,
        out_shape=out_shape,
        scratch_shapes=[pltpu.VMEM((n_seq, SSD_GROUPS, SSD_STATE, GROUP_WIDTH), F32),
                        pltpu.VMEM((n_seq, MLSTM_HEADS // 2, 2 * MLSTM_DK, 2 * MLSTM_DV), F32),
                        pltpu.VMEM((n_seq, MLSTM_HEADS, LANES), F32)],
        compiler_params=pltpu.CompilerParams(
            dimension_semantics=("arbitrary",), vmem_limit_bytes=VMEM_LIMIT_BYTES),
        name="mixer_fwd" if final else "mixer_bwd",
    )(*in_arrays)


def _epilogue_kernel(x_ref, ycat_ref, wout_ref, lng0_ref, lnb0_ref, lng1_ref, lnb1_ref,
                     wrh_ref, wrl_ref, br_ref, h1_ref, h1p_ref, sel_ref, gate_ref, cnt_ref):
    lane = lax.broadcasted_iota(jnp.int32, (1, LANES), 1)
    lane_f = lane.astype(F32)
    part_rows = x_ref.shape[0] // EPILOGUE_PARTS
    mixes = [_dot(ycat_ref[p * part_rows:(p + 1) * part_rows, :], wout_ref[...]) for p in range(EPILOGUE_PARTS)]
    colsum = jnp.zeros((1, LANES), F32)
    for p, mix in enumerate(mixes):
        r = slice(p * part_rows, (p + 1) * part_rows)
        h0 = _layer_norm(x_ref[r, :], lng0_ref[...], lnb0_ref[...])
        h1 = _layer_norm(DEEPNORM_ALPHA * h0 + mix, lng1_ref[...], lnb1_ref[...])
        h1_ref[r, :] = h1
        h1p_ref[r, :] = _pack_bf16_pairs(h1)
        hh = h1.astype(BF16)
        hl = (h1 - hh.astype(F32)).astype(BF16)
        both = _dot(hh, jnp.concatenate([wrh_ref[...], wrl_ref[...]], axis=1))
        logits = both[:, :LANES] + both[:, LANES:] + _dot(hl, wrh_ref[...]) + br_ref[...]
        logits = jnp.where(lane < N_EXPERTS, logits, -jnp.inf)
        work = logits
        sel = jnp.zeros(logits.shape, jnp.bool_)
        top = None
        for _ in range(TOP_K):
            m = jnp.max(work, axis=-1, keepdims=True)
            if top is None:
                top = m
            first = jnp.min(jnp.where(work == m, lane_f, float(LANES)), axis=-1, keepdims=True)
            pick = lane_f == first
            sel = jnp.logical_or(sel, pick)
            work = jnp.where(pick, -jnp.inf, work)
        e = jnp.where(sel, jnp.exp(logits - top), 0.0)
        gate_ref[r, :] = e / jnp.sum(e, axis=-1, keepdims=True)
        sel_f = sel.astype(F32)
        sel_ref[r, :] = sel_f
        colsum = colsum + jnp.sum(sel_f, axis=0, keepdims=True)

    @pl.when(pl.program_id(0) == 0)
    def _():
        cnt_ref[...] = jnp.zeros_like(cnt_ref)

    cnt_ref[0:1, :] = cnt_ref[0:1, :] + colsum


def _epilogue(x, ycat, ycat_row0, w_out, lng0, lnb0, lng1, lnb1, wrh, wrl, br):
    rows = x.shape[0]
    tm = EPILOGUE_ROWS
    assert rows % tm == 0 and ycat_row0 % tm == 0
    tile0 = ycat_row0 // tm
    row_spec = lambda n: pl.BlockSpec((tm, n), lambda i: (i, 0))
    const = lambda a: pl.BlockSpec(a.shape, lambda i: (0,) * a.ndim)
    out_shape = [jax.ShapeDtypeStruct((rows, D_MODEL), F32),
                 jax.ShapeDtypeStruct((rows, D_MODEL // 2), jnp.uint32),
                 jax.ShapeDtypeStruct((rows, LANES), F32),
                 jax.ShapeDtypeStruct((rows, LANES), F32),
                 jax.ShapeDtypeStruct((8, LANES), F32)]
    consts = (w_out, lng0, lnb0, lng1, lnb1, wrh, wrl, br)
    return pl.pallas_call(
        _epilogue_kernel,
        grid=(rows // tm,),
        in_specs=[row_spec(D_MODEL), pl.BlockSpec((tm, 2 * D_MODEL), lambda i: (tile0 + i, 0))]
                 + [const(a) for a in consts],
        out_specs=[row_spec(D_MODEL), row_spec(D_MODEL // 2), row_spec(LANES), row_spec(LANES),
                   pl.BlockSpec((8, LANES), lambda i: (0, 0))],
        out_shape=out_shape,
        compiler_params=pltpu.CompilerParams(
            dimension_semantics=("arbitrary",), vmem_limit_bytes=VMEM_LIMIT_BYTES),
        name="epilogue",
    )(x, ycat, *consts)


def _rank_kernel(sel_ref, gate_ref, cnt_ref, lstrict_ref, ucum_ref, dest_ref, gk_ref, stats_ref,
                 base_ref, *, trash_row):
    i = pl.program_id(0)
    sel = sel_ref[...]
    colsum = jnp.sum(sel, axis=0, keepdims=True)

    @pl.when(i == 0)
    def _():
        counts = cnt_ref[0:1, :]
        padded = jnp.ceil(counts / MOE_BLOCK) * MOE_BLOCK
        pend = _dot_exact_rhs(jnp.broadcast_to(padded, (8, LANES)), ucum_ref[...])[0:1, :]
        stats_ref[0:1, :] = counts
        stats_ref[1:2, :] = pend - padded
        stats_ref[2:3, :] = pend
        stats_ref[3:8, :] = jnp.zeros((5, LANES), F32)
        base_ref[...] = jnp.broadcast_to(pend - padded, base_ref.shape)

    before = _dot(lstrict_ref[...], sel.astype(BF16))
    pos = base_ref[0:1, :] + before
    base_ref[0:1, :] = base_ref[0:1, :] + colsum
    work = jnp.where(sel > 0.0, pos + 1.0, 0.0)
    gates = gate_ref[...]
    lane = lax.broadcasted_iota(jnp.int32, (1, LANES), 1)
    dest_cols = jnp.zeros(sel.shape, F32)
    for kk in range(TOP_K):
        m = jnp.max(work, axis=-1, keepdims=True)
        pick = jnp.logical_and(work == m, m > 0.0)
        gk_ref[:, kk:kk + 1] = jnp.sum(jnp.where(pick, gates, 0.0), axis=-1, keepdims=True)
        dest_cols = jnp.where(lane == kk, jnp.where(m > 0.0, m - 1.0, float(trash_row)), dest_cols)
        work = jnp.where(pick, 0.0, work)
    dest_ref[...] = dest_cols.T[0:8, :].astype(jnp.int32)


def _rank(sel, gates, counts, lstrict, ucum, trash_row):
    rows = sel.shape[0]
    tm = RANK_ROWS
    assert rows % tm == 0
    row_spec = lambda n: pl.BlockSpec((tm, n), lambda i: (i, 0))
    const = lambda a: pl.BlockSpec(a.shape, lambda i: (0,) * a.ndim)
    return pl.pallas_call(
        functools.partial(_rank_kernel, trash_row=trash_row),
        grid=(rows // tm,),
        in_specs=[row_spec(LANES), row_spec(LANES), const(counts), const(lstrict), const(ucum)],
        out_specs=[pl.BlockSpec((None, 8, tm), lambda i: (i, 0, 0)), row_spec(TOP_K),
                   pl.BlockSpec((8, LANES), lambda i: (0, 0))],
        out_shape=[jax.ShapeDtypeStruct((rows // tm, 8, tm), jnp.int32),
                   jax.ShapeDtypeStruct((rows, TOP_K), F32),
                   jax.ShapeDtypeStruct((8, LANES), F32)],
        scratch_shapes=[pltpu.VMEM((8, LANES), F32)],
        compiler_params=pltpu.CompilerParams(
            dimension_semantics=("arbitrary",), vmem_limit_bytes=VMEM_LIMIT_BYTES),
        name="rank",
    )(sel, gates, counts, lstrict, ucum)


def _sc_mesh():
    return plsc.VectorSubcoreMesh(core_axis_name="c", subcore_axis_name="s",
                                  num_cores=SC_CORES, num_subcores=SC_SUBCORES)


def _sc_scatter_rows(src, idx, n_out_rows):
    n_src, d = src.shape
    w, k = SC_WINDOW, SC_SCATTER_INFLIGHT
    assert idx.shape == (n_src // w, TOP_K, w) and n_src % (w * k * SC_WORKERS) == 0
    per_worker = n_src // (w * SC_WORKERS)
    idx2d = idx.reshape(n_src // w * TOP_K, w)

    @functools.partial(
        pl.kernel, mesh=_sc_mesh(),
        out_type=jax.ShapeDtypeStruct((n_out_rows, d), src.dtype),
        scratch_types=[pltpu.VMEM((k * TOP_K, w), jnp.int32), pltpu.VMEM((k, w, d), src.dtype),
                       pltpu.SemaphoreType.DMA((k,)), pltpu.SemaphoreType.DMA((k,))],
        name="sc_scatter_rows")
    def body(src_hbm, idx_hbm, out_hbm, idx_v, rows_v, sem_load, sem_scatter):
        wid = lax.axis_index("s") * SC_CORES + lax.axis_index("c")

        @pl.loop(0, per_worker // k)
        def _(i):
            win0 = wid * per_worker + i * k
            loads = [pltpu.async_copy(src_hbm.at[pl.ds(pl.multiple_of((win0 + b) * w, w), w)], rows_v.at[b],
                                      sem_load.at[b]) for b in range(k)]
            pltpu.sync_copy(idx_hbm.at[pl.ds(pl.multiple_of(win0 * TOP_K, TOP_K), k * TOP_K)], idx_v)
            scatters = []
            for b in range(k):
                loads[b].wait()
                scatters += [pltpu.async_copy(rows_v.at[b], out_hbm.at[idx_v.at[b * TOP_K + kk]],
                                              sem_scatter.at[b]) for kk in range(TOP_K)]
            for copy in scatters:
                copy.wait()

    return body(src, idx2d)


def _sc_gather_rows(table, idx):
    d = table.shape[1]
    w, k = SC_WINDOW, SC_GATHER_INFLIGHT
    assert idx.shape[0] % (w * k * SC_WORKERS) == 0
    per_worker = idx.shape[0] // (w * SC_WORKERS)
    idx2d = idx.reshape(-1, w)

    @functools.partial(
        pl.kernel, mesh=_sc_mesh(),
        out_type=jax.ShapeDtypeStruct((idx.shape[0], d), table.dtype),
        scratch_types=[pltpu.VMEM((k, w), jnp.int32), pltpu.VMEM((k, w, d), table.dtype),
                       pltpu.SemaphoreType.DMA((k,)), pltpu.SemaphoreType.DMA((k,))],
        name="sc_gather_rows")
    def body(table_hbm, idx_hbm, out_hbm, idx_v, rows_v, sem_gather, sem_store):
        wid = lax.axis_index("s") * SC_CORES + lax.axis_index("c")

        @pl.loop(0, per_worker // k)
        def _(i):
            win0 = wid * per_worker + i * k
            pltpu.sync_copy(idx_hbm.at[pl.ds(pl.multiple_of(win0, k), k)], idx_v)
            gathers = [pltpu.async_copy(table_hbm.at[idx_v.at[b]], rows_v.at[b], sem_gather.at[b])
                       for b in range(k)]
            stores = []
            for b in range(k):
                gathers[b].wait()
                stores.append(pltpu.async_copy(
                    rows_v.at[b], out_hbm.at[pl.ds(pl.multiple_of((win0 + b) * w, w), w)], sem_store.at[b]))
            for copy in stores:
                copy.wait()

    return body(table, idx2d)


def _ffn_kernel(blk_ref, exp_ref, valid_ref, first_ref, next_ref, slot_ref, nused_ref,
                xb_ref, w1_hbm, b1_ref, w2_hbm, b2_ref, yb_ref, w1f_ref, w2f_ref, w1b_ref, w2b_ref, sems):
    j = pl.program_id(0)
    active = j < nused_ref[0]

    def weight_copies(e, slot):
        return (pltpu.make_async_copy(w1_hbm.at[e], w1f_ref.at[slot], sems.at[0, slot]),
                pltpu.make_async_copy(w2_hbm.at[e], w2f_ref.at[slot], sems.at[1, slot]))

    @pl.when(jnp.logical_and(active, first_ref[j] == 1))
    def _():
        e = exp_ref[j]
        slot = slot_ref[j]
        nxt = next_ref[j]

        @pl.when(j == 0)
        def _():
            for copy in weight_copies(e, slot):
                copy.start()

        @pl.when(nxt >= 0)
        def _():
            for copy in weight_copies(nxt, 1 - slot):
                copy.start()

        for copy in weight_copies(e, slot):
            copy.wait()
        w1b_ref[...] = w1f_ref[slot].astype(BF16)
        w2b_ref[...] = w2f_ref[slot].astype(BF16)

    @pl.when(active)
    def _():
        row = lax.broadcasted_iota(jnp.int32, (MOE_BLOCK, 1), 0)
        x = jnp.where(row < valid_ref[j], _unpack_bf16_pairs(xb_ref[...]), 0.0).astype(BF16)
        hc = _dot(x, w1b_ref[...]) + b1_ref[0]
        gate = jnp.minimum(hc[:, :D_FF], SWIGLU_LIMIT)
        up = jnp.clip(hc[:, D_FF:], -SWIGLU_LIMIT, SWIGLU_LIMIT)
        act = (up + 1.0) * gate * _sigmoid(SWIGLU_ALPHA * gate)
        yb_ref[...] = _pack_bf16_pairs(_dot(act.astype(BF16), w2b_ref[...]) + b2_ref[0])


def _ffn(blk_idx, blk_exp, blk_valid, blk_first, blk_next, blk_slot, n_used, xb, w1, b1, w2, b2, n_blocks):
    bm = MOE_BLOCK
    grid_spec = pltpu.PrefetchScalarGridSpec(
        num_scalar_prefetch=7,
        grid=(n_blocks,),
        in_specs=[pl.BlockSpec((bm, D_MODEL // 2), lambda j, bi, be, *_: (bi[j], 0)),
                  pl.BlockSpec(memory_space=pl.ANY),
                  pl.BlockSpec((1, 1, 2 * D_FF), lambda j, bi, be, *_: (be[j], 0, 0)),
                  pl.BlockSpec(memory_space=pl.ANY),
                  pl.BlockSpec((1, 1, D_MODEL), lambda j, bi, be, *_: (be[j], 0, 0))],
        out_specs=pl.BlockSpec((bm, D_MODEL // 2), lambda j, bi, be, *_: (bi[j], 0)),
        scratch_shapes=[pltpu.VMEM((2, D_MODEL, 2 * D_FF), F32), pltpu.VMEM((2, D_FF, D_MODEL), F32),
                        pltpu.VMEM((D_MODEL, 2 * D_FF), BF16), pltpu.VMEM((D_FF, D_MODEL), BF16),
                        pltpu.SemaphoreType.DMA((2, 2))],
    )
    return pl.pallas_call(
        _ffn_kernel,
        grid_spec=grid_spec,
        out_shape=jax.ShapeDtypeStruct(xb.shape, jnp.uint32),
        compiler_params=pltpu.CompilerParams(
            dimension_semantics=("arbitrary",), vmem_limit_bytes=VMEM_LIMIT_BYTES),
        name="expert_ffn",
    )(blk_idx, blk_exp, blk_valid, blk_first, blk_next, blk_slot, n_used, xb, w1, b1, w2, b2)


def _combine_kernel(gk_ref, h1_ref, y0_ref, y1_ref, y2_ref, y3_ref, lng_ref, lnb_ref, out_ref):
    gk = gk_ref[...]
    ffn = gk[:, 0:1] * _unpack_bf16_pairs(y0_ref[...])
    for kk, y_ref in enumerate((y1_ref, y2_ref, y3_ref), start=1):
        ffn = ffn + gk[:, kk:kk + 1] * _unpack_bf16_pairs(y_ref[...])
    out_ref[...] = _layer_norm(DEEPNORM_ALPHA * h1_ref[...] + ffn, lng_ref[...], lnb_ref[...])


def _combine(gk, h1, ysel, lng, lnb):
    rows = h1.shape[0]
    tm = COMBINE_ROWS
    assert rows % tm == 0
    n_tiles = rows // tm
    const = lambda a: pl.BlockSpec(a.shape, lambda i: (0,) * a.ndim)
    ksel = lambda kk: pl.BlockSpec((tm, D_MODEL // 2), lambda i: (kk * n_tiles + i, 0))
    return pl.pallas_call(
        _combine_kernel,
        grid=(n_tiles,),
        in_specs=[pl.BlockSpec((tm, TOP_K), lambda i: (i, 0)),
                  pl.BlockSpec((tm, D_MODEL), lambda i: (i, 0)),
                  ksel(0), ksel(1), ksel(2), ksel(3),
                  const(lng), const(lnb)],
        out_specs=pl.BlockSpec((tm, D_MODEL), lambda i: (i, 0)),
        out_shape=jax.ShapeDtypeStruct((rows, D_MODEL), F32),
        compiler_params=pltpu.CompilerParams(
            dimension_semantics=("arbitrary",), vmem_limit_bytes=VMEM_LIMIT_BYTES),
        name="combine",
    )(gk, h1, ysel, ysel, ysel, ysel, lng, lnb)


def _row(v, width=None):
    v = v.reshape(1, -1).astype(F32)
    if width is not None and v.shape[1] < width:
        v = jnp.pad(v, ((0, 0), (0, width - v.shape[1])))
    return v


def _encode_all(x_a, x_b, meta_tokens, ln_emb_g, ln_emb_b, w_in, conv_w, conv_b, dt_bias, a_log,
                d_skip, ssd_norm_g, i_bias, f_bias, mlstm_norm_g, w_out, ln1_g, ln1_b, w_router, b_router,
                w1, b1, w2, b2, ln2_g, ln2_b):
    n_a, seq_len, _ = x_a.shape
    n_seq = n_a + x_b.shape[0]
    rows_a = n_a * seq_len
    assert seq_len % CHUNK == 0
    n_chunks = seq_len // CHUNK + 1
    n_tok = n_seq * seq_len

    sizes = (1024, CONV_CH, 2 * SSD_HEADS, 512, 512, 1024, 1024, 2 * MLSTM_HEADS, 2 * MLSTM_HEADS)
    offs = [0]
    for s in sizes:
        offs.append(offs[-1] + s)
    w_z, w_xbc, w_dt, w_q, w_k, w_v, w_o, w_i, w_f = [w_in[:, offs[j]:offs[j + 1]] for j in range(9)]
    w_big = jnp.concatenate([w_z, w_xbc, w_q, w_k * (MLSTM_DK ** -0.5), w_v, w_o], axis=1).astype(BF16)
    zpad = jnp.zeros((D_MODEL, LANES - GATE_END), F32)
    gate_cols = []
    for d in range(2):
        gate_cols += [w_dt[:, d * SSD_HEADS:(d + 1) * SSD_HEADS],
                      w_i[:, d * MLSTM_HEADS:(d + 1) * MLSTM_HEADS],
                      w_f[:, d * MLSTM_HEADS:(d + 1) * MLSTM_HEADS], zpad]
    w_gates = jnp.concatenate(gate_cols, axis=1).astype(BF16)
    gbias = [_row(jnp.concatenate([dt_bias[d], i_bias[d], f_bias[d]]), LANES) for d in range(2)]
    alog = [jnp.pad(jnp.broadcast_to(a_log[d].astype(F32)[:, None], (SSD_HEADS, LANES)),
                    ((0, GATE_END - SSD_HEADS), (0, 0))) for d in range(2)]
    head_of_col = jnp.arange(D_MODEL, dtype=jnp.int32) // SSD_HEAD_DIM
    lane_id = jnp.arange(LANES, dtype=jnp.int32)[:, None]
    expand = jnp.concatenate([lane_id == GATE_END + head_of_col[None, :],
                              lane_id == 2 * GATE_END + head_of_col[None, :]], axis=1).astype(BF16)
    dskip = _row(jnp.repeat(d_skip, SSD_HEAD_DIM))
    conv_w8 = jnp.pad(conv_w.astype(F32), ((0, 8 - CONV_W), (0, 0)))
    meta_tile = jnp.pad(meta_tokens.astype(F32), ((PAD_FRONT, INPROJ_ROWS - CHUNK), (0, 0)))

    z, xbc, q, k, v, o, gates_f, gates_b = _inproj(x_a, x_b, meta_tile, _row(ln_emb_g), _row(ln_emb_b),
                                                   w_big, w_gates)
    xs, bm, cm = _conv(xbc, conv_w8, _row(conv_b), n_chunks)
    yb, hb = _mixer_pass(xs, bm, cm, gates_b, q, k, v, gbias[1], alog[1], expand, n_chunks, reverse=True)
    (ycat,) = _mixer_pass(xs, bm, cm, gates_f, q, k, v, gbias[0], alog[0], expand, n_chunks,
                          reverse=False,
                          final_inputs=(z, o, yb, hb, dskip, _row(ssd_norm_g), _row(mlstm_norm_g)))

    wr = jnp.pad(w_router.astype(F32), ((0, 0), (0, LANES - N_EXPERTS)))
    wrh = wr.astype(BF16)
    wrl = (wr - wrh.astype(F32)).astype(BF16)
    w_out_b = w_out.astype(BF16)
    r_i = jnp.arange(RANK_ROWS, dtype=jnp.int32)
    lstrict = (r_i[None, :] < r_i[:, None]).astype(BF16)
    l_i = jnp.arange(LANES, dtype=jnp.int32)
    ucum = (l_i[:, None] <= l_i[None, :]).astype(BF16)
    ycat2d = ycat.reshape(n_tok, 2 * D_MODEL)
    outs = []
    for x_part, row0 in ((x_a, 0), (x_b, rows_a)):
        n_part = x_part.shape[0] * seq_len
        h1, h1p, sel, gates, expert_counts = _epilogue(x_part.reshape(n_part, D_MODEL), ycat2d, row0, w_out_b,
                                        _row(ln_emb_g), _row(ln_emb_b), _row(ln1_g), _row(ln1_b),
                                        wrh, wrl, _row(b_router, LANES))
        n_blocks = n_part * TOP_K // MOE_BLOCK + N_EXPERTS
        trash_row = n_blocks * MOE_BLOCK
        dest, gk, stats = _rank(sel, gates, expert_counts, lstrict, ucum, trash_row)
        counts = stats[0, :N_EXPERTS].astype(jnp.int32)
        starts = stats[1, :N_EXPERTS].astype(jnp.int32)
        pends = stats[2, :N_EXPERTS].astype(jnp.int32)
        n_used = pends[N_EXPERTS - 1] // MOE_BLOCK
        blk = jnp.minimum(jnp.arange(n_blocks, dtype=jnp.int32), jnp.maximum(n_used - 1, 0))
        blk_exp = jnp.minimum(
            jnp.sum((pends[None, :] <= (blk * MOE_BLOCK)[:, None]).astype(jnp.int32), axis=1),
            N_EXPERTS - 1).astype(jnp.int32)
        e_i = jnp.arange(N_EXPERTS, dtype=jnp.int32)[None, :]
        is_exp = blk_exp[:, None] == e_i
        blk_last = jnp.sum(jnp.where(is_exp, (starts + counts)[None, :], 0), axis=1)
        blk_valid = jnp.clip(blk_last - blk * MOE_BLOCK, 0, MOE_BLOCK).astype(jnp.int32)
        d4 = dest[:, :TOP_K, :]
        dest_km = d4.transpose(1, 0, 2).reshape(-1)
        dest_wm = d4.reshape(-1, TOP_K, RANK_ROWS // SC_WINDOW, SC_WINDOW).transpose(0, 2, 1, 3).reshape(
            n_part // SC_WINDOW, TOP_K, SC_WINDOW)
        xb = _sc_scatter_rows(h1p, dest_wm, trash_row + MOE_BLOCK)
        j_i = jnp.arange(n_blocks, dtype=jnp.int32)
        blk_first = jnp.logical_and(j_i < n_used, jnp.logical_or(j_i == 0, blk_exp != jnp.roll(blk_exp, 1)))
        blk_slot = ((jnp.cumsum(blk_first.astype(jnp.int32)) - 1) % 2).astype(jnp.int32)
        later = jnp.where(jnp.logical_and(e_i > blk_exp[:, None], counts[None, :] > 0), e_i, N_EXPERTS)
        blk_next = jnp.min(later, axis=1)
        blk_next = jnp.where(blk_next < N_EXPERTS, blk_next, -1).astype(jnp.int32)
        yexp = _ffn(blk, blk_exp, blk_valid, blk_first.astype(jnp.int32), blk_next, blk_slot,
                    n_used.reshape(1), xb, w1, b1.reshape(N_EXPERTS, 1, -1), w2,
                    b2.reshape(N_EXPERTS, 1, -1), n_blocks)
        ysel = _sc_gather_rows(yexp, dest_km)
        outs.append(_combine(gk, h1, ysel, _row(ln2_g), _row(ln2_b)))
    return outs


def kernel(x_prompt, x_sample, meta_tokens, ln_emb_g, ln_emb_b, w_in, conv_w, conv_b, dt_bias, a_log,
           d_skip, ssd_norm_g, i_bias, f_bias, mlstm_norm_g, w_out, ln1_g, ln1_b, w_router, b_router,
           w1, b1, w2, b2, ln2_g, ln2_b):
    assert x_prompt.shape[1:] == x_sample.shape[1:]
    n_p, seq_len, d = x_prompt.shape
    n_s = x_sample.shape[0]
    y_p, y_s = _encode_all(x_prompt.astype(F32), x_sample.astype(F32), meta_tokens, ln_emb_g, ln_emb_b, w_in[0], conv_w[0],
                           conv_b[0], dt_bias[0], a_log[0], d_skip[0], ssd_norm_g[0], i_bias[0], f_bias[0],
                           mlstm_norm_g[0], w_out[0], ln1_g[0], ln1_b[0], w_router[0], b_router[0],
                           w1[0], b1[0], w2[0], b2[0], ln2_g[0], ln2_b[0])
    return (y_p.reshape(n_p, seq_len, d), y_s.reshape(n_s, seq_len, d))
```

```python
import functools

import jax
import jax.numpy as jnp
from jax import lax
from jax.experimental import pallas as pl
from jax.experimental.pallas import tpu as pltpu
from jax.experimental.pallas import tpu_sc as plsc

F32 = jnp.float32
BF16 = jnp.bfloat16

D_MODEL = 1024
N_META = 16
CHUNK = 128
PAD_FRONT = CHUNK - N_META
SSD_HEADS = 16
SSD_HEAD_DIM = 64
SSD_GROUPS = 4
SSD_STATE = 128
HEADS_PER_GROUP = SSD_HEADS // SSD_GROUPS
GROUP_WIDTH = HEADS_PER_GROUP * SSD_HEAD_DIM
CONV_W = 5
CONV_HALF = CONV_W // 2
CONV_CH = D_MODEL + 2 * SSD_GROUPS * SSD_STATE
MLSTM_HEADS = 8
MLSTM_DK = 64
MLSTM_DV = 128
N_EXPERTS = 32
TOP_K = 4
D_FF = D_MODEL
SWIGLU_LIMIT = 7.0
SWIGLU_ALPHA = 1.702
DEEPNORM_ALPHA = 2.0 ** 0.25
LN_EPS = 1e-5
RMS_EPS = 1e-5
NEG_GATE = -1e30
LOG2E = 1.4426950408889634

LANES = 128
BF16_SUBLANES = 16
VMEM_LIMIT_BYTES = 56 * 1024 * 1024

GATE_DT0, GATE_I0, GATE_F0, GATE_END = 0, SSD_HEADS, SSD_HEADS + MLSTM_HEADS, SSD_HEADS + 2 * MLSTM_HEADS

INPROJ_ROWS = 512
CONV_ROWS = 1024
EPILOGUE_ROWS = 1024
RANK_ROWS = 512
COMBINE_ROWS = 512
EPILOGUE_PARTS = 4
INPROJ_PARTS = 2
MOE_BLOCK = 512

SC_CORES = 2
SC_SUBCORES = 16
SC_WORKERS = SC_CORES * SC_SUBCORES
SC_WINDOW = 32
SC_SCATTER_INFLIGHT = 2
SC_GATHER_INFLIGHT = 4


def _dot(a, b):
    return jnp.dot(a, b, preferred_element_type=F32)


def _dot_nt(a, b):
    return lax.dot_general(a, b, (((1,), (1,)), ((), ())), preferred_element_type=F32)


def _split3(x):
    hi = x.astype(BF16)
    r1 = x - hi.astype(F32)
    mid = r1.astype(BF16)
    lo = (r1 - mid.astype(F32)).astype(BF16)
    return hi, mid, lo


def _dot_exact_lhs(a_bf16, x):
    hi, mid, lo = _split3(x)
    return _dot(a_bf16, hi) + _dot(a_bf16, mid) + _dot(a_bf16, lo)


def _dot_exact_rhs(x, b_bf16):
    hi, mid, lo = _split3(x)
    return _dot(hi, b_bf16) + _dot(mid, b_bf16) + _dot(lo, b_bf16)


def _layer_norm(x, g, b):
    mu = jnp.mean(x, axis=-1, keepdims=True)
    xc = x - mu
    var = jnp.mean(xc * xc, axis=-1, keepdims=True)
    return xc * lax.rsqrt(var + LN_EPS) * g + b


def _sigmoid(x):
    return 1.0 / (1.0 + jnp.exp(-x))


def _log1p_exp_neg_abs(x):
    return jnp.log(1.0 + jnp.exp(-jnp.abs(x)))


def _pack_bf16_pairs(x):
    n = x.shape[1] // 2
    bits = lax.bitcast_convert_type(x.astype(BF16).astype(F32), jnp.uint32)
    return (bits[:, :n] >> 16) | bits[:, n:]


def _unpack_bf16_pairs(words):
    lo = lax.bitcast_convert_type(words << 16, F32)
    hi = lax.bitcast_convert_type(words & jnp.uint32(0xFFFF0000), F32)
    return jnp.concatenate([lo, hi], axis=1)


def _storage_chunk(c, n_chunks):
    return jnp.where(c == 0, n_chunks - 1, c - 1)


def _inproj_kernel(xa_ref, xb_ref, meta_ref, g_ref, b_ref, wbig_ref, wg_ref,
                   z_ref, xbc_ref, q_ref, k_ref, v_ref, o_ref, gf_ref, gb_ref, *, n_x_tiles, n_a):
    part_rows = xa_ref.shape[0] // INPROJ_PARTS

    def normed(p):
        r = slice(p * part_rows, (p + 1) * part_rows)
        x = jnp.where(pl.program_id(0) < n_a, xa_ref[r, :], xb_ref[r, :])
        x = jnp.where(pl.program_id(1) == n_x_tiles, meta_ref[r, :], x)
        return _layer_norm(x, g_ref[...], b_ref[...]).astype(BF16)

    h_next = normed(0)
    for p in range(INPROJ_PARTS):
        r = slice(p * part_rows, (p + 1) * part_rows)
        h = h_next

        def mm(c0, c1):
            return _dot(h, wbig_ref[:, c0:c1]).astype(BF16)

        z_ref[r, :] = mm(0, 1024)
        if p + 1 < INPROJ_PARTS:
            h_next = normed(p + 1)
        xbc_ref[r, 0:1024] = mm(1024, 2048)
        xbc_ref[r, 1024:2048] = mm(2048, 3072)
        q_ref[r, :] = mm(3072, 3584)
        k_ref[r, :] = mm(3584, 4096)
        v_ref[r, :] = mm(4096, 5120)
        o_ref[r, :] = mm(5120, 6144)
        gates = _dot(h, wg_ref[...])
        gf_ref[r, :] = gates[:, :LANES]
        gb_ref[r, :] = gates[:, LANES:]


def _inproj(x_a, x_b, meta_tile, ln_g, ln_b, w_big, w_gates):
    n_a, seq_len, _ = x_a.shape
    n_seq = n_a + x_b.shape[0]
    tm = INPROJ_ROWS
    assert seq_len % tm == 0 and tm >= CHUNK
    n_x_tiles = seq_len // tm
    rows = seq_len + CHUNK
    row_spec = lambda n: pl.BlockSpec((None, tm, n), lambda b, i: (b, i, 0))
    const = lambda a: pl.BlockSpec(a.shape, lambda b, i: (0,) * a.ndim)
    resident = lambda a: pl.BlockSpec(a.shape, lambda b, i: (0,) * a.ndim, pipeline_mode=pl.Buffered(1))
    widths = (1024, CONV_CH, 512, 512, 1024, 1024)
    out_shapes = [jax.ShapeDtypeStruct((n_seq, rows, w), BF16) for w in widths]
    out_shapes += [jax.ShapeDtypeStruct((n_seq, rows, LANES), F32)] * 2
    return pl.pallas_call(
        functools.partial(_inproj_kernel, n_x_tiles=n_x_tiles, n_a=n_a),
        grid=(n_seq, n_x_tiles + 1),
        in_specs=[pl.BlockSpec((None, tm, D_MODEL), lambda b, i: (
                      jnp.minimum(b, n_a - 1),
                      jnp.where(b < n_a, jnp.minimum(i, n_x_tiles - 1), n_x_tiles - 1), 0)),
                  pl.BlockSpec((None, tm, D_MODEL), lambda b, i: (
                      jnp.maximum(b - n_a, 0),
                      jnp.where(b < n_a, 0, jnp.minimum(i, n_x_tiles - 1)), 0)),
                  resident(meta_tile), const(ln_g), const(ln_b), resident(w_big), resident(w_gates)],
        out_specs=[row_spec(s.shape[2]) for s in out_shapes],
        out_shape=out_shapes,
        compiler_params=pltpu.CompilerParams(
            dimension_semantics=("arbitrary", "arbitrary"), vmem_limit_bytes=VMEM_LIMIT_BYTES),
        name="inproj",
    )(x_a, x_b, meta_tile, ln_g, ln_b, w_big, w_gates)


def _conv_kernel(prev_ref, main_ref, next_ref, shift_ref, w_ref, b_ref, xs_ref, bm_ref, cm_ref,
                 *, n_x_tiles, chunks_per_tile):
    i = pl.program_id(1)
    w = w_ref[...]
    bias = b_ref[...]
    shift = shift_ref[...]

    def conv_chunk(before, rows, after, pad_rows, j):
        shifted = _dot(shift, jnp.concatenate([before, rows, after], axis=0).astype(BF16))
        acc = bias + rows * w[CONV_HALF:CONV_HALF + 1, :]
        for jj, t in enumerate(t for t in range(CONV_W) if t != CONV_HALF):
            acc = acc + shifted[jj * CHUNK:(jj + 1) * CHUNK, :] * w[t:t + 1, :]
        y = acc * _sigmoid(acc)
        if pad_rows is not None:
            y = jnp.where(pad_rows, 0.0, y)
        r = slice(j * CHUNK, (j + 1) * CHUNK)
        xs_ref[r, :] = y[:, :D_MODEL].astype(BF16)
        bm_ref[r, :] = y[:, D_MODEL:D_MODEL + 512].astype(BF16)
        cm_ref[r, :] = y[:, D_MODEL + 512:].astype(BF16)

    @pl.when(i < n_x_tiles)
    def _():
        tile = main_ref[...].astype(F32)
        for j in range(chunks_per_tile):
            lo, hi = j * CHUNK, (j + 1) * CHUNK
            before = prev_ref[...].astype(F32) if j == 0 else tile[lo - BF16_SUBLANES:lo, :]
            if j == chunks_per_tile - 1:
                after = jnp.where(i == n_x_tiles - 1, 0.0, next_ref[...].astype(F32))
            else:
                after = tile[hi:hi + BF16_SUBLANES, :]
            conv_chunk(before, tile[lo:hi, :], after, None, j)

    @pl.when(i == n_x_tiles)
    def _():
        row = lax.broadcasted_iota(jnp.int32, (CHUNK, 1), 0)
        pad_rows = row < PAD_FRONT
        rows = jnp.where(pad_rows, 0.0, main_ref[0:CHUNK, :].astype(F32))
        conv_chunk(jnp.zeros((BF16_SUBLANES, CONV_CH), F32), rows, next_ref[...].astype(F32), pad_rows, 0)


def _conv(xbc, conv_w8, conv_b, n_chunks):
    n_seq, rows, _ = xbc.shape
    tm = CONV_ROWS
    seq_len = (n_chunks - 1) * CHUNK
    assert seq_len % tm == 0 and tm % CHUNK == 0
    n_x_tiles = seq_len // tm
    halo_per_tile = tm // BF16_SUBLANES
    meta_last_halo = rows // BF16_SUBLANES - 1

    def prev_map(b, i):
        before_tile = jnp.maximum(jnp.minimum(i, n_x_tiles - 1) * halo_per_tile - 1, 0)
        return (b, jnp.where(i == 0, meta_last_halo, before_tile), 0)

    def next_map(b, i):
        return (b, jnp.where(i >= n_x_tiles - 1, 0, (i + 1) * halo_per_tile) , 0)

    const = lambda a: pl.BlockSpec(a.shape, lambda b, i: (0,) * a.ndim)
    l_i = jnp.arange(CHUNK, dtype=jnp.int32)[:, None]
    j_i = jnp.arange(CHUNK + 2 * BF16_SUBLANES, dtype=jnp.int32)[None, :]
    shifts = jnp.concatenate([(j_i == BF16_SUBLANES + l_i + t - CONV_HALF)
                              for t in range(CONV_W) if t != CONV_HALF], axis=0).astype(BF16)
    out_shapes = [jax.ShapeDtypeStruct((n_seq, rows, D_MODEL), BF16),
                  jax.ShapeDtypeStruct((n_seq, rows, 512), BF16),
                  jax.ShapeDtypeStruct((n_seq, rows, 512), BF16)]
    tile_spec = lambda n: pl.BlockSpec((None, tm, n), lambda b, i: (b, i, 0))
    return pl.pallas_call(
        functools.partial(_conv_kernel, n_x_tiles=n_x_tiles, chunks_per_tile=tm // CHUNK),
        grid=(n_seq, n_x_tiles + 1),
        in_specs=[pl.BlockSpec((None, BF16_SUBLANES, CONV_CH), prev_map),
                  tile_spec(CONV_CH),
                  pl.BlockSpec((None, BF16_SUBLANES, CONV_CH), next_map),
                  const(shifts), const(conv_w8), const(conv_b)],
        out_specs=[tile_spec(s.shape[2]) for s in out_shapes],
        out_shape=out_shapes,
        compiler_params=pltpu.CompilerParams(
            dimension_semantics=("arbitrary", "arbitrary"), vmem_limit_bytes=VMEM_LIMIT_BYTES),
        name="conv",
    )(xbc, xbc, xbc, shifts, conv_w8, conv_b)


def _mixer_kernel(*refs, reverse, final, n_chunks, n_seq):
    if final:
        (xs_ref, bm_ref, cm_ref, g_ref, q_ref, k_ref, v_ref, z_ref, o_ref, yb_ref, hb_ref,
         gbias_ref, alog_ref, expand_ref, dskip_ref, ngs_ref, ngm_ref,
         ycat_ref, s_ref, cst_ref, m_ref) = refs
    else:
        (xs_ref, bm_ref, cm_ref, g_ref, q_ref, k_ref, v_ref,
         gbias_ref, alog_ref, expand_ref,
         yout_ref, hout_ref, s_ref, cst_ref, m_ref) = refs

    t = pl.program_id(0)
    c = (n_chunks - 1 - t) if reverse else t
    end = 0 if reverse else CHUNK - 1

    @pl.when(t == 0)
    def _():
        s_ref[...] = jnp.zeros_like(s_ref)
        cst_ref[...] = jnp.zeros_like(cst_ref)
        m_ref[...] = jnp.zeros_like(m_ref)

    row = lax.broadcasted_iota(jnp.int32, (CHUNK, 1), 0)
    col = lax.broadcasted_iota(jnp.int32, (1, CHUNK), 1)
    lane = col
    allowed = (col >= row) if reverse else (col <= row)
    tri = allowed.astype(BF16)
    tri_t = ((row >= col) if reverse else (row <= col)).astype(BF16)
    feat = lax.broadcasted_iota(jnp.int32, (GATE_END, 1), 0)
    is_dt = feat < GATE_I0
    is_i = jnp.logical_and(feat >= GATE_I0, feat < GATE_F0)
    is_f = feat >= GATE_F0
    pad_cols = jnp.logical_and(c == 0, col < PAD_FRONT)
    a_coef = -jnp.exp(alog_ref[...]) * LOG2E
    expand = expand_ref[...]
    left_half = lane < SSD_HEAD_DIM
    right_half = jnp.logical_not(left_half)
    top_half = row < MLSTM_DK
    ones_blk = jnp.ones((CHUNK, MLSTM_DV), BF16)
    full = (CHUNK, LANES)

    def independent_products(b):
        cgs = [cm_ref[b, :, g * SSD_STATE:(g + 1) * SSD_STATE] for g in range(SSD_GROUPS)]
        bgs = [bm_ref[b, :, g * SSD_STATE:(g + 1) * SSD_STATE] for g in range(SSD_GROUPS)]
        cbs = [_dot_nt(cgs[g], bgs[g]) for g in range(SSD_GROUPS)]
        bg_ts = [bgs[g].astype(F32).T.astype(BF16) for g in range(SSD_GROUPS)]
        q_pairs = [q_ref[b, :, p * LANES:(p + 1) * LANES] for p in range(MLSTM_HEADS // 2)]
        k_pairs = [k_ref[b, :, p * LANES:(p + 1) * LANES] for p in range(MLSTM_HEADS // 2)]
        qks = [_dot_nt(jnp.where(left_half if h % 2 == 0 else right_half, q_pairs[h // 2],
                                 jnp.zeros_like(q_pairs[h // 2])), k_pairs[h // 2])
               for h in range(MLSTM_HEADS)]
        k_pair_ts = [k_pairs[p].astype(F32).T for p in range(MLSTM_HEADS // 2)]
        xs = xs_ref[b]
        xs_rhs = []
        for p in range(SSD_HEADS // 2):
            xs_pair = xs[:, p * LANES:(p + 1) * LANES]
            zero_pair = jnp.zeros_like(xs_pair)
            xs_rhs.append(jnp.concatenate([jnp.where(left_half, xs_pair, zero_pair),
                                           jnp.where(right_half, xs_pair, zero_pair)], axis=0))
        z_gate = o_gate = None
        if final:
            zz = z_ref[b].astype(F32)
            z_gate = zz * _sigmoid(zz)
            o_gate = _sigmoid(o_ref[b].astype(F32))
        return cgs, cbs, bg_ts, q_pairs, qks, k_pair_ts, xs, xs_rhs, z_gate, o_gate

    def one_sequence(b, products):
        cgs, cbs, bg_ts, q_pairs, qks, k_pair_ts, xs, xs_rhs, z_gate, o_gate = products

        gr = (g_ref[b] + gbias_ref[...]).T[0:GATE_END, :]
        lse = _log1p_exp_neg_abs(gr)
        val_t = jnp.where(is_dt, jnp.maximum(gr, 0.0) + lse, jnp.where(is_i, gr, jnp.minimum(gr, 0.0) - lse))
        val_t = jnp.where(pad_cols, jnp.where(is_i, NEG_GATE, 0.0), val_t)
        u_t = jnp.where(is_dt, val_t * a_coef, jnp.where(is_f, val_t * LOG2E, 0.0))
        cums_t = _dot_exact_rhs(u_t, tri_t)
        cums_end = jnp.broadcast_to(cums_t[:, end:end + 1], cums_t.shape)
        p1_t = jnp.exp2(cums_t)
        p2_t = jnp.exp2(cums_end - cums_t) * val_t
        packed = jnp.concatenate([cums_t, p1_t, p2_t, val_t], axis=0).T
        ex = _dot(packed.astype(BF16), expand)
        ex1 = ex[:, :D_MODEL]
        ex2 = ex[:, D_MODEL:]
        chunk_decay = _dot_exact_rhs(jnp.broadcast_to(packed[end:end + 1, :], (8, LANES)),
                                     expand[:, :D_MODEL])[0:1, :]

        xsf = xs.astype(F32)
        xs_w = (xsf * ex2).astype(BF16)
        src_term = jnp.log(val_t[0:SSD_HEADS, :]) * LOG2E - cums_t[0:SSD_HEADS, :]
        for g in range(SSD_GROUPS):
            gs = slice(g * GROUP_WIDTH, (g + 1) * GROUP_WIDTH)
            s_new_all[b].append(chunk_decay[:, gs] * s_old[b][g] + _dot(bg_ts[g], xs_w[:, gs]))
        pair_lhs = []
        for pair in range(SSD_HEADS // 2):
            cb = cbs[pair // (HEADS_PER_GROUP // 2)]
            m_mats = []
            for h in (2 * pair, 2 * pair + 1):
                seg = jnp.broadcast_to(packed[:, h:h + 1], full) + src_term[h:h + 1, :]
                m_mats.append((cb * jnp.exp2(jnp.where(allowed, seg, -jnp.inf))).astype(BF16))
            pair_lhs.append(jnp.concatenate(m_mats, axis=1))
        y_diag = jnp.concatenate([_dot(pair_lhs[p], xs_rhs[p]) for p in range(SSD_HEADS // 2)], axis=1)
        y_off = jnp.concatenate([_dot(cgs[g], s_old[b][g].astype(BF16)) for g in range(SSD_GROUPS)], axis=1)
        y_ssd = y_diag + y_off * ex1

        bcum_t = cums_t[GATE_F0:GATE_END, :]
        ip_t = val_t[GATE_I0:GATE_F0, :] * LOG2E
        rep = bcum_t.shape
        g_rep = jnp.broadcast_to(bcum_t[:, end:end + 1], rep)
        a_t = g_rep - bcum_t + ip_t
        a_max = jnp.broadcast_to(jnp.max(a_t, axis=1, keepdims=True), rep)
        w_t = jnp.exp2(a_t - a_max)
        m_prev = m_old[b]
        m_new = jnp.maximum(g_rep + m_prev, a_max)
        s_prev = jnp.exp2(g_rep + m_prev - m_new)
        s_new = jnp.exp2(a_max - m_new)
        r_t = ip_t - bcum_t
        v_heads = [v_ref[b, :, h * MLSTM_DV:(h + 1) * MLSTM_DV] for h in range(MLSTM_HEADS)]
        for pair in range(MLSTM_HEADS // 2):
            h0, h1 = 2 * pair, 2 * pair + 1
            cst = cst_old[b][pair]
            w_rows = jnp.where(top_half, w_t[h0:h0 + 1, :], w_t[h1:h1 + 1, :])
            kw = (k_pair_ts[pair] * w_rows).astype(BF16)
            full_kv = _dot(kw, jnp.concatenate([v_heads[h0], v_heads[h1], ones_blk], axis=1))
            kvn = jnp.concatenate(
                [jnp.where(top_half, full_kv[:, :MLSTM_DV], full_kv[:, MLSTM_DV:2 * MLSTM_DV]),
                 full_kv[:, 2 * MLSTM_DV:]], axis=1)
            sp_rows = jnp.where(top_half, s_prev[h0:h0 + 1, :], s_prev[h1:h1 + 1, :])
            sn_rows = jnp.where(top_half, s_new[h0:h0 + 1, :], s_new[h1:h1 + 1, :])
            cst_new_all[b].append(jnp.concatenate([sp_rows, sp_rows], axis=1) * cst
                                  + jnp.concatenate([sn_rows, sn_rows], axis=1) * kvn)
        h_heads = []
        for pair in range(MLSTM_HEADS // 2):
            h0, h1 = 2 * pair, 2 * pair + 1
            q_pair_f = q_pairs[pair].astype(F32)
            cst_b = cst_old[b][pair].astype(BF16)
            for hh, h in enumerate((h0, h1)):
                keep = left_half if hh == 0 else right_half
                vh = v_heads[h]
                qk = qks[h]
                bc = jnp.broadcast_to(packed[:, GATE_F0 + h:GATE_F0 + h + 1], full)
                dlog = jnp.where(allowed, bc + r_t[h:h + 1, :], -jnp.inf)
                m_intra = jnp.broadcast_to(jnp.max(dlog, axis=1, keepdims=True), full)
                m_inter = bc + m_prev[h:h + 1, :]
                m_t = jnp.maximum(m_inter, m_intra)
                s_mat = (qk * jnp.exp2(dlog - m_t)).astype(BF16)
                qs = (jnp.where(keep, q_pair_f, 0.0) * jnp.exp2(m_inter - m_t)).astype(BF16)
                tot = _dot(jnp.concatenate([s_mat, qs], axis=1),
                           jnp.concatenate([jnp.concatenate([vh, ones_blk], axis=1), cst_b], axis=0))
                num = tot[:, :MLSTM_DV]
                den = tot[:, MLSTM_DV:]
                h_heads.append(num / jnp.maximum(jnp.abs(den), jnp.exp2(-m_t)))
        m_new_all[b] = m_new
        h_ml = jnp.concatenate(h_heads, axis=1)
        if not final:
            return y_ssd.astype(BF16), h_ml.astype(BF16)

        y_tot = y_ssd + yb_ref[b].astype(F32) + dskip_ref[...] * xsf
        y2 = y_tot * z_gate
        y_n = y2 * lax.rsqrt(jnp.mean(y2 * y2, axis=-1, keepdims=True) + RMS_EPS) * ngs_ref[...]
        h_tot = h_ml + hb_ref[b].astype(F32)
        segs = []
        for h in range(MLSTM_HEADS):
            seg = h_tot[:, h * MLSTM_DV:(h + 1) * MLSTM_DV]
            segs.append(seg * lax.rsqrt(jnp.mean(seg * seg, axis=-1, keepdims=True) + RMS_EPS))
        h_n = jnp.concatenate(segs, axis=1) * ngm_ref[...]
        y_ml = o_gate * h_n
        return y_n.astype(BF16), y_ml.astype(BF16)

    s_old = [[s_ref[b, g] for g in range(SSD_GROUPS)] for b in range(n_seq)]
    cst_old = [[cst_ref[b, p] for p in range(MLSTM_HEADS // 2)] for b in range(n_seq)]
    m_old = [m_ref[b] for b in range(n_seq)]
    s_new_all = [[] for _ in range(n_seq)]
    cst_new_all = [[] for _ in range(n_seq)]
    m_new_all = [None] * n_seq
    results = [one_sequence(b, independent_products(b)) for b in range(n_seq)]
    for b in range(n_seq):
        for g in range(SSD_GROUPS):
            s_ref[b, g] = s_new_all[b][g]
        for p in range(MLSTM_HEADS // 2):
            cst_ref[b, p] = cst_new_all[b][p]
        m_ref[b] = m_new_all[b]

    @pl.when(c > 0)
    def _():
        for b, (first, second) in enumerate(results):
            if final:
                ycat_ref[b, :, :D_MODEL] = first
                ycat_ref[b, :, D_MODEL:] = second
            else:
                yout_ref[b] = first
                hout_ref[b] = second


def _mixer_pass(xs, bm, cm, gates, q, k, v, gbias, alog, expand, n_chunks, *, reverse, final_inputs=None):
    final = final_inputs is not None
    n_seq = xs.shape[0]
    seq_len = (n_chunks - 1) * CHUNK

    def chunk_of(t):
        return (n_chunks - 1 - t) if reverse else t

    def pad_map(t):
        return (0, _storage_chunk(chunk_of(t), n_chunks), 0)

    def out_map(t):
        return (0, jnp.maximum(chunk_of(t) - 1, 0), 0)

    const = lambda a: pl.BlockSpec(a.shape, lambda t: (0,) * a.ndim)
    pad_spec = lambda n: pl.BlockSpec((n_seq, CHUNK, n), pad_map)
    out_spec = lambda n: pl.BlockSpec((n_seq, CHUNK, n), out_map)
    in_arrays = [xs, bm, cm, gates, q, k, v]
    in_specs = [pad_spec(a.shape[2]) for a in in_arrays]
    if final:
        z, o, yb, hb, dskip, ngs, ngm = final_inputs
        in_arrays += [z, o, yb, hb]
        in_specs += [pad_spec(1024), pad_spec(1024), out_spec(1024), out_spec(1024)]
        in_arrays += [gbias, alog, expand, dskip, ngs, ngm]
        in_specs += [const(a) for a in (gbias, alog, expand, dskip, ngs, ngm)]
        out_shape = [jax.ShapeDtypeStruct((n_seq, seq_len, 2 * D_MODEL), BF16)]
        out_specs = [out_spec(2 * D_MODEL)]
    else:
        in_arrays += [gbias, alog, expand]
        in_specs += [const(a) for a in (gbias, alog, expand)]
        out_shape = [jax.ShapeDtypeStruct((n_seq, seq_len, D_MODEL), BF16),
                     jax.ShapeDtypeStruct((n_seq, seq_len, D_MODEL), BF16)]
        out_specs = [out_spec(D_MODEL), out_spec(D_MODEL)]
    return pl.pallas_call(
        functools.partial(_mixer_kernel, reverse=reverse, final=final, n_chunks=n_chunks, n_seq=n_seq),
        grid=(n_chunks,),
        in_specs=in_specs,
        out_specs=out_specs,
        out_shape=out_shape,
        scratch_shapes=[pltpu.VMEM((n_seq, SSD_GROUPS, SSD_STATE, GROUP_WIDTH), F32),
                        pltpu.VMEM((n_seq, MLSTM_HEADS // 2, 2 * MLSTM_DK, 2 * MLSTM_DV), F32),
                        pltpu.VMEM((n_seq, MLSTM_HEADS, LANES), F32)],
        compiler_params=pltpu.CompilerParams(
            dimension_semantics=("arbitrary",), vmem_limit_bytes=VMEM_LIMIT_BYTES),
        name="mixer_fwd" if final else "mixer_bwd",
    )(*in_arrays)


def _epilogue_kernel(x_ref, ycat_ref, wout_ref, lng0_ref, lnb0_ref, lng1_ref, lnb1_ref,
                     wrh_ref, wrl_ref, br_ref, h1_ref, h1p_ref, sel_ref, gate_ref, cnt_ref):
    lane = lax.broadcasted_iota(jnp.int32, (1, LANES), 1)
    lane_f = lane.astype(F32)
    part_rows = x_ref.shape[0] // EPILOGUE_PARTS
    mixes = [_dot(ycat_ref[p * part_rows:(p + 1) * part_rows, :], wout_ref[...]) for p in range(EPILOGUE_PARTS)]
    colsum = jnp.zeros((1, LANES), F32)
    for p, mix in enumerate(mixes):
        r = slice(p * part_rows, (p + 1) * part_rows)
        h0 = _layer_norm(x_ref[r, :], lng0_ref[...], lnb0_ref[...])
        h1 = _layer_norm(DEEPNORM_ALPHA * h0 + mix, lng1_ref[...], lnb1_ref[...])
        h1_ref[r, :] = h1
        h1p_ref[r, :] = _pack_bf16_pairs(h1)
        hh = h1.astype(BF16)
        hl = (h1 - hh.astype(F32)).astype(BF16)
        both = _dot(hh, jnp.concatenate([wrh_ref[...], wrl_ref[...]], axis=1))
        logits = both[:, :LANES] + both[:, LANES:] + _dot(hl, wrh_ref[...]) + br_ref[...]
        logits = jnp.where(lane < N_EXPERTS, logits, -jnp.inf)
        work = logits
        sel = jnp.zeros(logits.shape, jnp.bool_)
        top = None
        for _ in range(TOP_K):
            m = jnp.max(work, axis=-1, keepdims=True)
            if top is None:
                top = m
            first = jnp.min(jnp.where(work == m, lane_f, float(LANES)), axis=-1, keepdims=True)
            pick = lane_f == first
            sel = jnp.logical_or(sel, pick)
            work = jnp.where(pick, -jnp.inf, work)
        e = jnp.where(sel, jnp.exp(logits - top), 0.0)
        gate_ref[r, :] = e / jnp.sum(e, axis=-1, keepdims=True)
        sel_f = sel.astype(F32)
        sel_ref[r, :] = sel_f
        colsum = colsum + jnp.sum(sel_f, axis=0, keepdims=True)

    @pl.when(pl.program_id(0) == 0)
    def _():
        cnt_ref[...] = jnp.zeros_like(cnt_ref)

    cnt_ref[0:1, :] = cnt_ref[0:1, :] + colsum


def _epilogue(x, ycat, ycat_row0, w_out, lng0, lnb0, lng1, lnb1, wrh, wrl, br):
    rows = x.shape[0]
    tm = EPILOGUE_ROWS
    assert rows % tm == 0 and ycat_row0 % tm == 0
    tile0 = ycat_row0 // tm
    row_spec = lambda n: pl.BlockSpec((tm, n), lambda i: (i, 0))
    const = lambda a: pl.BlockSpec(a.shape, lambda i: (0,) * a.ndim)
    out_shape = [jax.ShapeDtypeStruct((rows, D_MODEL), F32),
                 jax.ShapeDtypeStruct((rows, D_MODEL // 2), jnp.uint32),
                 jax.ShapeDtypeStruct((rows, LANES), F32),
                 jax.ShapeDtypeStruct((rows, LANES), F32),
                 jax.ShapeDtypeStruct((8, LANES), F32)]
    consts = (w_out, lng0, lnb0, lng1, lnb1, wrh, wrl, br)
    return pl.pallas_call(
        _epilogue_kernel,
        grid=(rows // tm,),
        in_specs=[row_spec(D_MODEL), pl.BlockSpec((tm, 2 * D_MODEL), lambda i: (tile0 + i, 0))]
                 + [const(a) for a in consts],
        out_specs=[row_spec(D_MODEL), row_spec(D_MODEL // 2), row_spec(LANES), row_spec(LANES),
                   pl.BlockSpec((8, LANES), lambda i: (0, 0))],
        out_shape=out_shape,
        compiler_params=pltpu.CompilerParams(
            dimension_semantics=("arbitrary",), vmem_limit_bytes=VMEM_LIMIT_BYTES),
        name="epilogue",
    )(x, ycat, *consts)


def _rank_kernel(sel_ref, gate_ref, cnt_ref, lstrict_ref, ucum_ref, dest_ref, gk_ref, stats_ref,
                 base_ref, *, trash_row):
    i = pl.program_id(0)
    sel = sel_ref[...]
    colsum = jnp.sum(sel, axis=0, keepdims=True)

    @pl.when(i == 0)
    def _():
        counts = cnt_ref[0:1, :]
        padded = jnp.ceil(counts / MOE_BLOCK) * MOE_BLOCK
        pend = _dot_exact_rhs(jnp.broadcast_to(padded, (8, LANES)), ucum_ref[...])[0:1, :]
        stats_ref[0:1, :] = counts
        stats_ref[1:2, :] = pend - padded
        stats_ref[2:3, :] = pend
        stats_ref[3:8, :] = jnp.zeros((5, LANES), F32)
        base_ref[...] = jnp.broadcast_to(pend - padded, base_ref.shape)

    before = _dot(lstrict_ref[...], sel.astype(BF16))
    pos = base_ref[0:1, :] + before
    base_ref[0:1, :] = base_ref[0:1, :] + colsum
    work = jnp.where(sel > 0.0, pos + 1.0, 0.0)
    gates = gate_ref[...]
    lane = lax.broadcasted_iota(jnp.int32, (1, LANES), 1)
    dest_cols = jnp.zeros(sel.shape, F32)
    for kk in range(TOP_K):
        m = jnp.max(work, axis=-1, keepdims=True)
        pick = jnp.logical_and(work == m, m > 0.0)
        gk_ref[:, kk:kk + 1] = jnp.sum(jnp.where(pick, gates, 0.0), axis=-1, keepdims=True)
        dest_cols = jnp.where(lane == kk, jnp.where(m > 0.0, m - 1.0, float(trash_row)), dest_cols)
        work = jnp.where(pick, 0.0, work)
    dest_ref[...] = dest_cols.T[0:8, :].astype(jnp.int32)


def _rank(sel, gates, counts, lstrict, ucum, trash_row):
    rows = sel.shape[0]
    tm = RANK_ROWS
    assert rows % tm == 0
    row_spec = lambda n: pl.BlockSpec((tm, n), lambda i: (i, 0))
    const = lambda a: pl.BlockSpec(a.shape, lambda i: (0,) * a.ndim)
    return pl.pallas_call(
        functools.partial(_rank_kernel, trash_row=trash_row),
        grid=(rows // tm,),
        in_specs=[row_spec(LANES), row_spec(LANES), const(counts), const(lstrict), const(ucum)],
        out_specs=[pl.BlockSpec((None, 8, tm), lambda i: (i, 0, 0)), row_spec(TOP_K),
                   pl.BlockSpec((8, LANES), lambda i: (0, 0))],
        out_shape=[jax.ShapeDtypeStruct((rows // tm, 8, tm), jnp.int32),
                   jax.ShapeDtypeStruct((rows, TOP_K), F32),
                   jax.ShapeDtypeStruct((8, LANES), F32)],
        scratch_shapes=[pltpu.VMEM((8, LANES), F32)],
        compiler_params=pltpu.CompilerParams(
            dimension_semantics=("arbitrary",), vmem_limit_bytes=VMEM_LIMIT_BYTES),
        name="rank",
    )(sel, gates, counts, lstrict, ucum)


def _sc_mesh():
    return plsc.VectorSubcoreMesh(core_axis_name="c", subcore_axis_name="s",
                                  num_cores=SC_CORES, num_subcores=SC_SUBCORES)


def _sc_scatter_rows(src, idx, n_out_rows):
    n_src, d = src.shape
    w, k = SC_WINDOW, SC_SCATTER_INFLIGHT
    assert idx.shape == (n_src // w, TOP_K, w) and n_src % (w * k * SC_WORKERS) == 0
    per_worker = n_src // (w * SC_WORKERS)
    idx2d = idx.reshape(n_src // w * TOP_K, w)

    @functools.partial(
        pl.kernel, mesh=_sc_mesh(),
        out_type=jax.ShapeDtypeStruct((n_out_rows, d), src.dtype),
        scratch_types=[pltpu.VMEM((k * TOP_K, w), jnp.int32), pltpu.VMEM((k, w, d), src.dtype),
                       pltpu.SemaphoreType.DMA((k,)), pltpu.SemaphoreType.DMA((k,))],
        name="sc_scatter_rows")
    def body(src_hbm, idx_hbm, out_hbm, idx_v, rows_v, sem_load, sem_scatter):
        wid = lax.axis_index("s") * SC_CORES + lax.axis_index("c")

        @pl.loop(0, per_worker // k)
        def _(i):
            win0 = wid * per_worker + i * k
            loads = [pltpu.async_copy(src_hbm.at[pl.ds(pl.multiple_of((win0 + b) * w, w), w)], rows_v.at[b],
                                      sem_load.at[b]) for b in range(k)]
            pltpu.sync_copy(idx_hbm.at[pl.ds(pl.multiple_of(win0 * TOP_K, TOP_K), k * TOP_K)], idx_v)
            scatters = []
            for b in range(k):
                loads[b].wait()
                scatters += [pltpu.async_copy(rows_v.at[b], out_hbm.at[idx_v.at[b * TOP_K + kk]],
                                              sem_scatter.at[b]) for kk in range(TOP_K)]
            for copy in scatters:
                copy.wait()

    return body(src, idx2d)


def _sc_gather_rows(table, idx):
    d = table.shape[1]
    w, k = SC_WINDOW, SC_GATHER_INFLIGHT
    assert idx.shape[0] % (w * k * SC_WORKERS) == 0
    per_worker = idx.shape[0] // (w * SC_WORKERS)
    idx2d = idx.reshape(-1, w)

    @functools.partial(
        pl.kernel, mesh=_sc_mesh(),
        out_type=jax.ShapeDtypeStruct((idx.shape[0], d), table.dtype),
        scratch_types=[pltpu.VMEM((k, w), jnp.int32), pltpu.VMEM((k, w, d), table.dtype),
                       pltpu.SemaphoreType.DMA((k,)), pltpu.SemaphoreType.DMA((k,))],
        name="sc_gather_rows")
    def body(table_hbm, idx_hbm, out_hbm, idx_v, rows_v, sem_gather, sem_store):
        wid = lax.axis_index("s") * SC_CORES + lax.axis_index("c")

        @pl.loop(0, per_worker // k)
        def _(i):
            win0 = wid * per_worker + i * k
            pltpu.sync_copy(idx_hbm.at[pl.ds(pl.multiple_of(win0, k), k)], idx_v)
            gathers = [pltpu.async_copy(table_hbm.at[idx_v.at[b]], rows_v.at[b], sem_gather.at[b])
                       for b in range(k)]
            stores = []
            for b in range(k):
                gathers[b].wait()
                stores.append(pltpu.async_copy(
                    rows_v.at[b], out_hbm.at[pl.ds(pl.multiple_of((win0 + b) * w, w), w)], sem_store.at[b]))
            for copy in stores:
                copy.wait()

    return body(table, idx2d)


def _ffn_kernel(blk_ref, exp_ref, valid_ref, first_ref, next_ref, slot_ref, nused_ref,
                xb_ref, w1_hbm, b1_ref, w2_hbm, b2_ref, yb_ref, w1f_ref, w2f_ref, w1b_ref, w2b_ref, sems):
    j = pl.program_id(0)
    active = j < nused_ref[0]

    def weight_copies(e, slot):
        return (pltpu.make_async_copy(w1_hbm.at[e], w1f_ref.at[slot], sems.at[0, slot]),
                pltpu.make_async_copy(w2_hbm.at[e], w2f_ref.at[slot], sems.at[1, slot]))

    @pl.when(jnp.logical_and(active, first_ref[j] == 1))
    def _():
        e = exp_ref[j]
        slot = slot_ref[j]
        nxt = next_ref[j]

        @pl.when(j == 0)
        def _():
            for copy in weight_copies(e, slot):
                copy.start()

        @pl.when(nxt >= 0)
        def _():
            for copy in weight_copies(nxt, 1 - slot):
                copy.start()

        for copy in weight_copies(e, slot):
            copy.wait()
        w1b_ref[...] = w1f_ref[slot].astype(BF16)
        w2b_ref[...] = w2f_ref[slot].astype(BF16)

    @pl.when(active)
    def _():
        row = lax.broadcasted_iota(jnp.int32, (MOE_BLOCK, 1), 0)
        x = jnp.where(row < valid_ref[j], _unpack_bf16_pairs(xb_ref[...]), 0.0).astype(BF16)
        hc = _dot(x, w1b_ref[...]) + b1_ref[0]
        gate = jnp.minimum(hc[:, :D_FF], SWIGLU_LIMIT)
        up = jnp.clip(hc[:, D_FF:], -SWIGLU_LIMIT, SWIGLU_LIMIT)
        act = (up + 1.0) * gate * _sigmoid(SWIGLU_ALPHA * gate)
        yb_ref[...] = _pack_bf16_pairs(_dot(act.astype(BF16), w2b_ref[...]) + b2_ref[0])


def _ffn(blk_idx, blk_exp, blk_valid, blk_first, blk_next, blk_slot, n_used, xb, w1, b1, w2, b2, n_blocks):
    bm = MOE_BLOCK
    grid_spec = pltpu.PrefetchScalarGridSpec(
        num_scalar_prefetch=7,
        grid=(n_blocks,),
        in_specs=[pl.BlockSpec((bm, D_MODEL // 2), lambda j, bi, be, *_: (bi[j], 0)),
                  pl.BlockSpec(memory_space=pl.ANY),
                  pl.BlockSpec((1, 1, 2 * D_FF), lambda j, bi, be, *_: (be[j], 0, 0)),
                  pl.BlockSpec(memory_space=pl.ANY),
                  pl.BlockSpec((1, 1, D_MODEL), lambda j, bi, be, *_: (be[j], 0, 0))],
        out_specs=pl.BlockSpec((bm, D_MODEL // 2), lambda j, bi, be, *_: (bi[j], 0)),
        scratch_shapes=[pltpu.VMEM((2, D_MODEL, 2 * D_FF), F32), pltpu.VMEM((2, D_FF, D_MODEL), F32),
                        pltpu.VMEM((D_MODEL, 2 * D_FF), BF16), pltpu.VMEM((D_FF, D_MODEL), BF16),
                        pltpu.SemaphoreType.DMA((2, 2))],
    )
    return pl.pallas_call(
        _ffn_kernel,
        grid_spec=grid_spec,
        out_shape=jax.ShapeDtypeStruct(xb.shape, jnp.uint32),
        compiler_params=pltpu.CompilerParams(
            dimension_semantics=("arbitrary",), vmem_limit_bytes=VMEM_LIMIT_BYTES),
        name="expert_ffn",
    )(blk_idx, blk_exp, blk_valid, blk_first, blk_next, blk_slot, n_used, xb, w1, b1, w2, b2)


def _combine_kernel(gk_ref, h1_ref, y0_ref, y1_ref, y2_ref, y3_ref, lng_ref, lnb_ref, out_ref):
    gk = gk_ref[...]
    ffn = gk[:, 0:1] * _unpack_bf16_pairs(y0_ref[...])
    for kk, y_ref in enumerate((y1_ref, y2_ref, y3_ref), start=1):
        ffn = ffn + gk[:, kk:kk + 1] * _unpack_bf16_pairs(y_ref[...])
    out_ref[...] = _layer_norm(DEEPNORM_ALPHA * h1_ref[...] + ffn, lng_ref[...], lnb_ref[...])


def _combine(gk, h1, ysel, lng, lnb):
    rows = h1.shape[0]
    tm = COMBINE_ROWS
    assert rows % tm == 0
    n_tiles = rows // tm
    const = lambda a: pl.BlockSpec(a.shape, lambda i: (0,) * a.ndim)
    ksel = lambda kk: pl.BlockSpec((tm, D_MODEL // 2), lambda i: (kk * n_tiles + i, 0))
    return pl.pallas_call(
        _combine_kernel,
        grid=(n_tiles,),
        in_specs=[pl.BlockSpec((tm, TOP_K), lambda i: (i, 0)),
                  pl.BlockSpec((tm, D_MODEL), lambda i: (i, 0)),
                  ksel(0), ksel(1), ksel(2), ksel(3),
                  const(lng), const(lnb)],
        out_specs=pl.BlockSpec((tm, D_MODEL), lambda i: (i, 0)),
        out_shape=jax.ShapeDtypeStruct((rows, D_MODEL), F32),
        compiler_params=pltpu.CompilerParams(
            dimension_semantics=("arbitrary",), vmem_limit_bytes=VMEM_LIMIT_BYTES),
        name="combine",
    )(gk, h1, ysel, ysel, ysel, ysel, lng, lnb)


def _row(v, width=None):
    v = v.reshape(1, -1).astype(F32)
    if width is not None and v.shape[1] < width:
        v = jnp.pad(v, ((0, 0), (0, width - v.shape[1])))
    return v


def _encode_all(x_a, x_b, meta_tokens, ln_emb_g, ln_emb_b, w_in, conv_w, conv_b, dt_bias, a_log,
                d_skip, ssd_norm_g, i_bias, f_bias, mlstm_norm_g, w_out, ln1_g, ln1_b, w_router, b_router,
                w1, b1, w2, b2, ln2_g, ln2_b):
    n_a, seq_len, _ = x_a.shape
    n_seq = n_a + x_b.shape[0]
    rows_a = n_a * seq_len
    assert seq_len % CHUNK == 0
    n_chunks = seq_len // CHUNK + 1
    n_tok = n_seq * seq_len

    sizes = (1024, CONV_CH, 2 * SSD_HEADS, 512, 512, 1024, 1024, 2 * MLSTM_HEADS, 2 * MLSTM_HEADS)
    offs = [0]
    for s in sizes:
        offs.append(offs[-1] + s)
    w_z, w_xbc, w_dt, w_q, w_k, w_v, w_o, w_i, w_f = [w_in[:, offs[j]:offs[j + 1]] for j in range(9)]
    w_big = jnp.concatenate([w_z, w_xbc, w_q, w_k * (MLSTM_DK ** -0.5), w_v, w_o], axis=1).astype(BF16)
    zpad = jnp.zeros((D_MODEL, LANES - GATE_END), F32)
    gate_cols = []
    for d in range(2):
        gate_cols += [w_dt[:, d * SSD_HEADS:(d + 1) * SSD_HEADS],
                      w_i[:, d * MLSTM_HEADS:(d + 1) * MLSTM_HEADS],
                      w_f[:, d * MLSTM_HEADS:(d + 1) * MLSTM_HEADS], zpad]
    w_gates = jnp.concatenate(gate_cols, axis=1).astype(BF16)
    gbias = [_row(jnp.concatenate([dt_bias[d], i_bias[d], f_bias[d]]), LANES) for d in range(2)]
    alog = [jnp.pad(jnp.broadcast_to(a_log[d].astype(F32)[:, None], (SSD_HEADS, LANES)),
                    ((0, GATE_END - SSD_HEADS), (0, 0))) for d in range(2)]
    head_of_col = jnp.arange(D_MODEL, dtype=jnp.int32) // SSD_HEAD_DIM
    lane_id = jnp.arange(LANES, dtype=jnp.int32)[:, None]
    expand = jnp.concatenate([lane_id == GATE_END + head_of_col[None, :],
                              lane_id == 2 * GATE_END + head_of_col[None, :]], axis=1).astype(BF16)
    dskip = _row(jnp.repeat(d_skip, SSD_HEAD_DIM))
    conv_w8 = jnp.pad(conv_w.astype(F32), ((0, 8 - CONV_W), (0, 0)))
    meta_tile = jnp.pad(meta_tokens.astype(F32), ((PAD_FRONT, INPROJ_ROWS - CHUNK), (0, 0)))

    z, xbc, q, k, v, o, gates_f, gates_b = _inproj(x_a, x_b, meta_tile, _row(ln_emb_g), _row(ln_emb_b),
                                                   w_big, w_gates)
    xs, bm, cm = _conv(xbc, conv_w8, _row(conv_b), n_chunks)
    yb, hb = _mixer_pass(xs, bm, cm, gates_b, q, k, v, gbias[1], alog[1], expand, n_chunks, reverse=True)
    (ycat,) = _mixer_pass(xs, bm, cm, gates_f, q, k, v, gbias[0], alog[0], expand, n_chunks,
                          reverse=False,
                          final_inputs=(z, o, yb, hb, dskip, _row(ssd_norm_g), _row(mlstm_norm_g)))

    wr = jnp.pad(w_router.astype(F32), ((0, 0), (0, LANES - N_EXPERTS)))
    wrh = wr.astype(BF16)
    wrl = (wr - wrh.astype(F32)).astype(BF16)
    w_out_b = w_out.astype(BF16)
    r_i = jnp.arange(RANK_ROWS, dtype=jnp.int32)
    lstrict = (r_i[None, :] < r_i[:, None]).astype(BF16)
    l_i = jnp.arange(LANES, dtype=jnp.int32)
    ucum = (l_i[:, None] <= l_i[None, :]).astype(BF16)
    ycat2d = ycat.reshape(n_tok, 2 * D_MODEL)
    outs = []
    for x_part, row0 in ((x_a, 0), (x_b, rows_a)):
        n_part = x_part.shape[0] * seq_len
        h1, h1p, sel, gates, expert_counts = _epilogue(x_part.reshape(n_part, D_MODEL), ycat2d, row0, w_out_b,
                                        _row(ln_emb_g), _row(ln_emb_b), _row(ln1_g), _row(ln1_b),
                                        wrh, wrl, _row(b_router, LANES))
        n_blocks = n_part * TOP_K // MOE_BLOCK + N_EXPERTS
        trash_row = n_blocks * MOE_BLOCK
        dest, gk, stats = _rank(sel, gates, expert_counts, lstrict, ucum, trash_row)
        counts = stats[0, :N_EXPERTS].astype(jnp.int32)
        starts = stats[1, :N_EXPERTS].astype(jnp.int32)
        pends = stats[2, :N_EXPERTS].astype(jnp.int32)
        n_used = pends[N_EXPERTS - 1] // MOE_BLOCK
        blk = jnp.minimum(jnp.arange(n_blocks, dtype=jnp.int32), jnp.maximum(n_used - 1, 0))
        blk_exp = jnp.minimum(
            jnp.sum((pends[None, :] <= (blk * MOE_BLOCK)[:, None]).astype(jnp.int32), axis=1),
            N_EXPERTS - 1).astype(jnp.int32)
        e_i = jnp.arange(N_EXPERTS, dtype=jnp.int32)[None, :]
        is_exp = blk_exp[:, None] == e_i
        blk_last = jnp.sum(jnp.where(is_exp, (starts + counts)[None, :], 0), axis=1)
        blk_valid = jnp.clip(blk_last - blk * MOE_BLOCK, 0, MOE_BLOCK).astype(jnp.int32)
        d4 = dest[:, :TOP_K, :]
        dest_km = d4.transpose(1, 0, 2).reshape(-1)
        dest_wm = d4.reshape(-1, TOP_K, RANK_ROWS // SC_WINDOW, SC_WINDOW).transpose(0, 2, 1, 3).reshape(
            n_part // SC_WINDOW, TOP_K, SC_WINDOW)
        xb = _sc_scatter_rows(h1p, dest_wm, trash_row + MOE_BLOCK)
        j_i = jnp.arange(n_blocks, dtype=jnp.int32)
        blk_first = jnp.logical_and(j_i < n_used, jnp.logical_or(j_i == 0, blk_exp != jnp.roll(blk_exp, 1)))
        blk_slot = ((jnp.cumsum(blk_first.astype(jnp.int32)) - 1) % 2).astype(jnp.int32)
        later = jnp.where(jnp.logical_and(e_i > blk_exp[:, None], counts[None, :] > 0), e_i, N_EXPERTS)
        blk_next = jnp.min(later, axis=1)
        blk_next = jnp.where(blk_next < N_EXPERTS, blk_next, -1).astype(jnp.int32)
        yexp = _ffn(blk, blk_exp, blk_valid, blk_first.astype(jnp.int32), blk_next, blk_slot,
                    n_used.reshape(1), xb, w1, b1.reshape(N_EXPERTS, 1, -1), w2,
                    b2.reshape(N_EXPERTS, 1, -1), n_blocks)
        ysel = _sc_gather_rows(yexp, dest_km)
        outs.append(_combine(gk, h1, ysel, _row(ln2_g), _row(ln2_b)))
    return outs


def kernel(x_prompt, x_sample, meta_tokens, ln_emb_g, ln_emb_b, w_in, conv_w, conv_b, dt_bias, a_log,
           d_skip, ssd_norm_g, i_bias, f_bias, mlstm_norm_g, w_out, ln1_g, ln1_b, w_router, b_router,
           w1, b1, w2, b2, ln2_g, ln2_b):
    assert x_prompt.shape[1:] == x_sample.shape[1:]
    n_p, seq_len, d = x_prompt.shape
    n_s = x_sample.shape[0]
    y_p, y_s = _encode_all(x_prompt.astype(F32), x_sample.astype(F32), meta_tokens, ln_emb_g, ln_emb_b, w_in[0], conv_w[0],
                           conv_b[0], dt_bias[0], a_log[0], d_skip[0], ssd_norm_g[0], i_bias[0], f_bias[0],
                           mlstm_norm_g[0], w_out[0], ln1_g[0], ln1_b[0], w_router[0], b_router[0],
                           w1[0], b1[0], w2[0], b2[0], ln2_g[0], ln2_b[0])
    return (y_p.reshape(n_p, seq_len, d), y_s.reshape(n_s, seq_len, d))
```
